```python
import math
import jax, jax.numpy as jnp
from jax import lax
import numpy as np

D_MODEL = 2048
BATCH = 1
SEQ = 8192
DEPTH = 2
DEC_BATCH = 32
DEC_SEQ = 16
PAST_LEN = 1024

CHUNK = 64
N_EVEN = (DEPTH + 1) // 2
N_ODD = DEPTH // 2
NORM_EPS = 1e-6
NEG = -1e30

A_HEADS = 16
A_HEAD_DIM = 64
A_WIDTH = A_HEADS * A_HEAD_DIM
A_PREV_CHUNKS = 8
A_BAND = (A_PREV_CHUNKS + 1) * CHUNK
A_REL_CLIP = 128
B_WINDOWS = (2, 4, 8, 16)
B_GROUP = 256
B_WIDTH = B_GROUP * len(B_WINDOWS)
B_HIST = max(B_WINDOWS) - 1
EVEN_IN = 3 * A_WIDTH + B_WIDTH
EVEN_MIX = A_WIDTH + B_WIDTH
C_HEADS = 16
C_KV_HEADS = 4
C_HEAD_DIM = 128
C_GROUPS = C_HEADS // C_KV_HEADS
IDX_HEADS = 8
IDX_DIM = 64
TOPK_MAX = 256
Q_BLOCK = 128
ODD_SPLITS = (C_HEADS * C_HEAD_DIM, C_KV_HEADS * C_HEAD_DIM, C_KV_HEADS * C_HEAD_DIM,
              IDX_HEADS * IDX_DIM, IDX_DIM, IDX_HEADS)
ODD_IN = sum(ODD_SPLITS)
ROPE_THETA = 500000.0
ROPE_FRAC = 4
D_FF = 5632
N_EXPERTS = 8
TOP_K_EXPERTS = 2
D_FF_EXPERT = 5632

kernel_name = "chunk_streaming_hybrid_encoder_step"

F32 = jnp.float32


def rmsnorm(x, g):
    xf = x.astype(F32)
    y = xf * lax.rsqrt(jnp.mean(xf * xf, axis=-1, keepdims=True) + NORM_EPS)
    return (y * g.astype(F32)).astype(x.dtype)


def partial_rope(x, pos):
    d = x.shape[-1]
    rot = d // ROPE_FRAC
    half = rot // 2
    inv = jnp.exp(-math.log(ROPE_THETA) * jnp.arange(half, dtype=F32) * (2.0 / rot))
    ang = pos.astype(F32)[:, None] * inv[None, :]
    cos = jnp.cos(ang)[None, :, None, :]
    sin = jnp.sin(ang)[None, :, None, :]
    xf = x.astype(F32)
    x1, x2, rest = xf[..., :half], xf[..., half:rot], xf[..., rot:]
    out = jnp.concatenate([x1 * cos - x2 * sin, x2 * cos + x1 * sin, rest], axis=-1)
    return out.astype(x.dtype)


def swiglu(h, wg, wu, wd):
    return (jax.nn.silu(h @ wg) * (h @ wu)) @ wd


def moe_swiglu(h, w_router, wg, wu, wd):
    logits = jnp.einsum('bld,de->ble', h, w_router).astype(F32)
    top_val, top_idx = lax.top_k(logits, TOP_K_EXPERTS)
    gates = jax.nn.softmax(top_val, axis=-1)
    dense_gate = jnp.sum(jax.nn.one_hot(top_idx, N_EXPERTS, dtype=F32) * gates[..., None], axis=-2)
    out = jnp.zeros_like(h)
    for e in range(N_EXPERTS):
        y = swiglu(h, wg[e], wu[e], wd[e])
        out = out + y * dense_gate[..., e:e + 1].astype(h.dtype)
    return out


def rel_bias_lookup(rel_bias, rel):
    return rel_bias[:, jnp.clip(rel, -A_REL_CLIP, A_REL_CLIP) + A_REL_CLIP].astype(F32)


def band_attend(q, k, v, bias, ok):
    s = jnp.einsum('bnqhd,bnkhd->bnhqk', q, k).astype(F32) * (A_HEAD_DIM ** -0.5) + bias
    s = jnp.where(ok[None, :, None], s, NEG)
    p = jax.nn.softmax(s, axis=-1).astype(v.dtype)
    return jnp.einsum('bnhqk,bnkhd->bnqhd', p, v)


def multiscale_pool(u_ext, n_hist, start_pos, pool_w, pool_scale):
    tot = u_ext.shape[1]
    L = tot - n_hist
    cs = jnp.pad(jnp.cumsum(u_ext.astype(F32), axis=1), ((0, 0), (1, 0), (0, 0)))
    hi = n_hist + jnp.arange(L) + 1
    pos = start_pos + jnp.arange(L)
    outs = []
    for g, w in enumerate(B_WINDOWS):
        lo = jnp.maximum(hi - w, 0)
        csg = cs[:, :, g * B_GROUP:(g + 1) * B_GROUP]
        cnt = jnp.minimum(pos + 1, w).astype(F32)
        mean = (csg[:, hi] - csg[:, lo]) / cnt[None, :, None]
        pooled = (mean - u_ext[:, n_hist:, g * B_GROUP:(g + 1) * B_GROUP].astype(F32)).astype(u_ext.dtype)
        outs.append(jnp.einsum('blc,cd->bld', pooled, pool_w[g]))
    return jnp.concatenate(outs, axis=-1) * pool_scale


def split_even(proj):
    b, L, _ = proj.shape
    shp = (b, L, A_HEADS, A_HEAD_DIM)
    q = proj[..., :A_WIDTH].reshape(shp)
    k = proj[..., A_WIDTH:2 * A_WIDTH].reshape(shp)
    v = proj[..., 2 * A_WIDTH:3 * A_WIDTH].reshape(shp)
    u = proj[..., 3 * A_WIDTH:]
    return q, k, v, u


def even_mixer_prompt(h, w_in, w_out, rel_bias, pool_w, pool_scale):
    b, L, _ = h.shape
    q, k, v, u = split_even(h @ w_in)
    nc = L // CHUNK
    pad = A_PREV_CHUNKS * CHUNK

    def band(t):
        tp = jnp.pad(t, ((0, 0), (pad, 0), (0, 0), (0, 0)))
        tp = tp.reshape(b, nc + A_PREV_CHUNKS, CHUNK, A_HEADS, A_HEAD_DIM)
        return jnp.concatenate([tp[:, j:j + nc] for j in range(A_PREV_CHUNKS + 1)], axis=2)

    k_loc = jnp.arange(A_BAND) - pad
    rel = jnp.arange(CHUNK)[:, None] - k_loc[None, :]
    k_abs = jnp.arange(nc)[:, None] * CHUNK + k_loc[None, :]
    ok = (k_abs >= 0)[:, None, :]
    qc = q.reshape(b, nc, CHUNK, A_HEADS, A_HEAD_DIM)
    a = band_attend(qc, band(k), band(v), rel_bias_lookup(rel_bias, rel), ok).reshape(b, L, A_WIDTH)
    p = multiscale_pool(u, 0, 0, pool_w, pool_scale)
    out = jnp.concatenate([a, p], axis=-1) @ w_out
    keep = min(A_BAND, L)
    return out, k[:, L - keep:], v[:, L - keep:], u[:, L - B_HIST:]


def even_mixer_sample(h, cache_k, cache_v, pool_hist, past, w_in, w_out, rel_bias, pool_w, pool_scale):
    b, ds, _ = h.shape
    q, k, v, u = split_even(h @ w_in)
    a_len = cache_k.shape[1]
    k_all = jnp.concatenate([cache_k, k], axis=1)
    v_all = jnp.concatenate([cache_v, v], axis=1)
    k_pos = past - a_len + jnp.arange(a_len + ds)
    q_pos = past + jnp.arange(ds)
    qch, kch = q_pos // CHUNK, k_pos // CHUNK
    ok = ((k_pos[None, :] >= 0) & (kch[None, :] <= qch[:, None])
          & (kch[None, :] >= qch[:, None] - A_PREV_CHUNKS))
    rel = q_pos[:, None] - k_pos[None, :]
    a = band_attend(q[:, None], k_all[:, None], v_all[:, None],
                    rel_bias_lookup(rel_bias, rel), ok[None])[:, 0].reshape(b, ds, A_WIDTH)
    u_ext = jnp.concatenate([pool_hist, u], axis=1)
    p = multiscale_pool(u_ext, B_HIST, past, pool_w, pool_scale)
    out = jnp.concatenate([a, p], axis=-1) @ w_out
    return out, k_all[:, ds:], v_all[:, ds:], u_ext[:, ds:]


def split_odd(proj, pos):
    b, L, _ = proj.shape
    q, k, v, qi, ki, wi = jnp.split(proj, np.cumsum(ODD_SPLITS)[:-1].tolist(), axis=-1)
    q = partial_rope(q.reshape(b, L, C_HEADS, C_HEAD_DIM), pos)
    k = partial_rope(k.reshape(b, L, C_KV_HEADS, C_HEAD_DIM), pos)
    v = v.reshape(b, L, C_KV_HEADS, C_HEAD_DIM)
    qi = partial_rope(qi.reshape(b, L, IDX_HEADS, IDX_DIM), pos)
    ki = partial_rope(ki[:, :, None, :], pos)[:, :, 0]
    wi = wi * (IDX_HEADS ** -0.5)
    return q, k, v, qi, ki, wi


def dsa_attend(q, k_all, v_all, qi, ki_all, wi, q_pos, k_pos, topk):
    b, nq = q.shape[:2]
    dots = jnp.einsum('bqhe,bse->bqhs', qi, ki_all).astype(F32) * (IDX_DIM ** -0.5)
    iscore = jnp.einsum('bqh,bqhs->bqs', wi.astype(F32), jax.nn.relu(dots))
    q_chunk = q_pos // CHUNK
    admissible = (k_pos[None, :] // CHUNK) <= q_chunk[:, None]
    iscore = jnp.where(admissible[None], iscore, NEG)
    _, sel = lax.top_k(iscore, topk)
    sel_ok = (k_pos[sel] // CHUNK) <= q_chunk[None, :, None]
    gather = jax.vmap(lambda t, s: t[s])
    ks = gather(k_all, sel)
    vs = gather(v_all, sel)
    qg = q.reshape(b, nq, C_KV_HEADS, C_GROUPS, C_HEAD_DIM)
    s = jnp.einsum('bqkgd,bqjkd->bqkgj', qg, ks).astype(F32) * (C_HEAD_DIM ** -0.5)
    s = jnp.where(sel_ok[:, :, None, None, :], s, NEG)
    p = jax.nn.softmax(s, axis=-1).astype(v_all.dtype)
    o = jnp.einsum('bqkgj,bqjkd->bqkgd', p, vs)
    return o.reshape(b, nq, C_HEADS * C_HEAD_DIM)


def odd_mixer_prompt(h, w_in, w_out):
    b, L, _ = h.shape
    pos = jnp.arange(L)
    q, k, v, qi, ki, wi = split_odd(h @ w_in, pos)
    topk = min(TOPK_MAX, L // 4)
    nb = L // Q_BLOCK

    def blocks(t):
        return jnp.moveaxis(t.reshape((b, nb, Q_BLOCK) + t.shape[2:]), 1, 0)

    def one_block(args):
        qb, qib, wib, pb = args
        return dsa_attend(qb, k, v, qib, ki, wib, pb, pos, topk)

    o = lax.map(one_block, (blocks(q), blocks(qi), blocks(wi), pos.reshape(nb, Q_BLOCK)))
    o = jnp.moveaxis(o, 0, 1).reshape(b, L, C_HEADS * C_HEAD_DIM)
    return o @ w_out, k, v, ki


def odd_mixer_sample(h, cache_k, cache_v, cache_idx, w_in, w_out):
    b, ds, _ = h.shape
    past = cache_k.shape[1]
    q_pos = past + jnp.arange(ds)
    q, k, v, qi, ki, wi = split_odd(h @ w_in, q_pos)
    k_all = jnp.concatenate([cache_k, k], axis=1)
    v_all = jnp.concatenate([cache_v, v], axis=1)
    ki_all = jnp.concatenate([cache_idx, ki], axis=1)
    k_pos = jnp.arange(past + ds)
    topk = min(TOPK_MAX, (past + ds) // 4)
    o = dsa_attend(q, k_all, v_all, qi, ki_all, wi, q_pos, k_pos, topk)
    return o @ w_out, k, v, ki


def setup_inputs(seed: int = 0) -> dict:
    key = jax.random.key(seed)
    ks = jax.random.split(key, 32)

    def nrm(k, shape, scale):
        return jax.random.normal(k, shape, F32) * scale

    a_len = min(A_BAND, PAST_LEN)
    return {
        "x_prompt": nrm(ks[0], (BATCH, SEQ, D_MODEL), 1.0),
        "x_sample": nrm(ks[1], (DEC_BATCH, DEC_SEQ, D_MODEL), 1.0),
        "cache_a_k": nrm(ks[2], (N_EVEN, DEC_BATCH, a_len, A_HEADS, A_HEAD_DIM), 1.0),
        "cache_a_v": nrm(ks[3], (N_EVEN, DEC_BATCH, a_len, A_HEADS, A_HEAD_DIM), 1.0),
        "state_pool": nrm(ks[4], (N_EVEN, DEC_BATCH, B_HIST, B_WIDTH), 1.0),
        "cache_c_k": nrm(ks[5], (N_ODD, DEC_BATCH, PAST_LEN, C_KV_HEADS, C_HEAD_DIM), 1.0),
        "cache_c_v": nrm(ks[6], (N_ODD, DEC_BATCH, PAST_LEN, C_KV_HEADS, C_HEAD_DIM), 1.0),
        "cache_c_idx": nrm(ks[7], (N_ODD, DEC_BATCH, PAST_LEN, IDX_DIM), 1.0),
        "norm_mix": 1.0 + nrm(ks[8], (DEPTH, D_MODEL), 0.02),
        "norm_ffn": 1.0 + nrm(ks[9], (DEPTH, D_MODEL), 0.02),
        "norm_final": 1.0 + nrm(ks[10], (D_MODEL,), 0.02),
        "w_in_even": nrm(ks[11], (N_EVEN, D_MODEL, EVEN_IN), D_MODEL ** -0.5),
        "w_out_even": nrm(ks[12], (N_EVEN, EVEN_MIX, D_MODEL), EVEN_MIX ** -0.5),
        "a_rel_bias": nrm(ks[13], (N_EVEN, A_HEADS, 2 * A_REL_CLIP + 1), 0.5),
        "pool_w": nrm(ks[14], (N_EVEN, len(B_WINDOWS), B_GROUP, B_GROUP), B_GROUP ** -0.5),
        "pool_scale": 1.0 + nrm(ks[15], (N_EVEN, B_WIDTH), 0.1),
        "ffn_w_gate": nrm(ks[16], (N_EVEN, D_MODEL, D_FF), D_MODEL ** -0.5),
        "ffn_w_up": nrm(ks[17], (N_EVEN, D_MODEL, D_FF), D_MODEL ** -0.5),
        "ffn_w_down": nrm(ks[18], (N_EVEN, D_FF, D_MODEL), D_FF ** -0.5),
        "w_in_odd": nrm(ks[19], (N_ODD, D_MODEL, ODD_IN), D_MODEL ** -0.5),
        "w_out_odd": nrm(ks[20], (N_ODD, C_HEADS * C_HEAD_DIM, D_MODEL), (C_HEADS * C_HEAD_DIM) ** -0.5),
        "moe_router": nrm(ks[21], (N_ODD, D_MODEL, N_EXPERTS), D_MODEL ** -0.5),
        "moe_w_gate": nrm(ks[22], (N_ODD, N_EXPERTS, D_MODEL, D_FF_EXPERT), D_MODEL ** -0.5),
        "moe_w_up": nrm(ks[23], (N_ODD, N_EXPERTS, D_MODEL, D_FF_EXPERT), D_MODEL ** -0.5),
        "moe_w_down": nrm(ks[24], (N_ODD, N_EXPERTS, D_FF_EXPERT, D_MODEL), D_FF_EXPERT ** -0.5),
    }


def reference(x_prompt, x_sample, cache_a_k, cache_a_v, state_pool, cache_c_k, cache_c_v, cache_c_idx,
              norm_mix, norm_ffn, norm_final, w_in_even, w_out_even, a_rel_bias, pool_w, pool_scale,
              ffn_w_gate, ffn_w_up, ffn_w_down, w_in_odd, w_out_odd,
              moe_router, moe_w_gate, moe_w_up, moe_w_down):
    past = cache_c_k.shape[2]
    yp, ys = x_prompt, x_sample
    akp, avp, plp, ckp, cvp, cip = [], [], [], [], [], []
    aks, avs, pls, cks, cvs, cis = [], [], [], [], [], []
    for layer in range(DEPTH):
        i = layer // 2
        hp = rmsnorm(yp, norm_mix[layer])
        hs = rmsnorm(ys, norm_mix[layer])
        if layer % 2 == 0:
            mp, k_p, v_p, u_p = even_mixer_prompt(hp, w_in_even[i], w_out_even[i], a_rel_bias[i],
                                                  pool_w[i], pool_scale[i])
            ms, k_s, v_s, u_s = even_mixer_sample(hs, cache_a_k[i], cache_a_v[i], state_pool[i], past,
                                                  w_in_even[i], w_out_even[i], a_rel_bias[i],
                                                  pool_w[i], pool_scale[i])
            akp.append(k_p); avp.append(v_p); plp.append(u_p)
            aks.append(k_s); avs.append(v_s); pls.append(u_s)
            yp = yp + mp
            ys = ys + ms
            yp = yp + swiglu(rmsnorm(yp, norm_ffn[layer]), ffn_w_gate[i], ffn_w_up[i], ffn_w_down[i])
            ys = ys + swiglu(rmsnorm(ys, norm_ffn[layer]), ffn_w_gate[i], ffn_w_up[i], ffn_w_down[i])
        else:
            mp, k_p, v_p, i_p = odd_mixer_prompt(hp, w_in_odd[i], w_out_odd[i])
            ms, k_s, v_s, i_s = odd_mixer_sample(hs, cache_c_k[i], cache_c_v[i], cache_c_idx[i],
                                                 w_in_odd[i], w_out_odd[i])
            ckp.append(k_p); cvp.append(v_p); cip.append(i_p)
            cks.append(k_s); cvs.append(v_s); cis.append(i_s)
            yp = yp + mp
            ys = ys + ms
            yp = yp + moe_swiglu(rmsnorm(yp, norm_ffn[layer]), moe_router[i], moe_w_gate[i], moe_w_up[i], moe_w_down[i])
            ys = ys + moe_swiglu(rmsnorm(ys, norm_ffn[layer]), moe_router[i], moe_w_gate[i], moe_w_up[i], moe_w_down[i])
    y_prompt = rmsnorm(yp, norm_final)
    y_sample = rmsnorm(ys, norm_final)
    return (y_prompt, y_sample,
            jnp.stack(akp), jnp.stack(avp), jnp.stack(plp),
            jnp.stack(ckp), jnp.stack(cvp), jnp.stack(cip),
            jnp.stack(aks), jnp.stack(avs), jnp.stack(pls),
            jnp.stack(cks), jnp.stack(cvs), jnp.stack(cis))
```

```python
import functools
import math

import jax
import jax.numpy as jnp
from jax import lax
from jax.experimental import pallas as pl
from jax.experimental.pallas import tpu as pltpu

F32 = jnp.float32
BF16 = jnp.bfloat16

NORM_EPS = 1e-6
NEG = -1e30
BIG = 1e30

CHUNK = 64
A_HEADS = 16
A_HEAD_DIM = 64
A_WIDTH = A_HEADS * A_HEAD_DIM
A_PREV_CHUNKS = 8
A_BAND = (A_PREV_CHUNKS + 1) * CHUNK
A_REL_CLIP = 128
B_WINDOWS = (2, 4, 8, 16)
B_GROUP = 256
B_WIDTH = B_GROUP * len(B_WINDOWS)
B_HIST = max(B_WINDOWS) - 1
C_HEADS = 16
C_KV_HEADS = 4
C_HEAD_DIM = 128
C_GROUPS = C_HEADS // C_KV_HEADS
IDX_HEADS = 8
IDX_DIM = 64
TOPK_MAX = 256
ROPE_THETA = 500000.0
ROPE_FRAC = 4
N_EXPERTS = 8

LANES = 128
POOL_HALO = 16
VMEM_LIMIT = 56 * 1024 * 1024

_NT = (((1,), (1,)), ((), ()))


def _params(*sem):
    return pltpu.CompilerParams(dimension_semantics=sem, vmem_limit_bytes=VMEM_LIMIT)


def _rms(x, g):
    ms = jnp.mean(x * x, axis=-1, keepdims=True)
    return x * lax.rsqrt(ms + NORM_EPS) * g


def _norm_proj_kernel(x_ref, g_ref, w_ref, o_ref, h_ref):
    @pl.when(pl.program_id(1) == 0)
    def _():
        h_ref[...] = _rms(x_ref[...], g_ref[...]).astype(BF16)

    o_ref[...] = jnp.dot(h_ref[...], w_ref[...], preferred_element_type=F32)


def norm_proj(x, g, w, tm, tn):
    m, d = x.shape
    n = w.shape[1]
    return pl.pallas_call(
        _norm_proj_kernel,
        grid=(m // tm, n // tn),
        in_specs=[pl.BlockSpec((tm, d), lambda i, j: (i, 0)),
                  pl.BlockSpec((1, d), lambda i, j: (0, 0)),
                  pl.BlockSpec((d, tn), lambda i, j: (0, j))],
        out_specs=pl.BlockSpec((tm, tn), lambda i, j: (i, j)),
        out_shape=jax.ShapeDtypeStruct((m, n), F32),
        scratch_shapes=[pltpu.VMEM((tm, d), BF16)],
        compiler_params=_params("parallel", "arbitrary"),
    )(x, g.reshape(1, d), w)


def _band_pairs(q_ref, q_row0, nq, kw_ref, vw_ref, k_row0, nk, bias_ref, key_ok, o_ref):
    lane = lax.broadcasted_iota(jnp.int32, (1, LANES), 1)
    first = lane < A_HEAD_DIM
    for hp in range(A_HEADS // 2):
        cs = slice(hp * LANES, (hp + 1) * LANES)
        qp = q_ref[pl.ds(q_row0, nq), cs]
        kp = kw_ref[pl.ds(k_row0, nk), cs]
        vp = vw_ref[pl.ds(k_row0, nk), cs]
        outs = []
        for half in range(2):
            keep = first if half == 0 else jnp.logical_not(first)
            qm = jnp.where(keep, qp, 0.0).astype(BF16)
            s = lax.dot_general(qm, kp, _NT, preferred_element_type=F32)
            s = s * (A_HEAD_DIM ** -0.5) + bias_ref[hp * 2 + half]
            if key_ok is not None:
                s = jnp.where(key_ok, s, NEG)
            mx = jnp.max(s, axis=-1, keepdims=True)
            e = jnp.exp(s - mx)
            l = jnp.sum(e, axis=-1, keepdims=True)
            outs.append(jnp.dot(e.astype(BF16), vp, preferred_element_type=F32) / l)
        o_ref[pl.ds(q_row0, nq), cs] = jnp.where(first, outs[0], outs[1]).astype(o_ref.dtype)


def _band_prompt_kernel(q_ref, kp_ref, kc_ref, vp_ref, vc_ref, bias_ref, o_ref, kw_ref, vw_ref, *, qb, pad):
    i = pl.program_id(0)
    kw_ref[0:pad, :] = kp_ref[...].astype(BF16)
    kw_ref[pad:pad + qb, :] = kc_ref[...].astype(BF16)
    vw_ref[0:pad, :] = vp_ref[...].astype(BF16)
    vw_ref[pad:pad + qb, :] = vc_ref[...].astype(BF16)
    col = lax.broadcasted_iota(jnp.int32, (1, A_BAND), 1)

    def chunk(cc, carry):
        r0 = pl.multiple_of(cc * CHUNK, CHUNK)
        first_valid = jnp.where(i == 0, pad - cc * CHUNK, 0)
        _band_pairs(q_ref, r0, CHUNK, kw_ref, vw_ref, r0, A_BAND, bias_ref, col >= first_valid, o_ref)
        return carry

    lax.fori_loop(0, qb // CHUNK, chunk, 0)


def band_prompt(proj, bias, lp):
    pad = A_PREV_CHUNKS * CHUNK
    qb = pad
    blk = (qb, A_WIDTH)
    prev = lambda c: (lambda i: (jnp.maximum(i - 1, 0), c))
    cur = lambda c: (lambda i: (i, c))
    return pl.pallas_call(
        functools.partial(_band_prompt_kernel, qb=qb, pad=pad),
        grid=(lp // qb,),
        in_specs=[pl.BlockSpec(blk, cur(0)),
                  pl.BlockSpec(blk, prev(1)), pl.BlockSpec(blk, cur(1)),
                  pl.BlockSpec(blk, prev(2)), pl.BlockSpec(blk, cur(2)),
                  pl.BlockSpec((A_HEADS, CHUNK, A_BAND), lambda i: (0, 0, 0))],
        out_specs=pl.BlockSpec(blk, lambda i: (i, 0)),
        out_shape=jax.ShapeDtypeStruct((lp, A_WIDTH), BF16),
        scratch_shapes=[pltpu.VMEM((pad + qb, A_WIDTH), BF16), pltpu.VMEM((pad + qb, A_WIDTH), BF16)],
        compiler_params=_params("parallel"),
    )(proj, proj, proj, proj, proj, bias)


def _band_sample_kernel(q_ref, kn_ref, vn_ref, ck_ref, cv_ref, bias_ref, o_ref, kw_ref, vw_ref, *, a_len, ds):
    kw_ref[0:a_len, :] = ck_ref[...].astype(BF16)
    kw_ref[a_len:a_len + ds, :] = kn_ref[...].astype(BF16)
    vw_ref[0:a_len, :] = cv_ref[...].astype(BF16)
    vw_ref[a_len:a_len + ds, :] = vn_ref[...].astype(BF16)
    _band_pairs(q_ref, 0, ds, kw_ref, vw_ref, 0, a_len + ds, bias_ref, None, o_ref)


def band_sample(proj, cache_k, cache_v, bias, lp, nb, ds):
    a_len = cache_k.shape[1]
    row = lambda c: (lambda b: (lp // ds + b, c))
    return pl.pallas_call(
        functools.partial(_band_sample_kernel, a_len=a_len, ds=ds),
        grid=(nb,),
        in_specs=[pl.BlockSpec((ds, A_WIDTH), row(0)),
                  pl.BlockSpec((ds, A_WIDTH), row(1)),
                  pl.BlockSpec((ds, A_WIDTH), row(2)),
                  pl.BlockSpec((None, a_len, A_WIDTH), lambda b: (b, 0, 0)),
                  pl.BlockSpec((None, a_len, A_WIDTH), lambda b: (b, 0, 0)),
                  pl.BlockSpec((A_HEADS, ds, a_len + ds), lambda b: (0, 0, 0))],
        out_specs=pl.BlockSpec((ds, A_WIDTH), lambda b: (b, 0)),
        out_shape=jax.ShapeDtypeStruct((nb * ds, A_WIDTH), BF16),
        scratch_shapes=[pltpu.VMEM((a_len + ds, A_WIDTH), BF16), pltpu.VMEM((a_len + ds, A_WIDTH), BF16)],
        compiler_params=_params("parallel"),
    )(proj, proj, proj, cache_k, cache_v, bias)


def _rel_bias_tile(rel_bias, q_pos, k_pos):
    rel = q_pos[:, None] - k_pos[None, :]
    return rel_bias[:, jnp.clip(rel, -A_REL_CLIP, A_REL_CLIP) + A_REL_CLIP].astype(F32)


def _pool_kernel(prev_ref, cur_ref, w_ref, sc_ref, o_ref, ext_ref, *, tm, prompt, pos0):
    i = pl.program_id(0)
    prev = prev_ref[...]
    if prompt:
        prev = jnp.where(i == 0, 0.0, prev)
        pos = i * tm + lax.broadcasted_iota(jnp.int32, (tm, 1), 0)
    else:
        pos = pos0 + lax.broadcasted_iota(jnp.int32, (tm, 1), 0)
    ext_ref[0:POOL_HALO, :] = prev
    ext_ref[POOL_HALO:POOL_HALO + tm, :] = cur_ref[...]
    for g, w in enumerate(B_WINDOWS):
        cs = slice(g * B_GROUP, (g + 1) * B_GROUP)
        tok = ext_ref[POOL_HALO:POOL_HALO + tm, cs]
        tot = tok
        for j in range(1, w):
            tot = tot + ext_ref[POOL_HALO - j:POOL_HALO - j + tm, cs]
        cnt = jnp.minimum(pos + 1, w).astype(F32)
        pooled = (tot / cnt - tok).astype(BF16)
        o = jnp.dot(pooled, w_ref[g], preferred_element_type=F32) * sc_ref[:, cs]
        o_ref[:, cs] = o.astype(o_ref.dtype)


def pool_prompt(proj, pool_w, pool_scale, lp, tm):
    ucol = 3 * A_WIDTH // B_WIDTH
    per = tm // POOL_HALO
    return pl.pallas_call(
        functools.partial(_pool_kernel, tm=tm, prompt=True, pos0=0),
        grid=(lp // tm,),
        in_specs=[pl.BlockSpec((POOL_HALO, B_WIDTH), lambda i: (jnp.maximum(i * per - 1, 0), ucol)),
                  pl.BlockSpec((tm, B_WIDTH), lambda i: (i, ucol)),
                  pl.BlockSpec((len(B_WINDOWS), B_GROUP, B_GROUP), lambda i: (0, 0, 0)),
                  pl.BlockSpec((1, B_WIDTH), lambda i: (0, 0))],
        out_specs=pl.BlockSpec((tm, B_WIDTH), lambda i: (i, 0)),
        out_shape=jax.ShapeDtypeStruct((lp, B_WIDTH), BF16),
        scratch_shapes=[pltpu.VMEM((POOL_HALO + tm, B_WIDTH), F32)],
        compiler_params=_params("parallel"),
    )(proj, proj, pool_w, pool_scale)


def pool_sample(u_ext, pool_w, pool_scale, past):
    nb, tot, _ = u_ext.shape
    ds = tot - POOL_HALO
    return pl.pallas_call(
        functools.partial(_pool_kernel, tm=ds, prompt=False, pos0=past),
        grid=(nb,),
        in_specs=[pl.BlockSpec((None, POOL_HALO, B_WIDTH), lambda b: (b, 0, 0)),
                  pl.BlockSpec((None, ds, B_WIDTH), lambda b: (b, POOL_HALO // ds, 0)),
                  pl.BlockSpec((len(B_WINDOWS), B_GROUP, B_GROUP), lambda b: (0, 0, 0)),
                  pl.BlockSpec((1, B_WIDTH), lambda b: (0, 0))],
        out_specs=pl.BlockSpec((ds, B_WIDTH), lambda b: (b, 0)),
        out_shape=jax.ShapeDtypeStruct((nb * ds, B_WIDTH), BF16),
        scratch_shapes=[pltpu.VMEM((POOL_HALO + ds, B_WIDTH), F32)],
        compiler_params=_params("parallel"),
    )(u_ext, u_ext, pool_w, pool_scale)


def _mm_res_kernel(*refs, n_in):
    xs, ws = refs[:n_in], refs[n_in:2 * n_in]
    res_ref, o_ref = refs[2 * n_in], refs[2 * n_in + 1]
    acc = res_ref[...]
    for x_ref, w_ref in zip(xs, ws):
        acc = acc + jnp.dot(x_ref[...], w_ref[...], preferred_element_type=F32)
    o_ref[...] = acc


def mm_res(xs, ws, res, tm, tn):
    m, n = res.shape
    n_in = len(xs)
    in_specs = ([pl.BlockSpec((tm, x.shape[1]), lambda i, j: (i, 0)) for x in xs]
                + [pl.BlockSpec((w.shape[0], tn), lambda i, j: (0, j)) for w in ws]
                + [pl.BlockSpec((tm, tn), lambda i, j: (i, j))])
    return pl.pallas_call(
        functools.partial(_mm_res_kernel, n_in=n_in),
        grid=(m // tm, n // tn),
        in_specs=in_specs,
        out_specs=pl.BlockSpec((tm, tn), lambda i, j: (i, j)),
        out_shape=jax.ShapeDtypeStruct((m, n), F32),
        compiler_params=_params("parallel", "arbitrary"),
    )(*xs, *ws, res)


def _swiglu_tile(h, wg, wu, wd):
    a = jnp.dot(h, wg, preferred_element_type=F32)
    b = jnp.dot(h, wu, preferred_element_type=F32)
    act = (a * jax.nn.sigmoid(a) * b).astype(BF16)
    return jnp.dot(act, wd, preferred_element_type=F32)


def _ffn_kernel(y_ref, g_ref, wg_ref, wu_ref, wd_ref, o_ref, h_ref, acc_ref):
    f = pl.program_id(1)

    @pl.when(f == 0)
    def _():
        h_ref[...] = _rms(y_ref[...], g_ref[...]).astype(BF16)
        acc_ref[...] = jnp.zeros_like(acc_ref)

    acc_ref[...] += _swiglu_tile(h_ref[...], wg_ref[...], wu_ref[...], wd_ref[...])

    @pl.when(f == pl.num_programs(1) - 1)
    def _():
        o_ref[...] = y_ref[...] + acc_ref[...]


def ffn(y, g, wg, wu, wd, tm, tf):
    m, d = y.shape
    ff = wg.shape[1]
    return pl.pallas_call(
        _ffn_kernel,
        grid=(m // tm, ff // tf),
        in_specs=[pl.BlockSpec((tm, d), lambda i, f: (i, 0)),
                  pl.BlockSpec((1, d), lambda i, f: (0, 0)),
                  pl.BlockSpec((d, tf), lambda i, f: (0, f)),
                  pl.BlockSpec((d, tf), lambda i, f: (0, f)),
                  pl.BlockSpec((tf, d), lambda i, f: (f, 0))],
        out_specs=pl.BlockSpec((tm, d), lambda i, f: (i, 0)),
        out_shape=jax.ShapeDtypeStruct((m, d), F32),
        scratch_shapes=[pltpu.VMEM((tm, d), BF16), pltpu.VMEM((tm, d), F32)],
        compiler_params=_params("parallel", "arbitrary"),
    )(y, g.reshape(1, d), wg, wu, wd)


def _rope_tables(pos, head_dim):
    rot = head_dim // ROPE_FRAC
    half = rot // 2
    inv = jnp.exp(-math.log(ROPE_THETA) * jnp.arange(half, dtype=F32) * (2.0 / rot))
    ang = pos.astype(F32)[:, None] * inv[None, :]
    cos, sin = jnp.cos(ang), jnp.sin(ang)
    m = pos.shape[0]
    one = jnp.ones((m, head_dim - rot), F32)
    zero_r = jnp.zeros((m, head_dim - rot), F32)
    zero_h = jnp.zeros((m, half), F32)
    c = jnp.concatenate([cos, cos, one], axis=1)
    s_dn = jnp.concatenate([-sin, zero_h, zero_r], axis=1)
    s_up = jnp.concatenate([zero_h, sin, zero_r], axis=1)
    rep = LANES // head_dim
    return jnp.stack([jnp.tile(c, (1, rep)), jnp.tile(s_dn, (1, rep)), jnp.tile(s_up, (1, rep))])


def _rot(x, tab_ref, half):
    return (x * tab_ref[0] + pltpu.roll(x, LANES - half, 1) * tab_ref[1]
            + pltpu.roll(x, half, 1) * tab_ref[2])


def _rope_kernel(main_ref, tail_ref, tq_ref, ti_ref, o_ref, *, n_qk, n_v, n_qi, wi_scale):
    half_qk = C_HEAD_DIM // ROPE_FRAC // 2
    half_i = IDX_DIM // ROPE_FRAC // 2
    for c in range(n_qk + n_v + n_qi):
        cs = slice(c * LANES, (c + 1) * LANES)
        x = main_ref[:, cs]
        if c < n_qk:
            x = _rot(x, tq_ref, half_qk)
        elif c >= n_qk + n_v:
            x = _rot(x, ti_ref, half_i)
        o_ref[:, cs] = x
    t = tail_ref[...]
    lane = lax.broadcasted_iota(jnp.int32, (1, LANES), 1)
    c = n_qk + n_v + n_qi
    o_ref[:, c * LANES:(c + 1) * LANES] = jnp.where(lane < IDX_DIM, _rot(t, ti_ref, half_i), t * wi_scale)


def rope_all(main, tail, tab_qk, tab_idx, tm):
    m, nmain = main.shape
    n_qk = (C_HEADS + C_KV_HEADS) * C_HEAD_DIM // LANES
    n_v = C_KV_HEADS * C_HEAD_DIM // LANES
    n_qi = IDX_HEADS * IDX_DIM // LANES
    wi_scale = (IDX_HEADS ** -0.5) * (IDX_DIM ** -0.5)
    return pl.pallas_call(
        functools.partial(_rope_kernel, n_qk=n_qk, n_v=n_v, n_qi=n_qi, wi_scale=wi_scale),
        grid=(m // tm,),
        in_specs=[pl.BlockSpec((tm, nmain), lambda i: (i, 0)),
                  pl.BlockSpec((tm, LANES), lambda i: (i, 0)),
                  pl.BlockSpec((3, tm, LANES), lambda i: (0, i, 0)),
                  pl.BlockSpec((3, tm, LANES), lambda i: (0, i, 0))],
        out_specs=pl.BlockSpec((tm, nmain + LANES), lambda i: (i, 0)),
        out_shape=jax.ShapeDtypeStruct((m, nmain + LANES), F32),
        compiler_params=_params("parallel"),
    )(main, tail, tab_qk, tab_idx)


def _dsa_kernel(q_ref, qi_ref, wi_ref, k_ref, v_ref, ki_ref, o_ref,
                sc_ref, qs_ref, acc_ref, m_ref, l_ref,
                *, tq, kb, n_valid, q_pos0, topk):
    i = pl.program_id(1)
    rows = C_GROUPS * tq
    sub = kb // LANES
    kf = float(topk)

    q_pos = q_pos0 + i * tq + lax.broadcasted_iota(jnp.int32, (tq, 1), 0)
    q_chunk = q_pos // CHUNK
    last_chunk = (q_pos0 + i * tq + tq - 1) // CHUNK
    kv_limit = jnp.minimum(n_valid, (last_chunk + 1) * CHUNK)
    nkb = (kv_limit + kb - 1) // kb

    lane = lax.broadcasted_iota(jnp.int32, (1, LANES), 1)
    first = lane < IDX_DIM
    wi = wi_ref[...]

    def score_block(b, carry):
        for c in range(sub):
            off = pl.multiple_of(b * kb + c * LANES, LANES)
            kib = ki_ref[pl.ds(off, LANES), :]
            acc = jnp.zeros((tq, LANES), F32)
            for hp in range(IDX_HEADS // 2):
                qp = qi_ref[:, hp * LANES:(hp + 1) * LANES]
                for half in range(2):
                    keep = first if half == 0 else jnp.logical_not(first)
                    qm = jnp.where(keep, qp, jnp.zeros_like(qp))
                    d = lax.dot_general(qm, kib, _NT, preferred_element_type=F32)
                    h = hp * 2 + half
                    acc = acc + jnp.maximum(d, 0.0) * wi[:, h:h + 1]
            k_pos = off + lane
            adm = jnp.logical_and(k_pos // CHUNK <= q_chunk, k_pos < n_valid)
            sc_ref[b, :, c * LANES:(c + 1) * LANES] = jnp.where(adm, acc, NEG)
        return carry

    lax.fori_loop(0, nkb, score_block, 0)

    def lane_sum(x):
        return jnp.sum(x, axis=1, keepdims=True)

    def count_ge(t):
        def body(b, acc):
            for c in range(sub):
                blk = sc_ref[b, :, c * LANES:(c + 1) * LANES]
                acc = acc + jnp.where(blk >= t, 1.0, 0.0)
            return acc
        return lane_sum(lax.fori_loop(0, nkb, body, jnp.zeros((tq, LANES), F32)))

    def stats(b, carry):
        mx, mn, cnt = carry
        for c in range(sub):
            blk = sc_ref[b, :, c * LANES:(c + 1) * LANES]
            ok = blk > 0.5 * NEG
            mx = jnp.maximum(mx, blk)
            mn = jnp.minimum(mn, jnp.where(ok, blk, BIG))
            cnt = cnt + jnp.where(ok, 1.0, 0.0)
        return mx, mn, cnt

    mx, mn, cnt = lax.fori_loop(
        0, nkb, stats,
        (jnp.full((tq, LANES), NEG, F32), jnp.full((tq, LANES), BIG, F32), jnp.zeros((tq, LANES), F32)))
    row_max = jnp.max(mx, axis=1, keepdims=True)
    row_min = jnp.min(mn, axis=1, keepdims=True)
    n_adm = lane_sum(cnt)

    done0 = jnp.where(n_adm <= kf, 1.0, 0.0)
    state0 = (row_min, jnp.full((tq, 1), BIG, F32), row_max, jnp.full((tq, 1), 0.5 * NEG, F32), done0)

    def bisect(state, n_steps):
        def cond(c):
            it, st = c
            return jnp.logical_and(it < n_steps, jnp.min(st[4]) < 0.5)

        def body(c):
            it, (lo, hi, mid, thr, done) = c
            cnt = count_ge(mid)
            live = done < 0.5
            hit = jnp.logical_and(live, cnt == kf)
            ge = cnt >= kf
            thr = jnp.where(hit, mid, thr)
            done = jnp.where(hit, 1.0, done)
            lo = jnp.where(ge, mid, lo)
            hi = jnp.where(ge, hi, mid)
            return it + 1, (lo, hi, 0.5 * (lo + hi), thr, done)

        return lax.while_loop(cond, body, (jnp.int32(0), state))[1]

    def snap(state):
        lo, hi, mid, thr, done = state

        def body(b, carry):
            v_lo, v_hi = carry
            for c in range(sub):
                blk = sc_ref[b, :, c * LANES:(c + 1) * LANES]
                v_lo = jnp.minimum(v_lo, jnp.where(blk >= lo, blk, BIG))
                v_hi = jnp.maximum(v_hi, jnp.where(blk < hi, blk, NEG))
            return v_lo, v_hi

        v_lo, v_hi = lax.fori_loop(0, nkb, body,
                                   (jnp.full((tq, LANES), BIG, F32), jnp.full((tq, LANES), NEG, F32)))
        v_lo = jnp.min(v_lo, axis=1, keepdims=True)
        v_hi = jnp.max(v_hi, axis=1, keepdims=True)
        live = done < 0.5
        tie = jnp.logical_and(live, v_lo == v_hi)
        thr = jnp.where(tie, v_lo, thr)
        done = jnp.where(tie, 1.0, done)
        lo = jnp.where(live, v_lo, lo)
        return (lo, hi, 0.5 * (lo + hi), thr, done), jnp.where(tie, 1.0, 0.0)

    state = bisect(state0, 32)

    def refine_cond(c):
        rounds, st, _ = c
        return jnp.logical_and(rounds < 10, jnp.min(st[4]) < 0.5)

    def refine_body(c):
        rounds, st, tie = c
        st, new_tie = snap(st)
        st = bisect(st, 32)
        return rounds + 1, st, jnp.maximum(tie, new_tie)

    _, state, tie = lax.while_loop(refine_cond, refine_body,
                                   (jnp.int32(0), state, jnp.zeros((tq, 1), F32)))
    thr = state[3]
    any_tie = jnp.max(tie) > 0.5

    @pl.when(jnp.logical_not(any_tie))
    def _():
        def body(b, carry):
            for c in range(sub):
                cs = slice(c * LANES, (c + 1) * LANES)
                sc_ref[b, :, cs] = jnp.where(sc_ref[b, :, cs] >= thr, 0.0, NEG)
            return carry
        lax.fori_loop(0, nkb, body, 0)

    @pl.when(any_tie)
    def _():
        def gt_body(b, acc):
            for c in range(sub):
                acc = acc + jnp.where(sc_ref[b, :, c * LANES:(c + 1) * LANES] > thr, 1.0, 0.0)
            return acc
        need = kf - lane_sum(lax.fori_loop(0, nkb, gt_body, jnp.zeros((tq, LANES), F32)))

        def count_eq_upto(j):
            def body(b, acc):
                for c in range(sub):
                    blk = sc_ref[b, :, c * LANES:(c + 1) * LANES]
                    idx = (b * kb + c * LANES + lane).astype(F32)
                    acc = acc + jnp.where(jnp.logical_and(blk == thr, idx <= j), 1.0, 0.0)
                return acc
            return lane_sum(lax.fori_loop(0, nkb, body, jnp.zeros((tq, LANES), F32)))

        def idx_body(_, c):
            lo_j, hi_j = c
            mid_j = jnp.floor(0.5 * (lo_j + hi_j))
            ok = count_eq_upto(mid_j) >= need
            return jnp.where(ok, lo_j, mid_j), jnp.where(ok, mid_j, hi_j)

        n_steps = max(1, math.ceil(math.log2(sc_ref.shape[0] * kb + 1)))
        _, last = lax.fori_loop(
            0, n_steps, idx_body,
            (jnp.full((tq, 1), -1.0, F32), jnp.full((tq, 1), float(sc_ref.shape[0] * kb), F32)))
        last = jnp.where(tie > 0.5, last, BIG)

        def body(b, carry):
            for c in range(sub):
                cs = slice(c * LANES, (c + 1) * LANES)
                blk = sc_ref[b, :, cs]
                idx = (b * kb + c * LANES + lane).astype(F32)
                sel = jnp.logical_or(blk > thr, jnp.logical_and(blk == thr, idx <= last))
                sc_ref[b, :, cs] = jnp.where(sel, 0.0, NEG)
            return carry
        lax.fori_loop(0, nkb, body, 0)

    for g in range(C_KV_HEADS):
        for hh in range(C_GROUPS):
            h = g * C_GROUPS + hh
            qs_ref[g, hh * tq:(hh + 1) * tq, :] = q_ref[:, h * C_HEAD_DIM:(h + 1) * C_HEAD_DIM]
    m_ref[...] = jnp.full(m_ref.shape, NEG, F32)
    l_ref[...] = jnp.zeros(l_ref.shape, F32)
    acc_ref[...] = jnp.zeros(acc_ref.shape, F32)

    def attend(b, carry):
        off = pl.multiple_of(b * kb, kb)
        bias = sc_ref[b]
        bias = jnp.concatenate([bias] * C_GROUPS, axis=0)
        for g in range(C_KV_HEADS):
            cs = slice(g * C_HEAD_DIM, (g + 1) * C_HEAD_DIM)
            kblk = k_ref[pl.ds(off, kb), cs]
            vblk = v_ref[pl.ds(off, kb), cs]
            s = lax.dot_general(qs_ref[g], kblk, _NT, preferred_element_type=F32)
            s = s * (C_HEAD_DIM ** -0.5) + bias
            m_old = m_ref[g]
            m_new = jnp.maximum(m_old, jnp.max(s, axis=1, keepdims=True))
            alpha = jnp.exp(m_old - m_new)
            p = jnp.exp(s - m_new)
            l_ref[g] = alpha * l_ref[g] + jnp.sum(p, axis=1, keepdims=True)
            acc_ref[g] = alpha * acc_ref[g] + jnp.dot(p.astype(BF16), vblk, preferred_element_type=F32)
            m_ref[g] = m_new
        return carry

    lax.fori_loop(0, nkb, attend, 0)

    for g in range(C_KV_HEADS):
        o = acc_ref[g] / l_ref[g]
        for hh in range(C_GROUPS):
            h = g * C_GROUPS + hh
            o_ref[:, h * C_HEAD_DIM:(h + 1) * C_HEAD_DIM] = o[hh * tq:(hh + 1) * tq, :].astype(o_ref.dtype)


def dsa(q, qi, wi, k, v, ki2, *, tq, kb, n_valid, q_pos0, topk):
    nbatch, nq, dq = q.shape
    s_pad = k.shape[1]
    rows = C_GROUPS * tq
    qmap = lambda b, i: (b, i, 0)
    kmap = lambda b, i: (b, 0, 0)
    return pl.pallas_call(
        functools.partial(_dsa_kernel, tq=tq, kb=kb, n_valid=n_valid, q_pos0=q_pos0, topk=topk),
        grid=(nbatch, nq // tq),
        in_specs=[pl.BlockSpec((None, tq, dq), qmap),
                  pl.BlockSpec((None, tq, qi.shape[2]), qmap),
                  pl.BlockSpec((None, tq, wi.shape[2]), qmap),
                  pl.BlockSpec((None, s_pad, k.shape[2]), kmap),
                  pl.BlockSpec((None, s_pad, v.shape[2]), kmap),
                  pl.BlockSpec((None, s_pad, LANES), kmap)],
        out_specs=pl.BlockSpec((None, tq, dq), qmap),
        out_shape=jax.ShapeDtypeStruct((nbatch, nq, dq), BF16),
        scratch_shapes=[pltpu.VMEM((s_pad // kb, tq, kb), F32),
                        pltpu.VMEM((C_KV_HEADS, rows, C_HEAD_DIM), BF16),
                        pltpu.VMEM((C_KV_HEADS, rows, C_HEAD_DIM), F32),
                        pltpu.VMEM((C_KV_HEADS, rows, 1), F32),
                        pltpu.VMEM((C_KV_HEADS, rows, 1), F32)],
        compiler_params=_params("parallel", "arbitrary"),
    )(q, qi, wi, k, v, ki2)


def _router_kernel(y_ref, g_ref, wr_ref, h_ref, gate_ref):
    h = _rms(y_ref[...], g_ref[...])
    h_ref[...] = h.astype(BF16)
    logits = jnp.dot(h, wr_ref[...], preferred_element_type=F32, precision=lax.Precision.HIGHEST)
    lane = lax.broadcasted_iota(jnp.int32, logits.shape, 1)
    lg = jnp.where(lane < N_EXPERTS, logits, NEG)
    m1 = jnp.max(lg, axis=1, keepdims=True)
    i1 = jnp.min(jnp.where(lg == m1, lane, LANES), axis=1, keepdims=True)
    lg2 = jnp.where(lane == i1, NEG, lg)
    m2 = jnp.max(lg2, axis=1, keepdims=True)
    i2 = jnp.min(jnp.where(lg2 == m2, lane, LANES), axis=1, keepdims=True)
    e = jnp.exp(m2 - m1)
    g1 = 1.0 / (1.0 + e)
    g2 = e / (1.0 + e)
    gate_ref[...] = jnp.where(lane == i1, g1, 0.0) + jnp.where(lane == i2, g2, 0.0)


def router(y, g, w_router_pad, tm):
    m, d = y.shape
    return pl.pallas_call(
        _router_kernel,
        grid=(m // tm,),
        in_specs=[pl.BlockSpec((tm, d), lambda i: (i, 0)),
                  pl.BlockSpec((1, d), lambda i: (0, 0)),
                  pl.BlockSpec((d, LANES), lambda i: (0, 0))],
        out_specs=[pl.BlockSpec((tm, d), lambda i: (i, 0)),
                   pl.BlockSpec((tm, LANES), lambda i: (i, 0))],
        out_shape=[jax.ShapeDtypeStruct((m, d), BF16), jax.ShapeDtypeStruct((m, LANES), F32)],
        compiler_params=_params("parallel"),
    )(y, g.reshape(1, d), w_router_pad)


def _moe_dense_kernel(y_ref, h_ref, gate_ref, gf_ref, wg_ref, wu_ref, wd_ref, o_ref, acc_ref):
    e = pl.program_id(1)
    f = pl.program_id(2)
    last_e = pl.num_programs(1) - 1
    last_f = pl.num_programs(2) - 1

    @pl.when(jnp.logical_and(e == 0, f == 0))
    def _():
        o_ref[...] = y_ref[...]

    @pl.when(f == 0)
    def _():
        acc_ref[...] = jnp.zeros_like(acc_ref)

    acc_ref[...] += _swiglu_tile(h_ref[...], wg_ref[...], wu_ref[...], wd_ref[...])

    @pl.when(f == last_f)
    def _():
        lane = lax.broadcasted_iota(jnp.int32, gate_ref.shape, 1)
        gcol = jnp.sum(jnp.where(lane == e, gate_ref[...], 0.0), axis=1, keepdims=True)
        o_ref[...] += acc_ref[...] * gcol

    @pl.when(jnp.logical_and(e == last_e, f == last_f))
    def _():
        o_ref[...] = _rms(o_ref[...], gf_ref[...])


def moe_dense_final(y, h, gates, g_final, wg, wu, wd, tm, tf):
    m, d = y.shape
    n_e, _, ff = wg.shape
    return pl.pallas_call(
        _moe_dense_kernel,
        grid=(m // tm, n_e, ff // tf),
        in_specs=[pl.BlockSpec((tm, d), lambda i, e, f: (i, 0)),
                  pl.BlockSpec((tm, d), lambda i, e, f: (i, 0)),
                  pl.BlockSpec((tm, LANES), lambda i, e, f: (i, 0)),
                  pl.BlockSpec((1, d), lambda i, e, f: (0, 0)),
                  pl.BlockSpec((None, d, tf), lambda i, e, f: (e, 0, f)),
                  pl.BlockSpec((None, d, tf), lambda i, e, f: (e, 0, f)),
                  pl.BlockSpec((None, tf, d), lambda i, e, f: (e, f, 0))],
        out_specs=pl.BlockSpec((tm, d), lambda i, e, f: (i, 0)),
        out_shape=jax.ShapeDtypeStruct((m, d), F32),
        scratch_shapes=[pltpu.VMEM((tm, d), F32)],
        compiler_params=_params("parallel", "arbitrary", "arbitrary"),
    )(y, h, gates, g_final.reshape(1, d), wg, wu, wd)


def kernel(x_prompt, x_sample, cache_a_k, cache_a_v, state_pool, cache_c_k, cache_c_v, cache_c_idx,
           norm_mix, norm_ffn, norm_final, w_in_even, w_out_even, a_rel_bias, pool_w, pool_scale,
           ffn_w_gate, ffn_w_up, ffn_w_down, w_in_odd, w_out_odd,
           moe_router, moe_w_gate, moe_w_up, moe_w_down):
    nbp, lp, d = x_prompt.shape
    nb, ds, _ = x_sample.shape
    past = cache_c_k.shape[2]
    a_len = cache_a_k.shape[2]
    assert nbp == 1 and lp % 512 == 0 and (nb * ds) % 512 == 0 and ds == POOL_HALO and past >= POOL_HALO
    ns = nb * ds
    tm = 512
    bf = lambda t: t.astype(BF16)

    x = jnp.concatenate([x_prompt.reshape(lp, d), x_sample.reshape(ns, d)], axis=0)

    proj0 = norm_proj(x, norm_mix[0], bf(w_in_even[0]), tm, 1024)
    k0 = proj0[:, A_WIDTH:2 * A_WIDTH]
    v0 = proj0[:, 2 * A_WIDTH:3 * A_WIDTH]
    u0 = proj0[:, 3 * A_WIDTH:]

    pad = A_PREV_CHUNKS * CHUNK
    bias_p = _rel_bias_tile(a_rel_bias[0], jnp.arange(CHUNK), jnp.arange(A_BAND) - pad)
    a_p = band_prompt(proj0, bias_p, lp)

    k_pos = past - a_len + jnp.arange(a_len + ds)
    q_pos = past + jnp.arange(ds)
    qch, kch = q_pos // CHUNK, k_pos // CHUNK
    ok = ((k_pos[None, :] >= 0) & (kch[None, :] <= qch[:, None])
          & (kch[None, :] >= qch[:, None] - A_PREV_CHUNKS))
    bias_s = jnp.where(ok[None], _rel_bias_tile(a_rel_bias[0], q_pos, k_pos), NEG)
    a_s = band_sample(proj0, cache_a_k[0].reshape(nb, a_len, A_WIDTH),
                      cache_a_v[0].reshape(nb, a_len, A_WIDTH), bias_s, lp, nb, ds)

    u_s = u0[lp:].reshape(nb, ds, B_WIDTH)
    u_hist = jnp.concatenate([state_pool[0], u_s], axis=1)
    u_ext = jnp.concatenate([jnp.zeros((nb, POOL_HALO - B_HIST, B_WIDTH), F32), u_hist], axis=1)
    pw = bf(pool_w[0])
    ps = pool_scale[0].reshape(1, B_WIDTH)
    p_p = pool_prompt(proj0, pw, ps, lp, tm)
    p_s = pool_sample(u_ext, pw, ps, past)

    a = jnp.concatenate([a_p, a_s], axis=0)
    p = jnp.concatenate([p_p, p_s], axis=0)
    wo = bf(w_out_even[0])
    y = mm_res([a, p], [wo[:A_WIDTH], wo[A_WIDTH:]], x, tm, 1024)
    y = ffn(y, norm_ffn[0], bf(ffn_w_gate[0]), bf(ffn_w_up[0]), bf(ffn_w_down[0]), tm, 512)

    n_q = C_HEADS * C_HEAD_DIM
    n_kv = C_KV_HEADS * C_HEAD_DIM
    n_qi = IDX_HEADS * IDX_DIM
    n_main = n_q + 2 * n_kv + n_qi
    w1 = w_in_odd[0]
    w_tail = jnp.pad(w1[:, n_main:], ((0, 0), (0, LANES - (w1.shape[1] - n_main))))
    main = norm_proj(y, norm_mix[1], bf(w1[:, :n_main]), tm, 512)
    tail = norm_proj(y, norm_mix[1], bf(w_tail), tm, LANES)
    pos = jnp.concatenate([jnp.arange(lp), jnp.tile(past + jnp.arange(ds), nb)])
    rot = rope_all(main, tail, _rope_tables(pos, C_HEAD_DIM), _rope_tables(pos, IDX_DIM), tm)

    q1 = rot[:, :n_q]
    k1 = rot[:, n_q:n_q + n_kv]
    v1 = rot[:, n_q + n_kv:n_q + 2 * n_kv]
    qi1 = rot[:, n_q + 2 * n_kv:n_main]
    ki1 = rot[:, n_main:n_main + IDX_DIM]
    wi1 = rot[:, n_main + IDX_DIM:n_main + IDX_DIM + IDX_HEADS]

    dup = lambda t: jnp.concatenate([t, t], axis=-1)
    o_p = dsa(bf(q1[:lp])[None], bf(qi1[:lp])[None], wi1[:lp][None],
              bf(k1[:lp])[None], bf(v1[:lp])[None], dup(bf(ki1[:lp]))[None],
              tq=128, kb=512, n_valid=lp, q_pos0=0, topk=min(TOPK_MAX, lp // 4))

    s_all = past + ds
    kb_s = 3 * LANES
    s_pad = -(-s_all // kb_s) * kb_s
    stack = lambda cache, new, w: jnp.concatenate(
        [bf(cache.reshape(nb, past, w)), bf(new.reshape(nb, ds, w)), jnp.zeros((nb, s_pad - s_all, w), BF16)], axis=1)
    o_s = dsa(bf(q1[lp:]).reshape(nb, ds, n_q), bf(qi1[lp:]).reshape(nb, ds, n_qi), wi1[lp:].reshape(nb, ds, IDX_HEADS),
              stack(cache_c_k[0], k1[lp:], n_kv), stack(cache_c_v[0], v1[lp:], n_kv),
              dup(stack(cache_c_idx[0], ki1[lp:], IDX_DIM)),
              tq=ds, kb=kb_s, n_valid=s_all, q_pos0=past, topk=min(TOPK_MAX, s_all // 4))

    o = jnp.concatenate([o_p.reshape(lp, n_q), o_s.reshape(ns, n_q)], axis=0)
    y = mm_res([o], [bf(w_out_odd[0])], y, tm, 1024)

    wr = jnp.pad(moe_router[0], ((0, 0), (0, LANES - N_EXPERTS)))
    h, gates = router(y, norm_ffn[1], wr, tm)
    y = moe_dense_final(y, h, gates, norm_final, bf(moe_w_gate[0]), bf(moe_w_up[0]), bf(moe_w_down[0]), tm, 512)

    y_prompt = y[:lp].reshape(1, lp, d)
    y_sample = y[lp:].reshape(nb, ds, d)
    keep = min(A_BAND, lp)
    heads = lambda t, n: t.reshape(1, n, -1, A_HEADS, A_HEAD_DIM)
    a_k_prompt = heads(k0[lp - keep:lp], 1)
    a_v_prompt = heads(v0[lp - keep:lp], 1)
    pool_prompt_out = u0[lp - B_HIST:lp].reshape(1, 1, B_HIST, B_WIDTH)
    c_k_prompt = k1[:lp].reshape(1, 1, lp, C_KV_HEADS, C_HEAD_DIM)
    c_v_prompt = v1[:lp].reshape(1, 1, lp, C_KV_HEADS, C_HEAD_DIM)
    c_idx_prompt = ki1[:lp].reshape(1, 1, lp, IDX_DIM)
    shift = lambda cache, new: jnp.concatenate(
        [cache[0], new.reshape(nb, ds, A_HEADS, A_HEAD_DIM)], axis=1)[:, ds:][None]
    a_k_sample = shift(cache_a_k, k0[lp:])
    a_v_sample = shift(cache_a_v, v0[lp:])
    pool_sample_out = u_hist[:, ds:][None]
    c_k_sample = k1[lp:].reshape(1, nb, ds, C_KV_HEADS, C_HEAD_DIM)
    c_v_sample = v1[lp:].reshape(1, nb, ds, C_KV_HEADS, C_HEAD_DIM)
    c_idx_sample = ki1[lp:].reshape(1, nb, ds, IDX_DIM)
    return (y_prompt, y_sample, a_k_prompt, a_v_prompt, pool_prompt_out,
            c_k_prompt, c_v_prompt, c_idx_prompt,
            a_k_sample, a_v_sample, pool_sample_out,
            c_k_sample, c_v_sample, c_idx_sample)
```

```python
import functools
import math

import jax
import jax.numpy as jnp
from jax import lax
from jax.experimental import pallas as pl
from jax.experimental.pallas import tpu as pltpu

F32 = jnp.float32
BF16 = jnp.bfloat16

NORM_EPS = 1e-6
NEG = -1e30
BIG = 1e30

CHUNK = 64
A_HEADS = 16
A_HEAD_DIM = 64
A_WIDTH = A_HEADS * A_HEAD_DIM
A_PREV_CHUNKS = 8
A_BAND = (A_PREV_CHUNKS + 1) * CHUNK
A_REL_CLIP = 128
B_WINDOWS = (2, 4, 8, 16)
B_GROUP = 256
B_WIDTH = B_GROUP * len(B_WINDOWS)
B_HIST = max(B_WINDOWS) - 1
C_HEADS = 16
C_KV_HEADS = 4
C_HEAD_DIM = 128
C_GROUPS = C_HEADS // C_KV_HEADS
IDX_HEADS = 8
IDX_DIM = 64
TOPK_MAX = 256
ROPE_THETA = 500000.0
ROPE_FRAC = 4
N_EXPERTS = 8

LANES = 128
POOL_HALO = 16
VMEM_LIMIT = 56 * 1024 * 1024

_NT = (((1,), (1,)), ((), ()))


def _params(*sem):
    return pltpu.CompilerParams(dimension_semantics=sem, vmem_limit_bytes=VMEM_LIMIT)


def _rms(x, g):
    ms = jnp.mean(x * x, axis=-1, keepdims=True)
    return x * lax.rsqrt(ms + NORM_EPS) * g


def _norm_proj_kernel(x_ref, g_ref, w_ref, o_ref, h_ref):
    @pl.when(pl.program_id(1) == 0)
    def _():
        h_ref[...] = _rms(x_ref[...], g_ref[...]).astype(BF16)

    o_ref[...] = jnp.dot(h_ref[...], w_ref[...], preferred_element_type=F32)


def norm_proj(x, g, w, tm, tn):
    m, d = x.shape
    n = w.shape[1]
    return pl.pallas_call(
        _norm_proj_kernel,
        grid=(m // tm, n // tn),
        in_specs=[pl.BlockSpec((tm, d), lambda i, j: (i, 0)),
                  pl.BlockSpec((1, d), lambda i, j: (0, 0)),
                  pl.BlockSpec((d, tn), lambda i, j: (0, j))],
        out_specs=pl.BlockSpec((tm, tn), lambda i, j: (i, j)),
        out_shape=jax.ShapeDtypeStruct((m, n), F32),
        scratch_shapes=[pltpu.VMEM((tm, d), BF16)],
        compiler_params=_params("parallel", "arbitrary"),
    )(x, g.reshape(1, d), w)


def _band_pairs(q_ref, q_row0, nq, kw_ref, vw_ref, k_row0, nk, bias_ref, key_ok, o_ref):
    lane = lax.broadcasted_iota(jnp.int32, (1, LANES), 1)
    first = lane < A_HEAD_DIM
    for hp in range(A_HEADS // 2):
        cs = slice(hp * LANES, (hp + 1) * LANES)
        qp = q_ref[pl.ds(q_row0, nq), cs]
        kp = kw_ref[pl.ds(k_row0, nk), cs]
        vp = vw_ref[pl.ds(k_row0, nk), cs]
        outs = []
        for half in range(2):
            keep = first if half == 0 else jnp.logical_not(first)
            qm = jnp.where(keep, qp, 0.0).astype(BF16)
            s = lax.dot_general(qm, kp, _NT, preferred_element_type=F32)
            s = s * (A_HEAD_DIM ** -0.5) + bias_ref[hp * 2 + half]
            if key_ok is not None:
                s = jnp.where(key_ok, s, NEG)
            mx = jnp.max(s, axis=-1, keepdims=True)
            e = jnp.exp(s - mx)
            l = jnp.sum(e, axis=-1, keepdims=True)
            outs.append(jnp.dot(e.astype(BF16), vp, preferred_element_type=F32) / l)
        o_ref[pl.ds(q_row0, nq), cs] = jnp.where(first, outs[0], outs[1]).astype(o_ref.dtype)


def _band_prompt_kernel(q_ref, kp_ref, kc_ref, vp_ref, vc_ref, bias_ref, o_ref, kw_ref, vw_ref, *, qb, pad):
    i = pl.program_id(0)
    kw_ref[0:pad, :] = kp_ref[...].astype(BF16)
    kw_ref[pad:pad + qb, :] = kc_ref[...].astype(BF16)
    vw_ref[0:pad, :] = vp_ref[...].astype(BF16)
    vw_ref[pad:pad + qb, :] = vc_ref[...].astype(BF16)
    col = lax.broadcasted_iota(jnp.int32, (1, A_BAND), 1)

    def chunk(cc, carry):
        r0 = pl.multiple_of(cc * CHUNK, CHUNK)
        first_valid = jnp.where(i == 0, pad - cc * CHUNK, 0)
        _band_pairs(q_ref, r0, CHUNK, kw_ref, vw_ref, r0, A_BAND, bias_ref, col >= first_valid, o_ref)
        return carry

    lax.fori_loop(0, qb // CHUNK, chunk, 0)


def band_prompt(proj, bias, lp):
    pad = A_PREV_CHUNKS * CHUNK
    qb = pad
    blk = (qb, A_WIDTH)
    prev = lambda c: (lambda i: (jnp.maximum(i - 1, 0), c))
    cur = lambda c: (lambda i: (i, c))
    return pl.pallas_call(
        functools.partial(_band_prompt_kernel, qb=qb, pad=pad),
        grid=(lp // qb,),
        in_specs=[pl.BlockSpec(blk, cur(0)),
                  pl.BlockSpec(blk, prev(1)), pl.BlockSpec(blk, cur(1)),
                  pl.BlockSpec(blk, prev(2)), pl.BlockSpec(blk, cur(2)),
                  pl.BlockSpec((A_HEADS, CHUNK, A_BAND), lambda i: (0, 0, 0))],
        out_specs=pl.BlockSpec(blk, lambda i: (i, 0)),
        out_shape=jax.ShapeDtypeStruct((lp, A_WIDTH), BF16),
        scratch_shapes=[pltpu.VMEM((pad + qb, A_WIDTH), BF16), pltpu.VMEM((pad + qb, A_WIDTH), BF16)],
        compiler_params=_params("parallel"),
    )(proj, proj, proj, proj, proj, bias)


def _band_sample_kernel(q_ref, kn_ref, vn_ref, ck_ref, cv_ref, bias_ref, o_ref, kw_ref, vw_ref, *, a_len, ds):
    kw_ref[0:a_len, :] = ck_ref[...].astype(BF16)
    kw_ref[a_len:a_len + ds, :] = kn_ref[...].astype(BF16)
    vw_ref[0:a_len, :] = cv_ref[...].astype(BF16)
    vw_ref[a_len:a_len + ds, :] = vn_ref[...].astype(BF16)
    _band_pairs(q_ref, 0, ds, kw_ref, vw_ref, 0, a_len + ds, bias_ref, None, o_ref)


def band_sample(proj, cache_k, cache_v, bias, lp, nb, ds):
    a_len = cache_k.shape[1]
    row = lambda c: (lambda b: (lp // ds + b, c))
    return pl.pallas_call(
        functools.partial(_band_sample_kernel, a_len=a_len, ds=ds),
        grid=(nb,),
        in_specs=[pl.BlockSpec((ds, A_WIDTH), row(0)),
                  pl.BlockSpec((ds, A_WIDTH), row(1)),
                  pl.BlockSpec((ds, A_WIDTH), row(2)),
                  pl.BlockSpec((None, a_len, A_WIDTH), lambda b: (b, 0, 0)),
                  pl.BlockSpec((None, a_len, A_WIDTH), lambda b: (b, 0, 0)),
                  pl.BlockSpec((A_HEADS, ds, a_len + ds), lambda b: (0, 0, 0))],
        out_specs=pl.BlockSpec((ds, A_WIDTH), lambda b: (b, 0)),
        out_shape=jax.ShapeDtypeStruct((nb * ds, A_WIDTH), BF16),
        scratch_shapes=[pltpu.VMEM((a_len + ds, A_WIDTH), BF16), pltpu.VMEM((a_len + ds, A_WIDTH), BF16)],
        compiler_params=_params("parallel"),
    )(proj, proj, proj, cache_k, cache_v, bias)


def _rel_bias_tile(rel_bias, q_pos, k_pos):
    rel = q_pos[:, None] - k_pos[None, :]
    return rel_bias[:, jnp.clip(rel, -A_REL_CLIP, A_REL_CLIP) + A_REL_CLIP].astype(F32)


def _pool_kernel(prev_ref, cur_ref, w_ref, sc_ref, o_ref, ext_ref, *, tm, prompt, pos0):
    i = pl.program_id(0)
    prev = prev_ref[...]
    if prompt:
        prev = jnp.where(i == 0, 0.0, prev)
        pos = i * tm + lax.broadcasted_iota(jnp.int32, (tm, 1), 0)
    else:
        pos = pos0 + lax.broadcasted_iota(jnp.int32, (tm, 1), 0)
    ext_ref[0:POOL_HALO, :] = prev
    ext_ref[POOL_HALO:POOL_HALO + tm, :] = cur_ref[...]
    for g, w in enumerate(B_WINDOWS):
        cs = slice(g * B_GROUP, (g + 1) * B_GROUP)
        tok = ext_ref[POOL_HALO:POOL_HALO + tm, cs]
        tot = tok
        for j in range(1, w):
            tot = tot + ext_ref[POOL_HALO - j:POOL_HALO - j + tm, cs]
        cnt = jnp.minimum(pos + 1, w).astype(F32)
        pooled = (tot / cnt - tok).astype(BF16)
        o = jnp.dot(pooled, w_ref[g], preferred_element_type=F32) * sc_ref[:, cs]
        o_ref[:, cs] = o.astype(o_ref.dtype)


def pool_prompt(proj, pool_w, pool_scale, lp, tm):
    ucol = 3 * A_WIDTH // B_WIDTH
    per = tm // POOL_HALO
    return pl.pallas_call(
        functools.partial(_pool_kernel, tm=tm, prompt=True, pos0=0),
        grid=(lp // tm,),
        in_specs=[pl.BlockSpec((POOL_HALO, B_WIDTH), lambda i: (jnp.maximum(i * per - 1, 0), ucol)),
                  pl.BlockSpec((tm, B_WIDTH), lambda i: (i, ucol)),
                  pl.BlockSpec((len(B_WINDOWS), B_GROUP, B_GROUP), lambda i: (0, 0, 0)),
                  pl.BlockSpec((1, B_WIDTH), lambda i: (0, 0))],
        out_specs=pl.BlockSpec((tm, B_WIDTH), lambda i: (i, 0)),
        out_shape=jax.ShapeDtypeStruct((lp, B_WIDTH), BF16),
        scratch_shapes=[pltpu.VMEM((POOL_HALO + tm, B_WIDTH), F32)],
        compiler_params=_params("parallel"),
    )(proj, proj, pool_w, pool_scale)


def pool_sample(u_ext, pool_w, pool_scale, past):
    nb, tot, _ = u_ext.shape
    ds = tot - POOL_HALO
    return pl.pallas_call(
        functools.partial(_pool_kernel, tm=ds, prompt=False, pos0=past),
        grid=(nb,),
        in_specs=[pl.BlockSpec((None, POOL_HALO, B_WIDTH), lambda b: (b, 0, 0)),
                  pl.BlockSpec((None, ds, B_WIDTH), lambda b: (b, POOL_HALO // ds, 0)),
                  pl.BlockSpec((len(B_WINDOWS), B_GROUP, B_GROUP), lambda b: (0, 0, 0)),
                  pl.BlockSpec((1, B_WIDTH), lambda b: (0, 0))],
        out_specs=pl.BlockSpec((ds, B_WIDTH), lambda b: (b, 0)),
        out_shape=jax.ShapeDtypeStruct((nb * ds, B_WIDTH), BF16),
        scratch_shapes=[pltpu.VMEM((POOL_HALO + ds, B_WIDTH), F32)],
        compiler_params=_params("parallel"),
    )(u_ext, u_ext, pool_w, pool_scale)


def _mm_res_kernel(*refs, n_in):
    xs, ws = refs[:n_in], refs[n_in:2 * n_in]
    res_ref, o_ref = refs[2 * n_in], refs[2 * n_in + 1]
    acc = res_ref[...]
    for x_ref, w_ref in zip(xs, ws):
        acc = acc + jnp.dot(x_ref[...], w_ref[...], preferred_element_type=F32)
    o_ref[...] = acc


def mm_res(xs, ws, res, tm, tn):
    m, n = res.shape
    n_in = len(xs)
    in_specs = ([pl.BlockSpec((tm, x.shape[1]), lambda i, j: (i, 0)) for x in xs]
                + [pl.BlockSpec((w.shape[0], tn), lambda i, j: (0, j)) for w in ws]
                + [pl.BlockSpec((tm, tn), lambda i, j: (i, j))])
    return pl.pallas_call(
        functools.partial(_mm_res_kernel, n_in=n_in),
        grid=(m // tm, n // tn),
        in_specs=in_specs,
        out_specs=pl.BlockSpec((tm, tn), lambda i, j: (i, j)),
        out_shape=jax.ShapeDtypeStruct((m, n), F32),
        compiler_params=_params("parallel", "arbitrary"),
    )(*xs, *ws, res)


def _swiglu_tile(h, wg, wu, wd):
    a = jnp.dot(h, wg, preferred_element_type=F32)
    b = jnp.dot(h, wu, preferred_element_type=F32)
    act = (a * jax.nn.sigmoid(a) * b).astype(BF16)
    return jnp.dot(act, wd, preferred_element_type=F32)


def _ffn_kernel(y_ref, g_ref, wg_ref, wu_ref, wd_ref, o_ref, h_ref, acc_ref):
    f = pl.program_id(1)

    @pl.when(f == 0)
    def _():
        h_ref[...] = _rms(y_ref[...], g_ref[...]).astype(BF16)
        acc_ref[...] = jnp.zeros_like(acc_ref)

    acc_ref[...] += _swiglu_tile(h_ref[...], wg_ref[...], wu_ref[...], wd_ref[...])

    @pl.when(f == pl.num_programs(1) - 1)
    def _():
        o_ref[...] = y_ref[...] + acc_ref[...]


def ffn(y, g, wg, wu, wd, tm, tf):
    m, d = y.shape
    ff = wg.shape[1]
    return pl.pallas_call(
        _ffn_kernel,
        grid=(m // tm, ff // tf),
        in_specs=[pl.BlockSpec((tm, d), lambda i, f: (i, 0)),
                  pl.BlockSpec((1, d), lambda i, f: (0, 0)),
                  pl.BlockSpec((d, tf), lambda i, f: (0, f)),
                  pl.BlockSpec((d, tf), lambda i, f: (0, f)),
                  pl.BlockSpec((tf, d), lambda i, f: (f, 0))],
        out_specs=pl.BlockSpec((tm, d), lambda i, f: (i, 0)),
        out_shape=jax.ShapeDtypeStruct((m, d), F32),
        scratch_shapes=[pltpu.VMEM((tm, d), BF16), pltpu.VMEM((tm, d), F32)],
        compiler_params=_params("parallel", "arbitrary"),
    )(y, g.reshape(1, d), wg, wu, wd)


def _rope_tables(pos, head_dim):
    rot = head_dim // ROPE_FRAC
    half = rot // 2
    inv = jnp.exp(-math.log(ROPE_THETA) * jnp.arange(half, dtype=F32) * (2.0 / rot))
    ang = pos.astype(F32)[:, None] * inv[None, :]
    cos, sin = jnp.cos(ang), jnp.sin(ang)
    m = pos.shape[0]
    one = jnp.ones((m, head_dim - rot), F32)
    zero_r = jnp.zeros((m, head_dim - rot), F32)
    zero_h = jnp.zeros((m, half), F32)
    c = jnp.concatenate([cos, cos, one], axis=1)
    s_dn = jnp.concatenate([-sin, zero_h, zero_r], axis=1)
    s_up = jnp.concatenate([zero_h, sin, zero_r], axis=1)
    rep = LANES // head_dim
    return jnp.stack([jnp.tile(c, (1, rep)), jnp.tile(s_dn, (1, rep)), jnp.tile(s_up, (1, rep))])


def _rot(x, tab_ref, half):
    return (x * tab_ref[0] + pltpu.roll(x, LANES - half, 1) * tab_ref[1]
            + pltpu.roll(x, half, 1) * tab_ref[2])


def _rope_kernel(main_ref, tail_ref, tq_ref, ti_ref, o_ref, *, n_qk, n_v, n_qi, wi_scale):
    half_qk = C_HEAD_DIM // ROPE_FRAC // 2
    half_i = IDX_DIM // ROPE_FRAC // 2
    for c in range(n_qk + n_v + n_qi):
        cs = slice(c * LANES, (c + 1) * LANES)
        x = main_ref[:, cs]
        if c < n_qk:
            x = _rot(x, tq_ref, half_qk)
        elif c >= n_qk + n_v:
            x = _rot(x, ti_ref, half_i)
        o_ref[:, cs] = x
    t = tail_ref[...]
    lane = lax.broadcasted_iota(jnp.int32, (1, LANES), 1)
    c = n_qk + n_v + n_qi
    o_ref[:, c * LANES:(c + 1) * LANES] = jnp.where(lane < IDX_DIM, _rot(t, ti_ref, half_i), t * wi_scale)


def rope_all(main, tail, tab_qk, tab_idx, tm):
    m, nmain = main.shape
    n_qk = (C_HEADS + C_KV_HEADS) * C_HEAD_DIM // LANES
    n_v = C_KV_HEADS * C_HEAD_DIM // LANES
    n_qi = IDX_HEADS * IDX_DIM // LANES
    wi_scale = (IDX_HEADS ** -0.5) * (IDX_DIM ** -0.5)
    return pl.pallas_call(
        functools.partial(_rope_kernel, n_qk=n_qk, n_v=n_v, n_qi=n_qi, wi_scale=wi_scale),
        grid=(m // tm,),
        in_specs=[pl.BlockSpec((tm, nmain), lambda i: (i, 0)),
                  pl.BlockSpec((tm, LANES), lambda i: (i, 0)),
                  pl.BlockSpec((3, tm, LANES), lambda i: (0, i, 0)),
                  pl.BlockSpec((3, tm, LANES), lambda i: (0, i, 0))],
        out_specs=pl.BlockSpec((tm, nmain + LANES), lambda i: (i, 0)),
        out_shape=jax.ShapeDtypeStruct((m, nmain + LANES), F32),
        compiler_params=_params("parallel"),
    )(main, tail, tab_qk, tab_idx)


def _dsa_kernel(q_ref, qi_ref, wi_ref, k_ref, v_ref, ki_ref, o_ref,
                sc_ref, qs_ref, acc_ref, m_ref, l_ref,
                *, tq, kb, n_valid, q_pos0, topk):
    i = pl.program_id(1)
    rows = C_GROUPS * tq
    sub = kb // LANES
    kf = float(topk)

    q_pos = q_pos0 + i * tq + lax.broadcasted_iota(jnp.int32, (tq, 1), 0)
    q_chunk = q_pos // CHUNK
    last_chunk = (q_pos0 + i * tq + tq - 1) // CHUNK
    kv_limit = jnp.minimum(n_valid, (last_chunk + 1) * CHUNK)
    nkb = (kv_limit + kb - 1) // kb

    lane = lax.broadcasted_iota(jnp.int32, (1, LANES), 1)
    first = lane < IDX_DIM
    wi = wi_ref[...]

    def score_block(b, carry):
        for c in range(sub):
            off = pl.multiple_of(b * kb + c * LANES, LANES)
            kib = ki_ref[pl.ds(off, LANES), :]
            acc = jnp.zeros((tq, LANES), F32)
            for hp in range(IDX_HEADS // 2):
                qp = qi_ref[:, hp * LANES:(hp + 1) * LANES]
                for half in range(2):
                    keep = first if half == 0 else jnp.logical_not(first)
                    qm = jnp.where(keep, qp, jnp.zeros_like(qp))
                    d = lax.dot_general(qm, kib, _NT, preferred_element_type=F32)
                    h = hp * 2 + half
                    acc = acc + jnp.maximum(d, 0.0) * wi[:, h:h + 1]
            k_pos = off + lane
            adm = jnp.logical_and(k_pos // CHUNK <= q_chunk, k_pos < n_valid)
            sc_ref[b, :, c * LANES:(c + 1) * LANES] = jnp.where(adm, acc, NEG)
        return carry

    lax.fori_loop(0, nkb, score_block, 0)

    def lane_sum(x):
        return jnp.sum(x, axis=1, keepdims=True)

    def count_ge(t):
        def body(b, acc):
            for c in range(sub):
                blk = sc_ref[b, :, c * LANES:(c + 1) * LANES]
                acc = acc + jnp.where(blk >= t, 1.0, 0.0)
            return acc
        return lane_sum(lax.fori_loop(0, nkb, body, jnp.zeros((tq, LANES), F32)))

    def stats(b, carry):
        mx, mn, cnt = carry
        for c in range(sub):
            blk = sc_ref[b, :, c * LANES:(c + 1) * LANES]
            ok = blk > 0.5 * NEG
            mx = jnp.maximum(mx, blk)
            mn = jnp.minimum(mn, jnp.where(ok, blk, BIG))
            cnt = cnt + jnp.where(ok, 1.0, 0.0)
        return mx, mn, cnt

    mx, mn, cnt = lax.fori_loop(
        0, nkb, stats,
        (jnp.full((tq, LANES), NEG, F32), jnp.full((tq, LANES), BIG, F32), jnp.zeros((tq, LANES), F32)))
    row_max = jnp.max(mx, axis=1, keepdims=True)
    row_min = jnp.min(mn, axis=1, keepdims=True)
    n_adm = lane_sum(cnt)

    done0 = jnp.where(n_adm <= kf, 1.0, 0.0)
    state0 = (row_min, jnp.full((tq, 1), BIG, F32), row_max, jnp.full((tq, 1), 0.5 * NEG, F32), done0)

    def bisect(state, n_steps):
        def cond(c):
            it, st = c
            return jnp.logical_and(it < n_steps, jnp.min(st[4]) < 0.5)

        def body(c):
            it, (lo, hi, mid, thr, done) = c
            cnt = count_ge(mid)
            live = done < 0.5
            hit = jnp.logical_and(live, cnt == kf)
            ge = cnt >= kf
            thr = jnp.where(hit, mid, thr)
            done = jnp.where(hit, 1.0, done)
            lo = jnp.where(ge, mid, lo)
            hi = jnp.where(ge, hi, mid)
            return it + 1, (lo, hi, 0.5 * (lo + hi), thr, done)

        return lax.while_loop(cond, body, (jnp.int32(0), state))[1]

    def snap(state):
        lo, hi, mid, thr, done = state

        def body(b, carry):
            v_lo, v_hi = carry
            for c in range(sub):
                blk = sc_ref[b, :, c * LANES:(c + 1) * LANES]
                v_lo = jnp.minimum(v_lo, jnp.where(blk >= lo, blk, BIG))
                v_hi = jnp.maximum(v_hi, jnp.where(blk < hi, blk, NEG))
            return v_lo, v_hi

        v_lo, v_hi = lax.fori_loop(0, nkb, body,
                                   (jnp.full((tq, LANES), BIG, F32), jnp.full((tq, LANES), NEG, F32)))
        v_lo = jnp.min(v_lo, axis=1, keepdims=True)
        v_hi = jnp.max(v_hi, axis=1, keepdims=True)
        live = done < 0.5
        tie = jnp.logical_and(live, v_lo == v_hi)
        thr = jnp.where(tie, v_lo, thr)
        done = jnp.where(tie, 1.0, done)
        lo = jnp.where(live, v_lo, lo)
        return (lo, hi, 0.5 * (lo + hi), thr, done), jnp.where(tie, 1.0, 0.0)

    state = bisect(state0, 32)

    def refine_cond(c):
        rounds, st, _ = c
        return jnp.logical_and(rounds < 10, jnp.min(st[4]) < 0.5)

    def refine_body(c):
        rounds, st, tie = c
        st, new_tie = snap(st)
        st = bisect(st, 32)
        return rounds + 1, st, jnp.maximum(tie, new_tie)

    _, state, tie = lax.while_loop(refine_cond, refine_body,
                                   (jnp.int32(0), state, jnp.zeros((tq, 1), F32)))
    thr = state[3]
    any_tie = jnp.max(tie) > 0.5

    @pl.when(jnp.logical_not(any_tie))
    def _():
        def body(b, carry):
            for c in range(sub):
                cs = slice(c * LANES, (c + 1) * LANES)
                sc_ref[b, :, cs] = jnp.where(sc_ref[b, :, cs] >= thr, 0.0, NEG)
            return carry
        lax.fori_loop(0, nkb, body, 0)

    @pl.when(any_tie)
    def _():
        def gt_body(b, acc):
            for c in range(sub):
                acc = acc + jnp.where(sc_ref[b, :, c * LANES:(c + 1) * LANES] > thr, 1.0, 0.0)
            return acc
        need = kf - lane_sum(lax.fori_loop(0, nkb, gt_body, jnp.zeros((tq, LANES), F32)))

        def count_eq_upto(j):
            def body(b, acc):
                for c in range(sub):
                    blk = sc_ref[b, :, c * LANES:(c + 1) * LANES]
                    idx = (b * kb + c * LANES + lane).astype(F32)
                    acc = acc + jnp.where(jnp.logical_and(blk == thr, idx <= j), 1.0, 0.0)
                return acc
            return lane_sum(lax.fori_loop(0, nkb, body, jnp.zeros((tq, LANES), F32)))

        def idx_body(_, c):
            lo_j, hi_j = c
            mid_j = jnp.floor(0.5 * (lo_j + hi_j))
            ok = count_eq_upto(mid_j) >= need
            return jnp.where(ok, lo_j, mid_j), jnp.where(ok, mid_j, hi_j)

        n_steps = max(1, math.ceil(math.log2(sc_ref.shape[0] * kb + 1)))
        _, last = lax.fori_loop(
            0, n_steps, idx_body,
            (jnp.full((tq, 1), -1.0, F32), jnp.full((tq, 1), float(sc_ref.shape[0] * kb), F32)))
        last = jnp.where(tie > 0.5, last, BIG)

        def body(b, carry):
            for c in range(sub):
                cs = slice(c * LANES, (c + 1) * LANES)
                blk = sc_ref[b, :, cs]
                idx = (b * kb + c * LANES + lane).astype(F32)
                sel = jnp.logical_or(blk > thr, jnp.logical_and(blk == thr, idx <= last))
                sc_ref[b, :, cs] = jnp.where(sel, 0.0, NEG)
            return carry
        lax.fori_loop(0, nkb, body, 0)

    for g in range(C_KV_HEADS):
        for hh in range(C_GROUPS):
            h = g * C_GROUPS + hh
            qs_ref[g, hh * tq:(hh + 1) * tq, :] = q_ref[:, h * C_HEAD_DIM:(h + 1) * C_HEAD_DIM]
    m_ref[...] = jnp.full(m_ref.shape, NEG, F32)
    l_ref[...] = jnp.zeros(l_ref.shape, F32)
    acc_ref[...] = jnp.zeros(acc_ref.shape, F32)

    def attend(b, carry):
        off = pl.multiple_of(b * kb, kb)
        bias = sc_ref[b]
        bias = jnp.concatenate([bias] * C_GROUPS, axis=0)
        for g in range(C_KV_HEADS):
            cs = slice(g * C_HEAD_DIM, (g + 1) * C_HEAD_DIM)
            kblk = k_ref[pl.ds(off, kb), cs]
            vblk = v_ref[pl.ds(off, kb), cs]
            s = lax.dot_general(qs_ref[g], kblk, _NT, preferred_element_type=F32)
            s = s * (C_HEAD_DIM ** -0.5) + bias
            m_old = m_ref[g]
            m_new = jnp.maximum(m_old, jnp.max(s, axis=1, keepdims=True))
            alpha = jnp.exp(m_old - m_new)
            p = jnp.exp(s - m_new)
            l_ref[g] = alpha * l_ref[g] + jnp.sum(p, axis=1, keepdims=True)
            acc_ref[g] = alpha * acc_ref[g] + jnp.dot(p.astype(BF16), vblk, preferred_element_type=F32)
            m_ref[g] = m_new
        return carry

    lax.fori_loop(0, nkb, attend, 0)

    for g in range(C_KV_HEADS):
        o = acc_ref[g] / l_ref[g]
        for hh in range(C_GROUPS):
            h = g * C_GROUPS + hh
            o_ref[:, h * C_HEAD_DIM:(h + 1) * C_HEAD_DIM] = o[hh * tq:(hh + 1) * tq, :].astype(o_ref.dtype)


def dsa(q, qi, wi, k, v, ki2, *, tq, kb, n_valid, q_pos0, topk):
    nbatch, nq, dq = q.shape
    s_pad = k.shape[1]
    rows = C_GROUPS * tq
    qmap = lambda b, i: (b, i, 0)
    kmap = lambda b, i: (b, 0, 0)
    return pl.pallas_call(
        functools.partial(_dsa_kernel, tq=tq, kb=kb, n_valid=n_valid, q_pos0=q_pos0, topk=topk),
        grid=(nbatch, nq // tq),
        in_specs=[pl.BlockSpec((None, tq, dq), qmap),
                  pl.BlockSpec((None, tq, qi.shape[2]), qmap),
                  pl.BlockSpec((None, tq, wi.shape[2]), qmap),
                  pl.BlockSpec((None, s_pad, k.shape[2]), kmap),
                  pl.BlockSpec((None, s_pad, v.shape[2]), kmap),
                  pl.BlockSpec((None, s_pad, LANES), kmap)],
        out_specs=pl.BlockSpec((None, tq, dq), qmap),
        out_shape=jax.ShapeDtypeStruct((nbatch, nq, dq), BF16),
        scratch_shapes=[pltpu.VMEM((s_pad // kb, tq, kb), F32),
                        pltpu.VMEM((C_KV_HEADS, rows, C_HEAD_DIM), BF16),
                        pltpu.VMEM((C_KV_HEADS, rows, C_HEAD_DIM), F32),
                        pltpu.VMEM((C_KV_HEADS, rows, 1), F32),
                        pltpu.VMEM((C_KV_HEADS, rows, 1), F32)],
        compiler_params=_params("parallel", "arbitrary"),
    )(q, qi, wi, k, v, ki2)


def _dsa_t_kernel(q_ref, qi_ref, wi_ref, k_ref, vt_ref, ki_ref, o_ref,
                  sc_ref, qim_ref, acc_ref, m_ref, l_ref,
                  *, tq, kb, n_valid, q_pos0, topk):
    i = pl.program_id(1)
    kf = float(topk)
    sub8 = kb // 8

    q_pos = q_pos0 + i * tq + lax.broadcasted_iota(jnp.int32, (1, tq), 1)
    q_chunk = q_pos // CHUNK
    last_chunk = (q_pos0 + i * tq + tq - 1) // CHUNK
    kv_limit = jnp.minimum(n_valid, (last_chunk + 1) * CHUNK)
    nkb = (kv_limit + kb - 1) // kb

    def col_sum(x):
        return jnp.sum(x.reshape(sub8, 8, tq), axis=0)

    def col_max(x):
        return jnp.max(x.reshape(sub8, 8, tq), axis=0)

    def col_min(x):
        return jnp.min(x.reshape(sub8, 8, tq), axis=0)

    lane = lax.broadcasted_iota(jnp.int32, (1, LANES), 1)
    first = lane < IDX_DIM
    for hp in range(IDX_HEADS // 2):
        qp = qi_ref[:, hp * LANES:(hp + 1) * LANES]
        qim_ref[2 * hp] = jnp.where(first, qp, jnp.zeros_like(qp))
        qim_ref[2 * hp + 1] = jnp.where(first, jnp.zeros_like(qp), qp)
    key_iota = lax.broadcasted_iota(jnp.int32, (kb, 1), 0)

    def score_block(b, carry):
        off = pl.multiple_of(b * kb, kb)
        kib = ki_ref[pl.ds(off, kb), :]
        acc = jnp.zeros((kb, tq), F32)
        for h in range(IDX_HEADS):
            d = lax.dot_general(kib, qim_ref[h], _NT, preferred_element_type=F32)
            acc = acc + jnp.maximum(d, 0.0) * wi_ref[h:h + 1, :]
        k_pos = off + key_iota
        adm = jnp.logical_and(k_pos // CHUNK <= q_chunk, k_pos < n_valid)
        sc_ref[pl.ds(off, kb), :] = jnp.where(adm, acc, NEG)
        return carry

    lax.fori_loop(0, nkb, score_block, 0)

    def blocks(body, init):
        def step(b, carry):
            off = pl.multiple_of(b * kb, kb)
            return body(off, sc_ref[pl.ds(off, kb), :], carry)
        return lax.fori_loop(0, nkb, step, init)

    def total(x):
        return jnp.sum(x, axis=0, keepdims=True)

    def count_ge(t):
        return total(blocks(lambda off, blk, acc: acc + col_sum(jnp.where(blk >= t, 1.0, 0.0)),
                            jnp.zeros((8, tq), F32)))

    def stats(off, blk, carry):
        mx, mn, cnt = carry
        ok = blk > 0.5 * NEG
        return (jnp.maximum(mx, col_max(blk)), jnp.minimum(mn, col_min(jnp.where(ok, blk, BIG))),
                cnt + col_sum(jnp.where(ok, 1.0, 0.0)))

    mx, mn, cnt = blocks(stats, (jnp.full((8, tq), NEG, F32), jnp.full((8, tq), BIG, F32),
                                 jnp.zeros((8, tq), F32)))
    row_max = jnp.max(mx, axis=0, keepdims=True)
    row_min = jnp.min(mn, axis=0, keepdims=True)
    n_adm = total(cnt)

    done0 = jnp.where(n_adm <= kf, 1.0, 0.0)
    state0 = (row_min, jnp.full((1, tq), BIG, F32), row_max, jnp.full((1, tq), 0.5 * NEG, F32), done0)

    def bisect(state, n_steps):
        def cond(c):
            it, st = c
            return jnp.logical_and(it < n_steps, jnp.min(st[4]) < 0.5)

        def body(c):
            it, (lo, hi, mid, thr, done) = c
            cnt = count_ge(mid)
            hit = jnp.logical_and(done < 0.5, cnt == kf)
            ge = cnt >= kf
            thr = jnp.where(hit, mid, thr)
            done = jnp.where(hit, 1.0, done)
            lo = jnp.where(ge, mid, lo)
            hi = jnp.where(ge, hi, mid)
            return it + 1, (lo, hi, 0.5 * (lo + hi), thr, done)

        return lax.while_loop(cond, body, (jnp.int32(0), state))[1]

    def snap(state):
        lo, hi, mid, thr, done = state

        def body(off, blk, carry):
            v_lo, v_hi = carry
            return (jnp.minimum(v_lo, col_min(jnp.where(blk >= lo, blk, BIG))),
                    jnp.maximum(v_hi, col_max(jnp.where(blk < hi, blk, NEG))))

        v_lo, v_hi = blocks(body, (jnp.full((8, tq), BIG, F32), jnp.full((8, tq), NEG, F32)))
        v_lo = jnp.min(v_lo, axis=0, keepdims=True)
        v_hi = jnp.max(v_hi, axis=0, keepdims=True)
        live = done < 0.5
        tie = jnp.logical_and(live, v_lo == v_hi)
        thr = jnp.where(tie, v_lo, thr)
        done = jnp.where(tie, 1.0, done)
        lo = jnp.where(live, v_lo, lo)
        return (lo, hi, 0.5 * (lo + hi), thr, done), jnp.where(tie, 1.0, 0.0)

    state = bisect(state0, 32)

    def refine_cond(c):
        rounds, st, _ = c
        return jnp.logical_and(rounds < 10, jnp.min(st[4]) < 0.5)

    def refine_body(c):
        rounds, st, tie = c
        st, new_tie = snap(st)
        st = bisect(st, 32)
        return rounds + 1, st, jnp.maximum(tie, new_tie)

    _, state, tie = lax.while_loop(refine_cond, refine_body,
                                   (jnp.int32(0), state, jnp.zeros((1, tq), F32)))
    thr = state[3]
    any_tie = jnp.max(tie) > 0.5

    @pl.when(jnp.logical_not(any_tie))
    def _():
        def body(off, blk, carry):
            sc_ref[pl.ds(off, kb), :] = jnp.where(blk >= thr, 0.0, NEG)
            return carry
        blocks(body, 0)

    @pl.when(any_tie)
    def _():
        need = kf - total(blocks(lambda off, blk, acc: acc + col_sum(jnp.where(blk > thr, 1.0, 0.0)),
                                 jnp.zeros((8, tq), F32)))

        def count_eq_upto(j):
            def body(off, blk, acc):
                idx = (off + key_iota).astype(F32)
                return acc + col_sum(jnp.where(jnp.logical_and(blk == thr, idx <= j), 1.0, 0.0))
            return total(blocks(body, jnp.zeros((8, tq), F32)))

        def idx_body(_, c):
            lo_j, hi_j = c
            mid_j = jnp.floor(0.5 * (lo_j + hi_j))
            ok = count_eq_upto(mid_j) >= need
            return jnp.where(ok, lo_j, mid_j), jnp.where(ok, mid_j, hi_j)

        n_keys = sc_ref.shape[0]
        n_steps = max(1, math.ceil(math.log2(n_keys + 1)))
        _, last = lax.fori_loop(0, n_steps, idx_body,
                                (jnp.full((1, tq), -1.0, F32), jnp.full((1, tq), float(n_keys), F32)))
        last = jnp.where(tie > 0.5, last, BIG)

        def body(off, blk, carry):
            idx = (off + key_iota).astype(F32)
            sel = jnp.logical_or(blk > thr, jnp.logical_and(blk == thr, idx <= last))
            sc_ref[pl.ds(off, kb), :] = jnp.where(sel, 0.0, NEG)
            return carry
        blocks(body, 0)

    m_ref[...] = jnp.full(m_ref.shape, NEG, F32)
    l_ref[...] = jnp.zeros(l_ref.shape, F32)
    acc_ref[...] = jnp.zeros(acc_ref.shape, F32)

    def attend(b, carry):
        off = pl.multiple_of(b * kb, kb)
        for h in range(C_HEADS):
            g = h // C_GROUPS
            gs = slice(g * C_HEAD_DIM, (g + 1) * C_HEAD_DIM)
            s = lax.dot_general(k_ref[pl.ds(off, kb), gs], q_ref[:, h * C_HEAD_DIM:(h + 1) * C_HEAD_DIM],
                                _NT, preferred_element_type=F32)
            s = s * (C_HEAD_DIM ** -0.5) + sc_ref[pl.ds(off, kb), :]
            m_old = m_ref[h]
            m_new = jnp.maximum(m_old, jnp.max(col_max(s), axis=0, keepdims=True))
            alpha = jnp.exp(m_old - m_new)
            p = jnp.exp(s - m_new)
            l_ref[h] = alpha * l_ref[h] + total(col_sum(p))
            acc_ref[h] = alpha * acc_ref[h] + jnp.dot(vt_ref[b, gs, :], p.astype(BF16),
                                                      preferred_element_type=F32)
            m_ref[h] = m_new
        return carry

    lax.fori_loop(0, nkb, attend, 0)

    for h in range(C_HEADS):
        o = acc_ref[h] / l_ref[h]
        o_ref[:, h * C_HEAD_DIM:(h + 1) * C_HEAD_DIM] = o.T.astype(o_ref.dtype)


def dsa_t(q, qi, wi_t, k, vt, ki2, *, tq, kb, n_valid, q_pos0, topk):
    nq, dq = q.shape
    s_pad = k.shape[0]
    return pl.pallas_call(
        functools.partial(_dsa_t_kernel, tq=tq, kb=kb, n_valid=n_valid, q_pos0=q_pos0, topk=topk),
        grid=(1, nq // tq),
        in_specs=[pl.BlockSpec((tq, dq), lambda b, i: (i, 0)),
                  pl.BlockSpec((tq, qi.shape[1]), lambda b, i: (i, 0)),
                  pl.BlockSpec((IDX_HEADS, tq), lambda b, i: (0, i)),
                  pl.BlockSpec(k.shape, lambda b, i: (0, 0)),
                  pl.BlockSpec(vt.shape, lambda b, i: (0, 0, 0)),
                  pl.BlockSpec(ki2.shape, lambda b, i: (0, 0))],
        out_specs=pl.BlockSpec((tq, dq), lambda b, i: (i, 0)),
        out_shape=jax.ShapeDtypeStruct((nq, dq), BF16),
        scratch_shapes=[pltpu.VMEM((s_pad, tq), F32),
                        pltpu.VMEM((IDX_HEADS, tq, LANES), BF16),
                        pltpu.VMEM((C_HEADS, C_HEAD_DIM, tq), F32),
                        pltpu.VMEM((C_HEADS, 1, tq), F32),
                        pltpu.VMEM((C_HEADS, 1, tq), F32)],
        compiler_params=_params("arbitrary", "arbitrary"),
    )(q, qi, wi_t, k, vt, ki2)


def _router_kernel(y_ref, g_ref, wr_ref, h_ref, gate_ref):
    h = _rms(y_ref[...], g_ref[...])
    h_ref[...] = h.astype(BF16)
    logits = jnp.dot(h, wr_ref[...], preferred_element_type=F32, precision=lax.Precision.HIGHEST)
    lane = lax.broadcasted_iota(jnp.int32, logits.shape, 1)
    lg = jnp.where(lane < N_EXPERTS, logits, NEG)
    m1 = jnp.max(lg, axis=1, keepdims=True)
    i1 = jnp.min(jnp.where(lg == m1, lane, LANES), axis=1, keepdims=True)
    lg2 = jnp.where(lane == i1, NEG, lg)
    m2 = jnp.max(lg2, axis=1, keepdims=True)
    i2 = jnp.min(jnp.where(lg2 == m2, lane, LANES), axis=1, keepdims=True)
    e = jnp.exp(m2 - m1)
    g1 = 1.0 / (1.0 + e)
    g2 = e / (1.0 + e)
    meta = jnp.where(lane == 0, i1.astype(F32), jnp.where(lane == 1, i2.astype(F32),
                     jnp.where(lane == 2, g1, jnp.where(lane == 3, g2, 0.0))))
    gate_ref[...] = meta


def router(y, g, w_router_pad, tm):
    m, d = y.shape
    return pl.pallas_call(
        _router_kernel,
        grid=(m // tm,),
        in_specs=[pl.BlockSpec((tm, d), lambda i: (i, 0)),
                  pl.BlockSpec((1, d), lambda i: (0, 0)),
                  pl.BlockSpec((d, LANES), lambda i: (0, 0))],
        out_specs=[pl.BlockSpec((tm, d), lambda i: (i, 0)),
                   pl.BlockSpec((tm, LANES), lambda i: (i, 0))],
        out_shape=[jax.ShapeDtypeStruct((m, d), BF16), jax.ShapeDtypeStruct((m, LANES), F32)],
        compiler_params=_params("parallel"),
    )(y, g.reshape(1, d), w_router_pad)


def route_plan(meta, sup):
    m = meta.shape[0]
    n_tiles = (2 * m) // sup + N_EXPERTS
    e_all = jnp.concatenate([meta[:, 0], meta[:, 1]]).astype(jnp.int32)
    g_all = jnp.concatenate([meta[:, 2], meta[:, 3]])
    onehot = (e_all[:, None] == jnp.arange(N_EXPERTS)[None, :]).astype(jnp.int32)
    rank = jnp.sum((jnp.cumsum(onehot, axis=0) - onehot) * onehot, axis=1)
    counts = jnp.sum(onehot, axis=0)
    n_super = (counts + sup - 1) // sup
    super_end = jnp.cumsum(n_super)
    super_start = super_end - n_super
    pos = (super_start * sup)[e_all] + rank
    tiles = jnp.arange(n_tiles)
    used = super_end[-1]
    t_eff = jnp.minimum(tiles, used - 1)
    tile_expert = jnp.minimum(jnp.searchsorted(super_end, t_eff, side="right"), N_EXPERTS - 1).astype(jnp.int32)
    rows = jnp.clip(counts[tile_expert] - (t_eff - super_start[tile_expert]) * sup, 0, sup)
    tile_rows = jnp.where(tiles < used, rows, 0).astype(jnp.int32)
    gate_rows = jnp.zeros((n_tiles * sup,), F32).at[pos].set(g_all)
    gate_rows = jnp.broadcast_to(gate_rows[:, None], (n_tiles * sup, LANES))
    return pos.astype(jnp.int32), gate_rows, tile_expert, tile_rows


def _gather_rows_kernel(pos_ref, h_ref, init_ref, o_ref, sem, *, n_tok, batch):
    del init_ref
    base = pl.program_id(0) * batch

    def issue(j, carry):
        r = base + j
        tok = jnp.where(r >= n_tok, r - n_tok, r)
        pltpu.make_async_copy(h_ref.at[tok], o_ref.at[pos_ref[r]], sem).start()
        return carry

    lax.fori_loop(0, batch, issue, 0)
    pltpu.make_async_copy(o_ref.at[pl.ds(0, batch)], o_ref.at[pl.ds(0, batch)], sem).wait()


def gather_rows(h3, pos, n_rows, batch):
    m = h3.shape[0]
    init = jnp.zeros((n_rows,) + h3.shape[1:], h3.dtype)
    return pl.pallas_call(
        functools.partial(_gather_rows_kernel, n_tok=m, batch=batch),
        grid_spec=pltpu.PrefetchScalarGridSpec(
            num_scalar_prefetch=1,
            grid=(pos.shape[0] // batch,),
            in_specs=[pl.BlockSpec(memory_space=pl.ANY), pl.BlockSpec(memory_space=pl.ANY)],
            out_specs=pl.BlockSpec(memory_space=pl.ANY),
            scratch_shapes=[pltpu.SemaphoreType.DMA(())]),
        out_shape=jax.ShapeDtypeStruct(init.shape, init.dtype),
        input_output_aliases={2: 0},
        compiler_params=_params("arbitrary"),
    )(pos, h3, init)


def _grouped_ffn_kernel(te_ref, tr_ref, x_ref, gate_ref, wg_ref, wu_ref, wd_ref, o_ref, *, sup, sub):
    t = pl.program_id(0)
    f = pl.program_id(1)
    last_f = pl.num_programs(1) - 1
    rows = tr_ref[t]
    for s in range(sup // sub):
        sl = slice(s * sub, (s + 1) * sub)

        @pl.when(jnp.logical_and(s * sub >= rows, f == 0))
        def _():
            o_ref[sl, :] = jnp.zeros((sub, o_ref.shape[1]), F32)

        @pl.when(s * sub < rows)
        def _():
            part = _swiglu_tile(x_ref[sl, :], wg_ref[...], wu_ref[...], wd_ref[...])

            @pl.when(f == 0)
            def _():
                o_ref[sl, :] = part

            @pl.when(jnp.logical_and(f > 0, f < last_f))
            def _():
                o_ref[sl, :] += part

            @pl.when(jnp.logical_and(f > 0, f == last_f))
            def _():
                o_ref[sl, :] = (o_ref[sl, :] + part) * gate_ref[sl, 0:1]


def grouped_ffn(xs, gate_rows, tile_expert, tile_rows, wg, wu, wd, sup, sub, tf):
    n_rows, d = xs.shape
    ff = wg.shape[2]
    n_f = ff // tf
    assert n_f > 1
    fidx = lambda t, f, te, tr: jnp.where(tr[t] > 0, f, n_f - 1)
    return pl.pallas_call(
        functools.partial(_grouped_ffn_kernel, sup=sup, sub=sub),
        grid_spec=pltpu.PrefetchScalarGridSpec(
            num_scalar_prefetch=2,
            grid=(n_rows // sup, n_f),
            in_specs=[pl.BlockSpec((sup, d), lambda t, f, te, tr: (t, 0)),
                      pl.BlockSpec((sup, LANES), lambda t, f, te, tr: (t, 0)),
                      pl.BlockSpec((None, d, tf), lambda t, f, te, tr: (te[t], 0, fidx(t, f, te, tr))),
                      pl.BlockSpec((None, d, tf), lambda t, f, te, tr: (te[t], 0, fidx(t, f, te, tr))),
                      pl.BlockSpec((None, tf, d), lambda t, f, te, tr: (te[t], fidx(t, f, te, tr), 0))],
            out_specs=pl.BlockSpec((sup, d), lambda t, f, te, tr: (t, 0))),
        out_shape=jax.ShapeDtypeStruct((n_rows, d), F32),
        compiler_params=_params("arbitrary", "arbitrary"),
    )(tile_expert, tile_rows, xs, gate_rows, wg, wu, wd)


def _combine_kernel(pos_ref, y_ref, g_ref, ys_ref, o_ref, buf_ref, sem, *, n_tok, tmc):
    base = pl.program_id(0) * tmc

    def issue(j, carry):
        t = base + j
        pltpu.make_async_copy(ys_ref.at[pos_ref[t]], buf_ref.at[0, j], sem).start()
        pltpu.make_async_copy(ys_ref.at[pos_ref[n_tok + t]], buf_ref.at[1, j], sem).start()
        return carry

    lax.fori_loop(0, tmc, issue, 0)
    pltpu.make_async_copy(buf_ref.at[0], buf_ref.at[0], sem).wait()
    pltpu.make_async_copy(buf_ref.at[1], buf_ref.at[1], sem).wait()
    x = y_ref[...] + buf_ref[0] + buf_ref[1]
    ms = jnp.sum(jnp.sum(x * x, axis=2, keepdims=True), axis=1, keepdims=True) / (x.shape[1] * x.shape[2])
    o_ref[...] = x * lax.rsqrt(ms + NORM_EPS) * g_ref[...]


def combine_final(y3, g_final3, ys3, pos, tmc):
    m = y3.shape[0]
    blk = (tmc,) + y3.shape[1:]
    return pl.pallas_call(
        functools.partial(_combine_kernel, n_tok=m, tmc=tmc),
        grid_spec=pltpu.PrefetchScalarGridSpec(
            num_scalar_prefetch=1,
            grid=(m // tmc,),
            in_specs=[pl.BlockSpec(blk, lambda i, pos: (i, 0, 0)),
                      pl.BlockSpec((1,) + y3.shape[1:], lambda i, pos: (0, 0, 0)),
                      pl.BlockSpec(memory_space=pl.ANY)],
            out_specs=pl.BlockSpec(blk, lambda i, pos: (i, 0, 0)),
            scratch_shapes=[pltpu.VMEM((2,) + blk, F32), pltpu.SemaphoreType.DMA(())]),
        out_shape=jax.ShapeDtypeStruct(y3.shape, F32),
        compiler_params=_params("arbitrary"),
    )(pos, y3, g_final3, ys3)


def kernel(x_prompt, x_sample, cache_a_k, cache_a_v, state_pool, cache_c_k, cache_c_v, cache_c_idx,
           norm_mix, norm_ffn, norm_final, w_in_even, w_out_even, a_rel_bias, pool_w, pool_scale,
           ffn_w_gate, ffn_w_up, ffn_w_down, w_in_odd, w_out_odd,
           moe_router, moe_w_gate, moe_w_up, moe_w_down):
    nbp, lp, d = x_prompt.shape
    nb, ds, _ = x_sample.shape
    past = cache_c_k.shape[2]
    a_len = cache_a_k.shape[2]
    assert nbp == 1 and lp % 512 == 0 and (nb * ds) % 512 == 0 and ds == POOL_HALO and past >= POOL_HALO
    ns = nb * ds
    m = lp + ns
    tm = 512
    bf = lambda t: t.astype(BF16)

    x = jnp.concatenate([x_prompt.reshape(lp, d), x_sample.reshape(ns, d)], axis=0)

    proj0 = norm_proj(x, norm_mix[0], bf(w_in_even[0]), tm, 1024)
    k0 = proj0[:, A_WIDTH:2 * A_WIDTH]
    v0 = proj0[:, 2 * A_WIDTH:3 * A_WIDTH]
    u0 = proj0[:, 3 * A_WIDTH:]

    pad = A_PREV_CHUNKS * CHUNK
    bias_p = _rel_bias_tile(a_rel_bias[0], jnp.arange(CHUNK), jnp.arange(A_BAND) - pad)
    a_p = band_prompt(proj0, bias_p, lp)

    k_pos = past - a_len + jnp.arange(a_len + ds)
    q_pos = past + jnp.arange(ds)
    qch, kch = q_pos // CHUNK, k_pos // CHUNK
    ok = ((k_pos[None, :] >= 0) & (kch[None, :] <= qch[:, None])
          & (kch[None, :] >= qch[:, None] - A_PREV_CHUNKS))
    bias_s = jnp.where(ok[None], _rel_bias_tile(a_rel_bias[0], q_pos, k_pos), NEG)
    a_s = band_sample(proj0, cache_a_k[0].reshape(nb, a_len, A_WIDTH),
                      cache_a_v[0].reshape(nb, a_len, A_WIDTH), bias_s, lp, nb, ds)

    u_s = u0[lp:].reshape(nb, ds, B_WIDTH)
    u_hist = jnp.concatenate([state_pool[0], u_s], axis=1)
    u_ext = jnp.concatenate([jnp.zeros((nb, POOL_HALO - B_HIST, B_WIDTH), F32), u_hist], axis=1)
    pw = bf(pool_w[0])
    ps = pool_scale[0].reshape(1, B_WIDTH)
    p_p = pool_prompt(proj0, pw, ps, lp, tm)
    p_s = pool_sample(u_ext, pw, ps, past)

    a = jnp.concatenate([a_p, a_s], axis=0)
    p = jnp.concatenate([p_p, p_s], axis=0)
    wo = bf(w_out_even[0])
    y = mm_res([a, p], [wo[:A_WIDTH], wo[A_WIDTH:]], x, tm, 1024)
    y = ffn(y, norm_ffn[0], bf(ffn_w_gate[0]), bf(ffn_w_up[0]), bf(ffn_w_down[0]), tm, 512)

    n_q = C_HEADS * C_HEAD_DIM
    n_kv = C_KV_HEADS * C_HEAD_DIM
    n_qi = IDX_HEADS * IDX_DIM
    n_main = n_q + 2 * n_kv + n_qi
    w1 = w_in_odd[0]
    w_tail = jnp.pad(w1[:, n_main:], ((0, 0), (0, LANES - (w1.shape[1] - n_main))))
    main = norm_proj(y, norm_mix[1], bf(w1[:, :n_main]), tm, 512)
    tail = norm_proj(y, norm_mix[1], bf(w_tail), tm, LANES)
    pos = jnp.concatenate([jnp.arange(lp), jnp.tile(past + jnp.arange(ds), nb)])
    rot = rope_all(main, tail, _rope_tables(pos, C_HEAD_DIM), _rope_tables(pos, IDX_DIM), tm)

    q1 = rot[:, :n_q]
    k1 = rot[:, n_q:n_q + n_kv]
    v1 = rot[:, n_q + n_kv:n_q + 2 * n_kv]
    qi1 = rot[:, n_q + 2 * n_kv:n_main]
    ki1 = rot[:, n_main:n_main + IDX_DIM]
    wi1 = rot[:, n_main + IDX_DIM:n_main + IDX_DIM + IDX_HEADS]

    dup = lambda t: jnp.concatenate([t, t], axis=-1)
    kb_p = 256
    vt = bf(v1[:lp]).reshape(lp // kb_p, kb_p, n_kv).transpose(0, 2, 1)
    o_p = dsa_t(bf(q1[:lp]), bf(qi1[:lp]), wi1[:lp].T, bf(k1[:lp]), vt, dup(bf(ki1[:lp])),
                tq=128, kb=kb_p, n_valid=lp, q_pos0=0, topk=min(TOPK_MAX, lp // 4))

    s_all = past + ds
    kb_s = 3 * LANES
    s_pad = -(-s_all // kb_s) * kb_s
    stack = lambda cache, new, w: jnp.concatenate(
        [bf(cache.reshape(nb, past, w)), bf(new.reshape(nb, ds, w)), jnp.zeros((nb, s_pad - s_all, w), BF16)], axis=1)
    o_s = dsa(bf(q1[lp:]).reshape(nb, ds, n_q), bf(qi1[lp:]).reshape(nb, ds, n_qi), wi1[lp:].reshape(nb, ds, IDX_HEADS),
              stack(cache_c_k[0], k1[lp:], n_kv), stack(cache_c_v[0], v1[lp:], n_kv),
              dup(stack(cache_c_idx[0], ki1[lp:], IDX_DIM)),
              tq=ds, kb=kb_s, n_valid=s_all, q_pos0=past, topk=min(TOPK_MAX, s_all // 4))

    o = jnp.concatenate([o_p.reshape(lp, n_q), o_s.reshape(ns, n_q)], axis=0)
    y = mm_res([o], [bf(w_out_odd[0])], y, tm, 1024)

    wr = jnp.pad(moe_router[0], ((0, 0), (0, LANES - N_EXPERTS)))
    h, meta = router(y, norm_ffn[1], wr, tm)
    sup = 1024
    pos_rows, gate_rows, tile_expert, tile_rows = route_plan(meta, sup)
    n_rows = gate_rows.shape[0]
    slab = (d // LANES, LANES)
    xs = gather_rows(h.reshape((m,) + slab), pos_rows, n_rows, 512)
    ys = grouped_ffn(xs.reshape(n_rows, d), gate_rows, tile_expert, tile_rows,
                     bf(moe_w_gate[0]), bf(moe_w_up[0]), bf(moe_w_down[0]), sup, 512, 512)
    y = combine_final(y.reshape((m,) + slab), norm_final.reshape((1,) + slab),
                      ys.reshape((n_rows,) + slab), pos_rows, 256).reshape(m, d)

    y_prompt = y[:lp].reshape(1, lp, d)
    y_sample = y[lp:].reshape(nb, ds, d)
    keep = min(A_BAND, lp)
    heads = lambda t, n: t.reshape(1, n, -1, A_HEADS, A_HEAD_DIM)
    a_k_prompt = heads(k0[lp - keep:lp], 1)
    a_v_prompt = heads(v0[lp - keep:lp], 1)
    pool_prompt_out = u0[lp - B_HIST:lp].reshape(1, 1, B_HIST, B_WIDTH)
    c_k_prompt = k1[:lp].reshape(1, 1, lp, C_KV_HEADS, C_HEAD_DIM)
    c_v_prompt = v1[:lp].reshape(1, 1, lp, C_KV_HEADS, C_HEAD_DIM)
    c_idx_prompt = ki1[:lp].reshape(1, 1, lp, IDX_DIM)
    shift = lambda cache, new: jnp.concatenate(
        [cache[0], new.reshape(nb, ds, A_HEADS, A_HEAD_DIM)], axis=1)[:, ds:][None]
    a_k_sample = shift(cache_a_k, k0[lp:])
    a_v_sample = shift(cache_a_v, v0[lp:])
    pool_sample_out = u_hist[:, ds:][None]
    c_k_sample = k1[lp:].reshape(1, nb, ds, C_KV_HEADS, C_HEAD_DIM)
    c_v_sample = v1[lp:].reshape(1, nb, ds, C_KV_HEADS, C_HEAD_DIM)
    c_idx_sample = ki1[lp:].reshape(1, nb, ds, IDX_DIM)
    return (y_prompt, y_sample, a_k_prompt, a_v_prompt, pool_prompt_out,
            c_k_prompt, c_v_prompt, c_idx_prompt,
            a_k_sample, a_v_sample, pool_sample_out,
            c_k_sample, c_v_sample, c_idx_sample)
```

```python
import functools
import math

import jax
import jax.numpy as jnp
from jax import lax
from jax.experimental import pallas as pl
from jax.experimental.pallas import tpu as pltpu

F32 = jnp.float32
BF16 = jnp.bfloat16

NORM_EPS = 1e-6
NEG = -1e30
BIG = 1e30

CHUNK = 64
A_HEADS = 16
A_HEAD_DIM = 64
A_WIDTH = A_HEADS * A_HEAD_DIM
A_PREV_CHUNKS = 8
A_BAND = (A_PREV_CHUNKS + 1) * CHUNK
A_REL_CLIP = 128
B_WINDOWS = (2, 4, 8, 16)
B_GROUP = 256
B_WIDTH = B_GROUP * len(B_WINDOWS)
B_HIST = max(B_WINDOWS) - 1
C_HEADS = 16
C_KV_HEADS = 4
C_HEAD_DIM = 128
C_GROUPS = C_HEADS // C_KV_HEADS
IDX_HEADS = 8
IDX_DIM = 64
TOPK_MAX = 256
ROPE_THETA = 500000.0
ROPE_FRAC = 4
N_EXPERTS = 8

LANES = 128
POOL_HALO = 16
VMEM_LIMIT = 56 * 1024 * 1024

_NT = (((1,), (1,)), ((), ()))


def _params(*sem):
    return pltpu.CompilerParams(dimension_semantics=sem, vmem_limit_bytes=VMEM_LIMIT)


def _rms(x, g):
    ms = jnp.mean(x * x, axis=-1, keepdims=True)
    return x * lax.rsqrt(ms + NORM_EPS) * g


def _norm_proj_kernel(x_ref, g_ref, w_ref, o_ref, h_ref):
    @pl.when(pl.program_id(1) == 0)
    def _():
        h_ref[...] = _rms(x_ref[...], g_ref[...]).astype(BF16)

    o_ref[...] = jnp.dot(h_ref[...], w_ref[...], preferred_element_type=F32)


def norm_proj(x, g, w, tm, tn):
    m, d = x.shape
    n = w.shape[1]
    return pl.pallas_call(
        _norm_proj_kernel,
        grid=(m // tm, n // tn),
        in_specs=[pl.BlockSpec((tm, d), lambda i, j: (i, 0)),
                  pl.BlockSpec((1, d), lambda i, j: (0, 0)),
                  pl.BlockSpec((d, tn), lambda i, j: (0, j))],
        out_specs=pl.BlockSpec((tm, tn), lambda i, j: (i, j)),
        out_shape=jax.ShapeDtypeStruct((m, n), F32),
        scratch_shapes=[pltpu.VMEM((tm, d), BF16)],
        compiler_params=_params("parallel", "arbitrary"),
    )(x, g.reshape(1, d), w)


def _band_pairs(q_ref, q_row0, nq, kw_ref, vw_ref, k_row0, nk, bias_ref, key_ok, o_ref):
    lane = lax.broadcasted_iota(jnp.int32, (1, LANES), 1)
    first = lane < A_HEAD_DIM
    for hp in range(A_HEADS // 2):
        cs = slice(hp * LANES, (hp + 1) * LANES)
        qp = q_ref[pl.ds(q_row0, nq), cs]
        kp = kw_ref[pl.ds(k_row0, nk), cs]
        vp = vw_ref[pl.ds(k_row0, nk), cs]
        qm = jnp.concatenate([jnp.where(first, qp, 0.0), jnp.where(first, 0.0, qp)], axis=0).astype(BF16)
        s = lax.dot_general(qm, kp, _NT, preferred_element_type=F32)
        s = s * (A_HEAD_DIM ** -0.5) + bias_ref[hp]
        if key_ok is not None:
            s = jnp.where(key_ok, s, NEG)
        mx = jnp.max(s, axis=-1, keepdims=True)
        e = jnp.exp(s - mx)
        l = jnp.sum(e, axis=-1, keepdims=True)
        o = jnp.dot(e.astype(BF16), vp, preferred_element_type=F32) / l
        o_ref[pl.ds(q_row0, nq), cs] = jnp.where(first, o[:nq], o[nq:]).astype(o_ref.dtype)


def _band_prompt_kernel(q_ref, kp_ref, kc_ref, vp_ref, vc_ref, bias_ref, o_ref, kw_ref, vw_ref, *, qb, pad):
    i = pl.program_id(0)
    kw_ref[0:pad, :] = kp_ref[...].astype(BF16)
    kw_ref[pad:pad + qb, :] = kc_ref[...].astype(BF16)
    vw_ref[0:pad, :] = vp_ref[...].astype(BF16)
    vw_ref[pad:pad + qb, :] = vc_ref[...].astype(BF16)
    col = lax.broadcasted_iota(jnp.int32, (1, A_BAND), 1)

    def chunk(cc, carry):
        r0 = pl.multiple_of(cc * CHUNK, CHUNK)
        first_valid = jnp.where(i == 0, pad - cc * CHUNK, 0)
        _band_pairs(q_ref, r0, CHUNK, kw_ref, vw_ref, r0, A_BAND, bias_ref, col >= first_valid, o_ref)
        return carry

    lax.fori_loop(0, qb // CHUNK, chunk, 0)


def band_prompt(proj, bias, lp):
    pad = A_PREV_CHUNKS * CHUNK
    qb = pad
    blk = (qb, A_WIDTH)
    prev = lambda c: (lambda i: (jnp.maximum(i - 1, 0), c))
    cur = lambda c: (lambda i: (i, c))
    return pl.pallas_call(
        functools.partial(_band_prompt_kernel, qb=qb, pad=pad),
        grid=(lp // qb,),
        in_specs=[pl.BlockSpec(blk, cur(0)),
                  pl.BlockSpec(blk, prev(1)), pl.BlockSpec(blk, cur(1)),
                  pl.BlockSpec(blk, prev(2)), pl.BlockSpec(blk, cur(2)),
                  pl.BlockSpec((A_HEADS // 2, 2 * CHUNK, A_BAND), lambda i: (0, 0, 0))],
        out_specs=pl.BlockSpec(blk, lambda i: (i, 0)),
        out_shape=jax.ShapeDtypeStruct((lp, A_WIDTH), BF16),
        scratch_shapes=[pltpu.VMEM((pad + qb, A_WIDTH), BF16), pltpu.VMEM((pad + qb, A_WIDTH), BF16)],
        compiler_params=_params("parallel"),
    )(proj, proj, proj, proj, proj, bias)


def _band_sample_kernel(q_ref, kn_ref, vn_ref, ck_ref, cv_ref, bias_ref, o_ref, kw_ref, vw_ref, *, a_len, ds):
    kw_ref[0:a_len, :] = ck_ref[...].astype(BF16)
    kw_ref[a_len:a_len + ds, :] = kn_ref[...].astype(BF16)
    vw_ref[0:a_len, :] = cv_ref[...].astype(BF16)
    vw_ref[a_len:a_len + ds, :] = vn_ref[...].astype(BF16)
    _band_pairs(q_ref, 0, ds, kw_ref, vw_ref, 0, a_len + ds, bias_ref, None, o_ref)


def band_sample(proj, cache_k, cache_v, bias, lp, nb, ds):
    a_len = cache_k.shape[1]
    row = lambda c: (lambda b: (lp // ds + b, c))
    return pl.pallas_call(
        functools.partial(_band_sample_kernel, a_len=a_len, ds=ds),
        grid=(nb,),
        in_specs=[pl.BlockSpec((ds, A_WIDTH), row(0)),
                  pl.BlockSpec((ds, A_WIDTH), row(1)),
                  pl.BlockSpec((ds, A_WIDTH), row(2)),
                  pl.BlockSpec((None, a_len, A_WIDTH), lambda b: (b, 0, 0)),
                  pl.BlockSpec((None, a_len, A_WIDTH), lambda b: (b, 0, 0)),
                  pl.BlockSpec((A_HEADS // 2, 2 * ds, a_len + ds), lambda b: (0, 0, 0))],
        out_specs=pl.BlockSpec((ds, A_WIDTH), lambda b: (b, 0)),
        out_shape=jax.ShapeDtypeStruct((nb * ds, A_WIDTH), BF16),
        scratch_shapes=[pltpu.VMEM((a_len + ds, A_WIDTH), BF16), pltpu.VMEM((a_len + ds, A_WIDTH), BF16)],
        compiler_params=_params("parallel"),
    )(proj, proj, proj, cache_k, cache_v, bias)


def _rel_bias_tile(rel_bias, q0, nq, k0, nk):
    rel_max = q0 - k0 + nq - 1
    rel = jnp.clip(rel_max - jnp.arange(nq + nk - 1), -A_REL_CLIP, A_REL_CLIP) + A_REL_CLIP
    ext = rel_bias[:, rel].astype(F32)
    return jnp.stack([ext[:, nq - 1 - i:nq - 1 - i + nk] for i in range(nq)], axis=1)


def _pair_rows(bias):
    h, nq, nk = bias.shape
    return bias.reshape(h // 2, 2 * nq, nk)


def _pool_kernel(prev_ref, cur_ref, w_ref, sc_ref, o_ref, ext_ref, *, tm, prompt, pos0):
    i = pl.program_id(0)
    prev = prev_ref[...]
    if prompt:
        prev = jnp.where(i == 0, 0.0, prev)
        pos = i * tm + lax.broadcasted_iota(jnp.int32, (tm, 1), 0)
    else:
        pos = pos0 + lax.broadcasted_iota(jnp.int32, (tm, 1), 0)
    ext_ref[0:POOL_HALO, :] = prev
    ext_ref[POOL_HALO:POOL_HALO + tm, :] = cur_ref[...]
    for g, w in enumerate(B_WINDOWS):
        cs = slice(g * B_GROUP, (g + 1) * B_GROUP)
        tok = ext_ref[POOL_HALO:POOL_HALO + tm, cs]
        tot = tok
        for j in range(1, w):
            tot = tot + ext_ref[POOL_HALO - j:POOL_HALO - j + tm, cs]
        cnt = jnp.minimum(pos + 1, w).astype(F32)
        pooled = (tot / cnt - tok).astype(BF16)
        o = jnp.dot(pooled, w_ref[g], preferred_element_type=F32) * sc_ref[:, cs]
        o_ref[:, cs] = o.astype(o_ref.dtype)


def pool_prompt(proj, pool_w, pool_scale, lp, tm):
    ucol = 3 * A_WIDTH // B_WIDTH
    per = tm // POOL_HALO
    return pl.pallas_call(
        functools.partial(_pool_kernel, tm=tm, prompt=True, pos0=0),
        grid=(lp // tm,),
        in_specs=[pl.BlockSpec((POOL_HALO, B_WIDTH), lambda i: (jnp.maximum(i * per - 1, 0), ucol)),
                  pl.BlockSpec((tm, B_WIDTH), lambda i: (i, ucol)),
                  pl.BlockSpec((len(B_WINDOWS), B_GROUP, B_GROUP), lambda i: (0, 0, 0)),
                  pl.BlockSpec((1, B_WIDTH), lambda i: (0, 0))],
        out_specs=pl.BlockSpec((tm, B_WIDTH), lambda i: (i, 0)),
        out_shape=jax.ShapeDtypeStruct((lp, B_WIDTH), BF16),
        scratch_shapes=[pltpu.VMEM((POOL_HALO + tm, B_WIDTH), F32)],
        compiler_params=_params("parallel"),
    )(proj, proj, pool_w, pool_scale)


def pool_sample(u_ext, pool_w, pool_scale, past):
    nb, tot, _ = u_ext.shape
    ds = tot - POOL_HALO
    return pl.pallas_call(
        functools.partial(_pool_kernel, tm=ds, prompt=False, pos0=past),
        grid=(nb,),
        in_specs=[pl.BlockSpec((None, POOL_HALO, B_WIDTH), lambda b: (b, 0, 0)),
                  pl.BlockSpec((None, ds, B_WIDTH), lambda b: (b, POOL_HALO // ds, 0)),
                  pl.BlockSpec((len(B_WINDOWS), B_GROUP, B_GROUP), lambda b: (0, 0, 0)),
                  pl.BlockSpec((1, B_WIDTH), lambda b: (0, 0))],
        out_specs=pl.BlockSpec((ds, B_WIDTH), lambda b: (b, 0)),
        out_shape=jax.ShapeDtypeStruct((nb * ds, B_WIDTH), BF16),
        scratch_shapes=[pltpu.VMEM((POOL_HALO + ds, B_WIDTH), F32)],
        compiler_params=_params("parallel"),
    )(u_ext, u_ext, pool_w, pool_scale)


def _mm_res_kernel(*refs, n_in):
    xs, ws = refs[:n_in], refs[n_in:2 * n_in]
    res_ref, o_ref = refs[2 * n_in], refs[2 * n_in + 1]
    acc = res_ref[...]
    for x_ref, w_ref in zip(xs, ws):
        acc = acc + jnp.dot(x_ref[...], w_ref[...], preferred_element_type=F32)
    o_ref[...] = acc


def mm_res(xs, ws, res, tm, tn):
    m, n = res.shape
    n_in = len(xs)
    in_specs = ([pl.BlockSpec((tm, x.shape[1]), lambda i, j: (i, 0)) for x in xs]
                + [pl.BlockSpec((w.shape[0], tn), lambda i, j: (0, j)) for w in ws]
                + [pl.BlockSpec((tm, tn), lambda i, j: (i, j))])
    return pl.pallas_call(
        functools.partial(_mm_res_kernel, n_in=n_in),
        grid=(m // tm, n // tn),
        in_specs=in_specs,
        out_specs=pl.BlockSpec((tm, tn), lambda i, j: (i, j)),
        out_shape=jax.ShapeDtypeStruct((m, n), F32),
        compiler_params=_params("parallel", "arbitrary"),
    )(*xs, *ws, res)


def _swiglu_tile(h, wg, wu, wd):
    a = jnp.dot(h, wg, preferred_element_type=F32)
    b = jnp.dot(h, wu, preferred_element_type=F32)
    act = (a * jax.nn.sigmoid(a) * b).astype(BF16)
    return jnp.dot(act, wd, preferred_element_type=F32)


def _ffn_kernel(y_ref, g_ref, wg_ref, wu_ref, wd_ref, o_ref, h_ref, acc_ref):
    f = pl.program_id(1)

    @pl.when(f == 0)
    def _():
        h_ref[...] = _rms(y_ref[...], g_ref[...]).astype(BF16)
        acc_ref[...] = jnp.zeros_like(acc_ref)

    acc_ref[...] += _swiglu_tile(h_ref[...], wg_ref[...], wu_ref[...], wd_ref[...])

    @pl.when(f == pl.num_programs(1) - 1)
    def _():
        o_ref[...] = y_ref[...] + acc_ref[...]


def ffn(y, g, wg, wu, wd, tm, tf):
    m, d = y.shape
    ff = wg.shape[1]
    return pl.pallas_call(
        _ffn_kernel,
        grid=(m // tm, ff // tf),
        in_specs=[pl.BlockSpec((tm, d), lambda i, f: (i, 0)),
                  pl.BlockSpec((1, d), lambda i, f: (0, 0)),
                  pl.BlockSpec((d, tf), lambda i, f: (0, f)),
                  pl.BlockSpec((d, tf), lambda i, f: (0, f)),
                  pl.BlockSpec((tf, d), lambda i, f: (f, 0))],
        out_specs=pl.BlockSpec((tm, d), lambda i, f: (i, 0)),
        out_shape=jax.ShapeDtypeStruct((m, d), F32),
        scratch_shapes=[pltpu.VMEM((tm, d), BF16), pltpu.VMEM((tm, d), F32)],
        compiler_params=_params("parallel", "arbitrary"),
    )(y, g.reshape(1, d), wg, wu, wd)


def _rope_tables(pos, head_dim):
    rot = head_dim // ROPE_FRAC
    half = rot // 2
    inv = jnp.exp(-math.log(ROPE_THETA) * jnp.arange(half, dtype=F32) * (2.0 / rot))
    ang = pos.astype(F32)[:, None] * inv[None, :]
    cos, sin = jnp.cos(ang), jnp.sin(ang)
    m = pos.shape[0]
    one = jnp.ones((m, head_dim - rot), F32)
    zero_r = jnp.zeros((m, head_dim - rot), F32)
    zero_h = jnp.zeros((m, half), F32)
    c = jnp.concatenate([cos, cos, one], axis=1)
    s_dn = jnp.concatenate([-sin, zero_h, zero_r], axis=1)
    s_up = jnp.concatenate([zero_h, sin, zero_r], axis=1)
    rep = LANES // head_dim
    return jnp.stack([jnp.tile(c, (1, rep)), jnp.tile(s_dn, (1, rep)), jnp.tile(s_up, (1, rep))])


def _rot(x, tab_ref, half):
    return (x * tab_ref[0] + pltpu.roll(x, LANES - half, 1) * tab_ref[1]
            + pltpu.roll(x, half, 1) * tab_ref[2])


def _rope_kernel(main_ref, tail_ref, tq_ref, ti_ref, o_ref, *, n_qk, n_v, n_qi, wi_scale):
    half_qk = C_HEAD_DIM // ROPE_FRAC // 2
    half_i = IDX_DIM // ROPE_FRAC // 2
    for c in range(n_qk + n_v + n_qi):
        cs = slice(c * LANES, (c + 1) * LANES)
        x = main_ref[:, cs]
        if c < n_qk:
            x = _rot(x, tq_ref, half_qk)
        elif c >= n_qk + n_v:
            x = _rot(x, ti_ref, half_i)
        o_ref[:, cs] = x
    t = tail_ref[...]
    lane = lax.broadcasted_iota(jnp.int32, (1, LANES), 1)
    c = n_qk + n_v + n_qi
    o_ref[:, c * LANES:(c + 1) * LANES] = jnp.where(lane < IDX_DIM, _rot(t, ti_ref, half_i), t * wi_scale)


def rope_all(main, tail, tab_qk, tab_idx, tm):
    m, nmain = main.shape
    n_qk = (C_HEADS + C_KV_HEADS) * C_HEAD_DIM // LANES
    n_v = C_KV_HEADS * C_HEAD_DIM // LANES
    n_qi = IDX_HEADS * IDX_DIM // LANES
    wi_scale = (IDX_HEADS ** -0.5) * (IDX_DIM ** -0.5)
    return pl.pallas_call(
        functools.partial(_rope_kernel, n_qk=n_qk, n_v=n_v, n_qi=n_qi, wi_scale=wi_scale),
        grid=(m // tm,),
        in_specs=[pl.BlockSpec((tm, nmain), lambda i: (i, 0)),
                  pl.BlockSpec((tm, LANES), lambda i: (i, 0)),
                  pl.BlockSpec((3, tm, LANES), lambda i: (0, i, 0)),
                  pl.BlockSpec((3, tm, LANES), lambda i: (0, i, 0))],
        out_specs=pl.BlockSpec((tm, nmain + LANES), lambda i: (i, 0)),
        out_shape=jax.ShapeDtypeStruct((m, nmain + LANES), F32),
        compiler_params=_params("parallel"),
    )(main, tail, tab_qk, tab_idx)


def _dsa_kernel(q_ref, qi_ref, wi_ref, k_ref, v_ref, ki_ref, o_ref,
                sc_ref, qs_ref, acc_ref, m_ref, l_ref,
                *, tq, kb, n_valid, q_pos0, topk):
    i = pl.program_id(1)
    rows = C_GROUPS * tq
    sub = kb // LANES
    kf = float(topk)

    q_pos = q_pos0 + i * tq + lax.broadcasted_iota(jnp.int32, (tq, 1), 0)
    q_chunk = q_pos // CHUNK
    last_chunk = (q_pos0 + i * tq + tq - 1) // CHUNK
    kv_limit = jnp.minimum(n_valid, (last_chunk + 1) * CHUNK)
    nkb = (kv_limit + kb - 1) // kb

    lane = lax.broadcasted_iota(jnp.int32, (1, LANES), 1)
    first = lane < IDX_DIM
    wi = wi_ref[...]

    def score_block(b, carry):
        for c in range(sub):
            off = pl.multiple_of(b * kb + c * LANES, LANES)
            kib = ki_ref[pl.ds(off, LANES), :]
            acc = jnp.zeros((tq, LANES), F32)
            for hp in range(IDX_HEADS // 2):
                qp = qi_ref[:, hp * LANES:(hp + 1) * LANES]
                for half in range(2):
                    keep = first if half == 0 else jnp.logical_not(first)
                    qm = jnp.where(keep, qp, jnp.zeros_like(qp))
                    d = lax.dot_general(qm, kib, _NT, preferred_element_type=F32)
                    h = hp * 2 + half
                    acc = acc + jnp.maximum(d, 0.0) * wi[:, h:h + 1]
            k_pos = off + lane
            adm = jnp.logical_and(k_pos // CHUNK <= q_chunk, k_pos < n_valid)
            sc_ref[b, :, c * LANES:(c + 1) * LANES] = jnp.where(adm, acc, NEG)
        return carry

    lax.fori_loop(0, nkb, score_block, 0)

    def lane_sum(x):
        return jnp.sum(x, axis=1, keepdims=True)

    def count_ge(t):
        def body(b, acc):
            for c in range(sub):
                blk = sc_ref[b, :, c * LANES:(c + 1) * LANES]
                acc = acc + jnp.where(blk >= t, 1.0, 0.0)
            return acc
        return lane_sum(lax.fori_loop(0, nkb, body, jnp.zeros((tq, LANES), F32)))

    def stats(b, carry):
        mx, mn, cnt = carry
        for c in range(sub):
            blk = sc_ref[b, :, c * LANES:(c + 1) * LANES]
            ok = blk > 0.5 * NEG
            mx = jnp.maximum(mx, blk)
            mn = jnp.minimum(mn, jnp.where(ok, blk, BIG))
            cnt = cnt + jnp.where(ok, 1.0, 0.0)
        return mx, mn, cnt

    mx, mn, cnt = lax.fori_loop(
        0, nkb, stats,
        (jnp.full((tq, LANES), NEG, F32), jnp.full((tq, LANES), BIG, F32), jnp.zeros((tq, LANES), F32)))
    row_max = jnp.max(mx, axis=1, keepdims=True)
    row_min = jnp.min(mn, axis=1, keepdims=True)
    n_adm = lane_sum(cnt)

    done0 = jnp.where(n_adm <= kf, 1.0, 0.0)
    state0 = (row_min, jnp.full((tq, 1), BIG, F32), row_max, jnp.full((tq, 1), 0.5 * NEG, F32), done0)

    def bisect(state, n_steps):
        def cond(c):
            it, st = c
            return jnp.logical_and(it < n_steps, jnp.min(st[4]) < 0.5)

        def body(c):
            it, (lo, hi, mid, thr, done) = c
            cnt = count_ge(mid)
            live = done < 0.5
            hit = jnp.logical_and(live, cnt == kf)
            ge = cnt >= kf
            thr = jnp.where(hit, mid, thr)
            done = jnp.where(hit, 1.0, done)
            lo = jnp.where(ge, mid, lo)
            hi = jnp.where(ge, hi, mid)
            return it + 1, (lo, hi, 0.5 * (lo + hi), thr, done)

        return lax.while_loop(cond, body, (jnp.int32(0), state))[1]

    def snap(state):
        lo, hi, mid, thr, done = state

        def body(b, carry):
            v_lo, v_hi = carry
            for c in range(sub):
                blk = sc_ref[b, :, c * LANES:(c + 1) * LANES]
                v_lo = jnp.minimum(v_lo, jnp.where(blk >= lo, blk, BIG))
                v_hi = jnp.maximum(v_hi, jnp.where(blk < hi, blk, NEG))
            return v_lo, v_hi

        v_lo, v_hi = lax.fori_loop(0, nkb, body,
                                   (jnp.full((tq, LANES), BIG, F32), jnp.full((tq, LANES), NEG, F32)))
        v_lo = jnp.min(v_lo, axis=1, keepdims=True)
        v_hi = jnp.max(v_hi, axis=1, keepdims=True)
        live = done < 0.5
        tie = jnp.logical_and(live, v_lo == v_hi)
        thr = jnp.where(tie, v_lo, thr)
        done = jnp.where(tie, 1.0, done)
        lo = jnp.where(live, v_lo, lo)
        return (lo, hi, 0.5 * (lo + hi), thr, done), jnp.where(tie, 1.0, 0.0)

    state = bisect(state0, 32)

    def refine_cond(c):
        rounds, st, _ = c
        return jnp.logical_and(rounds < 10, jnp.min(st[4]) < 0.5)

    def refine_body(c):
        rounds, st, tie = c
        st, new_tie = snap(st)
        st = bisect(st, 32)
        return rounds + 1, st, jnp.maximum(tie, new_tie)

    _, state, tie = lax.while_loop(refine_cond, refine_body,
                                   (jnp.int32(0), state, jnp.zeros((tq, 1), F32)))
    thr = state[3]
    any_tie = jnp.max(tie) > 0.5

    @pl.when(jnp.logical_not(any_tie))
    def _():
        def body(b, carry):
            for c in range(sub):
                cs = slice(c * LANES, (c + 1) * LANES)
                sc_ref[b, :, cs] = jnp.where(sc_ref[b, :, cs] >= thr, 0.0, NEG)
            return carry
        lax.fori_loop(0, nkb, body, 0)

    @pl.when(any_tie)
    def _():
        def gt_body(b, acc):
            for c in range(sub):
                acc = acc + jnp.where(sc_ref[b, :, c * LANES:(c + 1) * LANES] > thr, 1.0, 0.0)
            return acc
        need = kf - lane_sum(lax.fori_loop(0, nkb, gt_body, jnp.zeros((tq, LANES), F32)))

        def count_eq_upto(j):
            def body(b, acc):
                for c in range(sub):
                    blk = sc_ref[b, :, c * LANES:(c + 1) * LANES]
                    idx = (b * kb + c * LANES + lane).astype(F32)
                    acc = acc + jnp.where(jnp.logical_and(blk == thr, idx <= j), 1.0, 0.0)
                return acc
            return lane_sum(lax.fori_loop(0, nkb, body, jnp.zeros((tq, LANES), F32)))

        def idx_body(_, c):
            lo_j, hi_j = c
            mid_j = jnp.floor(0.5 * (lo_j + hi_j))
            ok = count_eq_upto(mid_j) >= need
            return jnp.where(ok, lo_j, mid_j), jnp.where(ok, mid_j, hi_j)

        n_steps = max(1, math.ceil(math.log2(sc_ref.shape[0] * kb + 1)))
        _, last = lax.fori_loop(
            0, n_steps, idx_body,
            (jnp.full((tq, 1), -1.0, F32), jnp.full((tq, 1), float(sc_ref.shape[0] * kb), F32)))
        last = jnp.where(tie > 0.5, last, BIG)

        def body(b, carry):
            for c in range(sub):
                cs = slice(c * LANES, (c + 1) * LANES)
                blk = sc_ref[b, :, cs]
                idx = (b * kb + c * LANES + lane).astype(F32)
                sel = jnp.logical_or(blk > thr, jnp.logical_and(blk == thr, idx <= last))
                sc_ref[b, :, cs] = jnp.where(sel, 0.0, NEG)
            return carry
        lax.fori_loop(0, nkb, body, 0)

    for g in range(C_KV_HEADS):
        for hh in range(C_GROUPS):
            h = g * C_GROUPS + hh
            qs_ref[g, hh * tq:(hh + 1) * tq, :] = q_ref[:, h * C_HEAD_DIM:(h + 1) * C_HEAD_DIM]
    m_ref[...] = jnp.full(m_ref.shape, NEG, F32)
    l_ref[...] = jnp.zeros(l_ref.shape, F32)
    acc_ref[...] = jnp.zeros(acc_ref.shape, F32)

    def attend(b, carry):
        off = pl.multiple_of(b * kb, kb)
        bias = sc_ref[b]
        bias = jnp.concatenate([bias] * C_GROUPS, axis=0)
        for g in range(C_KV_HEADS):
            cs = slice(g * C_HEAD_DIM, (g + 1) * C_HEAD_DIM)
            kblk = k_ref[pl.ds(off, kb), cs]
            vblk = v_ref[pl.ds(off, kb), cs]
            s = lax.dot_general(qs_ref[g], kblk, _NT, preferred_element_type=F32)
            s = s * (C_HEAD_DIM ** -0.5) + bias
            m_old = m_ref[g]
            m_new = jnp.maximum(m_old, jnp.max(s, axis=1, keepdims=True))
            alpha = jnp.exp(m_old - m_new)
            p = jnp.exp(s - m_new)
            l_ref[g] = alpha * l_ref[g] + jnp.sum(p, axis=1, keepdims=True)
            acc_ref[g] = alpha * acc_ref[g] + jnp.dot(p.astype(BF16), vblk, preferred_element_type=F32)
            m_ref[g] = m_new
        return carry

    lax.fori_loop(0, nkb, attend, 0)

    for g in range(C_KV_HEADS):
        o = acc_ref[g] / l_ref[g]
        for hh in range(C_GROUPS):
            h = g * C_GROUPS + hh
            o_ref[:, h * C_HEAD_DIM:(h + 1) * C_HEAD_DIM] = o[hh * tq:(hh + 1) * tq, :].astype(o_ref.dtype)


def dsa(q, qi, wi, k, v, ki2, *, tq, kb, n_valid, q_pos0, topk):
    nbatch, nq, dq = q.shape
    s_pad = k.shape[1]
    rows = C_GROUPS * tq
    qmap = lambda b, i: (b, i, 0)
    kmap = lambda b, i: (b, 0, 0)
    return pl.pallas_call(
        functools.partial(_dsa_kernel, tq=tq, kb=kb, n_valid=n_valid, q_pos0=q_pos0, topk=topk),
        grid=(nbatch, nq // tq),
        in_specs=[pl.BlockSpec((None, tq, dq), qmap),
                  pl.BlockSpec((None, tq, qi.shape[2]), qmap),
                  pl.BlockSpec((None, tq, wi.shape[2]), qmap),
                  pl.BlockSpec((None, s_pad, k.shape[2]), kmap),
                  pl.BlockSpec((None, s_pad, v.shape[2]), kmap),
                  pl.BlockSpec((None, s_pad, LANES), kmap)],
        out_specs=pl.BlockSpec((None, tq, dq), qmap),
        out_shape=jax.ShapeDtypeStruct((nbatch, nq, dq), BF16),
        scratch_shapes=[pltpu.VMEM((s_pad // kb, tq, kb), F32),
                        pltpu.VMEM((C_KV_HEADS, rows, C_HEAD_DIM), BF16),
                        pltpu.VMEM((C_KV_HEADS, rows, C_HEAD_DIM), F32),
                        pltpu.VMEM((C_KV_HEADS, rows, 1), F32),
                        pltpu.VMEM((C_KV_HEADS, rows, 1), F32)],
        compiler_params=_params("parallel", "arbitrary"),
    )(q, qi, wi, k, v, ki2)


def _dsa_t_kernel(q_ref, qi_ref, wi_ref, k_ref, vt_ref, ki_ref, o_ref,
                  sc_ref, qim_ref, acc_ref, m_ref, l_ref,
                  *, tq, kb, pb, cb, n_valid, q_pos0, topk):
    i = pl.program_id(1)
    kf = float(topk)

    q_pos = q_pos0 + i * tq + lax.broadcasted_iota(jnp.int32, (1, tq), 1)
    q_chunk = q_pos // CHUNK
    last_chunk = (q_pos0 + i * tq + tq - 1) // CHUNK
    kv_limit = jnp.minimum(n_valid, (last_chunk + 1) * CHUNK)
    nkb = (kv_limit + kb - 1) // kb
    ncb = (kv_limit + cb - 1) // cb

    def col_reduce(x, op):
        groups = x.shape[0] // 8
        chains = 8 if groups % 8 == 0 else 1
        return op(op(x.reshape(chains, groups // chains, 8, tq), axis=1), axis=0)

    col_sum = lambda x: col_reduce(x, jnp.sum)
    col_max = lambda x: col_reduce(x, jnp.max)
    col_min = lambda x: col_reduce(x, jnp.min)

    lane = lax.broadcasted_iota(jnp.int32, (1, LANES), 1)
    first = lane < IDX_DIM
    for hp in range(IDX_HEADS // 2):
        qp = qi_ref[:, hp * LANES:(hp + 1) * LANES]
        qim_ref[2 * hp] = jnp.where(first, qp, jnp.zeros_like(qp))
        qim_ref[2 * hp + 1] = jnp.where(first, jnp.zeros_like(qp), qp)
    block_iota = lax.broadcasted_iota(jnp.int32, (pb, 1), 0)

    def score_block(b, carry):
        off = pl.multiple_of(b * pb, pb)
        kib = ki_ref[pl.ds(off, pb), :]
        acc = jnp.zeros((pb, tq), F32)
        for h in range(IDX_HEADS):
            d = lax.dot_general(kib, qim_ref[h], _NT, preferred_element_type=F32)
            acc = acc + jnp.maximum(d, 0.0) * wi_ref[h:h + 1, :]
        k_pos = off + block_iota
        adm = jnp.logical_and(k_pos // CHUNK <= q_chunk, k_pos < n_valid)
        sc_ref[pl.ds(off, pb), :] = jnp.where(adm, acc, NEG)
        return carry

    lax.fori_loop(0, ncb * (cb // pb), score_block, 0)
    key_iota = lax.broadcasted_iota(jnp.int32, (cb, 1), 0)

    def blocks(body, init):
        def step(b, carry):
            off = pl.multiple_of(b * cb, cb)
            return body(off, sc_ref[pl.ds(off, cb), :], carry)
        return lax.fori_loop(0, ncb, step, init)

    def total(x):
        return jnp.sum(x, axis=0, keepdims=True)

    def count_ge(t):
        return total(blocks(lambda off, blk, acc: acc + col_sum(jnp.where(blk >= t, 1.0, 0.0)),
                            jnp.zeros((8, tq), F32)))

    def stats(off, blk, carry):
        mx, mn, cnt = carry
        ok = blk > 0.5 * NEG
        return (jnp.maximum(mx, col_max(blk)), jnp.minimum(mn, col_min(jnp.where(ok, blk, BIG))),
                cnt + col_sum(jnp.where(ok, 1.0, 0.0)))

    mx, mn, cnt = blocks(stats, (jnp.full((8, tq), NEG, F32), jnp.full((8, tq), BIG, F32),
                                 jnp.zeros((8, tq), F32)))
    row_max = jnp.max(mx, axis=0, keepdims=True)
    row_min = jnp.min(mn, axis=0, keepdims=True)
    n_adm = total(cnt)

    done0 = jnp.where(n_adm <= kf, 1.0, 0.0)
    state0 = (row_min, jnp.full((1, tq), BIG, F32), row_max, jnp.full((1, tq), 0.5 * NEG, F32), done0)

    def bisect(state, n_steps):
        def cond(c):
            it, st = c
            return jnp.logical_and(it < n_steps, jnp.min(st[4]) < 0.5)

        def body(c):
            it, (lo, hi, mid, thr, done) = c
            cnt = count_ge(mid)
            hit = jnp.logical_and(done < 0.5, cnt == kf)
            ge = cnt >= kf
            thr = jnp.where(hit, mid, thr)
            done = jnp.where(hit, 1.0, done)
            lo = jnp.where(ge, mid, lo)
            hi = jnp.where(ge, hi, mid)
            return it + 1, (lo, hi, 0.5 * (lo + hi), thr, done)

        return lax.while_loop(cond, body, (jnp.int32(0), state))[1]

    def snap(state):
        lo, hi, mid, thr, done = state

        def body(off, blk, carry):
            v_lo, v_hi = carry
            return (jnp.minimum(v_lo, col_min(jnp.where(blk >= lo, blk, BIG))),
                    jnp.maximum(v_hi, col_max(jnp.where(blk < hi, blk, NEG))))

        v_lo, v_hi = blocks(body, (jnp.full((8, tq), BIG, F32), jnp.full((8, tq), NEG, F32)))
        v_lo = jnp.min(v_lo, axis=0, keepdims=True)
        v_hi = jnp.max(v_hi, axis=0, keepdims=True)
        live = done < 0.5
        tie = jnp.logical_and(live, v_lo == v_hi)
        thr = jnp.where(tie, v_lo, thr)
        done = jnp.where(tie, 1.0, done)
        lo = jnp.where(live, v_lo, lo)
        return (lo, hi, 0.5 * (lo + hi), thr, done), jnp.where(tie, 1.0, 0.0)

    state = bisect(state0, 32)

    def refine_cond(c):
        rounds, st, _ = c
        return jnp.logical_and(rounds < 10, jnp.min(st[4]) < 0.5)

    def refine_body(c):
        rounds, st, tie = c
        st, new_tie = snap(st)
        st = bisect(st, 32)
        return rounds + 1, st, jnp.maximum(tie, new_tie)

    _, state, tie = lax.while_loop(refine_cond, refine_body,
                                   (jnp.int32(0), state, jnp.zeros((1, tq), F32)))
    thr = state[3]
    any_tie = jnp.max(tie) > 0.5

    @pl.when(jnp.logical_not(any_tie))
    def _():
        def body(off, blk, carry):
            sc_ref[pl.ds(off, cb), :] = jnp.where(blk >= thr, 0.0, NEG)
            return carry
        blocks(body, 0)

    @pl.when(any_tie)
    def _():
        need = kf - total(blocks(lambda off, blk, acc: acc + col_sum(jnp.where(blk > thr, 1.0, 0.0)),
                                 jnp.zeros((8, tq), F32)))

        def count_eq_upto(j):
            def body(off, blk, acc):
                idx = (off + key_iota).astype(F32)
                return acc + col_sum(jnp.where(jnp.logical_and(blk == thr, idx <= j), 1.0, 0.0))
            return total(blocks(body, jnp.zeros((8, tq), F32)))

        def idx_body(_, c):
            lo_j, hi_j = c
            mid_j = jnp.floor(0.5 * (lo_j + hi_j))
            ok = count_eq_upto(mid_j) >= need
            return jnp.where(ok, lo_j, mid_j), jnp.where(ok, mid_j, hi_j)

        n_keys = sc_ref.shape[0]
        n_steps = max(1, math.ceil(math.log2(n_keys + 1)))
        _, last = lax.fori_loop(0, n_steps, idx_body,
                                (jnp.full((1, tq), -1.0, F32), jnp.full((1, tq), float(n_keys), F32)))
        last = jnp.where(tie > 0.5, last, BIG)

        def body(off, blk, carry):
            idx = (off + key_iota).astype(F32)
            sel = jnp.logical_or(blk > thr, jnp.logical_and(blk == thr, idx <= last))
            sc_ref[pl.ds(off, cb), :] = jnp.where(sel, 0.0, NEG)
            return carry
        blocks(body, 0)

    m_ref[...] = jnp.full(m_ref.shape, NEG, F32)
    l_ref[...] = jnp.zeros(l_ref.shape, F32)
    acc_ref[...] = jnp.zeros(acc_ref.shape, F32)

    def attend(b, carry):
        off = pl.multiple_of(b * kb, kb)
        for h in range(C_HEADS):
            g = h // C_GROUPS
            gs = slice(g * C_HEAD_DIM, (g + 1) * C_HEAD_DIM)
            s = lax.dot_general(k_ref[pl.ds(off, kb), gs], q_ref[:, h * C_HEAD_DIM:(h + 1) * C_HEAD_DIM],
                                _NT, preferred_element_type=F32)
            s = s + sc_ref[pl.ds(off, kb), :]
            m_old = m_ref[h]
            m_new = jnp.maximum(m_old, jnp.max(col_max(s), axis=0, keepdims=True))
            alpha = jnp.exp2(m_old - m_new)
            p = jnp.exp2(s - m_new)
            l_ref[h] = alpha * l_ref[h] + total(col_sum(p))
            acc_ref[h] = alpha * acc_ref[h] + jnp.dot(vt_ref[b, gs, :], p.astype(BF16),
                                                      preferred_element_type=F32)
            m_ref[h] = m_new
        return carry

    lax.fori_loop(0, nkb, attend, 0)

    for h in range(C_HEADS):
        o = acc_ref[h] / l_ref[h]
        o_ref[:, h * C_HEAD_DIM:(h + 1) * C_HEAD_DIM] = o.T.astype(o_ref.dtype)


def dsa_t(q, qi, wi_t, k, vt, ki2, *, tq, kb, pb, cb, n_valid, q_pos0, topk):
    nq, dq = q.shape
    s_pad = k.shape[0]
    return pl.pallas_call(
        functools.partial(_dsa_t_kernel, tq=tq, kb=kb, pb=pb, cb=cb, n_valid=n_valid, q_pos0=q_pos0, topk=topk),
        grid=(1, nq // tq),
        in_specs=[pl.BlockSpec((tq, dq), lambda b, i: (i, 0)),
                  pl.BlockSpec((tq, qi.shape[1]), lambda b, i: (i, 0)),
                  pl.BlockSpec((IDX_HEADS, tq), lambda b, i: (0, i)),
                  pl.BlockSpec(k.shape, lambda b, i: (0, 0)),
                  pl.BlockSpec(vt.shape, lambda b, i: (0, 0, 0)),
                  pl.BlockSpec(ki2.shape, lambda b, i: (0, 0))],
        out_specs=pl.BlockSpec((tq, dq), lambda b, i: (i, 0)),
        out_shape=jax.ShapeDtypeStruct((nq, dq), BF16),
        scratch_shapes=[pltpu.VMEM((s_pad, tq), F32),
                        pltpu.VMEM((IDX_HEADS, tq, LANES), BF16),
                        pltpu.VMEM((C_HEADS, C_HEAD_DIM, tq), F32),
                        pltpu.VMEM((C_HEADS, 1, tq), F32),
                        pltpu.VMEM((C_HEADS, 1, tq), F32)],
        compiler_params=_params("arbitrary", "arbitrary"),
    )(q, qi, wi_t, k, vt, ki2)


def _router_kernel(y_ref, g_ref, wr_ref, h_ref, gate_ref):
    h = _rms(y_ref[...], g_ref[...])
    h_ref[...] = h.astype(BF16)
    logits = jnp.dot(h, wr_ref[...], preferred_element_type=F32, precision=lax.Precision.HIGHEST)
    lane = lax.broadcasted_iota(jnp.int32, logits.shape, 1)
    lg = jnp.where(lane < N_EXPERTS, logits, NEG)
    m1 = jnp.max(lg, axis=1, keepdims=True)
    i1 = jnp.min(jnp.where(lg == m1, lane, LANES), axis=1, keepdims=True)
    lg2 = jnp.where(lane == i1, NEG, lg)
    m2 = jnp.max(lg2, axis=1, keepdims=True)
    i2 = jnp.min(jnp.where(lg2 == m2, lane, LANES), axis=1, keepdims=True)
    e = jnp.exp(m2 - m1)
    g1 = 1.0 / (1.0 + e)
    g2 = e / (1.0 + e)
    meta = jnp.where(lane == 0, i1.astype(F32), jnp.where(lane == 1, i2.astype(F32),
                     jnp.where(lane == 2, g1, jnp.where(lane == 3, g2, 0.0))))
    gate_ref[...] = meta


def router(y, g, w_router_pad, tm):
    m, d = y.shape
    return pl.pallas_call(
        _router_kernel,
        grid=(m // tm,),
        in_specs=[pl.BlockSpec((tm, d), lambda i: (i, 0)),
                  pl.BlockSpec((1, d), lambda i: (0, 0)),
                  pl.BlockSpec((d, LANES), lambda i: (0, 0))],
        out_specs=[pl.BlockSpec((tm, d), lambda i: (i, 0)),
                   pl.BlockSpec((tm, LANES), lambda i: (i, 0))],
        out_shape=[jax.ShapeDtypeStruct((m, d), BF16), jax.ShapeDtypeStruct((m, LANES), F32)],
        compiler_params=_params("parallel"),
    )(y, g.reshape(1, d), w_router_pad)


def route_plan(meta, sup):
    m = meta.shape[0]
    n_tiles = (2 * m) // sup + N_EXPERTS
    e_all = jnp.concatenate([meta[:, 0], meta[:, 1]]).astype(jnp.int32)
    g_all = jnp.concatenate([meta[:, 2], meta[:, 3]])
    onehot = (e_all[:, None] == jnp.arange(N_EXPERTS)[None, :]).astype(jnp.int32)
    rank = jnp.sum((jnp.cumsum(onehot, axis=0) - onehot) * onehot, axis=1)
    counts = jnp.sum(onehot, axis=0)
    n_super = (counts + sup - 1) // sup
    super_end = jnp.cumsum(n_super)
    super_start = super_end - n_super
    pos = (super_start * sup)[e_all] + rank
    tiles = jnp.arange(n_tiles)
    used = super_end[-1]
    t_eff = jnp.minimum(tiles, used - 1)
    tile_expert = jnp.minimum(jnp.searchsorted(super_end, t_eff, side="right"), N_EXPERTS - 1).astype(jnp.int32)
    rows = jnp.clip(counts[tile_expert] - (t_eff - super_start[tile_expert]) * sup, 0, sup)
    tile_rows = jnp.where(tiles < used, rows, 0).astype(jnp.int32)
    gate_rows = jnp.zeros((n_tiles * sup,), F32).at[pos].set(g_all)
    gate_rows = jnp.broadcast_to(gate_rows[:, None], (n_tiles * sup, LANES))
    token = jnp.concatenate([jnp.arange(m), jnp.arange(m)]).astype(jnp.int32)
    row_token = jnp.zeros((n_tiles * sup,), jnp.int32).at[pos].set(token)
    return pos.astype(jnp.int32), row_token, gate_rows, tile_expert, tile_rows


def _gather_rows_kernel(tok_ref, h_ref, o_ref, sem, *, batch):
    base = pl.program_id(0) * batch

    def issue(j, carry):
        pltpu.make_async_copy(h_ref.at[tok_ref[base + j]], o_ref.at[j], sem).start()
        return carry

    lax.fori_loop(0, batch, issue, 0)
    pltpu.make_async_copy(o_ref, o_ref, sem).wait()


def gather_rows(h3, row_token, batch):
    n_rows = row_token.shape[0]
    blk = (batch,) + h3.shape[1:]
    return pl.pallas_call(
        functools.partial(_gather_rows_kernel, batch=batch),
        grid_spec=pltpu.PrefetchScalarGridSpec(
            num_scalar_prefetch=1,
            grid=(n_rows // batch,),
            in_specs=[pl.BlockSpec(memory_space=pl.ANY)],
            out_specs=pl.BlockSpec(blk, lambda i, tok: (i, 0, 0)),
            scratch_shapes=[pltpu.SemaphoreType.DMA(())]),
        out_shape=jax.ShapeDtypeStruct((n_rows,) + h3.shape[1:], h3.dtype),
        compiler_params=_params("arbitrary"),
    )(row_token, h3)


def _grouped_ffn_kernel(te_ref, tr_ref, x_ref, gate_ref, wg_ref, wu_ref, wd_ref, o_ref,
                        wgb_ref, wub_ref, wdb_ref, *, sup, sub):
    t = pl.program_id(0)
    f = pl.program_id(1)
    last_f = pl.num_programs(1) - 1
    rows = tr_ref[t]

    @pl.when(rows > 0)
    def _():
        wgb_ref[...] = wg_ref[...].astype(BF16)
        wub_ref[...] = wu_ref[...].astype(BF16)
        wdb_ref[...] = wd_ref[...].astype(BF16)

    for s in range(sup // sub):
        sl = slice(s * sub, (s + 1) * sub)

        @pl.when(jnp.logical_and(s * sub >= rows, f == 0))
        def _():
            o_ref[sl, :] = jnp.zeros((sub, o_ref.shape[1]), F32)

        @pl.when(s * sub < rows)
        def _():
            part = _swiglu_tile(x_ref[sl, :], wgb_ref[...], wub_ref[...], wdb_ref[...])

            @pl.when(f == 0)
            def _():
                o_ref[sl, :] = part

            @pl.when(jnp.logical_and(f > 0, f < last_f))
            def _():
                o_ref[sl, :] += part

            @pl.when(jnp.logical_and(f > 0, f == last_f))
            def _():
                o_ref[sl, :] = (o_ref[sl, :] + part) * gate_ref[sl, 0:1]


def grouped_ffn(xs, gate_rows, tile_expert, tile_rows, wg, wu, wd, sup, sub, tf):
    n_rows, d = xs.shape
    ff = wg.shape[2]
    n_f = ff // tf
    assert n_f > 1
    fidx = lambda t, f, te, tr: jnp.where(tr[t] > 0, f, n_f - 1)
    return pl.pallas_call(
        functools.partial(_grouped_ffn_kernel, sup=sup, sub=sub),
        grid_spec=pltpu.PrefetchScalarGridSpec(
            num_scalar_prefetch=2,
            grid=(n_rows // sup, n_f),
            in_specs=[pl.BlockSpec((sup, d), lambda t, f, te, tr: (t, 0)),
                      pl.BlockSpec((sup, LANES), lambda t, f, te, tr: (t, 0)),
                      pl.BlockSpec((None, d, tf), lambda t, f, te, tr: (te[t], 0, fidx(t, f, te, tr))),
                      pl.BlockSpec((None, d, tf), lambda t, f, te, tr: (te[t], 0, fidx(t, f, te, tr))),
                      pl.BlockSpec((None, tf, d), lambda t, f, te, tr: (te[t], fidx(t, f, te, tr), 0))],
            out_specs=pl.BlockSpec((sup, d), lambda t, f, te, tr: (t, 0)),
            scratch_shapes=[pltpu.VMEM((d, tf), BF16), pltpu.VMEM((d, tf), BF16), pltpu.VMEM((tf, d), BF16)]),
        out_shape=jax.ShapeDtypeStruct((n_rows, d), F32),
        compiler_params=_params("arbitrary", "arbitrary"),
    )(tile_expert, tile_rows, xs, gate_rows, wg, wu, wd)


def _combine_kernel(pos_ref, y_ref, g_ref, ys_ref, o_ref, buf_ref, sem, *, n_tok, tmc):
    base = pl.program_id(0) * tmc

    def issue(j, carry):
        t = base + j
        pltpu.make_async_copy(ys_ref.at[pos_ref[t]], buf_ref.at[0, j], sem).start()
        pltpu.make_async_copy(ys_ref.at[pos_ref[n_tok + t]], buf_ref.at[1, j], sem).start()
        return carry

    lax.fori_loop(0, tmc, issue, 0)
    pltpu.make_async_copy(buf_ref.at[0], buf_ref.at[0], sem).wait()
    pltpu.make_async_copy(buf_ref.at[1], buf_ref.at[1], sem).wait()
    x = y_ref[...] + buf_ref[0] + buf_ref[1]
    ms = jnp.sum(jnp.sum(x * x, axis=2, keepdims=True), axis=1, keepdims=True) / (x.shape[1] * x.shape[2])
    o_ref[...] = x * lax.rsqrt(ms + NORM_EPS) * g_ref[...]


def combine_final(y3, g_final3, ys3, pos, tmc):
    m = y3.shape[0]
    blk = (tmc,) + y3.shape[1:]
    return pl.pallas_call(
        functools.partial(_combine_kernel, n_tok=m, tmc=tmc),
        grid_spec=pltpu.PrefetchScalarGridSpec(
            num_scalar_prefetch=1,
            grid=(m // tmc,),
            in_specs=[pl.BlockSpec(blk, lambda i, pos: (i, 0, 0)),
                      pl.BlockSpec((1,) + y3.shape[1:], lambda i, pos: (0, 0, 0)),
                      pl.BlockSpec(memory_space=pl.ANY)],
            out_specs=pl.BlockSpec(blk, lambda i, pos: (i, 0, 0)),
            scratch_shapes=[pltpu.VMEM((2,) + blk, F32), pltpu.SemaphoreType.DMA(())]),
        out_shape=jax.ShapeDtypeStruct(y3.shape, F32),
        compiler_params=_params("arbitrary"),
    )(pos, y3, g_final3, ys3)


def kernel(x_prompt, x_sample, cache_a_k, cache_a_v, state_pool, cache_c_k, cache_c_v, cache_c_idx,
           norm_mix, norm_ffn, norm_final, w_in_even, w_out_even, a_rel_bias, pool_w, pool_scale,
           ffn_w_gate, ffn_w_up, ffn_w_down, w_in_odd, w_out_odd,
           moe_router, moe_w_gate, moe_w_up, moe_w_down):
    nbp, lp, d = x_prompt.shape
    nb, ds, _ = x_sample.shape
    past = cache_c_k.shape[2]
    a_len = cache_a_k.shape[2]
    assert nbp == 1 and lp % 512 == 0 and (nb * ds) % 512 == 0 and ds == POOL_HALO and past >= POOL_HALO
    ns = nb * ds
    m = lp + ns
    tm = 512
    bf = lambda t: t.astype(BF16)

    x = jnp.concatenate([x_prompt.reshape(lp, d), x_sample.reshape(ns, d)], axis=0)

    proj0 = norm_proj(x, norm_mix[0], bf(w_in_even[0]), tm, 1024)
    k0 = proj0[:, A_WIDTH:2 * A_WIDTH]
    v0 = proj0[:, 2 * A_WIDTH:3 * A_WIDTH]
    u0 = proj0[:, 3 * A_WIDTH:]

    pad = A_PREV_CHUNKS * CHUNK
    bias_p = _rel_bias_tile(a_rel_bias[0], 0, CHUNK, -pad, A_BAND)
    a_p = band_prompt(proj0, _pair_rows(bias_p), lp)

    k_pos = past - a_len + jnp.arange(a_len + ds)
    q_pos = past + jnp.arange(ds)
    qch, kch = q_pos // CHUNK, k_pos // CHUNK
    ok = ((k_pos[None, :] >= 0) & (kch[None, :] <= qch[:, None])
          & (kch[None, :] >= qch[:, None] - A_PREV_CHUNKS))
    bias_s = jnp.where(ok[None], _rel_bias_tile(a_rel_bias[0], past, ds, past - a_len, a_len + ds), NEG)
    a_s = band_sample(proj0, cache_a_k[0].reshape(nb, a_len, A_WIDTH),
                      cache_a_v[0].reshape(nb, a_len, A_WIDTH), _pair_rows(bias_s), lp, nb, ds)

    u_s = u0[lp:].reshape(nb, ds, B_WIDTH)
    u_hist = jnp.concatenate([state_pool[0], u_s], axis=1)
    u_ext = jnp.concatenate([jnp.zeros((nb, POOL_HALO - B_HIST, B_WIDTH), F32), u_hist], axis=1)
    pw = bf(pool_w[0])
    ps = pool_scale[0].reshape(1, B_WIDTH)
    p_p = pool_prompt(proj0, pw, ps, lp, tm)
    p_s = pool_sample(u_ext, pw, ps, past)

    a = jnp.concatenate([a_p, a_s], axis=0)
    p = jnp.concatenate([p_p, p_s], axis=0)
    wo = bf(w_out_even[0])
    y = mm_res([a, p], [wo[:A_WIDTH], wo[A_WIDTH:]], x, tm, 1024)
    y = ffn(y, norm_ffn[0], bf(ffn_w_gate[0]), bf(ffn_w_up[0]), bf(ffn_w_down[0]), tm, 512)

    n_q = C_HEADS * C_HEAD_DIM
    n_kv = C_KV_HEADS * C_HEAD_DIM
    n_qi = IDX_HEADS * IDX_DIM
    n_main = n_q + 2 * n_kv + n_qi
    w1 = w_in_odd[0]
    w_tail = jnp.pad(w1[:, n_main:], ((0, 0), (0, LANES - (w1.shape[1] - n_main))))
    main = norm_proj(y, norm_mix[1], bf(w1[:, :n_main]), tm, 512)
    tail = norm_proj(y, norm_mix[1], bf(w_tail), tm, LANES)
    pos = jnp.concatenate([jnp.arange(lp), jnp.tile(past + jnp.arange(ds), nb)])
    rot = rope_all(main, tail, _rope_tables(pos, C_HEAD_DIM), _rope_tables(pos, IDX_DIM), tm)

    q1 = rot[:, :n_q]
    k1 = rot[:, n_q:n_q + n_kv]
    v1 = rot[:, n_q + n_kv:n_q + 2 * n_kv]
    qi1 = rot[:, n_q + 2 * n_kv:n_main]
    ki1 = rot[:, n_main:n_main + IDX_DIM]
    wi1 = rot[:, n_main + IDX_DIM:n_main + IDX_DIM + IDX_HEADS]

    dup = lambda t: jnp.concatenate([t, t], axis=-1)
    kb_p = 128
    vt = bf(v1[:lp]).reshape(lp // kb_p, kb_p, n_kv).transpose(0, 2, 1)
    q_scale = (C_HEAD_DIM ** -0.5) * math.log2(math.e)
    o_p = dsa_t(bf(q1[:lp] * q_scale), bf(qi1[:lp]), wi1[:lp].T, bf(k1[:lp]), vt, dup(bf(ki1[:lp])),
                tq=256, kb=kb_p, pb=256, cb=1024, n_valid=lp, q_pos0=0, topk=min(TOPK_MAX, lp // 4))

    s_all = past + ds
    kb_s = 3 * LANES
    s_pad = -(-s_all // kb_s) * kb_s
    stack = lambda cache, new, w: jnp.concatenate(
        [bf(cache.reshape(nb, past, w)), bf(new.reshape(nb, ds, w)), jnp.zeros((nb, s_pad - s_all, w), BF16)], axis=1)
    o_s = dsa(bf(q1[lp:]).reshape(nb, ds, n_q), bf(qi1[lp:]).reshape(nb, ds, n_qi), wi1[lp:].reshape(nb, ds, IDX_HEADS),
              stack(cache_c_k[0], k1[lp:], n_kv), stack(cache_c_v[0], v1[lp:], n_kv),
              dup(stack(cache_c_idx[0], ki1[lp:], IDX_DIM)),
              tq=ds, kb=kb_s, n_valid=s_all, q_pos0=past, topk=min(TOPK_MAX, s_all // 4))

    o = jnp.concatenate([o_p.reshape(lp, n_q), o_s.reshape(ns, n_q)], axis=0)
    y = mm_res([o], [bf(w_out_odd[0])], y, tm, 1024)

    wr = jnp.pad(moe_router[0], ((0, 0), (0, LANES - N_EXPERTS)))
    h, meta = router(y, norm_ffn[1], wr, tm)
    sup = 1024
    pos_rows, row_token, gate_rows, tile_expert, tile_rows = route_plan(meta, sup)
    n_rows = gate_rows.shape[0]
    slab = (d // LANES, LANES)
    xs = gather_rows(h.reshape((m,) + slab), row_token, 512)
    ys = grouped_ffn(xs.reshape(n_rows, d), gate_rows, tile_expert, tile_rows,
                     moe_w_gate[0], moe_w_up[0], moe_w_down[0], sup, 512, 256)
    y = combine_final(y.reshape((m,) + slab), norm_final.reshape((1,) + slab),
                      ys.reshape((n_rows,) + slab), pos_rows, 256).reshape(m, d)

    y_prompt = y[:lp].reshape(1, lp, d)
    y_sample = y[lp:].reshape(nb, ds, d)
    keep = min(A_BAND, lp)
    heads = lambda t, n: t.reshape(1, n, -1, A_HEADS, A_HEAD_DIM)
    a_k_prompt = heads(k0[lp - keep:lp], 1)
    a_v_prompt = heads(v0[lp - keep:lp], 1)
    pool_prompt_out = u0[lp - B_HIST:lp].reshape(1, 1, B_HIST, B_WIDTH)
    c_k_prompt = k1[:lp].reshape(1, 1, lp, C_KV_HEADS, C_HEAD_DIM)
    c_v_prompt = v1[:lp].reshape(1, 1, lp, C_KV_HEADS, C_HEAD_DIM)
    c_idx_prompt = ki1[:lp].reshape(1, 1, lp, IDX_DIM)
    shift = lambda cache, new: jnp.concatenate(
        [cache[0], new.reshape(nb, ds, A_HEADS, A_HEAD_DIM)], axis=1)[:, ds:][None]
    a_k_sample = shift(cache_a_k, k0[lp:])
    a_v_sample = shift(cache_a_v, v0[lp:])
    pool_sample_out = u_hist[:, ds:][None]
    c_k_sample = k1[lp:].reshape(1, nb, ds, C_KV_HEADS, C_HEAD_DIM)
    c_v_sample = v1[lp:].reshape(1, nb, ds, C_KV_HEADS, C_HEAD_DIM)
    c_idx_sample = ki1[lp:].reshape(1, nb, ds, IDX_DIM)
    return (y_prompt, y_sample, a_k_prompt, a_v_prompt, pool_prompt_out,
            c_k_prompt, c_v_prompt, c_idx_prompt,
            a_k_sample, a_v_sample, pool_sample_out,
            c_k_sample, c_v_sample, c_idx_sample)
```

```python
import functools
import math

import jax
import jax.numpy as jnp
from jax import lax
from jax.experimental import pallas as pl
from jax.experimental.pallas import tpu as pltpu

F32 = jnp.float32
BF16 = jnp.bfloat16

NORM_EPS = 1e-6
NEG = -1e30
BIG = 1e30

CHUNK = 64
A_HEADS = 16
A_HEAD_DIM = 64
A_WIDTH = A_HEADS * A_HEAD_DIM
A_PREV_CHUNKS = 8
A_BAND = (A_PREV_CHUNKS + 1) * CHUNK
A_REL_CLIP = 128
B_WINDOWS = (2, 4, 8, 16)
B_GROUP = 256
B_WIDTH = B_GROUP * len(B_WINDOWS)
B_HIST = max(B_WINDOWS) - 1
C_HEADS = 16
C_KV_HEADS = 4
C_HEAD_DIM = 128
C_GROUPS = C_HEADS // C_KV_HEADS
IDX_HEADS = 8
IDX_DIM = 64
TOPK_MAX = 256
ROPE_THETA = 500000.0
ROPE_FRAC = 4
N_EXPERTS = 8

LANES = 128
POOL_HALO = 16
VMEM_LIMIT = 56 * 1024 * 1024

_NT = (((1,), (1,)), ((), ()))


def _params(*sem):
    return pltpu.CompilerParams(dimension_semantics=sem, vmem_limit_bytes=VMEM_LIMIT)


def _rms(x, g):
    ms = jnp.mean(x * x, axis=-1, keepdims=True)
    return x * lax.rsqrt(ms + NORM_EPS) * g


def _norm_proj_kernel(x_ref, g_ref, w_ref, o_ref, h_ref):
    @pl.when(pl.program_id(1) == 0)
    def _():
        h_ref[...] = _rms(x_ref[...], g_ref[...]).astype(BF16)

    o_ref[...] = jnp.dot(h_ref[...], w_ref[...], preferred_element_type=F32)


def norm_proj(x, g, w, tm, tn):
    m, d = x.shape
    n = w.shape[1]
    return pl.pallas_call(
        _norm_proj_kernel,
        grid=(m // tm, n // tn),
        in_specs=[pl.BlockSpec((tm, d), lambda i, j: (i, 0)),
                  pl.BlockSpec((1, d), lambda i, j: (0, 0)),
                  pl.BlockSpec((d, tn), lambda i, j: (0, j))],
        out_specs=pl.BlockSpec((tm, tn), lambda i, j: (i, j)),
        out_shape=jax.ShapeDtypeStruct((m, n), F32),
        scratch_shapes=[pltpu.VMEM((tm, d), BF16)],
        compiler_params=_params("parallel", "arbitrary"),
    )(x, g.reshape(1, d), w)


def _band_pairs(q_ref, q_row0, nq, kw_ref, vw_ref, k_row0, nk, bias_ref, key_ok, o_ref):
    lane = lax.broadcasted_iota(jnp.int32, (1, LANES), 1)
    first = lane < A_HEAD_DIM
    for hp in range(A_HEADS // 2):
        cs = slice(hp * LANES, (hp + 1) * LANES)
        qp = q_ref[pl.ds(q_row0, nq), cs]
        kp = kw_ref[pl.ds(k_row0, nk), cs]
        vp = vw_ref[pl.ds(k_row0, nk), cs]
        qm = jnp.concatenate([jnp.where(first, qp, 0.0), jnp.where(first, 0.0, qp)], axis=0).astype(BF16)
        s = lax.dot_general(qm, kp, _NT, preferred_element_type=F32)
        s = s * (A_HEAD_DIM ** -0.5) + bias_ref[hp]
        if key_ok is not None:
            s = jnp.where(key_ok, s, NEG)
        mx = jnp.max(s, axis=-1, keepdims=True)
        e = jnp.exp(s - mx)
        l = jnp.sum(e, axis=-1, keepdims=True)
        o = jnp.dot(e.astype(BF16), vp, preferred_element_type=F32) / l
        o_ref[pl.ds(q_row0, nq), cs] = jnp.where(first, o[:nq], o[nq:]).astype(o_ref.dtype)


def _band_prompt_kernel(q_ref, kp_ref, kc_ref, vp_ref, vc_ref, bias_ref, o_ref, kw_ref, vw_ref, *, qb, pad):
    i = pl.program_id(0)
    kw_ref[0:pad, :] = kp_ref[...].astype(BF16)
    kw_ref[pad:pad + qb, :] = kc_ref[...].astype(BF16)
    vw_ref[0:pad, :] = vp_ref[...].astype(BF16)
    vw_ref[pad:pad + qb, :] = vc_ref[...].astype(BF16)
    col = lax.broadcasted_iota(jnp.int32, (1, A_BAND), 1)

    def chunk(cc, carry):
        r0 = pl.multiple_of(cc * CHUNK, CHUNK)
        first_valid = jnp.where(i == 0, pad - cc * CHUNK, 0)
        _band_pairs(q_ref, r0, CHUNK, kw_ref, vw_ref, r0, A_BAND, bias_ref, col >= first_valid, o_ref)
        return carry

    lax.fori_loop(0, qb // CHUNK, chunk, 0)


def band_prompt(proj, bias, lp):
    pad = A_PREV_CHUNKS * CHUNK
    qb = pad
    blk = (qb, A_WIDTH)
    prev = lambda c: (lambda i: (jnp.maximum(i - 1, 0), c))
    cur = lambda c: (lambda i: (i, c))
    return pl.pallas_call(
        functools.partial(_band_prompt_kernel, qb=qb, pad=pad),
        grid=(lp // qb,),
        in_specs=[pl.BlockSpec(blk, cur(0)),
                  pl.BlockSpec(blk, prev(1)), pl.BlockSpec(blk, cur(1)),
                  pl.BlockSpec(blk, prev(2)), pl.BlockSpec(blk, cur(2)),
                  pl.BlockSpec((A_HEADS // 2, 2 * CHUNK, A_BAND), lambda i: (0, 0, 0))],
        out_specs=pl.BlockSpec(blk, lambda i: (i, 0)),
        out_shape=jax.ShapeDtypeStruct((lp, A_WIDTH), BF16),
        scratch_shapes=[pltpu.VMEM((pad + qb, A_WIDTH), BF16), pltpu.VMEM((pad + qb, A_WIDTH), BF16)],
        compiler_params=_params("parallel"),
    )(proj, proj, proj, proj, proj, bias)


def _band_sample_kernel(q_ref, kn_ref, vn_ref, ck_ref, cv_ref, bias_ref, o_ref, kw_ref, vw_ref, *, a_len, ds):
    kw_ref[0:a_len, :] = ck_ref[...].astype(BF16)
    kw_ref[a_len:a_len + ds, :] = kn_ref[...].astype(BF16)
    vw_ref[0:a_len, :] = cv_ref[...].astype(BF16)
    vw_ref[a_len:a_len + ds, :] = vn_ref[...].astype(BF16)
    _band_pairs(q_ref, 0, ds, kw_ref, vw_ref, 0, a_len + ds, bias_ref, None, o_ref)


def band_sample(proj, cache_k, cache_v, bias, lp, nb, ds):
    a_len = cache_k.shape[1]
    row = lambda c: (lambda b: (lp // ds + b, c))
    return pl.pallas_call(
        functools.partial(_band_sample_kernel, a_len=a_len, ds=ds),
        grid=(nb,),
        in_specs=[pl.BlockSpec((ds, A_WIDTH), row(0)),
                  pl.BlockSpec((ds, A_WIDTH), row(1)),
                  pl.BlockSpec((ds, A_WIDTH), row(2)),
                  pl.BlockSpec((None, a_len, A_WIDTH), lambda b: (b, 0, 0)),
                  pl.BlockSpec((None, a_len, A_WIDTH), lambda b: (b, 0, 0)),
                  pl.BlockSpec((A_HEADS // 2, 2 * ds, a_len + ds), lambda b: (0, 0, 0))],
        out_specs=pl.BlockSpec((ds, A_WIDTH), lambda b: (b, 0)),
        out_shape=jax.ShapeDtypeStruct((nb * ds, A_WIDTH), BF16),
        scratch_shapes=[pltpu.VMEM((a_len + ds, A_WIDTH), BF16), pltpu.VMEM((a_len + ds, A_WIDTH), BF16)],
        compiler_params=_params("parallel"),
    )(proj, proj, proj, cache_k, cache_v, bias)


def _rel_bias_tile(rel_bias, q0, nq, k0, nk):
    rel_max = q0 - k0 + nq - 1
    rel = jnp.clip(rel_max - jnp.arange(nq + nk - 1), -A_REL_CLIP, A_REL_CLIP) + A_REL_CLIP
    ext = rel_bias[:, rel].astype(F32)
    return jnp.stack([ext[:, nq - 1 - i:nq - 1 - i + nk] for i in range(nq)], axis=1)


def _pair_rows(bias):
    h, nq, nk = bias.shape
    return bias.reshape(h // 2, 2 * nq, nk)


def _pool_kernel(prev_ref, cur_ref, w_ref, sc_ref, o_ref, ext_ref, *, tm, prompt, pos0):
    i = pl.program_id(0)
    prev = prev_ref[...]
    if prompt:
        prev = jnp.where(i == 0, 0.0, prev)
        pos = i * tm + lax.broadcasted_iota(jnp.int32, (tm, 1), 0)
    else:
        pos = pos0 + lax.broadcasted_iota(jnp.int32, (tm, 1), 0)
    ext_ref[0:POOL_HALO, :] = prev
    ext_ref[POOL_HALO:POOL_HALO + tm, :] = cur_ref[...]
    for g, w in enumerate(B_WINDOWS):
        cs = slice(g * B_GROUP, (g + 1) * B_GROUP)
        tok = ext_ref[POOL_HALO:POOL_HALO + tm, cs]
        tot = tok
        for j in range(1, w):
            tot = tot + ext_ref[POOL_HALO - j:POOL_HALO - j + tm, cs]
        cnt = jnp.minimum(pos + 1, w).astype(F32)
        pooled = (tot / cnt - tok).astype(BF16)
        o = jnp.dot(pooled, w_ref[g], preferred_element_type=F32) * sc_ref[:, cs]
        o_ref[:, cs] = o.astype(o_ref.dtype)


def pool_prompt(proj, pool_w, pool_scale, lp, tm):
    ucol = 3 * A_WIDTH // B_WIDTH
    per = tm // POOL_HALO
    return pl.pallas_call(
        functools.partial(_pool_kernel, tm=tm, prompt=True, pos0=0),
        grid=(lp // tm,),
        in_specs=[pl.BlockSpec((POOL_HALO, B_WIDTH), lambda i: (jnp.maximum(i * per - 1, 0), ucol)),
                  pl.BlockSpec((tm, B_WIDTH), lambda i: (i, ucol)),
                  pl.BlockSpec((len(B_WINDOWS), B_GROUP, B_GROUP), lambda i: (0, 0, 0)),
                  pl.BlockSpec((1, B_WIDTH), lambda i: (0, 0))],
        out_specs=pl.BlockSpec((tm, B_WIDTH), lambda i: (i, 0)),
        out_shape=jax.ShapeDtypeStruct((lp, B_WIDTH), BF16),
        scratch_shapes=[pltpu.VMEM((POOL_HALO + tm, B_WIDTH), F32)],
        compiler_params=_params("parallel"),
    )(proj, proj, pool_w, pool_scale)


def pool_sample(u_ext, pool_w, pool_scale, past):
    nb, tot, _ = u_ext.shape
    ds = tot - POOL_HALO
    return pl.pallas_call(
        functools.partial(_pool_kernel, tm=ds, prompt=False, pos0=past),
        grid=(nb,),
        in_specs=[pl.BlockSpec((None, POOL_HALO, B_WIDTH), lambda b: (b, 0, 0)),
                  pl.BlockSpec((None, ds, B_WIDTH), lambda b: (b, POOL_HALO // ds, 0)),
                  pl.BlockSpec((len(B_WINDOWS), B_GROUP, B_GROUP), lambda b: (0, 0, 0)),
                  pl.BlockSpec((1, B_WIDTH), lambda b: (0, 0))],
        out_specs=pl.BlockSpec((ds, B_WIDTH), lambda b: (b, 0)),
        out_shape=jax.ShapeDtypeStruct((nb * ds, B_WIDTH), BF16),
        scratch_shapes=[pltpu.VMEM((POOL_HALO + ds, B_WIDTH), F32)],
        compiler_params=_params("parallel"),
    )(u_ext, u_ext, pool_w, pool_scale)


def _mm_res_kernel(*refs, n_in):
    xs, ws = refs[:n_in], refs[n_in:2 * n_in]
    res_ref, o_ref = refs[2 * n_in], refs[2 * n_in + 1]
    acc = res_ref[...]
    for x_ref, w_ref in zip(xs, ws):
        acc = acc + jnp.dot(x_ref[...], w_ref[...], preferred_element_type=F32)
    o_ref[...] = acc


def mm_res(xs, ws, res, tm, tn):
    m, n = res.shape
    n_in = len(xs)
    in_specs = ([pl.BlockSpec((tm, x.shape[1]), lambda i, j: (i, 0)) for x in xs]
                + [pl.BlockSpec((w.shape[0], tn), lambda i, j: (0, j)) for w in ws]
                + [pl.BlockSpec((tm, tn), lambda i, j: (i, j))])
    return pl.pallas_call(
        functools.partial(_mm_res_kernel, n_in=n_in),
        grid=(m // tm, n // tn),
        in_specs=in_specs,
        out_specs=pl.BlockSpec((tm, tn), lambda i, j: (i, j)),
        out_shape=jax.ShapeDtypeStruct((m, n), F32),
        compiler_params=_params("parallel", "arbitrary"),
    )(*xs, *ws, res)


def _swiglu_tile(h, wg, wu, wd):
    a = jnp.dot(h, wg, preferred_element_type=F32)
    b = jnp.dot(h, wu, preferred_element_type=F32)
    act = (a * jax.nn.sigmoid(a) * b).astype(BF16)
    return jnp.dot(act, wd, preferred_element_type=F32)


def _ffn_kernel(y_ref, g_ref, wg_ref, wu_ref, wd_ref, o_ref, h_ref, acc_ref):
    f = pl.program_id(1)

    @pl.when(f == 0)
    def _():
        h_ref[...] = _rms(y_ref[...], g_ref[...]).astype(BF16)
        acc_ref[...] = jnp.zeros_like(acc_ref)

    acc_ref[...] += _swiglu_tile(h_ref[...], wg_ref[...], wu_ref[...], wd_ref[...])

    @pl.when(f == pl.num_programs(1) - 1)
    def _():
        o_ref[...] = y_ref[...] + acc_ref[...]


def ffn(y, g, wg, wu, wd, tm, tf):
    m, d = y.shape
    ff = wg.shape[1]
    return pl.pallas_call(
        _ffn_kernel,
        grid=(m // tm, ff // tf),
        in_specs=[pl.BlockSpec((tm, d), lambda i, f: (i, 0)),
                  pl.BlockSpec((1, d), lambda i, f: (0, 0)),
                  pl.BlockSpec((d, tf), lambda i, f: (0, f)),
                  pl.BlockSpec((d, tf), lambda i, f: (0, f)),
                  pl.BlockSpec((tf, d), lambda i, f: (f, 0))],
        out_specs=pl.BlockSpec((tm, d), lambda i, f: (i, 0)),
        out_shape=jax.ShapeDtypeStruct((m, d), F32),
        scratch_shapes=[pltpu.VMEM((tm, d), BF16), pltpu.VMEM((tm, d), F32)],
        compiler_params=_params("parallel", "arbitrary"),
    )(y, g.reshape(1, d), wg, wu, wd)


def _rope_tables(pos, head_dim):
    rot = head_dim // ROPE_FRAC
    half = rot // 2
    inv = jnp.exp(-math.log(ROPE_THETA) * jnp.arange(half, dtype=F32) * (2.0 / rot))
    ang = pos.astype(F32)[:, None] * inv[None, :]
    cos, sin = jnp.cos(ang), jnp.sin(ang)
    m = pos.shape[0]
    one = jnp.ones((m, head_dim - rot), F32)
    zero_r = jnp.zeros((m, head_dim - rot), F32)
    zero_h = jnp.zeros((m, half), F32)
    c = jnp.concatenate([cos, cos, one], axis=1)
    s_dn = jnp.concatenate([-sin, zero_h, zero_r], axis=1)
    s_up = jnp.concatenate([zero_h, sin, zero_r], axis=1)
    rep = LANES // head_dim
    return jnp.stack([jnp.tile(c, (1, rep)), jnp.tile(s_dn, (1, rep)), jnp.tile(s_up, (1, rep))])


def _rot(x, tab_ref, half):
    return (x * tab_ref[0] + pltpu.roll(x, LANES - half, 1) * tab_ref[1]
            + pltpu.roll(x, half, 1) * tab_ref[2])


def _rope_kernel(main_ref, tail_ref, tq_ref, ti_ref, kv_ref, t32_ref, ob_ref,
                 *, n_q, n_k, n_v, n_qi, wi_scale, q_scale):
    half_qk = C_HEAD_DIM // ROPE_FRAC // 2
    half_i = IDX_DIM // ROPE_FRAC // 2
    for c in range(n_q + n_k + n_v + n_qi):
        cs = slice(c * LANES, (c + 1) * LANES)
        x = main_ref[:, cs]
        if c < n_q + n_k:
            x = _rot(x, tq_ref, half_qk)
        elif c >= n_q + n_k + n_v:
            x = _rot(x, ti_ref, half_i)
        if n_q <= c < n_q + n_k + n_v:
            kv_ref[:, (c - n_q) * LANES:(c - n_q + 1) * LANES] = x
        ob_ref[:, cs] = (x * q_scale if c < n_q else x).astype(BF16)
    t = tail_ref[...]
    lane = lax.broadcasted_iota(jnp.int32, (1, LANES), 1)
    r = _rot(t, ti_ref, half_i)
    t32_ref[...] = jnp.where(lane < IDX_DIM, r, t * wi_scale)
    c = n_q + n_k + n_v + n_qi
    ob_ref[:, c * LANES:(c + 1) * LANES] = jnp.where(lane < IDX_DIM, r, pltpu.roll(r, IDX_DIM, 1)).astype(BF16)


def rope_all(main, tail, tab_qk, tab_idx, tm):
    m, nmain = main.shape
    n_q = C_HEADS * C_HEAD_DIM // LANES
    n_k = C_KV_HEADS * C_HEAD_DIM // LANES
    n_qi = IDX_HEADS * IDX_DIM // LANES
    wi_scale = (IDX_HEADS ** -0.5) * (IDX_DIM ** -0.5)
    q_scale = (C_HEAD_DIM ** -0.5) * math.log2(math.e)
    row = lambda i: (i, 0)
    return pl.pallas_call(
        functools.partial(_rope_kernel, n_q=n_q, n_k=n_k, n_v=n_k, n_qi=n_qi, wi_scale=wi_scale, q_scale=q_scale),
        grid=(m // tm,),
        in_specs=[pl.BlockSpec((tm, nmain), row),
                  pl.BlockSpec((tm, LANES), row),
                  pl.BlockSpec((3, tm, LANES), lambda i: (0, i, 0)),
                  pl.BlockSpec((3, tm, LANES), lambda i: (0, i, 0))],
        out_specs=[pl.BlockSpec((tm, 2 * n_k * LANES), row),
                   pl.BlockSpec((tm, LANES), row),
                   pl.BlockSpec((tm, nmain + LANES), row)],
        out_shape=[jax.ShapeDtypeStruct((m, 2 * n_k * LANES), F32),
                   jax.ShapeDtypeStruct((m, LANES), F32),
                   jax.ShapeDtypeStruct((m, nmain + LANES), BF16)],
        compiler_params=_params("parallel"),
    )(main, tail, tab_qk, tab_idx)


def _dsa_kernel(q_ref, qi_ref, wi_ref, kn_ref, vn_ref, kin_ref, ck_ref, cv_ref, ci_ref, o_ref,
                k_ref, v_ref, ki_ref, sc_ref, qs_ref, acc_ref, m_ref, l_ref,
                *, tq, kb, past, n_valid, q_pos0, topk):
    i = pl.program_id(1)
    rows = C_GROUPS * tq
    sub = kb // LANES
    kf = float(topk)

    new = past + tq
    ci = ci_ref[...].astype(BF16)
    k_ref[0:past, :] = ck_ref[...].astype(BF16)
    v_ref[0:past, :] = cv_ref[...].astype(BF16)
    ki_ref[0:past, :] = jnp.concatenate([ci, ci], axis=1)
    k_ref[past:new, :] = kn_ref[...]
    v_ref[past:new, :] = vn_ref[...]
    ki_ref[past:new, :] = kin_ref[...]
    n_pad = k_ref.shape[0] - new
    k_ref[new:, :] = jnp.zeros((n_pad, k_ref.shape[1]), BF16)
    v_ref[new:, :] = jnp.zeros((n_pad, v_ref.shape[1]), BF16)
    ki_ref[new:, :] = jnp.zeros((n_pad, ki_ref.shape[1]), BF16)

    q_pos = q_pos0 + i * tq + lax.broadcasted_iota(jnp.int32, (tq, 1), 0)
    q_chunk = q_pos // CHUNK
    last_chunk = (q_pos0 + i * tq + tq - 1) // CHUNK
    kv_limit = jnp.minimum(n_valid, (last_chunk + 1) * CHUNK)
    nkb = (kv_limit + kb - 1) // kb

    lane = lax.broadcasted_iota(jnp.int32, (1, LANES), 1)
    first = lane < IDX_DIM
    wi = wi_ref[:, IDX_DIM:IDX_DIM + IDX_HEADS]

    def score_block(b, carry):
        for c in range(sub):
            off = pl.multiple_of(b * kb + c * LANES, LANES)
            kib = ki_ref[pl.ds(off, LANES), :]
            acc = jnp.zeros((tq, LANES), F32)
            for hp in range(IDX_HEADS // 2):
                qp = qi_ref[:, hp * LANES:(hp + 1) * LANES]
                for half in range(2):
                    keep = first if half == 0 else jnp.logical_not(first)
                    qm = jnp.where(keep, qp, jnp.zeros_like(qp))
                    d = lax.dot_general(qm, kib, _NT, preferred_element_type=F32)
                    h = hp * 2 + half
                    acc = acc + jnp.maximum(d, 0.0) * wi[:, h:h + 1]
            k_pos = off + lane
            adm = jnp.logical_and(k_pos // CHUNK <= q_chunk, k_pos < n_valid)
            sc_ref[b, :, c * LANES:(c + 1) * LANES] = jnp.where(adm, acc, NEG)
        return carry

    lax.fori_loop(0, nkb, score_block, 0)

    def lane_sum(x):
        return jnp.sum(x, axis=1, keepdims=True)

    def count_ge(t):
        def body(b, acc):
            for c in range(sub):
                blk = sc_ref[b, :, c * LANES:(c + 1) * LANES]
                acc = acc + jnp.where(blk >= t, 1.0, 0.0)
            return acc
        return lane_sum(lax.fori_loop(0, nkb, body, jnp.zeros((tq, LANES), F32)))

    def stats(b, carry):
        mx, mn, cnt = carry
        for c in range(sub):
            blk = sc_ref[b, :, c * LANES:(c + 1) * LANES]
            ok = blk > 0.5 * NEG
            mx = jnp.maximum(mx, blk)
            mn = jnp.minimum(mn, jnp.where(ok, blk, BIG))
            cnt = cnt + jnp.where(ok, 1.0, 0.0)
        return mx, mn, cnt

    mx, mn, cnt = lax.fori_loop(
        0, nkb, stats,
        (jnp.full((tq, LANES), NEG, F32), jnp.full((tq, LANES), BIG, F32), jnp.zeros((tq, LANES), F32)))
    row_max = jnp.max(mx, axis=1, keepdims=True)
    row_min = jnp.min(mn, axis=1, keepdims=True)
    n_adm = lane_sum(cnt)

    done0 = jnp.where(n_adm <= kf, 1.0, 0.0)
    state0 = (row_min, jnp.full((tq, 1), BIG, F32), row_max, jnp.full((tq, 1), 0.5 * NEG, F32), done0)

    def bisect(state, n_steps):
        def cond(c):
            it, st = c
            return jnp.logical_and(it < n_steps, jnp.min(st[4]) < 0.5)

        def body(c):
            it, (lo, hi, mid, thr, done) = c
            cnt = count_ge(mid)
            live = done < 0.5
            hit = jnp.logical_and(live, cnt == kf)
            ge = cnt >= kf
            thr = jnp.where(hit, mid, thr)
            done = jnp.where(hit, 1.0, done)
            lo = jnp.where(ge, mid, lo)
            hi = jnp.where(ge, hi, mid)
            return it + 1, (lo, hi, 0.5 * (lo + hi), thr, done)

        return lax.while_loop(cond, body, (jnp.int32(0), state))[1]

    def snap(state):
        lo, hi, mid, thr, done = state

        def body(b, carry):
            v_lo, v_hi = carry
            for c in range(sub):
                blk = sc_ref[b, :, c * LANES:(c + 1) * LANES]
                v_lo = jnp.minimum(v_lo, jnp.where(blk >= lo, blk, BIG))
                v_hi = jnp.maximum(v_hi, jnp.where(blk < hi, blk, NEG))
            return v_lo, v_hi

        v_lo, v_hi = lax.fori_loop(0, nkb, body,
                                   (jnp.full((tq, LANES), BIG, F32), jnp.full((tq, LANES), NEG, F32)))
        v_lo = jnp.min(v_lo, axis=1, keepdims=True)
        v_hi = jnp.max(v_hi, axis=1, keepdims=True)
        live = done < 0.5
        tie = jnp.logical_and(live, v_lo == v_hi)
        thr = jnp.where(tie, v_lo, thr)
        done = jnp.where(tie, 1.0, done)
        lo = jnp.where(live, v_lo, lo)
        return (lo, hi, 0.5 * (lo + hi), thr, done), jnp.where(tie, 1.0, 0.0)

    state = bisect(state0, 32)

    def refine_cond(c):
        rounds, st, _ = c
        return jnp.logical_and(rounds < 10, jnp.min(st[4]) < 0.5)

    def refine_body(c):
        rounds, st, tie = c
        st, new_tie = snap(st)
        st = bisect(st, 32)
        return rounds + 1, st, jnp.maximum(tie, new_tie)

    _, state, tie = lax.while_loop(refine_cond, refine_body,
                                   (jnp.int32(0), state, jnp.zeros((tq, 1), F32)))
    thr = state[3]
    any_tie = jnp.max(tie) > 0.5

    @pl.when(jnp.logical_not(any_tie))
    def _():
        def body(b, carry):
            for c in range(sub):
                cs = slice(c * LANES, (c + 1) * LANES)
                sc_ref[b, :, cs] = jnp.where(sc_ref[b, :, cs] >= thr, 0.0, NEG)
            return carry
        lax.fori_loop(0, nkb, body, 0)

    @pl.when(any_tie)
    def _():
        def gt_body(b, acc):
            for c in range(sub):
                acc = acc + jnp.where(sc_ref[b, :, c * LANES:(c + 1) * LANES] > thr, 1.0, 0.0)
            return acc
        need = kf - lane_sum(lax.fori_loop(0, nkb, gt_body, jnp.zeros((tq, LANES), F32)))

        def count_eq_upto(j):
            def body(b, acc):
                for c in range(sub):
                    blk = sc_ref[b, :, c * LANES:(c + 1) * LANES]
                    idx = (b * kb + c * LANES + lane).astype(F32)
                    acc = acc + jnp.where(jnp.logical_and(blk == thr, idx <= j), 1.0, 0.0)
                return acc
            return lane_sum(lax.fori_loop(0, nkb, body, jnp.zeros((tq, LANES), F32)))

        def idx_body(_, c):
            lo_j, hi_j = c
            mid_j = jnp.floor(0.5 * (lo_j + hi_j))
            ok = count_eq_upto(mid_j) >= need
            return jnp.where(ok, lo_j, mid_j), jnp.where(ok, mid_j, hi_j)

        n_steps = max(1, math.ceil(math.log2(sc_ref.shape[0] * kb + 1)))
        _, last = lax.fori_loop(
            0, n_steps, idx_body,
            (jnp.full((tq, 1), -1.0, F32), jnp.full((tq, 1), float(sc_ref.shape[0] * kb), F32)))
        last = jnp.where(tie > 0.5, last, BIG)

        def body(b, carry):
            for c in range(sub):
                cs = slice(c * LANES, (c + 1) * LANES)
                blk = sc_ref[b, :, cs]
                idx = (b * kb + c * LANES + lane).astype(F32)
                sel = jnp.logical_or(blk > thr, jnp.logical_and(blk == thr, idx <= last))
                sc_ref[b, :, cs] = jnp.where(sel, 0.0, NEG)
            return carry
        lax.fori_loop(0, nkb, body, 0)

    for g in range(C_KV_HEADS):
        for hh in range(C_GROUPS):
            h = g * C_GROUPS + hh
            qs_ref[g, hh * tq:(hh + 1) * tq, :] = q_ref[:, h * C_HEAD_DIM:(h + 1) * C_HEAD_DIM]
    m_ref[...] = jnp.full(m_ref.shape, NEG, F32)
    l_ref[...] = jnp.zeros(l_ref.shape, F32)
    acc_ref[...] = jnp.zeros(acc_ref.shape, F32)

    def attend(b, carry):
        off = pl.multiple_of(b * kb, kb)
        bias = sc_ref[b]
        bias = jnp.concatenate([bias] * C_GROUPS, axis=0)
        for g in range(C_KV_HEADS):
            cs = slice(g * C_HEAD_DIM, (g + 1) * C_HEAD_DIM)
            kblk = k_ref[pl.ds(off, kb), cs]
            vblk = v_ref[pl.ds(off, kb), cs]
            s = lax.dot_general(qs_ref[g], kblk, _NT, preferred_element_type=F32)
            s = s + bias
            m_old = m_ref[g]
            m_new = jnp.maximum(m_old, jnp.max(s, axis=1, keepdims=True))
            alpha = jnp.exp2(m_old - m_new)
            p = jnp.exp2(s - m_new)
            l_ref[g] = alpha * l_ref[g] + jnp.sum(p, axis=1, keepdims=True)
            acc_ref[g] = alpha * acc_ref[g] + jnp.dot(p.astype(BF16), vblk, preferred_element_type=F32)
            m_ref[g] = m_new
        return carry

    lax.fori_loop(0, nkb, attend, 0)

    for g in range(C_KV_HEADS):
        o = acc_ref[g] / l_ref[g]
        for hh in range(C_GROUPS):
            h = g * C_GROUPS + hh
            o_ref[:, h * C_HEAD_DIM:(h + 1) * C_HEAD_DIM] = o[hh * tq:(hh + 1) * tq, :].astype(o_ref.dtype)


def dsa_sample(ob, t32, cache_k, cache_v, cache_i, *, row0, kb, topk):
    nb, past, n_kv = cache_k.shape
    ds = (ob.shape[0] - row0) // nb
    n_q = C_HEADS * C_HEAD_DIM
    n_qi = IDX_HEADS * IDX_DIM
    s_all = past + ds
    s_pad = -(-s_all // kb) * kb
    rows = C_GROUPS * ds
    col = lambda w, off: (lambda b, i: (row0 // ds + b, off // w))
    cmap = lambda b, i: (b, 0, 0)
    return pl.pallas_call(
        functools.partial(_dsa_kernel, tq=ds, kb=kb, past=past, n_valid=s_all, q_pos0=past, topk=topk),
        grid=(nb, 1),
        in_specs=[pl.BlockSpec((ds, n_q), col(n_q, 0)),
                  pl.BlockSpec((ds, n_qi), col(n_qi, n_q + 2 * n_kv)),
                  pl.BlockSpec((ds, LANES), col(LANES, 0)),
                  pl.BlockSpec((ds, n_kv), col(n_kv, n_q)),
                  pl.BlockSpec((ds, n_kv), col(n_kv, n_q + n_kv)),
                  pl.BlockSpec((ds, LANES), col(LANES, n_q + 2 * n_kv + n_qi)),
                  pl.BlockSpec((None, past, n_kv), cmap),
                  pl.BlockSpec((None, past, n_kv), cmap),
                  pl.BlockSpec((None, past, IDX_DIM), cmap)],
        out_specs=pl.BlockSpec((ds, n_q), lambda b, i: (b, 0)),
        out_shape=jax.ShapeDtypeStruct((nb * ds, n_q), BF16),
        scratch_shapes=[pltpu.VMEM((s_pad, n_kv), BF16),
                        pltpu.VMEM((s_pad, n_kv), BF16),
                        pltpu.VMEM((s_pad, LANES), BF16),
                        pltpu.VMEM((s_pad // kb, ds, kb), F32),
                        pltpu.VMEM((C_KV_HEADS, rows, C_HEAD_DIM), BF16),
                        pltpu.VMEM((C_KV_HEADS, rows, C_HEAD_DIM), F32),
                        pltpu.VMEM((C_KV_HEADS, rows, 1), F32),
                        pltpu.VMEM((C_KV_HEADS, rows, 1), F32)],
        compiler_params=_params("parallel", "arbitrary"),
    )(ob, ob, t32, ob, ob, ob, cache_k, cache_v, cache_i)


def _dsa_t_kernel(q_ref, qi_ref, wi_ref, k_ref, vt_ref, ki_ref, o_ref,
                  sc_ref, qim_ref, acc_ref, m_ref, l_ref,
                  *, tq, kb, pb, cb, n_valid, q_pos0, topk):
    i = pl.program_id(1)
    kf = float(topk)

    q_pos = q_pos0 + i * tq + lax.broadcasted_iota(jnp.int32, (1, tq), 1)
    q_chunk = q_pos // CHUNK
    last_chunk = (q_pos0 + i * tq + tq - 1) // CHUNK
    kv_limit = jnp.minimum(n_valid, (last_chunk + 1) * CHUNK)
    nkb = (kv_limit + kb - 1) // kb
    ncb = (kv_limit + cb - 1) // cb

    def col_reduce(x, op):
        groups = x.shape[0] // 8
        chains = 8 if groups % 8 == 0 else 1
        return op(op(x.reshape(chains, groups // chains, 8, tq), axis=1), axis=0)

    col_sum = lambda x: col_reduce(x, jnp.sum)
    col_max = lambda x: col_reduce(x, jnp.max)
    col_min = lambda x: col_reduce(x, jnp.min)

    lane = lax.broadcasted_iota(jnp.int32, (1, LANES), 1)
    first = lane < IDX_DIM
    for hp in range(IDX_HEADS // 2):
        qp = qi_ref[:, hp * LANES:(hp + 1) * LANES]
        qim_ref[2 * hp] = jnp.where(first, qp, jnp.zeros_like(qp))
        qim_ref[2 * hp + 1] = jnp.where(first, jnp.zeros_like(qp), qp)
    block_iota = lax.broadcasted_iota(jnp.int32, (pb, 1), 0)

    def score_block(b, carry):
        off = pl.multiple_of(b * pb, pb)
        kib = ki_ref[pl.ds(off, pb), :]
        acc = jnp.zeros((pb, tq), F32)
        for h in range(IDX_HEADS):
            d = lax.dot_general(kib, qim_ref[h], _NT, preferred_element_type=F32)
            acc = acc + jnp.maximum(d, 0.0) * wi_ref[h:h + 1, :]
        k_pos = off + block_iota
        adm = jnp.logical_and(k_pos // CHUNK <= q_chunk, k_pos < n_valid)
        sc_ref[pl.ds(off, pb), :] = jnp.where(adm, acc, NEG)
        return carry

    lax.fori_loop(0, ncb * (cb // pb), score_block, 0)
    key_iota = lax.broadcasted_iota(jnp.int32, (cb, 1), 0)

    def blocks(body, init):
        def step(b, carry):
            off = pl.multiple_of(b * cb, cb)
            return body(off, sc_ref[pl.ds(off, cb), :], carry)
        return lax.fori_loop(0, ncb, step, init)

    def total(x):
        return jnp.sum(x, axis=0, keepdims=True)

    def count_ge(t):
        return total(blocks(lambda off, blk, acc: acc + col_sum(jnp.where(blk >= t, 1.0, 0.0)),
                            jnp.zeros((8, tq), F32)))

    def stats(off, blk, carry):
        mx, mn, cnt = carry
        ok = blk > 0.5 * NEG
        return (jnp.maximum(mx, col_max(blk)), jnp.minimum(mn, col_min(jnp.where(ok, blk, BIG))),
                cnt + col_sum(jnp.where(ok, 1.0, 0.0)))

    mx, mn, cnt = blocks(stats, (jnp.full((8, tq), NEG, F32), jnp.full((8, tq), BIG, F32),
                                 jnp.zeros((8, tq), F32)))
    row_max = jnp.max(mx, axis=0, keepdims=True)
    row_min = jnp.min(mn, axis=0, keepdims=True)
    n_adm = total(cnt)

    done0 = jnp.where(n_adm <= kf, 1.0, 0.0)
    state0 = (row_min, jnp.full((1, tq), BIG, F32), row_max, jnp.full((1, tq), 0.5 * NEG, F32), done0)

    def bisect(state, n_steps):
        def cond(c):
            it, st = c
            return jnp.logical_and(it < n_steps, jnp.min(st[4]) < 0.5)

        def body(c):
            it, (lo, hi, mid, thr, done) = c
            cnt = count_ge(mid)
            hit = jnp.logical_and(done < 0.5, cnt == kf)
            ge = cnt >= kf
            thr = jnp.where(hit, mid, thr)
            done = jnp.where(hit, 1.0, done)
            lo = jnp.where(ge, mid, lo)
            hi = jnp.where(ge, hi, mid)
            return it + 1, (lo, hi, 0.5 * (lo + hi), thr, done)

        return lax.while_loop(cond, body, (jnp.int32(0), state))[1]

    def snap(state):
        lo, hi, mid, thr, done = state

        def body(off, blk, carry):
            v_lo, v_hi = carry
            return (jnp.minimum(v_lo, col_min(jnp.where(blk >= lo, blk, BIG))),
                    jnp.maximum(v_hi, col_max(jnp.where(blk < hi, blk, NEG))))

        v_lo, v_hi = blocks(body, (jnp.full((8, tq), BIG, F32), jnp.full((8, tq), NEG, F32)))
        v_lo = jnp.min(v_lo, axis=0, keepdims=True)
        v_hi = jnp.max(v_hi, axis=0, keepdims=True)
        live = done < 0.5
        tie = jnp.logical_and(live, v_lo == v_hi)
        thr = jnp.where(tie, v_lo, thr)
        done = jnp.where(tie, 1.0, done)
        lo = jnp.where(live, v_lo, lo)
        return (lo, hi, 0.5 * (lo + hi), thr, done), jnp.where(tie, 1.0, 0.0)

    state = bisect(state0, 32)

    def refine_cond(c):
        rounds, st, _ = c
        return jnp.logical_and(rounds < 10, jnp.min(st[4]) < 0.5)

    def refine_body(c):
        rounds, st, tie = c
        st, new_tie = snap(st)
        st = bisect(st, 32)
        return rounds + 1, st, jnp.maximum(tie, new_tie)

    _, state, tie = lax.while_loop(refine_cond, refine_body,
                                   (jnp.int32(0), state, jnp.zeros((1, tq), F32)))
    thr = state[3]
    any_tie = jnp.max(tie) > 0.5

    @pl.when(jnp.logical_not(any_tie))
    def _():
        def body(off, blk, carry):
            sc_ref[pl.ds(off, cb), :] = jnp.where(blk >= thr, 0.0, NEG)
            return carry
        blocks(body, 0)

    @pl.when(any_tie)
    def _():
        need = kf - total(blocks(lambda off, blk, acc: acc + col_sum(jnp.where(blk > thr, 1.0, 0.0)),
                                 jnp.zeros((8, tq), F32)))

        def count_eq_upto(j):
            def body(off, blk, acc):
                idx = (off + key_iota).astype(F32)
                return acc + col_sum(jnp.where(jnp.logical_and(blk == thr, idx <= j), 1.0, 0.0))
            return total(blocks(body, jnp.zeros((8, tq), F32)))

        def idx_body(_, c):
            lo_j, hi_j = c
            mid_j = jnp.floor(0.5 * (lo_j + hi_j))
            ok = count_eq_upto(mid_j) >= need
            return jnp.where(ok, lo_j, mid_j), jnp.where(ok, mid_j, hi_j)

        n_keys = sc_ref.shape[0]
        n_steps = max(1, math.ceil(math.log2(n_keys + 1)))
        _, last = lax.fori_loop(0, n_steps, idx_body,
                                (jnp.full((1, tq), -1.0, F32), jnp.full((1, tq), float(n_keys), F32)))
        last = jnp.where(tie > 0.5, last, BIG)

        def body(off, blk, carry):
            idx = (off + key_iota).astype(F32)
            sel = jnp.logical_or(blk > thr, jnp.logical_and(blk == thr, idx <= last))
            sc_ref[pl.ds(off, cb), :] = jnp.where(sel, 0.0, NEG)
            return carry
        blocks(body, 0)

    m_ref[...] = jnp.full(m_ref.shape, NEG, F32)
    l_ref[...] = jnp.zeros(l_ref.shape, F32)
    acc_ref[...] = jnp.zeros(acc_ref.shape, F32)

    def attend(b, carry):
        off = pl.multiple_of(b * kb, kb)
        for h in range(C_HEADS):
            g = h // C_GROUPS
            gs = slice(g * C_HEAD_DIM, (g + 1) * C_HEAD_DIM)
            s = lax.dot_general(k_ref[pl.ds(off, kb), gs], q_ref[:, h * C_HEAD_DIM:(h + 1) * C_HEAD_DIM],
                                _NT, preferred_element_type=F32)
            s = s + sc_ref[pl.ds(off, kb), :]
            m_old = m_ref[h]
            m_new = jnp.maximum(m_old, jnp.max(col_max(s), axis=0, keepdims=True))
            alpha = jnp.exp2(m_old - m_new)
            p = jnp.exp2(s - m_new)
            l_ref[h] = alpha * l_ref[h] + total(col_sum(p))
            acc_ref[h] = alpha * acc_ref[h] + jnp.dot(vt_ref[b, gs, :], p.astype(BF16),
                                                      preferred_element_type=F32)
            m_ref[h] = m_new
        return carry

    lax.fori_loop(0, nkb, attend, 0)

    for h in range(C_HEADS):
        o = acc_ref[h] / l_ref[h]
        o_ref[:, h * C_HEAD_DIM:(h + 1) * C_HEAD_DIM] = o.T.astype(o_ref.dtype)


def dsa_t(ob, wi_t, vt, *, nq, tq, kb, pb, cb, topk):
    dq = C_HEADS * C_HEAD_DIM
    n_kv = C_KV_HEADS * C_HEAD_DIM
    n_qi = IDX_HEADS * IDX_DIM
    s_pad = nq
    return pl.pallas_call(
        functools.partial(_dsa_t_kernel, tq=tq, kb=kb, pb=pb, cb=cb, n_valid=nq, q_pos0=0, topk=topk),
        grid=(1, nq // tq),
        in_specs=[pl.BlockSpec((tq, dq), lambda b, i: (i, 0)),
                  pl.BlockSpec((tq, n_qi), lambda b, i: (i, (dq + 2 * n_kv) // n_qi)),
                  pl.BlockSpec((IDX_HEADS, tq), lambda b, i: (0, i)),
                  pl.BlockSpec((nq, n_kv), lambda b, i: (0, dq // n_kv)),
                  pl.BlockSpec(vt.shape, lambda b, i: (0, 0, 0)),
                  pl.BlockSpec((nq, LANES), lambda b, i: (0, (dq + 2 * n_kv + n_qi) // LANES))],
        out_specs=pl.BlockSpec((tq, dq), lambda b, i: (i, 0)),
        out_shape=jax.ShapeDtypeStruct((nq, dq), BF16),
        scratch_shapes=[pltpu.VMEM((s_pad, tq), F32),
                        pltpu.VMEM((IDX_HEADS, tq, LANES), BF16),
                        pltpu.VMEM((C_HEADS, C_HEAD_DIM, tq), F32),
                        pltpu.VMEM((C_HEADS, 1, tq), F32),
                        pltpu.VMEM((C_HEADS, 1, tq), F32)],
        compiler_params=_params("arbitrary", "arbitrary"),
    )(ob, ob, wi_t, ob, vt, ob)


def _router_kernel(y_ref, g_ref, wr_ref, h_ref, gate_ref):
    h = _rms(y_ref[...], g_ref[...])
    h_ref[...] = h.astype(BF16)
    logits = jnp.dot(h, wr_ref[...], preferred_element_type=F32, precision=lax.Precision.HIGHEST)
    lane = lax.broadcasted_iota(jnp.int32, logits.shape, 1)
    lg = jnp.where(lane < N_EXPERTS, logits, NEG)
    m1 = jnp.max(lg, axis=1, keepdims=True)
    i1 = jnp.min(jnp.where(lg == m1, lane, LANES), axis=1, keepdims=True)
    lg2 = jnp.where(lane == i1, NEG, lg)
    m2 = jnp.max(lg2, axis=1, keepdims=True)
    i2 = jnp.min(jnp.where(lg2 == m2, lane, LANES), axis=1, keepdims=True)
    e = jnp.exp(m2 - m1)
    g1 = 1.0 / (1.0 + e)
    g2 = e / (1.0 + e)
    meta = jnp.where(lane == 0, i1.astype(F32), jnp.where(lane == 1, i2.astype(F32),
                     jnp.where(lane == 2, g1, jnp.where(lane == 3, g2, 0.0))))
    gate_ref[...] = meta


def router(y, g, w_router_pad, tm):
    m, d = y.shape
    return pl.pallas_call(
        _router_kernel,
        grid=(m // tm,),
        in_specs=[pl.BlockSpec((tm, d), lambda i: (i, 0)),
                  pl.BlockSpec((1, d), lambda i: (0, 0)),
                  pl.BlockSpec((d, LANES), lambda i: (0, 0))],
        out_specs=[pl.BlockSpec((tm, d), lambda i: (i, 0)),
                   pl.BlockSpec((tm, LANES), lambda i: (i, 0))],
        out_shape=[jax.ShapeDtypeStruct((m, d), BF16), jax.ShapeDtypeStruct((m, LANES), F32)],
        compiler_params=_params("parallel"),
    )(y, g.reshape(1, d), w_router_pad)


def route_plan(meta, sup):
    m = meta.shape[0]
    n_tiles = (2 * m) // sup + N_EXPERTS
    e_all = jnp.concatenate([meta[:, 0], meta[:, 1]]).astype(jnp.int32)
    onehot = (e_all[:, None] == jnp.arange(N_EXPERTS)[None, :]).astype(jnp.int32)
    rank = jnp.sum((jnp.cumsum(onehot, axis=0) - onehot) * onehot, axis=1)
    counts = jnp.sum(onehot, axis=0)
    n_super = (counts + sup - 1) // sup
    super_end = jnp.cumsum(n_super)
    super_start = super_end - n_super
    pos = (super_start * sup)[e_all] + rank
    tiles = jnp.arange(n_tiles)
    used = super_end[-1]
    t_eff = jnp.minimum(tiles, used - 1)
    tile_expert = jnp.minimum(jnp.searchsorted(super_end, t_eff, side="right"), N_EXPERTS - 1).astype(jnp.int32)
    rows = jnp.clip(counts[tile_expert] - (t_eff - super_start[tile_expert]) * sup, 0, sup)
    tile_rows = jnp.where(tiles < used, rows, 0).astype(jnp.int32)
    token = jnp.concatenate([jnp.arange(m), jnp.arange(m)]).astype(jnp.int32)
    row_token = (jnp.arange(n_tiles * sup, dtype=jnp.int32) % m).at[pos].set(token)
    return pos.astype(jnp.int32), row_token, tile_expert, tile_rows


def _gather_rows_kernel(tok_ref, h_ref, o_ref, sem, *, batch):
    base = pl.program_id(0) * batch

    def issue(j, carry):
        pltpu.make_async_copy(h_ref.at[tok_ref[base + j]], o_ref.at[j], sem).start()
        return carry

    lax.fori_loop(0, batch, issue, 0)
    pltpu.make_async_copy(o_ref, o_ref, sem).wait()


def gather_rows(h3, row_token, batch):
    n_rows = row_token.shape[0]
    blk = (batch,) + h3.shape[1:]
    return pl.pallas_call(
        functools.partial(_gather_rows_kernel, batch=batch),
        grid_spec=pltpu.PrefetchScalarGridSpec(
            num_scalar_prefetch=1,
            grid=(n_rows // batch,),
            in_specs=[pl.BlockSpec(memory_space=pl.ANY)],
            out_specs=pl.BlockSpec(blk, lambda i, tok: (i, 0, 0)),
            scratch_shapes=[pltpu.SemaphoreType.DMA(())]),
        out_shape=jax.ShapeDtypeStruct((n_rows,) + h3.shape[1:], h3.dtype),
        compiler_params=_params("arbitrary"),
    )(row_token, h3)


def _grouped_ffn_kernel(te_ref, tr_ref, x_ref, wg_ref, wu_ref, wd_ref, o_ref,
                        wgb_ref, wub_ref, wdb_ref, *, sup, sub):
    t = pl.program_id(0)
    f = pl.program_id(1)
    rows = tr_ref[t]

    @pl.when(rows > 0)
    def _():
        wgb_ref[...] = wg_ref[...].astype(BF16)
        wub_ref[...] = wu_ref[...].astype(BF16)
        wdb_ref[...] = wd_ref[...].astype(BF16)

    for s in range(sup // sub):
        sl = slice(s * sub, (s + 1) * sub)

        @pl.when(jnp.logical_and(s * sub >= rows, f == 0))
        def _():
            o_ref[sl, :] = jnp.zeros((sub, o_ref.shape[1]), F32)

        @pl.when(s * sub < rows)
        def _():
            part = _swiglu_tile(x_ref[sl, :], wgb_ref[...], wub_ref[...], wdb_ref[...])

            @pl.when(f == 0)
            def _():
                o_ref[sl, :] = part

            @pl.when(f > 0)
            def _():
                o_ref[sl, :] += part


def grouped_ffn(xs, tile_expert, tile_rows, wg, wu, wd, sup, sub, tf):
    n_rows, d = xs.shape
    ff = wg.shape[2]
    n_f = ff // tf
    fidx = lambda t, f, te, tr: jnp.where(tr[t] > 0, f, n_f - 1)
    return pl.pallas_call(
        functools.partial(_grouped_ffn_kernel, sup=sup, sub=sub),
        grid_spec=pltpu.PrefetchScalarGridSpec(
            num_scalar_prefetch=2,
            grid=(n_rows // sup, n_f),
            in_specs=[pl.BlockSpec((sup, d), lambda t, f, te, tr: (t, 0), pipeline_mode=pl.Buffered(1)),
                      pl.BlockSpec((None, d, tf), lambda t, f, te, tr: (te[t], 0, fidx(t, f, te, tr))),
                      pl.BlockSpec((None, d, tf), lambda t, f, te, tr: (te[t], 0, fidx(t, f, te, tr))),
                      pl.BlockSpec((None, tf, d), lambda t, f, te, tr: (te[t], fidx(t, f, te, tr), 0))],
            out_specs=pl.BlockSpec((sup, d), lambda t, f, te, tr: (t, 0), pipeline_mode=pl.Buffered(1)),
            scratch_shapes=[pltpu.VMEM((d, tf), BF16), pltpu.VMEM((d, tf), BF16), pltpu.VMEM((tf, d), BF16)]),
        out_shape=jax.ShapeDtypeStruct((n_rows, d), F32),
        compiler_params=_params("arbitrary", "arbitrary"),
    )(tile_expert, tile_rows, xs, wg, wu, wd)


def _combine_kernel(pos_ref, y_ref, g1_ref, g2_ref, g_ref, ys_ref, o_ref, buf_ref, sem, *, n_tok, tmc):
    base = pl.program_id(0) * tmc

    def issue(j, carry):
        t = base + j
        pltpu.make_async_copy(ys_ref.at[pos_ref[t]], buf_ref.at[0, j], sem).start()
        pltpu.make_async_copy(ys_ref.at[pos_ref[n_tok + t]], buf_ref.at[1, j], sem).start()
        return carry

    lax.fori_loop(0, tmc, issue, 0)
    pltpu.make_async_copy(buf_ref.at[0], buf_ref.at[0], sem).wait()
    pltpu.make_async_copy(buf_ref.at[1], buf_ref.at[1], sem).wait()
    x = y_ref[...] + (buf_ref[0] * g1_ref[...] + buf_ref[1] * g2_ref[...])
    ms = jnp.sum(jnp.sum(x * x, axis=2, keepdims=True), axis=1, keepdims=True) / (x.shape[1] * x.shape[2])
    o_ref[...] = x * lax.rsqrt(ms + NORM_EPS) * g_ref[...]


def combine_final(y3, gate1, gate2, g_final3, ys3, pos, tmc):
    m = y3.shape[0]
    blk = (tmc,) + y3.shape[1:]
    gblk = (tmc, 1, y3.shape[2])
    return pl.pallas_call(
        functools.partial(_combine_kernel, n_tok=m, tmc=tmc),
        grid_spec=pltpu.PrefetchScalarGridSpec(
            num_scalar_prefetch=1,
            grid=(m // tmc,),
            in_specs=[pl.BlockSpec(blk, lambda i, pos: (i, 0, 0)),
                      pl.BlockSpec(gblk, lambda i, pos: (i, 0, 0)),
                      pl.BlockSpec(gblk, lambda i, pos: (i, 0, 0)),
                      pl.BlockSpec((1,) + y3.shape[1:], lambda i, pos: (0, 0, 0)),
                      pl.BlockSpec(memory_space=pl.ANY)],
            out_specs=pl.BlockSpec(blk, lambda i, pos: (i, 0, 0)),
            scratch_shapes=[pltpu.VMEM((2,) + blk, F32), pltpu.SemaphoreType.DMA(())]),
        out_shape=jax.ShapeDtypeStruct(y3.shape, F32),
        compiler_params=_params("arbitrary"),
    )(pos, y3, gate1, gate2, g_final3, ys3)


def kernel(x_prompt, x_sample, cache_a_k, cache_a_v, state_pool, cache_c_k, cache_c_v, cache_c_idx,
           norm_mix, norm_ffn, norm_final, w_in_even, w_out_even, a_rel_bias, pool_w, pool_scale,
           ffn_w_gate, ffn_w_up, ffn_w_down, w_in_odd, w_out_odd,
           moe_router, moe_w_gate, moe_w_up, moe_w_down):
    nbp, lp, d = x_prompt.shape
    nb, ds, _ = x_sample.shape
    past = cache_c_k.shape[2]
    a_len = cache_a_k.shape[2]
    assert nbp == 1 and lp % 512 == 0 and (nb * ds) % 512 == 0 and ds == POOL_HALO and past >= POOL_HALO
    ns = nb * ds
    m = lp + ns
    tm = 512
    bf = lambda t: t.astype(BF16)

    x = jnp.concatenate([x_prompt.reshape(lp, d), x_sample.reshape(ns, d)], axis=0)

    proj0 = norm_proj(x, norm_mix[0], bf(w_in_even[0]), tm, 1024)
    k0 = proj0[:, A_WIDTH:2 * A_WIDTH]
    v0 = proj0[:, 2 * A_WIDTH:3 * A_WIDTH]
    u0 = proj0[:, 3 * A_WIDTH:]

    pad = A_PREV_CHUNKS * CHUNK
    bias_p = _rel_bias_tile(a_rel_bias[0], 0, CHUNK, -pad, A_BAND)
    a_p = band_prompt(proj0, _pair_rows(bias_p), lp)

    k_pos = past - a_len + jnp.arange(a_len + ds)
    q_pos = past + jnp.arange(ds)
    qch, kch = q_pos // CHUNK, k_pos // CHUNK
    ok = ((k_pos[None, :] >= 0) & (kch[None, :] <= qch[:, None])
          & (kch[None, :] >= qch[:, None] - A_PREV_CHUNKS))
    bias_s = jnp.where(ok[None], _rel_bias_tile(a_rel_bias[0], past, ds, past - a_len, a_len + ds), NEG)
    a_s = band_sample(proj0, cache_a_k[0].reshape(nb, a_len, A_WIDTH),
                      cache_a_v[0].reshape(nb, a_len, A_WIDTH), _pair_rows(bias_s), lp, nb, ds)

    u_s = u0[lp:].reshape(nb, ds, B_WIDTH)
    u_hist = jnp.concatenate([state_pool[0], u_s], axis=1)
    u_ext = jnp.concatenate([jnp.zeros((nb, POOL_HALO - B_HIST, B_WIDTH), F32), u_hist], axis=1)
    pw = bf(pool_w[0])
    ps = pool_scale[0].reshape(1, B_WIDTH)
    p_p = pool_prompt(proj0, pw, ps, lp, tm)
    p_s = pool_sample(u_ext, pw, ps, past)

    a = jnp.concatenate([a_p, a_s], axis=0)
    p = jnp.concatenate([p_p, p_s], axis=0)
    wo = bf(w_out_even[0])
    y = mm_res([a, p], [wo[:A_WIDTH], wo[A_WIDTH:]], x, tm, 1024)
    y = ffn(y, norm_ffn[0], bf(ffn_w_gate[0]), bf(ffn_w_up[0]), bf(ffn_w_down[0]), tm, 512)

    n_q = C_HEADS * C_HEAD_DIM
    n_kv = C_KV_HEADS * C_HEAD_DIM
    n_qi = IDX_HEADS * IDX_DIM
    n_main = n_q + 2 * n_kv + n_qi
    w1 = w_in_odd[0]
    w_tail = jnp.pad(w1[:, n_main:], ((0, 0), (0, LANES - (w1.shape[1] - n_main))))
    main = norm_proj(y, norm_mix[1], bf(w1[:, :n_main]), tm, 512)
    tail = norm_proj(y, norm_mix[1], bf(w_tail), tm, LANES)
    pos = jnp.concatenate([jnp.arange(lp), jnp.tile(past + jnp.arange(ds), nb)])
    kv1, t32, ob = rope_all(main, tail, _rope_tables(pos, C_HEAD_DIM), _rope_tables(pos, IDX_DIM), tm)
    k1 = kv1[:, :n_kv]
    v1 = kv1[:, n_kv:]
    ki1 = t32[:, :IDX_DIM]

    kb_p = 128
    vt = ob[:lp, n_q + n_kv:n_q + 2 * n_kv].reshape(lp // kb_p, kb_p, n_kv).transpose(0, 2, 1)
    wi_t = t32[:lp, IDX_DIM:IDX_DIM + IDX_HEADS].T
    o_p = dsa_t(ob, wi_t, vt, nq=lp, tq=256, kb=kb_p, pb=256, cb=1024, topk=min(TOPK_MAX, lp // 4))
    o_s = dsa_sample(ob, t32, cache_c_k[0].reshape(nb, past, n_kv), cache_c_v[0].reshape(nb, past, n_kv),
                     cache_c_idx[0], row0=lp, kb=3 * LANES, topk=min(TOPK_MAX, (past + ds) // 4))

    o = jnp.concatenate([o_p, o_s], axis=0)
    y = mm_res([o], [bf(w_out_odd[0])], y, tm, 1024)

    wr = jnp.pad(moe_router[0], ((0, 0), (0, LANES - N_EXPERTS)))
    h, meta = router(y, norm_ffn[1], wr, tm)
    sup = 1024
    pos_rows, row_token, tile_expert, tile_rows = route_plan(meta, sup)
    n_rows = row_token.shape[0]
    slab = (d // LANES, LANES)
    xs = gather_rows(h.reshape((m,) + slab), row_token, 512)
    ys = grouped_ffn(xs.reshape(n_rows, d), tile_expert, tile_rows,
                     moe_w_gate[0], moe_w_up[0], moe_w_down[0], sup, 256, 512)
    lanes_of = lambda col: jnp.broadcast_to(meta[:, col][:, None, None], (m, 1, LANES))
    y = combine_final(y.reshape((m,) + slab), lanes_of(2), lanes_of(3), norm_final.reshape((1,) + slab),
                      ys.reshape((n_rows,) + slab), pos_rows, 256).reshape(m, d)

    y_prompt = y[:lp].reshape(1, lp, d)
    y_sample = y[lp:].reshape(nb, ds, d)
    keep = min(A_BAND, lp)
    heads = lambda t, n: t.reshape(1, n, -1, A_HEADS, A_HEAD_DIM)
    a_k_prompt = heads(k0[lp - keep:lp], 1)
    a_v_prompt = heads(v0[lp - keep:lp], 1)
    pool_prompt_out = u0[lp - B_HIST:lp].reshape(1, 1, B_HIST, B_WIDTH)
    c_k_prompt = k1[:lp].reshape(1, 1, lp, C_KV_HEADS, C_HEAD_DIM)
    c_v_prompt = v1[:lp].reshape(1, 1, lp, C_KV_HEADS, C_HEAD_DIM)
    c_idx_prompt = ki1[:lp].reshape(1, 1, lp, IDX_DIM)
    shift = lambda cache, new: jnp.concatenate(
        [cache[0], new.reshape(nb, ds, A_HEADS, A_HEAD_DIM)], axis=1)[:, ds:][None]
    a_k_sample = shift(cache_a_k, k0[lp:])
    a_v_sample = shift(cache_a_v, v0[lp:])
    pool_sample_out = u_hist[:, ds:][None]
    c_k_sample = k1[lp:].reshape(1, nb, ds, C_KV_HEADS, C_HEAD_DIM)
    c_v_sample = v1[lp:].reshape(1, nb, ds, C_KV_HEADS, C_HEAD_DIM)
    c_idx_sample = ki1[lp:].reshape(1, nb, ds, IDX_DIM)
    return (y_prompt, y_sample, a_k_prompt, a_v_prompt, pool_prompt_out,
            c_k_prompt, c_v_prompt, c_idx_prompt,
            a_k_sample, a_v_sample, pool_sample_out,
            c_k_sample, c_v_sample, c_idx_sample)
```

```python
import functools
import math

import jax
import jax.numpy as jnp
from jax import lax
from jax.experimental import pallas as pl
from jax.experimental.pallas import tpu as pltpu

F32 = jnp.float32
BF16 = jnp.bfloat16

NORM_EPS = 1e-6
NEG = -1e30
BIG = 1e30

CHUNK = 64
A_HEADS = 16
A_HEAD_DIM = 64
A_WIDTH = A_HEADS * A_HEAD_DIM
A_PREV_CHUNKS = 8
A_BAND = (A_PREV_CHUNKS + 1) * CHUNK
A_REL_CLIP = 128
B_WINDOWS = (2, 4, 8, 16)
B_GROUP = 256
B_WIDTH = B_GROUP * len(B_WINDOWS)
B_HIST = max(B_WINDOWS) - 1
C_HEADS = 16
C_KV_HEADS = 4
C_HEAD_DIM = 128
C_GROUPS = C_HEADS // C_KV_HEADS
IDX_HEADS = 8
IDX_DIM = 64
TOPK_MAX = 256
ROPE_THETA = 500000.0
ROPE_FRAC = 4
N_EXPERTS = 8

LANES = 128
POOL_HALO = 16
VMEM_LIMIT = 56 * 1024 * 1024

_NT = (((1,), (1,)), ((), ()))


def _params(*sem):
    return pltpu.CompilerParams(dimension_semantics=sem, vmem_limit_bytes=VMEM_LIMIT)


def _rms(x, g):
    ms = jnp.mean(x * x, axis=-1, keepdims=True)
    return x * lax.rsqrt(ms + NORM_EPS) * g


def _norm_proj_kernel(x_ref, g_ref, w_ref, o_ref, h_ref):
    @pl.when(pl.program_id(1) == 0)
    def _():
        h_ref[...] = _rms(x_ref[...], g_ref[...]).astype(BF16)

    o_ref[...] = jnp.dot(h_ref[...], w_ref[...], preferred_element_type=F32)


def norm_proj(x, g, w, tm, tn):
    m, d = x.shape
    n = w.shape[1]
    return pl.pallas_call(
        _norm_proj_kernel,
        grid=(m // tm, n // tn),
        in_specs=[pl.BlockSpec((tm, d), lambda i, j: (i, 0)),
                  pl.BlockSpec((1, d), lambda i, j: (0, 0)),
                  pl.BlockSpec((d, tn), lambda i, j: (0, j))],
        out_specs=pl.BlockSpec((tm, tn), lambda i, j: (i, j)),
        out_shape=jax.ShapeDtypeStruct((m, n), F32),
        scratch_shapes=[pltpu.VMEM((tm, d), BF16)],
        compiler_params=_params("parallel", "arbitrary"),
    )(x, g.reshape(1, d), w)


def _band_pairs(q_ref, q_row0, nq, kw_ref, vw_ref, k_row0, nk, bias_ref, key_ok, o_ref):
    lane = lax.broadcasted_iota(jnp.int32, (1, LANES), 1)
    first = lane < A_HEAD_DIM
    for hp in range(A_HEADS // 2):
        cs = slice(hp * LANES, (hp + 1) * LANES)
        qp = q_ref[pl.ds(q_row0, nq), cs]
        kp = kw_ref[pl.ds(k_row0, nk), cs]
        vp = vw_ref[pl.ds(k_row0, nk), cs]
        qm = jnp.concatenate([jnp.where(first, qp, 0.0), jnp.where(first, 0.0, qp)], axis=0).astype(BF16)
        s = lax.dot_general(qm, kp, _NT, preferred_element_type=F32)
        s = s * (A_HEAD_DIM ** -0.5) + bias_ref[hp]
        if key_ok is not None:
            s = jnp.where(key_ok, s, NEG)
        mx = jnp.max(s, axis=-1, keepdims=True)
        e = jnp.exp(s - mx)
        l = jnp.sum(e, axis=-1, keepdims=True)
        o = jnp.dot(e.astype(BF16), vp, preferred_element_type=F32) / l
        o_ref[pl.ds(q_row0, nq), cs] = jnp.where(first, o[:nq], o[nq:]).astype(o_ref.dtype)


def _band_prompt_kernel(q_ref, kp_ref, kc_ref, vp_ref, vc_ref, bias_ref, o_ref, kw_ref, vw_ref, *, qb, pad):
    i = pl.program_id(0)
    kw_ref[0:pad, :] = kp_ref[...].astype(BF16)
    kw_ref[pad:pad + qb, :] = kc_ref[...].astype(BF16)
    vw_ref[0:pad, :] = vp_ref[...].astype(BF16)
    vw_ref[pad:pad + qb, :] = vc_ref[...].astype(BF16)
    col = lax.broadcasted_iota(jnp.int32, (1, A_BAND), 1)

    def chunk(cc, carry):
        r0 = pl.multiple_of(cc * CHUNK, CHUNK)
        first_valid = jnp.where(i == 0, pad - cc * CHUNK, 0)
        _band_pairs(q_ref, r0, CHUNK, kw_ref, vw_ref, r0, A_BAND, bias_ref, col >= first_valid, o_ref)
        return carry

    lax.fori_loop(0, qb // CHUNK, chunk, 0)


def band_prompt(proj, bias, lp):
    pad = A_PREV_CHUNKS * CHUNK
    qb = pad
    blk = (qb, A_WIDTH)
    prev = lambda c: (lambda i: (jnp.maximum(i - 1, 0), c))
    cur = lambda c: (lambda i: (i, c))
    return pl.pallas_call(
        functools.partial(_band_prompt_kernel, qb=qb, pad=pad),
        grid=(lp // qb,),
        in_specs=[pl.BlockSpec(blk, cur(0)),
                  pl.BlockSpec(blk, prev(1)), pl.BlockSpec(blk, cur(1)),
                  pl.BlockSpec(blk, prev(2)), pl.BlockSpec(blk, cur(2)),
                  pl.BlockSpec((A_HEADS // 2, 2 * CHUNK, A_BAND), lambda i: (0, 0, 0))],
        out_specs=pl.BlockSpec(blk, lambda i: (i, 0)),
        out_shape=jax.ShapeDtypeStruct((lp, A_WIDTH), BF16),
        scratch_shapes=[pltpu.VMEM((pad + qb, A_WIDTH), BF16), pltpu.VMEM((pad + qb, A_WIDTH), BF16)],
        compiler_params=_params("parallel"),
    )(proj, proj, proj, proj, proj, bias)


def _band_sample_kernel(q_ref, kn_ref, vn_ref, ck_ref, cv_ref, bias_ref, o_ref, kw_ref, vw_ref, *, a_len, ds):
    for h in range(A_HEADS):
        hs = slice(h * A_HEAD_DIM, (h + 1) * A_HEAD_DIM)
        kw_ref[0:a_len, hs] = ck_ref[:, h, :].astype(BF16)
        vw_ref[0:a_len, hs] = cv_ref[:, h, :].astype(BF16)
    kw_ref[a_len:a_len + ds, :] = kn_ref[...].astype(BF16)
    vw_ref[a_len:a_len + ds, :] = vn_ref[...].astype(BF16)
    _band_pairs(q_ref, 0, ds, kw_ref, vw_ref, 0, a_len + ds, bias_ref, None, o_ref)


def band_sample(proj, cache_k, cache_v, bias, lp, nb, ds):
    a_len = cache_k.shape[1]
    row = lambda c: (lambda b: (lp // ds + b, c))
    return pl.pallas_call(
        functools.partial(_band_sample_kernel, a_len=a_len, ds=ds),
        grid=(nb,),
        in_specs=[pl.BlockSpec((ds, A_WIDTH), row(0)),
                  pl.BlockSpec((ds, A_WIDTH), row(1)),
                  pl.BlockSpec((ds, A_WIDTH), row(2)),
                  pl.BlockSpec((None, a_len, A_HEADS, A_HEAD_DIM), lambda b: (b, 0, 0, 0)),
                  pl.BlockSpec((None, a_len, A_HEADS, A_HEAD_DIM), lambda b: (b, 0, 0, 0)),
                  pl.BlockSpec((A_HEADS // 2, 2 * ds, a_len + ds), lambda b: (0, 0, 0))],
        out_specs=pl.BlockSpec((ds, A_WIDTH), lambda b: (b, 0)),
        out_shape=jax.ShapeDtypeStruct((nb * ds, A_WIDTH), BF16),
        scratch_shapes=[pltpu.VMEM((a_len + ds, A_WIDTH), BF16), pltpu.VMEM((a_len + ds, A_WIDTH), BF16)],
        compiler_params=_params("parallel"),
    )(proj, proj, proj, cache_k, cache_v, bias)


def _rel_bias_tile(rel_bias, q0, nq, k0, nk):
    rel_max = q0 - k0 + nq - 1
    rel = jnp.clip(rel_max - jnp.arange(nq + nk - 1), -A_REL_CLIP, A_REL_CLIP) + A_REL_CLIP
    ext = rel_bias[:, rel].astype(F32)
    return jnp.stack([ext[:, nq - 1 - i:nq - 1 - i + nk] for i in range(nq)], axis=1)


def _pair_rows(bias):
    h, nq, nk = bias.shape
    return bias.reshape(h // 2, 2 * nq, nk)


def _pool_kernel(prev_ref, cur_ref, w_ref, sc_ref, o_ref, ext_ref, *, tm, prompt, pos0):
    i = pl.program_id(0)
    prev = prev_ref[...]
    if prompt:
        prev = jnp.where(i == 0, 0.0, prev)
        pos = i * tm + lax.broadcasted_iota(jnp.int32, (tm, 1), 0)
    else:
        pos = pos0 + lax.broadcasted_iota(jnp.int32, (tm, 1), 0)
    ext_ref[0:POOL_HALO, :] = prev
    ext_ref[POOL_HALO:POOL_HALO + tm, :] = cur_ref[...]
    for g, w in enumerate(B_WINDOWS):
        cs = slice(g * B_GROUP, (g + 1) * B_GROUP)
        tok = ext_ref[POOL_HALO:POOL_HALO + tm, cs]
        tot = tok
        for j in range(1, w):
            tot = tot + ext_ref[POOL_HALO - j:POOL_HALO - j + tm, cs]
        cnt = jnp.minimum(pos + 1, w).astype(F32)
        pooled = (tot / cnt - tok).astype(BF16)
        o = jnp.dot(pooled, w_ref[g], preferred_element_type=F32) * sc_ref[:, cs]
        o_ref[:, cs] = o.astype(o_ref.dtype)


def pool_prompt(proj, pool_w, pool_scale, lp, tm):
    ucol = 3 * A_WIDTH // B_WIDTH
    per = tm // POOL_HALO
    return pl.pallas_call(
        functools.partial(_pool_kernel, tm=tm, prompt=True, pos0=0),
        grid=(lp // tm,),
        in_specs=[pl.BlockSpec((POOL_HALO, B_WIDTH), lambda i: (jnp.maximum(i * per - 1, 0), ucol)),
                  pl.BlockSpec((tm, B_WIDTH), lambda i: (i, ucol)),
                  pl.BlockSpec((len(B_WINDOWS), B_GROUP, B_GROUP), lambda i: (0, 0, 0)),
                  pl.BlockSpec((1, B_WIDTH), lambda i: (0, 0))],
        out_specs=pl.BlockSpec((tm, B_WIDTH), lambda i: (i, 0)),
        out_shape=jax.ShapeDtypeStruct((lp, B_WIDTH), BF16),
        scratch_shapes=[pltpu.VMEM((POOL_HALO + tm, B_WIDTH), F32)],
        compiler_params=_params("parallel"),
    )(proj, proj, pool_w, pool_scale)


def pool_sample(u_ext, pool_w, pool_scale, past):
    nb, tot, _ = u_ext.shape
    ds = tot - POOL_HALO
    return pl.pallas_call(
        functools.partial(_pool_kernel, tm=ds, prompt=False, pos0=past),
        grid=(nb,),
        in_specs=[pl.BlockSpec((None, POOL_HALO, B_WIDTH), lambda b: (b, 0, 0)),
                  pl.BlockSpec((None, ds, B_WIDTH), lambda b: (b, POOL_HALO // ds, 0)),
                  pl.BlockSpec((len(B_WINDOWS), B_GROUP, B_GROUP), lambda b: (0, 0, 0)),
                  pl.BlockSpec((1, B_WIDTH), lambda b: (0, 0))],
        out_specs=pl.BlockSpec((ds, B_WIDTH), lambda b: (b, 0)),
        out_shape=jax.ShapeDtypeStruct((nb * ds, B_WIDTH), BF16),
        scratch_shapes=[pltpu.VMEM((POOL_HALO + ds, B_WIDTH), F32)],
        compiler_params=_params("parallel"),
    )(u_ext, u_ext, pool_w, pool_scale)


def _mm_res_kernel(*refs, n_in):
    xs, ws = refs[:n_in], refs[n_in:2 * n_in]
    res_ref, o_ref = refs[2 * n_in], refs[2 * n_in + 1]
    acc = res_ref[...]
    for x_ref, w_ref in zip(xs, ws):
        acc = acc + jnp.dot(x_ref[...], w_ref[...], preferred_element_type=F32)
    o_ref[...] = acc


def mm_res(xs, ws, res, tm, tn):
    m, n = res.shape
    n_in = len(xs)
    in_specs = ([pl.BlockSpec((tm, x.shape[1]), lambda i, j: (i, 0)) for x in xs]
                + [pl.BlockSpec((w.shape[0], tn), lambda i, j: (0, j)) for w in ws]
                + [pl.BlockSpec((tm, tn), lambda i, j: (i, j))])
    return pl.pallas_call(
        functools.partial(_mm_res_kernel, n_in=n_in),
        grid=(m // tm, n // tn),
        in_specs=in_specs,
        out_specs=pl.BlockSpec((tm, tn), lambda i, j: (i, j)),
        out_shape=jax.ShapeDtypeStruct((m, n), F32),
        compiler_params=_params("parallel", "arbitrary"),
    )(*xs, *ws, res)


def _swiglu_tile(h, wg, wu, wd):
    a = jnp.dot(h, wg, preferred_element_type=F32)
    b = jnp.dot(h, wu, preferred_element_type=F32)
    act = (a * jax.nn.sigmoid(a) * b).astype(BF16)
    return jnp.dot(act, wd, preferred_element_type=F32)


def _ffn_kernel(y_ref, g_ref, wg_ref, wu_ref, wd_ref, o_ref, h_ref, acc_ref):
    f = pl.program_id(1)

    @pl.when(f == 0)
    def _():
        h_ref[...] = _rms(y_ref[...], g_ref[...]).astype(BF16)
        acc_ref[...] = jnp.zeros_like(acc_ref)

    acc_ref[...] += _swiglu_tile(h_ref[...], wg_ref[...], wu_ref[...], wd_ref[...])

    @pl.when(f == pl.num_programs(1) - 1)
    def _():
        o_ref[...] = y_ref[...] + acc_ref[...]


def ffn(y, g, wg, wu, wd, tm, tf):
    m, d = y.shape
    ff = wg.shape[1]
    return pl.pallas_call(
        _ffn_kernel,
        grid=(m // tm, ff // tf),
        in_specs=[pl.BlockSpec((tm, d), lambda i, f: (i, 0)),
                  pl.BlockSpec((1, d), lambda i, f: (0, 0)),
                  pl.BlockSpec((d, tf), lambda i, f: (0, f)),
                  pl.BlockSpec((d, tf), lambda i, f: (0, f)),
                  pl.BlockSpec((tf, d), lambda i, f: (f, 0))],
        out_specs=pl.BlockSpec((tm, d), lambda i, f: (i, 0)),
        out_shape=jax.ShapeDtypeStruct((m, d), F32),
        scratch_shapes=[pltpu.VMEM((tm, d), BF16), pltpu.VMEM((tm, d), F32)],
        compiler_params=_params("parallel", "arbitrary"),
    )(y, g.reshape(1, d), wg, wu, wd)


def _rope_tables(pos, head_dim):
    rot = head_dim // ROPE_FRAC
    half = rot // 2
    inv = jnp.exp(-math.log(ROPE_THETA) * jnp.arange(half, dtype=F32) * (2.0 / rot))
    ang = pos.astype(F32)[:, None] * inv[None, :]
    cos, sin = jnp.cos(ang), jnp.sin(ang)
    m = pos.shape[0]
    one = jnp.ones((m, head_dim - rot), F32)
    zero_r = jnp.zeros((m, head_dim - rot), F32)
    zero_h = jnp.zeros((m, half), F32)
    c = jnp.concatenate([cos, cos, one], axis=1)
    s_dn = jnp.concatenate([-sin, zero_h, zero_r], axis=1)
    s_up = jnp.concatenate([zero_h, sin, zero_r], axis=1)
    rep = LANES // head_dim
    return jnp.stack([jnp.tile(c, (1, rep)), jnp.tile(s_dn, (1, rep)), jnp.tile(s_up, (1, rep))])


def _rot(x, tab_ref, half):
    return (x * tab_ref[0] + pltpu.roll(x, LANES - half, 1) * tab_ref[1]
            + pltpu.roll(x, half, 1) * tab_ref[2])


def _rope_kernel(main_ref, tail_ref, tq_ref, ti_ref, kv_ref, t32_ref, ob_ref,
                 *, n_q, n_k, n_v, n_qi, wi_scale, q_scale):
    half_qk = C_HEAD_DIM // ROPE_FRAC // 2
    half_i = IDX_DIM // ROPE_FRAC // 2
    for c in range(n_q + n_k + n_v + n_qi):
        cs = slice(c * LANES, (c + 1) * LANES)
        x = main_ref[:, cs]
        if c < n_q + n_k:
            x = _rot(x, tq_ref, half_qk)
        elif c >= n_q + n_k + n_v:
            x = _rot(x, ti_ref, half_i)
        if n_q <= c < n_q + n_k + n_v:
            kv_ref[:, (c - n_q) * LANES:(c - n_q + 1) * LANES] = x
        ob_ref[:, cs] = (x * q_scale if c < n_q else x).astype(BF16)
    t = tail_ref[...]
    lane = lax.broadcasted_iota(jnp.int32, (1, LANES), 1)
    r = _rot(t, ti_ref, half_i)
    t32_ref[...] = jnp.where(lane < IDX_DIM, r, t * wi_scale)
    c = n_q + n_k + n_v + n_qi
    ob_ref[:, c * LANES:(c + 1) * LANES] = jnp.where(lane < IDX_DIM, r, pltpu.roll(r, IDX_DIM, 1)).astype(BF16)


def rope_all(main, tail, tab_qk, tab_idx, tm):
    m, nmain = main.shape
    n_q = C_HEADS * C_HEAD_DIM // LANES
    n_k = C_KV_HEADS * C_HEAD_DIM // LANES
    n_qi = IDX_HEADS * IDX_DIM // LANES
    wi_scale = (IDX_HEADS ** -0.5) * (IDX_DIM ** -0.5)
    q_scale = (C_HEAD_DIM ** -0.5) * math.log2(math.e)
    row = lambda i: (i, 0)
    return pl.pallas_call(
        functools.partial(_rope_kernel, n_q=n_q, n_k=n_k, n_v=n_k, n_qi=n_qi, wi_scale=wi_scale, q_scale=q_scale),
        grid=(m // tm,),
        in_specs=[pl.BlockSpec((tm, nmain), row),
                  pl.BlockSpec((tm, LANES), row),
                  pl.BlockSpec((3, tm, LANES), lambda i: (0, i, 0)),
                  pl.BlockSpec((3, tm, LANES), lambda i: (0, i, 0))],
        out_specs=[pl.BlockSpec((tm, 2 * n_k * LANES), row),
                   pl.BlockSpec((tm, LANES), row),
                   pl.BlockSpec((tm, nmain + LANES), row)],
        out_shape=[jax.ShapeDtypeStruct((m, 2 * n_k * LANES), F32),
                   jax.ShapeDtypeStruct((m, LANES), F32),
                   jax.ShapeDtypeStruct((m, nmain + LANES), BF16)],
        compiler_params=_params("parallel"),
    )(main, tail, tab_qk, tab_idx)


def _dsa_kernel(q_ref, qi_ref, wi_ref, kn_ref, vn_ref, kin_ref, ck_ref, cv_ref, ci_ref, o_ref,
                k_ref, v_ref, ki_ref, sc_ref, qs_ref, acc_ref, m_ref, l_ref,
                *, tq, kb, past, n_valid, q_pos0, topk):
    i = pl.program_id(1)
    rows = C_GROUPS * tq
    sub = kb // LANES
    kf = float(topk)

    new = past + tq
    ci = ci_ref[...].astype(BF16)
    for g in range(C_KV_HEADS):
        gs = slice(g * C_HEAD_DIM, (g + 1) * C_HEAD_DIM)
        k_ref[0:past, gs] = ck_ref[:, g, :].astype(BF16)
        v_ref[0:past, gs] = cv_ref[:, g, :].astype(BF16)
    ki_ref[0:past, :] = jnp.concatenate([ci, ci], axis=1)
    k_ref[past:new, :] = kn_ref[...]
    v_ref[past:new, :] = vn_ref[...]
    ki_ref[past:new, :] = kin_ref[...]
    n_pad = k_ref.shape[0] - new
    k_ref[new:, :] = jnp.zeros((n_pad, k_ref.shape[1]), BF16)
    v_ref[new:, :] = jnp.zeros((n_pad, v_ref.shape[1]), BF16)
    ki_ref[new:, :] = jnp.zeros((n_pad, ki_ref.shape[1]), BF16)

    q_pos = q_pos0 + i * tq + lax.broadcasted_iota(jnp.int32, (tq, 1), 0)
    q_chunk = q_pos // CHUNK
    last_chunk = (q_pos0 + i * tq + tq - 1) // CHUNK
    kv_limit = jnp.minimum(n_valid, (last_chunk + 1) * CHUNK)
    nkb = (kv_limit + kb - 1) // kb

    lane = lax.broadcasted_iota(jnp.int32, (1, LANES), 1)
    first = lane < IDX_DIM
    wi = wi_ref[:, IDX_DIM:IDX_DIM + IDX_HEADS]

    def score_block(b, carry):
        for c in range(sub):
            off = pl.multiple_of(b * kb + c * LANES, LANES)
            kib = ki_ref[pl.ds(off, LANES), :]
            acc = jnp.zeros((tq, LANES), F32)
            for hp in range(IDX_HEADS // 2):
                qp = qi_ref[:, hp * LANES:(hp + 1) * LANES]
                for half in range(2):
                    keep = first if half == 0 else jnp.logical_not(first)
                    qm = jnp.where(keep, qp, jnp.zeros_like(qp))
                    d = lax.dot_general(qm, kib, _NT, preferred_element_type=F32)
                    h = hp * 2 + half
                    acc = acc + jnp.maximum(d, 0.0) * wi[:, h:h + 1]
            k_pos = off + lane
            adm = jnp.logical_and(k_pos // CHUNK <= q_chunk, k_pos < n_valid)
            sc_ref[b, :, c * LANES:(c + 1) * LANES] = jnp.where(adm, acc, NEG)
        return carry

    lax.fori_loop(0, nkb, score_block, 0)

    def lane_sum(x):
        return jnp.sum(x, axis=1, keepdims=True)

    def count_ge(t):
        def body(b, acc):
            for c in range(sub):
                blk = sc_ref[b, :, c * LANES:(c + 1) * LANES]
                acc = acc + jnp.where(blk >= t, 1.0, 0.0)
            return acc
        return lane_sum(lax.fori_loop(0, nkb, body, jnp.zeros((tq, LANES), F32)))

    def stats(b, carry):
        mx, mn, cnt = carry
        for c in range(sub):
            blk = sc_ref[b, :, c * LANES:(c + 1) * LANES]
            ok = blk > 0.5 * NEG
            mx = jnp.maximum(mx, blk)
            mn = jnp.minimum(mn, jnp.where(ok, blk, BIG))
            cnt = cnt + jnp.where(ok, 1.0, 0.0)
        return mx, mn, cnt

    mx, mn, cnt = lax.fori_loop(
        0, nkb, stats,
        (jnp.full((tq, LANES), NEG, F32), jnp.full((tq, LANES), BIG, F32), jnp.zeros((tq, LANES), F32)))
    row_max = jnp.max(mx, axis=1, keepdims=True)
    row_min = jnp.min(mn, axis=1, keepdims=True)
    n_adm = lane_sum(cnt)

    done0 = jnp.where(n_adm <= kf, 1.0, 0.0)
    state0 = (row_min, jnp.full((tq, 1), BIG, F32), row_max, jnp.full((tq, 1), 0.5 * NEG, F32), done0)

    def bisect(state, n_steps):
        def cond(c):
            it, st = c
            return jnp.logical_and(it < n_steps, jnp.min(st[4]) < 0.5)

        def body(c):
            it, (lo, hi, mid, thr, done) = c
            cnt = count_ge(mid)
            live = done < 0.5
            hit = jnp.logical_and(live, cnt == kf)
            ge = cnt >= kf
            thr = jnp.where(hit, mid, thr)
            done = jnp.where(hit, 1.0, done)
            lo = jnp.where(ge, mid, lo)
            hi = jnp.where(ge, hi, mid)
            return it + 1, (lo, hi, 0.5 * (lo + hi), thr, done)

        return lax.while_loop(cond, body, (jnp.int32(0), state))[1]

    def snap(state):
        lo, hi, mid, thr, done = state

        def body(b, carry):
            v_lo, v_hi = carry
            for c in range(sub):
                blk = sc_ref[b, :, c * LANES:(c + 1) * LANES]
                v_lo = jnp.minimum(v_lo, jnp.where(blk >= lo, blk, BIG))
                v_hi = jnp.maximum(v_hi, jnp.where(blk < hi, blk, NEG))
            return v_lo, v_hi

        v_lo, v_hi = lax.fori_loop(0, nkb, body,
                                   (jnp.full((tq, LANES), BIG, F32), jnp.full((tq, LANES), NEG, F32)))
        v_lo = jnp.min(v_lo, axis=1, keepdims=True)
        v_hi = jnp.max(v_hi, axis=1, keepdims=True)
        live = done < 0.5
        tie = jnp.logical_and(live, v_lo == v_hi)
        thr = jnp.where(tie, v_lo, thr)
        done = jnp.where(tie, 1.0, done)
        lo = jnp.where(live, v_lo, lo)
        return (lo, hi, 0.5 * (lo + hi), thr, done), jnp.where(tie, 1.0, 0.0)

    state = bisect(state0, 32)

    def refine_cond(c):
        rounds, st, _ = c
        return jnp.logical_and(rounds < 10, jnp.min(st[4]) < 0.5)

    def refine_body(c):
        rounds, st, tie = c
        st, new_tie = snap(st)
        st = bisect(st, 32)
        return rounds + 1, st, jnp.maximum(tie, new_tie)

    _, state, tie = lax.while_loop(refine_cond, refine_body,
                                   (jnp.int32(0), state, jnp.zeros((tq, 1), F32)))
    thr = state[3]
    any_tie = jnp.max(tie) > 0.5

    @pl.when(jnp.logical_not(any_tie))
    def _():
        def body(b, carry):
            for c in range(sub):
                cs = slice(c * LANES, (c + 1) * LANES)
                sc_ref[b, :, cs] = jnp.where(sc_ref[b, :, cs] >= thr, 0.0, NEG)
            return carry
        lax.fori_loop(0, nkb, body, 0)

    @pl.when(any_tie)
    def _():
        def gt_body(b, acc):
            for c in range(sub):
                acc = acc + jnp.where(sc_ref[b, :, c * LANES:(c + 1) * LANES] > thr, 1.0, 0.0)
            return acc
        need = kf - lane_sum(lax.fori_loop(0, nkb, gt_body, jnp.zeros((tq, LANES), F32)))

        def count_eq_upto(j):
            def body(b, acc):
                for c in range(sub):
                    blk = sc_ref[b, :, c * LANES:(c + 1) * LANES]
                    idx = (b * kb + c * LANES + lane).astype(F32)
                    acc = acc + jnp.where(jnp.logical_and(blk == thr, idx <= j), 1.0, 0.0)
                return acc
            return lane_sum(lax.fori_loop(0, nkb, body, jnp.zeros((tq, LANES), F32)))

        def idx_body(_, c):
            lo_j, hi_j = c
            mid_j = jnp.floor(0.5 * (lo_j + hi_j))
            ok = count_eq_upto(mid_j) >= need
            return jnp.where(ok, lo_j, mid_j), jnp.where(ok, mid_j, hi_j)

        n_steps = max(1, math.ceil(math.log2(sc_ref.shape[0] * kb + 1)))
        _, last = lax.fori_loop(
            0, n_steps, idx_body,
            (jnp.full((tq, 1), -1.0, F32), jnp.full((tq, 1), float(sc_ref.shape[0] * kb), F32)))
        last = jnp.where(tie > 0.5, last, BIG)

        def body(b, carry):
            for c in range(sub):
                cs = slice(c * LANES, (c + 1) * LANES)
                blk = sc_ref[b, :, cs]
                idx = (b * kb + c * LANES + lane).astype(F32)
                sel = jnp.logical_or(blk > thr, jnp.logical_and(blk == thr, idx <= last))
                sc_ref[b, :, cs] = jnp.where(sel, 0.0, NEG)
            return carry
        lax.fori_loop(0, nkb, body, 0)

    for g in range(C_KV_HEADS):
        for hh in range(C_GROUPS):
            h = g * C_GROUPS + hh
            qs_ref[g, hh * tq:(hh + 1) * tq, :] = q_ref[:, h * C_HEAD_DIM:(h + 1) * C_HEAD_DIM]
    m_ref[...] = jnp.full(m_ref.shape, NEG, F32)
    l_ref[...] = jnp.zeros(l_ref.shape, F32)
    acc_ref[...] = jnp.zeros(acc_ref.shape, F32)

    def attend(b, carry):
        off = pl.multiple_of(b * kb, kb)
        bias = sc_ref[b]
        bias = jnp.concatenate([bias] * C_GROUPS, axis=0)
        for g in range(C_KV_HEADS):
            cs = slice(g * C_HEAD_DIM, (g + 1) * C_HEAD_DIM)
            kblk = k_ref[pl.ds(off, kb), cs]
            vblk = v_ref[pl.ds(off, kb), cs]
            s = lax.dot_general(qs_ref[g], kblk, _NT, preferred_element_type=F32)
            s = s + bias
            m_old = m_ref[g]
            m_new = jnp.maximum(m_old, jnp.max(s, axis=1, keepdims=True))
            alpha = jnp.exp2(m_old - m_new)
            p = jnp.exp2(s - m_new)
            l_ref[g] = alpha * l_ref[g] + jnp.sum(p, axis=1, keepdims=True)
            acc_ref[g] = alpha * acc_ref[g] + jnp.dot(p.astype(BF16), vblk, preferred_element_type=F32)
            m_ref[g] = m_new
        return carry

    lax.fori_loop(0, nkb, attend, 0)

    for g in range(C_KV_HEADS):
        o = acc_ref[g] / l_ref[g]
        for hh in range(C_GROUPS):
            h = g * C_GROUPS + hh
            o_ref[:, h * C_HEAD_DIM:(h + 1) * C_HEAD_DIM] = o[hh * tq:(hh + 1) * tq, :].astype(o_ref.dtype)


def dsa_sample(ob, t32, cache_k, cache_v, cache_i, *, row0, kb, topk):
    nb, past = cache_k.shape[:2]
    n_kv = C_KV_HEADS * C_HEAD_DIM
    ds = (ob.shape[0] - row0) // nb
    n_q = C_HEADS * C_HEAD_DIM
    n_qi = IDX_HEADS * IDX_DIM
    s_all = past + ds
    s_pad = -(-s_all // kb) * kb
    rows = C_GROUPS * ds
    col = lambda w, off: (lambda b, i: (row0 // ds + b, off // w))
    cmap = lambda b, i: (b, 0, 0)
    return pl.pallas_call(
        functools.partial(_dsa_kernel, tq=ds, kb=kb, past=past, n_valid=s_all, q_pos0=past, topk=topk),
        grid=(nb, 1),
        in_specs=[pl.BlockSpec((ds, n_q), col(n_q, 0)),
                  pl.BlockSpec((ds, n_qi), col(n_qi, n_q + 2 * n_kv)),
                  pl.BlockSpec((ds, LANES), col(LANES, 0)),
                  pl.BlockSpec((ds, n_kv), col(n_kv, n_q)),
                  pl.BlockSpec((ds, n_kv), col(n_kv, n_q + n_kv)),
                  pl.BlockSpec((ds, LANES), col(LANES, n_q + 2 * n_kv + n_qi)),
                  pl.BlockSpec((None, past, C_KV_HEADS, C_HEAD_DIM), lambda b, i: (b, 0, 0, 0)),
                  pl.BlockSpec((None, past, C_KV_HEADS, C_HEAD_DIM), lambda b, i: (b, 0, 0, 0)),
                  pl.BlockSpec((None, past, IDX_DIM), cmap)],
        out_specs=pl.BlockSpec((ds, n_q), lambda b, i: (b, 0)),
        out_shape=jax.ShapeDtypeStruct((nb * ds, n_q), BF16),
        scratch_shapes=[pltpu.VMEM((s_pad, n_kv), BF16),
                        pltpu.VMEM((s_pad, n_kv), BF16),
                        pltpu.VMEM((s_pad, LANES), BF16),
                        pltpu.VMEM((s_pad // kb, ds, kb), F32),
                        pltpu.VMEM((C_KV_HEADS, rows, C_HEAD_DIM), BF16),
                        pltpu.VMEM((C_KV_HEADS, rows, C_HEAD_DIM), F32),
                        pltpu.VMEM((C_KV_HEADS, rows, 1), F32),
                        pltpu.VMEM((C_KV_HEADS, rows, 1), F32)],
        compiler_params=_params("parallel", "arbitrary"),
    )(ob, ob, t32, ob, ob, ob, cache_k, cache_v, cache_i)


def _dsa_t_kernel(q_ref, qi_ref, wi_ref, k_ref, vt_ref, ki_ref, o_ref,
                  sc_ref, qim_ref, acc_ref, m_ref, l_ref,
                  *, tq, kb, pb, cb, n_valid, q_pos0, topk):
    i = pl.program_id(1)
    kf = float(topk)

    q_pos = q_pos0 + i * tq + lax.broadcasted_iota(jnp.int32, (1, tq), 1)
    q_chunk = q_pos // CHUNK
    last_chunk = (q_pos0 + i * tq + tq - 1) // CHUNK
    kv_limit = jnp.minimum(n_valid, (last_chunk + 1) * CHUNK)
    nkb = (kv_limit + kb - 1) // kb
    ncb = (kv_limit + cb - 1) // cb

    def col_reduce(x, op):
        groups = x.shape[0] // 8
        chains = 8 if groups % 8 == 0 else 1
        return op(op(x.reshape(chains, groups // chains, 8, tq), axis=1), axis=0)

    col_sum = lambda x: col_reduce(x, jnp.sum)
    col_max = lambda x: col_reduce(x, jnp.max)
    col_min = lambda x: col_reduce(x, jnp.min)

    lane = lax.broadcasted_iota(jnp.int32, (1, LANES), 1)
    first = lane < IDX_DIM
    for hp in range(IDX_HEADS // 2):
        qp = qi_ref[:, hp * LANES:(hp + 1) * LANES]
        qim_ref[2 * hp] = jnp.where(first, qp, jnp.zeros_like(qp))
        qim_ref[2 * hp + 1] = jnp.where(first, jnp.zeros_like(qp), qp)
    block_iota = lax.broadcasted_iota(jnp.int32, (pb, 1), 0)

    def score_block(b, carry):
        off = pl.multiple_of(b * pb, pb)
        kib = ki_ref[pl.ds(off, pb), :]
        acc = jnp.zeros((pb, tq), F32)
        for h in range(IDX_HEADS):
            d = lax.dot_general(kib, qim_ref[h], _NT, preferred_element_type=F32)
            acc = acc + jnp.maximum(d, 0.0) * wi_ref[h:h + 1, :]
        k_pos = off + block_iota
        adm = jnp.logical_and(k_pos // CHUNK <= q_chunk, k_pos < n_valid)
        sc_ref[pl.ds(off, pb), :] = jnp.where(adm, acc, NEG)
        return carry

    lax.fori_loop(0, ncb * (cb // pb), score_block, 0)
    key_iota = lax.broadcasted_iota(jnp.int32, (cb, 1), 0)

    def blocks(body, init):
        def step(b, carry):
            off = pl.multiple_of(b * cb, cb)
            return body(off, sc_ref[pl.ds(off, cb), :], carry)
        return lax.fori_loop(0, ncb, step, init)

    def total(x):
        return jnp.sum(x, axis=0, keepdims=True)

    def count_ge(t):
        return total(blocks(lambda off, blk, acc: acc + col_sum(jnp.where(blk >= t, 1.0, 0.0)),
                            jnp.zeros((8, tq), F32)))

    def stats(off, blk, carry):
        mx, mn, cnt = carry
        ok = blk > 0.5 * NEG
        return (jnp.maximum(mx, col_max(blk)), jnp.minimum(mn, col_min(jnp.where(ok, blk, BIG))),
                cnt + col_sum(jnp.where(ok, 1.0, 0.0)))

    mx, mn, cnt = blocks(stats, (jnp.full((8, tq), NEG, F32), jnp.full((8, tq), BIG, F32),
                                 jnp.zeros((8, tq), F32)))
    row_max = jnp.max(mx, axis=0, keepdims=True)
    row_min = jnp.min(mn, axis=0, keepdims=True)
    n_adm = total(cnt)

    done0 = jnp.where(n_adm <= kf, 1.0, 0.0)
    state0 = (row_min, jnp.full((1, tq), BIG, F32), row_max, jnp.full((1, tq), 0.5 * NEG, F32), done0)

    def bisect(state, n_steps):
        def cond(c):
            it, st = c
            return jnp.logical_and(it < n_steps, jnp.min(st[4]) < 0.5)

        def body(c):
            it, (lo, hi, mid, thr, done) = c
            cnt = count_ge(mid)
            hit = jnp.logical_and(done < 0.5, cnt == kf)
            ge = cnt >= kf
            thr = jnp.where(hit, mid, thr)
            done = jnp.where(hit, 1.0, done)
            lo = jnp.where(ge, mid, lo)
            hi = jnp.where(ge, hi, mid)
            return it + 1, (lo, hi, 0.5 * (lo + hi), thr, done)

        return lax.while_loop(cond, body, (jnp.int32(0), state))[1]

    def snap(state):
        lo, hi, mid, thr, done = state

        def body(off, blk, carry):
            v_lo, v_hi = carry
            return (jnp.minimum(v_lo, col_min(jnp.where(blk >= lo, blk, BIG))),
                    jnp.maximum(v_hi, col_max(jnp.where(blk < hi, blk, NEG))))

        v_lo, v_hi = blocks(body, (jnp.full((8, tq), BIG, F32), jnp.full((8, tq), NEG, F32)))
        v_lo = jnp.min(v_lo, axis=0, keepdims=True)
        v_hi = jnp.max(v_hi, axis=0, keepdims=True)
        live = done < 0.5
        tie = jnp.logical_and(live, v_lo == v_hi)
        thr = jnp.where(tie, v_lo, thr)
        done = jnp.where(tie, 1.0, done)
        lo = jnp.where(live, v_lo, lo)
        return (lo, hi, 0.5 * (lo + hi), thr, done), jnp.where(tie, 1.0, 0.0)

    state = bisect(state0, 32)

    def refine_cond(c):
        rounds, st, _ = c
        return jnp.logical_and(rounds < 10, jnp.min(st[4]) < 0.5)

    def refine_body(c):
        rounds, st, tie = c
        st, new_tie = snap(st)
        st = bisect(st, 32)
        return rounds + 1, st, jnp.maximum(tie, new_tie)

    _, state, tie = lax.while_loop(refine_cond, refine_body,
                                   (jnp.int32(0), state, jnp.zeros((1, tq), F32)))
    thr = state[3]
    any_tie = jnp.max(tie) > 0.5

    @pl.when(jnp.logical_not(any_tie))
    def _():
        def body(off, blk, carry):
            sc_ref[pl.ds(off, cb), :] = jnp.where(blk >= thr, 0.0, NEG)
            return carry
        blocks(body, 0)

    @pl.when(any_tie)
    def _():
        need = kf - total(blocks(lambda off, blk, acc: acc + col_sum(jnp.where(blk > thr, 1.0, 0.0)),
                                 jnp.zeros((8, tq), F32)))

        def count_eq_upto(j):
            def body(off, blk, acc):
                idx = (off + key_iota).astype(F32)
                return acc + col_sum(jnp.where(jnp.logical_and(blk == thr, idx <= j), 1.0, 0.0))
            return total(blocks(body, jnp.zeros((8, tq), F32)))

        def idx_body(_, c):
            lo_j, hi_j = c
            mid_j = jnp.floor(0.5 * (lo_j + hi_j))
            ok = count_eq_upto(mid_j) >= need
            return jnp.where(ok, lo_j, mid_j), jnp.where(ok, mid_j, hi_j)

        n_keys = sc_ref.shape[0]
        n_steps = max(1, math.ceil(math.log2(n_keys + 1)))
        _, last = lax.fori_loop(0, n_steps, idx_body,
                                (jnp.full((1, tq), -1.0, F32), jnp.full((1, tq), float(n_keys), F32)))
        last = jnp.where(tie > 0.5, last, BIG)

        def body(off, blk, carry):
            idx = (off + key_iota).astype(F32)
            sel = jnp.logical_or(blk > thr, jnp.logical_and(blk == thr, idx <= last))
            sc_ref[pl.ds(off, cb), :] = jnp.where(sel, 0.0, NEG)
            return carry
        blocks(body, 0)

    m_ref[...] = jnp.full(m_ref.shape, NEG, F32)
    l_ref[...] = jnp.zeros(l_ref.shape, F32)
    acc_ref[...] = jnp.zeros(acc_ref.shape, F32)

    def attend(b, carry):
        off = pl.multiple_of(b * kb, kb)
        for h in range(C_HEADS):
            g = h // C_GROUPS
            gs = slice(g * C_HEAD_DIM, (g + 1) * C_HEAD_DIM)
            s = lax.dot_general(k_ref[pl.ds(off, kb), gs], q_ref[:, h * C_HEAD_DIM:(h + 1) * C_HEAD_DIM],
                                _NT, preferred_element_type=F32)
            s = s + sc_ref[pl.ds(off, kb), :]
            m_old = m_ref[h]
            m_new = jnp.maximum(m_old, jnp.max(col_max(s), axis=0, keepdims=True))
            alpha = jnp.exp2(m_old - m_new)
            p = jnp.exp2(s - m_new)
            l_ref[h] = alpha * l_ref[h] + total(col_sum(p))
            acc_ref[h] = alpha * acc_ref[h] + jnp.dot(vt_ref[b, gs, :], p.astype(BF16),
                                                      preferred_element_type=F32)
            m_ref[h] = m_new
        return carry

    lax.fori_loop(0, nkb, attend, 0)

    for h in range(C_HEADS):
        o = acc_ref[h] / l_ref[h]
        o_ref[:, h * C_HEAD_DIM:(h + 1) * C_HEAD_DIM] = o.T.astype(o_ref.dtype)


def dsa_t(ob, wi_t, vt, *, nq, tq, kb, pb, cb, topk):
    dq = C_HEADS * C_HEAD_DIM
    n_kv = C_KV_HEADS * C_HEAD_DIM
    n_qi = IDX_HEADS * IDX_DIM
    s_pad = nq
    return pl.pallas_call(
        functools.partial(_dsa_t_kernel, tq=tq, kb=kb, pb=pb, cb=cb, n_valid=nq, q_pos0=0, topk=topk),
        grid=(1, nq // tq),
        in_specs=[pl.BlockSpec((tq, dq), lambda b, i: (i, 0)),
                  pl.BlockSpec((tq, n_qi), lambda b, i: (i, (dq + 2 * n_kv) // n_qi)),
                  pl.BlockSpec((IDX_HEADS, tq), lambda b, i: (0, i)),
                  pl.BlockSpec((nq, n_kv), lambda b, i: (0, dq // n_kv)),
                  pl.BlockSpec(vt.shape, lambda b, i: (0, 0, 0)),
                  pl.BlockSpec((nq, LANES), lambda b, i: (0, (dq + 2 * n_kv + n_qi) // LANES))],
        out_specs=pl.BlockSpec((tq, dq), lambda b, i: (i, 0)),
        out_shape=jax.ShapeDtypeStruct((nq, dq), BF16),
        scratch_shapes=[pltpu.VMEM((s_pad, tq), F32),
                        pltpu.VMEM((IDX_HEADS, tq, LANES), BF16),
                        pltpu.VMEM((C_HEADS, C_HEAD_DIM, tq), F32),
                        pltpu.VMEM((C_HEADS, 1, tq), F32),
                        pltpu.VMEM((C_HEADS, 1, tq), F32)],
        compiler_params=_params("arbitrary", "arbitrary"),
    )(ob, ob, wi_t, ob, vt, ob)


def _router_kernel(y_ref, g_ref, wr_ref, h_ref, gate_ref):
    h = _rms(y_ref[...], g_ref[...])
    h_ref[...] = h.astype(BF16)
    logits = jnp.dot(h, wr_ref[...], preferred_element_type=F32, precision=lax.Precision.HIGHEST)
    lane = lax.broadcasted_iota(jnp.int32, logits.shape, 1)
    lg = jnp.where(lane < N_EXPERTS, logits, NEG)
    m1 = jnp.max(lg, axis=1, keepdims=True)
    i1 = jnp.min(jnp.where(lg == m1, lane, LANES), axis=1, keepdims=True)
    lg2 = jnp.where(lane == i1, NEG, lg)
    m2 = jnp.max(lg2, axis=1, keepdims=True)
    i2 = jnp.min(jnp.where(lg2 == m2, lane, LANES), axis=1, keepdims=True)
    e = jnp.exp(m2 - m1)
    g1 = 1.0 / (1.0 + e)
    g2 = e / (1.0 + e)
    meta = jnp.where(lane == 0, i1.astype(F32), jnp.where(lane == 1, i2.astype(F32),
                     jnp.where(lane == 2, g1, jnp.where(lane == 3, g2, 0.0))))
    gate_ref[...] = meta


def router(y, g, w_router_pad, tm):
    m, d = y.shape
    return pl.pallas_call(
        _router_kernel,
        grid=(m // tm,),
        in_specs=[pl.BlockSpec((tm, d), lambda i: (i, 0)),
                  pl.BlockSpec((1, d), lambda i: (0, 0)),
                  pl.BlockSpec((d, LANES), lambda i: (0, 0))],
        out_specs=[pl.BlockSpec((tm, d), lambda i: (i, 0)),
                   pl.BlockSpec((tm, LANES), lambda i: (i, 0))],
        out_shape=[jax.ShapeDtypeStruct((m, d), BF16), jax.ShapeDtypeStruct((m, LANES), F32)],
        compiler_params=_params("parallel"),
    )(y, g.reshape(1, d), w_router_pad)


def route_plan(meta, sup):
    m = meta.shape[0]
    n_tiles = (2 * m) // sup + N_EXPERTS
    e_all = jnp.concatenate([meta[:, 0], meta[:, 1]]).astype(jnp.int32)
    onehot = (e_all[:, None] == jnp.arange(N_EXPERTS)[None, :]).astype(jnp.int32)
    rank = jnp.sum((jnp.cumsum(onehot, axis=0) - onehot) * onehot, axis=1)
    counts = jnp.sum(onehot, axis=0)
    n_super = (counts + sup - 1) // sup
    super_end = jnp.cumsum(n_super)
    super_start = super_end - n_super
    pos = (super_start * sup)[e_all] + rank
    tiles = jnp.arange(n_tiles)
    used = super_end[-1]
    t_eff = jnp.minimum(tiles, used - 1)
    tile_expert = jnp.minimum(jnp.searchsorted(super_end, t_eff, side="right"), N_EXPERTS - 1).astype(jnp.int32)
    rows = jnp.clip(counts[tile_expert] - (t_eff - super_start[tile_expert]) * sup, 0, sup)
    tile_rows = jnp.where(tiles < used, rows, 0).astype(jnp.int32)
    token = jnp.concatenate([jnp.arange(m), jnp.arange(m)]).astype(jnp.int32)
    row_token = (jnp.arange(n_tiles * sup, dtype=jnp.int32) % m).at[pos].set(token)
    return pos.astype(jnp.int32), row_token, tile_expert, tile_rows


def _gather_rows_kernel(tok_ref, h_ref, o_ref, sem, *, batch):
    base = pl.program_id(0) * batch

    def issue(j, carry):
        pltpu.make_async_copy(h_ref.at[tok_ref[base + j]], o_ref.at[j], sem).start()
        return carry

    lax.fori_loop(0, batch, issue, 0)
    pltpu.make_async_copy(o_ref, o_ref, sem).wait()


def gather_rows(h3, row_token, batch):
    n_rows = row_token.shape[0]
    blk = (batch,) + h3.shape[1:]
    return pl.pallas_call(
        functools.partial(_gather_rows_kernel, batch=batch),
        grid_spec=pltpu.PrefetchScalarGridSpec(
            num_scalar_prefetch=1,
            grid=(n_rows // batch,),
            in_specs=[pl.BlockSpec(memory_space=pl.ANY)],
            out_specs=pl.BlockSpec(blk, lambda i, tok: (i, 0, 0)),
            scratch_shapes=[pltpu.SemaphoreType.DMA(())]),
        out_shape=jax.ShapeDtypeStruct((n_rows,) + h3.shape[1:], h3.dtype),
        compiler_params=_params("arbitrary"),
    )(row_token, h3)


def _grouped_ffn_kernel(te_ref, tr_ref, x_ref, wg_ref, wu_ref, wd_ref, o_ref,
                        wgb_ref, wub_ref, wdb_ref, *, sup, sub):
    t = pl.program_id(0)
    f = pl.program_id(1)
    rows = tr_ref[t]

    @pl.when(rows > 0)
    def _():
        wgb_ref[...] = wg_ref[...].astype(BF16)
        wub_ref[...] = wu_ref[...].astype(BF16)
        wdb_ref[...] = wd_ref[...].astype(BF16)

    for s in range(sup // sub):
        sl = slice(s * sub, (s + 1) * sub)

        @pl.when(jnp.logical_and(s * sub >= rows, f == 0))
        def _():
            o_ref[sl, :] = jnp.zeros((sub, o_ref.shape[1]), F32)

        @pl.when(s * sub < rows)
        def _():
            part = _swiglu_tile(x_ref[sl, :], wgb_ref[...], wub_ref[...], wdb_ref[...])

            @pl.when(f == 0)
            def _():
                o_ref[sl, :] = part

            @pl.when(f > 0)
            def _():
                o_ref[sl, :] += part


def grouped_ffn(xs, tile_expert, tile_rows, wg, wu, wd, sup, sub, tf):
    n_rows, d = xs.shape
    ff = wg.shape[2]
    n_f = ff // tf
    fidx = lambda t, f, te, tr: jnp.where(tr[t] > 0, f, n_f - 1)
    return pl.pallas_call(
        functools.partial(_grouped_ffn_kernel, sup=sup, sub=sub),
        grid_spec=pltpu.PrefetchScalarGridSpec(
            num_scalar_prefetch=2,
            grid=(n_rows // sup, n_f),
            in_specs=[pl.BlockSpec((sup, d), lambda t, f, te, tr: (t, 0), pipeline_mode=pl.Buffered(1)),
                      pl.BlockSpec((None, d, tf), lambda t, f, te, tr: (te[t], 0, fidx(t, f, te, tr))),
                      pl.BlockSpec((None, d, tf), lambda t, f, te, tr: (te[t], 0, fidx(t, f, te, tr))),
                      pl.BlockSpec((None, tf, d), lambda t, f, te, tr: (te[t], fidx(t, f, te, tr), 0))],
            out_specs=pl.BlockSpec((sup, d), lambda t, f, te, tr: (t, 0), pipeline_mode=pl.Buffered(1)),
            scratch_shapes=[pltpu.VMEM((d, tf), BF16), pltpu.VMEM((d, tf), BF16), pltpu.VMEM((tf, d), BF16)]),
        out_shape=jax.ShapeDtypeStruct((n_rows, d), F32),
        compiler_params=_params("arbitrary", "arbitrary"),
    )(tile_expert, tile_rows, xs, wg, wu, wd)


def _combine_kernel(pos_ref, y_ref, meta_ref, g_ref, ys_ref, op_ref, os_ref, buf_ref, x_ref, sem,
                    *, n_tok, tmc, prompt_tiles):
    i = pl.program_id(0)
    base = i * tmc

    def issue(j, carry):
        t = base + j
        pltpu.make_async_copy(ys_ref.at[pos_ref[t]], buf_ref.at[0, j], sem).start()
        pltpu.make_async_copy(ys_ref.at[pos_ref[n_tok + t]], buf_ref.at[1, j], sem).start()
        return carry

    lax.fori_loop(0, tmc, issue, 0)
    pltpu.make_async_copy(buf_ref.at[0], buf_ref.at[0], sem).wait()
    pltpu.make_async_copy(buf_ref.at[1], buf_ref.at[1], sem).wait()
    g1 = meta_ref[:, 2:3]
    g2 = meta_ref[:, 3:4]
    ss = jnp.zeros((tmc, 1), F32)
    for c in range(buf_ref.shape[2]):
        cs = slice(c * LANES, (c + 1) * LANES)
        x = y_ref[:, cs] + (buf_ref[0, :, c, :] * g1 + buf_ref[1, :, c, :] * g2)
        ss = ss + jnp.sum(x * x, axis=1, keepdims=True)
        x_ref[:, cs] = x
    inv = lax.rsqrt(ss / x_ref.shape[1] + NORM_EPS)

    @pl.when(i < prompt_tiles)
    def _():
        op_ref[...] = x_ref[...] * inv * g_ref[...]

    @pl.when(i >= prompt_tiles)
    def _():
        os_ref[...] = x_ref[...] * inv * g_ref[...]


def combine_final(y, meta, g_final, ys3, pos, n_prompt, tmc):
    m, d = y.shape
    slab = ys3.shape[1:]
    prompt_tiles = n_prompt // tmc
    return pl.pallas_call(
        functools.partial(_combine_kernel, n_tok=m, tmc=tmc, prompt_tiles=prompt_tiles),
        grid_spec=pltpu.PrefetchScalarGridSpec(
            num_scalar_prefetch=1,
            grid=(m // tmc,),
            in_specs=[pl.BlockSpec((tmc, d), lambda i, pos: (i, 0)),
                      pl.BlockSpec((tmc, LANES), lambda i, pos: (i, 0)),
                      pl.BlockSpec((1, d), lambda i, pos: (0, 0)),
                      pl.BlockSpec(memory_space=pl.ANY)],
            out_specs=[pl.BlockSpec((tmc, d), lambda i, pos: (jnp.minimum(i, prompt_tiles - 1), 0)),
                       pl.BlockSpec((tmc, d), lambda i, pos: (jnp.maximum(i - prompt_tiles, 0), 0))],
            scratch_shapes=[pltpu.VMEM((2, tmc) + slab, F32), pltpu.VMEM((tmc, d), F32),
                            pltpu.SemaphoreType.DMA(())]),
        out_shape=[jax.ShapeDtypeStruct((n_prompt, d), F32), jax.ShapeDtypeStruct((m - n_prompt, d), F32)],
        compiler_params=_params("arbitrary"),
    )(pos, y, meta, g_final.reshape(1, d), ys3)


def kernel(x_prompt, x_sample, cache_a_k, cache_a_v, state_pool, cache_c_k, cache_c_v, cache_c_idx,
           norm_mix, norm_ffn, norm_final, w_in_even, w_out_even, a_rel_bias, pool_w, pool_scale,
           ffn_w_gate, ffn_w_up, ffn_w_down, w_in_odd, w_out_odd,
           moe_router, moe_w_gate, moe_w_up, moe_w_down):
    nbp, lp, d = x_prompt.shape
    nb, ds, _ = x_sample.shape
    past = cache_c_k.shape[2]
    a_len = cache_a_k.shape[2]
    assert nbp == 1 and lp % 512 == 0 and (nb * ds) % 512 == 0 and ds == POOL_HALO and past >= POOL_HALO
    ns = nb * ds
    m = lp + ns
    tm = 512
    bf = lambda t: t.astype(BF16)

    x = jnp.concatenate([x_prompt.reshape(lp, d), x_sample.reshape(ns, d)], axis=0)

    proj0 = norm_proj(x, norm_mix[0], bf(w_in_even[0]), tm, 1024)
    k0 = proj0[:, A_WIDTH:2 * A_WIDTH]
    v0 = proj0[:, 2 * A_WIDTH:3 * A_WIDTH]
    u0 = proj0[:, 3 * A_WIDTH:]

    pad = A_PREV_CHUNKS * CHUNK
    bias_p = _rel_bias_tile(a_rel_bias[0], 0, CHUNK, -pad, A_BAND)
    a_p = band_prompt(proj0, _pair_rows(bias_p), lp)

    k_pos = past - a_len + jnp.arange(a_len + ds)
    q_pos = past + jnp.arange(ds)
    qch, kch = q_pos // CHUNK, k_pos // CHUNK
    ok = ((k_pos[None, :] >= 0) & (kch[None, :] <= qch[:, None])
          & (kch[None, :] >= qch[:, None] - A_PREV_CHUNKS))
    bias_s = jnp.where(ok[None], _rel_bias_tile(a_rel_bias[0], past, ds, past - a_len, a_len + ds), NEG)
    a_s = band_sample(proj0, cache_a_k[0], cache_a_v[0], _pair_rows(bias_s), lp, nb, ds)

    u_s = u0[lp:].reshape(nb, ds, B_WIDTH)
    u_hist = jnp.concatenate([state_pool[0], u_s], axis=1)
    u_ext = jnp.concatenate([jnp.zeros((nb, POOL_HALO - B_HIST, B_WIDTH), F32), u_hist], axis=1)
    pw = bf(pool_w[0])
    ps = pool_scale[0].reshape(1, B_WIDTH)
    p_p = pool_prompt(proj0, pw, ps, lp, tm)
    p_s = pool_sample(u_ext, pw, ps, past)

    a = jnp.concatenate([a_p, a_s], axis=0)
    p = jnp.concatenate([p_p, p_s], axis=0)
    wo = bf(w_out_even[0])
    y = mm_res([a, p], [wo[:A_WIDTH], wo[A_WIDTH:]], x, tm, 1024)
    y = ffn(y, norm_ffn[0], bf(ffn_w_gate[0]), bf(ffn_w_up[0]), bf(ffn_w_down[0]), tm, 512)

    n_q = C_HEADS * C_HEAD_DIM
    n_kv = C_KV_HEADS * C_HEAD_DIM
    n_qi = IDX_HEADS * IDX_DIM
    n_main = n_q + 2 * n_kv + n_qi
    w1 = w_in_odd[0]
    w_tail = jnp.pad(w1[:, n_main:], ((0, 0), (0, LANES - (w1.shape[1] - n_main))))
    main = norm_proj(y, norm_mix[1], bf(w1[:, :n_main]), tm, 512)
    tail = norm_proj(y, norm_mix[1], bf(w_tail), tm, LANES)
    pos = jnp.concatenate([jnp.arange(lp), jnp.tile(past + jnp.arange(ds), nb)])
    kv1, t32, ob = rope_all(main, tail, _rope_tables(pos, C_HEAD_DIM), _rope_tables(pos, IDX_DIM), tm)
    k1 = kv1[:, :n_kv]
    v1 = kv1[:, n_kv:]
    ki1 = t32[:, :IDX_DIM]

    kb_p = 128
    vt = ob[:lp, n_q + n_kv:n_q + 2 * n_kv].reshape(lp // kb_p, kb_p, n_kv).transpose(0, 2, 1)
    wi_t = t32[:lp, IDX_DIM:IDX_DIM + IDX_HEADS].T
    o_p = dsa_t(ob, wi_t, vt, nq=lp, tq=256, kb=kb_p, pb=256, cb=1024, topk=min(TOPK_MAX, lp // 4))
    o_s = dsa_sample(ob, t32, cache_c_k[0], cache_c_v[0], cache_c_idx[0],
                     row0=lp, kb=3 * LANES, topk=min(TOPK_MAX, (past + ds) // 4))

    o = jnp.concatenate([o_p, o_s], axis=0)
    y = mm_res([o], [bf(w_out_odd[0])], y, tm, 1024)

    wr = jnp.pad(moe_router[0], ((0, 0), (0, LANES - N_EXPERTS)))
    h, meta = router(y, norm_ffn[1], wr, tm)
    sup = 1536
    pos_rows, row_token, tile_expert, tile_rows = route_plan(meta, sup)
    n_rows = row_token.shape[0]
    slab = (d // LANES, LANES)
    xs = gather_rows(h.reshape((m,) + slab), row_token, 512)
    ys = grouped_ffn(xs.reshape(n_rows, d), tile_expert, tile_rows,
                     moe_w_gate[0], moe_w_up[0], moe_w_down[0], sup, 256, 512)
    y_p, y_s = combine_final(y, meta, norm_final, ys.reshape((n_rows,) + slab), pos_rows, lp, 256)

    y_prompt = y_p.reshape(1, lp, d)
    y_sample = y_s.reshape(nb, ds, d)
    keep = min(A_BAND, lp)
    heads = lambda t, n: t.reshape(1, n, -1, A_HEADS, A_HEAD_DIM)
    a_k_prompt = heads(k0[lp - keep:lp], 1)
    a_v_prompt = heads(v0[lp - keep:lp], 1)
    pool_prompt_out = u0[lp - B_HIST:lp].reshape(1, 1, B_HIST, B_WIDTH)
    c_k_prompt = k1[:lp].reshape(1, 1, lp, C_KV_HEADS, C_HEAD_DIM)
    c_v_prompt = v1[:lp].reshape(1, 1, lp, C_KV_HEADS, C_HEAD_DIM)
    c_idx_prompt = ki1[:lp].reshape(1, 1, lp, IDX_DIM)
    shift = lambda cache, new: jnp.concatenate(
        [cache[0], new.reshape(nb, ds, A_HEADS, A_HEAD_DIM)], axis=1)[:, ds:][None]
    a_k_sample = shift(cache_a_k, k0[lp:])
    a_v_sample = shift(cache_a_v, v0[lp:])
    pool_sample_out = u_hist[:, ds:][None]
    c_k_sample = k1[lp:].reshape(1, nb, ds, C_KV_HEADS, C_HEAD_DIM)
    c_v_sample = v1[lp:].reshape(1, nb, ds, C_KV_HEADS, C_HEAD_DIM)
    c_idx_sample = ki1[lp:].reshape(1, nb, ds, IDX_DIM)
    return (y_prompt, y_sample, a_k_prompt, a_v_prompt, pool_prompt_out,
            c_k_prompt, c_v_prompt, c_idx_prompt,
            a_k_sample, a_v_sample, pool_sample_out,
            c_k_sample, c_v_sample, c_idx_sample)
```

```python
import functools
import math

import jax
import jax.numpy as jnp
from jax import lax
from jax.experimental import pallas as pl
from jax.experimental.pallas import tpu as pltpu

F32 = jnp.float32
BF16 = jnp.bfloat16

NORM_EPS = 1e-6
NEG = -1e30
BIG = 1e30
SHIFT_SLACK = 1.001
MIN_SOFTMAX_SUM = 2.0 ** -80

CHUNK = 64
A_HEADS = 16
A_HEAD_DIM = 64
A_WIDTH = A_HEADS * A_HEAD_DIM
A_PREV_CHUNKS = 8
A_BAND = (A_PREV_CHUNKS + 1) * CHUNK
A_REL_CLIP = 128
B_WINDOWS = (2, 4, 8, 16)
B_GROUP = 256
B_WIDTH = B_GROUP * len(B_WINDOWS)
B_HIST = max(B_WINDOWS) - 1
C_HEADS = 16
C_KV_HEADS = 4
C_HEAD_DIM = 128
C_GROUPS = C_HEADS // C_KV_HEADS
IDX_HEADS = 8
IDX_DIM = 64
TOPK_MAX = 256
ROPE_THETA = 500000.0
ROPE_FRAC = 4
N_EXPERTS = 8

LANES = 128
POOL_HALO = 16
VMEM_LIMIT = 56 * 1024 * 1024

_NT = (((1,), (1,)), ((), ()))


def _params(*sem):
    return pltpu.CompilerParams(dimension_semantics=sem, vmem_limit_bytes=VMEM_LIMIT)


def _rms(x, g):
    ms = jnp.mean(x * x, axis=-1, keepdims=True)
    return x * lax.rsqrt(ms + NORM_EPS) * g


def _norm_proj_kernel(x_ref, g_ref, w_ref, o_ref, h_ref):
    @pl.when(pl.program_id(1) == 0)
    def _():
        h_ref[...] = _rms(x_ref[...], g_ref[...]).astype(BF16)

    o_ref[...] = jnp.dot(h_ref[...], w_ref[...], preferred_element_type=F32)


def norm_proj(x, g, w, tm, tn):
    m, d = x.shape
    n = w.shape[1]
    return pl.pallas_call(
        _norm_proj_kernel,
        grid=(m // tm, n // tn),
        in_specs=[pl.BlockSpec((tm, d), lambda i, j: (i, 0)),
                  pl.BlockSpec((1, d), lambda i, j: (0, 0)),
                  pl.BlockSpec((d, tn), lambda i, j: (0, j))],
        out_specs=pl.BlockSpec((tm, tn), lambda i, j: (i, j)),
        out_shape=jax.ShapeDtypeStruct((m, n), F32),
        scratch_shapes=[pltpu.VMEM((tm, d), BF16)],
        compiler_params=_params("parallel", "arbitrary"),
    )(x, g.reshape(1, d), w)


def _band_pairs(q_ref, q_row0, nq, kw_ref, vw_ref, k_row0, nk, bias_ref, key_ok, o_ref):
    lane = lax.broadcasted_iota(jnp.int32, (1, LANES), 1)
    first = lane < A_HEAD_DIM
    for hp in range(A_HEADS // 2):
        cs = slice(hp * LANES, (hp + 1) * LANES)
        qp = q_ref[pl.ds(q_row0, nq), cs]
        kp = kw_ref[pl.ds(k_row0, nk), cs]
        vp = vw_ref[pl.ds(k_row0, nk), cs]
        qm = jnp.concatenate([jnp.where(first, qp, 0.0), jnp.where(first, 0.0, qp)], axis=0).astype(BF16)
        s = lax.dot_general(qm, kp, _NT, preferred_element_type=F32)
        s = s * (A_HEAD_DIM ** -0.5) + bias_ref[hp]
        if key_ok is not None:
            s = jnp.where(key_ok, s, NEG)
        mx = jnp.max(s, axis=-1, keepdims=True)
        e = jnp.exp(s - mx)
        l = jnp.sum(e, axis=-1, keepdims=True)
        o = jnp.dot(e.astype(BF16), vp, preferred_element_type=F32) / l
        o_ref[pl.ds(q_row0, nq), cs] = jnp.where(first, o[:nq], o[nq:]).astype(o_ref.dtype)


def _band_prompt_kernel(q_ref, kp_ref, kc_ref, vp_ref, vc_ref, bias_ref, o_ref, kw_ref, vw_ref, *, qb, pad):
    i = pl.program_id(0)
    kw_ref[0:pad, :] = kp_ref[...].astype(BF16)
    kw_ref[pad:pad + qb, :] = kc_ref[...].astype(BF16)
    vw_ref[0:pad, :] = vp_ref[...].astype(BF16)
    vw_ref[pad:pad + qb, :] = vc_ref[...].astype(BF16)
    col = lax.broadcasted_iota(jnp.int32, (1, A_BAND), 1)

    def chunk(cc, carry):
        r0 = pl.multiple_of(cc * CHUNK, CHUNK)
        first_valid = jnp.where(i == 0, pad - cc * CHUNK, 0)
        _band_pairs(q_ref, r0, CHUNK, kw_ref, vw_ref, r0, A_BAND, bias_ref, col >= first_valid, o_ref)
        return carry

    lax.fori_loop(0, qb // CHUNK, chunk, 0)


def band_prompt(proj, bias, lp):
    pad = A_PREV_CHUNKS * CHUNK
    qb = pad
    blk = (qb, A_WIDTH)
    prev = lambda c: (lambda i: (jnp.maximum(i - 1, 0), c))
    cur = lambda c: (lambda i: (i, c))
    return pl.pallas_call(
        functools.partial(_band_prompt_kernel, qb=qb, pad=pad),
        grid=(lp // qb,),
        in_specs=[pl.BlockSpec(blk, cur(0)),
                  pl.BlockSpec(blk, prev(1)), pl.BlockSpec(blk, cur(1)),
                  pl.BlockSpec(blk, prev(2)), pl.BlockSpec(blk, cur(2)),
                  pl.BlockSpec((A_HEADS // 2, 2 * CHUNK, A_BAND), lambda i: (0, 0, 0))],
        out_specs=pl.BlockSpec(blk, lambda i: (i, 0)),
        out_shape=jax.ShapeDtypeStruct((lp, A_WIDTH), BF16),
        scratch_shapes=[pltpu.VMEM((pad + qb, A_WIDTH), BF16), pltpu.VMEM((pad + qb, A_WIDTH), BF16)],
        compiler_params=_params("parallel"),
    )(proj, proj, proj, proj, proj, bias)


def _band_sample_kernel(q_ref, kn_ref, vn_ref, ck_ref, cv_ref, bias_ref, o_ref, kw_ref, vw_ref, *, a_len, ds):
    kw_ref[0:a_len, :] = ck_ref[...].astype(BF16)
    kw_ref[a_len:a_len + ds, :] = kn_ref[...].astype(BF16)
    vw_ref[0:a_len, :] = cv_ref[...].astype(BF16)
    vw_ref[a_len:a_len + ds, :] = vn_ref[...].astype(BF16)
    _band_pairs(q_ref, 0, ds, kw_ref, vw_ref, 0, a_len + ds, bias_ref, None, o_ref)


def band_sample(proj, cache_k, cache_v, bias, lp, nb, ds):
    a_len = cache_k.shape[1]
    row = lambda c: (lambda b: (lp // ds + b, c))
    return pl.pallas_call(
        functools.partial(_band_sample_kernel, a_len=a_len, ds=ds),
        grid=(nb,),
        in_specs=[pl.BlockSpec((ds, A_WIDTH), row(0)),
                  pl.BlockSpec((ds, A_WIDTH), row(1)),
                  pl.BlockSpec((ds, A_WIDTH), row(2)),
                  pl.BlockSpec((None, a_len, A_WIDTH), lambda b: (b, 0, 0)),
                  pl.BlockSpec((None, a_len, A_WIDTH), lambda b: (b, 0, 0)),
                  pl.BlockSpec((A_HEADS // 2, 2 * ds, a_len + ds), lambda b: (0, 0, 0))],
        out_specs=pl.BlockSpec((ds, A_WIDTH), lambda b: (b, 0)),
        out_shape=jax.ShapeDtypeStruct((nb * ds, A_WIDTH), BF16),
        scratch_shapes=[pltpu.VMEM((a_len + ds, A_WIDTH), BF16), pltpu.VMEM((a_len + ds, A_WIDTH), BF16)],
        compiler_params=_params("parallel"),
    )(proj, proj, proj, cache_k, cache_v, bias)


def _rel_bias_tile(rel_bias, q0, nq, k0, nk):
    rel_max = q0 - k0 + nq - 1
    rel = jnp.clip(rel_max - jnp.arange(nq + nk - 1), -A_REL_CLIP, A_REL_CLIP) + A_REL_CLIP
    ext = rel_bias[:, rel].astype(F32)
    return jnp.stack([ext[:, nq - 1 - i:nq - 1 - i + nk] for i in range(nq)], axis=1)


def _pair_rows(bias):
    h, nq, nk = bias.shape
    return bias.reshape(h // 2, 2 * nq, nk)


def _pool_kernel(prev_ref, cur_ref, w_ref, sc_ref, o_ref, ext_ref, *, tm, prompt, pos0):
    i = pl.program_id(0)
    prev = prev_ref[...]
    if prompt:
        prev = jnp.where(i == 0, 0.0, prev)
        pos = i * tm + lax.broadcasted_iota(jnp.int32, (tm, 1), 0)
    else:
        pos = pos0 + lax.broadcasted_iota(jnp.int32, (tm, 1), 0)
    ext_ref[0:POOL_HALO, :] = prev
    ext_ref[POOL_HALO:POOL_HALO + tm, :] = cur_ref[...]
    for g, w in enumerate(B_WINDOWS):
        cs = slice(g * B_GROUP, (g + 1) * B_GROUP)
        tok = ext_ref[POOL_HALO:POOL_HALO + tm, cs]
        tot = tok
        for j in range(1, w):
            tot = tot + ext_ref[POOL_HALO - j:POOL_HALO - j + tm, cs]
        cnt = jnp.minimum(pos + 1, w).astype(F32)
        pooled = (tot / cnt - tok).astype(BF16)
        o = jnp.dot(pooled, w_ref[g], preferred_element_type=F32) * sc_ref[:, cs]
        o_ref[:, cs] = o.astype(o_ref.dtype)


def pool_prompt(proj, pool_w, pool_scale, lp, tm):
    ucol = 3 * A_WIDTH // B_WIDTH
    per = tm // POOL_HALO
    return pl.pallas_call(
        functools.partial(_pool_kernel, tm=tm, prompt=True, pos0=0),
        grid=(lp // tm,),
        in_specs=[pl.BlockSpec((POOL_HALO, B_WIDTH), lambda i: (jnp.maximum(i * per - 1, 0), ucol)),
                  pl.BlockSpec((tm, B_WIDTH), lambda i: (i, ucol)),
                  pl.BlockSpec((len(B_WINDOWS), B_GROUP, B_GROUP), lambda i: (0, 0, 0)),
                  pl.BlockSpec((1, B_WIDTH), lambda i: (0, 0))],
        out_specs=pl.BlockSpec((tm, B_WIDTH), lambda i: (i, 0)),
        out_shape=jax.ShapeDtypeStruct((lp, B_WIDTH), BF16),
        scratch_shapes=[pltpu.VMEM((POOL_HALO + tm, B_WIDTH), F32)],
        compiler_params=_params("parallel"),
    )(proj, proj, pool_w, pool_scale)


def pool_sample(u_ext, pool_w, pool_scale, past):
    nb, tot, _ = u_ext.shape
    ds = tot - POOL_HALO
    return pl.pallas_call(
        functools.partial(_pool_kernel, tm=ds, prompt=False, pos0=past),
        grid=(nb,),
        in_specs=[pl.BlockSpec((None, POOL_HALO, B_WIDTH), lambda b: (b, 0, 0)),
                  pl.BlockSpec((None, ds, B_WIDTH), lambda b: (b, POOL_HALO // ds, 0)),
                  pl.BlockSpec((len(B_WINDOWS), B_GROUP, B_GROUP), lambda b: (0, 0, 0)),
                  pl.BlockSpec((1, B_WIDTH), lambda b: (0, 0))],
        out_specs=pl.BlockSpec((ds, B_WIDTH), lambda b: (b, 0)),
        out_shape=jax.ShapeDtypeStruct((nb * ds, B_WIDTH), BF16),
        scratch_shapes=[pltpu.VMEM((POOL_HALO + ds, B_WIDTH), F32)],
        compiler_params=_params("parallel"),
    )(u_ext, u_ext, pool_w, pool_scale)


def _mm_res_kernel(*refs, n_in):
    xs, ws = refs[:n_in], refs[n_in:2 * n_in]
    res_ref, o_ref = refs[2 * n_in], refs[2 * n_in + 1]
    acc = res_ref[...]
    for x_ref, w_ref in zip(xs, ws):
        acc = acc + jnp.dot(x_ref[...], w_ref[...], preferred_element_type=F32)
    o_ref[...] = acc


def mm_res(xs, ws, res, tm, tn):
    m, n = res.shape
    n_in = len(xs)
    in_specs = ([pl.BlockSpec((tm, x.shape[1]), lambda i, j: (i, 0)) for x in xs]
                + [pl.BlockSpec((w.shape[0], tn), lambda i, j: (0, j)) for w in ws]
                + [pl.BlockSpec((tm, tn), lambda i, j: (i, j))])
    return pl.pallas_call(
        functools.partial(_mm_res_kernel, n_in=n_in),
        grid=(m // tm, n // tn),
        in_specs=in_specs,
        out_specs=pl.BlockSpec((tm, tn), lambda i, j: (i, j)),
        out_shape=jax.ShapeDtypeStruct((m, n), F32),
        compiler_params=_params("parallel", "arbitrary"),
    )(*xs, *ws, res)


def _swiglu_tile(h, wg, wu, wd):
    a = jnp.dot(h, wg, preferred_element_type=F32)
    b = jnp.dot(h, wu, preferred_element_type=F32)
    act = (a * jax.nn.sigmoid(a) * b).astype(BF16)
    return jnp.dot(act, wd, preferred_element_type=F32)


def _ffn_kernel(y_ref, g_ref, wg_ref, wu_ref, wd_ref, o_ref, h_ref, acc_ref):
    f = pl.program_id(1)

    @pl.when(f == 0)
    def _():
        h_ref[...] = _rms(y_ref[...], g_ref[...]).astype(BF16)
        acc_ref[...] = jnp.zeros_like(acc_ref)

    acc_ref[...] += _swiglu_tile(h_ref[...], wg_ref[...], wu_ref[...], wd_ref[...])

    @pl.when(f == pl.num_programs(1) - 1)
    def _():
        o_ref[...] = y_ref[...] + acc_ref[...]


def ffn(y, g, wg, wu, wd, tm, tf):
    m, d = y.shape
    ff = wg.shape[1]
    return pl.pallas_call(
        _ffn_kernel,
        grid=(m // tm, ff // tf),
        in_specs=[pl.BlockSpec((tm, d), lambda i, f: (i, 0)),
                  pl.BlockSpec((1, d), lambda i, f: (0, 0)),
                  pl.BlockSpec((d, tf), lambda i, f: (0, f)),
                  pl.BlockSpec((d, tf), lambda i, f: (0, f)),
                  pl.BlockSpec((tf, d), lambda i, f: (f, 0))],
        out_specs=pl.BlockSpec((tm, d), lambda i, f: (i, 0)),
        out_shape=jax.ShapeDtypeStruct((m, d), F32),
        scratch_shapes=[pltpu.VMEM((tm, d), BF16), pltpu.VMEM((tm, d), F32)],
        compiler_params=_params("parallel", "arbitrary"),
    )(y, g.reshape(1, d), wg, wu, wd)


def _rope_tables(pos, head_dim):
    rot = head_dim // ROPE_FRAC
    half = rot // 2
    inv = jnp.exp(-math.log(ROPE_THETA) * jnp.arange(half, dtype=F32) * (2.0 / rot))
    ang = pos.astype(F32)[:, None] * inv[None, :]
    cos, sin = jnp.cos(ang), jnp.sin(ang)
    m = pos.shape[0]
    one = jnp.ones((m, head_dim - rot), F32)
    zero_r = jnp.zeros((m, head_dim - rot), F32)
    zero_h = jnp.zeros((m, half), F32)
    c = jnp.concatenate([cos, cos, one], axis=1)
    s_dn = jnp.concatenate([-sin, zero_h, zero_r], axis=1)
    s_up = jnp.concatenate([zero_h, sin, zero_r], axis=1)
    rep = LANES // head_dim
    return jnp.stack([jnp.tile(c, (1, rep)), jnp.tile(s_dn, (1, rep)), jnp.tile(s_up, (1, rep))])


def _rot(x, tab_ref, half):
    return (x * tab_ref[0] + pltpu.roll(x, LANES - half, 1) * tab_ref[1]
            + pltpu.roll(x, half, 1) * tab_ref[2])


def _rope_kernel(main_ref, tail_ref, tq_ref, ti_ref, kv_ref, t32_ref, ob_ref,
                 *, n_q, n_k, n_v, n_qi, wi_scale, q_scale):
    half_qk = C_HEAD_DIM // ROPE_FRAC // 2
    half_i = IDX_DIM // ROPE_FRAC // 2
    for c in range(n_q + n_k + n_v + n_qi):
        cs = slice(c * LANES, (c + 1) * LANES)
        x = main_ref[:, cs]
        if c < n_q + n_k:
            x = _rot(x, tq_ref, half_qk)
        elif c >= n_q + n_k + n_v:
            x = _rot(x, ti_ref, half_i)
        if n_q <= c < n_q + n_k + n_v:
            kv_ref[:, (c - n_q) * LANES:(c - n_q + 1) * LANES] = x
        ob_ref[:, cs] = (x * q_scale if c < n_q else x).astype(BF16)
    t = tail_ref[...]
    lane = lax.broadcasted_iota(jnp.int32, (1, LANES), 1)
    r = _rot(t, ti_ref, half_i)
    t32_ref[...] = jnp.where(lane < IDX_DIM, r, t * wi_scale)
    c = n_q + n_k + n_v + n_qi
    ob_ref[:, c * LANES:(c + 1) * LANES] = jnp.where(lane < IDX_DIM, r, pltpu.roll(r, IDX_DIM, 1)).astype(BF16)


def rope_all(main, tail, tab_qk, tab_idx, tm):
    m, nmain = main.shape
    n_q = C_HEADS * C_HEAD_DIM // LANES
    n_k = C_KV_HEADS * C_HEAD_DIM // LANES
    n_qi = IDX_HEADS * IDX_DIM // LANES
    wi_scale = (IDX_HEADS ** -0.5) * (IDX_DIM ** -0.5)
    q_scale = (C_HEAD_DIM ** -0.5) * math.log2(math.e)
    row = lambda i: (i, 0)
    return pl.pallas_call(
        functools.partial(_rope_kernel, n_q=n_q, n_k=n_k, n_v=n_k, n_qi=n_qi, wi_scale=wi_scale, q_scale=q_scale),
        grid=(m // tm,),
        in_specs=[pl.BlockSpec((tm, nmain), row),
                  pl.BlockSpec((tm, LANES), row),
                  pl.BlockSpec((3, tm, LANES), lambda i: (0, i, 0)),
                  pl.BlockSpec((3, tm, LANES), lambda i: (0, i, 0))],
        out_specs=[pl.BlockSpec((tm, 2 * n_k * LANES), row),
                   pl.BlockSpec((tm, LANES), row),
                   pl.BlockSpec((tm, nmain + LANES), row)],
        out_shape=[jax.ShapeDtypeStruct((m, 2 * n_k * LANES), F32),
                   jax.ShapeDtypeStruct((m, LANES), F32),
                   jax.ShapeDtypeStruct((m, nmain + LANES), BF16)],
        compiler_params=_params("parallel"),
    )(main, tail, tab_qk, tab_idx)


def _dsa_kernel(q_ref, qi_ref, wi_ref, kn_ref, vn_ref, kin_ref, ck_ref, cv_ref, ci_ref, o_ref,
                k_ref, v_ref, ki_ref, sc_ref, qs_ref, acc_ref, m_ref, l_ref,
                *, tq, kb, past, n_valid, q_pos0, topk):
    i = pl.program_id(1)
    rows = C_GROUPS * tq
    sub = kb // LANES
    kf = float(topk)

    new = past + tq
    ci = ci_ref[...].astype(BF16)
    k_ref[0:past, :] = ck_ref[...].astype(BF16)
    v_ref[0:past, :] = cv_ref[...].astype(BF16)
    ki_ref[0:past, :] = jnp.concatenate([ci, ci], axis=1)
    k_ref[past:new, :] = kn_ref[...]
    v_ref[past:new, :] = vn_ref[...]
    ki_ref[past:new, :] = kin_ref[...]
    n_pad = k_ref.shape[0] - new
    k_ref[new:, :] = jnp.zeros((n_pad, k_ref.shape[1]), BF16)
    v_ref[new:, :] = jnp.zeros((n_pad, v_ref.shape[1]), BF16)
    ki_ref[new:, :] = jnp.zeros((n_pad, ki_ref.shape[1]), BF16)

    q_pos = q_pos0 + i * tq + lax.broadcasted_iota(jnp.int32, (tq, 1), 0)
    q_chunk = q_pos // CHUNK
    last_chunk = (q_pos0 + i * tq + tq - 1) // CHUNK
    kv_limit = jnp.minimum(n_valid, (last_chunk + 1) * CHUNK)
    nkb = (kv_limit + kb - 1) // kb

    lane = lax.broadcasted_iota(jnp.int32, (1, LANES), 1)
    first = lane < IDX_DIM
    wi = wi_ref[:, IDX_DIM:IDX_DIM + IDX_HEADS]

    def score_block(b, carry):
        for c in range(sub):
            off = pl.multiple_of(b * kb + c * LANES, LANES)
            kib = ki_ref[pl.ds(off, LANES), :]
            acc = jnp.zeros((tq, LANES), F32)
            for hp in range(IDX_HEADS // 2):
                qp = qi_ref[:, hp * LANES:(hp + 1) * LANES]
                for half in range(2):
                    keep = first if half == 0 else jnp.logical_not(first)
                    qm = jnp.where(keep, qp, jnp.zeros_like(qp))
                    d = lax.dot_general(qm, kib, _NT, preferred_element_type=F32)
                    h = hp * 2 + half
                    acc = acc + jnp.maximum(d, 0.0) * wi[:, h:h + 1]
            k_pos = off + lane
            adm = jnp.logical_and(k_pos // CHUNK <= q_chunk, k_pos < n_valid)
            sc_ref[b, :, c * LANES:(c + 1) * LANES] = jnp.where(adm, acc, NEG)
        return carry

    lax.fori_loop(0, nkb, score_block, 0)

    def lane_sum(x):
        return jnp.sum(x, axis=1, keepdims=True)

    def count_ge(t):
        def body(b, acc):
            for c in range(sub):
                blk = sc_ref[b, :, c * LANES:(c + 1) * LANES]
                acc = acc + jnp.where(blk >= t, 1.0, 0.0)
            return acc
        return lane_sum(lax.fori_loop(0, nkb, body, jnp.zeros((tq, LANES), F32)))

    def stats(b, carry):
        mx, mn, cnt = carry
        for c in range(sub):
            blk = sc_ref[b, :, c * LANES:(c + 1) * LANES]
            ok = blk > 0.5 * NEG
            mx = jnp.maximum(mx, blk)
            mn = jnp.minimum(mn, jnp.where(ok, blk, BIG))
            cnt = cnt + jnp.where(ok, 1.0, 0.0)
        return mx, mn, cnt

    mx, mn, cnt = lax.fori_loop(
        0, nkb, stats,
        (jnp.full((tq, LANES), NEG, F32), jnp.full((tq, LANES), BIG, F32), jnp.zeros((tq, LANES), F32)))
    row_max = jnp.max(mx, axis=1, keepdims=True)
    row_min = jnp.min(mn, axis=1, keepdims=True)
    n_adm = lane_sum(cnt)

    done0 = jnp.where(n_adm <= kf, 1.0, 0.0)
    state0 = (row_min, jnp.full((tq, 1), BIG, F32), row_max, jnp.full((tq, 1), 0.5 * NEG, F32), done0)

    def bisect(state, n_steps):
        def cond(c):
            it, st = c
            return jnp.logical_and(it < n_steps, jnp.min(st[4]) < 0.5)

        def body(c):
            it, (lo, hi, mid, thr, done) = c
            cnt = count_ge(mid)
            live = done < 0.5
            hit = jnp.logical_and(live, cnt == kf)
            ge = cnt >= kf
            thr = jnp.where(hit, mid, thr)
            done = jnp.where(hit, 1.0, done)
            lo = jnp.where(ge, mid, lo)
            hi = jnp.where(ge, hi, mid)
            return it + 1, (lo, hi, 0.5 * (lo + hi), thr, done)

        return lax.while_loop(cond, body, (jnp.int32(0), state))[1]

    def snap(state):
        lo, hi, mid, thr, done = state

        def body(b, carry):
            v_lo, v_hi = carry
            for c in range(sub):
                blk = sc_ref[b, :, c * LANES:(c + 1) * LANES]
                v_lo = jnp.minimum(v_lo, jnp.where(blk >= lo, blk, BIG))
                v_hi = jnp.maximum(v_hi, jnp.where(blk < hi, blk, NEG))
            return v_lo, v_hi

        v_lo, v_hi = lax.fori_loop(0, nkb, body,
                                   (jnp.full((tq, LANES), BIG, F32), jnp.full((tq, LANES), NEG, F32)))
        v_lo = jnp.min(v_lo, axis=1, keepdims=True)
        v_hi = jnp.max(v_hi, axis=1, keepdims=True)
        live = done < 0.5
        tie = jnp.logical_and(live, v_lo == v_hi)
        thr = jnp.where(tie, v_lo, thr)
        done = jnp.where(tie, 1.0, done)
        lo = jnp.where(live, v_lo, lo)
        return (lo, hi, 0.5 * (lo + hi), thr, done), jnp.where(tie, 1.0, 0.0)

    state = bisect(state0, 32)

    def refine_cond(c):
        rounds, st, _ = c
        return jnp.logical_and(rounds < 10, jnp.min(st[4]) < 0.5)

    def refine_body(c):
        rounds, st, tie = c
        st, new_tie = snap(st)
        st = bisect(st, 32)
        return rounds + 1, st, jnp.maximum(tie, new_tie)

    _, state, tie = lax.while_loop(refine_cond, refine_body,
                                   (jnp.int32(0), state, jnp.zeros((tq, 1), F32)))
    thr = state[3]
    any_tie = jnp.max(tie) > 0.5

    @pl.when(jnp.logical_not(any_tie))
    def _():
        def body(b, carry):
            for c in range(sub):
                cs = slice(c * LANES, (c + 1) * LANES)
                sc_ref[b, :, cs] = jnp.where(sc_ref[b, :, cs] >= thr, 0.0, NEG)
            return carry
        lax.fori_loop(0, nkb, body, 0)

    @pl.when(any_tie)
    def _():
        def gt_body(b, acc):
            for c in range(sub):
                acc = acc + jnp.where(sc_ref[b, :, c * LANES:(c + 1) * LANES] > thr, 1.0, 0.0)
            return acc
        need = kf - lane_sum(lax.fori_loop(0, nkb, gt_body, jnp.zeros((tq, LANES), F32)))

        def count_eq_upto(j):
            def body(b, acc):
                for c in range(sub):
                    blk = sc_ref[b, :, c * LANES:(c + 1) * LANES]
                    idx = (b * kb + c * LANES + lane).astype(F32)
                    acc = acc + jnp.where(jnp.logical_and(blk == thr, idx <= j), 1.0, 0.0)
                return acc
            return lane_sum(lax.fori_loop(0, nkb, body, jnp.zeros((tq, LANES), F32)))

        def idx_body(_, c):
            lo_j, hi_j = c
            mid_j = jnp.floor(0.5 * (lo_j + hi_j))
            ok = count_eq_upto(mid_j) >= need
            return jnp.where(ok, lo_j, mid_j), jnp.where(ok, mid_j, hi_j)

        n_steps = max(1, math.ceil(math.log2(sc_ref.shape[0] * kb + 1)))
        _, last = lax.fori_loop(
            0, n_steps, idx_body,
            (jnp.full((tq, 1), -1.0, F32), jnp.full((tq, 1), float(sc_ref.shape[0] * kb), F32)))
        last = jnp.where(tie > 0.5, last, BIG)

        def body(b, carry):
            for c in range(sub):
                cs = slice(c * LANES, (c + 1) * LANES)
                blk = sc_ref[b, :, cs]
                idx = (b * kb + c * LANES + lane).astype(F32)
                sel = jnp.logical_or(blk > thr, jnp.logical_and(blk == thr, idx <= last))
                sc_ref[b, :, cs] = jnp.where(sel, 0.0, NEG)
            return carry
        lax.fori_loop(0, nkb, body, 0)

    for g in range(C_KV_HEADS):
        for hh in range(C_GROUPS):
            h = g * C_GROUPS + hh
            qs_ref[g, hh * tq:(hh + 1) * tq, :] = q_ref[:, h * C_HEAD_DIM:(h + 1) * C_HEAD_DIM]
    m_ref[...] = jnp.full(m_ref.shape, NEG, F32)
    l_ref[...] = jnp.zeros(l_ref.shape, F32)
    acc_ref[...] = jnp.zeros(acc_ref.shape, F32)

    def attend(b, carry):
        off = pl.multiple_of(b * kb, kb)
        bias = sc_ref[b]
        bias = jnp.concatenate([bias] * C_GROUPS, axis=0)
        for g in range(C_KV_HEADS):
            cs = slice(g * C_HEAD_DIM, (g + 1) * C_HEAD_DIM)
            kblk = k_ref[pl.ds(off, kb), cs]
            vblk = v_ref[pl.ds(off, kb), cs]
            s = lax.dot_general(qs_ref[g], kblk, _NT, preferred_element_type=F32)
            s = s + bias
            m_old = m_ref[g]
            m_new = jnp.maximum(m_old, jnp.max(s, axis=1, keepdims=True))
            alpha = jnp.exp2(m_old - m_new)
            p = jnp.exp2(s - m_new)
            l_ref[g] = alpha * l_ref[g] + jnp.sum(p, axis=1, keepdims=True)
            acc_ref[g] = alpha * acc_ref[g] + jnp.dot(p.astype(BF16), vblk, preferred_element_type=F32)
            m_ref[g] = m_new
        return carry

    lax.fori_loop(0, nkb, attend, 0)

    for g in range(C_KV_HEADS):
        o = acc_ref[g] / l_ref[g]
        for hh in range(C_GROUPS):
            h = g * C_GROUPS + hh
            o_ref[:, h * C_HEAD_DIM:(h + 1) * C_HEAD_DIM] = o[hh * tq:(hh + 1) * tq, :].astype(o_ref.dtype)


def dsa_sample(ob, t32, cache_k, cache_v, cache_i, *, row0, kb, topk):
    nb, past = cache_k.shape[:2]
    n_kv = C_KV_HEADS * C_HEAD_DIM
    ds = (ob.shape[0] - row0) // nb
    n_q = C_HEADS * C_HEAD_DIM
    n_qi = IDX_HEADS * IDX_DIM
    s_all = past + ds
    s_pad = -(-s_all // kb) * kb
    rows = C_GROUPS * ds
    col = lambda w, off: (lambda b, i: (row0 // ds + b, off // w))
    cmap = lambda b, i: (b, 0, 0)
    return pl.pallas_call(
        functools.partial(_dsa_kernel, tq=ds, kb=kb, past=past, n_valid=s_all, q_pos0=past, topk=topk),
        grid=(nb, 1),
        in_specs=[pl.BlockSpec((ds, n_q), col(n_q, 0)),
                  pl.BlockSpec((ds, n_qi), col(n_qi, n_q + 2 * n_kv)),
                  pl.BlockSpec((ds, LANES), col(LANES, 0)),
                  pl.BlockSpec((ds, n_kv), col(n_kv, n_q)),
                  pl.BlockSpec((ds, n_kv), col(n_kv, n_q + n_kv)),
                  pl.BlockSpec((ds, LANES), col(LANES, n_q + 2 * n_kv + n_qi)),
                  pl.BlockSpec((None, past, n_kv), cmap),
                  pl.BlockSpec((None, past, n_kv), cmap),
                  pl.BlockSpec((None, past, IDX_DIM), cmap)],
        out_specs=pl.BlockSpec((ds, n_q), lambda b, i: (b, 0)),
        out_shape=jax.ShapeDtypeStruct((nb * ds, n_q), BF16),
        scratch_shapes=[pltpu.VMEM((s_pad, n_kv), BF16),
                        pltpu.VMEM((s_pad, n_kv), BF16),
                        pltpu.VMEM((s_pad, LANES), BF16),
                        pltpu.VMEM((s_pad // kb, ds, kb), F32),
                        pltpu.VMEM((C_KV_HEADS, rows, C_HEAD_DIM), BF16),
                        pltpu.VMEM((C_KV_HEADS, rows, C_HEAD_DIM), F32),
                        pltpu.VMEM((C_KV_HEADS, rows, 1), F32),
                        pltpu.VMEM((C_KV_HEADS, rows, 1), F32)],
        compiler_params=_params("parallel", "arbitrary"),
    )(ob, ob, t32, ob, ob, ob, cache_k, cache_v, cache_i)


def _dsa_t_kernel(q_ref, qi_ref, wi_ref, k_ref, vt_ref, ki_ref, o_ref,
                  sc_ref, qim_ref, acc_ref, m_ref, l_ref, l8_ref, kmax_ref,
                  *, tq, kb, qb, pb, cb, n_valid, q_pos0, topk):
    i = pl.program_id(1)
    kf = float(topk)

    q_pos = q_pos0 + i * tq + lax.broadcasted_iota(jnp.int32, (1, tq), 1)
    q_chunk = q_pos // CHUNK
    last_chunk = (q_pos0 + i * tq + tq - 1) // CHUNK
    kv_limit = jnp.minimum(n_valid, (last_chunk + 1) * CHUNK)
    nkb = (kv_limit + kb - 1) // kb
    ncb = (kv_limit + cb - 1) // cb

    def col_reduce(x, op):
        groups = x.shape[0] // 8
        chains = 8 if groups % 8 == 0 else 1
        return op(op(x.reshape(chains, groups // chains, 8, tq), axis=1), axis=0)

    col_sum = lambda x: col_reduce(x, jnp.sum)
    col_max = lambda x: col_reduce(x, jnp.max)
    col_min = lambda x: col_reduce(x, jnp.min)

    lane = lax.broadcasted_iota(jnp.int32, (1, LANES), 1)
    first = lane < IDX_DIM
    for hp in range(IDX_HEADS // 2):
        qp = qi_ref[:, hp * LANES:(hp + 1) * LANES]
        qim_ref[2 * hp] = jnp.where(first, qp, jnp.zeros_like(qp))
        qim_ref[2 * hp + 1] = jnp.where(first, jnp.zeros_like(qp), qp)
    block_iota = lax.broadcasted_iota(jnp.int32, (pb, 1), 0)

    def score_block(b, carry):
        off = pl.multiple_of(b * pb, pb)
        kib = ki_ref[pl.ds(off, pb), :]
        acc = jnp.zeros((pb, tq), F32)
        for h in range(IDX_HEADS):
            d = lax.dot_general(kib, qim_ref[h], _NT, preferred_element_type=F32)
            acc = acc + jnp.maximum(d, 0.0) * wi_ref[h:h + 1, :]
        k_pos = off + block_iota
        adm = jnp.logical_and(k_pos // CHUNK <= q_chunk, k_pos < n_valid)
        sc_ref[pl.ds(off, pb), :] = jnp.where(adm, acc, NEG)
        return carry

    lax.fori_loop(0, ncb * (cb // pb), score_block, 0)
    key_iota = lax.broadcasted_iota(jnp.int32, (cb, 1), 0)

    def blocks(body, init):
        def step(b, carry):
            off = pl.multiple_of(b * cb, cb)
            return body(off, sc_ref[pl.ds(off, cb), :], carry)
        return lax.fori_loop(0, ncb, step, init)

    def total(x):
        return jnp.sum(x, axis=0, keepdims=True)

    def count_ge(t):
        return total(blocks(lambda off, blk, acc: acc + col_sum(jnp.where(blk >= t, 1.0, 0.0)),
                            jnp.zeros((8, tq), F32)))

    def stats(off, blk, carry):
        mx, mn, cnt = carry
        ok = blk > 0.5 * NEG
        return (jnp.maximum(mx, col_max(blk)), jnp.minimum(mn, col_min(jnp.where(ok, blk, BIG))),
                cnt + col_sum(jnp.where(ok, 1.0, 0.0)))

    mx, mn, cnt = blocks(stats, (jnp.full((8, tq), NEG, F32), jnp.full((8, tq), BIG, F32),
                                 jnp.zeros((8, tq), F32)))
    row_max = jnp.max(mx, axis=0, keepdims=True)
    row_min = jnp.min(mn, axis=0, keepdims=True)
    n_adm = total(cnt)

    done0 = jnp.where(n_adm <= kf, 1.0, 0.0)
    state0 = (row_min, jnp.full((1, tq), BIG, F32), row_max, jnp.full((1, tq), 0.5 * NEG, F32), done0)

    def bisect(state, n_steps):
        def cond(c):
            it, st = c
            return jnp.logical_and(it < n_steps, jnp.min(st[4]) < 0.5)

        def body(c):
            it, (lo, hi, mid, thr, done) = c
            cnt = count_ge(mid)
            hit = jnp.logical_and(done < 0.5, cnt == kf)
            ge = cnt >= kf
            thr = jnp.where(hit, mid, thr)
            done = jnp.where(hit, 1.0, done)
            lo = jnp.where(ge, mid, lo)
            hi = jnp.where(ge, hi, mid)
            return it + 1, (lo, hi, 0.5 * (lo + hi), thr, done)

        return lax.while_loop(cond, body, (jnp.int32(0), state))[1]

    def snap(state):
        lo, hi, mid, thr, done = state

        def body(off, blk, carry):
            v_lo, v_hi = carry
            return (jnp.minimum(v_lo, col_min(jnp.where(blk >= lo, blk, BIG))),
                    jnp.maximum(v_hi, col_max(jnp.where(blk < hi, blk, NEG))))

        v_lo, v_hi = blocks(body, (jnp.full((8, tq), BIG, F32), jnp.full((8, tq), NEG, F32)))
        v_lo = jnp.min(v_lo, axis=0, keepdims=True)
        v_hi = jnp.max(v_hi, axis=0, keepdims=True)
        live = done < 0.5
        tie = jnp.logical_and(live, v_lo == v_hi)
        thr = jnp.where(tie, v_lo, thr)
        done = jnp.where(tie, 1.0, done)
        lo = jnp.where(live, v_lo, lo)
        return (lo, hi, 0.5 * (lo + hi), thr, done), jnp.where(tie, 1.0, 0.0)

    state = bisect(state0, 32)

    def refine_cond(c):
        rounds, st, _ = c
        return jnp.logical_and(rounds < 10, jnp.min(st[4]) < 0.5)

    def refine_body(c):
        rounds, st, tie = c
        st, new_tie = snap(st)
        st = bisect(st, 32)
        return rounds + 1, st, jnp.maximum(tie, new_tie)

    _, state, tie = lax.while_loop(refine_cond, refine_body,
                                   (jnp.int32(0), state, jnp.zeros((1, tq), F32)))
    thr = state[3]
    any_tie = jnp.max(tie) > 0.5

    @pl.when(jnp.logical_not(any_tie))
    def _():
        def body(off, blk, carry):
            sc_ref[pl.ds(off, cb), :] = jnp.where(blk >= thr, 0.0, NEG)
            return carry
        blocks(body, 0)

    @pl.when(any_tie)
    def _():
        need = kf - total(blocks(lambda off, blk, acc: acc + col_sum(jnp.where(blk > thr, 1.0, 0.0)),
                                 jnp.zeros((8, tq), F32)))

        def count_eq_upto(j):
            def body(off, blk, acc):
                idx = (off + key_iota).astype(F32)
                return acc + col_sum(jnp.where(jnp.logical_and(blk == thr, idx <= j), 1.0, 0.0))
            return total(blocks(body, jnp.zeros((8, tq), F32)))

        def idx_body(_, c):
            lo_j, hi_j = c
            mid_j = jnp.floor(0.5 * (lo_j + hi_j))
            ok = count_eq_upto(mid_j) >= need
            return jnp.where(ok, lo_j, mid_j), jnp.where(ok, mid_j, hi_j)

        n_keys = sc_ref.shape[0]
        n_steps = max(1, math.ceil(math.log2(n_keys + 1)))
        _, last = lax.fori_loop(0, n_steps, idx_body,
                                (jnp.full((1, tq), -1.0, F32), jnp.full((1, tq), float(n_keys), F32)))
        last = jnp.where(tie > 0.5, last, BIG)

        def body(off, blk, carry):
            idx = (off + key_iota).astype(F32)
            sel = jnp.logical_or(blk > thr, jnp.logical_and(blk == thr, idx <= last))
            sc_ref[pl.ds(off, cb), :] = jnp.where(sel, 0.0, NEG)
            return carry
        blocks(body, 0)

    halves = kb // qb

    def head_scores(off, h):
        gs = slice((h // C_GROUPS) * C_HEAD_DIM, (h // C_GROUPS + 1) * C_HEAD_DIM)
        out = []
        for j in range(halves):
            rows = pl.ds(off + j * qb, qb)
            s = lax.dot_general(k_ref[rows, gs], q_ref[:, h * C_HEAD_DIM:(h + 1) * C_HEAD_DIM],
                                _NT, preferred_element_type=F32)
            out.append(s + sc_ref[rows, :])
        return out, gs

    @pl.when(i == 0)
    def _():
        for g in range(C_KV_HEADS):
            gs = slice(g * C_HEAD_DIM, (g + 1) * C_HEAD_DIM)

            def body(b, mx):
                kk = k_ref[pl.ds(pl.multiple_of(b * cb, cb), cb), gs].astype(F32)
                return jnp.maximum(mx, jnp.sum(kk * kk, axis=1, keepdims=True))

            mx = lax.fori_loop(0, k_ref.shape[0] // cb, body, jnp.zeros((cb, 1), F32))
            kmax_ref[g] = jnp.full((1, tq), 1.0, F32) * jnp.max(mx)

    ones = jnp.ones((8, C_HEAD_DIM), F32)
    for h in range(C_HEADS):
        qh = q_ref[:, h * C_HEAD_DIM:(h + 1) * C_HEAD_DIM].astype(F32)
        qn2 = lax.dot_general(ones, qh * qh, _NT, preferred_element_type=F32,
                              precision=lax.Precision.HIGHEST)[0:1, :]
        m_ref[h] = jnp.sqrt(qn2 * kmax_ref[h // C_GROUPS]) * SHIFT_SLACK
    l8_ref[...] = jnp.zeros(l8_ref.shape, F32)
    acc_ref[...] = jnp.zeros(acc_ref.shape, F32)

    def attend_fixed(b, carry):
        off = pl.multiple_of(b * kb, kb)
        for h in range(C_HEADS):
            ss, gs = head_scores(off, h)
            ps = [jnp.exp2(s - m_ref[h]) for s in ss]
            l8_ref[h] += sum(col_sum(p) for p in ps)
            p_all = jnp.concatenate([p.astype(BF16) for p in ps], axis=0)
            acc_ref[h] += jnp.dot(vt_ref[b, gs, :], p_all, preferred_element_type=F32)
        return carry

    lax.fori_loop(0, nkb, attend_fixed, 0)
    l_min = jnp.full((1, tq), BIG, F32)
    for h in range(C_HEADS):
        l_ref[h] = total(l8_ref[h])
        l_min = jnp.minimum(l_min, l_ref[h])

    @pl.when(jnp.logical_not(jnp.min(l_min) >= MIN_SOFTMAX_SUM))
    def _():
        m_ref[...] = jnp.full(m_ref.shape, NEG, F32)
        l_ref[...] = jnp.zeros(l_ref.shape, F32)
        acc_ref[...] = jnp.zeros(acc_ref.shape, F32)

        def attend(b, carry):
            off = pl.multiple_of(b * kb, kb)
            for h in range(C_HEADS):
                ss, gs = head_scores(off, h)
                m_old = m_ref[h]
                m_new = m_old
                for s in ss:
                    m_new = jnp.maximum(m_new, jnp.max(col_max(s), axis=0, keepdims=True))
                alpha = jnp.exp2(m_old - m_new)
                ps = [jnp.exp2(s - m_new) for s in ss]
                l_ref[h] = alpha * l_ref[h] + total(sum(col_sum(p) for p in ps))
                p_all = jnp.concatenate([p.astype(BF16) for p in ps], axis=0)
                acc_ref[h] = alpha * acc_ref[h] + jnp.dot(vt_ref[b, gs, :], p_all,
                                                          preferred_element_type=F32)
                m_ref[h] = m_new
            return carry

        lax.fori_loop(0, nkb, attend, 0)

    for h in range(C_HEADS):
        o = acc_ref[h] / l_ref[h]
        o_ref[:, h * C_HEAD_DIM:(h + 1) * C_HEAD_DIM] = o.T.astype(o_ref.dtype)


def dsa_t(ob, wi_t, vt, *, nq, tq, kb, qb, pb, cb, topk):
    dq = C_HEADS * C_HEAD_DIM
    n_kv = C_KV_HEADS * C_HEAD_DIM
    n_qi = IDX_HEADS * IDX_DIM
    s_pad = nq
    return pl.pallas_call(
        functools.partial(_dsa_t_kernel, tq=tq, kb=kb, qb=qb, pb=pb, cb=cb, n_valid=nq, q_pos0=0, topk=topk),
        grid=(1, nq // tq),
        in_specs=[pl.BlockSpec((tq, dq), lambda b, i: (i, 0)),
                  pl.BlockSpec((tq, n_qi), lambda b, i: (i, (dq + 2 * n_kv) // n_qi)),
                  pl.BlockSpec((IDX_HEADS, tq), lambda b, i: (0, i)),
                  pl.BlockSpec((nq, n_kv), lambda b, i: (0, dq // n_kv)),
                  pl.BlockSpec(vt.shape, lambda b, i: (0, 0, 0)),
                  pl.BlockSpec((nq, LANES), lambda b, i: (0, (dq + 2 * n_kv + n_qi) // LANES))],
        out_specs=pl.BlockSpec((tq, dq), lambda b, i: (i, 0)),
        out_shape=jax.ShapeDtypeStruct((nq, dq), BF16),
        scratch_shapes=[pltpu.VMEM((s_pad, tq), F32),
                        pltpu.VMEM((IDX_HEADS, tq, LANES), BF16),
                        pltpu.VMEM((C_HEADS, C_HEAD_DIM, tq), F32),
                        pltpu.VMEM((C_HEADS, 1, tq), F32),
                        pltpu.VMEM((C_HEADS, 1, tq), F32),
                        pltpu.VMEM((C_HEADS, 8, tq), F32),
                        pltpu.VMEM((C_KV_HEADS, 1, tq), F32)],
        compiler_params=_params("arbitrary", "arbitrary"),
    )(ob, ob, wi_t, ob, vt, ob)


def _router_kernel(y_ref, g_ref, wr_ref, h_ref, gate_ref):
    h = _rms(y_ref[...], g_ref[...])
    h_ref[...] = h.astype(BF16)
    logits = jnp.dot(h, wr_ref[...], preferred_element_type=F32, precision=lax.Precision.HIGHEST)
    lane = lax.broadcasted_iota(jnp.int32, logits.shape, 1)
    lg = jnp.where(lane < N_EXPERTS, logits, NEG)
    m1 = jnp.max(lg, axis=1, keepdims=True)
    i1 = jnp.min(jnp.where(lg == m1, lane, LANES), axis=1, keepdims=True)
    lg2 = jnp.where(lane == i1, NEG, lg)
    m2 = jnp.max(lg2, axis=1, keepdims=True)
    i2 = jnp.min(jnp.where(lg2 == m2, lane, LANES), axis=1, keepdims=True)
    e = jnp.exp(m2 - m1)
    g1 = 1.0 / (1.0 + e)
    g2 = e / (1.0 + e)
    meta = jnp.where(lane == 0, i1.astype(F32), jnp.where(lane == 1, i2.astype(F32),
                     jnp.where(lane == 2, g1, jnp.where(lane == 3, g2, 0.0))))
    gate_ref[...] = meta


def router(y, g, w_router_pad, tm):
    m, d = y.shape
    return pl.pallas_call(
        _router_kernel,
        grid=(m // tm,),
        in_specs=[pl.BlockSpec((tm, d), lambda i: (i, 0)),
                  pl.BlockSpec((1, d), lambda i: (0, 0)),
                  pl.BlockSpec((d, LANES), lambda i: (0, 0))],
        out_specs=[pl.BlockSpec((tm, d), lambda i: (i, 0)),
                   pl.BlockSpec((tm, LANES), lambda i: (i, 0))],
        out_shape=[jax.ShapeDtypeStruct((m, d), BF16), jax.ShapeDtypeStruct((m, LANES), F32)],
        compiler_params=_params("parallel"),
    )(y, g.reshape(1, d), w_router_pad)


def route_plan(meta, sup):
    m = meta.shape[0]
    n_tiles = (2 * m) // sup + N_EXPERTS
    e_all = jnp.concatenate([meta[:, 0], meta[:, 1]]).astype(jnp.int32)
    onehot = (e_all[:, None] == jnp.arange(N_EXPERTS)[None, :]).astype(jnp.int32)
    rank = jnp.sum((jnp.cumsum(onehot, axis=0) - onehot) * onehot, axis=1)
    counts = jnp.sum(onehot, axis=0)
    n_super = (counts + sup - 1) // sup
    super_end = jnp.cumsum(n_super)
    super_start = super_end - n_super
    pos = (super_start * sup)[e_all] + rank
    tiles = jnp.arange(n_tiles)
    used = super_end[-1]
    t_eff = jnp.minimum(tiles, used - 1)
    tile_expert = jnp.minimum(jnp.searchsorted(super_end, t_eff, side="right"), N_EXPERTS - 1).astype(jnp.int32)
    rows = jnp.clip(counts[tile_expert] - (t_eff - super_start[tile_expert]) * sup, 0, sup)
    tile_rows = jnp.where(tiles < used, rows, 0).astype(jnp.int32)
    token = jnp.concatenate([jnp.arange(m), jnp.arange(m)]).astype(jnp.int32)
    row_token = (jnp.arange(n_tiles * sup, dtype=jnp.int32) % m).at[pos].set(token)
    return pos.astype(jnp.int32), row_token, tile_expert, tile_rows


def _gather_rows_kernel(tok_ref, h_ref, o_ref, sem, *, batch):
    base = pl.program_id(0) * batch

    def issue(j, carry):
        pltpu.make_async_copy(h_ref.at[tok_ref[base + j]], o_ref.at[j], sem).start()
        return carry

    lax.fori_loop(0, batch, issue, 0)
    pltpu.make_async_copy(o_ref, o_ref, sem).wait()


def gather_rows(h3, row_token, batch):
    n_rows = row_token.shape[0]
    blk = (batch,) + h3.shape[1:]
    return pl.pallas_call(
        functools.partial(_gather_rows_kernel, batch=batch),
        grid_spec=pltpu.PrefetchScalarGridSpec(
            num_scalar_prefetch=1,
            grid=(n_rows // batch,),
            in_specs=[pl.BlockSpec(memory_space=pl.ANY)],
            out_specs=pl.BlockSpec(blk, lambda i, tok: (i, 0, 0)),
            scratch_shapes=[pltpu.SemaphoreType.DMA(())]),
        out_shape=jax.ShapeDtypeStruct((n_rows,) + h3.shape[1:], h3.dtype),
        compiler_params=_params("arbitrary"),
    )(row_token, h3)


def _grouped_ffn_kernel(te_ref, tr_ref, x_ref, wg_ref, wu_ref, wd_ref, o_ref,
                        wgb_ref, wub_ref, wdb_ref, *, sup, sub):
    t = pl.program_id(0)
    f = pl.program_id(1)
    rows = tr_ref[t]

    @pl.when(rows > 0)
    def _():
        wgb_ref[...] = wg_ref[...].astype(BF16)
        wub_ref[...] = wu_ref[...].astype(BF16)
        wdb_ref[...] = wd_ref[...].astype(BF16)

    for s in range(sup // sub):
        sl = slice(s * sub, (s + 1) * sub)

        @pl.when(jnp.logical_and(s * sub >= rows, f == 0))
        def _():
            o_ref[sl, :] = jnp.zeros((sub, o_ref.shape[1]), F32)

        @pl.when(s * sub < rows)
        def _():
            part = _swiglu_tile(x_ref[sl, :], wgb_ref[...], wub_ref[...], wdb_ref[...])

            @pl.when(f == 0)
            def _():
                o_ref[sl, :] = part

            @pl.when(f > 0)
            def _():
                o_ref[sl, :] += part


def grouped_ffn(xs, tile_expert, tile_rows, wg, wu, wd, sup, sub, tf):
    n_rows, d = xs.shape
    ff = wg.shape[2]
    n_f = ff // tf
    fidx = lambda t, f, te, tr: jnp.where(tr[t] > 0, f, n_f - 1)
    return pl.pallas_call(
        functools.partial(_grouped_ffn_kernel, sup=sup, sub=sub),
        grid_spec=pltpu.PrefetchScalarGridSpec(
            num_scalar_prefetch=2,
            grid=(n_rows // sup, n_f),
            in_specs=[pl.BlockSpec((sup, d), lambda t, f, te, tr: (t, 0), pipeline_mode=pl.Buffered(1)),
                      pl.BlockSpec((None, d, tf), lambda t, f, te, tr: (te[t], 0, fidx(t, f, te, tr))),
                      pl.BlockSpec((None, d, tf), lambda t, f, te, tr: (te[t], 0, fidx(t, f, te, tr))),
                      pl.BlockSpec((None, tf, d), lambda t, f, te, tr: (te[t], fidx(t, f, te, tr), 0))],
            out_specs=pl.BlockSpec((sup, d), lambda t, f, te, tr: (t, 0), pipeline_mode=pl.Buffered(1)),
            scratch_shapes=[pltpu.VMEM((d, tf), BF16), pltpu.VMEM((d, tf), BF16), pltpu.VMEM((tf, d), BF16)]),
        out_shape=jax.ShapeDtypeStruct((n_rows, d), F32),
        compiler_params=_params("arbitrary", "arbitrary"),
    )(tile_expert, tile_rows, xs, wg, wu, wd)


def _combine_kernel(pos_ref, y_ref, meta_ref, g_ref, ys_ref, op_ref, os_ref, buf_ref, x_ref, sem,
                    *, n_tok, tmc, prompt_tiles):
    i = pl.program_id(0)
    base = i * tmc

    def issue(j, carry):
        t = base + j
        pltpu.make_async_copy(ys_ref.at[pos_ref[t]], buf_ref.at[0, j], sem).start()
        pltpu.make_async_copy(ys_ref.at[pos_ref[n_tok + t]], buf_ref.at[1, j], sem).start()
        return carry

    lax.fori_loop(0, tmc, issue, 0)
    pltpu.make_async_copy(buf_ref.at[0], buf_ref.at[0], sem).wait()
    pltpu.make_async_copy(buf_ref.at[1], buf_ref.at[1], sem).wait()
    g1 = meta_ref[:, 2:3]
    g2 = meta_ref[:, 3:4]
    ss = jnp.zeros((tmc, 1), F32)
    for c in range(buf_ref.shape[2]):
        cs = slice(c * LANES, (c + 1) * LANES)
        x = y_ref[:, cs] + (buf_ref[0, :, c, :] * g1 + buf_ref[1, :, c, :] * g2)
        ss = ss + jnp.sum(x * x, axis=1, keepdims=True)
        x_ref[:, cs] = x
    inv = lax.rsqrt(ss / x_ref.shape[1] + NORM_EPS)

    @pl.when(i < prompt_tiles)
    def _():
        op_ref[...] = x_ref[...] * inv * g_ref[...]

    @pl.when(i >= prompt_tiles)
    def _():
        os_ref[...] = x_ref[...] * inv * g_ref[...]


def combine_final(y, meta, g_final, ys3, pos, n_prompt, tmc):
    m, d = y.shape
    slab = ys3.shape[1:]
    prompt_tiles = n_prompt // tmc
    return pl.pallas_call(
        functools.partial(_combine_kernel, n_tok=m, tmc=tmc, prompt_tiles=prompt_tiles),
        grid_spec=pltpu.PrefetchScalarGridSpec(
            num_scalar_prefetch=1,
            grid=(m // tmc,),
            in_specs=[pl.BlockSpec((tmc, d), lambda i, pos: (i, 0)),
                      pl.BlockSpec((tmc, LANES), lambda i, pos: (i, 0)),
                      pl.BlockSpec((1, d), lambda i, pos: (0, 0)),
                      pl.BlockSpec(memory_space=pl.ANY)],
            out_specs=[pl.BlockSpec((tmc, d), lambda i, pos: (jnp.minimum(i, prompt_tiles - 1), 0)),
                       pl.BlockSpec((tmc, d), lambda i, pos: (jnp.maximum(i - prompt_tiles, 0), 0))],
            scratch_shapes=[pltpu.VMEM((2, tmc) + slab, F32), pltpu.VMEM((tmc, d), F32),
                            pltpu.SemaphoreType.DMA(())]),
        out_shape=[jax.ShapeDtypeStruct((n_prompt, d), F32), jax.ShapeDtypeStruct((m - n_prompt, d), F32)],
        compiler_params=_params("arbitrary"),
    )(pos, y, meta, g_final.reshape(1, d), ys3)


def kernel(x_prompt, x_sample, cache_a_k, cache_a_v, state_pool, cache_c_k, cache_c_v, cache_c_idx,
           norm_mix, norm_ffn, norm_final, w_in_even, w_out_even, a_rel_bias, pool_w, pool_scale,
           ffn_w_gate, ffn_w_up, ffn_w_down, w_in_odd, w_out_odd,
           moe_router, moe_w_gate, moe_w_up, moe_w_down):
    nbp, lp, d = x_prompt.shape
    nb, ds, _ = x_sample.shape
    past = cache_c_k.shape[2]
    a_len = cache_a_k.shape[2]
    assert nbp == 1 and lp % 512 == 0 and (nb * ds) % 512 == 0 and ds == POOL_HALO and past >= POOL_HALO
    ns = nb * ds
    m = lp + ns
    tm = 512
    bf = lambda t: t.astype(BF16)

    x = jnp.concatenate([x_prompt.reshape(lp, d), x_sample.reshape(ns, d)], axis=0)

    proj0 = norm_proj(x, norm_mix[0], bf(w_in_even[0]), tm, 1024)
    k0 = proj0[:, A_WIDTH:2 * A_WIDTH]
    v0 = proj0[:, 2 * A_WIDTH:3 * A_WIDTH]
    u0 = proj0[:, 3 * A_WIDTH:]

    pad = A_PREV_CHUNKS * CHUNK
    bias_p = _rel_bias_tile(a_rel_bias[0], 0, CHUNK, -pad, A_BAND)
    a_p = band_prompt(proj0, _pair_rows(bias_p), lp)

    k_pos = past - a_len + jnp.arange(a_len + ds)
    q_pos = past + jnp.arange(ds)
    qch, kch = q_pos // CHUNK, k_pos // CHUNK
    ok = ((k_pos[None, :] >= 0) & (kch[None, :] <= qch[:, None])
          & (kch[None, :] >= qch[:, None] - A_PREV_CHUNKS))
    bias_s = jnp.where(ok[None], _rel_bias_tile(a_rel_bias[0], past, ds, past - a_len, a_len + ds), NEG)
    a_s = band_sample(proj0, cache_a_k[0].reshape(nb, a_len, A_WIDTH),
                      cache_a_v[0].reshape(nb, a_len, A_WIDTH), _pair_rows(bias_s), lp, nb, ds)

    u_s = u0[lp:].reshape(nb, ds, B_WIDTH)
    u_hist = jnp.concatenate([state_pool[0], u_s], axis=1)
    u_ext = jnp.concatenate([jnp.zeros((nb, POOL_HALO - B_HIST, B_WIDTH), F32), u_hist], axis=1)
    pw = bf(pool_w[0])
    ps = pool_scale[0].reshape(1, B_WIDTH)
    p_p = pool_prompt(proj0, pw, ps, lp, tm)
    p_s = pool_sample(u_ext, pw, ps, past)

    a = jnp.concatenate([a_p, a_s], axis=0)
    p = jnp.concatenate([p_p, p_s], axis=0)
    wo = bf(w_out_even[0])
    y = mm_res([a, p], [wo[:A_WIDTH], wo[A_WIDTH:]], x, tm, 1024)
    y = ffn(y, norm_ffn[0], bf(ffn_w_gate[0]), bf(ffn_w_up[0]), bf(ffn_w_down[0]), tm, 512)

    n_q = C_HEADS * C_HEAD_DIM
    n_kv = C_KV_HEADS * C_HEAD_DIM
    n_qi = IDX_HEADS * IDX_DIM
    n_main = n_q + 2 * n_kv + n_qi
    w1 = w_in_odd[0]
    w_tail = jnp.pad(w1[:, n_main:], ((0, 0), (0, LANES - (w1.shape[1] - n_main))))
    main = norm_proj(y, norm_mix[1], bf(w1[:, :n_main]), tm, 512)
    tail = norm_proj(y, norm_mix[1], bf(w_tail), tm, LANES)
    pos = jnp.concatenate([jnp.arange(lp), jnp.tile(past + jnp.arange(ds), nb)])
    kv1, t32, ob = rope_all(main, tail, _rope_tables(pos, C_HEAD_DIM), _rope_tables(pos, IDX_DIM), tm)
    k1 = kv1[:, :n_kv]
    v1 = kv1[:, n_kv:]
    ki1 = t32[:, :IDX_DIM]

    kb_p = 128
    vt = ob[:lp, n_q + n_kv:n_q + 2 * n_kv].reshape(lp // kb_p, kb_p, n_kv).transpose(0, 2, 1)
    wi_t = t32[:lp, IDX_DIM:IDX_DIM + IDX_HEADS].T
    o_p = dsa_t(ob, wi_t, vt, nq=lp, tq=256, kb=kb_p, qb=128, pb=256, cb=1024, topk=min(TOPK_MAX, lp // 4))
    o_s = dsa_sample(ob, t32, cache_c_k[0].reshape(nb, past, n_kv), cache_c_v[0].reshape(nb, past, n_kv), cache_c_idx[0],
                     row0=lp, kb=3 * LANES, topk=min(TOPK_MAX, (past + ds) // 4))

    o = jnp.concatenate([o_p, o_s], axis=0)
    y = mm_res([o], [bf(w_out_odd[0])], y, tm, 1024)

    wr = jnp.pad(moe_router[0], ((0, 0), (0, LANES - N_EXPERTS)))
    h, meta = router(y, norm_ffn[1], wr, tm)
    sup = 1536
    pos_rows, row_token, tile_expert, tile_rows = route_plan(meta, sup)
    n_rows = row_token.shape[0]
    slab = (d // LANES, LANES)
    xs = gather_rows(h.reshape((m,) + slab), row_token, 512)
    ys = grouped_ffn(xs.reshape(n_rows, d), tile_expert, tile_rows,
                     moe_w_gate[0], moe_w_up[0], moe_w_down[0], sup, 256, 512)
    y_p, y_s = combine_final(y, meta, norm_final, ys.reshape((n_rows,) + slab), pos_rows, lp, 256)

    y_prompt = y_p.reshape(1, lp, d)
    y_sample = y_s.reshape(nb, ds, d)
    keep = min(A_BAND, lp)
    heads = lambda t, n: t.reshape(1, n, -1, A_HEADS, A_HEAD_DIM)
    a_k_prompt = heads(k0[lp - keep:lp], 1)
    a_v_prompt = heads(v0[lp - keep:lp], 1)
    pool_prompt_out = u0[lp - B_HIST:lp].reshape(1, 1, B_HIST, B_WIDTH)
    c_k_prompt = k1[:lp].reshape(1, 1, lp, C_KV_HEADS, C_HEAD_DIM)
    c_v_prompt = v1[:lp].reshape(1, 1, lp, C_KV_HEADS, C_HEAD_DIM)
    c_idx_prompt = ki1[:lp].reshape(1, 1, lp, IDX_DIM)
    shift = lambda cache, new: jnp.concatenate(
        [cache[0], new.reshape(nb, ds, A_HEADS, A_HEAD_DIM)], axis=1)[:, ds:][None]
    a_k_sample = shift(cache_a_k, k0[lp:])
    a_v_sample = shift(cache_a_v, v0[lp:])
    pool_sample_out = u_hist[:, ds:][None]
    c_k_sample = k1[lp:].reshape(1, nb, ds, C_KV_HEADS, C_HEAD_DIM)
    c_v_sample = v1[lp:].reshape(1, nb, ds, C_KV_HEADS, C_HEAD_DIM)
    c_idx_sample = ki1[lp:].reshape(1, nb, ds, IDX_DIM)
    return (y_prompt, y_sample, a_k_prompt, a_v_prompt, pool_prompt_out,
            c_k_prompt, c_v_prompt, c_idx_prompt,
            a_k_sample, a_v_sample, pool_sample_out,
            c_k_sample, c_v_sample, c_idx_sample)
```

```python
import functools
import math

import jax
import jax.numpy as jnp
from jax import lax
from jax.experimental import pallas as pl
from jax.experimental.pallas import tpu as pltpu

F32 = jnp.float32
BF16 = jnp.bfloat16

NORM_EPS = 1e-6
NEG = -1e30
BIG = 1e30
SHIFT_SLACK = 1.001
MIN_SOFTMAX_SUM = 2.0 ** -80

CHUNK = 64
A_HEADS = 16
A_HEAD_DIM = 64
A_WIDTH = A_HEADS * A_HEAD_DIM
A_PREV_CHUNKS = 8
A_BAND = (A_PREV_CHUNKS + 1) * CHUNK
A_REL_CLIP = 128
B_WINDOWS = (2, 4, 8, 16)
B_GROUP = 256
B_WIDTH = B_GROUP * len(B_WINDOWS)
B_HIST = max(B_WINDOWS) - 1
C_HEADS = 16
C_KV_HEADS = 4
C_HEAD_DIM = 128
C_GROUPS = C_HEADS // C_KV_HEADS
IDX_HEADS = 8
IDX_DIM = 64
TOPK_MAX = 256
ROPE_THETA = 500000.0
ROPE_FRAC = 4
N_EXPERTS = 8

LANES = 128
POOL_HALO = 16
VMEM_LIMIT = 56 * 1024 * 1024

_NT = (((1,), (1,)), ((), ()))


def _params(*sem):
    return pltpu.CompilerParams(dimension_semantics=sem, vmem_limit_bytes=VMEM_LIMIT)


def _rms(x, g):
    ms = jnp.mean(x * x, axis=-1, keepdims=True)
    return x * lax.rsqrt(ms + NORM_EPS) * g


def _norm_proj_kernel(x_ref, g_ref, w_ref, o_ref, h_ref):
    @pl.when(pl.program_id(1) == 0)
    def _():
        h_ref[...] = _rms(x_ref[...], g_ref[...]).astype(BF16)

    o_ref[...] = jnp.dot(h_ref[...], w_ref[...], preferred_element_type=F32)


def norm_proj(x, g, w, tm, tn):
    m, d = x.shape
    n = w.shape[1]
    return pl.pallas_call(
        _norm_proj_kernel,
        grid=(m // tm, n // tn),
        in_specs=[pl.BlockSpec((tm, d), lambda i, j: (i, 0)),
                  pl.BlockSpec((1, d), lambda i, j: (0, 0)),
                  pl.BlockSpec((d, tn), lambda i, j: (0, j))],
        out_specs=pl.BlockSpec((tm, tn), lambda i, j: (i, j)),
        out_shape=jax.ShapeDtypeStruct((m, n), F32),
        scratch_shapes=[pltpu.VMEM((tm, d), BF16)],
        compiler_params=_params("parallel", "arbitrary"),
    )(x, g.reshape(1, d), w)


def _band_pairs(q_ref, q_row0, nq, kw_ref, vw_ref, k_row0, nk, bias_ref, key_ok, o_ref):
    lane = lax.broadcasted_iota(jnp.int32, (1, LANES), 1)
    first = lane < A_HEAD_DIM
    for hp in range(A_HEADS // 2):
        cs = slice(hp * LANES, (hp + 1) * LANES)
        qp = q_ref[pl.ds(q_row0, nq), cs]
        kp = kw_ref[pl.ds(k_row0, nk), cs]
        vp = vw_ref[pl.ds(k_row0, nk), cs]
        qm = jnp.concatenate([jnp.where(first, qp, 0.0), jnp.where(first, 0.0, qp)], axis=0).astype(BF16)
        s = lax.dot_general(qm, kp, _NT, preferred_element_type=F32)
        s = s * (A_HEAD_DIM ** -0.5) + bias_ref[hp]
        if key_ok is not None:
            s = jnp.where(key_ok, s, NEG)
        mx = jnp.max(s, axis=-1, keepdims=True)
        e = jnp.exp(s - mx)
        l = jnp.sum(e, axis=-1, keepdims=True)
        o = jnp.dot(e.astype(BF16), vp, preferred_element_type=F32) / l
        o_ref[pl.ds(q_row0, nq), cs] = jnp.where(first, o[:nq], o[nq:]).astype(o_ref.dtype)


def _band_prompt_kernel(q_ref, kp_ref, kc_ref, vp_ref, vc_ref, bias_ref, o_ref, kw_ref, vw_ref, *, qb, pad):
    i = pl.program_id(0)
    kw_ref[0:pad, :] = kp_ref[...].astype(BF16)
    kw_ref[pad:pad + qb, :] = kc_ref[...].astype(BF16)
    vw_ref[0:pad, :] = vp_ref[...].astype(BF16)
    vw_ref[pad:pad + qb, :] = vc_ref[...].astype(BF16)
    col = lax.broadcasted_iota(jnp.int32, (1, A_BAND), 1)

    def chunk(cc, carry):
        r0 = pl.multiple_of(cc * CHUNK, CHUNK)
        first_valid = jnp.where(i == 0, pad - cc * CHUNK, 0)
        _band_pairs(q_ref, r0, CHUNK, kw_ref, vw_ref, r0, A_BAND, bias_ref, col >= first_valid, o_ref)
        return carry

    lax.fori_loop(0, qb // CHUNK, chunk, 0)


def band_prompt(proj, bias, lp):
    pad = A_PREV_CHUNKS * CHUNK
    qb = pad
    blk = (qb, A_WIDTH)
    prev = lambda c: (lambda i: (jnp.maximum(i - 1, 0), c))
    cur = lambda c: (lambda i: (i, c))
    return pl.pallas_call(
        functools.partial(_band_prompt_kernel, qb=qb, pad=pad),
        grid=(lp // qb,),
        in_specs=[pl.BlockSpec(blk, cur(0)),
                  pl.BlockSpec(blk, prev(1)), pl.BlockSpec(blk, cur(1)),
                  pl.BlockSpec(blk, prev(2)), pl.BlockSpec(blk, cur(2)),
                  pl.BlockSpec((A_HEADS // 2, 2 * CHUNK, A_BAND), lambda i: (0, 0, 0))],
        out_specs=pl.BlockSpec(blk, lambda i: (i, 0)),
        out_shape=jax.ShapeDtypeStruct((lp, A_WIDTH), BF16),
        scratch_shapes=[pltpu.VMEM((pad + qb, A_WIDTH), BF16), pltpu.VMEM((pad + qb, A_WIDTH), BF16)],
        compiler_params=_params("parallel"),
    )(proj, proj, proj, proj, proj, bias)


def _band_sample_kernel(q_ref, kn_ref, vn_ref, ck_ref, cv_ref, bias_ref, o_ref, kw_ref, vw_ref, *, a_len, ds):
    kw_ref[0:a_len, :] = ck_ref[...].astype(BF16)
    kw_ref[a_len:a_len + ds, :] = kn_ref[...].astype(BF16)
    vw_ref[0:a_len, :] = cv_ref[...].astype(BF16)
    vw_ref[a_len:a_len + ds, :] = vn_ref[...].astype(BF16)
    _band_pairs(q_ref, 0, ds, kw_ref, vw_ref, 0, a_len + ds, bias_ref, None, o_ref)


def band_sample(proj, cache_k, cache_v, bias, lp, nb, ds):
    a_len = cache_k.shape[1]
    row = lambda c: (lambda b: (lp // ds + b, c))
    return pl.pallas_call(
        functools.partial(_band_sample_kernel, a_len=a_len, ds=ds),
        grid=(nb,),
        in_specs=[pl.BlockSpec((ds, A_WIDTH), row(0)),
                  pl.BlockSpec((ds, A_WIDTH), row(1)),
                  pl.BlockSpec((ds, A_WIDTH), row(2)),
                  pl.BlockSpec((None, a_len, A_WIDTH), lambda b: (b, 0, 0)),
                  pl.BlockSpec((None, a_len, A_WIDTH), lambda b: (b, 0, 0)),
                  pl.BlockSpec((A_HEADS // 2, 2 * ds, a_len + ds), lambda b: (0, 0, 0))],
        out_specs=pl.BlockSpec((ds, A_WIDTH), lambda b: (b, 0)),
        out_shape=jax.ShapeDtypeStruct((nb * ds, A_WIDTH), BF16),
        scratch_shapes=[pltpu.VMEM((a_len + ds, A_WIDTH), BF16), pltpu.VMEM((a_len + ds, A_WIDTH), BF16)],
        compiler_params=_params("parallel"),
    )(proj, proj, proj, cache_k, cache_v, bias)


def _rel_bias_tile(rel_bias, q0, nq, k0, nk):
    rel_max = q0 - k0 + nq - 1
    rel = jnp.clip(rel_max - jnp.arange(nq + nk - 1), -A_REL_CLIP, A_REL_CLIP) + A_REL_CLIP
    ext = rel_bias[:, rel].astype(F32)
    return jnp.stack([ext[:, nq - 1 - i:nq - 1 - i + nk] for i in range(nq)], axis=1)


def _pair_rows(bias):
    h, nq, nk = bias.shape
    return bias.reshape(h // 2, 2 * nq, nk)


def _pool_kernel(prev_ref, cur_ref, w_ref, sc_ref, o_ref, ext_ref, *, tm, prompt, pos0):
    i = pl.program_id(0)
    prev = prev_ref[...]
    if prompt:
        prev = jnp.where(i == 0, 0.0, prev)
        pos = i * tm + lax.broadcasted_iota(jnp.int32, (tm, 1), 0)
    else:
        pos = pos0 + lax.broadcasted_iota(jnp.int32, (tm, 1), 0)
    ext_ref[0:POOL_HALO, :] = prev
    ext_ref[POOL_HALO:POOL_HALO + tm, :] = cur_ref[...]
    for g, w in enumerate(B_WINDOWS):
        cs = slice(g * B_GROUP, (g + 1) * B_GROUP)
        tok = ext_ref[POOL_HALO:POOL_HALO + tm, cs]
        tot = tok
        for j in range(1, w):
            tot = tot + ext_ref[POOL_HALO - j:POOL_HALO - j + tm, cs]
        cnt = jnp.minimum(pos + 1, w).astype(F32)
        pooled = (tot / cnt - tok).astype(BF16)
        o = jnp.dot(pooled, w_ref[g], preferred_element_type=F32) * sc_ref[:, cs]
        o_ref[:, cs] = o.astype(o_ref.dtype)


def pool_prompt(proj, pool_w, pool_scale, lp, tm):
    ucol = 3 * A_WIDTH // B_WIDTH
    per = tm // POOL_HALO
    return pl.pallas_call(
        functools.partial(_pool_kernel, tm=tm, prompt=True, pos0=0),
        grid=(lp // tm,),
        in_specs=[pl.BlockSpec((POOL_HALO, B_WIDTH), lambda i: (jnp.maximum(i * per - 1, 0), ucol)),
                  pl.BlockSpec((tm, B_WIDTH), lambda i: (i, ucol)),
                  pl.BlockSpec((len(B_WINDOWS), B_GROUP, B_GROUP), lambda i: (0, 0, 0)),
                  pl.BlockSpec((1, B_WIDTH), lambda i: (0, 0))],
        out_specs=pl.BlockSpec((tm, B_WIDTH), lambda i: (i, 0)),
        out_shape=jax.ShapeDtypeStruct((lp, B_WIDTH), BF16),
        scratch_shapes=[pltpu.VMEM((POOL_HALO + tm, B_WIDTH), F32)],
        compiler_params=_params("parallel"),
    )(proj, proj, pool_w, pool_scale)


def pool_sample(u_ext, pool_w, pool_scale, past):
    nb, tot, _ = u_ext.shape
    ds = tot - POOL_HALO
    return pl.pallas_call(
        functools.partial(_pool_kernel, tm=ds, prompt=False, pos0=past),
        grid=(nb,),
        in_specs=[pl.BlockSpec((None, POOL_HALO, B_WIDTH), lambda b: (b, 0, 0)),
                  pl.BlockSpec((None, ds, B_WIDTH), lambda b: (b, POOL_HALO // ds, 0)),
                  pl.BlockSpec((len(B_WINDOWS), B_GROUP, B_GROUP), lambda b: (0, 0, 0)),
                  pl.BlockSpec((1, B_WIDTH), lambda b: (0, 0))],
        out_specs=pl.BlockSpec((ds, B_WIDTH), lambda b: (b, 0)),
        out_shape=jax.ShapeDtypeStruct((nb * ds, B_WIDTH), BF16),
        scratch_shapes=[pltpu.VMEM((POOL_HALO + ds, B_WIDTH), F32)],
        compiler_params=_params("parallel"),
    )(u_ext, u_ext, pool_w, pool_scale)


def _mm_res_kernel(*refs, n_in):
    xs, ws = refs[:n_in], refs[n_in:2 * n_in]
    res_ref, o_ref = refs[2 * n_in], refs[2 * n_in + 1]
    acc = res_ref[...]
    for x_ref, w_ref in zip(xs, ws):
        acc = acc + jnp.dot(x_ref[...], w_ref[...], preferred_element_type=F32)
    o_ref[...] = acc


def mm_res(xs, ws, res, tm, tn):
    m, n = res.shape
    n_in = len(xs)
    in_specs = ([pl.BlockSpec((tm, x.shape[1]), lambda i, j: (i, 0)) for x in xs]
                + [pl.BlockSpec((w.shape[0], tn), lambda i, j: (0, j)) for w in ws]
                + [pl.BlockSpec((tm, tn), lambda i, j: (i, j))])
    return pl.pallas_call(
        functools.partial(_mm_res_kernel, n_in=n_in),
        grid=(m // tm, n // tn),
        in_specs=in_specs,
        out_specs=pl.BlockSpec((tm, tn), lambda i, j: (i, j)),
        out_shape=jax.ShapeDtypeStruct((m, n), F32),
        compiler_params=_params("parallel", "arbitrary"),
    )(*xs, *ws, res)


def _swiglu_tile(h, wg, wu, wd):
    a = jnp.dot(h, wg, preferred_element_type=F32)
    b = jnp.dot(h, wu, preferred_element_type=F32)
    act = (a * jax.nn.sigmoid(a) * b).astype(BF16)
    return jnp.dot(act, wd, preferred_element_type=F32)


def _ffn_kernel(y_ref, g_ref, wg_ref, wu_ref, wd_ref, o_ref, h_ref, acc_ref):
    f = pl.program_id(1)

    @pl.when(f == 0)
    def _():
        h_ref[...] = _rms(y_ref[...], g_ref[...]).astype(BF16)
        acc_ref[...] = jnp.zeros_like(acc_ref)

    acc_ref[...] += _swiglu_tile(h_ref[...], wg_ref[...], wu_ref[...], wd_ref[...])

    @pl.when(f == pl.num_programs(1) - 1)
    def _():
        o_ref[...] = y_ref[...] + acc_ref[...]


def ffn(y, g, wg, wu, wd, tm, tf):
    m, d = y.shape
    ff = wg.shape[1]
    return pl.pallas_call(
        _ffn_kernel,
        grid=(m // tm, ff // tf),
        in_specs=[pl.BlockSpec((tm, d), lambda i, f: (i, 0)),
                  pl.BlockSpec((1, d), lambda i, f: (0, 0)),
                  pl.BlockSpec((d, tf), lambda i, f: (0, f)),
                  pl.BlockSpec((d, tf), lambda i, f: (0, f)),
                  pl.BlockSpec((tf, d), lambda i, f: (f, 0))],
        out_specs=pl.BlockSpec((tm, d), lambda i, f: (i, 0)),
        out_shape=jax.ShapeDtypeStruct((m, d), F32),
        scratch_shapes=[pltpu.VMEM((tm, d), BF16), pltpu.VMEM((tm, d), F32)],
        compiler_params=_params("parallel", "arbitrary"),
    )(y, g.reshape(1, d), wg, wu, wd)


def _rope_tables(pos, head_dim):
    rot = head_dim // ROPE_FRAC
    half = rot // 2
    inv = jnp.exp(-math.log(ROPE_THETA) * jnp.arange(half, dtype=F32) * (2.0 / rot))
    ang = pos.astype(F32)[:, None] * inv[None, :]
    cos, sin = jnp.cos(ang), jnp.sin(ang)
    m = pos.shape[0]
    one = jnp.ones((m, head_dim - rot), F32)
    zero_r = jnp.zeros((m, head_dim - rot), F32)
    zero_h = jnp.zeros((m, half), F32)
    c = jnp.concatenate([cos, cos, one], axis=1)
    s_dn = jnp.concatenate([-sin, zero_h, zero_r], axis=1)
    s_up = jnp.concatenate([zero_h, sin, zero_r], axis=1)
    rep = LANES // head_dim
    return jnp.stack([jnp.tile(c, (1, rep)), jnp.tile(s_dn, (1, rep)), jnp.tile(s_up, (1, rep))])


def _rot(x, tab_ref, half):
    return (x * tab_ref[0] + pltpu.roll(x, LANES - half, 1) * tab_ref[1]
            + pltpu.roll(x, half, 1) * tab_ref[2])


def _rope_kernel(main_ref, tail_ref, tq_ref, ti_ref, kv_ref, t32_ref, ob_ref,
                 *, n_q, n_k, n_v, n_qi, wi_scale, q_scale):
    half_qk = C_HEAD_DIM // ROPE_FRAC // 2
    half_i = IDX_DIM // ROPE_FRAC // 2
    for c in range(n_q + n_k + n_v + n_qi):
        cs = slice(c * LANES, (c + 1) * LANES)
        x = main_ref[:, cs]
        if c < n_q + n_k:
            x = _rot(x, tq_ref, half_qk)
        elif c >= n_q + n_k + n_v:
            x = _rot(x, ti_ref, half_i)
        if n_q <= c < n_q + n_k + n_v:
            kv_ref[:, (c - n_q) * LANES:(c - n_q + 1) * LANES] = x
        ob_ref[:, cs] = (x * q_scale if c < n_q else x).astype(BF16)
    t = tail_ref[...]
    lane = lax.broadcasted_iota(jnp.int32, (1, LANES), 1)
    r = _rot(t, ti_ref, half_i)
    t32_ref[...] = jnp.where(lane < IDX_DIM, r, t * wi_scale)
    c = n_q + n_k + n_v + n_qi
    ob_ref[:, c * LANES:(c + 1) * LANES] = jnp.where(lane < IDX_DIM, r, pltpu.roll(r, IDX_DIM, 1)).astype(BF16)


def rope_all(main, tail, tab_qk, tab_idx, tm):
    m, nmain = main.shape
    n_q = C_HEADS * C_HEAD_DIM // LANES
    n_k = C_KV_HEADS * C_HEAD_DIM // LANES
    n_qi = IDX_HEADS * IDX_DIM // LANES
    wi_scale = (IDX_HEADS ** -0.5) * (IDX_DIM ** -0.5)
    q_scale = (C_HEAD_DIM ** -0.5) * math.log2(math.e)
    row = lambda i: (i, 0)
    return pl.pallas_call(
        functools.partial(_rope_kernel, n_q=n_q, n_k=n_k, n_v=n_k, n_qi=n_qi, wi_scale=wi_scale, q_scale=q_scale),
        grid=(m // tm,),
        in_specs=[pl.BlockSpec((tm, nmain), row),
                  pl.BlockSpec((tm, LANES), row),
                  pl.BlockSpec((3, tm, LANES), lambda i: (0, i, 0)),
                  pl.BlockSpec((3, tm, LANES), lambda i: (0, i, 0))],
        out_specs=[pl.BlockSpec((tm, 2 * n_k * LANES), row),
                   pl.BlockSpec((tm, LANES), row),
                   pl.BlockSpec((tm, nmain + LANES), row)],
        out_shape=[jax.ShapeDtypeStruct((m, 2 * n_k * LANES), F32),
                   jax.ShapeDtypeStruct((m, LANES), F32),
                   jax.ShapeDtypeStruct((m, nmain + LANES), BF16)],
        compiler_params=_params("parallel"),
    )(main, tail, tab_qk, tab_idx)


def _dsa_kernel(q_ref, qi_ref, wi_ref, kn_ref, vn_ref, kin_ref, ck_ref, cv_ref, ci_ref, o_ref,
                k_ref, v_ref, ki_ref, sc_ref, qs_ref, acc_ref, m_ref, l_ref,
                *, tq, kb, past, n_valid, q_pos0, topk):
    i = pl.program_id(1)
    rows = C_GROUPS * tq
    sub = kb // LANES
    kf = float(topk)

    new = past + tq
    ci = ci_ref[...].astype(BF16)
    k_ref[0:past, :] = ck_ref[...].astype(BF16)
    v_ref[0:past, :] = cv_ref[...].astype(BF16)
    ki_ref[0:past, :] = jnp.concatenate([ci, ci], axis=1)
    k_ref[past:new, :] = kn_ref[...]
    v_ref[past:new, :] = vn_ref[...]
    ki_ref[past:new, :] = kin_ref[...]
    n_pad = k_ref.shape[0] - new
    k_ref[new:, :] = jnp.zeros((n_pad, k_ref.shape[1]), BF16)
    v_ref[new:, :] = jnp.zeros((n_pad, v_ref.shape[1]), BF16)
    ki_ref[new:, :] = jnp.zeros((n_pad, ki_ref.shape[1]), BF16)

    q_pos = q_pos0 + i * tq + lax.broadcasted_iota(jnp.int32, (tq, 1), 0)
    q_chunk = q_pos // CHUNK
    last_chunk = (q_pos0 + i * tq + tq - 1) // CHUNK
    kv_limit = jnp.minimum(n_valid, (last_chunk + 1) * CHUNK)
    nkb = (kv_limit + kb - 1) // kb

    lane = lax.broadcasted_iota(jnp.int32, (1, LANES), 1)
    first = lane < IDX_DIM
    wi = wi_ref[:, IDX_DIM:IDX_DIM + IDX_HEADS]

    def score_block(b, carry):
        for c in range(sub):
            off = pl.multiple_of(b * kb + c * LANES, LANES)
            kib = ki_ref[pl.ds(off, LANES), :]
            acc = jnp.zeros((tq, LANES), F32)
            for hp in range(IDX_HEADS // 2):
                qp = qi_ref[:, hp * LANES:(hp + 1) * LANES]
                for half in range(2):
                    keep = first if half == 0 else jnp.logical_not(first)
                    qm = jnp.where(keep, qp, jnp.zeros_like(qp))
                    d = lax.dot_general(qm, kib, _NT, preferred_element_type=F32)
                    h = hp * 2 + half
                    acc = acc + jnp.maximum(d, 0.0) * wi[:, h:h + 1]
            k_pos = off + lane
            adm = jnp.logical_and(k_pos // CHUNK <= q_chunk, k_pos < n_valid)
            sc_ref[b, :, c * LANES:(c + 1) * LANES] = jnp.where(adm, acc, NEG)
        return carry

    lax.fori_loop(0, nkb, score_block, 0)

    def lane_sum(x):
        return jnp.sum(x, axis=1, keepdims=True)

    def count_ge(t):
        def body(b, acc):
            for c in range(sub):
                blk = sc_ref[b, :, c * LANES:(c + 1) * LANES]
                acc = acc + jnp.where(blk >= t, 1.0, 0.0)
            return acc
        return lane_sum(lax.fori_loop(0, nkb, body, jnp.zeros((tq, LANES), F32)))

    def stats(b, carry):
        mx, mn, cnt = carry
        for c in range(sub):
            blk = sc_ref[b, :, c * LANES:(c + 1) * LANES]
            ok = blk > 0.5 * NEG
            mx = jnp.maximum(mx, blk)
            mn = jnp.minimum(mn, jnp.where(ok, blk, BIG))
            cnt = cnt + jnp.where(ok, 1.0, 0.0)
        return mx, mn, cnt

    mx, mn, cnt = lax.fori_loop(
        0, nkb, stats,
        (jnp.full((tq, LANES), NEG, F32), jnp.full((tq, LANES), BIG, F32), jnp.zeros((tq, LANES), F32)))
    row_max = jnp.max(mx, axis=1, keepdims=True)
    row_min = jnp.min(mn, axis=1, keepdims=True)
    n_adm = lane_sum(cnt)

    done0 = jnp.where(n_adm <= kf, 1.0, 0.0)
    state0 = (row_min, jnp.full((tq, 1), BIG, F32), row_max, jnp.full((tq, 1), 0.5 * NEG, F32), done0)

    def bisect(state, n_steps):
        def cond(c):
            it, st = c
            return jnp.logical_and(it < n_steps, jnp.min(st[4]) < 0.5)

        def body(c):
            it, (lo, hi, mid, thr, done) = c
            cnt = count_ge(mid)
            live = done < 0.5
            hit = jnp.logical_and(live, cnt == kf)
            ge = cnt >= kf
            thr = jnp.where(hit, mid, thr)
            done = jnp.where(hit, 1.0, done)
            lo = jnp.where(ge, mid, lo)
            hi = jnp.where(ge, hi, mid)
            return it + 1, (lo, hi, 0.5 * (lo + hi), thr, done)

        return lax.while_loop(cond, body, (jnp.int32(0), state))[1]

    def snap(state):
        lo, hi, mid, thr, done = state

        def body(b, carry):
            v_lo, v_hi = carry
            for c in range(sub):
                blk = sc_ref[b, :, c * LANES:(c + 1) * LANES]
                v_lo = jnp.minimum(v_lo, jnp.where(blk >= lo, blk, BIG))
                v_hi = jnp.maximum(v_hi, jnp.where(blk < hi, blk, NEG))
            return v_lo, v_hi

        v_lo, v_hi = lax.fori_loop(0, nkb, body,
                                   (jnp.full((tq, LANES), BIG, F32), jnp.full((tq, LANES), NEG, F32)))
        v_lo = jnp.min(v_lo, axis=1, keepdims=True)
        v_hi = jnp.max(v_hi, axis=1, keepdims=True)
        live = done < 0.5
        tie = jnp.logical_and(live, v_lo == v_hi)
        thr = jnp.where(tie, v_lo, thr)
        done = jnp.where(tie, 1.0, done)
        lo = jnp.where(live, v_lo, lo)
        return (lo, hi, 0.5 * (lo + hi), thr, done), jnp.where(tie, 1.0, 0.0)

    state = bisect(state0, 32)

    def refine_cond(c):
        rounds, st, _ = c
        return jnp.logical_and(rounds < 10, jnp.min(st[4]) < 0.5)

    def refine_body(c):
        rounds, st, tie = c
        st, new_tie = snap(st)
        st = bisect(st, 32)
        return rounds + 1, st, jnp.maximum(tie, new_tie)

    _, state, tie = lax.while_loop(refine_cond, refine_body,
                                   (jnp.int32(0), state, jnp.zeros((tq, 1), F32)))
    thr = state[3]
    any_tie = jnp.max(tie) > 0.5

    @pl.when(jnp.logical_not(any_tie))
    def _():
        def body(b, carry):
            for c in range(sub):
                cs = slice(c * LANES, (c + 1) * LANES)
                sc_ref[b, :, cs] = jnp.where(sc_ref[b, :, cs] >= thr, 0.0, NEG)
            return carry
        lax.fori_loop(0, nkb, body, 0)

    @pl.when(any_tie)
    def _():
        def gt_body(b, acc):
            for c in range(sub):
                acc = acc + jnp.where(sc_ref[b, :, c * LANES:(c + 1) * LANES] > thr, 1.0, 0.0)
            return acc
        need = kf - lane_sum(lax.fori_loop(0, nkb, gt_body, jnp.zeros((tq, LANES), F32)))

        def count_eq_upto(j):
            def body(b, acc):
                for c in range(sub):
                    blk = sc_ref[b, :, c * LANES:(c + 1) * LANES]
                    idx = (b * kb + c * LANES + lane).astype(F32)
                    acc = acc + jnp.where(jnp.logical_and(blk == thr, idx <= j), 1.0, 0.0)
                return acc
            return lane_sum(lax.fori_loop(0, nkb, body, jnp.zeros((tq, LANES), F32)))

        def idx_body(_, c):
            lo_j, hi_j = c
            mid_j = jnp.floor(0.5 * (lo_j + hi_j))
            ok = count_eq_upto(mid_j) >= need
            return jnp.where(ok, lo_j, mid_j), jnp.where(ok, mid_j, hi_j)

        n_steps = max(1, math.ceil(math.log2(sc_ref.shape[0] * kb + 1)))
        _, last = lax.fori_loop(
            0, n_steps, idx_body,
            (jnp.full((tq, 1), -1.0, F32), jnp.full((tq, 1), float(sc_ref.shape[0] * kb), F32)))
        last = jnp.where(tie > 0.5, last, BIG)

        def body(b, carry):
            for c in range(sub):
                cs = slice(c * LANES, (c + 1) * LANES)
                blk = sc_ref[b, :, cs]
                idx = (b * kb + c * LANES + lane).astype(F32)
                sel = jnp.logical_or(blk > thr, jnp.logical_and(blk == thr, idx <= last))
                sc_ref[b, :, cs] = jnp.where(sel, 0.0, NEG)
            return carry
        lax.fori_loop(0, nkb, body, 0)

    for g in range(C_KV_HEADS):
        for hh in range(C_GROUPS):
            h = g * C_GROUPS + hh
            qs_ref[g, hh * tq:(hh + 1) * tq, :] = q_ref[:, h * C_HEAD_DIM:(h + 1) * C_HEAD_DIM]
    m_ref[...] = jnp.full(m_ref.shape, NEG, F32)
    l_ref[...] = jnp.zeros(l_ref.shape, F32)
    acc_ref[...] = jnp.zeros(acc_ref.shape, F32)

    def attend(b, carry):
        off = pl.multiple_of(b * kb, kb)
        bias = sc_ref[b]
        bias = jnp.concatenate([bias] * C_GROUPS, axis=0)
        for g in range(C_KV_HEADS):
            cs = slice(g * C_HEAD_DIM, (g + 1) * C_HEAD_DIM)
            kblk = k_ref[pl.ds(off, kb), cs]
            vblk = v_ref[pl.ds(off, kb), cs]
            s = lax.dot_general(qs_ref[g], kblk, _NT, preferred_element_type=F32)
            s = s + bias
            m_old = m_ref[g]
            m_new = jnp.maximum(m_old, jnp.max(s, axis=1, keepdims=True))
            alpha = jnp.exp2(m_old - m_new)
            p = jnp.exp2(s - m_new)
            l_ref[g] = alpha * l_ref[g] + jnp.sum(p, axis=1, keepdims=True)
            acc_ref[g] = alpha * acc_ref[g] + jnp.dot(p.astype(BF16), vblk, preferred_element_type=F32)
            m_ref[g] = m_new
        return carry

    lax.fori_loop(0, nkb, attend, 0)

    for g in range(C_KV_HEADS):
        o = acc_ref[g] / l_ref[g]
        for hh in range(C_GROUPS):
            h = g * C_GROUPS + hh
            o_ref[:, h * C_HEAD_DIM:(h + 1) * C_HEAD_DIM] = o[hh * tq:(hh + 1) * tq, :].astype(o_ref.dtype)


def dsa_sample(ob, t32, cache_k, cache_v, cache_i, *, row0, kb, topk):
    nb, past = cache_k.shape[:2]
    n_kv = C_KV_HEADS * C_HEAD_DIM
    ds = (ob.shape[0] - row0) // nb
    n_q = C_HEADS * C_HEAD_DIM
    n_qi = IDX_HEADS * IDX_DIM
    s_all = past + ds
    s_pad = -(-s_all // kb) * kb
    rows = C_GROUPS * ds
    col = lambda w, off: (lambda b, i: (row0 // ds + b, off // w))
    cmap = lambda b, i: (b, 0, 0)
    return pl.pallas_call(
        functools.partial(_dsa_kernel, tq=ds, kb=kb, past=past, n_valid=s_all, q_pos0=past, topk=topk),
        grid=(nb, 1),
        in_specs=[pl.BlockSpec((ds, n_q), col(n_q, 0)),
                  pl.BlockSpec((ds, n_qi), col(n_qi, n_q + 2 * n_kv)),
                  pl.BlockSpec((ds, LANES), col(LANES, 0)),
                  pl.BlockSpec((ds, n_kv), col(n_kv, n_q)),
                  pl.BlockSpec((ds, n_kv), col(n_kv, n_q + n_kv)),
                  pl.BlockSpec((ds, LANES), col(LANES, n_q + 2 * n_kv + n_qi)),
                  pl.BlockSpec((None, past, n_kv), cmap),
                  pl.BlockSpec((None, past, n_kv), cmap),
                  pl.BlockSpec((None, past, IDX_DIM), cmap)],
        out_specs=pl.BlockSpec((ds, n_q), lambda b, i: (b, 0)),
        out_shape=jax.ShapeDtypeStruct((nb * ds, n_q), BF16),
        scratch_shapes=[pltpu.VMEM((s_pad, n_kv), BF16),
                        pltpu.VMEM((s_pad, n_kv), BF16),
                        pltpu.VMEM((s_pad, LANES), BF16),
                        pltpu.VMEM((s_pad // kb, ds, kb), F32),
                        pltpu.VMEM((C_KV_HEADS, rows, C_HEAD_DIM), BF16),
                        pltpu.VMEM((C_KV_HEADS, rows, C_HEAD_DIM), F32),
                        pltpu.VMEM((C_KV_HEADS, rows, 1), F32),
                        pltpu.VMEM((C_KV_HEADS, rows, 1), F32)],
        compiler_params=_params("parallel", "arbitrary"),
    )(ob, ob, t32, ob, ob, ob, cache_k, cache_v, cache_i)


def _dsa_t_kernel(q_ref, qi_ref, wi_ref, k_ref, vt_ref, ki_ref, o_ref,
                  sc_ref, qim_ref, acc_ref, m_ref, l_ref, l8_ref, kmax_ref,
                  *, tq, kb, qb, pb, cb, n_valid, q_pos0, topk):
    i = pl.program_id(1)
    kf = float(topk)

    q_pos = q_pos0 + i * tq + lax.broadcasted_iota(jnp.int32, (1, tq), 1)
    q_chunk = q_pos // CHUNK
    last_chunk = (q_pos0 + i * tq + tq - 1) // CHUNK
    kv_limit = jnp.minimum(n_valid, (last_chunk + 1) * CHUNK)
    nkb = (kv_limit + kb - 1) // kb
    ncb = (kv_limit + cb - 1) // cb

    def col_reduce(x, op):
        groups = x.shape[0] // 8
        chains = 8 if groups % 8 == 0 else 1
        return op(op(x.reshape(chains, groups // chains, 8, tq), axis=1), axis=0)

    col_sum = lambda x: col_reduce(x, jnp.sum)
    col_max = lambda x: col_reduce(x, jnp.max)
    col_min = lambda x: col_reduce(x, jnp.min)

    lane = lax.broadcasted_iota(jnp.int32, (1, LANES), 1)
    first = lane < IDX_DIM
    for hp in range(IDX_HEADS // 2):
        qp = qi_ref[:, hp * LANES:(hp + 1) * LANES]
        qim_ref[2 * hp] = jnp.where(first, qp, jnp.zeros_like(qp))
        qim_ref[2 * hp + 1] = jnp.where(first, jnp.zeros_like(qp), qp)
    block_iota = lax.broadcasted_iota(jnp.int32, (pb, 1), 0)

    def score_block(b, carry):
        off = pl.multiple_of(b * pb, pb)
        kib = ki_ref[pl.ds(off, pb), :]
        acc = jnp.zeros((pb, tq), F32)
        for h in range(IDX_HEADS):
            d = lax.dot_general(kib, qim_ref[h], _NT, preferred_element_type=F32)
            acc = acc + jnp.maximum(d, 0.0) * wi_ref[h:h + 1, :]
        k_pos = off + block_iota
        adm = jnp.logical_and(k_pos // CHUNK <= q_chunk, k_pos < n_valid)
        sc_ref[pl.ds(off, pb), :] = jnp.where(adm, acc, NEG)
        return carry

    lax.fori_loop(0, ncb * (cb // pb), score_block, 0)
    key_iota = lax.broadcasted_iota(jnp.int32, (cb, 1), 0)

    def blocks(body, init):
        def step(b, carry):
            off = pl.multiple_of(b * cb, cb)
            return body(off, sc_ref[pl.ds(off, cb), :], carry)
        return lax.fori_loop(0, ncb, step, init)

    def total(x):
        return jnp.sum(x, axis=0, keepdims=True)

    def count_ge(t):
        return total(blocks(lambda off, blk, acc: acc + col_sum(jnp.where(blk >= t, 1.0, 0.0)),
                            jnp.zeros((8, tq), F32)))

    def stats(off, blk, carry):
        mx, mn, cnt, pos, nonneg = carry
        ok = blk > 0.5 * NEG
        return (jnp.maximum(mx, col_max(blk)), jnp.minimum(mn, col_min(jnp.where(ok, blk, BIG))),
                cnt + col_sum(jnp.where(ok, 1.0, 0.0)),
                pos + col_sum(jnp.where(blk > 0.0, 1.0, 0.0)),
                nonneg + col_sum(jnp.where(blk >= 0.0, 1.0, 0.0)))

    zeros8 = jnp.zeros((8, tq), F32)
    mx, mn, cnt, pos, nonneg = blocks(stats, (jnp.full((8, tq), NEG, F32), jnp.full((8, tq), BIG, F32),
                                              zeros8, zeros8, zeros8))
    row_max = jnp.max(mx, axis=0, keepdims=True)
    row_min = jnp.min(mn, axis=0, keepdims=True)
    n_adm = total(cnt)
    n_pos = total(pos)
    n_nonneg = total(nonneg)

    open_row = n_adm > kf
    above = n_pos >= kf
    below = n_nonneg < kf
    zero_tie = jnp.logical_and(open_row, jnp.logical_not(jnp.logical_or(above, below)))
    lo0 = jnp.where(above, 0.0, row_min)
    hi0 = jnp.where(below, 0.0, BIG)
    mid0 = jnp.where(below, 0.5 * (lo0 + hi0), row_max)
    thr0 = jnp.where(zero_tie, 0.0, 0.5 * NEG)
    done0 = jnp.where(jnp.logical_or(jnp.logical_not(open_row), zero_tie), 1.0, 0.0)
    tie0 = jnp.where(zero_tie, 1.0, 0.0)
    state0 = (lo0, hi0, mid0, thr0, done0)

    def bisect(state, n_steps):
        def cond(c):
            it, st = c
            return jnp.logical_and(it < n_steps, jnp.min(st[4]) < 0.5)

        def body(c):
            it, (lo, hi, mid, thr, done) = c
            cnt = count_ge(mid)
            hit = jnp.logical_and(done < 0.5, cnt == kf)
            ge = cnt >= kf
            thr = jnp.where(hit, mid, thr)
            done = jnp.where(hit, 1.0, done)
            lo = jnp.where(ge, mid, lo)
            hi = jnp.where(ge, hi, mid)
            return it + 1, (lo, hi, 0.5 * (lo + hi), thr, done)

        return lax.while_loop(cond, body, (jnp.int32(0), state))[1]

    def snap(state):
        lo, hi, mid, thr, done = state

        def body(off, blk, carry):
            v_lo, v_hi = carry
            return (jnp.minimum(v_lo, col_min(jnp.where(blk >= lo, blk, BIG))),
                    jnp.maximum(v_hi, col_max(jnp.where(blk < hi, blk, NEG))))

        v_lo, v_hi = blocks(body, (jnp.full((8, tq), BIG, F32), jnp.full((8, tq), NEG, F32)))
        v_lo = jnp.min(v_lo, axis=0, keepdims=True)
        v_hi = jnp.max(v_hi, axis=0, keepdims=True)
        live = done < 0.5
        tie = jnp.logical_and(live, v_lo == v_hi)
        thr = jnp.where(tie, v_lo, thr)
        done = jnp.where(tie, 1.0, done)
        lo = jnp.where(live, v_lo, lo)
        return (lo, hi, 0.5 * (lo + hi), thr, done), jnp.where(tie, 1.0, 0.0)

    state = bisect(state0, 32)

    def refine_cond(c):
        rounds, st, _ = c
        return jnp.logical_and(rounds < 10, jnp.min(st[4]) < 0.5)

    def refine_body(c):
        rounds, st, tie = c
        st, new_tie = snap(st)
        st = bisect(st, 32)
        return rounds + 1, st, jnp.maximum(tie, new_tie)

    _, state, tie = lax.while_loop(refine_cond, refine_body, (jnp.int32(0), state, tie0))
    thr = state[3]
    any_tie = jnp.max(tie) > 0.5

    @pl.when(jnp.logical_not(any_tie))
    def _():
        def body(off, blk, carry):
            sc_ref[pl.ds(off, cb), :] = jnp.where(blk >= thr, 0.0, NEG)
            return carry
        blocks(body, 0)

    @pl.when(any_tie)
    def _():
        need = kf - total(blocks(lambda off, blk, acc: acc + col_sum(jnp.where(blk > thr, 1.0, 0.0)),
                                 jnp.zeros((8, tq), F32)))

        def count_eq_upto(j):
            def body(off, blk, acc):
                idx = (off + key_iota).astype(F32)
                return acc + col_sum(jnp.where(jnp.logical_and(blk == thr, idx <= j), 1.0, 0.0))
            return total(blocks(body, jnp.zeros((8, tq), F32)))

        def idx_cond(c):
            it, (lo_j, hi_j, settled) = c
            return jnp.logical_and(it < n_steps, jnp.min(settled) < 0.5)

        def idx_body(c):
            it, (lo_j, hi_j, settled) = c
            mid_j = jnp.floor(0.5 * (lo_j + hi_j))
            cnt = count_eq_upto(mid_j)
            ok = cnt >= need
            live = settled < 0.5
            lo_j = jnp.where(jnp.logical_and(live, jnp.logical_not(ok)), mid_j, lo_j)
            hi_j = jnp.where(jnp.logical_and(live, ok), mid_j, hi_j)
            settled = jnp.where(jnp.logical_or(cnt == need, hi_j - lo_j <= 1.0), 1.0, settled)
            return it + 1, (lo_j, hi_j, settled)

        n_keys = sc_ref.shape[0]
        n_steps = max(1, math.ceil(math.log2(n_keys + 1))) + 1
        _, (_, last, _) = lax.while_loop(
            idx_cond, idx_body,
            (jnp.int32(0), (jnp.full((1, tq), -1.0, F32), jnp.full((1, tq), float(n_keys), F32),
                            jnp.where(tie > 0.5, 0.0, 1.0))))
        last = jnp.where(tie > 0.5, last, BIG)

        def body(off, blk, carry):
            idx = (off + key_iota).astype(F32)
            sel = jnp.logical_or(blk > thr, jnp.logical_and(blk == thr, idx <= last))
            sc_ref[pl.ds(off, cb), :] = jnp.where(sel, 0.0, NEG)
            return carry
        blocks(body, 0)

    halves = kb // qb

    def head_scores(off, h):
        gs = slice((h // C_GROUPS) * C_HEAD_DIM, (h // C_GROUPS + 1) * C_HEAD_DIM)
        out = []
        for j in range(halves):
            rows = pl.ds(off + j * qb, qb)
            s = lax.dot_general(k_ref[rows, gs], q_ref[:, h * C_HEAD_DIM:(h + 1) * C_HEAD_DIM],
                                _NT, preferred_element_type=F32)
            out.append(s + sc_ref[rows, :])
        return out, gs

    @pl.when(i == 0)
    def _():
        for g in range(C_KV_HEADS):
            gs = slice(g * C_HEAD_DIM, (g + 1) * C_HEAD_DIM)

            def body(b, mx):
                kk = k_ref[pl.ds(pl.multiple_of(b * cb, cb), cb), gs].astype(F32)
                return jnp.maximum(mx, jnp.sum(kk * kk, axis=1, keepdims=True))

            mx = lax.fori_loop(0, k_ref.shape[0] // cb, body, jnp.zeros((cb, 1), F32))
            kmax_ref[g] = jnp.full((1, tq), 1.0, F32) * jnp.max(mx)

    ones = jnp.ones((8, C_HEAD_DIM), F32)
    for h in range(C_HEADS):
        qh = q_ref[:, h * C_HEAD_DIM:(h + 1) * C_HEAD_DIM].astype(F32)
        qn2 = lax.dot_general(ones, qh * qh, _NT, preferred_element_type=F32,
                              precision=lax.Precision.HIGHEST)[0:1, :]
        m_ref[h] = jnp.sqrt(qn2 * kmax_ref[h // C_GROUPS]) * SHIFT_SLACK
    l8_ref[...] = jnp.zeros(l8_ref.shape, F32)
    acc_ref[...] = jnp.zeros(acc_ref.shape, F32)

    def attend_fixed(b, carry):
        off = pl.multiple_of(b * kb, kb)
        for h in range(C_HEADS):
            ss, gs = head_scores(off, h)
            ps = [jnp.exp2(s - m_ref[h]) for s in ss]
            l8_ref[h] += sum(col_sum(p) for p in ps)
            p_all = jnp.concatenate([p.astype(BF16) for p in ps], axis=0)
            acc_ref[h] += jnp.dot(vt_ref[b, gs, :], p_all, preferred_element_type=F32)
        return carry

    lax.fori_loop(0, nkb, attend_fixed, 0)
    l_min = jnp.full((1, tq), BIG, F32)
    for h in range(C_HEADS):
        l_ref[h] = total(l8_ref[h])
        l_min = jnp.minimum(l_min, l_ref[h])

    @pl.when(jnp.logical_not(jnp.min(l_min) >= MIN_SOFTMAX_SUM))
    def _():
        m_ref[...] = jnp.full(m_ref.shape, NEG, F32)
        l_ref[...] = jnp.zeros(l_ref.shape, F32)
        acc_ref[...] = jnp.zeros(acc_ref.shape, F32)

        def attend(b, carry):
            off = pl.multiple_of(b * kb, kb)
            for h in range(C_HEADS):
                ss, gs = head_scores(off, h)
                m_old = m_ref[h]
                m_new = m_old
                for s in ss:
                    m_new = jnp.maximum(m_new, jnp.max(col_max(s), axis=0, keepdims=True))
                alpha = jnp.exp2(m_old - m_new)
                ps = [jnp.exp2(s - m_new) for s in ss]
                l_ref[h] = alpha * l_ref[h] + total(sum(col_sum(p) for p in ps))
                p_all = jnp.concatenate([p.astype(BF16) for p in ps], axis=0)
                acc_ref[h] = alpha * acc_ref[h] + jnp.dot(vt_ref[b, gs, :], p_all,
                                                          preferred_element_type=F32)
                m_ref[h] = m_new
            return carry

        lax.fori_loop(0, nkb, attend, 0)

    for h in range(C_HEADS):
        o = acc_ref[h] / l_ref[h]
        o_ref[:, h * C_HEAD_DIM:(h + 1) * C_HEAD_DIM] = o.T.astype(o_ref.dtype)


def dsa_t(ob, wi_t, vt, *, nq, tq, kb, qb, pb, cb, topk):
    dq = C_HEADS * C_HEAD_DIM
    n_kv = C_KV_HEADS * C_HEAD_DIM
    n_qi = IDX_HEADS * IDX_DIM
    s_pad = nq
    return pl.pallas_call(
        functools.partial(_dsa_t_kernel, tq=tq, kb=kb, qb=qb, pb=pb, cb=cb, n_valid=nq, q_pos0=0, topk=topk),
        grid=(1, nq // tq),
        in_specs=[pl.BlockSpec((tq, dq), lambda b, i: (i, 0)),
                  pl.BlockSpec((tq, n_qi), lambda b, i: (i, (dq + 2 * n_kv) // n_qi)),
                  pl.BlockSpec((IDX_HEADS, tq), lambda b, i: (0, i)),
                  pl.BlockSpec((nq, n_kv), lambda b, i: (0, dq // n_kv)),
                  pl.BlockSpec(vt.shape, lambda b, i: (0, 0, 0)),
                  pl.BlockSpec((nq, LANES), lambda b, i: (0, (dq + 2 * n_kv + n_qi) // LANES))],
        out_specs=pl.BlockSpec((tq, dq), lambda b, i: (i, 0)),
        out_shape=jax.ShapeDtypeStruct((nq, dq), BF16),
        scratch_shapes=[pltpu.VMEM((s_pad, tq), F32),
                        pltpu.VMEM((IDX_HEADS, tq, LANES), BF16),
                        pltpu.VMEM((C_HEADS, C_HEAD_DIM, tq), F32),
                        pltpu.VMEM((C_HEADS, 1, tq), F32),
                        pltpu.VMEM((C_HEADS, 1, tq), F32),
                        pltpu.VMEM((C_HEADS, 8, tq), F32),
                        pltpu.VMEM((C_KV_HEADS, 1, tq), F32)],
        compiler_params=_params("arbitrary", "arbitrary"),
    )(ob, ob, wi_t, ob, vt, ob)


def _router_kernel(y_ref, g_ref, wr_ref, h_ref, gate_ref):
    h = _rms(y_ref[...], g_ref[...])
    h_ref[...] = h.astype(BF16)
    logits = jnp.dot(h, wr_ref[...], preferred_element_type=F32, precision=lax.Precision.HIGHEST)
    lane = lax.broadcasted_iota(jnp.int32, logits.shape, 1)
    lg = jnp.where(lane < N_EXPERTS, logits, NEG)
    m1 = jnp.max(lg, axis=1, keepdims=True)
    i1 = jnp.min(jnp.where(lg == m1, lane, LANES), axis=1, keepdims=True)
    lg2 = jnp.where(lane == i1, NEG, lg)
    m2 = jnp.max(lg2, axis=1, keepdims=True)
    i2 = jnp.min(jnp.where(lg2 == m2, lane, LANES), axis=1, keepdims=True)
    e = jnp.exp(m2 - m1)
    g1 = 1.0 / (1.0 + e)
    g2 = e / (1.0 + e)
    meta = jnp.where(lane == 0, i1.astype(F32), jnp.where(lane == 1, i2.astype(F32),
                     jnp.where(lane == 2, g1, jnp.where(lane == 3, g2, 0.0))))
    gate_ref[...] = meta


def router(y, g, w_router_pad, tm):
    m, d = y.shape
    return pl.pallas_call(
        _router_kernel,
        grid=(m // tm,),
        in_specs=[pl.BlockSpec((tm, d), lambda i: (i, 0)),
                  pl.BlockSpec((1, d), lambda i: (0, 0)),
                  pl.BlockSpec((d, LANES), lambda i: (0, 0))],
        out_specs=[pl.BlockSpec((tm, d), lambda i: (i, 0)),
                   pl.BlockSpec((tm, LANES), lambda i: (i, 0))],
        out_shape=[jax.ShapeDtypeStruct((m, d), BF16), jax.ShapeDtypeStruct((m, LANES), F32)],
        compiler_params=_params("parallel"),
    )(y, g.reshape(1, d), w_router_pad)


def route_plan(meta, sup):
    m = meta.shape[0]
    n_tiles = (2 * m) // sup + N_EXPERTS
    e_all = jnp.concatenate([meta[:, 0], meta[:, 1]]).astype(jnp.int32)
    onehot = (e_all[:, None] == jnp.arange(N_EXPERTS)[None, :]).astype(jnp.int32)
    rank = jnp.sum((jnp.cumsum(onehot, axis=0) - onehot) * onehot, axis=1)
    counts = jnp.sum(onehot, axis=0)
    n_super = (counts + sup - 1) // sup
    super_end = jnp.cumsum(n_super)
    super_start = super_end - n_super
    pos = (super_start * sup)[e_all] + rank
    tiles = jnp.arange(n_tiles)
    used = super_end[-1]
    t_eff = jnp.minimum(tiles, used - 1)
    tile_expert = jnp.minimum(jnp.searchsorted(super_end, t_eff, side="right"), N_EXPERTS - 1).astype(jnp.int32)
    rows = jnp.clip(counts[tile_expert] - (t_eff - super_start[tile_expert]) * sup, 0, sup)
    tile_rows = jnp.where(tiles < used, rows, 0).astype(jnp.int32)
    token = jnp.concatenate([jnp.arange(m), jnp.arange(m)]).astype(jnp.int32)
    row_token = (jnp.arange(n_tiles * sup, dtype=jnp.int32) % m).at[pos].set(token)
    return pos.astype(jnp.int32), row_token, tile_expert, tile_rows


def _gather_rows_kernel(tok_ref, h_ref, o_ref, sem, *, batch):
    base = pl.program_id(0) * batch

    def issue(j, carry):
        pltpu.make_async_copy(h_ref.at[tok_ref[base + j]], o_ref.at[j], sem).start()
        return carry

    lax.fori_loop(0, batch, issue, 0)
    pltpu.make_async_copy(o_ref, o_ref, sem).wait()


def gather_rows(h3, row_token, batch):
    n_rows = row_token.shape[0]
    blk = (batch,) + h3.shape[1:]
    return pl.pallas_call(
        functools.partial(_gather_rows_kernel, batch=batch),
        grid_spec=pltpu.PrefetchScalarGridSpec(
            num_scalar_prefetch=1,
            grid=(n_rows // batch,),
            in_specs=[pl.BlockSpec(memory_space=pl.ANY)],
            out_specs=pl.BlockSpec(blk, lambda i, tok: (i, 0, 0)),
            scratch_shapes=[pltpu.SemaphoreType.DMA(())]),
        out_shape=jax.ShapeDtypeStruct((n_rows,) + h3.shape[1:], h3.dtype),
        compiler_params=_params("arbitrary"),
    )(row_token, h3)


def _grouped_ffn_kernel(te_ref, tr_ref, x_ref, wg_ref, wu_ref, wd_ref, o_ref,
                        wgb_ref, wub_ref, wdb_ref, *, sup, sub):
    t = pl.program_id(0)
    f = pl.program_id(1)
    rows = tr_ref[t]

    @pl.when(rows > 0)
    def _():
        wgb_ref[...] = wg_ref[...].astype(BF16)
        wub_ref[...] = wu_ref[...].astype(BF16)
        wdb_ref[...] = wd_ref[...].astype(BF16)

    for s in range(sup // sub):
        sl = slice(s * sub, (s + 1) * sub)

        @pl.when(jnp.logical_and(s * sub >= rows, f == 0))
        def _():
            o_ref[sl, :] = jnp.zeros((sub, o_ref.shape[1]), F32)

        @pl.when(s * sub < rows)
        def _():
            part = _swiglu_tile(x_ref[sl, :], wgb_ref[...], wub_ref[...], wdb_ref[...])

            @pl.when(f == 0)
            def _():
                o_ref[sl, :] = part

            @pl.when(f > 0)
            def _():
                o_ref[sl, :] += part


def grouped_ffn(xs, tile_expert, tile_rows, wg, wu, wd, sup, sub, tf):
    n_rows, d = xs.shape
    ff = wg.shape[2]
    n_f = ff // tf
    fidx = lambda t, f, te, tr: jnp.where(tr[t] > 0, f, n_f - 1)
    return pl.pallas_call(
        functools.partial(_grouped_ffn_kernel, sup=sup, sub=sub),
        grid_spec=pltpu.PrefetchScalarGridSpec(
            num_scalar_prefetch=2,
            grid=(n_rows // sup, n_f),
            in_specs=[pl.BlockSpec((sup, d), lambda t, f, te, tr: (t, 0), pipeline_mode=pl.Buffered(1)),
                      pl.BlockSpec((None, d, tf), lambda t, f, te, tr: (te[t], 0, fidx(t, f, te, tr))),
                      pl.BlockSpec((None, d, tf), lambda t, f, te, tr: (te[t], 0, fidx(t, f, te, tr))),
                      pl.BlockSpec((None, tf, d), lambda t, f, te, tr: (te[t], fidx(t, f, te, tr), 0))],
            out_specs=pl.BlockSpec((sup, d), lambda t, f, te, tr: (t, 0), pipeline_mode=pl.Buffered(1)),
            scratch_shapes=[pltpu.VMEM((d, tf), BF16), pltpu.VMEM((d, tf), BF16), pltpu.VMEM((tf, d), BF16)]),
        out_shape=jax.ShapeDtypeStruct((n_rows, d), F32),
        compiler_params=_params("arbitrary", "arbitrary"),
    )(tile_expert, tile_rows, xs, wg, wu, wd)


def _combine_kernel(pos_ref, y_ref, meta_ref, g_ref, ys_ref, op_ref, os_ref, buf_ref, x_ref, sem,
                    *, n_tok, tmc, prompt_tiles):
    i = pl.program_id(0)
    base = i * tmc

    def issue(j, carry):
        t = base + j
        pltpu.make_async_copy(ys_ref.at[pos_ref[t]], buf_ref.at[0, j], sem).start()
        pltpu.make_async_copy(ys_ref.at[pos_ref[n_tok + t]], buf_ref.at[1, j], sem).start()
        return carry

    lax.fori_loop(0, tmc, issue, 0)
    pltpu.make_async_copy(buf_ref.at[0], buf_ref.at[0], sem).wait()
    pltpu.make_async_copy(buf_ref.at[1], buf_ref.at[1], sem).wait()
    g1 = meta_ref[:, 2:3]
    g2 = meta_ref[:, 3:4]
    ss = jnp.zeros((tmc, 1), F32)
    for c in range(buf_ref.shape[2]):
        cs = slice(c * LANES, (c + 1) * LANES)
        x = y_ref[:, cs] + (buf_ref[0, :, c, :] * g1 + buf_ref[1, :, c, :] * g2)
        ss = ss + jnp.sum(x * x, axis=1, keepdims=True)
        x_ref[:, cs] = x
    inv = lax.rsqrt(ss / x_ref.shape[1] + NORM_EPS)

    @pl.when(i < prompt_tiles)
    def _():
        op_ref[...] = x_ref[...] * inv * g_ref[...]

    @pl.when(i >= prompt_tiles)
    def _():
        os_ref[...] = x_ref[...] * inv * g_ref[...]


def combine_final(y, meta, g_final, ys3, pos, n_prompt, tmc):
    m, d = y.shape
    slab = ys3.shape[1:]
    prompt_tiles = n_prompt // tmc
    return pl.pallas_call(
        functools.partial(_combine_kernel, n_tok=m, tmc=tmc, prompt_tiles=prompt_tiles),
        grid_spec=pltpu.PrefetchScalarGridSpec(
            num_scalar_prefetch=1,
            grid=(m // tmc,),
            in_specs=[pl.BlockSpec((tmc, d), lambda i, pos: (i, 0)),
                      pl.BlockSpec((tmc, LANES), lambda i, pos: (i, 0)),
                      pl.BlockSpec((1, d), lambda i, pos: (0, 0)),
                      pl.BlockSpec(memory_space=pl.ANY)],
            out_specs=[pl.BlockSpec((tmc, d), lambda i, pos: (jnp.minimum(i, prompt_tiles - 1), 0)),
                       pl.BlockSpec((tmc, d), lambda i, pos: (jnp.maximum(i - prompt_tiles, 0), 0))],
            scratch_shapes=[pltpu.VMEM((2, tmc) + slab, F32), pltpu.VMEM((tmc, d), F32),
                            pltpu.SemaphoreType.DMA(())]),
        out_shape=[jax.ShapeDtypeStruct((n_prompt, d), F32), jax.ShapeDtypeStruct((m - n_prompt, d), F32)],
        compiler_params=_params("arbitrary"),
    )(pos, y, meta, g_final.reshape(1, d), ys3)


def kernel(x_prompt, x_sample, cache_a_k, cache_a_v, state_pool, cache_c_k, cache_c_v, cache_c_idx,
           norm_mix, norm_ffn, norm_final, w_in_even, w_out_even, a_rel_bias, pool_w, pool_scale,
           ffn_w_gate, ffn_w_up, ffn_w_down, w_in_odd, w_out_odd,
           moe_router, moe_w_gate, moe_w_up, moe_w_down):
    nbp, lp, d = x_prompt.shape
    nb, ds, _ = x_sample.shape
    past = cache_c_k.shape[2]
    a_len = cache_a_k.shape[2]
    assert nbp == 1 and lp % 512 == 0 and (nb * ds) % 512 == 0 and ds == POOL_HALO and past >= POOL_HALO
    ns = nb * ds
    m = lp + ns
    tm = 512
    bf = lambda t: t.astype(BF16)

    x = jnp.concatenate([x_prompt.reshape(lp, d), x_sample.reshape(ns, d)], axis=0)

    proj0 = norm_proj(x, norm_mix[0], bf(w_in_even[0]), tm, 1024)
    k0 = proj0[:, A_WIDTH:2 * A_WIDTH]
    v0 = proj0[:, 2 * A_WIDTH:3 * A_WIDTH]
    u0 = proj0[:, 3 * A_WIDTH:]

    pad = A_PREV_CHUNKS * CHUNK
    bias_p = _rel_bias_tile(a_rel_bias[0], 0, CHUNK, -pad, A_BAND)
    a_p = band_prompt(proj0, _pair_rows(bias_p), lp)

    k_pos = past - a_len + jnp.arange(a_len + ds)
    q_pos = past + jnp.arange(ds)
    qch, kch = q_pos // CHUNK, k_pos // CHUNK
    ok = ((k_pos[None, :] >= 0) & (kch[None, :] <= qch[:, None])
          & (kch[None, :] >= qch[:, None] - A_PREV_CHUNKS))
    bias_s = jnp.where(ok[None], _rel_bias_tile(a_rel_bias[0], past, ds, past - a_len, a_len + ds), NEG)
    a_s = band_sample(proj0, cache_a_k[0].reshape(nb, a_len, A_WIDTH),
                      cache_a_v[0].reshape(nb, a_len, A_WIDTH), _pair_rows(bias_s), lp, nb, ds)

    u_s = u0[lp:].reshape(nb, ds, B_WIDTH)
    u_hist = jnp.concatenate([state_pool[0], u_s], axis=1)
    u_ext = jnp.concatenate([jnp.zeros((nb, POOL_HALO - B_HIST, B_WIDTH), F32), u_hist], axis=1)
    pw = bf(pool_w[0])
    ps = pool_scale[0].reshape(1, B_WIDTH)
    p_p = pool_prompt(proj0, pw, ps, lp, tm)
    p_s = pool_sample(u_ext, pw, ps, past)

    a = jnp.concatenate([a_p, a_s], axis=0)
    p = jnp.concatenate([p_p, p_s], axis=0)
    wo = bf(w_out_even[0])
    y = mm_res([a, p], [wo[:A_WIDTH], wo[A_WIDTH:]], x, tm, 1024)
    y = ffn(y, norm_ffn[0], bf(ffn_w_gate[0]), bf(ffn_w_up[0]), bf(ffn_w_down[0]), tm, 512)

    n_q = C_HEADS * C_HEAD_DIM
    n_kv = C_KV_HEADS * C_HEAD_DIM
    n_qi = IDX_HEADS * IDX_DIM
    n_main = n_q + 2 * n_kv + n_qi
    w1 = w_in_odd[0]
    w_tail = jnp.pad(w1[:, n_main:], ((0, 0), (0, LANES - (w1.shape[1] - n_main))))
    main = norm_proj(y, norm_mix[1], bf(w1[:, :n_main]), tm, 512)
    tail = norm_proj(y, norm_mix[1], bf(w_tail), tm, LANES)
    pos = jnp.concatenate([jnp.arange(lp), jnp.tile(past + jnp.arange(ds), nb)])
    kv1, t32, ob = rope_all(main, tail, _rope_tables(pos, C_HEAD_DIM), _rope_tables(pos, IDX_DIM), tm)
    k1 = kv1[:, :n_kv]
    v1 = kv1[:, n_kv:]
    ki1 = t32[:, :IDX_DIM]

    kb_p = 128
    vt = ob[:lp, n_q + n_kv:n_q + 2 * n_kv].reshape(lp // kb_p, kb_p, n_kv).transpose(0, 2, 1)
    wi_t = t32[:lp, IDX_DIM:IDX_DIM + IDX_HEADS].T
    o_p = dsa_t(ob, wi_t, vt, nq=lp, tq=256, kb=kb_p, qb=128, pb=256, cb=1024, topk=min(TOPK_MAX, lp // 4))
    o_s = dsa_sample(ob, t32, cache_c_k[0].reshape(nb, past, n_kv), cache_c_v[0].reshape(nb, past, n_kv), cache_c_idx[0],
                     row0=lp, kb=3 * LANES, topk=min(TOPK_MAX, (past + ds) // 4))

    o = jnp.concatenate([o_p, o_s], axis=0)
    y = mm_res([o], [bf(w_out_odd[0])], y, tm, 1024)

    wr = jnp.pad(moe_router[0], ((0, 0), (0, LANES - N_EXPERTS)))
    h, meta = router(y, norm_ffn[1], wr, tm)
    sup = 1536
    pos_rows, row_token, tile_expert, tile_rows = route_plan(meta, sup)
    n_rows = row_token.shape[0]
    slab = (d // LANES, LANES)
    xs = gather_rows(h.reshape((m,) + slab), row_token, 512)
    ys = grouped_ffn(xs.reshape(n_rows, d), tile_expert, tile_rows,
                     moe_w_gate[0], moe_w_up[0], moe_w_down[0], sup, 256, 512)
    y_p, y_s = combine_final(y, meta, norm_final, ys.reshape((n_rows,) + slab), pos_rows, lp, 256)

    y_prompt = y_p.reshape(1, lp, d)
    y_sample = y_s.reshape(nb, ds, d)
    keep = min(A_BAND, lp)
    heads = lambda t, n: t.reshape(1, n, -1, A_HEADS, A_HEAD_DIM)
    a_k_prompt = heads(k0[lp - keep:lp], 1)
    a_v_prompt = heads(v0[lp - keep:lp], 1)
    pool_prompt_out = u0[lp - B_HIST:lp].reshape(1, 1, B_HIST, B_WIDTH)
    c_k_prompt = k1[:lp].reshape(1, 1, lp, C_KV_HEADS, C_HEAD_DIM)
    c_v_prompt = v1[:lp].reshape(1, 1, lp, C_KV_HEADS, C_HEAD_DIM)
    c_idx_prompt = ki1[:lp].reshape(1, 1, lp, IDX_DIM)
    shift = lambda cache, new: jnp.concatenate(
        [cache[0], new.reshape(nb, ds, A_HEADS, A_HEAD_DIM)], axis=1)[:, ds:][None]
    a_k_sample = shift(cache_a_k, k0[lp:])
    a_v_sample = shift(cache_a_v, v0[lp:])
    pool_sample_out = u_hist[:, ds:][None]
    c_k_sample = k1[lp:].reshape(1, nb, ds, C_KV_HEADS, C_HEAD_DIM)
    c_v_sample = v1[lp:].reshape(1, nb, ds, C_KV_HEADS, C_HEAD_DIM)
    c_idx_sample = ki1[lp:].reshape(1, nb, ds, IDX_DIM)
    return (y_prompt, y_sample, a_k_prompt, a_v_prompt, pool_prompt_out,
            c_k_prompt, c_v_prompt, c_idx_prompt,
            a_k_sample, a_v_sample, pool_sample_out,
            c_k_sample, c_v_sample, c_idx_sample)
```

```python
import functools
import math

import jax
import jax.numpy as jnp
from jax import lax
from jax.experimental import pallas as pl
from jax.experimental.pallas import tpu as pltpu

F32 = jnp.float32
BF16 = jnp.bfloat16

NORM_EPS = 1e-6
NEG = -1e30
BIG = 1e30
SHIFT_SLACK = 1.001
MIN_SOFTMAX_SUM = 2.0 ** -80

CHUNK = 64
A_HEADS = 16
A_HEAD_DIM = 64
A_WIDTH = A_HEADS * A_HEAD_DIM
A_PREV_CHUNKS = 8
A_BAND = (A_PREV_CHUNKS + 1) * CHUNK
A_REL_CLIP = 128
B_WINDOWS = (2, 4, 8, 16)
B_GROUP = 256
B_WIDTH = B_GROUP * len(B_WINDOWS)
B_HIST = max(B_WINDOWS) - 1
C_HEADS = 16
C_KV_HEADS = 4
C_HEAD_DIM = 128
C_GROUPS = C_HEADS // C_KV_HEADS
IDX_HEADS = 8
IDX_DIM = 64
TOPK_MAX = 256
ROPE_THETA = 500000.0
ROPE_FRAC = 4
N_EXPERTS = 8

LANES = 128
POOL_HALO = 16
BAND_CHUNKS_PER_TRIP = 2
VMEM_LIMIT = 56 * 1024 * 1024

_NT = (((1,), (1,)), ((), ()))


def _params(*sem):
    return pltpu.CompilerParams(dimension_semantics=sem, vmem_limit_bytes=VMEM_LIMIT)


def _rms(x, g):
    ms = jnp.mean(x * x, axis=-1, keepdims=True)
    return x * lax.rsqrt(ms + NORM_EPS) * g


def _norm_proj_kernel(x_ref, g_ref, w_ref, o_ref, h_ref):
    @pl.when(pl.program_id(1) == 0)
    def _():
        h_ref[...] = _rms(x_ref[...], g_ref[...]).astype(BF16)

    o_ref[...] = jnp.dot(h_ref[...], w_ref[...], preferred_element_type=F32)


def norm_proj(x, g, w, tm, tn):
    m, d = x.shape
    n = w.shape[1]
    return pl.pallas_call(
        _norm_proj_kernel,
        grid=(m // tm, n // tn),
        in_specs=[pl.BlockSpec((tm, d), lambda i, j: (i, 0)),
                  pl.BlockSpec((1, d), lambda i, j: (0, 0)),
                  pl.BlockSpec((d, tn), lambda i, j: (0, j))],
        out_specs=pl.BlockSpec((tm, tn), lambda i, j: (i, j)),
        out_shape=jax.ShapeDtypeStruct((m, n), F32),
        scratch_shapes=[pltpu.VMEM((tm, d), BF16)],
        compiler_params=_params("parallel", "arbitrary"),
    )(x, g.reshape(1, d), w)


def _band_pairs(q_ref, nq, kw_ref, vw_ref, nk, bias_ref, windows, o_ref):
    lane = lax.broadcasted_iota(jnp.int32, (1, LANES), 1)
    first = lane < A_HEAD_DIM
    for hp in range(A_HEADS // 2):
        cs = slice(hp * LANES, (hp + 1) * LANES)
        for q_row0, k_row0, key_ok in windows:
            qp = q_ref[pl.ds(q_row0, nq), cs]
            kp = kw_ref[pl.ds(k_row0, nk), cs]
            vp = vw_ref[pl.ds(k_row0, nk), cs]
            qm = jnp.concatenate([jnp.where(first, qp, 0.0), jnp.where(first, 0.0, qp)], axis=0).astype(BF16)
            s = lax.dot_general(qm, kp, _NT, preferred_element_type=F32)
            s = s * (A_HEAD_DIM ** -0.5) + bias_ref[hp]
            if key_ok is not None:
                s = jnp.where(key_ok, s, NEG)
            mx = jnp.max(s, axis=-1, keepdims=True)
            e = jnp.exp(s - mx)
            l = jnp.sum(e, axis=-1, keepdims=True)
            o = jnp.dot(e.astype(BF16), vp, preferred_element_type=F32) / l
            o_ref[pl.ds(q_row0, nq), cs] = jnp.where(first, o[:nq], o[nq:]).astype(o_ref.dtype)


def _band_prompt_kernel(q_ref, kp_ref, kc_ref, vp_ref, vc_ref, bias_ref, o_ref, kw_ref, vw_ref, *, qb, pad):
    i = pl.program_id(0)
    kw_ref[0:pad, :] = kp_ref[...].astype(BF16)
    kw_ref[pad:pad + qb, :] = kc_ref[...].astype(BF16)
    vw_ref[0:pad, :] = vp_ref[...].astype(BF16)
    vw_ref[pad:pad + qb, :] = vc_ref[...].astype(BF16)
    col = lax.broadcasted_iota(jnp.int32, (1, A_BAND), 1)

    def chunks(c2, carry):
        windows = []
        for u in range(BAND_CHUNKS_PER_TRIP):
            cc = c2 * BAND_CHUNKS_PER_TRIP + u
            r0 = pl.multiple_of(cc * CHUNK, CHUNK)
            first_valid = jnp.where(i == 0, pad - cc * CHUNK, 0)
            windows.append((r0, r0, col >= first_valid))
        _band_pairs(q_ref, CHUNK, kw_ref, vw_ref, A_BAND, bias_ref, windows, o_ref)
        return carry

    lax.fori_loop(0, qb // CHUNK // BAND_CHUNKS_PER_TRIP, chunks, 0)


def band_prompt(proj, bias, lp):
    pad = A_PREV_CHUNKS * CHUNK
    qb = pad
    blk = (qb, A_WIDTH)
    prev = lambda c: (lambda i: (jnp.maximum(i - 1, 0), c))
    cur = lambda c: (lambda i: (i, c))
    return pl.pallas_call(
        functools.partial(_band_prompt_kernel, qb=qb, pad=pad),
        grid=(lp // qb,),
        in_specs=[pl.BlockSpec(blk, cur(0)),
                  pl.BlockSpec(blk, prev(1)), pl.BlockSpec(blk, cur(1)),
                  pl.BlockSpec(blk, prev(2)), pl.BlockSpec(blk, cur(2)),
                  pl.BlockSpec((A_HEADS // 2, 2 * CHUNK, A_BAND), lambda i: (0, 0, 0))],
        out_specs=pl.BlockSpec(blk, lambda i: (i, 0)),
        out_shape=jax.ShapeDtypeStruct((lp, A_WIDTH), BF16),
        scratch_shapes=[pltpu.VMEM((pad + qb, A_WIDTH), BF16), pltpu.VMEM((pad + qb, A_WIDTH), BF16)],
        compiler_params=_params("parallel"),
    )(proj, proj, proj, proj, proj, bias)


def _band_sample_kernel(q_ref, kn_ref, vn_ref, ck_ref, cv_ref, bias_ref, o_ref, kw_ref, vw_ref, *, a_len, ds):
    kw_ref[0:a_len, :] = ck_ref[...].astype(BF16)
    kw_ref[a_len:a_len + ds, :] = kn_ref[...].astype(BF16)
    vw_ref[0:a_len, :] = cv_ref[...].astype(BF16)
    vw_ref[a_len:a_len + ds, :] = vn_ref[...].astype(BF16)
    _band_pairs(q_ref, ds, kw_ref, vw_ref, a_len + ds, bias_ref, [(0, 0, None)], o_ref)


def band_sample(proj, cache_k, cache_v, bias, lp, nb, ds):
    a_len = cache_k.shape[1]
    row = lambda c: (lambda b: (lp // ds + b, c))
    return pl.pallas_call(
        functools.partial(_band_sample_kernel, a_len=a_len, ds=ds),
        grid=(nb,),
        in_specs=[pl.BlockSpec((ds, A_WIDTH), row(0)),
                  pl.BlockSpec((ds, A_WIDTH), row(1)),
                  pl.BlockSpec((ds, A_WIDTH), row(2)),
                  pl.BlockSpec((None, a_len, A_WIDTH), lambda b: (b, 0, 0)),
                  pl.BlockSpec((None, a_len, A_WIDTH), lambda b: (b, 0, 0)),
                  pl.BlockSpec((A_HEADS // 2, 2 * ds, a_len + ds), lambda b: (0, 0, 0))],
        out_specs=pl.BlockSpec((ds, A_WIDTH), lambda b: (b, 0)),
        out_shape=jax.ShapeDtypeStruct((nb * ds, A_WIDTH), BF16),
        scratch_shapes=[pltpu.VMEM((a_len + ds, A_WIDTH), BF16), pltpu.VMEM((a_len + ds, A_WIDTH), BF16)],
        compiler_params=_params("parallel"),
    )(proj, proj, proj, cache_k, cache_v, bias)


def _rel_bias_tile(rel_bias, q0, nq, k0, nk):
    rel_max = q0 - k0 + nq - 1
    rel = jnp.clip(rel_max - jnp.arange(nq + nk - 1), -A_REL_CLIP, A_REL_CLIP) + A_REL_CLIP
    ext = rel_bias[:, rel].astype(F32)
    return jnp.stack([ext[:, nq - 1 - i:nq - 1 - i + nk] for i in range(nq)], axis=1)


def _pair_rows(bias):
    h, nq, nk = bias.shape
    return bias.reshape(h // 2, 2 * nq, nk)


def _pool_kernel(prev_ref, cur_ref, w_ref, sc_ref, o_ref, ext_ref, *, tm, prompt, pos0):
    i = pl.program_id(0)
    prev = prev_ref[...]
    if prompt:
        prev = jnp.where(i == 0, 0.0, prev)
        pos = i * tm + lax.broadcasted_iota(jnp.int32, (tm, 1), 0)
    else:
        pos = pos0 + lax.broadcasted_iota(jnp.int32, (tm, 1), 0)
    ext_ref[0:POOL_HALO, :] = prev
    ext_ref[POOL_HALO:POOL_HALO + tm, :] = cur_ref[...]
    for g, w in enumerate(B_WINDOWS):
        cs = slice(g * B_GROUP, (g + 1) * B_GROUP)
        tok = ext_ref[POOL_HALO:POOL_HALO + tm, cs]
        tot = tok
        for j in range(1, w):
            tot = tot + ext_ref[POOL_HALO - j:POOL_HALO - j + tm, cs]
        cnt = jnp.minimum(pos + 1, w).astype(F32)
        pooled = (tot / cnt - tok).astype(BF16)
        o = jnp.dot(pooled, w_ref[g], preferred_element_type=F32) * sc_ref[:, cs]
        o_ref[:, cs] = o.astype(o_ref.dtype)


def pool_prompt(proj, pool_w, pool_scale, lp, tm):
    ucol = 3 * A_WIDTH // B_WIDTH
    per = tm // POOL_HALO
    return pl.pallas_call(
        functools.partial(_pool_kernel, tm=tm, prompt=True, pos0=0),
        grid=(lp // tm,),
        in_specs=[pl.BlockSpec((POOL_HALO, B_WIDTH), lambda i: (jnp.maximum(i * per - 1, 0), ucol)),
                  pl.BlockSpec((tm, B_WIDTH), lambda i: (i, ucol)),
                  pl.BlockSpec((len(B_WINDOWS), B_GROUP, B_GROUP), lambda i: (0, 0, 0)),
                  pl.BlockSpec((1, B_WIDTH), lambda i: (0, 0))],
        out_specs=pl.BlockSpec((tm, B_WIDTH), lambda i: (i, 0)),
        out_shape=jax.ShapeDtypeStruct((lp, B_WIDTH), BF16),
        scratch_shapes=[pltpu.VMEM((POOL_HALO + tm, B_WIDTH), F32)],
        compiler_params=_params("parallel"),
    )(proj, proj, pool_w, pool_scale)


def pool_sample(u_ext, pool_w, pool_scale, past):
    nb, tot, _ = u_ext.shape
    ds = tot - POOL_HALO
    return pl.pallas_call(
        functools.partial(_pool_kernel, tm=ds, prompt=False, pos0=past),
        grid=(nb,),
        in_specs=[pl.BlockSpec((None, POOL_HALO, B_WIDTH), lambda b: (b, 0, 0)),
                  pl.BlockSpec((None, ds, B_WIDTH), lambda b: (b, POOL_HALO // ds, 0)),
                  pl.BlockSpec((len(B_WINDOWS), B_GROUP, B_GROUP), lambda b: (0, 0, 0)),
                  pl.BlockSpec((1, B_WIDTH), lambda b: (0, 0))],
        out_specs=pl.BlockSpec((ds, B_WIDTH), lambda b: (b, 0)),
        out_shape=jax.ShapeDtypeStruct((nb * ds, B_WIDTH), BF16),
        scratch_shapes=[pltpu.VMEM((POOL_HALO + ds, B_WIDTH), F32)],
        compiler_params=_params("parallel"),
    )(u_ext, u_ext, pool_w, pool_scale)


def _mm_res_kernel(*refs, n_in):
    xs, ws = refs[:n_in], refs[n_in:2 * n_in]
    res_ref, o_ref = refs[2 * n_in], refs[2 * n_in + 1]
    acc = res_ref[...]
    for x_ref, w_ref in zip(xs, ws):
        acc = acc + jnp.dot(x_ref[...], w_ref[...], preferred_element_type=F32)
    o_ref[...] = acc


def mm_res(xs, ws, res, tm, tn):
    m, n = res.shape
    n_in = len(xs)
    in_specs = ([pl.BlockSpec((tm, x.shape[1]), lambda i, j: (i, 0)) for x in xs]
                + [pl.BlockSpec((w.shape[0], tn), lambda i, j: (0, j)) for w in ws]
                + [pl.BlockSpec((tm, tn), lambda i, j: (i, j))])
    return pl.pallas_call(
        functools.partial(_mm_res_kernel, n_in=n_in),
        grid=(m // tm, n // tn),
        in_specs=in_specs,
        out_specs=pl.BlockSpec((tm, tn), lambda i, j: (i, j)),
        out_shape=jax.ShapeDtypeStruct((m, n), F32),
        compiler_params=_params("parallel", "arbitrary"),
    )(*xs, *ws, res)


def _swiglu_tile(h, wg, wu, wd):
    a = jnp.dot(h, wg, preferred_element_type=F32)
    b = jnp.dot(h, wu, preferred_element_type=F32)
    act = (a * jax.nn.sigmoid(a) * b).astype(BF16)
    return jnp.dot(act, wd, preferred_element_type=F32)


def _ffn_kernel(y_ref, g_ref, wg_ref, wu_ref, wd_ref, o_ref, h_ref, acc_ref):
    f = pl.program_id(1)

    @pl.when(f == 0)
    def _():
        h_ref[...] = _rms(y_ref[...], g_ref[...]).astype(BF16)
        acc_ref[...] = jnp.zeros_like(acc_ref)

    acc_ref[...] += _swiglu_tile(h_ref[...], wg_ref[...], wu_ref[...], wd_ref[...])

    @pl.when(f == pl.num_programs(1) - 1)
    def _():
        o_ref[...] = y_ref[...] + acc_ref[...]


def ffn(y, g, wg, wu, wd, tm, tf):
    m, d = y.shape
    ff = wg.shape[1]
    return pl.pallas_call(
        _ffn_kernel,
        grid=(m // tm, ff // tf),
        in_specs=[pl.BlockSpec((tm, d), lambda i, f: (i, 0)),
                  pl.BlockSpec((1, d), lambda i, f: (0, 0)),
                  pl.BlockSpec((d, tf), lambda i, f: (0, f)),
                  pl.BlockSpec((d, tf), lambda i, f: (0, f)),
                  pl.BlockSpec((tf, d), lambda i, f: (f, 0))],
        out_specs=pl.BlockSpec((tm, d), lambda i, f: (i, 0)),
        out_shape=jax.ShapeDtypeStruct((m, d), F32),
        scratch_shapes=[pltpu.VMEM((tm, d), BF16), pltpu.VMEM((tm, d), F32)],
        compiler_params=_params("parallel", "arbitrary"),
    )(y, g.reshape(1, d), wg, wu, wd)


def _rope_tables(pos, head_dim):
    rot = head_dim // ROPE_FRAC
    half = rot // 2
    inv = jnp.exp(-math.log(ROPE_THETA) * jnp.arange(half, dtype=F32) * (2.0 / rot))
    ang = pos.astype(F32)[:, None] * inv[None, :]
    cos, sin = jnp.cos(ang), jnp.sin(ang)
    m = pos.shape[0]
    one = jnp.ones((m, head_dim - rot), F32)
    zero_r = jnp.zeros((m, head_dim - rot), F32)
    zero_h = jnp.zeros((m, half), F32)
    c = jnp.concatenate([cos, cos, one], axis=1)
    s_dn = jnp.concatenate([-sin, zero_h, zero_r], axis=1)
    s_up = jnp.concatenate([zero_h, sin, zero_r], axis=1)
    rep = LANES // head_dim
    return jnp.stack([jnp.tile(c, (1, rep)), jnp.tile(s_dn, (1, rep)), jnp.tile(s_up, (1, rep))])


def _rot(x, tab_ref, half):
    return (x * tab_ref[0] + pltpu.roll(x, LANES - half, 1) * tab_ref[1]
            + pltpu.roll(x, half, 1) * tab_ref[2])


def _rope_kernel(main_ref, tail_ref, tq_ref, ti_ref, kv_ref, t32_ref, ob_ref,
                 *, n_q, n_k, n_v, n_qi, wi_scale, q_scale):
    half_qk = C_HEAD_DIM // ROPE_FRAC // 2
    half_i = IDX_DIM // ROPE_FRAC // 2
    for c in range(n_q + n_k + n_v + n_qi):
        cs = slice(c * LANES, (c + 1) * LANES)
        x = main_ref[:, cs]
        if c < n_q + n_k:
            x = _rot(x, tq_ref, half_qk)
        elif c >= n_q + n_k + n_v:
            x = _rot(x, ti_ref, half_i)
        if n_q <= c < n_q + n_k + n_v:
            kv_ref[:, (c - n_q) * LANES:(c - n_q + 1) * LANES] = x
        ob_ref[:, cs] = (x * q_scale if c < n_q else x).astype(BF16)
    t = tail_ref[...]
    lane = lax.broadcasted_iota(jnp.int32, (1, LANES), 1)
    r = _rot(t, ti_ref, half_i)
    t32_ref[...] = jnp.where(lane < IDX_DIM, r, t * wi_scale)
    c = n_q + n_k + n_v + n_qi
    ob_ref[:, c * LANES:(c + 1) * LANES] = jnp.where(lane < IDX_DIM, r, pltpu.roll(r, IDX_DIM, 1)).astype(BF16)


def rope_all(main, tail, tab_qk, tab_idx, tm):
    m, nmain = main.shape
    n_q = C_HEADS * C_HEAD_DIM // LANES
    n_k = C_KV_HEADS * C_HEAD_DIM // LANES
    n_qi = IDX_HEADS * IDX_DIM // LANES
    wi_scale = (IDX_HEADS ** -0.5) * (IDX_DIM ** -0.5)
    q_scale = (C_HEAD_DIM ** -0.5) * math.log2(math.e)
    row = lambda i: (i, 0)
    return pl.pallas_call(
        functools.partial(_rope_kernel, n_q=n_q, n_k=n_k, n_v=n_k, n_qi=n_qi, wi_scale=wi_scale, q_scale=q_scale),
        grid=(m // tm,),
        in_specs=[pl.BlockSpec((tm, nmain), row),
                  pl.BlockSpec((tm, LANES), row),
                  pl.BlockSpec((3, tm, LANES), lambda i: (0, i, 0)),
                  pl.BlockSpec((3, tm, LANES), lambda i: (0, i, 0))],
        out_specs=[pl.BlockSpec((tm, 2 * n_k * LANES), row),
                   pl.BlockSpec((tm, LANES), row),
                   pl.BlockSpec((tm, nmain + LANES), row)],
        out_shape=[jax.ShapeDtypeStruct((m, 2 * n_k * LANES), F32),
                   jax.ShapeDtypeStruct((m, LANES), F32),
                   jax.ShapeDtypeStruct((m, nmain + LANES), BF16)],
        compiler_params=_params("parallel"),
    )(main, tail, tab_qk, tab_idx)


def _dsa_kernel(q_ref, qi_ref, wi_ref, kn_ref, vn_ref, kin_ref, ck_ref, cv_ref, ci_ref, o_ref,
                k_ref, v_ref, ki_ref, sc_ref, qs_ref, acc_ref, m_ref, l_ref,
                *, tq, kb, past, n_valid, q_pos0, topk):
    i = pl.program_id(1)
    rows = C_GROUPS * tq
    sub = kb // LANES
    kf = float(topk)

    new = past + tq
    ci = ci_ref[...].astype(BF16)
    k_ref[0:past, :] = ck_ref[...].astype(BF16)
    v_ref[0:past, :] = cv_ref[...].astype(BF16)
    ki_ref[0:past, :] = jnp.concatenate([ci, ci], axis=1)
    k_ref[past:new, :] = kn_ref[...]
    v_ref[past:new, :] = vn_ref[...]
    ki_ref[past:new, :] = kin_ref[...]
    n_pad = k_ref.shape[0] - new
    k_ref[new:, :] = jnp.zeros((n_pad, k_ref.shape[1]), BF16)
    v_ref[new:, :] = jnp.zeros((n_pad, v_ref.shape[1]), BF16)
    ki_ref[new:, :] = jnp.zeros((n_pad, ki_ref.shape[1]), BF16)

    q_pos = q_pos0 + i * tq + lax.broadcasted_iota(jnp.int32, (tq, 1), 0)
    q_chunk = q_pos // CHUNK
    last_chunk = (q_pos0 + i * tq + tq - 1) // CHUNK
    kv_limit = jnp.minimum(n_valid, (last_chunk + 1) * CHUNK)
    nkb = (kv_limit + kb - 1) // kb

    lane = lax.broadcasted_iota(jnp.int32, (1, LANES), 1)
    first = lane < IDX_DIM
    wi = wi_ref[:, IDX_DIM:IDX_DIM + IDX_HEADS]

    def score_block(b, carry):
        for c in range(sub):
            off = pl.multiple_of(b * kb + c * LANES, LANES)
            kib = ki_ref[pl.ds(off, LANES), :]
            acc = jnp.zeros((tq, LANES), F32)
            for hp in range(IDX_HEADS // 2):
                qp = qi_ref[:, hp * LANES:(hp + 1) * LANES]
                for half in range(2):
                    keep = first if half == 0 else jnp.logical_not(first)
                    qm = jnp.where(keep, qp, jnp.zeros_like(qp))
                    d = lax.dot_general(qm, kib, _NT, preferred_element_type=F32)
                    h = hp * 2 + half
                    acc = acc + jnp.maximum(d, 0.0) * wi[:, h:h + 1]
            k_pos = off + lane
            adm = jnp.logical_and(k_pos // CHUNK <= q_chunk, k_pos < n_valid)
            sc_ref[b, :, c * LANES:(c + 1) * LANES] = jnp.where(adm, acc, NEG)
        return carry

    lax.fori_loop(0, nkb, score_block, 0)

    def lane_sum(x):
        return jnp.sum(x, axis=1, keepdims=True)

    def count_ge(t):
        def body(b, acc):
            for c in range(sub):
                blk = sc_ref[b, :, c * LANES:(c + 1) * LANES]
                acc = acc + jnp.where(blk >= t, 1.0, 0.0)
            return acc
        return lane_sum(lax.fori_loop(0, nkb, body, jnp.zeros((tq, LANES), F32)))

    def stats(b, carry):
        mx, mn, cnt = carry
        for c in range(sub):
            blk = sc_ref[b, :, c * LANES:(c + 1) * LANES]
            ok = blk > 0.5 * NEG
            mx = jnp.maximum(mx, blk)
            mn = jnp.minimum(mn, jnp.where(ok, blk, BIG))
            cnt = cnt + jnp.where(ok, 1.0, 0.0)
        return mx, mn, cnt

    mx, mn, cnt = lax.fori_loop(
        0, nkb, stats,
        (jnp.full((tq, LANES), NEG, F32), jnp.full((tq, LANES), BIG, F32), jnp.zeros((tq, LANES), F32)))
    row_max = jnp.max(mx, axis=1, keepdims=True)
    row_min = jnp.min(mn, axis=1, keepdims=True)
    n_adm = lane_sum(cnt)

    done0 = jnp.where(n_adm <= kf, 1.0, 0.0)
    state0 = (row_min, jnp.full((tq, 1), BIG, F32), row_max, jnp.full((tq, 1), 0.5 * NEG, F32), done0)

    def bisect(state, n_steps):
        def cond(c):
            it, st = c
            return jnp.logical_and(it < n_steps, jnp.min(st[4]) < 0.5)

        def body(c):
            it, (lo, hi, mid, thr, done) = c
            cnt = count_ge(mid)
            live = done < 0.5
            hit = jnp.logical_and(live, cnt == kf)
            ge = cnt >= kf
            thr = jnp.where(hit, mid, thr)
            done = jnp.where(hit, 1.0, done)
            lo = jnp.where(ge, mid, lo)
            hi = jnp.where(ge, hi, mid)
            return it + 1, (lo, hi, 0.5 * (lo + hi), thr, done)

        return lax.while_loop(cond, body, (jnp.int32(0), state))[1]

    def snap(state):
        lo, hi, mid, thr, done = state

        def body(b, carry):
            v_lo, v_hi = carry
            for c in range(sub):
                blk = sc_ref[b, :, c * LANES:(c + 1) * LANES]
                v_lo = jnp.minimum(v_lo, jnp.where(blk >= lo, blk, BIG))
                v_hi = jnp.maximum(v_hi, jnp.where(blk < hi, blk, NEG))
            return v_lo, v_hi

        v_lo, v_hi = lax.fori_loop(0, nkb, body,
                                   (jnp.full((tq, LANES), BIG, F32), jnp.full((tq, LANES), NEG, F32)))
        v_lo = jnp.min(v_lo, axis=1, keepdims=True)
        v_hi = jnp.max(v_hi, axis=1, keepdims=True)
        live = done < 0.5
        tie = jnp.logical_and(live, v_lo == v_hi)
        thr = jnp.where(tie, v_lo, thr)
        done = jnp.where(tie, 1.0, done)
        lo = jnp.where(live, v_lo, lo)
        return (lo, hi, 0.5 * (lo + hi), thr, done), jnp.where(tie, 1.0, 0.0)

    state = bisect(state0, 32)

    def refine_cond(c):
        rounds, st, _ = c
        return jnp.logical_and(rounds < 10, jnp.min(st[4]) < 0.5)

    def refine_body(c):
        rounds, st, tie = c
        st, new_tie = snap(st)
        st = bisect(st, 32)
        return rounds + 1, st, jnp.maximum(tie, new_tie)

    _, state, tie = lax.while_loop(refine_cond, refine_body,
                                   (jnp.int32(0), state, jnp.zeros((tq, 1), F32)))
    thr = state[3]
    any_tie = jnp.max(tie) > 0.5

    @pl.when(jnp.logical_not(any_tie))
    def _():
        def body(b, carry):
            for c in range(sub):
                cs = slice(c * LANES, (c + 1) * LANES)
                sc_ref[b, :, cs] = jnp.where(sc_ref[b, :, cs] >= thr, 0.0, NEG)
            return carry
        lax.fori_loop(0, nkb, body, 0)

    @pl.when(any_tie)
    def _():
        def gt_body(b, acc):
            for c in range(sub):
                acc = acc + jnp.where(sc_ref[b, :, c * LANES:(c + 1) * LANES] > thr, 1.0, 0.0)
            return acc
        need = kf - lane_sum(lax.fori_loop(0, nkb, gt_body, jnp.zeros((tq, LANES), F32)))

        def count_eq_upto(j):
            def body(b, acc):
                for c in range(sub):
                    blk = sc_ref[b, :, c * LANES:(c + 1) * LANES]
                    idx = (b * kb + c * LANES + lane).astype(F32)
                    acc = acc + jnp.where(jnp.logical_and(blk == thr, idx <= j), 1.0, 0.0)
                return acc
            return lane_sum(lax.fori_loop(0, nkb, body, jnp.zeros((tq, LANES), F32)))

        def idx_body(_, c):
            lo_j, hi_j = c
            mid_j = jnp.floor(0.5 * (lo_j + hi_j))
            ok = count_eq_upto(mid_j) >= need
            return jnp.where(ok, lo_j, mid_j), jnp.where(ok, mid_j, hi_j)

        n_steps = max(1, math.ceil(math.log2(sc_ref.shape[0] * kb + 1)))
        _, last = lax.fori_loop(
            0, n_steps, idx_body,
            (jnp.full((tq, 1), -1.0, F32), jnp.full((tq, 1), float(sc_ref.shape[0] * kb), F32)))
        last = jnp.where(tie > 0.5, last, BIG)

        def body(b, carry):
            for c in range(sub):
                cs = slice(c * LANES, (c + 1) * LANES)
                blk = sc_ref[b, :, cs]
                idx = (b * kb + c * LANES + lane).astype(F32)
                sel = jnp.logical_or(blk > thr, jnp.logical_and(blk == thr, idx <= last))
                sc_ref[b, :, cs] = jnp.where(sel, 0.0, NEG)
            return carry
        lax.fori_loop(0, nkb, body, 0)

    for g in range(C_KV_HEADS):
        for hh in range(C_GROUPS):
            h = g * C_GROUPS + hh
            qs_ref[g, hh * tq:(hh + 1) * tq, :] = q_ref[:, h * C_HEAD_DIM:(h + 1) * C_HEAD_DIM]
    m_ref[...] = jnp.full(m_ref.shape, NEG, F32)
    l_ref[...] = jnp.zeros(l_ref.shape, F32)
    acc_ref[...] = jnp.zeros(acc_ref.shape, F32)

    def attend(b, carry):
        off = pl.multiple_of(b * kb, kb)
        bias = sc_ref[b]
        bias = jnp.concatenate([bias] * C_GROUPS, axis=0)
        for g in range(C_KV_HEADS):
            cs = slice(g * C_HEAD_DIM, (g + 1) * C_HEAD_DIM)
            kblk = k_ref[pl.ds(off, kb), cs]
            vblk = v_ref[pl.ds(off, kb), cs]
            s = lax.dot_general(qs_ref[g], kblk, _NT, preferred_element_type=F32)
            s = s + bias
            m_old = m_ref[g]
            m_new = jnp.maximum(m_old, jnp.max(s, axis=1, keepdims=True))
            alpha = jnp.exp2(m_old - m_new)
            p = jnp.exp2(s - m_new)
            l_ref[g] = alpha * l_ref[g] + jnp.sum(p, axis=1, keepdims=True)
            acc_ref[g] = alpha * acc_ref[g] + jnp.dot(p.astype(BF16), vblk, preferred_element_type=F32)
            m_ref[g] = m_new
        return carry

    lax.fori_loop(0, nkb, attend, 0)

    for g in range(C_KV_HEADS):
        o = acc_ref[g] / l_ref[g]
        for hh in range(C_GROUPS):
            h = g * C_GROUPS + hh
            o_ref[:, h * C_HEAD_DIM:(h + 1) * C_HEAD_DIM] = o[hh * tq:(hh + 1) * tq, :].astype(o_ref.dtype)


def dsa_sample(ob, t32, cache_k, cache_v, cache_i, *, row0, kb, topk):
    nb, past = cache_k.shape[:2]
    n_kv = C_KV_HEADS * C_HEAD_DIM
    ds = (ob.shape[0] - row0) // nb
    n_q = C_HEADS * C_HEAD_DIM
    n_qi = IDX_HEADS * IDX_DIM
    s_all = past + ds
    s_pad = -(-s_all // kb) * kb
    rows = C_GROUPS * ds
    col = lambda w, off: (lambda b, i: (row0 // ds + b, off // w))
    cmap = lambda b, i: (b, 0, 0)
    return pl.pallas_call(
        functools.partial(_dsa_kernel, tq=ds, kb=kb, past=past, n_valid=s_all, q_pos0=past, topk=topk),
        grid=(nb, 1),
        in_specs=[pl.BlockSpec((ds, n_q), col(n_q, 0)),
                  pl.BlockSpec((ds, n_qi), col(n_qi, n_q + 2 * n_kv)),
                  pl.BlockSpec((ds, LANES), col(LANES, 0)),
                  pl.BlockSpec((ds, n_kv), col(n_kv, n_q)),
                  pl.BlockSpec((ds, n_kv), col(n_kv, n_q + n_kv)),
                  pl.BlockSpec((ds, LANES), col(LANES, n_q + 2 * n_kv + n_qi)),
                  pl.BlockSpec((None, past, n_kv), cmap),
                  pl.BlockSpec((None, past, n_kv), cmap),
                  pl.BlockSpec((None, past, IDX_DIM), cmap)],
        out_specs=pl.BlockSpec((ds, n_q), lambda b, i: (b, 0)),
        out_shape=jax.ShapeDtypeStruct((nb * ds, n_q), BF16),
        scratch_shapes=[pltpu.VMEM((s_pad, n_kv), BF16),
                        pltpu.VMEM((s_pad, n_kv), BF16),
                        pltpu.VMEM((s_pad, LANES), BF16),
                        pltpu.VMEM((s_pad // kb, ds, kb), F32),
                        pltpu.VMEM((C_KV_HEADS, rows, C_HEAD_DIM), BF16),
                        pltpu.VMEM((C_KV_HEADS, rows, C_HEAD_DIM), F32),
                        pltpu.VMEM((C_KV_HEADS, rows, 1), F32),
                        pltpu.VMEM((C_KV_HEADS, rows, 1), F32)],
        compiler_params=_params("parallel", "arbitrary"),
    )(ob, ob, t32, ob, ob, ob, cache_k, cache_v, cache_i)


def _dsa_t_kernel(q_ref, qi_ref, wi_ref, k_ref, vt_ref, ki_ref, o_ref,
                  sc_ref, qim_ref, acc_ref, m_ref, l_ref, l8_ref, kmax_ref,
                  *, tq, kb, qb, pb, cb, n_valid, q_pos0, topk):
    i = pl.program_id(1)
    kf = float(topk)

    q_pos = q_pos0 + i * tq + lax.broadcasted_iota(jnp.int32, (1, tq), 1)
    q_chunk = q_pos // CHUNK
    last_chunk = (q_pos0 + i * tq + tq - 1) // CHUNK
    kv_limit = jnp.minimum(n_valid, (last_chunk + 1) * CHUNK)
    nkb = (kv_limit + kb - 1) // kb
    ncb = (kv_limit + cb - 1) // cb

    def col_reduce(x, op):
        groups = x.shape[0] // 8
        chains = 8 if groups % 8 == 0 else 1
        return op(op(x.reshape(chains, groups // chains, 8, tq), axis=1), axis=0)

    col_sum = lambda x: col_reduce(x, jnp.sum)
    col_max = lambda x: col_reduce(x, jnp.max)
    col_min = lambda x: col_reduce(x, jnp.min)

    lane = lax.broadcasted_iota(jnp.int32, (1, LANES), 1)
    first = lane < IDX_DIM
    for hp in range(IDX_HEADS // 2):
        qp = qi_ref[:, hp * LANES:(hp + 1) * LANES]
        qim_ref[2 * hp] = jnp.where(first, qp, jnp.zeros_like(qp))
        qim_ref[2 * hp + 1] = jnp.where(first, jnp.zeros_like(qp), qp)
    block_iota = lax.broadcasted_iota(jnp.int32, (pb, 1), 0)

    def score_block(b, carry):
        off = pl.multiple_of(b * pb, pb)
        kib = ki_ref[pl.ds(off, pb), :]
        acc = jnp.zeros((pb, tq), F32)
        for h in range(IDX_HEADS):
            d = lax.dot_general(kib, qim_ref[h], _NT, preferred_element_type=F32)
            acc = acc + jnp.maximum(d, 0.0) * wi_ref[h:h + 1, :]
        k_pos = off + block_iota
        adm = jnp.logical_and(k_pos // CHUNK <= q_chunk, k_pos < n_valid)
        sc_ref[pl.ds(off, pb), :] = jnp.where(adm, acc, NEG)
        return carry

    lax.fori_loop(0, ncb * (cb // pb), score_block, 0)
    key_iota = lax.broadcasted_iota(jnp.int32, (cb, 1), 0)

    def blocks(body, init):
        def step(b, carry):
            off = pl.multiple_of(b * cb, cb)
            return body(off, sc_ref[pl.ds(off, cb), :], carry)
        return lax.fori_loop(0, ncb, step, init)

    def total(x):
        return jnp.sum(x, axis=0, keepdims=True)

    def count_ge(t):
        return total(blocks(lambda off, blk, acc: acc + col_sum(jnp.where(blk >= t, 1.0, 0.0)),
                            jnp.zeros((8, tq), F32)))

    def stats(off, blk, carry):
        mx, mn, cnt, pos, nonneg = carry
        ok = blk > 0.5 * NEG
        return (jnp.maximum(mx, col_max(blk)), jnp.minimum(mn, col_min(jnp.where(ok, blk, BIG))),
                cnt + col_sum(jnp.where(ok, 1.0, 0.0)),
                pos + col_sum(jnp.where(blk > 0.0, 1.0, 0.0)),
                nonneg + col_sum(jnp.where(blk >= 0.0, 1.0, 0.0)))

    zeros8 = jnp.zeros((8, tq), F32)
    mx, mn, cnt, pos, nonneg = blocks(stats, (jnp.full((8, tq), NEG, F32), jnp.full((8, tq), BIG, F32),
                                              zeros8, zeros8, zeros8))
    row_max = jnp.max(mx, axis=0, keepdims=True)
    row_min = jnp.min(mn, axis=0, keepdims=True)
    n_adm = total(cnt)
    n_pos = total(pos)
    n_nonneg = total(nonneg)

    open_row = n_adm > kf
    above = n_pos >= kf
    below = n_nonneg < kf
    zero_tie = jnp.logical_and(open_row, jnp.logical_not(jnp.logical_or(above, below)))
    lo0 = jnp.where(above, 0.0, row_min)
    hi0 = jnp.where(below, 0.0, BIG)
    mid0 = jnp.where(below, 0.5 * (lo0 + hi0), row_max)
    thr0 = jnp.where(zero_tie, 0.0, 0.5 * NEG)
    done0 = jnp.where(jnp.logical_or(jnp.logical_not(open_row), zero_tie), 1.0, 0.0)
    tie0 = jnp.where(zero_tie, 1.0, 0.0)
    state0 = (lo0, hi0, mid0, thr0, done0)

    def bisect(state, n_steps):
        def cond(c):
            it, st = c
            return jnp.logical_and(it < n_steps, jnp.min(st[4]) < 0.5)

        def body(c):
            it, (lo, hi, mid, thr, done) = c
            cnt = count_ge(mid)
            hit = jnp.logical_and(done < 0.5, cnt == kf)
            ge = cnt >= kf
            thr = jnp.where(hit, mid, thr)
            done = jnp.where(hit, 1.0, done)
            lo = jnp.where(ge, mid, lo)
            hi = jnp.where(ge, hi, mid)
            return it + 1, (lo, hi, 0.5 * (lo + hi), thr, done)

        return lax.while_loop(cond, body, (jnp.int32(0), state))[1]

    def snap(state):
        lo, hi, mid, thr, done = state

        def body(off, blk, carry):
            v_lo, v_hi = carry
            return (jnp.minimum(v_lo, col_min(jnp.where(blk >= lo, blk, BIG))),
                    jnp.maximum(v_hi, col_max(jnp.where(blk < hi, blk, NEG))))

        v_lo, v_hi = blocks(body, (jnp.full((8, tq), BIG, F32), jnp.full((8, tq), NEG, F32)))
        v_lo = jnp.min(v_lo, axis=0, keepdims=True)
        v_hi = jnp.max(v_hi, axis=0, keepdims=True)
        live = done < 0.5
        tie = jnp.logical_and(live, v_lo == v_hi)
        thr = jnp.where(tie, v_lo, thr)
        done = jnp.where(tie, 1.0, done)
        lo = jnp.where(live, v_lo, lo)
        return (lo, hi, 0.5 * (lo + hi), thr, done), jnp.where(tie, 1.0, 0.0)

    state = bisect(state0, 32)

    def refine_cond(c):
        rounds, st, _ = c
        return jnp.logical_and(rounds < 10, jnp.min(st[4]) < 0.5)

    def refine_body(c):
        rounds, st, tie = c
        st, new_tie = snap(st)
        st = bisect(st, 32)
        return rounds + 1, st, jnp.maximum(tie, new_tie)

    _, state, tie = lax.while_loop(refine_cond, refine_body, (jnp.int32(0), state, tie0))
    thr = state[3]
    any_tie = jnp.max(tie) > 0.5

    @pl.when(jnp.logical_not(any_tie))
    def _():
        def body(off, blk, carry):
            sc_ref[pl.ds(off, cb), :] = jnp.where(blk >= thr, 0.0, NEG)
            return carry
        blocks(body, 0)

    @pl.when(any_tie)
    def _():
        need = kf - total(blocks(lambda off, blk, acc: acc + col_sum(jnp.where(blk > thr, 1.0, 0.0)),
                                 jnp.zeros((8, tq), F32)))

        def count_eq_upto(j):
            def body(off, blk, acc):
                idx = (off + key_iota).astype(F32)
                return acc + col_sum(jnp.where(jnp.logical_and(blk == thr, idx <= j), 1.0, 0.0))
            return total(blocks(body, jnp.zeros((8, tq), F32)))

        def idx_cond(c):
            it, (lo_j, hi_j, settled) = c
            return jnp.logical_and(it < n_steps, jnp.min(settled) < 0.5)

        def idx_body(c):
            it, (lo_j, hi_j, settled) = c
            mid_j = jnp.floor(0.5 * (lo_j + hi_j))
            cnt = count_eq_upto(mid_j)
            ok = cnt >= need
            live = settled < 0.5
            lo_j = jnp.where(jnp.logical_and(live, jnp.logical_not(ok)), mid_j, lo_j)
            hi_j = jnp.where(jnp.logical_and(live, ok), mid_j, hi_j)
            settled = jnp.where(jnp.logical_or(cnt == need, hi_j - lo_j <= 1.0), 1.0, settled)
            return it + 1, (lo_j, hi_j, settled)

        n_keys = sc_ref.shape[0]
        n_steps = max(1, math.ceil(math.log2(n_keys + 1))) + 1
        _, (_, last, _) = lax.while_loop(
            idx_cond, idx_body,
            (jnp.int32(0), (jnp.full((1, tq), -1.0, F32), jnp.full((1, tq), float(n_keys), F32),
                            jnp.where(tie > 0.5, 0.0, 1.0))))
        last = jnp.where(tie > 0.5, last, BIG)

        def body(off, blk, carry):
            idx = (off + key_iota).astype(F32)
            sel = jnp.logical_or(blk > thr, jnp.logical_and(blk == thr, idx <= last))
            sc_ref[pl.ds(off, cb), :] = jnp.where(sel, 0.0, NEG)
            return carry
        blocks(body, 0)

    halves = kb // qb

    def head_scores(off, h):
        gs = slice((h // C_GROUPS) * C_HEAD_DIM, (h // C_GROUPS + 1) * C_HEAD_DIM)
        out = []
        for j in range(halves):
            rows = pl.ds(off + j * qb, qb)
            s = lax.dot_general(k_ref[rows, gs], q_ref[:, h * C_HEAD_DIM:(h + 1) * C_HEAD_DIM],
                                _NT, preferred_element_type=F32)
            out.append(s + sc_ref[rows, :])
        return out, gs

    @pl.when(i == 0)
    def _():
        for g in range(C_KV_HEADS):
            gs = slice(g * C_HEAD_DIM, (g + 1) * C_HEAD_DIM)

            def body(b, mx):
                kk = k_ref[pl.ds(pl.multiple_of(b * cb, cb), cb), gs].astype(F32)
                return jnp.maximum(mx, jnp.sum(kk * kk, axis=1, keepdims=True))

            mx = lax.fori_loop(0, k_ref.shape[0] // cb, body, jnp.zeros((cb, 1), F32))
            kmax_ref[g] = jnp.full((1, tq), 1.0, F32) * jnp.max(mx)

    ones = jnp.ones((8, C_HEAD_DIM), F32)
    for h in range(C_HEADS):
        qh = q_ref[:, h * C_HEAD_DIM:(h + 1) * C_HEAD_DIM].astype(F32)
        qn2 = lax.dot_general(ones, qh * qh, _NT, preferred_element_type=F32,
                              precision=lax.Precision.HIGHEST)[0:1, :]
        m_ref[h] = jnp.sqrt(qn2 * kmax_ref[h // C_GROUPS]) * SHIFT_SLACK
    l8_ref[...] = jnp.zeros(l8_ref.shape, F32)
    acc_ref[...] = jnp.zeros(acc_ref.shape, F32)

    def attend_fixed(b0, n):
        for h in range(C_HEADS):
            lsum, upd = None, None
            for u in range(n):
                b = b0 + u
                ss, gs = head_scores(pl.multiple_of(b * kb, kb), h)
                ps = [jnp.exp2(s - m_ref[h]) for s in ss]
                part = sum(col_sum(p) for p in ps)
                p_all = jnp.concatenate([p.astype(BF16) for p in ps], axis=0)
                pv = jnp.dot(vt_ref[b, gs, :], p_all, preferred_element_type=F32)
                lsum = part if lsum is None else lsum + part
                upd = pv if upd is None else upd + pv
            l8_ref[h] += lsum
            acc_ref[h] += upd

    def four_blocks(t, carry):
        attend_fixed(4 * t, 4)
        return carry

    lax.fori_loop(0, nkb // 4, four_blocks, 0)

    @pl.when(nkb % 4 >= 2)
    def _():
        attend_fixed((nkb // 4) * 4, 2)
    l_min = jnp.full((1, tq), BIG, F32)
    for h in range(C_HEADS):
        l_ref[h] = total(l8_ref[h])
        l_min = jnp.minimum(l_min, l_ref[h])

    @pl.when(jnp.logical_not(jnp.min(l_min) >= MIN_SOFTMAX_SUM))
    def _():
        m_ref[...] = jnp.full(m_ref.shape, NEG, F32)
        l_ref[...] = jnp.zeros(l_ref.shape, F32)
        acc_ref[...] = jnp.zeros(acc_ref.shape, F32)

        def attend(b, carry):
            off = pl.multiple_of(b * kb, kb)
            for h in range(C_HEADS):
                ss, gs = head_scores(off, h)
                m_old = m_ref[h]
                m_new = m_old
                for s in ss:
                    m_new = jnp.maximum(m_new, jnp.max(col_max(s), axis=0, keepdims=True))
                alpha = jnp.exp2(m_old - m_new)
                ps = [jnp.exp2(s - m_new) for s in ss]
                l_ref[h] = alpha * l_ref[h] + total(sum(col_sum(p) for p in ps))
                p_all = jnp.concatenate([p.astype(BF16) for p in ps], axis=0)
                acc_ref[h] = alpha * acc_ref[h] + jnp.dot(vt_ref[b, gs, :], p_all,
                                                          preferred_element_type=F32)
                m_ref[h] = m_new
            return carry

        lax.fori_loop(0, nkb, attend, 0)

    for h in range(C_HEADS):
        o = acc_ref[h] / l_ref[h]
        o_ref[:, h * C_HEAD_DIM:(h + 1) * C_HEAD_DIM] = o.T.astype(o_ref.dtype)


def dsa_t(ob, wi_t, vt, *, nq, tq, kb, qb, pb, cb, topk):
    dq = C_HEADS * C_HEAD_DIM
    n_kv = C_KV_HEADS * C_HEAD_DIM
    n_qi = IDX_HEADS * IDX_DIM
    s_pad = nq
    assert tq % (2 * kb) == 0 and tq % CHUNK == 0 and nq % cb == 0 and cb % pb == 0 and kb % qb == 0
    return pl.pallas_call(
        functools.partial(_dsa_t_kernel, tq=tq, kb=kb, qb=qb, pb=pb, cb=cb, n_valid=nq, q_pos0=0, topk=topk),
        grid=(1, nq // tq),
        in_specs=[pl.BlockSpec((tq, dq), lambda b, i: (i, 0)),
                  pl.BlockSpec((tq, n_qi), lambda b, i: (i, (dq + 2 * n_kv) // n_qi)),
                  pl.BlockSpec((IDX_HEADS, tq), lambda b, i: (0, i)),
                  pl.BlockSpec((nq, n_kv), lambda b, i: (0, dq // n_kv)),
                  pl.BlockSpec(vt.shape, lambda b, i: (0, 0, 0)),
                  pl.BlockSpec((nq, LANES), lambda b, i: (0, (dq + 2 * n_kv + n_qi) // LANES))],
        out_specs=pl.BlockSpec((tq, dq), lambda b, i: (i, 0)),
        out_shape=jax.ShapeDtypeStruct((nq, dq), BF16),
        scratch_shapes=[pltpu.VMEM((s_pad, tq), F32),
                        pltpu.VMEM((IDX_HEADS, tq, LANES), BF16),
                        pltpu.VMEM((C_HEADS, C_HEAD_DIM, tq), F32),
                        pltpu.VMEM((C_HEADS, 1, tq), F32),
                        pltpu.VMEM((C_HEADS, 1, tq), F32),
                        pltpu.VMEM((C_HEADS, 8, tq), F32),
                        pltpu.VMEM((C_KV_HEADS, 1, tq), F32)],
        compiler_params=_params("arbitrary", "arbitrary"),
    )(ob, ob, wi_t, ob, vt, ob)


def _router_kernel(y_ref, g_ref, wr_ref, h_ref, gate_ref):
    h = _rms(y_ref[...], g_ref[...])
    h_ref[...] = h.astype(BF16)
    logits = jnp.dot(h, wr_ref[...], preferred_element_type=F32, precision=lax.Precision.HIGHEST)
    lane = lax.broadcasted_iota(jnp.int32, logits.shape, 1)
    lg = jnp.where(lane < N_EXPERTS, logits, NEG)
    m1 = jnp.max(lg, axis=1, keepdims=True)
    i1 = jnp.min(jnp.where(lg == m1, lane, LANES), axis=1, keepdims=True)
    lg2 = jnp.where(lane == i1, NEG, lg)
    m2 = jnp.max(lg2, axis=1, keepdims=True)
    i2 = jnp.min(jnp.where(lg2 == m2, lane, LANES), axis=1, keepdims=True)
    e = jnp.exp(m2 - m1)
    g1 = 1.0 / (1.0 + e)
    g2 = e / (1.0 + e)
    meta = jnp.where(lane == 0, i1.astype(F32), jnp.where(lane == 1, i2.astype(F32),
                     jnp.where(lane == 2, g1, jnp.where(lane == 3, g2, 0.0))))
    gate_ref[...] = meta


def router(y, g, w_router_pad, tm):
    m, d = y.shape
    return pl.pallas_call(
        _router_kernel,
        grid=(m // tm,),
        in_specs=[pl.BlockSpec((tm, d), lambda i: (i, 0)),
                  pl.BlockSpec((1, d), lambda i: (0, 0)),
                  pl.BlockSpec((d, LANES), lambda i: (0, 0))],
        out_specs=[pl.BlockSpec((tm, d), lambda i: (i, 0)),
                   pl.BlockSpec((tm, LANES), lambda i: (i, 0))],
        out_shape=[jax.ShapeDtypeStruct((m, d), BF16), jax.ShapeDtypeStruct((m, LANES), F32)],
        compiler_params=_params("parallel"),
    )(y, g.reshape(1, d), w_router_pad)


def route_plan(meta, sup):
    m = meta.shape[0]
    n_tiles = (2 * m) // sup + N_EXPERTS
    e_all = jnp.concatenate([meta[:, 0], meta[:, 1]]).astype(jnp.int32)
    onehot = (e_all[:, None] == jnp.arange(N_EXPERTS)[None, :]).astype(jnp.int32)
    rank = jnp.sum((jnp.cumsum(onehot, axis=0) - onehot) * onehot, axis=1)
    counts = jnp.sum(onehot, axis=0)
    n_super = (counts + sup - 1) // sup
    super_end = jnp.cumsum(n_super)
    super_start = super_end - n_super
    pos = (super_start * sup)[e_all] + rank
    tiles = jnp.arange(n_tiles)
    used = super_end[-1]
    t_eff = jnp.minimum(tiles, used - 1)
    tile_expert = jnp.minimum(jnp.searchsorted(super_end, t_eff, side="right"), N_EXPERTS - 1).astype(jnp.int32)
    rows = jnp.clip(counts[tile_expert] - (t_eff - super_start[tile_expert]) * sup, 0, sup)
    tile_rows = jnp.where(tiles < used, rows, 0).astype(jnp.int32)
    token = jnp.concatenate([jnp.arange(m), jnp.arange(m)]).astype(jnp.int32)
    row_token = (jnp.arange(n_tiles * sup, dtype=jnp.int32) % m).at[pos].set(token)
    return pos.astype(jnp.int32), row_token, tile_expert, tile_rows


def _gather_rows_kernel(tok_ref, h_ref, o_ref, sem, *, batch):
    base = pl.program_id(0) * batch

    def issue(j, carry):
        pltpu.make_async_copy(h_ref.at[tok_ref[base + j]], o_ref.at[j], sem).start()
        return carry

    lax.fori_loop(0, batch, issue, 0)
    pltpu.make_async_copy(o_ref, o_ref, sem).wait()


def gather_rows(h3, row_token, batch):
    n_rows = row_token.shape[0]
    blk = (batch,) + h3.shape[1:]
    return pl.pallas_call(
        functools.partial(_gather_rows_kernel, batch=batch),
        grid_spec=pltpu.PrefetchScalarGridSpec(
            num_scalar_prefetch=1,
            grid=(n_rows // batch,),
            in_specs=[pl.BlockSpec(memory_space=pl.ANY)],
            out_specs=pl.BlockSpec(blk, lambda i, tok: (i, 0, 0)),
            scratch_shapes=[pltpu.SemaphoreType.DMA(())]),
        out_shape=jax.ShapeDtypeStruct((n_rows,) + h3.shape[1:], h3.dtype),
        compiler_params=_params("arbitrary"),
    )(row_token, h3)


def _grouped_ffn_kernel(te_ref, tr_ref, x_ref, wg_ref, wu_ref, wd_ref, o_ref,
                        wgb_ref, wub_ref, wdb_ref, *, sup, sub):
    t = pl.program_id(0)
    f = pl.program_id(1)
    rows = tr_ref[t]

    @pl.when(rows > 0)
    def _():
        wgb_ref[...] = wg_ref[...].astype(BF16)
        wub_ref[...] = wu_ref[...].astype(BF16)
        wdb_ref[...] = wd_ref[...].astype(BF16)

    for s in range(sup // sub):
        sl = slice(s * sub, (s + 1) * sub)

        @pl.when(jnp.logical_and(s * sub >= rows, f == 0))
        def _():
            o_ref[sl, :] = jnp.zeros((sub, o_ref.shape[1]), F32)

        @pl.when(s * sub < rows)
        def _():
            part = _swiglu_tile(x_ref[sl, :], wgb_ref[...], wub_ref[...], wdb_ref[...])

            @pl.when(f == 0)
            def _():
                o_ref[sl, :] = part

            @pl.when(f > 0)
            def _():
                o_ref[sl, :] += part


def grouped_ffn(xs, tile_expert, tile_rows, wg, wu, wd, sup, sub, tf):
    n_rows, d = xs.shape
    ff = wg.shape[2]
    n_f = ff // tf
    fidx = lambda t, f, te, tr: jnp.where(tr[t] > 0, f, n_f - 1)
    return pl.pallas_call(
        functools.partial(_grouped_ffn_kernel, sup=sup, sub=sub),
        grid_spec=pltpu.PrefetchScalarGridSpec(
            num_scalar_prefetch=2,
            grid=(n_rows // sup, n_f),
            in_specs=[pl.BlockSpec((sup, d), lambda t, f, te, tr: (t, 0), pipeline_mode=pl.Buffered(1)),
                      pl.BlockSpec((None, d, tf), lambda t, f, te, tr: (te[t], 0, fidx(t, f, te, tr))),
                      pl.BlockSpec((None, d, tf), lambda t, f, te, tr: (te[t], 0, fidx(t, f, te, tr))),
                      pl.BlockSpec((None, tf, d), lambda t, f, te, tr: (te[t], fidx(t, f, te, tr), 0))],
            out_specs=pl.BlockSpec((sup, d), lambda t, f, te, tr: (t, 0), pipeline_mode=pl.Buffered(1)),
            scratch_shapes=[pltpu.VMEM((d, tf), BF16), pltpu.VMEM((d, tf), BF16), pltpu.VMEM((tf, d), BF16)]),
        out_shape=jax.ShapeDtypeStruct((n_rows, d), F32),
        compiler_params=_params("arbitrary", "arbitrary"),
    )(tile_expert, tile_rows, xs, wg, wu, wd)


def _combine_kernel(pos_ref, y_ref, meta_ref, g_ref, ys_ref, op_ref, os_ref, buf_ref, x_ref, sem,
                    *, n_tok, tmc, prompt_tiles):
    i = pl.program_id(0)
    base = i * tmc

    def issue(j, carry):
        t = base + j
        pltpu.make_async_copy(ys_ref.at[pos_ref[t]], buf_ref.at[0, j], sem).start()
        pltpu.make_async_copy(ys_ref.at[pos_ref[n_tok + t]], buf_ref.at[1, j], sem).start()
        return carry

    lax.fori_loop(0, tmc, issue, 0)
    pltpu.make_async_copy(buf_ref.at[0], buf_ref.at[0], sem).wait()
    pltpu.make_async_copy(buf_ref.at[1], buf_ref.at[1], sem).wait()
    g1 = meta_ref[:, 2:3]
    g2 = meta_ref[:, 3:4]
    ss = jnp.zeros((tmc, 1), F32)
    for c in range(buf_ref.shape[2]):
        cs = slice(c * LANES, (c + 1) * LANES)
        x = y_ref[:, cs] + (buf_ref[0, :, c, :] * g1 + buf_ref[1, :, c, :] * g2)
        ss = ss + jnp.sum(x * x, axis=1, keepdims=True)
        x_ref[:, cs] = x
    inv = lax.rsqrt(ss / x_ref.shape[1] + NORM_EPS)

    @pl.when(i < prompt_tiles)
    def _():
        op_ref[...] = x_ref[...] * inv * g_ref[...]

    @pl.when(i >= prompt_tiles)
    def _():
        os_ref[...] = x_ref[...] * inv * g_ref[...]


def combine_final(y, meta, g_final, ys3, pos, n_prompt, tmc):
    m, d = y.shape
    slab = ys3.shape[1:]
    prompt_tiles = n_prompt // tmc
    return pl.pallas_call(
        functools.partial(_combine_kernel, n_tok=m, tmc=tmc, prompt_tiles=prompt_tiles),
        grid_spec=pltpu.PrefetchScalarGridSpec(
            num_scalar_prefetch=1,
            grid=(m // tmc,),
            in_specs=[pl.BlockSpec((tmc, d), lambda i, pos: (i, 0)),
                      pl.BlockSpec((tmc, LANES), lambda i, pos: (i, 0)),
                      pl.BlockSpec((1, d), lambda i, pos: (0, 0)),
                      pl.BlockSpec(memory_space=pl.ANY)],
            out_specs=[pl.BlockSpec((tmc, d), lambda i, pos: (jnp.minimum(i, prompt_tiles - 1), 0)),
                       pl.BlockSpec((tmc, d), lambda i, pos: (jnp.maximum(i - prompt_tiles, 0), 0))],
            scratch_shapes=[pltpu.VMEM((2, tmc) + slab, F32), pltpu.VMEM((tmc, d), F32),
                            pltpu.SemaphoreType.DMA(())]),
        out_shape=[jax.ShapeDtypeStruct((n_prompt, d), F32), jax.ShapeDtypeStruct((m - n_prompt, d), F32)],
        compiler_params=_params("arbitrary"),
    )(pos, y, meta, g_final.reshape(1, d), ys3)


def kernel(x_prompt, x_sample, cache_a_k, cache_a_v, state_pool, cache_c_k, cache_c_v, cache_c_idx,
           norm_mix, norm_ffn, norm_final, w_in_even, w_out_even, a_rel_bias, pool_w, pool_scale,
           ffn_w_gate, ffn_w_up, ffn_w_down, w_in_odd, w_out_odd,
           moe_router, moe_w_gate, moe_w_up, moe_w_down):
    nbp, lp, d = x_prompt.shape
    nb, ds, _ = x_sample.shape
    past = cache_c_k.shape[2]
    a_len = cache_a_k.shape[2]
    assert nbp == 1 and lp % 512 == 0 and (nb * ds) % 512 == 0 and ds == POOL_HALO and past >= POOL_HALO
    ns = nb * ds
    m = lp + ns
    tm = 512
    bf = lambda t: t.astype(BF16)

    x = jnp.concatenate([x_prompt.reshape(lp, d), x_sample.reshape(ns, d)], axis=0)

    proj0 = norm_proj(x, norm_mix[0], bf(w_in_even[0]), tm, 1024)
    k0 = proj0[:, A_WIDTH:2 * A_WIDTH]
    v0 = proj0[:, 2 * A_WIDTH:3 * A_WIDTH]
    u0 = proj0[:, 3 * A_WIDTH:]

    pad = A_PREV_CHUNKS * CHUNK
    bias_p = _rel_bias_tile(a_rel_bias[0], 0, CHUNK, -pad, A_BAND)
    a_p = band_prompt(proj0, _pair_rows(bias_p), lp)

    k_pos = past - a_len + jnp.arange(a_len + ds)
    q_pos = past + jnp.arange(ds)
    qch, kch = q_pos // CHUNK, k_pos // CHUNK
    ok = ((k_pos[None, :] >= 0) & (kch[None, :] <= qch[:, None])
          & (kch[None, :] >= qch[:, None] - A_PREV_CHUNKS))
    bias_s = jnp.where(ok[None], _rel_bias_tile(a_rel_bias[0], past, ds, past - a_len, a_len + ds), NEG)
    a_s = band_sample(proj0, cache_a_k[0].reshape(nb, a_len, A_WIDTH),
                      cache_a_v[0].reshape(nb, a_len, A_WIDTH), _pair_rows(bias_s), lp, nb, ds)

    u_s = u0[lp:].reshape(nb, ds, B_WIDTH)
    u_hist = jnp.concatenate([state_pool[0], u_s], axis=1)
    u_ext = jnp.concatenate([jnp.zeros((nb, POOL_HALO - B_HIST, B_WIDTH), F32), u_hist], axis=1)
    pw = bf(pool_w[0])
    ps = pool_scale[0].reshape(1, B_WIDTH)
    p_p = pool_prompt(proj0, pw, ps, lp, tm)
    p_s = pool_sample(u_ext, pw, ps, past)

    a = jnp.concatenate([a_p, a_s], axis=0)
    p = jnp.concatenate([p_p, p_s], axis=0)
    wo = bf(w_out_even[0])
    y = mm_res([a, p], [wo[:A_WIDTH], wo[A_WIDTH:]], x, tm, 1024)
    y = ffn(y, norm_ffn[0], bf(ffn_w_gate[0]), bf(ffn_w_up[0]), bf(ffn_w_down[0]), tm, 512)

    n_q = C_HEADS * C_HEAD_DIM
    n_kv = C_KV_HEADS * C_HEAD_DIM
    n_qi = IDX_HEADS * IDX_DIM
    n_main = n_q + 2 * n_kv + n_qi
    w1 = w_in_odd[0]
    w_tail = jnp.pad(w1[:, n_main:], ((0, 0), (0, LANES - (w1.shape[1] - n_main))))
    main = norm_proj(y, norm_mix[1], bf(w1[:, :n_main]), tm, 512)
    tail = norm_proj(y, norm_mix[1], bf(w_tail), tm, LANES)
    pos = jnp.concatenate([jnp.arange(lp), jnp.tile(past + jnp.arange(ds), nb)])
    kv1, t32, ob = rope_all(main, tail, _rope_tables(pos, C_HEAD_DIM), _rope_tables(pos, IDX_DIM), tm)
    k1 = kv1[:, :n_kv]
    v1 = kv1[:, n_kv:]
    ki1 = t32[:, :IDX_DIM]

    kb_p = 128
    vt = ob[:lp, n_q + n_kv:n_q + 2 * n_kv].reshape(lp // kb_p, kb_p, n_kv).transpose(0, 2, 1)
    wi_t = t32[:lp, IDX_DIM:IDX_DIM + IDX_HEADS].T
    o_p = dsa_t(ob, wi_t, vt, nq=lp, tq=256, kb=kb_p, qb=128, pb=256, cb=1024, topk=min(TOPK_MAX, lp // 4))
    o_s = dsa_sample(ob, t32, cache_c_k[0].reshape(nb, past, n_kv), cache_c_v[0].reshape(nb, past, n_kv), cache_c_idx[0],
                     row0=lp, kb=3 * LANES, topk=min(TOPK_MAX, (past + ds) // 4))

    o = jnp.concatenate([o_p, o_s], axis=0)
    y = mm_res([o], [bf(w_out_odd[0])], y, tm, 1024)

    wr = jnp.pad(moe_router[0], ((0, 0), (0, LANES - N_EXPERTS)))
    h, meta = router(y, norm_ffn[1], wr, tm)
    sup = 1536
    pos_rows, row_token, tile_expert, tile_rows = route_plan(meta, sup)
    n_rows = row_token.shape[0]
    slab = (d // LANES, LANES)
    xs = gather_rows(h.reshape((m,) + slab), row_token, 512)
    ys = grouped_ffn(xs.reshape(n_rows, d), tile_expert, tile_rows,
                     moe_w_gate[0], moe_w_up[0], moe_w_down[0], sup, 256, 512)
    y_p, y_s = combine_final(y, meta, norm_final, ys.reshape((n_rows,) + slab), pos_rows, lp, 256)

    y_prompt = y_p.reshape(1, lp, d)
    y_sample = y_s.reshape(nb, ds, d)
    keep = min(A_BAND, lp)
    heads = lambda t, n: t.reshape(1, n, -1, A_HEADS, A_HEAD_DIM)
    a_k_prompt = heads(k0[lp - keep:lp], 1)
    a_v_prompt = heads(v0[lp - keep:lp], 1)
    pool_prompt_out = u0[lp - B_HIST:lp].reshape(1, 1, B_HIST, B_WIDTH)
    c_k_prompt = k1[:lp].reshape(1, 1, lp, C_KV_HEADS, C_HEAD_DIM)
    c_v_prompt = v1[:lp].reshape(1, 1, lp, C_KV_HEADS, C_HEAD_DIM)
    c_idx_prompt = ki1[:lp].reshape(1, 1, lp, IDX_DIM)
    shift = lambda cache, new: jnp.concatenate(
        [cache[0], new.reshape(nb, ds, A_HEADS, A_HEAD_DIM)], axis=1)[:, ds:][None]
    a_k_sample = shift(cache_a_k, k0[lp:])
    a_v_sample = shift(cache_a_v, v0[lp:])
    pool_sample_out = u_hist[:, ds:][None]
    c_k_sample = k1[lp:].reshape(1, nb, ds, C_KV_HEADS, C_HEAD_DIM)
    c_v_sample = v1[lp:].reshape(1, nb, ds, C_KV_HEADS, C_HEAD_DIM)
    c_idx_sample = ki1[lp:].reshape(1, nb, ds, IDX_DIM)
    return (y_prompt, y_sample, a_k_prompt, a_v_prompt, pool_prompt_out,
            c_k_prompt, c_v_prompt, c_idx_prompt,
            a_k_sample, a_v_sample, pool_sample_out,
            c_k_sample, c_v_sample, c_idx_sample)
```

```python
import functools
import math

import jax
import jax.numpy as jnp
from jax import lax
from jax.experimental import pallas as pl
from jax.experimental.pallas import tpu as pltpu

F32 = jnp.float32
BF16 = jnp.bfloat16

NORM_EPS = 1e-6
NEG = -1e30
BIG = 1e30
SHIFT_SLACK = 1.01
MIN_SOFTMAX_SUM = 2.0 ** -80

CHUNK = 64
A_HEADS = 16
A_HEAD_DIM = 64
A_WIDTH = A_HEADS * A_HEAD_DIM
A_PREV_CHUNKS = 8
A_BAND = (A_PREV_CHUNKS + 1) * CHUNK
A_REL_CLIP = 128
B_WINDOWS = (2, 4, 8, 16)
B_GROUP = 256
B_WIDTH = B_GROUP * len(B_WINDOWS)
B_HIST = max(B_WINDOWS) - 1
C_HEADS = 16
C_KV_HEADS = 4
C_HEAD_DIM = 128
C_GROUPS = C_HEADS // C_KV_HEADS
IDX_HEADS = 8
IDX_DIM = 64
TOPK_MAX = 256
ROPE_THETA = 500000.0
ROPE_FRAC = 4
N_EXPERTS = 8

LANES = 128
POOL_HALO = 16
BAND_CHUNKS_PER_TRIP = 2
VMEM_LIMIT = 56 * 1024 * 1024

_NT = (((1,), (1,)), ((), ()))


def _params(*sem):
    return pltpu.CompilerParams(dimension_semantics=sem, vmem_limit_bytes=VMEM_LIMIT)


def _rms(x, g):
    ms = jnp.mean(x * x, axis=-1, keepdims=True)
    return x * lax.rsqrt(ms + NORM_EPS) * g


def _norm_proj_kernel(x_ref, g_ref, w_ref, o_ref, h_ref):
    @pl.when(pl.program_id(1) == 0)
    def _():
        h_ref[...] = _rms(x_ref[...], g_ref[...]).astype(BF16)

    o_ref[...] = jnp.dot(h_ref[...], w_ref[...], preferred_element_type=F32)


def norm_proj(x, g, w, tm, tn):
    m, d = x.shape
    n = w.shape[1]
    return pl.pallas_call(
        _norm_proj_kernel,
        grid=(m // tm, n // tn),
        in_specs=[pl.BlockSpec((tm, d), lambda i, j: (i, 0)),
                  pl.BlockSpec((1, d), lambda i, j: (0, 0)),
                  pl.BlockSpec((d, tn), lambda i, j: (0, j))],
        out_specs=pl.BlockSpec((tm, tn), lambda i, j: (i, j)),
        out_shape=jax.ShapeDtypeStruct((m, n), F32),
        scratch_shapes=[pltpu.VMEM((tm, d), BF16)],
        compiler_params=_params("parallel", "arbitrary"),
    )(x, g.reshape(1, d), w)


def _band_pairs(q_ref, nq, kw_ref, vw_ref, nk, bias_ref, windows, o_ref):
    lane = lax.broadcasted_iota(jnp.int32, (1, LANES), 1)
    first = lane < A_HEAD_DIM
    for hp in range(A_HEADS // 2):
        cs = slice(hp * LANES, (hp + 1) * LANES)
        for q_row0, k_row0, key_ok in windows:
            qp = q_ref[pl.ds(q_row0, nq), cs]
            kp = kw_ref[pl.ds(k_row0, nk), cs]
            vp = vw_ref[pl.ds(k_row0, nk), cs]
            qm = jnp.concatenate([jnp.where(first, qp, 0.0), jnp.where(first, 0.0, qp)], axis=0).astype(BF16)
            s = lax.dot_general(qm, kp, _NT, preferred_element_type=F32)
            s = s * (A_HEAD_DIM ** -0.5) + bias_ref[hp]
            if key_ok is not None:
                s = jnp.where(key_ok, s, NEG)
            mx = jnp.max(s, axis=-1, keepdims=True)
            e = jnp.exp(s - mx)
            l = jnp.sum(e, axis=-1, keepdims=True)
            o = jnp.dot(e.astype(BF16), vp, preferred_element_type=F32) / l
            o_ref[pl.ds(q_row0, nq), cs] = jnp.where(first, o[:nq], o[nq:]).astype(o_ref.dtype)


def _band_prompt_kernel(q_ref, kp_ref, kc_ref, vp_ref, vc_ref, bias_ref, o_ref, kw_ref, vw_ref, *, qb, pad):
    i = pl.program_id(0)
    kw_ref[0:pad, :] = kp_ref[...].astype(BF16)
    kw_ref[pad:pad + qb, :] = kc_ref[...].astype(BF16)
    vw_ref[0:pad, :] = vp_ref[...].astype(BF16)
    vw_ref[pad:pad + qb, :] = vc_ref[...].astype(BF16)
    col = lax.broadcasted_iota(jnp.int32, (1, A_BAND), 1)

    def chunks(c2, carry):
        windows = []
        for u in range(BAND_CHUNKS_PER_TRIP):
            cc = c2 * BAND_CHUNKS_PER_TRIP + u
            r0 = pl.multiple_of(cc * CHUNK, CHUNK)
            first_valid = jnp.where(i == 0, pad - cc * CHUNK, 0)
            windows.append((r0, r0, col >= first_valid))
        _band_pairs(q_ref, CHUNK, kw_ref, vw_ref, A_BAND, bias_ref, windows, o_ref)
        return carry

    lax.fori_loop(0, qb // CHUNK // BAND_CHUNKS_PER_TRIP, chunks, 0)


def band_prompt(proj, bias, lp):
    pad = A_PREV_CHUNKS * CHUNK
    qb = pad
    blk = (qb, A_WIDTH)
    prev = lambda c: (lambda i: (jnp.maximum(i - 1, 0), c))
    cur = lambda c: (lambda i: (i, c))
    return pl.pallas_call(
        functools.partial(_band_prompt_kernel, qb=qb, pad=pad),
        grid=(lp // qb,),
        in_specs=[pl.BlockSpec(blk, cur(0)),
                  pl.BlockSpec(blk, prev(1)), pl.BlockSpec(blk, cur(1)),
                  pl.BlockSpec(blk, prev(2)), pl.BlockSpec(blk, cur(2)),
                  pl.BlockSpec((A_HEADS // 2, 2 * CHUNK, A_BAND), lambda i: (0, 0, 0))],
        out_specs=pl.BlockSpec(blk, lambda i: (i, 0)),
        out_shape=jax.ShapeDtypeStruct((lp, A_WIDTH), BF16),
        scratch_shapes=[pltpu.VMEM((pad + qb, A_WIDTH), BF16), pltpu.VMEM((pad + qb, A_WIDTH), BF16)],
        compiler_params=_params("parallel"),
    )(proj, proj, proj, proj, proj, bias)


def _band_sample_kernel(q_ref, kn_ref, vn_ref, ck_ref, cv_ref, bias_ref, o_ref, kw_ref, vw_ref, *, a_len, ds):
    kw_ref[0:a_len, :] = ck_ref[...].astype(BF16)
    kw_ref[a_len:a_len + ds, :] = kn_ref[...].astype(BF16)
    vw_ref[0:a_len, :] = cv_ref[...].astype(BF16)
    vw_ref[a_len:a_len + ds, :] = vn_ref[...].astype(BF16)
    _band_pairs(q_ref, ds, kw_ref, vw_ref, a_len + ds, bias_ref, [(0, 0, None)], o_ref)


def band_sample(proj, cache_k, cache_v, bias, lp, nb, ds):
    a_len = cache_k.shape[1]
    row = lambda c: (lambda b: (lp // ds + b, c))
    return pl.pallas_call(
        functools.partial(_band_sample_kernel, a_len=a_len, ds=ds),
        grid=(nb,),
        in_specs=[pl.BlockSpec((ds, A_WIDTH), row(0)),
                  pl.BlockSpec((ds, A_WIDTH), row(1)),
                  pl.BlockSpec((ds, A_WIDTH), row(2)),
                  pl.BlockSpec((None, a_len, A_WIDTH), lambda b: (b, 0, 0)),
                  pl.BlockSpec((None, a_len, A_WIDTH), lambda b: (b, 0, 0)),
                  pl.BlockSpec((A_HEADS // 2, 2 * ds, a_len + ds), lambda b: (0, 0, 0))],
        out_specs=pl.BlockSpec((ds, A_WIDTH), lambda b: (b, 0)),
        out_shape=jax.ShapeDtypeStruct((nb * ds, A_WIDTH), BF16),
        scratch_shapes=[pltpu.VMEM((a_len + ds, A_WIDTH), BF16), pltpu.VMEM((a_len + ds, A_WIDTH), BF16)],
        compiler_params=_params("parallel"),
    )(proj, proj, proj, cache_k, cache_v, bias)


def _rel_bias_tile(rel_bias, q0, nq, k0, nk):
    rel_max = q0 - k0 + nq - 1
    rel = jnp.clip(rel_max - jnp.arange(nq + nk - 1), -A_REL_CLIP, A_REL_CLIP) + A_REL_CLIP
    ext = rel_bias[:, rel].astype(F32)
    return jnp.stack([ext[:, nq - 1 - i:nq - 1 - i + nk] for i in range(nq)], axis=1)


def _pair_rows(bias):
    h, nq, nk = bias.shape
    return bias.reshape(h // 2, 2 * nq, nk)


def _pool_kernel(prev_ref, cur_ref, w_ref, sc_ref, o_ref, ext_ref, *, tm, prompt, pos0):
    i = pl.program_id(0)
    prev = prev_ref[...]
    if prompt:
        prev = jnp.where(i == 0, 0.0, prev)
        pos = i * tm + lax.broadcasted_iota(jnp.int32, (tm, 1), 0)
    else:
        pos = pos0 + lax.broadcasted_iota(jnp.int32, (tm, 1), 0)
    ext_ref[0:POOL_HALO, :] = prev
    ext_ref[POOL_HALO:POOL_HALO + tm, :] = cur_ref[...]
    for g, w in enumerate(B_WINDOWS):
        cs = slice(g * B_GROUP, (g + 1) * B_GROUP)
        tok = ext_ref[POOL_HALO:POOL_HALO + tm, cs]
        tot = tok
        for j in range(1, w):
            tot = tot + ext_ref[POOL_HALO - j:POOL_HALO - j + tm, cs]
        cnt = jnp.minimum(pos + 1, w).astype(F32)
        pooled = (tot / cnt - tok).astype(BF16)
        o = jnp.dot(pooled, w_ref[g], preferred_element_type=F32) * sc_ref[:, cs]
        o_ref[:, cs] = o.astype(o_ref.dtype)


def pool_prompt(proj, pool_w, pool_scale, lp, tm):
    ucol = 3 * A_WIDTH // B_WIDTH
    per = tm // POOL_HALO
    return pl.pallas_call(
        functools.partial(_pool_kernel, tm=tm, prompt=True, pos0=0),
        grid=(lp // tm,),
        in_specs=[pl.BlockSpec((POOL_HALO, B_WIDTH), lambda i: (jnp.maximum(i * per - 1, 0), ucol)),
                  pl.BlockSpec((tm, B_WIDTH), lambda i: (i, ucol)),
                  pl.BlockSpec((len(B_WINDOWS), B_GROUP, B_GROUP), lambda i: (0, 0, 0)),
                  pl.BlockSpec((1, B_WIDTH), lambda i: (0, 0))],
        out_specs=pl.BlockSpec((tm, B_WIDTH), lambda i: (i, 0)),
        out_shape=jax.ShapeDtypeStruct((lp, B_WIDTH), BF16),
        scratch_shapes=[pltpu.VMEM((POOL_HALO + tm, B_WIDTH), F32)],
        compiler_params=_params("parallel"),
    )(proj, proj, pool_w, pool_scale)


def pool_sample(u_ext, pool_w, pool_scale, past):
    nb, tot, _ = u_ext.shape
    ds = tot - POOL_HALO
    return pl.pallas_call(
        functools.partial(_pool_kernel, tm=ds, prompt=False, pos0=past),
        grid=(nb,),
        in_specs=[pl.BlockSpec((None, POOL_HALO, B_WIDTH), lambda b: (b, 0, 0)),
                  pl.BlockSpec((None, ds, B_WIDTH), lambda b: (b, POOL_HALO // ds, 0)),
                  pl.BlockSpec((len(B_WINDOWS), B_GROUP, B_GROUP), lambda b: (0, 0, 0)),
                  pl.BlockSpec((1, B_WIDTH), lambda b: (0, 0))],
        out_specs=pl.BlockSpec((ds, B_WIDTH), lambda b: (b, 0)),
        out_shape=jax.ShapeDtypeStruct((nb * ds, B_WIDTH), BF16),
        scratch_shapes=[pltpu.VMEM((POOL_HALO + ds, B_WIDTH), F32)],
        compiler_params=_params("parallel"),
    )(u_ext, u_ext, pool_w, pool_scale)


def _mm_res_kernel(*refs, n_in):
    xs, ws = refs[:n_in], refs[n_in:2 * n_in]
    res_ref, o_ref = refs[2 * n_in], refs[2 * n_in + 1]
    acc = res_ref[...]
    for x_ref, w_ref in zip(xs, ws):
        acc = acc + jnp.dot(x_ref[...], w_ref[...], preferred_element_type=F32)
    o_ref[...] = acc


def mm_res(xs, ws, res, tm, tn):
    m, n = res.shape
    n_in = len(xs)
    in_specs = ([pl.BlockSpec((tm, x.shape[1]), lambda i, j: (i, 0)) for x in xs]
                + [pl.BlockSpec((w.shape[0], tn), lambda i, j: (0, j)) for w in ws]
                + [pl.BlockSpec((tm, tn), lambda i, j: (i, j))])
    return pl.pallas_call(
        functools.partial(_mm_res_kernel, n_in=n_in),
        grid=(m // tm, n // tn),
        in_specs=in_specs,
        out_specs=pl.BlockSpec((tm, tn), lambda i, j: (i, j)),
        out_shape=jax.ShapeDtypeStruct((m, n), F32),
        compiler_params=_params("parallel", "arbitrary"),
    )(*xs, *ws, res)


def _swiglu_tile(h, wg, wu, wd):
    a = jnp.dot(h, wg, preferred_element_type=F32)
    b = jnp.dot(h, wu, preferred_element_type=F32)
    act = (a * jax.nn.sigmoid(a) * b).astype(BF16)
    return jnp.dot(act, wd, preferred_element_type=F32)


def _ffn_kernel(y_ref, g_ref, wg_ref, wu_ref, wd_ref, o_ref, h_ref, acc_ref):
    f = pl.program_id(1)

    @pl.when(f == 0)
    def _():
        h_ref[...] = _rms(y_ref[...], g_ref[...]).astype(BF16)
        acc_ref[...] = jnp.zeros_like(acc_ref)

    acc_ref[...] += _swiglu_tile(h_ref[...], wg_ref[...], wu_ref[...], wd_ref[...])

    @pl.when(f == pl.num_programs(1) - 1)
    def _():
        o_ref[...] = y_ref[...] + acc_ref[...]


def ffn(y, g, wg, wu, wd, tm, tf):
    m, d = y.shape
    ff = wg.shape[1]
    return pl.pallas_call(
        _ffn_kernel,
        grid=(m // tm, ff // tf),
        in_specs=[pl.BlockSpec((tm, d), lambda i, f: (i, 0)),
                  pl.BlockSpec((1, d), lambda i, f: (0, 0)),
                  pl.BlockSpec((d, tf), lambda i, f: (0, f)),
                  pl.BlockSpec((d, tf), lambda i, f: (0, f)),
                  pl.BlockSpec((tf, d), lambda i, f: (f, 0))],
        out_specs=pl.BlockSpec((tm, d), lambda i, f: (i, 0)),
        out_shape=jax.ShapeDtypeStruct((m, d), F32),
        scratch_shapes=[pltpu.VMEM((tm, d), BF16), pltpu.VMEM((tm, d), F32)],
        compiler_params=_params("parallel", "arbitrary"),
    )(y, g.reshape(1, d), wg, wu, wd)


def _rope_tables(pos, head_dim):
    rot = head_dim // ROPE_FRAC
    half = rot // 2
    inv = jnp.exp(-math.log(ROPE_THETA) * jnp.arange(half, dtype=F32) * (2.0 / rot))
    ang = pos.astype(F32)[:, None] * inv[None, :]
    cos, sin = jnp.cos(ang), jnp.sin(ang)
    m = pos.shape[0]
    one = jnp.ones((m, head_dim - rot), F32)
    zero_r = jnp.zeros((m, head_dim - rot), F32)
    zero_h = jnp.zeros((m, half), F32)
    c = jnp.concatenate([cos, cos, one], axis=1)
    s_dn = jnp.concatenate([-sin, zero_h, zero_r], axis=1)
    s_up = jnp.concatenate([zero_h, sin, zero_r], axis=1)
    rep = LANES // head_dim
    return jnp.stack([jnp.tile(c, (1, rep)), jnp.tile(s_dn, (1, rep)), jnp.tile(s_up, (1, rep))])


def _rot(x, tab_ref, half):
    return (x * tab_ref[0] + pltpu.roll(x, LANES - half, 1) * tab_ref[1]
            + pltpu.roll(x, half, 1) * tab_ref[2])


def _rope_kernel(main_ref, tail_ref, tq_ref, ti_ref, kv_ref, t32_ref, ob_ref,
                 *, n_q, n_k, n_v, n_qi, wi_scale, q_scale):
    half_qk = C_HEAD_DIM // ROPE_FRAC // 2
    half_i = IDX_DIM // ROPE_FRAC // 2
    for c in range(n_q + n_k + n_v + n_qi):
        cs = slice(c * LANES, (c + 1) * LANES)
        x = main_ref[:, cs]
        if c < n_q + n_k:
            x = _rot(x, tq_ref, half_qk)
        elif c >= n_q + n_k + n_v:
            x = _rot(x, ti_ref, half_i)
        if n_q <= c < n_q + n_k + n_v:
            kv_ref[:, (c - n_q) * LANES:(c - n_q + 1) * LANES] = x
        ob_ref[:, cs] = (x * q_scale if c < n_q else x).astype(BF16)
    t = tail_ref[...]
    lane = lax.broadcasted_iota(jnp.int32, (1, LANES), 1)
    r = _rot(t, ti_ref, half_i)
    t32_ref[...] = jnp.where(lane < IDX_DIM, r, t * wi_scale)
    c = n_q + n_k + n_v + n_qi
    ob_ref[:, c * LANES:(c + 1) * LANES] = jnp.where(lane < IDX_DIM, r, pltpu.roll(r, IDX_DIM, 1)).astype(BF16)


def rope_all(main, tail, tab_qk, tab_idx, tm):
    m, nmain = main.shape
    n_q = C_HEADS * C_HEAD_DIM // LANES
    n_k = C_KV_HEADS * C_HEAD_DIM // LANES
    n_qi = IDX_HEADS * IDX_DIM // LANES
    wi_scale = (IDX_HEADS ** -0.5) * (IDX_DIM ** -0.5)
    q_scale = (C_HEAD_DIM ** -0.5) * math.log2(math.e)
    row = lambda i: (i, 0)
    return pl.pallas_call(
        functools.partial(_rope_kernel, n_q=n_q, n_k=n_k, n_v=n_k, n_qi=n_qi, wi_scale=wi_scale, q_scale=q_scale),
        grid=(m // tm,),
        in_specs=[pl.BlockSpec((tm, nmain), row),
                  pl.BlockSpec((tm, LANES), row),
                  pl.BlockSpec((3, tm, LANES), lambda i: (0, i, 0)),
                  pl.BlockSpec((3, tm, LANES), lambda i: (0, i, 0))],
        out_specs=[pl.BlockSpec((tm, 2 * n_k * LANES), row),
                   pl.BlockSpec((tm, LANES), row),
                   pl.BlockSpec((tm, nmain + LANES), row)],
        out_shape=[jax.ShapeDtypeStruct((m, 2 * n_k * LANES), F32),
                   jax.ShapeDtypeStruct((m, LANES), F32),
                   jax.ShapeDtypeStruct((m, nmain + LANES), BF16)],
        compiler_params=_params("parallel"),
    )(main, tail, tab_qk, tab_idx)


def _dsa_kernel(q_ref, qi_ref, wi_ref, kn_ref, vn_ref, kin_ref, *refs, tq, kb, past, n_valid, q_pos0, topk):
    ck_refs, cv_refs = refs[:C_KV_HEADS], refs[C_KV_HEADS:2 * C_KV_HEADS]
    ci_ref, o_ref, k_ref, v_ref, ki_ref, sc_ref, qs_ref, acc_ref, m_ref, l_ref = refs[2 * C_KV_HEADS:]
    i = pl.program_id(1)
    rows = C_GROUPS * tq
    sub = kb // LANES
    kf = float(topk)

    new = past + tq
    ci = ci_ref[...].astype(BF16)
    for g in range(C_KV_HEADS):
        gs = slice(g * C_HEAD_DIM, (g + 1) * C_HEAD_DIM)
        k_ref[0:past, gs] = ck_refs[g][...].astype(BF16)
        v_ref[0:past, gs] = cv_refs[g][...].astype(BF16)
    ki_ref[0:past, :] = jnp.concatenate([ci, ci], axis=1)
    k_ref[past:new, :] = kn_ref[...]
    v_ref[past:new, :] = vn_ref[...]
    ki_ref[past:new, :] = kin_ref[...]
    n_pad = k_ref.shape[0] - new
    k_ref[new:, :] = jnp.zeros((n_pad, k_ref.shape[1]), BF16)
    v_ref[new:, :] = jnp.zeros((n_pad, v_ref.shape[1]), BF16)
    ki_ref[new:, :] = jnp.zeros((n_pad, ki_ref.shape[1]), BF16)

    q_pos = q_pos0 + i * tq + lax.broadcasted_iota(jnp.int32, (tq, 1), 0)
    q_chunk = q_pos // CHUNK
    last_chunk = (q_pos0 + i * tq + tq - 1) // CHUNK
    kv_limit = jnp.minimum(n_valid, (last_chunk + 1) * CHUNK)
    nkb = (kv_limit + kb - 1) // kb

    lane = lax.broadcasted_iota(jnp.int32, (1, LANES), 1)
    first = lane < IDX_DIM
    wi = wi_ref[:, IDX_DIM:IDX_DIM + IDX_HEADS]

    def score_block(b, carry):
        for c in range(sub):
            off = pl.multiple_of(b * kb + c * LANES, LANES)
            kib = ki_ref[pl.ds(off, LANES), :]
            acc = jnp.zeros((tq, LANES), F32)
            for hp in range(IDX_HEADS // 2):
                qp = qi_ref[:, hp * LANES:(hp + 1) * LANES]
                for half in range(2):
                    keep = first if half == 0 else jnp.logical_not(first)
                    qm = jnp.where(keep, qp, jnp.zeros_like(qp))
                    d = lax.dot_general(qm, kib, _NT, preferred_element_type=F32)
                    h = hp * 2 + half
                    acc = acc + jnp.maximum(d, 0.0) * wi[:, h:h + 1]
            k_pos = off + lane
            adm = jnp.logical_and(k_pos // CHUNK <= q_chunk, k_pos < n_valid)
            sc_ref[b, :, c * LANES:(c + 1) * LANES] = jnp.where(adm, acc, NEG)
        return carry

    lax.fori_loop(0, nkb, score_block, 0)

    def lane_sum(x):
        return jnp.sum(x, axis=1, keepdims=True)

    def count_ge(t):
        def body(b, acc):
            for c in range(sub):
                blk = sc_ref[b, :, c * LANES:(c + 1) * LANES]
                acc = acc + jnp.where(blk >= t, 1.0, 0.0)
            return acc
        return lane_sum(lax.fori_loop(0, nkb, body, jnp.zeros((tq, LANES), F32)))

    def stats(b, carry):
        mx, mn, cnt = carry
        for c in range(sub):
            blk = sc_ref[b, :, c * LANES:(c + 1) * LANES]
            ok = blk > 0.5 * NEG
            mx = jnp.maximum(mx, blk)
            mn = jnp.minimum(mn, jnp.where(ok, blk, BIG))
            cnt = cnt + jnp.where(ok, 1.0, 0.0)
        return mx, mn, cnt

    mx, mn, cnt = lax.fori_loop(
        0, nkb, stats,
        (jnp.full((tq, LANES), NEG, F32), jnp.full((tq, LANES), BIG, F32), jnp.zeros((tq, LANES), F32)))
    row_max = jnp.max(mx, axis=1, keepdims=True)
    row_min = jnp.min(mn, axis=1, keepdims=True)
    n_adm = lane_sum(cnt)

    done0 = jnp.where(n_adm <= kf, 1.0, 0.0)
    state0 = (row_min, jnp.full((tq, 1), BIG, F32), row_max, jnp.full((tq, 1), 0.5 * NEG, F32), done0)

    def bisect(state, n_steps):
        def cond(c):
            it, st = c
            return jnp.logical_and(it < n_steps, jnp.min(st[4]) < 0.5)

        def body(c):
            it, (lo, hi, mid, thr, done) = c
            cnt = count_ge(mid)
            live = done < 0.5
            hit = jnp.logical_and(live, cnt == kf)
            ge = cnt >= kf
            thr = jnp.where(hit, mid, thr)
            done = jnp.where(hit, 1.0, done)
            lo = jnp.where(ge, mid, lo)
            hi = jnp.where(ge, hi, mid)
            return it + 1, (lo, hi, 0.5 * (lo + hi), thr, done)

        return lax.while_loop(cond, body, (jnp.int32(0), state))[1]

    def snap(state):
        lo, hi, mid, thr, done = state

        def body(b, carry):
            v_lo, v_hi = carry
            for c in range(sub):
                blk = sc_ref[b, :, c * LANES:(c + 1) * LANES]
                v_lo = jnp.minimum(v_lo, jnp.where(blk >= lo, blk, BIG))
                v_hi = jnp.maximum(v_hi, jnp.where(blk < hi, blk, NEG))
            return v_lo, v_hi

        v_lo, v_hi = lax.fori_loop(0, nkb, body,
                                   (jnp.full((tq, LANES), BIG, F32), jnp.full((tq, LANES), NEG, F32)))
        v_lo = jnp.min(v_lo, axis=1, keepdims=True)
        v_hi = jnp.max(v_hi, axis=1, keepdims=True)
        live = done < 0.5
        tie = jnp.logical_and(live, v_lo == v_hi)
        thr = jnp.where(tie, v_lo, thr)
        done = jnp.where(tie, 1.0, done)
        lo = jnp.where(live, v_lo, lo)
        return (lo, hi, 0.5 * (lo + hi), thr, done), jnp.where(tie, 1.0, 0.0)

    state = bisect(state0, 32)

    def refine_cond(c):
        rounds, st, _ = c
        return jnp.logical_and(rounds < 10, jnp.min(st[4]) < 0.5)

    def refine_body(c):
        rounds, st, tie = c
        st, new_tie = snap(st)
        st = bisect(st, 32)
        return rounds + 1, st, jnp.maximum(tie, new_tie)

    _, state, tie = lax.while_loop(refine_cond, refine_body,
                                   (jnp.int32(0), state, jnp.zeros((tq, 1), F32)))
    thr = state[3]
    any_tie = jnp.max(tie) > 0.5

    @pl.when(jnp.logical_not(any_tie))
    def _():
        def body(b, carry):
            for c in range(sub):
                cs = slice(c * LANES, (c + 1) * LANES)
                sc_ref[b, :, cs] = jnp.where(sc_ref[b, :, cs] >= thr, 0.0, NEG)
            return carry
        lax.fori_loop(0, nkb, body, 0)

    @pl.when(any_tie)
    def _():
        def gt_body(b, acc):
            for c in range(sub):
                acc = acc + jnp.where(sc_ref[b, :, c * LANES:(c + 1) * LANES] > thr, 1.0, 0.0)
            return acc
        need = kf - lane_sum(lax.fori_loop(0, nkb, gt_body, jnp.zeros((tq, LANES), F32)))

        def count_eq_upto(j):
            def body(b, acc):
                for c in range(sub):
                    blk = sc_ref[b, :, c * LANES:(c + 1) * LANES]
                    idx = (b * kb + c * LANES + lane).astype(F32)
                    acc = acc + jnp.where(jnp.logical_and(blk == thr, idx <= j), 1.0, 0.0)
                return acc
            return lane_sum(lax.fori_loop(0, nkb, body, jnp.zeros((tq, LANES), F32)))

        def idx_body(_, c):
            lo_j, hi_j = c
            mid_j = jnp.floor(0.5 * (lo_j + hi_j))
            ok = count_eq_upto(mid_j) >= need
            return jnp.where(ok, lo_j, mid_j), jnp.where(ok, mid_j, hi_j)

        n_steps = max(1, math.ceil(math.log2(sc_ref.shape[0] * kb + 1)))
        _, last = lax.fori_loop(
            0, n_steps, idx_body,
            (jnp.full((tq, 1), -1.0, F32), jnp.full((tq, 1), float(sc_ref.shape[0] * kb), F32)))
        last = jnp.where(tie > 0.5, last, BIG)

        def body(b, carry):
            for c in range(sub):
                cs = slice(c * LANES, (c + 1) * LANES)
                blk = sc_ref[b, :, cs]
                idx = (b * kb + c * LANES + lane).astype(F32)
                sel = jnp.logical_or(blk > thr, jnp.logical_and(blk == thr, idx <= last))
                sc_ref[b, :, cs] = jnp.where(sel, 0.0, NEG)
            return carry
        lax.fori_loop(0, nkb, body, 0)

    for g in range(C_KV_HEADS):
        for hh in range(C_GROUPS):
            h = g * C_GROUPS + hh
            qs_ref[g, hh * tq:(hh + 1) * tq, :] = q_ref[:, h * C_HEAD_DIM:(h + 1) * C_HEAD_DIM]
    m_ref[...] = jnp.full(m_ref.shape, NEG, F32)
    l_ref[...] = jnp.zeros(l_ref.shape, F32)
    acc_ref[...] = jnp.zeros(acc_ref.shape, F32)

    def attend(b, carry):
        off = pl.multiple_of(b * kb, kb)
        bias = sc_ref[b]
        bias = jnp.concatenate([bias] * C_GROUPS, axis=0)
        for g in range(C_KV_HEADS):
            cs = slice(g * C_HEAD_DIM, (g + 1) * C_HEAD_DIM)
            kblk = k_ref[pl.ds(off, kb), cs]
            vblk = v_ref[pl.ds(off, kb), cs]
            s = lax.dot_general(qs_ref[g], kblk, _NT, preferred_element_type=F32)
            s = s + bias
            m_old = m_ref[g]
            m_new = jnp.maximum(m_old, jnp.max(s, axis=1, keepdims=True))
            alpha = jnp.exp2(m_old - m_new)
            p = jnp.exp2(s - m_new)
            l_ref[g] = alpha * l_ref[g] + jnp.sum(p, axis=1, keepdims=True)
            acc_ref[g] = alpha * acc_ref[g] + jnp.dot(p.astype(BF16), vblk, preferred_element_type=F32)
            m_ref[g] = m_new
        return carry

    lax.fori_loop(0, nkb, attend, 0)

    for g in range(C_KV_HEADS):
        o = acc_ref[g] / l_ref[g]
        for hh in range(C_GROUPS):
            h = g * C_GROUPS + hh
            o_ref[:, h * C_HEAD_DIM:(h + 1) * C_HEAD_DIM] = o[hh * tq:(hh + 1) * tq, :].astype(o_ref.dtype)


def dsa_sample(ob, t32, cache_k, cache_v, cache_i, *, row0, kb, topk):
    nb, past = cache_k.shape[:2]
    head = lambda g: pl.BlockSpec((None, past, C_HEAD_DIM), lambda b, i: (b, 0, g))
    heads = [head(g) for g in range(C_KV_HEADS)]
    n_kv = C_KV_HEADS * C_HEAD_DIM
    ds = (ob.shape[0] - row0) // nb
    n_q = C_HEADS * C_HEAD_DIM
    n_qi = IDX_HEADS * IDX_DIM
    s_all = past + ds
    s_pad = -(-s_all // kb) * kb
    rows = C_GROUPS * ds
    col = lambda w, off: (lambda b, i: (row0 // ds + b, off // w))
    cmap = lambda b, i: (b, 0, 0)
    return pl.pallas_call(
        functools.partial(_dsa_kernel, tq=ds, kb=kb, past=past, n_valid=s_all, q_pos0=past, topk=topk),
        grid=(nb, 1),
        in_specs=[pl.BlockSpec((ds, n_q), col(n_q, 0)),
                  pl.BlockSpec((ds, n_qi), col(n_qi, n_q + 2 * n_kv)),
                  pl.BlockSpec((ds, LANES), col(LANES, 0)),
                  pl.BlockSpec((ds, n_kv), col(n_kv, n_q)),
                  pl.BlockSpec((ds, n_kv), col(n_kv, n_q + n_kv)),
                  pl.BlockSpec((ds, LANES), col(LANES, n_q + 2 * n_kv + n_qi))]
                 + heads + heads + [pl.BlockSpec((None, past, IDX_DIM), cmap)],
        out_specs=pl.BlockSpec((ds, n_q), lambda b, i: (b, 0)),
        out_shape=jax.ShapeDtypeStruct((nb * ds, n_q), BF16),
        scratch_shapes=[pltpu.VMEM((s_pad, n_kv), BF16),
                        pltpu.VMEM((s_pad, n_kv), BF16),
                        pltpu.VMEM((s_pad, LANES), BF16),
                        pltpu.VMEM((s_pad // kb, ds, kb), F32),
                        pltpu.VMEM((C_KV_HEADS, rows, C_HEAD_DIM), BF16),
                        pltpu.VMEM((C_KV_HEADS, rows, C_HEAD_DIM), F32),
                        pltpu.VMEM((C_KV_HEADS, rows, 1), F32),
                        pltpu.VMEM((C_KV_HEADS, rows, 1), F32)],
        compiler_params=_params("parallel", "arbitrary"),
    )(ob, ob, t32, ob, ob, ob, *([cache_k] * C_KV_HEADS), *([cache_v] * C_KV_HEADS), cache_i)


def _dsa_t_kernel(q_ref, qi_ref, wi_ref, k_ref, vt_ref, ki_ref, o_ref,
                  sc_ref, qim_ref, acc_ref, m_ref, l_ref, l8_ref, kmax_ref,
                  *, tq, kb, qb, pb, cb, n_valid, q_pos0, topk):
    i = pl.program_id(1)
    kf = float(topk)

    q_pos = q_pos0 + i * tq + lax.broadcasted_iota(jnp.int32, (1, tq), 1)
    q_chunk = q_pos // CHUNK
    last_chunk = (q_pos0 + i * tq + tq - 1) // CHUNK
    kv_limit = jnp.minimum(n_valid, (last_chunk + 1) * CHUNK)
    nkb = (kv_limit + kb - 1) // kb
    ncb = (kv_limit + cb - 1) // cb

    def col_reduce(x, op):
        groups = x.shape[0] // 8
        chains = 8 if groups % 8 == 0 else 1
        return op(op(x.reshape(chains, groups // chains, 8, tq), axis=1), axis=0)

    col_sum = lambda x: col_reduce(x, jnp.sum)
    col_max = lambda x: col_reduce(x, jnp.max)
    col_min = lambda x: col_reduce(x, jnp.min)

    lane = lax.broadcasted_iota(jnp.int32, (1, LANES), 1)
    first = lane < IDX_DIM
    for hp in range(IDX_HEADS // 2):
        qp = qi_ref[:, hp * LANES:(hp + 1) * LANES]
        qim_ref[2 * hp] = jnp.where(first, qp, jnp.zeros_like(qp))
        qim_ref[2 * hp + 1] = jnp.where(first, jnp.zeros_like(qp), qp)
    block_iota = lax.broadcasted_iota(jnp.int32, (pb, 1), 0)

    def score_block(b, carry):
        off = pl.multiple_of(b * pb, pb)
        kib = ki_ref[pl.ds(off, pb), :]
        parts = []
        for h in range(IDX_HEADS):
            d = lax.dot_general(kib, qim_ref[h], _NT, preferred_element_type=F32)
            parts.append(jnp.maximum(d, 0.0) * wi_ref[h:h + 1, :])
        while len(parts) > 1:
            parts = [a + b for a, b in zip(parts[0::2], parts[1::2])]
        acc = parts[0]
        k_pos = off + block_iota
        adm = jnp.logical_and(k_pos // CHUNK <= q_chunk, k_pos < n_valid)
        sc_ref[pl.ds(off, pb), :] = jnp.where(adm, acc, NEG)
        return carry

    lax.fori_loop(0, ncb * (cb // pb), score_block, 0)
    key_iota = lax.broadcasted_iota(jnp.int32, (cb, 1), 0)

    def blocks(body, init):
        def step(b, carry):
            off = pl.multiple_of(b * cb, cb)
            return body(off, sc_ref[pl.ds(off, cb), :], carry)
        return lax.fori_loop(0, ncb, step, init)

    def total(x):
        return jnp.sum(x, axis=0, keepdims=True)

    def count_ge(t):
        return total(blocks(lambda off, blk, acc: acc + col_sum(jnp.where(blk >= t, 1.0, 0.0)),
                            jnp.zeros((8, tq), F32)))

    def stats(off, blk, carry):
        mx, mn, cnt, pos, nonneg = carry
        ok = blk > 0.5 * NEG
        return (jnp.maximum(mx, col_max(blk)), jnp.minimum(mn, col_min(jnp.where(ok, blk, BIG))),
                cnt + col_sum(jnp.where(ok, 1.0, 0.0)),
                pos + col_sum(jnp.where(blk > 0.0, 1.0, 0.0)),
                nonneg + col_sum(jnp.where(blk >= 0.0, 1.0, 0.0)))

    zeros8 = jnp.zeros((8, tq), F32)
    mx, mn, cnt, pos, nonneg = blocks(stats, (jnp.full((8, tq), NEG, F32), jnp.full((8, tq), BIG, F32),
                                              zeros8, zeros8, zeros8))
    row_max = jnp.max(mx, axis=0, keepdims=True)
    row_min = jnp.min(mn, axis=0, keepdims=True)
    n_adm = total(cnt)
    n_pos = total(pos)
    n_nonneg = total(nonneg)

    open_row = n_adm > kf
    above = n_pos >= kf
    below = n_nonneg < kf
    zero_tie = jnp.logical_and(open_row, jnp.logical_not(jnp.logical_or(above, below)))
    lo0 = jnp.where(above, 0.0, row_min)
    hi0 = jnp.where(below, 0.0, BIG)
    mid0 = jnp.where(below, 0.5 * (lo0 + hi0), row_max)
    thr0 = jnp.where(zero_tie, 0.0, 0.5 * NEG)
    done0 = jnp.where(jnp.logical_or(jnp.logical_not(open_row), zero_tie), 1.0, 0.0)
    tie0 = jnp.where(zero_tie, 1.0, 0.0)
    state0 = (lo0, hi0, mid0, thr0, done0)

    def bisect(state, n_steps):
        def cond(c):
            it, st = c
            return jnp.logical_and(it < n_steps, jnp.min(st[4]) < 0.5)

        def body(c):
            it, (lo, hi, mid, thr, done) = c
            cnt = count_ge(mid)
            hit = jnp.logical_and(done < 0.5, cnt == kf)
            ge = cnt >= kf
            thr = jnp.where(hit, mid, thr)
            done = jnp.where(hit, 1.0, done)
            lo = jnp.where(ge, mid, lo)
            hi = jnp.where(ge, hi, mid)
            return it + 1, (lo, hi, 0.5 * (lo + hi), thr, done)

        return lax.while_loop(cond, body, (jnp.int32(0), state))[1]

    def snap(state):
        lo, hi, mid, thr, done = state

        def body(off, blk, carry):
            v_lo, v_hi = carry
            return (jnp.minimum(v_lo, col_min(jnp.where(blk >= lo, blk, BIG))),
                    jnp.maximum(v_hi, col_max(jnp.where(blk < hi, blk, NEG))))

        v_lo, v_hi = blocks(body, (jnp.full((8, tq), BIG, F32), jnp.full((8, tq), NEG, F32)))
        v_lo = jnp.min(v_lo, axis=0, keepdims=True)
        v_hi = jnp.max(v_hi, axis=0, keepdims=True)
        live = done < 0.5
        tie = jnp.logical_and(live, v_lo == v_hi)
        thr = jnp.where(tie, v_lo, thr)
        done = jnp.where(tie, 1.0, done)
        lo = jnp.where(live, v_lo, lo)
        return (lo, hi, 0.5 * (lo + hi), thr, done), jnp.where(tie, 1.0, 0.0)

    state = bisect(state0, 32)

    def refine_cond(c):
        rounds, st, _ = c
        return jnp.logical_and(rounds < 10, jnp.min(st[4]) < 0.5)

    def refine_body(c):
        rounds, st, tie = c
        st, new_tie = snap(st)
        st = bisect(st, 32)
        return rounds + 1, st, jnp.maximum(tie, new_tie)

    _, state, tie = lax.while_loop(refine_cond, refine_body, (jnp.int32(0), state, tie0))
    thr = state[3]
    any_tie = jnp.max(tie) > 0.5

    @pl.when(jnp.logical_not(any_tie))
    def _():
        def body(off, blk, carry):
            sc_ref[pl.ds(off, cb), :] = jnp.where(blk >= thr, 0.0, NEG)
            return carry
        blocks(body, 0)

    @pl.when(any_tie)
    def _():
        need = kf - total(blocks(lambda off, blk, acc: acc + col_sum(jnp.where(blk > thr, 1.0, 0.0)),
                                 jnp.zeros((8, tq), F32)))

        def count_eq_upto(j):
            def body(off, blk, acc):
                idx = (off + key_iota).astype(F32)
                return acc + col_sum(jnp.where(jnp.logical_and(blk == thr, idx <= j), 1.0, 0.0))
            return total(blocks(body, jnp.zeros((8, tq), F32)))

        def idx_cond(c):
            it, (lo_j, hi_j, settled) = c
            return jnp.logical_and(it < n_steps, jnp.min(settled) < 0.5)

        def idx_body(c):
            it, (lo_j, hi_j, settled) = c
            mid_j = jnp.floor(0.5 * (lo_j + hi_j))
            cnt = count_eq_upto(mid_j)
            ok = cnt >= need
            live = settled < 0.5
            lo_j = jnp.where(jnp.logical_and(live, jnp.logical_not(ok)), mid_j, lo_j)
            hi_j = jnp.where(jnp.logical_and(live, ok), mid_j, hi_j)
            settled = jnp.where(jnp.logical_or(cnt == need, hi_j - lo_j <= 1.0), 1.0, settled)
            return it + 1, (lo_j, hi_j, settled)

        n_keys = sc_ref.shape[0]
        n_steps = max(1, math.ceil(math.log2(n_keys + 1))) + 1
        _, (_, last, _) = lax.while_loop(
            idx_cond, idx_body,
            (jnp.int32(0), (jnp.full((1, tq), -1.0, F32), jnp.full((1, tq), float(n_keys), F32),
                            jnp.where(tie > 0.5, 0.0, 1.0))))
        last = jnp.where(tie > 0.5, last, BIG)

        def body(off, blk, carry):
            idx = (off + key_iota).astype(F32)
            sel = jnp.logical_or(blk > thr, jnp.logical_and(blk == thr, idx <= last))
            sc_ref[pl.ds(off, cb), :] = jnp.where(sel, 0.0, NEG)
            return carry
        blocks(body, 0)

    halves = kb // qb

    def head_scores(off, h):
        gs = slice((h // C_GROUPS) * C_HEAD_DIM, (h // C_GROUPS + 1) * C_HEAD_DIM)
        out = []
        for j in range(halves):
            rows = pl.ds(off + j * qb, qb)
            s = lax.dot_general(k_ref[rows, gs], q_ref[:, h * C_HEAD_DIM:(h + 1) * C_HEAD_DIM],
                                _NT, preferred_element_type=F32)
            out.append(s + sc_ref[rows, :])
        return out, gs

    @pl.when(i == 0)
    def _():
        for g in range(C_KV_HEADS):
            gs = slice(g * C_HEAD_DIM, (g + 1) * C_HEAD_DIM)

            def body(b, mx):
                kk = k_ref[pl.ds(pl.multiple_of(b * cb, cb), cb), gs].astype(F32)
                return jnp.maximum(mx, jnp.sum(kk * kk, axis=1, keepdims=True))

            mx = lax.fori_loop(0, k_ref.shape[0] // cb, body, jnp.zeros((cb, 1), F32))
            kmax_ref[g] = jnp.full((1, tq), 1.0, F32) * jnp.max(mx)

    ones = jnp.ones((8, C_HEAD_DIM), BF16)
    for h in range(C_HEADS):
        qh = q_ref[:, h * C_HEAD_DIM:(h + 1) * C_HEAD_DIM].astype(F32)
        qn2 = lax.dot_general(ones, (qh * qh).astype(BF16), _NT, preferred_element_type=F32)[0:1, :]
        m_ref[h] = jnp.sqrt(qn2 * kmax_ref[h // C_GROUPS]) * SHIFT_SLACK
    l8_ref[...] = jnp.zeros(l8_ref.shape, F32)
    acc_ref[...] = jnp.zeros(acc_ref.shape, F32)

    def attend_fixed(b0, n):
        for h in range(C_HEADS):
            lsum, upd = None, None
            for u in range(n):
                b = b0 + u
                ss, gs = head_scores(pl.multiple_of(b * kb, kb), h)
                ps = [jnp.exp2(s - m_ref[h]) for s in ss]
                part = sum(col_sum(p) for p in ps)
                p_all = jnp.concatenate([p.astype(BF16) for p in ps], axis=0)
                pv = jnp.dot(vt_ref[b, gs, :], p_all, preferred_element_type=F32)
                lsum = part if lsum is None else lsum + part
                upd = pv if upd is None else upd + pv
            l8_ref[h] += lsum
            acc_ref[h] += upd

    def four_blocks(t, carry):
        attend_fixed(4 * t, 4)
        return carry

    lax.fori_loop(0, nkb // 4, four_blocks, 0)

    @pl.when(nkb % 4 >= 2)
    def _():
        attend_fixed((nkb // 4) * 4, 2)
    l_min = jnp.full((1, tq), BIG, F32)
    for h in range(C_HEADS):
        l_ref[h] = total(l8_ref[h])
        l_min = jnp.minimum(l_min, l_ref[h])

    @pl.when(jnp.logical_not(jnp.min(l_min) >= MIN_SOFTMAX_SUM))
    def _():
        m_ref[...] = jnp.full(m_ref.shape, NEG, F32)
        l_ref[...] = jnp.zeros(l_ref.shape, F32)
        acc_ref[...] = jnp.zeros(acc_ref.shape, F32)

        def attend(b, carry):
            off = pl.multiple_of(b * kb, kb)
            for h in range(C_HEADS):
                ss, gs = head_scores(off, h)
                m_old = m_ref[h]
                m_new = m_old
                for s in ss:
                    m_new = jnp.maximum(m_new, jnp.max(col_max(s), axis=0, keepdims=True))
                alpha = jnp.exp2(m_old - m_new)
                ps = [jnp.exp2(s - m_new) for s in ss]
                l_ref[h] = alpha * l_ref[h] + total(sum(col_sum(p) for p in ps))
                p_all = jnp.concatenate([p.astype(BF16) for p in ps], axis=0)
                acc_ref[h] = alpha * acc_ref[h] + jnp.dot(vt_ref[b, gs, :], p_all,
                                                          preferred_element_type=F32)
                m_ref[h] = m_new
            return carry

        lax.fori_loop(0, nkb, attend, 0)

    for h in range(C_HEADS):
        o = acc_ref[h] / l_ref[h]
        o_ref[:, h * C_HEAD_DIM:(h + 1) * C_HEAD_DIM] = o.T.astype(o_ref.dtype)


def dsa_t(ob, wi_t, vt, *, nq, tq, kb, qb, pb, cb, topk):
    dq = C_HEADS * C_HEAD_DIM
    n_kv = C_KV_HEADS * C_HEAD_DIM
    n_qi = IDX_HEADS * IDX_DIM
    s_pad = nq
    assert tq % (2 * kb) == 0 and tq % CHUNK == 0 and nq % cb == 0 and cb % pb == 0 and kb % qb == 0
    return pl.pallas_call(
        functools.partial(_dsa_t_kernel, tq=tq, kb=kb, qb=qb, pb=pb, cb=cb, n_valid=nq, q_pos0=0, topk=topk),
        grid=(1, nq // tq),
        in_specs=[pl.BlockSpec((tq, dq), lambda b, i: (i, 0)),
                  pl.BlockSpec((tq, n_qi), lambda b, i: (i, (dq + 2 * n_kv) // n_qi)),
                  pl.BlockSpec((IDX_HEADS, tq), lambda b, i: (0, i)),
                  pl.BlockSpec((nq, n_kv), lambda b, i: (0, dq // n_kv)),
                  pl.BlockSpec(vt.shape, lambda b, i: (0, 0, 0)),
                  pl.BlockSpec((nq, LANES), lambda b, i: (0, (dq + 2 * n_kv + n_qi) // LANES))],
        out_specs=pl.BlockSpec((tq, dq), lambda b, i: (i, 0)),
        out_shape=jax.ShapeDtypeStruct((nq, dq), BF16),
        scratch_shapes=[pltpu.VMEM((s_pad, tq), F32),
                        pltpu.VMEM((IDX_HEADS, tq, LANES), BF16),
                        pltpu.VMEM((C_HEADS, C_HEAD_DIM, tq), F32),
                        pltpu.VMEM((C_HEADS, 1, tq), F32),
                        pltpu.VMEM((C_HEADS, 1, tq), F32),
                        pltpu.VMEM((C_HEADS, 8, tq), F32),
                        pltpu.VMEM((C_KV_HEADS, 1, tq), F32)],
        compiler_params=_params("arbitrary", "arbitrary"),
    )(ob, ob, wi_t, ob, vt, ob)


def _router_kernel(y_ref, g_ref, wr_ref, h_ref, gate_ref):
    h = _rms(y_ref[...], g_ref[...])
    h_ref[...] = h.astype(BF16)
    logits = jnp.dot(h, wr_ref[...], preferred_element_type=F32, precision=lax.Precision.HIGHEST)
    lane = lax.broadcasted_iota(jnp.int32, logits.shape, 1)
    lg = jnp.where(lane < N_EXPERTS, logits, NEG)
    m1 = jnp.max(lg, axis=1, keepdims=True)
    i1 = jnp.min(jnp.where(lg == m1, lane, LANES), axis=1, keepdims=True)
    lg2 = jnp.where(lane == i1, NEG, lg)
    m2 = jnp.max(lg2, axis=1, keepdims=True)
    i2 = jnp.min(jnp.where(lg2 == m2, lane, LANES), axis=1, keepdims=True)
    e = jnp.exp(m2 - m1)
    g1 = 1.0 / (1.0 + e)
    g2 = e / (1.0 + e)
    meta = jnp.where(lane == 0, i1.astype(F32), jnp.where(lane == 1, i2.astype(F32),
                     jnp.where(lane == 2, g1, jnp.where(lane == 3, g2, 0.0))))
    gate_ref[...] = meta


def router(y, g, w_router_pad, tm):
    m, d = y.shape
    return pl.pallas_call(
        _router_kernel,
        grid=(m // tm,),
        in_specs=[pl.BlockSpec((tm, d), lambda i: (i, 0)),
                  pl.BlockSpec((1, d), lambda i: (0, 0)),
                  pl.BlockSpec((d, LANES), lambda i: (0, 0))],
        out_specs=[pl.BlockSpec((tm, d), lambda i: (i, 0)),
                   pl.BlockSpec((tm, LANES), lambda i: (i, 0))],
        out_shape=[jax.ShapeDtypeStruct((m, d), BF16), jax.ShapeDtypeStruct((m, LANES), F32)],
        compiler_params=_params("parallel"),
    )(y, g.reshape(1, d), w_router_pad)


def route_plan(meta, sup):
    m = meta.shape[0]
    n_tiles = (2 * m) // sup + N_EXPERTS
    e_all = jnp.concatenate([meta[:, 0], meta[:, 1]]).astype(jnp.int32)
    onehot = (e_all[:, None] == jnp.arange(N_EXPERTS)[None, :]).astype(jnp.int32)
    rank = jnp.sum((jnp.cumsum(onehot, axis=0) - onehot) * onehot, axis=1)
    counts = jnp.sum(onehot, axis=0)
    n_super = (counts + sup - 1) // sup
    super_end = jnp.cumsum(n_super)
    super_start = super_end - n_super
    pos = (super_start * sup)[e_all] + rank
    tiles = jnp.arange(n_tiles)
    used = super_end[-1]
    t_eff = jnp.minimum(tiles, used - 1)
    tile_expert = jnp.minimum(jnp.searchsorted(super_end, t_eff, side="right"), N_EXPERTS - 1).astype(jnp.int32)
    rows = jnp.clip(counts[tile_expert] - (t_eff - super_start[tile_expert]) * sup, 0, sup)
    tile_rows = jnp.where(tiles < used, rows, 0).astype(jnp.int32)
    token = jnp.concatenate([jnp.arange(m), jnp.arange(m)]).astype(jnp.int32)
    row_token = (jnp.arange(n_tiles * sup, dtype=jnp.int32) % m).at[pos].set(token)
    return pos.astype(jnp.int32), row_token, tile_expert, tile_rows


def _gather_rows_kernel(tok_ref, h_ref, o_ref, sem, *, batch):
    base = pl.program_id(0) * batch

    def issue(j, carry):
        pltpu.make_async_copy(h_ref.at[tok_ref[base + j]], o_ref.at[j], sem).start()
        return carry

    lax.fori_loop(0, batch, issue, 0)
    pltpu.make_async_copy(o_ref, o_ref, sem).wait()


def gather_rows(h3, row_token, batch):
    n_rows = row_token.shape[0]
    blk = (batch,) + h3.shape[1:]
    return pl.pallas_call(
        functools.partial(_gather_rows_kernel, batch=batch),
        grid_spec=pltpu.PrefetchScalarGridSpec(
            num_scalar_prefetch=1,
            grid=(n_rows // batch,),
            in_specs=[pl.BlockSpec(memory_space=pl.ANY)],
            out_specs=pl.BlockSpec(blk, lambda i, tok: (i, 0, 0)),
            scratch_shapes=[pltpu.SemaphoreType.DMA(())]),
        out_shape=jax.ShapeDtypeStruct((n_rows,) + h3.shape[1:], h3.dtype),
        compiler_params=_params("arbitrary"),
    )(row_token, h3)


def _grouped_ffn_kernel(te_ref, tr_ref, x_ref, wg_ref, wu_ref, wd_ref, o_ref,
                        wgb_ref, wub_ref, wdb_ref, *, sup, sub):
    t = pl.program_id(0)
    f = pl.program_id(1)
    rows = tr_ref[t]

    @pl.when(rows > 0)
    def _():
        wgb_ref[...] = wg_ref[...].astype(BF16)
        wub_ref[...] = wu_ref[...].astype(BF16)
        wdb_ref[...] = wd_ref[...].astype(BF16)

    for s in range(sup // sub):
        sl = slice(s * sub, (s + 1) * sub)

        @pl.when(jnp.logical_and(s * sub >= rows, f == 0))
        def _():
            o_ref[sl, :] = jnp.zeros((sub, o_ref.shape[1]), F32)

        @pl.when(s * sub < rows)
        def _():
            part = _swiglu_tile(x_ref[sl, :], wgb_ref[...], wub_ref[...], wdb_ref[...])

            @pl.when(f == 0)
            def _():
                o_ref[sl, :] = part

            @pl.when(f > 0)
            def _():
                o_ref[sl, :] += part


def grouped_ffn(xs, tile_expert, tile_rows, wg, wu, wd, sup, sub, tf):
    n_rows, d = xs.shape
    ff = wg.shape[2]
    n_f = ff // tf
    fidx = lambda t, f, te, tr: jnp.where(tr[t] > 0, f, n_f - 1)
    return pl.pallas_call(
        functools.partial(_grouped_ffn_kernel, sup=sup, sub=sub),
        grid_spec=pltpu.PrefetchScalarGridSpec(
            num_scalar_prefetch=2,
            grid=(n_rows // sup, n_f),
            in_specs=[pl.BlockSpec((sup, d), lambda t, f, te, tr: (t, 0), pipeline_mode=pl.Buffered(1)),
                      pl.BlockSpec((None, d, tf), lambda t, f, te, tr: (te[t], 0, fidx(t, f, te, tr))),
                      pl.BlockSpec((None, d, tf), lambda t, f, te, tr: (te[t], 0, fidx(t, f, te, tr))),
                      pl.BlockSpec((None, tf, d), lambda t, f, te, tr: (te[t], fidx(t, f, te, tr), 0))],
            out_specs=pl.BlockSpec((sup, d), lambda t, f, te, tr: (t, 0), pipeline_mode=pl.Buffered(1)),
            scratch_shapes=[pltpu.VMEM((d, tf), BF16), pltpu.VMEM((d, tf), BF16), pltpu.VMEM((tf, d), BF16)]),
        out_shape=jax.ShapeDtypeStruct((n_rows, d), F32),
        compiler_params=_params("arbitrary", "arbitrary"),
    )(tile_expert, tile_rows, xs, wg, wu, wd)


def _combine_kernel(pos_ref, y_ref, meta_ref, g_ref, ys_ref, op_ref, os_ref, buf_ref, x_ref, sem,
                    *, n_tok, tmc, prompt_tiles):
    i = pl.program_id(0)
    slot = i % 2

    def issue(tile, into):
        def body(j, carry):
            t = tile * tmc + j
            pltpu.make_async_copy(ys_ref.at[pos_ref[t]], buf_ref.at[into, 0, j], sem.at[into]).start()
            pltpu.make_async_copy(ys_ref.at[pos_ref[n_tok + t]], buf_ref.at[into, 1, j], sem.at[into]).start()
            return carry
        lax.fori_loop(0, tmc, body, 0)

    @pl.when(i == 0)
    def _():
        issue(0, 0)

    @pl.when(i + 1 < pl.num_programs(0))
    def _():
        issue(i + 1, 1 - slot)

    for e in range(2):
        pltpu.make_async_copy(buf_ref.at[slot, e], buf_ref.at[slot, e], sem.at[slot]).wait()
    g1 = meta_ref[:, 2:3]
    g2 = meta_ref[:, 3:4]
    ss = jnp.zeros((tmc, 1), F32)
    for c in range(buf_ref.shape[3]):
        cs = slice(c * LANES, (c + 1) * LANES)
        x = y_ref[:, cs] + (buf_ref[slot, 0, :, c, :] * g1 + buf_ref[slot, 1, :, c, :] * g2)
        ss = ss + jnp.sum(x * x, axis=1, keepdims=True)
        x_ref[:, cs] = x
    inv = lax.rsqrt(ss / x_ref.shape[1] + NORM_EPS)

    @pl.when(i < prompt_tiles)
    def _():
        op_ref[...] = x_ref[...] * inv * g_ref[...]

    @pl.when(i >= prompt_tiles)
    def _():
        os_ref[...] = x_ref[...] * inv * g_ref[...]


def combine_final(y, meta, g_final, ys3, pos, n_prompt, tmc):
    m, d = y.shape
    slab = ys3.shape[1:]
    prompt_tiles = n_prompt // tmc
    return pl.pallas_call(
        functools.partial(_combine_kernel, n_tok=m, tmc=tmc, prompt_tiles=prompt_tiles),
        grid_spec=pltpu.PrefetchScalarGridSpec(
            num_scalar_prefetch=1,
            grid=(m // tmc,),
            in_specs=[pl.BlockSpec((tmc, d), lambda i, pos: (i, 0)),
                      pl.BlockSpec((tmc, LANES), lambda i, pos: (i, 0)),
                      pl.BlockSpec((1, d), lambda i, pos: (0, 0)),
                      pl.BlockSpec(memory_space=pl.ANY)],
            out_specs=[pl.BlockSpec((tmc, d), lambda i, pos: (jnp.minimum(i, prompt_tiles - 1), 0)),
                       pl.BlockSpec((tmc, d), lambda i, pos: (jnp.maximum(i - prompt_tiles, 0), 0))],
            scratch_shapes=[pltpu.VMEM((2, 2, tmc) + slab, F32), pltpu.VMEM((tmc, d), F32),
                            pltpu.SemaphoreType.DMA((2,))]),
        out_shape=[jax.ShapeDtypeStruct((n_prompt, d), F32), jax.ShapeDtypeStruct((m - n_prompt, d), F32)],
        compiler_params=_params("arbitrary"),
    )(pos, y, meta, g_final.reshape(1, d), ys3)


def kernel(x_prompt, x_sample, cache_a_k, cache_a_v, state_pool, cache_c_k, cache_c_v, cache_c_idx,
           norm_mix, norm_ffn, norm_final, w_in_even, w_out_even, a_rel_bias, pool_w, pool_scale,
           ffn_w_gate, ffn_w_up, ffn_w_down, w_in_odd, w_out_odd,
           moe_router, moe_w_gate, moe_w_up, moe_w_down):
    nbp, lp, d = x_prompt.shape
    nb, ds, _ = x_sample.shape
    past = cache_c_k.shape[2]
    a_len = cache_a_k.shape[2]
    assert nbp == 1 and lp % 512 == 0 and (nb * ds) % 512 == 0 and ds == POOL_HALO and past >= POOL_HALO
    ns = nb * ds
    m = lp + ns
    tm = 512
    bf = lambda t: t.astype(BF16)

    x = jnp.concatenate([x_prompt.reshape(lp, d), x_sample.reshape(ns, d)], axis=0)

    proj0 = norm_proj(x, norm_mix[0], bf(w_in_even[0]), tm, 1024)
    k0 = proj0[:, A_WIDTH:2 * A_WIDTH]
    v0 = proj0[:, 2 * A_WIDTH:3 * A_WIDTH]
    u0 = proj0[:, 3 * A_WIDTH:]

    pad = A_PREV_CHUNKS * CHUNK
    bias_p = _rel_bias_tile(a_rel_bias[0], 0, CHUNK, -pad, A_BAND)
    a_p = band_prompt(proj0, _pair_rows(bias_p), lp)

    k_pos = past - a_len + jnp.arange(a_len + ds)
    q_pos = past + jnp.arange(ds)
    qch, kch = q_pos // CHUNK, k_pos // CHUNK
    ok = ((k_pos[None, :] >= 0) & (kch[None, :] <= qch[:, None])
          & (kch[None, :] >= qch[:, None] - A_PREV_CHUNKS))
    bias_s = jnp.where(ok[None], _rel_bias_tile(a_rel_bias[0], past, ds, past - a_len, a_len + ds), NEG)
    a_s = band_sample(proj0, cache_a_k[0].reshape(nb, a_len, A_WIDTH),
                      cache_a_v[0].reshape(nb, a_len, A_WIDTH), _pair_rows(bias_s), lp, nb, ds)

    u_s = u0[lp:].reshape(nb, ds, B_WIDTH)
    u_hist = jnp.concatenate([state_pool[0], u_s], axis=1)
    u_ext = jnp.concatenate([jnp.zeros((nb, POOL_HALO - B_HIST, B_WIDTH), F32), u_hist], axis=1)
    pw = bf(pool_w[0])
    ps = pool_scale[0].reshape(1, B_WIDTH)
    p_p = pool_prompt(proj0, pw, ps, lp, tm)
    p_s = pool_sample(u_ext, pw, ps, past)

    a = jnp.concatenate([a_p, a_s], axis=0)
    p = jnp.concatenate([p_p, p_s], axis=0)
    wo = bf(w_out_even[0])
    y = mm_res([a, p], [wo[:A_WIDTH], wo[A_WIDTH:]], x, tm, 1024)
    y = ffn(y, norm_ffn[0], bf(ffn_w_gate[0]), bf(ffn_w_up[0]), bf(ffn_w_down[0]), tm, 512)

    n_q = C_HEADS * C_HEAD_DIM
    n_kv = C_KV_HEADS * C_HEAD_DIM
    n_qi = IDX_HEADS * IDX_DIM
    n_main = n_q + 2 * n_kv + n_qi
    w1 = w_in_odd[0]
    w_tail = jnp.pad(w1[:, n_main:], ((0, 0), (0, LANES - (w1.shape[1] - n_main))))
    main = norm_proj(y, norm_mix[1], bf(w1[:, :n_main]), tm, 512)
    tail = norm_proj(y, norm_mix[1], bf(w_tail), tm, LANES)
    pos = jnp.concatenate([jnp.arange(lp), jnp.tile(past + jnp.arange(ds), nb)])
    kv1, t32, ob = rope_all(main, tail, _rope_tables(pos, C_HEAD_DIM), _rope_tables(pos, IDX_DIM), tm)
    k1 = kv1[:, :n_kv]
    v1 = kv1[:, n_kv:]
    ki1 = t32[:, :IDX_DIM]

    kb_p = 128
    vt = ob[:lp, n_q + n_kv:n_q + 2 * n_kv].reshape(lp // kb_p, kb_p, n_kv).transpose(0, 2, 1)
    wi_t = t32[:lp, IDX_DIM:IDX_DIM + IDX_HEADS].T
    o_p = dsa_t(ob, wi_t, vt, nq=lp, tq=256, kb=kb_p, qb=128, pb=256, cb=1024, topk=min(TOPK_MAX, lp // 4))
    o_s = dsa_sample(ob, t32, cache_c_k[0].reshape(nb, past, n_kv), cache_c_v[0].reshape(nb, past, n_kv), cache_c_idx[0],
                     row0=lp, kb=3 * LANES, topk=min(TOPK_MAX, (past + ds) // 4))

    o = jnp.concatenate([o_p, o_s], axis=0)
    y = mm_res([o], [bf(w_out_odd[0])], y, tm, 1024)

    wr = jnp.pad(moe_router[0], ((0, 0), (0, LANES - N_EXPERTS)))
    h, meta = router(y, norm_ffn[1], wr, tm)
    sup = 1536
    pos_rows, row_token, tile_expert, tile_rows = route_plan(meta, sup)
    n_rows = row_token.shape[0]
    slab = (d // LANES, LANES)
    xs = gather_rows(h.reshape((m,) + slab), row_token, sup)
    ys = grouped_ffn(xs.reshape(n_rows, d), tile_expert, tile_rows,
                     moe_w_gate[0], moe_w_up[0], moe_w_down[0], sup, 256, 512)
    y_p, y_s = combine_final(y, meta, norm_final, ys.reshape((n_rows,) + slab), pos_rows, lp, 256)

    y_prompt = y_p.reshape(1, lp, d)
    y_sample = y_s.reshape(nb, ds, d)
    keep = min(A_BAND, lp)
    heads = lambda t, n: t.reshape(1, n, -1, A_HEADS, A_HEAD_DIM)
    a_k_prompt = heads(k0[lp - keep:lp], 1)
    a_v_prompt = heads(v0[lp - keep:lp], 1)
    pool_prompt_out = u0[lp - B_HIST:lp].reshape(1, 1, B_HIST, B_WIDTH)
    c_k_prompt = k1[:lp].reshape(1, 1, lp, C_KV_HEADS, C_HEAD_DIM)
    c_v_prompt = v1[:lp].reshape(1, 1, lp, C_KV_HEADS, C_HEAD_DIM)
    c_idx_prompt = ki1[:lp].reshape(1, 1, lp, IDX_DIM)
    shift = lambda cache, new: jnp.concatenate(
        [cache[0], new.reshape(nb, ds, A_HEADS, A_HEAD_DIM)], axis=1)[:, ds:][None]
    a_k_sample = shift(cache_a_k, k0[lp:])
    a_v_sample = shift(cache_a_v, v0[lp:])
    pool_sample_out = u_hist[:, ds:][None]
    c_k_sample = k1[lp:].reshape(1, nb, ds, C_KV_HEADS, C_HEAD_DIM)
    c_v_sample = v1[lp:].reshape(1, nb, ds, C_KV_HEADS, C_HEAD_DIM)
    c_idx_sample = ki1[lp:].reshape(1, nb, ds, IDX_DIM)
    return (y_prompt, y_sample, a_k_prompt, a_v_prompt, pool_prompt_out,
            c_k_prompt, c_v_prompt, c_idx_prompt,
            a_k_sample, a_v_sample, pool_sample_out,
            c_k_sample, c_v_sample, c_idx_sample)
```

```python
import functools
import math

import jax
import jax.numpy as jnp
from jax import lax
from jax.experimental import pallas as pl
from jax.experimental.pallas import tpu as pltpu

F32 = jnp.float32
BF16 = jnp.bfloat16

NORM_EPS = 1e-6
NEG = -1e30
BIG = 1e30
SHIFT_SLACK = 1.01
MIN_SOFTMAX_SUM = 2.0 ** -80

CHUNK = 64
A_HEADS = 16
A_HEAD_DIM = 64
A_WIDTH = A_HEADS * A_HEAD_DIM
A_PREV_CHUNKS = 8
A_BAND = (A_PREV_CHUNKS + 1) * CHUNK
A_REL_CLIP = 128
B_WINDOWS = (2, 4, 8, 16)
B_GROUP = 256
B_WIDTH = B_GROUP * len(B_WINDOWS)
B_HIST = max(B_WINDOWS) - 1
C_HEADS = 16
C_KV_HEADS = 4
C_HEAD_DIM = 128
C_GROUPS = C_HEADS // C_KV_HEADS
IDX_HEADS = 8
IDX_DIM = 64
TOPK_MAX = 256
ROPE_THETA = 500000.0
ROPE_FRAC = 4
N_EXPERTS = 8

LANES = 128
POOL_HALO = 16
BAND_CHUNKS_PER_TRIP = 2
VMEM_LIMIT = 56 * 1024 * 1024

_NT = (((1,), (1,)), ((), ()))


def _params(*sem):
    return pltpu.CompilerParams(dimension_semantics=sem, vmem_limit_bytes=VMEM_LIMIT)


def _rms(x, g):
    ms = jnp.mean(x * x, axis=-1, keepdims=True)
    return x * lax.rsqrt(ms + NORM_EPS) * g


def _norm_proj_kernel(x_ref, g_ref, w_ref, o_ref, h_ref):
    @pl.when(pl.program_id(1) == 0)
    def _():
        h_ref[...] = _rms(x_ref[...], g_ref[...]).astype(BF16)

    o_ref[...] = jnp.dot(h_ref[...], w_ref[...], preferred_element_type=F32)


def norm_proj(x, g, w, tm, tn):
    m, d = x.shape
    n = w.shape[1]
    return pl.pallas_call(
        _norm_proj_kernel,
        grid=(m // tm, n // tn),
        in_specs=[pl.BlockSpec((tm, d), lambda i, j: (i, 0)),
                  pl.BlockSpec((1, d), lambda i, j: (0, 0)),
                  pl.BlockSpec((d, tn), lambda i, j: (0, j))],
        out_specs=pl.BlockSpec((tm, tn), lambda i, j: (i, j)),
        out_shape=jax.ShapeDtypeStruct((m, n), F32),
        scratch_shapes=[pltpu.VMEM((tm, d), BF16)],
        compiler_params=_params("parallel", "arbitrary"),
    )(x, g.reshape(1, d), w)


def _stack_norm_proj_kernel(xp_ref, xs_ref, g_ref, w_ref, o_ref, x_ref, h_ref, *, prompt_tiles):
    i = pl.program_id(0)

    def first_column(src_ref):
        x = src_ref[...]
        x_ref[...] = x
        h_ref[...] = _rms(x, g_ref[...]).astype(BF16)

    @pl.when(jnp.logical_and(pl.program_id(1) == 0, i < prompt_tiles))
    def _():
        first_column(xp_ref)

    @pl.when(jnp.logical_and(pl.program_id(1) == 0, i >= prompt_tiles))
    def _():
        first_column(xs_ref)

    o_ref[...] = jnp.dot(h_ref[...], w_ref[...], preferred_element_type=F32)


def stack_norm_proj(xp, xs, g, w, tm, tn):
    lp, d = xp.shape
    m = lp + xs.shape[0]
    n = w.shape[1]
    prompt_tiles = lp // tm
    return pl.pallas_call(
        functools.partial(_stack_norm_proj_kernel, prompt_tiles=prompt_tiles),
        grid=(m // tm, n // tn),
        in_specs=[pl.BlockSpec((tm, d), lambda i, j: (jnp.minimum(i, prompt_tiles - 1), 0)),
                  pl.BlockSpec((tm, d), lambda i, j: (jnp.maximum(i - prompt_tiles, 0), 0)),
                  pl.BlockSpec((1, d), lambda i, j: (0, 0)),
                  pl.BlockSpec((d, tn), lambda i, j: (0, j))],
        out_specs=[pl.BlockSpec((tm, tn), lambda i, j: (i, j)),
                   pl.BlockSpec((tm, d), lambda i, j: (i, 0))],
        out_shape=[jax.ShapeDtypeStruct((m, n), F32), jax.ShapeDtypeStruct((m, d), F32)],
        scratch_shapes=[pltpu.VMEM((tm, d), BF16)],
        compiler_params=_params("parallel", "arbitrary"),
    )(xp, xs, g.reshape(1, d), w)


def _band_pairs(q_ref, nq, kw_ref, vw_ref, nk, bias_ref, windows, o_ref):
    lane = lax.broadcasted_iota(jnp.int32, (1, LANES), 1)
    first = lane < A_HEAD_DIM
    for hp in range(A_HEADS // 2):
        cs = slice(hp * LANES, (hp + 1) * LANES)
        for q_row0, k_row0, key_ok in windows:
            qp = q_ref[pl.ds(q_row0, nq), cs]
            kp = kw_ref[pl.ds(k_row0, nk), cs]
            vp = vw_ref[pl.ds(k_row0, nk), cs]
            qm = jnp.concatenate([jnp.where(first, qp, 0.0), jnp.where(first, 0.0, qp)], axis=0).astype(BF16)
            s = lax.dot_general(qm, kp, _NT, preferred_element_type=F32)
            s = s * (A_HEAD_DIM ** -0.5) + bias_ref[hp]
            if key_ok is not None:
                s = jnp.where(key_ok, s, NEG)
            mx = jnp.max(s, axis=-1, keepdims=True)
            e = jnp.exp(s - mx)
            l = jnp.sum(e, axis=-1, keepdims=True)
            o = jnp.dot(e.astype(BF16), vp, preferred_element_type=F32) / l
            o_ref[pl.ds(q_row0, nq), cs] = jnp.where(first, o[:nq], o[nq:]).astype(o_ref.dtype)


def _band_prompt_kernel(q_ref, kp_ref, kc_ref, vp_ref, vc_ref, bias_ref, o_ref, kw_ref, vw_ref, *, qb, pad):
    i = pl.program_id(0)
    kw_ref[0:pad, :] = kp_ref[...].astype(BF16)
    kw_ref[pad:pad + qb, :] = kc_ref[...].astype(BF16)
    vw_ref[0:pad, :] = vp_ref[...].astype(BF16)
    vw_ref[pad:pad + qb, :] = vc_ref[...].astype(BF16)
    col = lax.broadcasted_iota(jnp.int32, (1, A_BAND), 1)

    def chunks(c2, carry):
        windows = []
        for u in range(BAND_CHUNKS_PER_TRIP):
            cc = c2 * BAND_CHUNKS_PER_TRIP + u
            r0 = pl.multiple_of(cc * CHUNK, CHUNK)
            first_valid = jnp.where(i == 0, pad - cc * CHUNK, 0)
            windows.append((r0, r0, col >= first_valid))
        _band_pairs(q_ref, CHUNK, kw_ref, vw_ref, A_BAND, bias_ref, windows, o_ref)
        return carry

    lax.fori_loop(0, qb // CHUNK // BAND_CHUNKS_PER_TRIP, chunks, 0)


def band_prompt(proj, bias, lp):
    pad = A_PREV_CHUNKS * CHUNK
    qb = pad
    blk = (qb, A_WIDTH)
    prev = lambda c: (lambda i: (jnp.maximum(i - 1, 0), c))
    cur = lambda c: (lambda i: (i, c))
    return pl.pallas_call(
        functools.partial(_band_prompt_kernel, qb=qb, pad=pad),
        grid=(lp // qb,),
        in_specs=[pl.BlockSpec(blk, cur(0)),
                  pl.BlockSpec(blk, prev(1)), pl.BlockSpec(blk, cur(1)),
                  pl.BlockSpec(blk, prev(2)), pl.BlockSpec(blk, cur(2)),
                  pl.BlockSpec((A_HEADS // 2, 2 * CHUNK, A_BAND), lambda i: (0, 0, 0))],
        out_specs=pl.BlockSpec(blk, lambda i: (i, 0)),
        out_shape=jax.ShapeDtypeStruct((proj.shape[0], A_WIDTH), BF16),
        scratch_shapes=[pltpu.VMEM((pad + qb, A_WIDTH), BF16), pltpu.VMEM((pad + qb, A_WIDTH), BF16)],
        compiler_params=_params("parallel"),
    )(proj, proj, proj, proj, proj, bias)


def _band_sample_kernel(q_ref, kn_ref, vn_ref, ck_ref, cv_ref, bias_ref, dst_ref, o_ref, kw_ref, vw_ref,
                        *, a_len, ds):
    del dst_ref
    kw_ref[0:a_len, :] = ck_ref[...].astype(BF16)
    kw_ref[a_len:a_len + ds, :] = kn_ref[...].astype(BF16)
    vw_ref[0:a_len, :] = cv_ref[...].astype(BF16)
    vw_ref[a_len:a_len + ds, :] = vn_ref[...].astype(BF16)
    _band_pairs(q_ref, ds, kw_ref, vw_ref, a_len + ds, bias_ref, [(0, 0, None)], o_ref)


def band_sample(proj, cache_k, cache_v, bias, dst, lp, nb, ds):
    a_len = cache_k.shape[1]
    row = lambda c: (lambda b: (lp // ds + b, c))
    return pl.pallas_call(
        functools.partial(_band_sample_kernel, a_len=a_len, ds=ds),
        grid=(nb,),
        in_specs=[pl.BlockSpec((ds, A_WIDTH), row(0)),
                  pl.BlockSpec((ds, A_WIDTH), row(1)),
                  pl.BlockSpec((ds, A_WIDTH), row(2)),
                  pl.BlockSpec((None, a_len, A_WIDTH), lambda b: (b, 0, 0)),
                  pl.BlockSpec((None, a_len, A_WIDTH), lambda b: (b, 0, 0)),
                  pl.BlockSpec((A_HEADS // 2, 2 * ds, a_len + ds), lambda b: (0, 0, 0)),
                  pl.BlockSpec(memory_space=pl.ANY)],
        out_specs=pl.BlockSpec((ds, A_WIDTH), row(0)),
        out_shape=jax.ShapeDtypeStruct(dst.shape, dst.dtype),
        input_output_aliases={6: 0},
        scratch_shapes=[pltpu.VMEM((a_len + ds, A_WIDTH), BF16), pltpu.VMEM((a_len + ds, A_WIDTH), BF16)],
        compiler_params=_params("parallel"),
    )(proj, proj, proj, cache_k, cache_v, bias, dst)


def _rel_bias_tile(rel_bias, q0, nq, k0, nk):
    rel_max = q0 - k0 + nq - 1
    rel = jnp.clip(rel_max - jnp.arange(nq + nk - 1), -A_REL_CLIP, A_REL_CLIP) + A_REL_CLIP
    ext = rel_bias[:, rel].astype(F32)
    return jnp.stack([ext[:, nq - 1 - i:nq - 1 - i + nk] for i in range(nq)], axis=1)


def _pair_rows(bias):
    h, nq, nk = bias.shape
    return bias.reshape(h // 2, 2 * nq, nk)


def _pool_kernel(prev_ref, cur_ref, w_ref, sc_ref, *refs, tm, prompt, pos0):
    o_ref, ext_ref = refs[-2:]
    i = pl.program_id(0)
    prev = prev_ref[...]
    if prompt:
        prev = jnp.where(i == 0, 0.0, prev)
        pos = i * tm + lax.broadcasted_iota(jnp.int32, (tm, 1), 0)
    else:
        pos = pos0 + lax.broadcasted_iota(jnp.int32, (tm, 1), 0)
    ext_ref[0:POOL_HALO, :] = prev
    ext_ref[POOL_HALO:POOL_HALO + tm, :] = cur_ref[...]
    for g, w in enumerate(B_WINDOWS):
        cs = slice(g * B_GROUP, (g + 1) * B_GROUP)
        tok = ext_ref[POOL_HALO:POOL_HALO + tm, cs]
        tot = tok
        for j in range(1, w):
            tot = tot + ext_ref[POOL_HALO - j:POOL_HALO - j + tm, cs]
        cnt = jnp.minimum(pos + 1, w).astype(F32)
        pooled = (tot / cnt - tok).astype(BF16)
        o = jnp.dot(pooled, w_ref[g], preferred_element_type=F32) * sc_ref[:, cs]
        o_ref[:, cs] = o.astype(o_ref.dtype)


def pool_prompt(proj, pool_w, pool_scale, lp, tm):
    ucol = 3 * A_WIDTH // B_WIDTH
    per = tm // POOL_HALO
    return pl.pallas_call(
        functools.partial(_pool_kernel, tm=tm, prompt=True, pos0=0),
        grid=(lp // tm,),
        in_specs=[pl.BlockSpec((POOL_HALO, B_WIDTH), lambda i: (jnp.maximum(i * per - 1, 0), ucol)),
                  pl.BlockSpec((tm, B_WIDTH), lambda i: (i, ucol)),
                  pl.BlockSpec((len(B_WINDOWS), B_GROUP, B_GROUP), lambda i: (0, 0, 0)),
                  pl.BlockSpec((1, B_WIDTH), lambda i: (0, 0))],
        out_specs=pl.BlockSpec((tm, B_WIDTH), lambda i: (i, 0)),
        out_shape=jax.ShapeDtypeStruct((proj.shape[0], B_WIDTH), BF16),
        scratch_shapes=[pltpu.VMEM((POOL_HALO + tm, B_WIDTH), F32)],
        compiler_params=_params("parallel"),
    )(proj, proj, pool_w, pool_scale)


def pool_sample(u_ext, pool_w, pool_scale, dst, row0, past):
    nb, tot, _ = u_ext.shape
    ds = tot - POOL_HALO
    return pl.pallas_call(
        functools.partial(_pool_kernel, tm=ds, prompt=False, pos0=past),
        grid=(nb,),
        in_specs=[pl.BlockSpec((None, POOL_HALO, B_WIDTH), lambda b: (b, 0, 0)),
                  pl.BlockSpec((None, ds, B_WIDTH), lambda b: (b, POOL_HALO // ds, 0)),
                  pl.BlockSpec((len(B_WINDOWS), B_GROUP, B_GROUP), lambda b: (0, 0, 0)),
                  pl.BlockSpec((1, B_WIDTH), lambda b: (0, 0)),
                  pl.BlockSpec(memory_space=pl.ANY)],
        out_specs=pl.BlockSpec((ds, B_WIDTH), lambda b: (row0 // ds + b, 0)),
        out_shape=jax.ShapeDtypeStruct(dst.shape, dst.dtype),
        input_output_aliases={4: 0},
        scratch_shapes=[pltpu.VMEM((POOL_HALO + ds, B_WIDTH), F32)],
        compiler_params=_params("parallel"),
    )(u_ext, u_ext, pool_w, pool_scale, dst)


def _mm_res_kernel(*refs, n_in):
    xs, ws = refs[:n_in], refs[n_in:2 * n_in]
    res_ref, o_ref = refs[2 * n_in], refs[2 * n_in + 1]
    acc = res_ref[...]
    for x_ref, w_ref in zip(xs, ws):
        acc = acc + jnp.dot(x_ref[...], w_ref[...], preferred_element_type=F32)
    o_ref[...] = acc


def mm_res(xs, ws, res, tm, tn):
    m, n = res.shape
    n_in = len(xs)
    in_specs = ([pl.BlockSpec((tm, x.shape[1]), lambda i, j: (i, 0)) for x in xs]
                + [pl.BlockSpec((w.shape[0], tn), lambda i, j: (0, j)) for w in ws]
                + [pl.BlockSpec((tm, tn), lambda i, j: (i, j))])
    return pl.pallas_call(
        functools.partial(_mm_res_kernel, n_in=n_in),
        grid=(m // tm, n // tn),
        in_specs=in_specs,
        out_specs=pl.BlockSpec((tm, tn), lambda i, j: (i, j)),
        out_shape=jax.ShapeDtypeStruct((m, n), F32),
        compiler_params=_params("parallel", "arbitrary"),
    )(*xs, *ws, res)


def _swiglu_tile(h, wg, wu, wd):
    a = jnp.dot(h, wg, preferred_element_type=F32)
    b = jnp.dot(h, wu, preferred_element_type=F32)
    act = (a * jax.nn.sigmoid(a) * b).astype(BF16)
    return jnp.dot(act, wd, preferred_element_type=F32)


def _ffn_kernel(y_ref, g_ref, wg_ref, wu_ref, wd_ref, o_ref, h_ref, acc_ref):
    f = pl.program_id(1)

    @pl.when(f == 0)
    def _():
        h_ref[...] = _rms(y_ref[...], g_ref[...]).astype(BF16)
        acc_ref[...] = jnp.zeros_like(acc_ref)

    acc_ref[...] += _swiglu_tile(h_ref[...], wg_ref[...], wu_ref[...], wd_ref[...])

    @pl.when(f == pl.num_programs(1) - 1)
    def _():
        o_ref[...] = y_ref[...] + acc_ref[...]


def ffn(y, g, wg, wu, wd, tm, tf):
    m, d = y.shape
    ff = wg.shape[1]
    return pl.pallas_call(
        _ffn_kernel,
        grid=(m // tm, ff // tf),
        in_specs=[pl.BlockSpec((tm, d), lambda i, f: (i, 0)),
                  pl.BlockSpec((1, d), lambda i, f: (0, 0)),
                  pl.BlockSpec((d, tf), lambda i, f: (0, f)),
                  pl.BlockSpec((d, tf), lambda i, f: (0, f)),
                  pl.BlockSpec((tf, d), lambda i, f: (f, 0))],
        out_specs=pl.BlockSpec((tm, d), lambda i, f: (i, 0)),
        out_shape=jax.ShapeDtypeStruct((m, d), F32),
        scratch_shapes=[pltpu.VMEM((tm, d), BF16), pltpu.VMEM((tm, d), F32)],
        compiler_params=_params("parallel", "arbitrary"),
    )(y, g.reshape(1, d), wg, wu, wd)


def _rope_tables(pos, head_dim):
    rot = head_dim // ROPE_FRAC
    half = rot // 2
    inv = jnp.exp(-math.log(ROPE_THETA) * jnp.arange(half, dtype=F32) * (2.0 / rot))
    ang = pos.astype(F32)[:, None] * inv[None, :]
    cos, sin = jnp.cos(ang), jnp.sin(ang)
    m = pos.shape[0]
    one = jnp.ones((m, head_dim - rot), F32)
    zero_r = jnp.zeros((m, head_dim - rot), F32)
    zero_h = jnp.zeros((m, half), F32)
    c = jnp.concatenate([cos, cos, one], axis=1)
    s_dn = jnp.concatenate([-sin, zero_h, zero_r], axis=1)
    s_up = jnp.concatenate([zero_h, sin, zero_r], axis=1)
    rep = LANES // head_dim
    return jnp.stack([jnp.tile(c, (1, rep)), jnp.tile(s_dn, (1, rep)), jnp.tile(s_up, (1, rep))])


def _rot(x, tab_ref, half):
    return (x * tab_ref[0] + pltpu.roll(x, LANES - half, 1) * tab_ref[1]
            + pltpu.roll(x, half, 1) * tab_ref[2])


def _rope_kernel(main_ref, tail_ref, tq_ref, ti_ref, kv_ref, t32_ref, ob_ref,
                 *, n_q, n_k, n_v, n_qi, wi_scale, q_scale):
    half_qk = C_HEAD_DIM // ROPE_FRAC // 2
    half_i = IDX_DIM // ROPE_FRAC // 2
    for c in range(n_q + n_k + n_v + n_qi):
        cs = slice(c * LANES, (c + 1) * LANES)
        x = main_ref[:, cs]
        if c < n_q + n_k:
            x = _rot(x, tq_ref, half_qk)
        elif c >= n_q + n_k + n_v:
            x = _rot(x, ti_ref, half_i)
        if n_q <= c < n_q + n_k + n_v:
            kv_ref[:, (c - n_q) * LANES:(c - n_q + 1) * LANES] = x
        ob_ref[:, cs] = (x * q_scale if c < n_q else x).astype(BF16)
    t = tail_ref[...]
    lane = lax.broadcasted_iota(jnp.int32, (1, LANES), 1)
    r = _rot(t, ti_ref, half_i)
    t32_ref[...] = jnp.where(lane < IDX_DIM, r, t * wi_scale)
    c = n_q + n_k + n_v + n_qi
    ob_ref[:, c * LANES:(c + 1) * LANES] = jnp.where(lane < IDX_DIM, r, pltpu.roll(r, IDX_DIM, 1)).astype(BF16)


def rope_all(main, tail, tab_qk, tab_idx, tm):
    m, nmain = main.shape
    n_q = C_HEADS * C_HEAD_DIM // LANES
    n_k = C_KV_HEADS * C_HEAD_DIM // LANES
    n_qi = IDX_HEADS * IDX_DIM // LANES
    wi_scale = (IDX_HEADS ** -0.5) * (IDX_DIM ** -0.5)
    q_scale = (C_HEAD_DIM ** -0.5) * math.log2(math.e)
    row = lambda i: (i, 0)
    return pl.pallas_call(
        functools.partial(_rope_kernel, n_q=n_q, n_k=n_k, n_v=n_k, n_qi=n_qi, wi_scale=wi_scale, q_scale=q_scale),
        grid=(m // tm,),
        in_specs=[pl.BlockSpec((tm, nmain), row),
                  pl.BlockSpec((tm, LANES), row),
                  pl.BlockSpec((3, tm, LANES), lambda i: (0, i, 0)),
                  pl.BlockSpec((3, tm, LANES), lambda i: (0, i, 0))],
        out_specs=[pl.BlockSpec((tm, 2 * n_k * LANES), row),
                   pl.BlockSpec((tm, LANES), row),
                   pl.BlockSpec((tm, nmain + LANES), row)],
        out_shape=[jax.ShapeDtypeStruct((m, 2 * n_k * LANES), F32),
                   jax.ShapeDtypeStruct((m, LANES), F32),
                   jax.ShapeDtypeStruct((m, nmain + LANES), BF16)],
        compiler_params=_params("parallel"),
    )(main, tail, tab_qk, tab_idx)


def _dsa_kernel(q_ref, qi_ref, wi_ref, kn_ref, vn_ref, kin_ref, *refs, tq, kb, past, n_valid, q_pos0, topk):
    ck_refs, cv_refs = refs[:C_KV_HEADS], refs[C_KV_HEADS:2 * C_KV_HEADS]
    ci_ref = refs[2 * C_KV_HEADS]
    o_ref, k_ref, v_ref, ki_ref, sc_ref, qs_ref, acc_ref, m_ref, l_ref = refs[2 * C_KV_HEADS + 2:]
    i = pl.program_id(1)
    rows = C_GROUPS * tq
    sub = kb // LANES
    kf = float(topk)

    new = past + tq
    ci = ci_ref[...].astype(BF16)
    for g in range(C_KV_HEADS):
        gs = slice(g * C_HEAD_DIM, (g + 1) * C_HEAD_DIM)
        k_ref[0:past, gs] = ck_refs[g][...].astype(BF16)
        v_ref[0:past, gs] = cv_refs[g][...].astype(BF16)
    ki_ref[0:past, :] = jnp.concatenate([ci, ci], axis=1)
    k_ref[past:new, :] = kn_ref[...]
    v_ref[past:new, :] = vn_ref[...]
    ki_ref[past:new, :] = kin_ref[...]
    n_pad = k_ref.shape[0] - new
    k_ref[new:, :] = jnp.zeros((n_pad, k_ref.shape[1]), BF16)
    v_ref[new:, :] = jnp.zeros((n_pad, v_ref.shape[1]), BF16)
    ki_ref[new:, :] = jnp.zeros((n_pad, ki_ref.shape[1]), BF16)

    q_pos = q_pos0 + i * tq + lax.broadcasted_iota(jnp.int32, (tq, 1), 0)
    q_chunk = q_pos // CHUNK
    last_chunk = (q_pos0 + i * tq + tq - 1) // CHUNK
    kv_limit = jnp.minimum(n_valid, (last_chunk + 1) * CHUNK)
    nkb = (kv_limit + kb - 1) // kb

    lane = lax.broadcasted_iota(jnp.int32, (1, LANES), 1)
    first = lane < IDX_DIM
    wi = wi_ref[:, IDX_DIM:IDX_DIM + IDX_HEADS]

    def score_block(b, carry):
        for c in range(sub):
            off = pl.multiple_of(b * kb + c * LANES, LANES)
            kib = ki_ref[pl.ds(off, LANES), :]
            acc = jnp.zeros((tq, LANES), F32)
            for hp in range(IDX_HEADS // 2):
                qp = qi_ref[:, hp * LANES:(hp + 1) * LANES]
                for half in range(2):
                    keep = first if half == 0 else jnp.logical_not(first)
                    qm = jnp.where(keep, qp, jnp.zeros_like(qp))
                    d = lax.dot_general(qm, kib, _NT, preferred_element_type=F32)
                    h = hp * 2 + half
                    acc = acc + jnp.maximum(d, 0.0) * wi[:, h:h + 1]
            k_pos = off + lane
            adm = jnp.logical_and(k_pos // CHUNK <= q_chunk, k_pos < n_valid)
            sc_ref[b, :, c * LANES:(c + 1) * LANES] = jnp.where(adm, acc, NEG)
        return carry

    lax.fori_loop(0, nkb, score_block, 0)

    def lane_sum(x):
        return jnp.sum(x, axis=1, keepdims=True)

    def count_ge(t):
        def body(b, acc):
            for c in range(sub):
                blk = sc_ref[b, :, c * LANES:(c + 1) * LANES]
                acc = acc + jnp.where(blk >= t, 1.0, 0.0)
            return acc
        return lane_sum(lax.fori_loop(0, nkb, body, jnp.zeros((tq, LANES), F32)))

    def stats(b, carry):
        mx, mn, cnt = carry
        for c in range(sub):
            blk = sc_ref[b, :, c * LANES:(c + 1) * LANES]
            ok = blk > 0.5 * NEG
            mx = jnp.maximum(mx, blk)
            mn = jnp.minimum(mn, jnp.where(ok, blk, BIG))
            cnt = cnt + jnp.where(ok, 1.0, 0.0)
        return mx, mn, cnt

    mx, mn, cnt = lax.fori_loop(
        0, nkb, stats,
        (jnp.full((tq, LANES), NEG, F32), jnp.full((tq, LANES), BIG, F32), jnp.zeros((tq, LANES), F32)))
    row_max = jnp.max(mx, axis=1, keepdims=True)
    row_min = jnp.min(mn, axis=1, keepdims=True)
    n_adm = lane_sum(cnt)

    done0 = jnp.where(n_adm <= kf, 1.0, 0.0)
    state0 = (row_min, jnp.full((tq, 1), BIG, F32), row_max, jnp.full((tq, 1), 0.5 * NEG, F32), done0)

    def bisect(state, n_steps):
        def cond(c):
            it, st = c
            return jnp.logical_and(it < n_steps, jnp.min(st[4]) < 0.5)

        def body(c):
            it, (lo, hi, mid, thr, done) = c
            cnt = count_ge(mid)
            live = done < 0.5
            hit = jnp.logical_and(live, cnt == kf)
            ge = cnt >= kf
            thr = jnp.where(hit, mid, thr)
            done = jnp.where(hit, 1.0, done)
            lo = jnp.where(ge, mid, lo)
            hi = jnp.where(ge, hi, mid)
            return it + 1, (lo, hi, 0.5 * (lo + hi), thr, done)

        return lax.while_loop(cond, body, (jnp.int32(0), state))[1]

    def snap(state):
        lo, hi, mid, thr, done = state

        def body(b, carry):
            v_lo, v_hi = carry
            for c in range(sub):
                blk = sc_ref[b, :, c * LANES:(c + 1) * LANES]
                v_lo = jnp.minimum(v_lo, jnp.where(blk >= lo, blk, BIG))
                v_hi = jnp.maximum(v_hi, jnp.where(blk < hi, blk, NEG))
            return v_lo, v_hi

        v_lo, v_hi = lax.fori_loop(0, nkb, body,
                                   (jnp.full((tq, LANES), BIG, F32), jnp.full((tq, LANES), NEG, F32)))
        v_lo = jnp.min(v_lo, axis=1, keepdims=True)
        v_hi = jnp.max(v_hi, axis=1, keepdims=True)
        live = done < 0.5
        tie = jnp.logical_and(live, v_lo == v_hi)
        thr = jnp.where(tie, v_lo, thr)
        done = jnp.where(tie, 1.0, done)
        lo = jnp.where(live, v_lo, lo)
        return (lo, hi, 0.5 * (lo + hi), thr, done), jnp.where(tie, 1.0, 0.0)

    state = bisect(state0, 32)

    def refine_cond(c):
        rounds, st, _ = c
        return jnp.logical_and(rounds < 10, jnp.min(st[4]) < 0.5)

    def refine_body(c):
        rounds, st, tie = c
        st, new_tie = snap(st)
        st = bisect(st, 32)
        return rounds + 1, st, jnp.maximum(tie, new_tie)

    _, state, tie = lax.while_loop(refine_cond, refine_body,
                                   (jnp.int32(0), state, jnp.zeros((tq, 1), F32)))
    thr = state[3]
    any_tie = jnp.max(tie) > 0.5

    @pl.when(jnp.logical_not(any_tie))
    def _():
        def body(b, carry):
            for c in range(sub):
                cs = slice(c * LANES, (c + 1) * LANES)
                sc_ref[b, :, cs] = jnp.where(sc_ref[b, :, cs] >= thr, 0.0, NEG)
            return carry
        lax.fori_loop(0, nkb, body, 0)

    @pl.when(any_tie)
    def _():
        def gt_body(b, acc):
            for c in range(sub):
                acc = acc + jnp.where(sc_ref[b, :, c * LANES:(c + 1) * LANES] > thr, 1.0, 0.0)
            return acc
        need = kf - lane_sum(lax.fori_loop(0, nkb, gt_body, jnp.zeros((tq, LANES), F32)))

        def count_eq_upto(j):
            def body(b, acc):
                for c in range(sub):
                    blk = sc_ref[b, :, c * LANES:(c + 1) * LANES]
                    idx = (b * kb + c * LANES + lane).astype(F32)
                    acc = acc + jnp.where(jnp.logical_and(blk == thr, idx <= j), 1.0, 0.0)
                return acc
            return lane_sum(lax.fori_loop(0, nkb, body, jnp.zeros((tq, LANES), F32)))

        def idx_body(_, c):
            lo_j, hi_j = c
            mid_j = jnp.floor(0.5 * (lo_j + hi_j))
            ok = count_eq_upto(mid_j) >= need
            return jnp.where(ok, lo_j, mid_j), jnp.where(ok, mid_j, hi_j)

        n_steps = max(1, math.ceil(math.log2(sc_ref.shape[0] * kb + 1)))
        _, last = lax.fori_loop(
            0, n_steps, idx_body,
            (jnp.full((tq, 1), -1.0, F32), jnp.full((tq, 1), float(sc_ref.shape[0] * kb), F32)))
        last = jnp.where(tie > 0.5, last, BIG)

        def body(b, carry):
            for c in range(sub):
                cs = slice(c * LANES, (c + 1) * LANES)
                blk = sc_ref[b, :, cs]
                idx = (b * kb + c * LANES + lane).astype(F32)
                sel = jnp.logical_or(blk > thr, jnp.logical_and(blk == thr, idx <= last))
                sc_ref[b, :, cs] = jnp.where(sel, 0.0, NEG)
            return carry
        lax.fori_loop(0, nkb, body, 0)

    for g in range(C_KV_HEADS):
        for hh in range(C_GROUPS):
            h = g * C_GROUPS + hh
            qs_ref[g, hh * tq:(hh + 1) * tq, :] = q_ref[:, h * C_HEAD_DIM:(h + 1) * C_HEAD_DIM]
    m_ref[...] = jnp.full(m_ref.shape, NEG, F32)
    l_ref[...] = jnp.zeros(l_ref.shape, F32)
    acc_ref[...] = jnp.zeros(acc_ref.shape, F32)

    def attend(b, carry):
        off = pl.multiple_of(b * kb, kb)
        bias = sc_ref[b]
        bias = jnp.concatenate([bias] * C_GROUPS, axis=0)
        for g in range(C_KV_HEADS):
            cs = slice(g * C_HEAD_DIM, (g + 1) * C_HEAD_DIM)
            kblk = k_ref[pl.ds(off, kb), cs]
            vblk = v_ref[pl.ds(off, kb), cs]
            s = lax.dot_general(qs_ref[g], kblk, _NT, preferred_element_type=F32)
            s = s + bias
            m_old = m_ref[g]
            m_new = jnp.maximum(m_old, jnp.max(s, axis=1, keepdims=True))
            alpha = jnp.exp2(m_old - m_new)
            p = jnp.exp2(s - m_new)
            l_ref[g] = alpha * l_ref[g] + jnp.sum(p, axis=1, keepdims=True)
            acc_ref[g] = alpha * acc_ref[g] + jnp.dot(p.astype(BF16), vblk, preferred_element_type=F32)
            m_ref[g] = m_new
        return carry

    lax.fori_loop(0, nkb, attend, 0)

    for g in range(C_KV_HEADS):
        o = acc_ref[g] / l_ref[g]
        for hh in range(C_GROUPS):
            h = g * C_GROUPS + hh
            o_ref[:, h * C_HEAD_DIM:(h + 1) * C_HEAD_DIM] = o[hh * tq:(hh + 1) * tq, :].astype(o_ref.dtype)


def dsa_sample(ob, t32, cache_k, cache_v, cache_i, dst, *, row0, kb, topk):
    nb, past = cache_k.shape[:2]
    head = lambda g: pl.BlockSpec((None, past, C_HEAD_DIM), lambda b, i: (b, 0, g))
    heads = [head(g) for g in range(C_KV_HEADS)]
    n_kv = C_KV_HEADS * C_HEAD_DIM
    ds = (ob.shape[0] - row0) // nb
    n_q = C_HEADS * C_HEAD_DIM
    n_qi = IDX_HEADS * IDX_DIM
    s_all = past + ds
    s_pad = -(-s_all // kb) * kb
    rows = C_GROUPS * ds
    col = lambda w, off: (lambda b, i: (row0 // ds + b, off // w))
    cmap = lambda b, i: (b, 0, 0)
    return pl.pallas_call(
        functools.partial(_dsa_kernel, tq=ds, kb=kb, past=past, n_valid=s_all, q_pos0=past, topk=topk),
        grid=(nb, 1),
        in_specs=[pl.BlockSpec((ds, n_q), col(n_q, 0)),
                  pl.BlockSpec((ds, n_qi), col(n_qi, n_q + 2 * n_kv)),
                  pl.BlockSpec((ds, LANES), col(LANES, 0)),
                  pl.BlockSpec((ds, n_kv), col(n_kv, n_q)),
                  pl.BlockSpec((ds, n_kv), col(n_kv, n_q + n_kv)),
                  pl.BlockSpec((ds, LANES), col(LANES, n_q + 2 * n_kv + n_qi))]
                 + heads + heads + [pl.BlockSpec((None, past, IDX_DIM), cmap),
                                    pl.BlockSpec(memory_space=pl.ANY)],
        out_specs=pl.BlockSpec((ds, n_q), col(n_q, 0)),
        out_shape=jax.ShapeDtypeStruct(dst.shape, dst.dtype),
        input_output_aliases={7 + 2 * C_KV_HEADS: 0},
        scratch_shapes=[pltpu.VMEM((s_pad, n_kv), BF16),
                        pltpu.VMEM((s_pad, n_kv), BF16),
                        pltpu.VMEM((s_pad, LANES), BF16),
                        pltpu.VMEM((s_pad // kb, ds, kb), F32),
                        pltpu.VMEM((C_KV_HEADS, rows, C_HEAD_DIM), BF16),
                        pltpu.VMEM((C_KV_HEADS, rows, C_HEAD_DIM), F32),
                        pltpu.VMEM((C_KV_HEADS, rows, 1), F32),
                        pltpu.VMEM((C_KV_HEADS, rows, 1), F32)],
        compiler_params=_params("parallel", "arbitrary"),
    )(ob, ob, t32, ob, ob, ob, *([cache_k] * C_KV_HEADS), *([cache_v] * C_KV_HEADS), cache_i, dst)


def _dsa_t_kernel(q_ref, qi_ref, wi_ref, k_ref, vt_ref, ki_ref, o_ref,
                  sc_ref, qim_ref, acc_ref, m_ref, l_ref, l8_ref, kmax_ref,
                  *, tq, kb, qb, pb, cb, n_valid, q_pos0, topk):
    i = pl.program_id(1)
    kf = float(topk)

    q_pos = q_pos0 + i * tq + lax.broadcasted_iota(jnp.int32, (1, tq), 1)
    q_chunk = q_pos // CHUNK
    last_chunk = (q_pos0 + i * tq + tq - 1) // CHUNK
    kv_limit = jnp.minimum(n_valid, (last_chunk + 1) * CHUNK)
    nkb = (kv_limit + kb - 1) // kb
    ncb = (kv_limit + cb - 1) // cb

    def col_reduce(x, op):
        groups = x.shape[0] // 8
        chains = 8 if groups % 8 == 0 else 1
        return op(op(x.reshape(chains, groups // chains, 8, tq), axis=1), axis=0)

    col_sum = lambda x: col_reduce(x, jnp.sum)
    col_max = lambda x: col_reduce(x, jnp.max)
    col_min = lambda x: col_reduce(x, jnp.min)

    lane = lax.broadcasted_iota(jnp.int32, (1, LANES), 1)
    first = lane < IDX_DIM
    for hp in range(IDX_HEADS // 2):
        qp = qi_ref[:, hp * LANES:(hp + 1) * LANES]
        qim_ref[2 * hp] = jnp.where(first, qp, jnp.zeros_like(qp))
        qim_ref[2 * hp + 1] = jnp.where(first, jnp.zeros_like(qp), qp)
    block_iota = lax.broadcasted_iota(jnp.int32, (pb, 1), 0)

    def score_block(b, carry):
        off = pl.multiple_of(b * pb, pb)
        kib = ki_ref[pl.ds(off, pb), :]
        parts = []
        for h in range(IDX_HEADS):
            d = lax.dot_general(kib, qim_ref[h], _NT, preferred_element_type=F32)
            parts.append(jnp.maximum(d, 0.0) * wi_ref[h:h + 1, :])
        while len(parts) > 1:
            parts = [a + b for a, b in zip(parts[0::2], parts[1::2])]
        acc = parts[0]
        k_pos = off + block_iota
        adm = jnp.logical_and(k_pos // CHUNK <= q_chunk, k_pos < n_valid)
        sc_ref[pl.ds(off, pb), :] = jnp.where(adm, acc, NEG)
        return carry

    lax.fori_loop(0, ncb * (cb // pb), score_block, 0)
    key_iota = lax.broadcasted_iota(jnp.int32, (cb, 1), 0)

    def blocks(body, init):
        def step(b, carry):
            off = pl.multiple_of(b * cb, cb)
            return body(off, sc_ref[pl.ds(off, cb), :], carry)
        return lax.fori_loop(0, ncb, step, init)

    def total(x):
        return jnp.sum(x, axis=0, keepdims=True)

    def count_ge(t):
        return total(blocks(lambda off, blk, acc: acc + col_sum(jnp.where(blk >= t, 1.0, 0.0)),
                            jnp.zeros((8, tq), F32)))

    def stats(off, blk, carry):
        mx, mn, cnt, pos, nonneg = carry
        ok = blk > 0.5 * NEG
        return (jnp.maximum(mx, col_max(blk)), jnp.minimum(mn, col_min(jnp.where(ok, blk, BIG))),
                cnt + col_sum(jnp.where(ok, 1.0, 0.0)),
                pos + col_sum(jnp.where(blk > 0.0, 1.0, 0.0)),
                nonneg + col_sum(jnp.where(blk >= 0.0, 1.0, 0.0)))

    zeros8 = jnp.zeros((8, tq), F32)
    mx, mn, cnt, pos, nonneg = blocks(stats, (jnp.full((8, tq), NEG, F32), jnp.full((8, tq), BIG, F32),
                                              zeros8, zeros8, zeros8))
    row_max = jnp.max(mx, axis=0, keepdims=True)
    row_min = jnp.min(mn, axis=0, keepdims=True)
    n_adm = total(cnt)
    n_pos = total(pos)
    n_nonneg = total(nonneg)

    open_row = n_adm > kf
    above = n_pos >= kf
    below = n_nonneg < kf
    zero_tie = jnp.logical_and(open_row, jnp.logical_not(jnp.logical_or(above, below)))
    lo0 = jnp.where(above, 0.0, row_min)
    hi0 = jnp.where(below, 0.0, BIG)
    mid0 = jnp.where(below, 0.5 * (lo0 + hi0), row_max)
    thr0 = jnp.where(zero_tie, 0.0, 0.5 * NEG)
    done0 = jnp.where(jnp.logical_or(jnp.logical_not(open_row), zero_tie), 1.0, 0.0)
    tie0 = jnp.where(zero_tie, 1.0, 0.0)
    state0 = (lo0, hi0, mid0, thr0, done0)

    def bisect(state, n_steps):
        def cond(c):
            it, st = c
            return jnp.logical_and(it < n_steps, jnp.min(st[4]) < 0.5)

        def body(c):
            it, (lo, hi, mid, thr, done) = c
            cnt = count_ge(mid)
            hit = jnp.logical_and(done < 0.5, cnt == kf)
            ge = cnt >= kf
            thr = jnp.where(hit, mid, thr)
            done = jnp.where(hit, 1.0, done)
            lo = jnp.where(ge, mid, lo)
            hi = jnp.where(ge, hi, mid)
            return it + 1, (lo, hi, 0.5 * (lo + hi), thr, done)

        return lax.while_loop(cond, body, (jnp.int32(0), state))[1]

    def snap(state):
        lo, hi, mid, thr, done = state

        def body(off, blk, carry):
            v_lo, v_hi = carry
            return (jnp.minimum(v_lo, col_min(jnp.where(blk >= lo, blk, BIG))),
                    jnp.maximum(v_hi, col_max(jnp.where(blk < hi, blk, NEG))))

        v_lo, v_hi = blocks(body, (jnp.full((8, tq), BIG, F32), jnp.full((8, tq), NEG, F32)))
        v_lo = jnp.min(v_lo, axis=0, keepdims=True)
        v_hi = jnp.max(v_hi, axis=0, keepdims=True)
        live = done < 0.5
        tie = jnp.logical_and(live, v_lo == v_hi)
        thr = jnp.where(tie, v_lo, thr)
        done = jnp.where(tie, 1.0, done)
        lo = jnp.where(live, v_lo, lo)
        return (lo, hi, 0.5 * (lo + hi), thr, done), jnp.where(tie, 1.0, 0.0)

    state = bisect(state0, 32)

    def refine_cond(c):
        rounds, st, _ = c
        return jnp.logical_and(rounds < 10, jnp.min(st[4]) < 0.5)

    def refine_body(c):
        rounds, st, tie = c
        st, new_tie = snap(st)
        st = bisect(st, 32)
        return rounds + 1, st, jnp.maximum(tie, new_tie)

    _, state, tie = lax.while_loop(refine_cond, refine_body, (jnp.int32(0), state, tie0))
    thr = state[3]
    any_tie = jnp.max(tie) > 0.5

    @pl.when(jnp.logical_not(any_tie))
    def _():
        def body(off, blk, carry):
            sc_ref[pl.ds(off, cb), :] = jnp.where(blk >= thr, 0.0, NEG)
            return carry
        blocks(body, 0)

    @pl.when(any_tie)
    def _():
        need = kf - total(blocks(lambda off, blk, acc: acc + col_sum(jnp.where(blk > thr, 1.0, 0.0)),
                                 jnp.zeros((8, tq), F32)))

        def count_eq_upto(j):
            def body(off, blk, acc):
                idx = (off + key_iota).astype(F32)
                return acc + col_sum(jnp.where(jnp.logical_and(blk == thr, idx <= j), 1.0, 0.0))
            return total(blocks(body, jnp.zeros((8, tq), F32)))

        def idx_cond(c):
            it, (lo_j, hi_j, settled) = c
            return jnp.logical_and(it < n_steps, jnp.min(settled) < 0.5)

        def idx_body(c):
            it, (lo_j, hi_j, settled) = c
            mid_j = jnp.floor(0.5 * (lo_j + hi_j))
            cnt = count_eq_upto(mid_j)
            ok = cnt >= need
            live = settled < 0.5
            lo_j = jnp.where(jnp.logical_and(live, jnp.logical_not(ok)), mid_j, lo_j)
            hi_j = jnp.where(jnp.logical_and(live, ok), mid_j, hi_j)
            settled = jnp.where(jnp.logical_or(cnt == need, hi_j - lo_j <= 1.0), 1.0, settled)
            return it + 1, (lo_j, hi_j, settled)

        n_keys = sc_ref.shape[0]
        n_steps = max(1, math.ceil(math.log2(n_keys + 1))) + 1
        _, (_, last, _) = lax.while_loop(
            idx_cond, idx_body,
            (jnp.int32(0), (jnp.full((1, tq), -1.0, F32), jnp.full((1, tq), float(n_keys), F32),
                            jnp.where(tie > 0.5, 0.0, 1.0))))
        last = jnp.where(tie > 0.5, last, BIG)

        def body(off, blk, carry):
            idx = (off + key_iota).astype(F32)
            sel = jnp.logical_or(blk > thr, jnp.logical_and(blk == thr, idx <= last))
            sc_ref[pl.ds(off, cb), :] = jnp.where(sel, 0.0, NEG)
            return carry
        blocks(body, 0)

    halves = kb // qb

    def head_scores(off, h):
        gs = slice((h // C_GROUPS) * C_HEAD_DIM, (h // C_GROUPS + 1) * C_HEAD_DIM)
        out = []
        for j in range(halves):
            rows = pl.ds(off + j * qb, qb)
            s = lax.dot_general(k_ref[rows, gs], q_ref[:, h * C_HEAD_DIM:(h + 1) * C_HEAD_DIM],
                                _NT, preferred_element_type=F32)
            out.append(s + sc_ref[rows, :])
        return out, gs

    @pl.when(i == 0)
    def _():
        for g in range(C_KV_HEADS):
            gs = slice(g * C_HEAD_DIM, (g + 1) * C_HEAD_DIM)

            def body(b, mx):
                kk = k_ref[pl.ds(pl.multiple_of(b * cb, cb), cb), gs].astype(F32)
                return jnp.maximum(mx, jnp.sum(kk * kk, axis=1, keepdims=True))

            mx = lax.fori_loop(0, k_ref.shape[0] // cb, body, jnp.zeros((cb, 1), F32))
            kmax_ref[g] = jnp.full((1, tq), 1.0, F32) * jnp.max(mx)

    ones = jnp.ones((8, C_HEAD_DIM), BF16)
    for h in range(C_HEADS):
        qh = q_ref[:, h * C_HEAD_DIM:(h + 1) * C_HEAD_DIM].astype(F32)
        qn2 = lax.dot_general(ones, (qh * qh).astype(BF16), _NT, preferred_element_type=F32)[0:1, :]
        m_ref[h] = jnp.sqrt(qn2 * kmax_ref[h // C_GROUPS]) * SHIFT_SLACK
    l8_ref[...] = jnp.zeros(l8_ref.shape, F32)
    acc_ref[...] = jnp.zeros(acc_ref.shape, F32)

    def attend_fixed(b0, n):
        for h in range(C_HEADS):
            lsum, upd = None, None
            for u in range(n):
                b = b0 + u
                ss, gs = head_scores(pl.multiple_of(b * kb, kb), h)
                ps = [jnp.exp2(s - m_ref[h]) for s in ss]
                part = sum(col_sum(p) for p in ps)
                p_all = jnp.concatenate([p.astype(BF16) for p in ps], axis=0)
                pv = jnp.dot(vt_ref[b, gs, :], p_all, preferred_element_type=F32)
                lsum = part if lsum is None else lsum + part
                upd = pv if upd is None else upd + pv
            l8_ref[h] += lsum
            acc_ref[h] += upd

    def four_blocks(t, carry):
        attend_fixed(4 * t, 4)
        return carry

    lax.fori_loop(0, nkb // 4, four_blocks, 0)

    @pl.when(nkb % 4 >= 2)
    def _():
        attend_fixed((nkb // 4) * 4, 2)
    l_min = jnp.full((1, tq), BIG, F32)
    for h in range(C_HEADS):
        l_ref[h] = total(l8_ref[h])
        l_min = jnp.minimum(l_min, l_ref[h])

    @pl.when(jnp.logical_not(jnp.min(l_min) >= MIN_SOFTMAX_SUM))
    def _():
        m_ref[...] = jnp.full(m_ref.shape, NEG, F32)
        l_ref[...] = jnp.zeros(l_ref.shape, F32)
        acc_ref[...] = jnp.zeros(acc_ref.shape, F32)

        def attend(b, carry):
            off = pl.multiple_of(b * kb, kb)
            for h in range(C_HEADS):
                ss, gs = head_scores(off, h)
                m_old = m_ref[h]
                m_new = m_old
                for s in ss:
                    m_new = jnp.maximum(m_new, jnp.max(col_max(s), axis=0, keepdims=True))
                alpha = jnp.exp2(m_old - m_new)
                ps = [jnp.exp2(s - m_new) for s in ss]
                l_ref[h] = alpha * l_ref[h] + total(sum(col_sum(p) for p in ps))
                p_all = jnp.concatenate([p.astype(BF16) for p in ps], axis=0)
                acc_ref[h] = alpha * acc_ref[h] + jnp.dot(vt_ref[b, gs, :], p_all,
                                                          preferred_element_type=F32)
                m_ref[h] = m_new
            return carry

        lax.fori_loop(0, nkb, attend, 0)

    for h in range(C_HEADS):
        o = acc_ref[h] / l_ref[h]
        o_ref[:, h * C_HEAD_DIM:(h + 1) * C_HEAD_DIM] = o.T.astype(o_ref.dtype)


def dsa_t(ob, wi_t, vt, *, nq, tq, kb, qb, pb, cb, topk):
    dq = C_HEADS * C_HEAD_DIM
    n_kv = C_KV_HEADS * C_HEAD_DIM
    n_qi = IDX_HEADS * IDX_DIM
    s_pad = nq
    assert tq % (2 * kb) == 0 and tq % CHUNK == 0 and nq % cb == 0 and cb % pb == 0 and kb % qb == 0
    return pl.pallas_call(
        functools.partial(_dsa_t_kernel, tq=tq, kb=kb, qb=qb, pb=pb, cb=cb, n_valid=nq, q_pos0=0, topk=topk),
        grid=(1, nq // tq),
        in_specs=[pl.BlockSpec((tq, dq), lambda b, i: (i, 0)),
                  pl.BlockSpec((tq, n_qi), lambda b, i: (i, (dq + 2 * n_kv) // n_qi)),
                  pl.BlockSpec((IDX_HEADS, tq), lambda b, i: (0, i)),
                  pl.BlockSpec((nq, n_kv), lambda b, i: (0, dq // n_kv)),
                  pl.BlockSpec(vt.shape, lambda b, i: (0, 0, 0)),
                  pl.BlockSpec((nq, LANES), lambda b, i: (0, (dq + 2 * n_kv + n_qi) // LANES))],
        out_specs=pl.BlockSpec((tq, dq), lambda b, i: (i, 0)),
        out_shape=jax.ShapeDtypeStruct((ob.shape[0], dq), BF16),
        scratch_shapes=[pltpu.VMEM((s_pad, tq), F32),
                        pltpu.VMEM((IDX_HEADS, tq, LANES), BF16),
                        pltpu.VMEM((C_HEADS, C_HEAD_DIM, tq), F32),
                        pltpu.VMEM((C_HEADS, 1, tq), F32),
                        pltpu.VMEM((C_HEADS, 1, tq), F32),
                        pltpu.VMEM((C_HEADS, 8, tq), F32),
                        pltpu.VMEM((C_KV_HEADS, 1, tq), F32)],
        compiler_params=_params("arbitrary", "arbitrary"),
    )(ob, ob, wi_t, ob, vt, ob)


def _router_kernel(y_ref, g_ref, wr_ref, h_ref, gate_ref):
    h = _rms(y_ref[...], g_ref[...])
    h_ref[...] = h.astype(BF16)
    logits = jnp.dot(h, wr_ref[...], preferred_element_type=F32, precision=lax.Precision.HIGHEST)
    lane = lax.broadcasted_iota(jnp.int32, logits.shape, 1)
    lg = jnp.where(lane < N_EXPERTS, logits, NEG)
    m1 = jnp.max(lg, axis=1, keepdims=True)
    i1 = jnp.min(jnp.where(lg == m1, lane, LANES), axis=1, keepdims=True)
    lg2 = jnp.where(lane == i1, NEG, lg)
    m2 = jnp.max(lg2, axis=1, keepdims=True)
    i2 = jnp.min(jnp.where(lg2 == m2, lane, LANES), axis=1, keepdims=True)
    e = jnp.exp(m2 - m1)
    g1 = 1.0 / (1.0 + e)
    g2 = e / (1.0 + e)
    meta = jnp.where(lane == 0, i1.astype(F32), jnp.where(lane == 1, i2.astype(F32),
                     jnp.where(lane == 2, g1, jnp.where(lane == 3, g2, 0.0))))
    gate_ref[...] = meta


def router(y, g, w_router_pad, tm):
    m, d = y.shape
    return pl.pallas_call(
        _router_kernel,
        grid=(m // tm,),
        in_specs=[pl.BlockSpec((tm, d), lambda i: (i, 0)),
                  pl.BlockSpec((1, d), lambda i: (0, 0)),
                  pl.BlockSpec((d, LANES), lambda i: (0, 0))],
        out_specs=[pl.BlockSpec((tm, d), lambda i: (i, 0)),
                   pl.BlockSpec((tm, LANES), lambda i: (i, 0))],
        out_shape=[jax.ShapeDtypeStruct((m, d), BF16), jax.ShapeDtypeStruct((m, LANES), F32)],
        compiler_params=_params("parallel"),
    )(y, g.reshape(1, d), w_router_pad)


def route_plan(meta, sup):
    m = meta.shape[0]
    n_tiles = (2 * m) // sup + N_EXPERTS
    e_all = jnp.concatenate([meta[:, 0], meta[:, 1]]).astype(jnp.int32)
    onehot = (e_all[:, None] == jnp.arange(N_EXPERTS)[None, :]).astype(jnp.int32)
    rank = jnp.sum((jnp.cumsum(onehot, axis=0) - onehot) * onehot, axis=1)
    counts = jnp.sum(onehot, axis=0)
    n_super = (counts + sup - 1) // sup
    super_end = jnp.cumsum(n_super)
    super_start = super_end - n_super
    pos = (super_start * sup)[e_all] + rank
    tiles = jnp.arange(n_tiles)
    used = super_end[-1]
    t_eff = jnp.minimum(tiles, used - 1)
    tile_expert = jnp.minimum(jnp.searchsorted(super_end, t_eff, side="right"), N_EXPERTS - 1).astype(jnp.int32)
    rows = jnp.clip(counts[tile_expert] - (t_eff - super_start[tile_expert]) * sup, 0, sup)
    tile_rows = jnp.where(tiles < used, rows, 0).astype(jnp.int32)
    token = jnp.concatenate([jnp.arange(m), jnp.arange(m)]).astype(jnp.int32)
    row_token = (jnp.arange(n_tiles * sup, dtype=jnp.int32) % m).at[pos].set(token)
    return pos.astype(jnp.int32), row_token, tile_expert, tile_rows


def _gather_rows_kernel(tok_ref, h_ref, o_ref, sem, *, batch):
    base = pl.program_id(0) * batch

    def issue(j, carry):
        pltpu.make_async_copy(h_ref.at[tok_ref[base + j]], o_ref.at[j], sem).start()
        return carry

    lax.fori_loop(0, batch, issue, 0)
    pltpu.make_async_copy(o_ref, o_ref, sem).wait()


def gather_rows(h3, row_token, batch):
    n_rows = row_token.shape[0]
    blk = (batch,) + h3.shape[1:]
    return pl.pallas_call(
        functools.partial(_gather_rows_kernel, batch=batch),
        grid_spec=pltpu.PrefetchScalarGridSpec(
            num_scalar_prefetch=1,
            grid=(n_rows // batch,),
            in_specs=[pl.BlockSpec(memory_space=pl.ANY)],
            out_specs=pl.BlockSpec(blk, lambda i, tok: (i, 0, 0)),
            scratch_shapes=[pltpu.SemaphoreType.DMA(())]),
        out_shape=jax.ShapeDtypeStruct((n_rows,) + h3.shape[1:], h3.dtype),
        compiler_params=_params("arbitrary"),
    )(row_token, h3)


def _grouped_ffn_kernel(te_ref, tr_ref, x_ref, wg_ref, wu_ref, wd_ref, o_ref,
                        wgb_ref, wub_ref, wdb_ref, *, sup, sub):
    t = pl.program_id(0)
    f = pl.program_id(1)
    rows = tr_ref[t]

    @pl.when(rows > 0)
    def _():
        wgb_ref[...] = wg_ref[...].astype(BF16)
        wub_ref[...] = wu_ref[...].astype(BF16)
        wdb_ref[...] = wd_ref[...].astype(BF16)

    for s in range(sup // sub):
        sl = slice(s * sub, (s + 1) * sub)

        @pl.when(jnp.logical_and(s * sub >= rows, f == 0))
        def _():
            o_ref[sl, :] = jnp.zeros((sub, o_ref.shape[1]), F32)

        @pl.when(s * sub < rows)
        def _():
            part = _swiglu_tile(x_ref[sl, :], wgb_ref[...], wub_ref[...], wdb_ref[...])

            @pl.when(f == 0)
            def _():
                o_ref[sl, :] = part

            @pl.when(f > 0)
            def _():
                o_ref[sl, :] += part


def grouped_ffn(xs, tile_expert, tile_rows, wg, wu, wd, sup, sub, tf):
    n_rows, d = xs.shape
    ff = wg.shape[2]
    n_f = ff // tf
    fidx = lambda t, f, te, tr: jnp.where(tr[t] > 0, f, n_f - 1)
    return pl.pallas_call(
        functools.partial(_grouped_ffn_kernel, sup=sup, sub=sub),
        grid_spec=pltpu.PrefetchScalarGridSpec(
            num_scalar_prefetch=2,
            grid=(n_rows // sup, n_f),
            in_specs=[pl.BlockSpec((sup, d), lambda t, f, te, tr: (t, 0), pipeline_mode=pl.Buffered(1)),
                      pl.BlockSpec((None, d, tf), lambda t, f, te, tr: (te[t], 0, fidx(t, f, te, tr))),
                      pl.BlockSpec((None, d, tf), lambda t, f, te, tr: (te[t], 0, fidx(t, f, te, tr))),
                      pl.BlockSpec((None, tf, d), lambda t, f, te, tr: (te[t], fidx(t, f, te, tr), 0))],
            out_specs=pl.BlockSpec((sup, d), lambda t, f, te, tr: (t, 0), pipeline_mode=pl.Buffered(1)),
            scratch_shapes=[pltpu.VMEM((d, tf), BF16), pltpu.VMEM((d, tf), BF16), pltpu.VMEM((tf, d), BF16)]),
        out_shape=jax.ShapeDtypeStruct((n_rows, d), F32),
        compiler_params=_params("arbitrary", "arbitrary"),
    )(tile_expert, tile_rows, xs, wg, wu, wd)


def _combine_kernel(pos_ref, y_ref, meta_ref, g_ref, ys_ref, op_ref, os_ref, buf_ref, x_ref, sem,
                    *, n_tok, tmc, prompt_tiles):
    i = pl.program_id(0)
    slot = i % 2

    def issue(tile, into):
        def body(j, carry):
            t = tile * tmc + j
            pltpu.make_async_copy(ys_ref.at[pos_ref[t]], buf_ref.at[into, 0, j], sem.at[into]).start()
            pltpu.make_async_copy(ys_ref.at[pos_ref[n_tok + t]], buf_ref.at[into, 1, j], sem.at[into]).start()
            return carry
        lax.fori_loop(0, tmc, body, 0)

    @pl.when(i == 0)
    def _():
        issue(0, 0)

    @pl.when(i + 1 < pl.num_programs(0))
    def _():
        issue(i + 1, 1 - slot)

    for e in range(2):
        pltpu.make_async_copy(buf_ref.at[slot, e], buf_ref.at[slot, e], sem.at[slot]).wait()
    g1 = meta_ref[:, 2:3]
    g2 = meta_ref[:, 3:4]
    ss = jnp.zeros((tmc, 1), F32)
    for c in range(buf_ref.shape[3]):
        cs = slice(c * LANES, (c + 1) * LANES)
        x = y_ref[:, cs] + (buf_ref[slot, 0, :, c, :] * g1 + buf_ref[slot, 1, :, c, :] * g2)
        ss = ss + jnp.sum(x * x, axis=1, keepdims=True)
        x_ref[:, cs] = x
    inv = lax.rsqrt(ss / x_ref.shape[1] + NORM_EPS)

    @pl.when(i < prompt_tiles)
    def _():
        op_ref[...] = x_ref[...] * inv * g_ref[...]

    @pl.when(i >= prompt_tiles)
    def _():
        os_ref[...] = x_ref[...] * inv * g_ref[...]


def combine_final(y, meta, g_final, ys3, pos, n_prompt, tmc):
    m, d = y.shape
    slab = ys3.shape[1:]
    prompt_tiles = n_prompt // tmc
    return pl.pallas_call(
        functools.partial(_combine_kernel, n_tok=m, tmc=tmc, prompt_tiles=prompt_tiles),
        grid_spec=pltpu.PrefetchScalarGridSpec(
            num_scalar_prefetch=1,
            grid=(m // tmc,),
            in_specs=[pl.BlockSpec((tmc, d), lambda i, pos: (i, 0)),
                      pl.BlockSpec((tmc, LANES), lambda i, pos: (i, 0)),
                      pl.BlockSpec((1, d), lambda i, pos: (0, 0)),
                      pl.BlockSpec(memory_space=pl.ANY)],
            out_specs=[pl.BlockSpec((tmc, d), lambda i, pos: (jnp.minimum(i, prompt_tiles - 1), 0)),
                       pl.BlockSpec((tmc, d), lambda i, pos: (jnp.maximum(i - prompt_tiles, 0), 0))],
            scratch_shapes=[pltpu.VMEM((2, 2, tmc) + slab, F32), pltpu.VMEM((tmc, d), F32),
                            pltpu.SemaphoreType.DMA((2,))]),
        out_shape=[jax.ShapeDtypeStruct((n_prompt, d), F32), jax.ShapeDtypeStruct((m - n_prompt, d), F32)],
        compiler_params=_params("arbitrary"),
    )(pos, y, meta, g_final.reshape(1, d), ys3)


def kernel(x_prompt, x_sample, cache_a_k, cache_a_v, state_pool, cache_c_k, cache_c_v, cache_c_idx,
           norm_mix, norm_ffn, norm_final, w_in_even, w_out_even, a_rel_bias, pool_w, pool_scale,
           ffn_w_gate, ffn_w_up, ffn_w_down, w_in_odd, w_out_odd,
           moe_router, moe_w_gate, moe_w_up, moe_w_down):
    nbp, lp, d = x_prompt.shape
    nb, ds, _ = x_sample.shape
    past = cache_c_k.shape[2]
    a_len = cache_a_k.shape[2]
    assert nbp == 1 and lp % 512 == 0 and (nb * ds) % 512 == 0 and ds == POOL_HALO and past >= POOL_HALO
    ns = nb * ds
    m = lp + ns
    tm = 512
    bf = lambda t: t.astype(BF16)

    proj0, x = stack_norm_proj(x_prompt.reshape(lp, d), x_sample.reshape(ns, d), norm_mix[0],
                               bf(w_in_even[0]), tm, 1024)
    k0 = proj0[:, A_WIDTH:2 * A_WIDTH]
    v0 = proj0[:, 2 * A_WIDTH:3 * A_WIDTH]
    u0 = proj0[:, 3 * A_WIDTH:]

    pad = A_PREV_CHUNKS * CHUNK
    bias_p = _rel_bias_tile(a_rel_bias[0], 0, CHUNK, -pad, A_BAND)
    a_p = band_prompt(proj0, _pair_rows(bias_p), lp)

    k_pos = past - a_len + jnp.arange(a_len + ds)
    q_pos = past + jnp.arange(ds)
    qch, kch = q_pos // CHUNK, k_pos // CHUNK
    ok = ((k_pos[None, :] >= 0) & (kch[None, :] <= qch[:, None])
          & (kch[None, :] >= qch[:, None] - A_PREV_CHUNKS))
    bias_s = jnp.where(ok[None], _rel_bias_tile(a_rel_bias[0], past, ds, past - a_len, a_len + ds), NEG)
    a = band_sample(proj0, cache_a_k[0].reshape(nb, a_len, A_WIDTH),
                    cache_a_v[0].reshape(nb, a_len, A_WIDTH), _pair_rows(bias_s), a_p, lp, nb, ds)

    u_s = u0[lp:].reshape(nb, ds, B_WIDTH)
    u_hist = jnp.concatenate([state_pool[0], u_s], axis=1)
    u_ext = jnp.concatenate([jnp.zeros((nb, POOL_HALO - B_HIST, B_WIDTH), F32), u_hist], axis=1)
    pw = bf(pool_w[0])
    ps = pool_scale[0].reshape(1, B_WIDTH)
    p_p = pool_prompt(proj0, pw, ps, lp, tm)
    p = pool_sample(u_ext, pw, ps, p_p, lp, past)
    wo = bf(w_out_even[0])
    y = mm_res([a, p], [wo[:A_WIDTH], wo[A_WIDTH:]], x, tm, d)
    y = ffn(y, norm_ffn[0], bf(ffn_w_gate[0]), bf(ffn_w_up[0]), bf(ffn_w_down[0]), tm, 512)

    n_q = C_HEADS * C_HEAD_DIM
    n_kv = C_KV_HEADS * C_HEAD_DIM
    n_qi = IDX_HEADS * IDX_DIM
    n_main = n_q + 2 * n_kv + n_qi
    w1 = w_in_odd[0]
    w_tail = jnp.pad(w1[:, n_main:], ((0, 0), (0, LANES - (w1.shape[1] - n_main))))
    main = norm_proj(y, norm_mix[1], bf(w1[:, :n_main]), tm, n_main // 2)
    tail = norm_proj(y, norm_mix[1], bf(w_tail), tm, LANES)
    pos = jnp.concatenate([jnp.arange(lp), jnp.tile(past + jnp.arange(ds), nb)])
    kv1, t32, ob = rope_all(main, tail, _rope_tables(pos, C_HEAD_DIM), _rope_tables(pos, IDX_DIM), tm)
    k1 = kv1[:, :n_kv]
    v1 = kv1[:, n_kv:]
    ki1 = t32[:, :IDX_DIM]

    kb_p = 128
    vt = ob[:lp, n_q + n_kv:n_q + 2 * n_kv].reshape(lp // kb_p, kb_p, n_kv).transpose(0, 2, 1)
    wi_t = t32[:lp, IDX_DIM:IDX_DIM + IDX_HEADS].T
    o_p = dsa_t(ob, wi_t, vt, nq=lp, tq=256, kb=kb_p, qb=128, pb=256, cb=1024, topk=min(TOPK_MAX, lp // 4))
    o = dsa_sample(ob, t32, cache_c_k[0].reshape(nb, past, n_kv), cache_c_v[0].reshape(nb, past, n_kv), cache_c_idx[0],
                   o_p, row0=lp, kb=3 * LANES, topk=min(TOPK_MAX, (past + ds) // 4))
    y = mm_res([o], [bf(w_out_odd[0])], y, tm, d)

    wr = jnp.pad(moe_router[0], ((0, 0), (0, LANES - N_EXPERTS)))
    h, meta = router(y, norm_ffn[1], wr, tm)
    sup = 1536
    pos_rows, row_token, tile_expert, tile_rows = route_plan(meta, sup)
    n_rows = row_token.shape[0]
    slab = (d // LANES, LANES)
    xs = gather_rows(h.reshape((m,) + slab), row_token, sup)
    ys = grouped_ffn(xs.reshape(n_rows, d), tile_expert, tile_rows,
                     moe_w_gate[0], moe_w_up[0], moe_w_down[0], sup, 256, 512)
    y_p, y_s = combine_final(y, meta, norm_final, ys.reshape((n_rows,) + slab), pos_rows, lp, 256)

    y_prompt = y_p.reshape(1, lp, d)
    y_sample = y_s.reshape(nb, ds, d)
    keep = min(A_BAND, lp)
    heads = lambda t, n: t.reshape(1, n, -1, A_HEADS, A_HEAD_DIM)
    a_k_prompt = heads(k0[lp - keep:lp], 1)
    a_v_prompt = heads(v0[lp - keep:lp], 1)
    pool_prompt_out = u0[lp - B_HIST:lp].reshape(1, 1, B_HIST, B_WIDTH)
    c_k_prompt = k1[:lp].reshape(1, 1, lp, C_KV_HEADS, C_HEAD_DIM)
    c_v_prompt = v1[:lp].reshape(1, 1, lp, C_KV_HEADS, C_HEAD_DIM)
    c_idx_prompt = ki1[:lp].reshape(1, 1, lp, IDX_DIM)
    shift = lambda cache, new: jnp.concatenate(
        [cache[0], new.reshape(nb, ds, A_HEADS, A_HEAD_DIM)], axis=1)[:, ds:][None]
    a_k_sample = shift(cache_a_k, k0[lp:])
    a_v_sample = shift(cache_a_v, v0[lp:])
    pool_sample_out = u_hist[:, ds:][None]
    c_k_sample = k1[lp:].reshape(1, nb, ds, C_KV_HEADS, C_HEAD_DIM)
    c_v_sample = v1[lp:].reshape(1, nb, ds, C_KV_HEADS, C_HEAD_DIM)
    c_idx_sample = ki1[lp:].reshape(1, nb, ds, IDX_DIM)
    return (y_prompt, y_sample, a_k_prompt, a_v_prompt, pool_prompt_out,
            c_k_prompt, c_v_prompt, c_idx_prompt,
            a_k_sample, a_v_sample, pool_sample_out,
            c_k_sample, c_v_sample, c_idx_sample)
```

```python
import functools
import math

import jax
import jax.numpy as jnp
from jax import lax
from jax.experimental import pallas as pl
from jax.experimental.pallas import tpu as pltpu

F32 = jnp.float32
BF16 = jnp.bfloat16

NORM_EPS = 1e-6
NEG = -1e30
BIG = 1e30
SHIFT_SLACK = 1.01
MIN_SOFTMAX_SUM = 2.0 ** -80

CHUNK = 64
A_HEADS = 16
A_HEAD_DIM = 64
A_WIDTH = A_HEADS * A_HEAD_DIM
A_PREV_CHUNKS = 8
A_BAND = (A_PREV_CHUNKS + 1) * CHUNK
A_REL_CLIP = 128
B_WINDOWS = (2, 4, 8, 16)
B_GROUP = 256
B_WIDTH = B_GROUP * len(B_WINDOWS)
B_HIST = max(B_WINDOWS) - 1
C_HEADS = 16
C_KV_HEADS = 4
C_HEAD_DIM = 128
C_GROUPS = C_HEADS // C_KV_HEADS
IDX_HEADS = 8
IDX_DIM = 64
TOPK_MAX = 256
ROPE_THETA = 500000.0
ROPE_FRAC = 4
N_EXPERTS = 8

LANES = 128
POOL_HALO = 16
BAND_CHUNKS_PER_TRIP = 2
VMEM_LIMIT = 56 * 1024 * 1024

_NT = (((1,), (1,)), ((), ()))


def _params(*sem):
    return pltpu.CompilerParams(dimension_semantics=sem, vmem_limit_bytes=VMEM_LIMIT)


def _rms(x, g):
    ms = jnp.mean(x * x, axis=-1, keepdims=True)
    return x * lax.rsqrt(ms + NORM_EPS) * g


def _norm_proj_kernel(x_ref, g_ref, w_ref, o_ref, h_ref):
    @pl.when(pl.program_id(1) == 0)
    def _():
        h_ref[...] = _rms(x_ref[...], g_ref[...]).astype(BF16)

    o_ref[...] = jnp.dot(h_ref[...], w_ref[...], preferred_element_type=F32)


def norm_proj(x, g, w, tm, tn):
    m, d = x.shape
    n = w.shape[1]
    return pl.pallas_call(
        _norm_proj_kernel,
        grid=(m // tm, n // tn),
        in_specs=[pl.BlockSpec((tm, d), lambda i, j: (i, 0)),
                  pl.BlockSpec((1, d), lambda i, j: (0, 0)),
                  pl.BlockSpec((d, tn), lambda i, j: (0, j))],
        out_specs=pl.BlockSpec((tm, tn), lambda i, j: (i, j)),
        out_shape=jax.ShapeDtypeStruct((m, n), F32),
        scratch_shapes=[pltpu.VMEM((tm, d), BF16)],
        compiler_params=_params("parallel", "arbitrary"),
    )(x, g.reshape(1, d), w)


def _stack_norm_proj_kernel(xp_ref, xs_ref, g_ref, w_ref, o_ref, x_ref, h_ref, *, prompt_tiles):
    i = pl.program_id(0)

    def first_column(src_ref):
        x = src_ref[...]
        x_ref[...] = x
        h_ref[...] = _rms(x, g_ref[...]).astype(BF16)

    @pl.when(jnp.logical_and(pl.program_id(1) == 0, i < prompt_tiles))
    def _():
        first_column(xp_ref)

    @pl.when(jnp.logical_and(pl.program_id(1) == 0, i >= prompt_tiles))
    def _():
        first_column(xs_ref)

    o_ref[...] = jnp.dot(h_ref[...], w_ref[...], preferred_element_type=F32)


def stack_norm_proj(xp, xs, g, w, tm, tn):
    lp, d = xp.shape
    m = lp + xs.shape[0]
    n = w.shape[1]
    prompt_tiles = lp // tm
    return pl.pallas_call(
        functools.partial(_stack_norm_proj_kernel, prompt_tiles=prompt_tiles),
        grid=(m // tm, n // tn),
        in_specs=[pl.BlockSpec((tm, d), lambda i, j: (jnp.minimum(i, prompt_tiles - 1), 0)),
                  pl.BlockSpec((tm, d), lambda i, j: (jnp.maximum(i - prompt_tiles, 0), 0)),
                  pl.BlockSpec((1, d), lambda i, j: (0, 0)),
                  pl.BlockSpec((d, tn), lambda i, j: (0, j))],
        out_specs=[pl.BlockSpec((tm, tn), lambda i, j: (i, j)),
                   pl.BlockSpec((tm, d), lambda i, j: (i, 0))],
        out_shape=[jax.ShapeDtypeStruct((m, n), F32), jax.ShapeDtypeStruct((m, d), F32)],
        scratch_shapes=[pltpu.VMEM((tm, d), BF16)],
        compiler_params=_params("parallel", "arbitrary"),
    )(xp, xs, g.reshape(1, d), w)


def _band_pairs(q_ref, nq, kw_ref, vw_ref, nk, bias_ref, windows, o_ref):
    lane = lax.broadcasted_iota(jnp.int32, (1, LANES), 1)
    first = lane < A_HEAD_DIM
    for hp in range(A_HEADS // 2):
        cs = slice(hp * LANES, (hp + 1) * LANES)
        for q_row0, k_row0, key_ok in windows:
            qp = q_ref[pl.ds(q_row0, nq), cs]
            kp = kw_ref[pl.ds(k_row0, nk), cs]
            vp = vw_ref[pl.ds(k_row0, nk), cs]
            qm = jnp.concatenate([jnp.where(first, qp, 0.0), jnp.where(first, 0.0, qp)], axis=0).astype(BF16)
            s = lax.dot_general(qm, kp, _NT, preferred_element_type=F32)
            s = s * (A_HEAD_DIM ** -0.5) + bias_ref[hp]
            if key_ok is not None:
                s = jnp.where(key_ok, s, NEG)
            mx = jnp.max(s, axis=-1, keepdims=True)
            e = jnp.exp(s - mx)
            l = jnp.sum(e, axis=-1, keepdims=True)
            o = jnp.dot(e.astype(BF16), vp, preferred_element_type=F32) / l
            o_ref[pl.ds(q_row0, nq), cs] = jnp.where(first, o[:nq], o[nq:]).astype(o_ref.dtype)


def _band_prompt_kernel(q_ref, kp_ref, kc_ref, vp_ref, vc_ref, bias_ref, o_ref, kw_ref, vw_ref,
                        *, qb, pad, prompt_blocks):
    i = pl.program_id(0)

    @pl.when(i >= prompt_blocks)
    def _():
        o_ref[...] = jnp.zeros(o_ref.shape, o_ref.dtype)

    @pl.when(i < prompt_blocks)
    def _():
        _band_prompt_block(i, q_ref, kp_ref, kc_ref, vp_ref, vc_ref, bias_ref, o_ref, kw_ref, vw_ref, qb, pad)


def _band_prompt_block(i, q_ref, kp_ref, kc_ref, vp_ref, vc_ref, bias_ref, o_ref, kw_ref, vw_ref, qb, pad):
    kw_ref[0:pad, :] = kp_ref[...].astype(BF16)
    kw_ref[pad:pad + qb, :] = kc_ref[...].astype(BF16)
    vw_ref[0:pad, :] = vp_ref[...].astype(BF16)
    vw_ref[pad:pad + qb, :] = vc_ref[...].astype(BF16)
    col = lax.broadcasted_iota(jnp.int32, (1, A_BAND), 1)

    def chunks(c2, carry):
        windows = []
        for u in range(BAND_CHUNKS_PER_TRIP):
            cc = c2 * BAND_CHUNKS_PER_TRIP + u
            r0 = pl.multiple_of(cc * CHUNK, CHUNK)
            first_valid = jnp.where(i == 0, pad - cc * CHUNK, 0)
            windows.append((r0, r0, col >= first_valid))
        _band_pairs(q_ref, CHUNK, kw_ref, vw_ref, A_BAND, bias_ref, windows, o_ref)
        return carry

    lax.fori_loop(0, qb // CHUNK // BAND_CHUNKS_PER_TRIP, chunks, 0)


def band_prompt(proj, bias, lp):
    pad = A_PREV_CHUNKS * CHUNK
    qb = pad
    blk = (qb, A_WIDTH)
    prev = lambda c: (lambda i: (jnp.maximum(i - 1, 0), c))
    cur = lambda c: (lambda i: (i, c))
    return pl.pallas_call(
        functools.partial(_band_prompt_kernel, qb=qb, pad=pad, prompt_blocks=lp // qb),
        grid=(proj.shape[0] // qb,),
        in_specs=[pl.BlockSpec(blk, cur(0)),
                  pl.BlockSpec(blk, prev(1)), pl.BlockSpec(blk, cur(1)),
                  pl.BlockSpec(blk, prev(2)), pl.BlockSpec(blk, cur(2)),
                  pl.BlockSpec((A_HEADS // 2, 2 * CHUNK, A_BAND), lambda i: (0, 0, 0))],
        out_specs=pl.BlockSpec(blk, lambda i: (i, 0)),
        out_shape=jax.ShapeDtypeStruct((proj.shape[0], A_WIDTH), BF16),
        scratch_shapes=[pltpu.VMEM((pad + qb, A_WIDTH), BF16), pltpu.VMEM((pad + qb, A_WIDTH), BF16)],
        compiler_params=_params("parallel"),
    )(proj, proj, proj, proj, proj, bias)


def _band_sample_kernel(q_ref, kn_ref, vn_ref, ck_ref, cv_ref, bias_ref, dst_ref, o_ref, kw_ref, vw_ref,
                        *, a_len, ds):
    del dst_ref
    kw_ref[0:a_len, :] = ck_ref[...].astype(BF16)
    kw_ref[a_len:a_len + ds, :] = kn_ref[...].astype(BF16)
    vw_ref[0:a_len, :] = cv_ref[...].astype(BF16)
    vw_ref[a_len:a_len + ds, :] = vn_ref[...].astype(BF16)
    _band_pairs(q_ref, ds, kw_ref, vw_ref, a_len + ds, bias_ref, [(0, 0, None)], o_ref)


def band_sample(proj, cache_k, cache_v, bias, dst, lp, nb, ds):
    a_len = cache_k.shape[1]
    row = lambda c: (lambda b: (lp // ds + b, c))
    return pl.pallas_call(
        functools.partial(_band_sample_kernel, a_len=a_len, ds=ds),
        grid=(nb,),
        in_specs=[pl.BlockSpec((ds, A_WIDTH), row(0)),
                  pl.BlockSpec((ds, A_WIDTH), row(1)),
                  pl.BlockSpec((ds, A_WIDTH), row(2)),
                  pl.BlockSpec((None, a_len, A_WIDTH), lambda b: (b, 0, 0)),
                  pl.BlockSpec((None, a_len, A_WIDTH), lambda b: (b, 0, 0)),
                  pl.BlockSpec((A_HEADS // 2, 2 * ds, a_len + ds), lambda b: (0, 0, 0)),
                  pl.BlockSpec(memory_space=pl.ANY)],
        out_specs=pl.BlockSpec((ds, A_WIDTH), row(0)),
        out_shape=jax.ShapeDtypeStruct(dst.shape, dst.dtype),
        input_output_aliases={6: 0},
        scratch_shapes=[pltpu.VMEM((a_len + ds, A_WIDTH), BF16), pltpu.VMEM((a_len + ds, A_WIDTH), BF16)],
        compiler_params=_params("parallel"),
    )(proj, proj, proj, cache_k, cache_v, bias, dst)


def _rel_bias_tile(rel_bias, q0, nq, k0, nk):
    rel_max = q0 - k0 + nq - 1
    rel = jnp.clip(rel_max - jnp.arange(nq + nk - 1), -A_REL_CLIP, A_REL_CLIP) + A_REL_CLIP
    ext = rel_bias[:, rel].astype(F32)
    return jnp.stack([ext[:, nq - 1 - i:nq - 1 - i + nk] for i in range(nq)], axis=1)


def _pair_rows(bias):
    h, nq, nk = bias.shape
    return bias.reshape(h // 2, 2 * nq, nk)


def _pool_kernel(prev_ref, cur_ref, w_ref, sc_ref, *refs, tm, prompt, pos0, prompt_blocks=None):
    o_ref, ext_ref = refs[-2:]
    i = pl.program_id(0)
    if prompt_blocks is None:
        _pool_block(i, prev_ref, cur_ref, w_ref, sc_ref, o_ref, ext_ref, tm, prompt, pos0)
        return

    @pl.when(i >= prompt_blocks)
    def _():
        o_ref[...] = jnp.zeros(o_ref.shape, o_ref.dtype)

    @pl.when(i < prompt_blocks)
    def _():
        _pool_block(i, prev_ref, cur_ref, w_ref, sc_ref, o_ref, ext_ref, tm, prompt, pos0)


def _pool_block(i, prev_ref, cur_ref, w_ref, sc_ref, o_ref, ext_ref, tm, prompt, pos0):
    prev = prev_ref[...]
    if prompt:
        prev = jnp.where(i == 0, 0.0, prev)
        pos = i * tm + lax.broadcasted_iota(jnp.int32, (tm, 1), 0)
    else:
        pos = pos0 + lax.broadcasted_iota(jnp.int32, (tm, 1), 0)
    ext_ref[0:POOL_HALO, :] = prev
    ext_ref[POOL_HALO:POOL_HALO + tm, :] = cur_ref[...]
    for g, w in enumerate(B_WINDOWS):
        cs = slice(g * B_GROUP, (g + 1) * B_GROUP)
        tok = ext_ref[POOL_HALO:POOL_HALO + tm, cs]
        tot = tok
        for j in range(1, w):
            tot = tot + ext_ref[POOL_HALO - j:POOL_HALO - j + tm, cs]
        cnt = jnp.minimum(pos + 1, w).astype(F32)
        pooled = (tot / cnt - tok).astype(BF16)
        o = jnp.dot(pooled, w_ref[g], preferred_element_type=F32) * sc_ref[:, cs]
        o_ref[:, cs] = o.astype(o_ref.dtype)


def pool_prompt(proj, pool_w, pool_scale, lp, tm):
    ucol = 3 * A_WIDTH // B_WIDTH
    per = tm // POOL_HALO
    return pl.pallas_call(
        functools.partial(_pool_kernel, tm=tm, prompt=True, pos0=0, prompt_blocks=lp // tm),
        grid=(proj.shape[0] // tm,),
        in_specs=[pl.BlockSpec((POOL_HALO, B_WIDTH), lambda i: (jnp.maximum(i * per - 1, 0), ucol)),
                  pl.BlockSpec((tm, B_WIDTH), lambda i: (i, ucol)),
                  pl.BlockSpec((len(B_WINDOWS), B_GROUP, B_GROUP), lambda i: (0, 0, 0)),
                  pl.BlockSpec((1, B_WIDTH), lambda i: (0, 0))],
        out_specs=pl.BlockSpec((tm, B_WIDTH), lambda i: (i, 0)),
        out_shape=jax.ShapeDtypeStruct((proj.shape[0], B_WIDTH), BF16),
        scratch_shapes=[pltpu.VMEM((POOL_HALO + tm, B_WIDTH), F32)],
        compiler_params=_params("parallel"),
    )(proj, proj, pool_w, pool_scale)


def pool_sample(u_ext, pool_w, pool_scale, dst, row0, past):
    nb, tot, _ = u_ext.shape
    ds = tot - POOL_HALO
    return pl.pallas_call(
        functools.partial(_pool_kernel, tm=ds, prompt=False, pos0=past),
        grid=(nb,),
        in_specs=[pl.BlockSpec((None, POOL_HALO, B_WIDTH), lambda b: (b, 0, 0)),
                  pl.BlockSpec((None, ds, B_WIDTH), lambda b: (b, POOL_HALO // ds, 0)),
                  pl.BlockSpec((len(B_WINDOWS), B_GROUP, B_GROUP), lambda b: (0, 0, 0)),
                  pl.BlockSpec((1, B_WIDTH), lambda b: (0, 0)),
                  pl.BlockSpec(memory_space=pl.ANY)],
        out_specs=pl.BlockSpec((ds, B_WIDTH), lambda b: (row0 // ds + b, 0)),
        out_shape=jax.ShapeDtypeStruct(dst.shape, dst.dtype),
        input_output_aliases={4: 0},
        scratch_shapes=[pltpu.VMEM((POOL_HALO + ds, B_WIDTH), F32)],
        compiler_params=_params("parallel"),
    )(u_ext, u_ext, pool_w, pool_scale, dst)


def _mm_res_kernel(*refs, n_in):
    xs, ws = refs[:n_in], refs[n_in:2 * n_in]
    res_ref, o_ref = refs[2 * n_in], refs[2 * n_in + 1]
    acc = res_ref[...]
    for x_ref, w_ref in zip(xs, ws):
        acc = acc + jnp.dot(x_ref[...], w_ref[...], preferred_element_type=F32)
    o_ref[...] = acc


def mm_res(xs, ws, res, tm, tn):
    m, n = res.shape
    n_in = len(xs)
    in_specs = ([pl.BlockSpec((tm, x.shape[1]), lambda i, j: (i, 0)) for x in xs]
                + [pl.BlockSpec((w.shape[0], tn), lambda i, j: (0, j)) for w in ws]
                + [pl.BlockSpec((tm, tn), lambda i, j: (i, j))])
    return pl.pallas_call(
        functools.partial(_mm_res_kernel, n_in=n_in),
        grid=(m // tm, n // tn),
        in_specs=in_specs,
        out_specs=pl.BlockSpec((tm, tn), lambda i, j: (i, j)),
        out_shape=jax.ShapeDtypeStruct((m, n), F32),
        compiler_params=_params("parallel", "arbitrary"),
    )(*xs, *ws, res)


def _swiglu_tile(h, wg, wu, wd):
    a = jnp.dot(h, wg, preferred_element_type=F32)
    b = jnp.dot(h, wu, preferred_element_type=F32)
    act = (a * jax.nn.sigmoid(a) * b).astype(BF16)
    return jnp.dot(act, wd, preferred_element_type=F32)


def _ffn_kernel(y_ref, g_ref, wg_ref, wu_ref, wd_ref, o_ref, h_ref, acc_ref):
    f = pl.program_id(1)

    @pl.when(f == 0)
    def _():
        h_ref[...] = _rms(y_ref[...], g_ref[...]).astype(BF16)
        acc_ref[...] = jnp.zeros_like(acc_ref)

    acc_ref[...] += _swiglu_tile(h_ref[...], wg_ref[...], wu_ref[...], wd_ref[...])

    @pl.when(f == pl.num_programs(1) - 1)
    def _():
        o_ref[...] = y_ref[...] + acc_ref[...]


def ffn(y, g, wg, wu, wd, tm, tf):
    m, d = y.shape
    ff = wg.shape[1]
    return pl.pallas_call(
        _ffn_kernel,
        grid=(m // tm, ff // tf),
        in_specs=[pl.BlockSpec((tm, d), lambda i, f: (i, 0)),
                  pl.BlockSpec((1, d), lambda i, f: (0, 0)),
                  pl.BlockSpec((d, tf), lambda i, f: (0, f)),
                  pl.BlockSpec((d, tf), lambda i, f: (0, f)),
                  pl.BlockSpec((tf, d), lambda i, f: (f, 0))],
        out_specs=pl.BlockSpec((tm, d), lambda i, f: (i, 0)),
        out_shape=jax.ShapeDtypeStruct((m, d), F32),
        scratch_shapes=[pltpu.VMEM((tm, d), BF16), pltpu.VMEM((tm, d), F32)],
        compiler_params=_params("parallel", "arbitrary"),
    )(y, g.reshape(1, d), wg, wu, wd)


def _rope_tables(pos, head_dim):
    rot = head_dim // ROPE_FRAC
    half = rot // 2
    inv = jnp.exp(-math.log(ROPE_THETA) * jnp.arange(half, dtype=F32) * (2.0 / rot))
    ang = pos.astype(F32)[:, None] * inv[None, :]
    cos, sin = jnp.cos(ang), jnp.sin(ang)
    m = pos.shape[0]
    one = jnp.ones((m, head_dim - rot), F32)
    zero_r = jnp.zeros((m, head_dim - rot), F32)
    zero_h = jnp.zeros((m, half), F32)
    c = jnp.concatenate([cos, cos, one], axis=1)
    s_dn = jnp.concatenate([-sin, zero_h, zero_r], axis=1)
    s_up = jnp.concatenate([zero_h, sin, zero_r], axis=1)
    rep = LANES // head_dim
    return jnp.stack([jnp.tile(c, (1, rep)), jnp.tile(s_dn, (1, rep)), jnp.tile(s_up, (1, rep))])


def _rot(x, tab_ref, half):
    return (x * tab_ref[0] + pltpu.roll(x, LANES - half, 1) * tab_ref[1]
            + pltpu.roll(x, half, 1) * tab_ref[2])


def _rope_kernel(main_ref, tail_ref, tq_ref, ti_ref, kv_ref, t32_ref, ob_ref,
                 *, n_q, n_k, n_v, n_qi, wi_scale, q_scale):
    half_qk = C_HEAD_DIM // ROPE_FRAC // 2
    half_i = IDX_DIM // ROPE_FRAC // 2
    for c in range(n_q + n_k + n_v + n_qi):
        cs = slice(c * LANES, (c + 1) * LANES)
        x = main_ref[:, cs]
        if c < n_q + n_k:
            x = _rot(x, tq_ref, half_qk)
        elif c >= n_q + n_k + n_v:
            x = _rot(x, ti_ref, half_i)
        if n_q <= c < n_q + n_k + n_v:
            kv_ref[:, (c - n_q) * LANES:(c - n_q + 1) * LANES] = x
        ob_ref[:, cs] = (x * q_scale if c < n_q else x).astype(BF16)
    t = tail_ref[...]
    lane = lax.broadcasted_iota(jnp.int32, (1, LANES), 1)
    r = _rot(t, ti_ref, half_i)
    t32_ref[...] = jnp.where(lane < IDX_DIM, r, t * wi_scale)
    c = n_q + n_k + n_v + n_qi
    ob_ref[:, c * LANES:(c + 1) * LANES] = jnp.where(lane < IDX_DIM, r, pltpu.roll(r, IDX_DIM, 1)).astype(BF16)


def rope_all(main, tail, tab_qk, tab_idx, tm):
    m, nmain = main.shape
    n_q = C_HEADS * C_HEAD_DIM // LANES
    n_k = C_KV_HEADS * C_HEAD_DIM // LANES
    n_qi = IDX_HEADS * IDX_DIM // LANES
    wi_scale = (IDX_HEADS ** -0.5) * (IDX_DIM ** -0.5)
    q_scale = (C_HEAD_DIM ** -0.5) * math.log2(math.e)
    row = lambda i: (i, 0)
    return pl.pallas_call(
        functools.partial(_rope_kernel, n_q=n_q, n_k=n_k, n_v=n_k, n_qi=n_qi, wi_scale=wi_scale, q_scale=q_scale),
        grid=(m // tm,),
        in_specs=[pl.BlockSpec((tm, nmain), row),
                  pl.BlockSpec((tm, LANES), row),
                  pl.BlockSpec((3, tm, LANES), lambda i: (0, i, 0)),
                  pl.BlockSpec((3, tm, LANES), lambda i: (0, i, 0))],
        out_specs=[pl.BlockSpec((tm, 2 * n_k * LANES), row),
                   pl.BlockSpec((tm, LANES), row),
                   pl.BlockSpec((tm, nmain + LANES), row)],
        out_shape=[jax.ShapeDtypeStruct((m, 2 * n_k * LANES), F32),
                   jax.ShapeDtypeStruct((m, LANES), F32),
                   jax.ShapeDtypeStruct((m, nmain + LANES), BF16)],
        compiler_params=_params("parallel"),
    )(main, tail, tab_qk, tab_idx)


def _dsa_kernel(q_ref, qi_ref, wi_ref, kn_ref, vn_ref, kin_ref, *refs, tq, kb, past, n_valid, q_pos0, topk):
    ck_refs, cv_refs = refs[:C_KV_HEADS], refs[C_KV_HEADS:2 * C_KV_HEADS]
    ci_ref, o_ref, k_ref, v_ref, ki_ref, sc_ref, qs_ref, acc_ref, m_ref, l_ref = refs[2 * C_KV_HEADS:]
    i = 0
    rows = C_GROUPS * tq
    sub = kb // LANES
    kf = float(topk)

    new = past + tq
    ci = ci_ref[...].astype(BF16)
    for g in range(C_KV_HEADS):
        gs = slice(g * C_HEAD_DIM, (g + 1) * C_HEAD_DIM)
        k_ref[0:past, gs] = ck_refs[g][...].astype(BF16)
        v_ref[0:past, gs] = cv_refs[g][...].astype(BF16)
    ki_ref[0:past, :] = jnp.concatenate([ci, ci], axis=1)
    k_ref[past:new, :] = kn_ref[...]
    v_ref[past:new, :] = vn_ref[...]
    ki_ref[past:new, :] = kin_ref[...]
    n_pad = k_ref.shape[0] - new
    k_ref[new:, :] = jnp.zeros((n_pad, k_ref.shape[1]), BF16)
    v_ref[new:, :] = jnp.zeros((n_pad, v_ref.shape[1]), BF16)
    ki_ref[new:, :] = jnp.zeros((n_pad, ki_ref.shape[1]), BF16)

    q_pos = q_pos0 + i * tq + lax.broadcasted_iota(jnp.int32, (tq, 1), 0)
    q_chunk = q_pos // CHUNK
    last_chunk = (q_pos0 + i * tq + tq - 1) // CHUNK
    kv_limit = min(n_valid, (last_chunk + 1) * CHUNK)
    nkb = (kv_limit + kb - 1) // kb

    lane = lax.broadcasted_iota(jnp.int32, (1, LANES), 1)
    first = lane < IDX_DIM
    wi = wi_ref[:, IDX_DIM:IDX_DIM + IDX_HEADS]

    def score_block(b, carry):
        for c in range(sub):
            off = pl.multiple_of(b * kb + c * LANES, LANES)
            kib = ki_ref[pl.ds(off, LANES), :]
            acc = jnp.zeros((tq, LANES), F32)
            for hp in range(IDX_HEADS // 2):
                qp = qi_ref[:, hp * LANES:(hp + 1) * LANES]
                for half in range(2):
                    keep = first if half == 0 else jnp.logical_not(first)
                    qm = jnp.where(keep, qp, jnp.zeros_like(qp))
                    d = lax.dot_general(qm, kib, _NT, preferred_element_type=F32)
                    h = hp * 2 + half
                    acc = acc + jnp.maximum(d, 0.0) * wi[:, h:h + 1]
            k_pos = off + lane
            adm = jnp.logical_and(k_pos // CHUNK <= q_chunk, k_pos < n_valid)
            sc_ref[b, :, c * LANES:(c + 1) * LANES] = jnp.where(adm, acc, NEG)
        return carry

    lax.fori_loop(0, nkb, score_block, 0)

    def lane_sum(x):
        return jnp.sum(x, axis=1, keepdims=True)

    def count_ge(t):
        def body(b, acc):
            for c in range(sub):
                blk = sc_ref[b, :, c * LANES:(c + 1) * LANES]
                acc = acc + jnp.where(blk >= t, 1.0, 0.0)
            return acc
        return lane_sum(lax.fori_loop(0, nkb, body, jnp.zeros((tq, LANES), F32)))

    def stats(b, carry):
        mx, mn, cnt = carry
        for c in range(sub):
            blk = sc_ref[b, :, c * LANES:(c + 1) * LANES]
            ok = blk > 0.5 * NEG
            mx = jnp.maximum(mx, blk)
            mn = jnp.minimum(mn, jnp.where(ok, blk, BIG))
            cnt = cnt + jnp.where(ok, 1.0, 0.0)
        return mx, mn, cnt

    mx, mn, cnt = lax.fori_loop(
        0, nkb, stats,
        (jnp.full((tq, LANES), NEG, F32), jnp.full((tq, LANES), BIG, F32), jnp.zeros((tq, LANES), F32)))
    row_max = jnp.max(mx, axis=1, keepdims=True)
    row_min = jnp.min(mn, axis=1, keepdims=True)
    n_adm = lane_sum(cnt)

    done0 = jnp.where(n_adm <= kf, 1.0, 0.0)
    state0 = (row_min, jnp.full((tq, 1), BIG, F32), row_max, jnp.full((tq, 1), 0.5 * NEG, F32), done0)

    def bisect(state, n_steps):
        def cond(c):
            it, st = c
            return jnp.logical_and(it < n_steps, jnp.min(st[4]) < 0.5)

        def body(c):
            it, (lo, hi, mid, thr, done) = c
            cnt = count_ge(mid)
            live = done < 0.5
            hit = jnp.logical_and(live, cnt == kf)
            ge = cnt >= kf
            thr = jnp.where(hit, mid, thr)
            done = jnp.where(hit, 1.0, done)
            lo = jnp.where(ge, mid, lo)
            hi = jnp.where(ge, hi, mid)
            return it + 1, (lo, hi, 0.5 * (lo + hi), thr, done)

        return lax.while_loop(cond, body, (jnp.int32(0), state))[1]

    def snap(state):
        lo, hi, mid, thr, done = state

        def body(b, carry):
            v_lo, v_hi = carry
            for c in range(sub):
                blk = sc_ref[b, :, c * LANES:(c + 1) * LANES]
                v_lo = jnp.minimum(v_lo, jnp.where(blk >= lo, blk, BIG))
                v_hi = jnp.maximum(v_hi, jnp.where(blk < hi, blk, NEG))
            return v_lo, v_hi

        v_lo, v_hi = lax.fori_loop(0, nkb, body,
                                   (jnp.full((tq, LANES), BIG, F32), jnp.full((tq, LANES), NEG, F32)))
        v_lo = jnp.min(v_lo, axis=1, keepdims=True)
        v_hi = jnp.max(v_hi, axis=1, keepdims=True)
        live = done < 0.5
        tie = jnp.logical_and(live, v_lo == v_hi)
        thr = jnp.where(tie, v_lo, thr)
        done = jnp.where(tie, 1.0, done)
        lo = jnp.where(live, v_lo, lo)
        return (lo, hi, 0.5 * (lo + hi), thr, done), jnp.where(tie, 1.0, 0.0)

    state = bisect(state0, 32)

    def refine_cond(c):
        rounds, st, _ = c
        return jnp.logical_and(rounds < 10, jnp.min(st[4]) < 0.5)

    def refine_body(c):
        rounds, st, tie = c
        st, new_tie = snap(st)
        st = bisect(st, 32)
        return rounds + 1, st, jnp.maximum(tie, new_tie)

    _, state, tie = lax.while_loop(refine_cond, refine_body,
                                   (jnp.int32(0), state, jnp.zeros((tq, 1), F32)))
    thr = state[3]
    any_tie = jnp.max(tie) > 0.5

    @pl.when(jnp.logical_not(any_tie))
    def _():
        def body(b, carry):
            for c in range(sub):
                cs = slice(c * LANES, (c + 1) * LANES)
                sc_ref[b, :, cs] = jnp.where(sc_ref[b, :, cs] >= thr, 0.0, NEG)
            return carry
        lax.fori_loop(0, nkb, body, 0)

    @pl.when(any_tie)
    def _():
        def gt_body(b, acc):
            for c in range(sub):
                acc = acc + jnp.where(sc_ref[b, :, c * LANES:(c + 1) * LANES] > thr, 1.0, 0.0)
            return acc
        need = kf - lane_sum(lax.fori_loop(0, nkb, gt_body, jnp.zeros((tq, LANES), F32)))

        def count_eq_upto(j):
            def body(b, acc):
                for c in range(sub):
                    blk = sc_ref[b, :, c * LANES:(c + 1) * LANES]
                    idx = (b * kb + c * LANES + lane).astype(F32)
                    acc = acc + jnp.where(jnp.logical_and(blk == thr, idx <= j), 1.0, 0.0)
                return acc
            return lane_sum(lax.fori_loop(0, nkb, body, jnp.zeros((tq, LANES), F32)))

        def idx_body(_, c):
            lo_j, hi_j = c
            mid_j = jnp.floor(0.5 * (lo_j + hi_j))
            ok = count_eq_upto(mid_j) >= need
            return jnp.where(ok, lo_j, mid_j), jnp.where(ok, mid_j, hi_j)

        n_steps = max(1, math.ceil(math.log2(sc_ref.shape[0] * kb + 1)))
        _, last = lax.fori_loop(
            0, n_steps, idx_body,
            (jnp.full((tq, 1), -1.0, F32), jnp.full((tq, 1), float(sc_ref.shape[0] * kb), F32)))
        last = jnp.where(tie > 0.5, last, BIG)

        def body(b, carry):
            for c in range(sub):
                cs = slice(c * LANES, (c + 1) * LANES)
                blk = sc_ref[b, :, cs]
                idx = (b * kb + c * LANES + lane).astype(F32)
                sel = jnp.logical_or(blk > thr, jnp.logical_and(blk == thr, idx <= last))
                sc_ref[b, :, cs] = jnp.where(sel, 0.0, NEG)
            return carry
        lax.fori_loop(0, nkb, body, 0)

    for g in range(C_KV_HEADS):
        for hh in range(C_GROUPS):
            h = g * C_GROUPS + hh
            qs_ref[g, hh * tq:(hh + 1) * tq, :] = q_ref[:, h * C_HEAD_DIM:(h + 1) * C_HEAD_DIM]
    m_ref[...] = jnp.full(m_ref.shape, NEG, F32)
    l_ref[...] = jnp.zeros(l_ref.shape, F32)
    acc_ref[...] = jnp.zeros(acc_ref.shape, F32)

    def attend(b, carry):
        off = pl.multiple_of(b * kb, kb)
        bias = sc_ref[b]
        bias = jnp.concatenate([bias] * C_GROUPS, axis=0)
        for g in range(C_KV_HEADS):
            cs = slice(g * C_HEAD_DIM, (g + 1) * C_HEAD_DIM)
            kblk = k_ref[pl.ds(off, kb), cs]
            vblk = v_ref[pl.ds(off, kb), cs]
            s = lax.dot_general(qs_ref[g], kblk, _NT, preferred_element_type=F32)
            s = s + bias
            m_old = m_ref[g]
            m_new = jnp.maximum(m_old, jnp.max(s, axis=1, keepdims=True))
            alpha = jnp.exp2(m_old - m_new)
            p = jnp.exp2(s - m_new)
            l_ref[g] = alpha * l_ref[g] + jnp.sum(p, axis=1, keepdims=True)
            acc_ref[g] = alpha * acc_ref[g] + jnp.dot(p.astype(BF16), vblk, preferred_element_type=F32)
            m_ref[g] = m_new
        return carry

    lax.fori_loop(0, nkb, attend, 0, unroll=True)

    for g in range(C_KV_HEADS):
        o = acc_ref[g] / l_ref[g]
        for hh in range(C_GROUPS):
            h = g * C_GROUPS + hh
            o_ref[:, h * C_HEAD_DIM:(h + 1) * C_HEAD_DIM] = o[hh * tq:(hh + 1) * tq, :].astype(o_ref.dtype)


def dsa_sample(ob, t32, cache_k, cache_v, cache_i, *, row0, kb, topk):
    nb, past = cache_k.shape[:2]
    head = lambda g: pl.BlockSpec((None, past, C_HEAD_DIM), lambda b, i: (b, 0, g))
    heads = [head(g) for g in range(C_KV_HEADS)]
    n_kv = C_KV_HEADS * C_HEAD_DIM
    ds = (ob.shape[0] - row0) // nb
    n_q = C_HEADS * C_HEAD_DIM
    n_qi = IDX_HEADS * IDX_DIM
    s_all = past + ds
    s_pad = -(-s_all // kb) * kb
    rows = C_GROUPS * ds
    col = lambda w, off: (lambda b, i: (row0 // ds + b, off // w))
    cmap = lambda b, i: (b, 0, 0)
    return pl.pallas_call(
        functools.partial(_dsa_kernel, tq=ds, kb=kb, past=past, n_valid=s_all, q_pos0=past, topk=topk),
        grid=(nb, 1),
        in_specs=[pl.BlockSpec((ds, n_q), col(n_q, 0)),
                  pl.BlockSpec((ds, n_qi), col(n_qi, n_q + 2 * n_kv)),
                  pl.BlockSpec((ds, LANES), col(LANES, 0)),
                  pl.BlockSpec((ds, n_kv), col(n_kv, n_q)),
                  pl.BlockSpec((ds, n_kv), col(n_kv, n_q + n_kv)),
                  pl.BlockSpec((ds, LANES), col(LANES, n_q + 2 * n_kv + n_qi))]
                 + heads + heads + [pl.BlockSpec((None, past, IDX_DIM), cmap)],
        out_specs=pl.BlockSpec((ds, n_q), lambda b, i: (b, 0)),
        out_shape=jax.ShapeDtypeStruct((nb * ds, n_q), BF16),
        scratch_shapes=[pltpu.VMEM((s_pad, n_kv), BF16),
                        pltpu.VMEM((s_pad, n_kv), BF16),
                        pltpu.VMEM((s_pad, LANES), BF16),
                        pltpu.VMEM((s_pad // kb, ds, kb), F32),
                        pltpu.VMEM((C_KV_HEADS, rows, C_HEAD_DIM), BF16),
                        pltpu.VMEM((C_KV_HEADS, rows, C_HEAD_DIM), F32),
                        pltpu.VMEM((C_KV_HEADS, rows, 1), F32),
                        pltpu.VMEM((C_KV_HEADS, rows, 1), F32)],
        compiler_params=_params("parallel", "arbitrary"),
    )(ob, ob, t32, ob, ob, ob, *([cache_k] * C_KV_HEADS), *([cache_v] * C_KV_HEADS), cache_i)


def _dsa_t_kernel(q_ref, qi_ref, wi_ref, k_ref, vt_ref, ki_ref, o_ref,
                  sc_ref, qim_ref, acc_ref, m_ref, l_ref, l8_ref, kmax_ref,
                  *, tq, kb, qb, pb, cb, n_valid, q_pos0, topk):
    i = pl.program_id(1)
    kf = float(topk)

    q_pos = q_pos0 + i * tq + lax.broadcasted_iota(jnp.int32, (1, tq), 1)
    q_chunk = q_pos // CHUNK
    last_chunk = (q_pos0 + i * tq + tq - 1) // CHUNK
    kv_limit = jnp.minimum(n_valid, (last_chunk + 1) * CHUNK)
    nkb = (kv_limit + kb - 1) // kb
    ncb = (kv_limit + cb - 1) // cb

    def col_reduce(x, op):
        groups = x.shape[0] // 8
        chains = 8 if groups % 8 == 0 else 1
        return op(op(x.reshape(chains, groups // chains, 8, tq), axis=1), axis=0)

    col_sum = lambda x: col_reduce(x, jnp.sum)
    col_max = lambda x: col_reduce(x, jnp.max)
    col_min = lambda x: col_reduce(x, jnp.min)

    lane = lax.broadcasted_iota(jnp.int32, (1, LANES), 1)
    first = lane < IDX_DIM
    for hp in range(IDX_HEADS // 2):
        qp = qi_ref[:, hp * LANES:(hp + 1) * LANES]
        qim_ref[2 * hp] = jnp.where(first, qp, jnp.zeros_like(qp))
        qim_ref[2 * hp + 1] = jnp.where(first, jnp.zeros_like(qp), qp)
    block_iota = lax.broadcasted_iota(jnp.int32, (pb, 1), 0)

    def score_block(b, carry):
        off = pl.multiple_of(b * pb, pb)
        kib = ki_ref[pl.ds(off, pb), :]
        parts = []
        for h in range(IDX_HEADS):
            d = lax.dot_general(kib, qim_ref[h], _NT, preferred_element_type=F32)
            parts.append(jnp.maximum(d, 0.0) * wi_ref[h:h + 1, :])
        while len(parts) > 1:
            parts = [a + b for a, b in zip(parts[0::2], parts[1::2])]
        acc = parts[0]
        k_pos = off + block_iota
        adm = jnp.logical_and(k_pos // CHUNK <= q_chunk, k_pos < n_valid)
        sc_ref[pl.ds(off, pb), :] = jnp.where(adm, acc, NEG)
        return carry

    lax.fori_loop(0, ncb * (cb // pb), score_block, 0)
    key_iota = lax.broadcasted_iota(jnp.int32, (cb, 1), 0)

    def blocks(body, init):
        def step(b, carry):
            off = pl.multiple_of(b * cb, cb)
            return body(off, sc_ref[pl.ds(off, cb), :], carry)
        return lax.fori_loop(0, ncb, step, init)

    def total(x):
        return jnp.sum(x, axis=0, keepdims=True)

    def count_ge(t):
        return total(blocks(lambda off, blk, acc: acc + col_sum(jnp.where(blk >= t, 1.0, 0.0)),
                            jnp.zeros((8, tq), F32)))

    def stats(off, blk, carry):
        mx, mn, cnt, pos, nonneg = carry
        ok = blk > 0.5 * NEG
        return (jnp.maximum(mx, col_max(blk)), jnp.minimum(mn, col_min(jnp.where(ok, blk, BIG))),
                cnt + col_sum(jnp.where(ok, 1.0, 0.0)),
                pos + col_sum(jnp.where(blk > 0.0, 1.0, 0.0)),
                nonneg + col_sum(jnp.where(blk >= 0.0, 1.0, 0.0)))

    zeros8 = jnp.zeros((8, tq), F32)
    mx, mn, cnt, pos, nonneg = blocks(stats, (jnp.full((8, tq), NEG, F32), jnp.full((8, tq), BIG, F32),
                                              zeros8, zeros8, zeros8))
    row_max = jnp.max(mx, axis=0, keepdims=True)
    row_min = jnp.min(mn, axis=0, keepdims=True)
    n_adm = total(cnt)
    n_pos = total(pos)
    n_nonneg = total(nonneg)

    open_row = n_adm > kf
    above = n_pos >= kf
    below = n_nonneg < kf
    zero_tie = jnp.logical_and(open_row, jnp.logical_not(jnp.logical_or(above, below)))
    lo0 = jnp.where(above, 0.0, row_min)
    hi0 = jnp.where(below, 0.0, BIG)
    mid0 = jnp.where(below, 0.5 * (lo0 + hi0), row_max)
    thr0 = jnp.where(zero_tie, 0.0, 0.5 * NEG)
    done0 = jnp.where(jnp.logical_or(jnp.logical_not(open_row), zero_tie), 1.0, 0.0)
    tie0 = jnp.where(zero_tie, 1.0, 0.0)
    state0 = (lo0, hi0, mid0, thr0, done0)

    def bisect(state, n_steps):
        def cond(c):
            it, st = c
            return jnp.logical_and(it < n_steps, jnp.min(st[4]) < 0.5)

        def body(c):
            it, (lo, hi, mid, thr, done) = c
            cnt = count_ge(mid)
            hit = jnp.logical_and(done < 0.5, cnt == kf)
            ge = cnt >= kf
            thr = jnp.where(hit, mid, thr)
            done = jnp.where(hit, 1.0, done)
            lo = jnp.where(ge, mid, lo)
            hi = jnp.where(ge, hi, mid)
            return it + 1, (lo, hi, 0.5 * (lo + hi), thr, done)

        return lax.while_loop(cond, body, (jnp.int32(0), state))[1]

    def snap(state):
        lo, hi, mid, thr, done = state

        def body(off, blk, carry):
            v_lo, v_hi = carry
            return (jnp.minimum(v_lo, col_min(jnp.where(blk >= lo, blk, BIG))),
                    jnp.maximum(v_hi, col_max(jnp.where(blk < hi, blk, NEG))))

        v_lo, v_hi = blocks(body, (jnp.full((8, tq), BIG, F32), jnp.full((8, tq), NEG, F32)))
        v_lo = jnp.min(v_lo, axis=0, keepdims=True)
        v_hi = jnp.max(v_hi, axis=0, keepdims=True)
        live = done < 0.5
        tie = jnp.logical_and(live, v_lo == v_hi)
        thr = jnp.where(tie, v_lo, thr)
        done = jnp.where(tie, 1.0, done)
        lo = jnp.where(live, v_lo, lo)
        return (lo, hi, 0.5 * (lo + hi), thr, done), jnp.where(tie, 1.0, 0.0)

    state = bisect(state0, 32)

    def refine_cond(c):
        rounds, st, _ = c
        return jnp.logical_and(rounds < 10, jnp.min(st[4]) < 0.5)

    def refine_body(c):
        rounds, st, tie = c
        st, new_tie = snap(st)
        st = bisect(st, 32)
        return rounds + 1, st, jnp.maximum(tie, new_tie)

    _, state, tie = lax.while_loop(refine_cond, refine_body, (jnp.int32(0), state, tie0))
    thr = state[3]
    any_tie = jnp.max(tie) > 0.5

    @pl.when(jnp.logical_not(any_tie))
    def _():
        def body(off, blk, carry):
            sc_ref[pl.ds(off, cb), :] = jnp.where(blk >= thr, 0.0, NEG)
            return carry
        blocks(body, 0)

    @pl.when(any_tie)
    def _():
        need = kf - total(blocks(lambda off, blk, acc: acc + col_sum(jnp.where(blk > thr, 1.0, 0.0)),
                                 jnp.zeros((8, tq), F32)))

        def count_eq_upto(j):
            def body(off, blk, acc):
                idx = (off + key_iota).astype(F32)
                return acc + col_sum(jnp.where(jnp.logical_and(blk == thr, idx <= j), 1.0, 0.0))
            return total(blocks(body, jnp.zeros((8, tq), F32)))

        def idx_cond(c):
            it, (lo_j, hi_j, settled) = c
            return jnp.logical_and(it < n_steps, jnp.min(settled) < 0.5)

        def idx_body(c):
            it, (lo_j, hi_j, settled) = c
            mid_j = jnp.floor(0.5 * (lo_j + hi_j))
            cnt = count_eq_upto(mid_j)
            ok = cnt >= need
            live = settled < 0.5
            lo_j = jnp.where(jnp.logical_and(live, jnp.logical_not(ok)), mid_j, lo_j)
            hi_j = jnp.where(jnp.logical_and(live, ok), mid_j, hi_j)
            settled = jnp.where(jnp.logical_or(cnt == need, hi_j - lo_j <= 1.0), 1.0, settled)
            return it + 1, (lo_j, hi_j, settled)

        n_keys = sc_ref.shape[0]
        n_steps = max(1, math.ceil(math.log2(n_keys + 1))) + 1
        _, (_, last, _) = lax.while_loop(
            idx_cond, idx_body,
            (jnp.int32(0), (jnp.full((1, tq), -1.0, F32), jnp.full((1, tq), float(n_keys), F32),
                            jnp.where(tie > 0.5, 0.0, 1.0))))
        last = jnp.where(tie > 0.5, last, BIG)

        def body(off, blk, carry):
            idx = (off + key_iota).astype(F32)
            sel = jnp.logical_or(blk > thr, jnp.logical_and(blk == thr, idx <= last))
            sc_ref[pl.ds(off, cb), :] = jnp.where(sel, 0.0, NEG)
            return carry
        blocks(body, 0)

    halves = kb // qb

    def head_scores(off, h):
        gs = slice((h // C_GROUPS) * C_HEAD_DIM, (h // C_GROUPS + 1) * C_HEAD_DIM)
        out = []
        for j in range(halves):
            rows = pl.ds(off + j * qb, qb)
            s = lax.dot_general(k_ref[rows, gs], q_ref[:, h * C_HEAD_DIM:(h + 1) * C_HEAD_DIM],
                                _NT, preferred_element_type=F32)
            out.append(s + sc_ref[rows, :])
        return out, gs

    @pl.when(i == 0)
    def _():
        for g in range(C_KV_HEADS):
            gs = slice(g * C_HEAD_DIM, (g + 1) * C_HEAD_DIM)

            def body(b, mx):
                kk = k_ref[pl.ds(pl.multiple_of(b * cb, cb), cb), gs].astype(F32)
                return jnp.maximum(mx, jnp.sum(kk * kk, axis=1, keepdims=True))

            mx = lax.fori_loop(0, k_ref.shape[0] // cb, body, jnp.zeros((cb, 1), F32))
            kmax_ref[g] = jnp.full((1, tq), 1.0, F32) * jnp.max(mx)

    ones = jnp.ones((8, C_HEAD_DIM), BF16)
    for h in range(C_HEADS):
        qh = q_ref[:, h * C_HEAD_DIM:(h + 1) * C_HEAD_DIM].astype(F32)
        qn2 = lax.dot_general(ones, (qh * qh).astype(BF16), _NT, preferred_element_type=F32)[0:1, :]
        m_ref[h] = jnp.sqrt(qn2 * kmax_ref[h // C_GROUPS]) * SHIFT_SLACK
    l8_ref[...] = jnp.zeros(l8_ref.shape, F32)
    acc_ref[...] = jnp.zeros(acc_ref.shape, F32)

    def attend_fixed(b0, n):
        for h in range(C_HEADS):
            lsum, upd = None, None
            for u in range(n):
                b = b0 + u
                ss, gs = head_scores(pl.multiple_of(b * kb, kb), h)
                ps = [jnp.exp2(s - m_ref[h]) for s in ss]
                part = sum(col_sum(p) for p in ps)
                p_all = jnp.concatenate([p.astype(BF16) for p in ps], axis=0)
                pv = jnp.dot(vt_ref[b, gs, :], p_all, preferred_element_type=F32)
                lsum = part if lsum is None else lsum + part
                upd = pv if upd is None else upd + pv
            l8_ref[h] += lsum
            acc_ref[h] += upd

    def four_blocks(t, carry):
        attend_fixed(4 * t, 4)
        return carry

    lax.fori_loop(0, nkb // 4, four_blocks, 0)

    @pl.when(nkb % 4 >= 2)
    def _():
        attend_fixed((nkb // 4) * 4, 2)
    l_min = jnp.full((1, tq), BIG, F32)
    for h in range(C_HEADS):
        l_ref[h] = total(l8_ref[h])
        l_min = jnp.minimum(l_min, l_ref[h])

    @pl.when(jnp.logical_not(jnp.min(l_min) >= MIN_SOFTMAX_SUM))
    def _():
        m_ref[...] = jnp.full(m_ref.shape, NEG, F32)
        l_ref[...] = jnp.zeros(l_ref.shape, F32)
        acc_ref[...] = jnp.zeros(acc_ref.shape, F32)

        def attend(b, carry):
            off = pl.multiple_of(b * kb, kb)
            for h in range(C_HEADS):
                ss, gs = head_scores(off, h)
                m_old = m_ref[h]
                m_new = m_old
                for s in ss:
                    m_new = jnp.maximum(m_new, jnp.max(col_max(s), axis=0, keepdims=True))
                alpha = jnp.exp2(m_old - m_new)
                ps = [jnp.exp2(s - m_new) for s in ss]
                l_ref[h] = alpha * l_ref[h] + total(sum(col_sum(p) for p in ps))
                p_all = jnp.concatenate([p.astype(BF16) for p in ps], axis=0)
                acc_ref[h] = alpha * acc_ref[h] + jnp.dot(vt_ref[b, gs, :], p_all,
                                                          preferred_element_type=F32)
                m_ref[h] = m_new
            return carry

        lax.fori_loop(0, nkb, attend, 0)

    for h in range(C_HEADS):
        o = acc_ref[h] / l_ref[h]
        o_ref[:, h * C_HEAD_DIM:(h + 1) * C_HEAD_DIM] = o.T.astype(o_ref.dtype)


def dsa_t(ob, wi_t, vt, *, nq, tq, kb, qb, pb, cb, topk):
    dq = C_HEADS * C_HEAD_DIM
    n_kv = C_KV_HEADS * C_HEAD_DIM
    n_qi = IDX_HEADS * IDX_DIM
    s_pad = nq
    assert tq % (2 * kb) == 0 and tq % CHUNK == 0 and nq % cb == 0 and cb % pb == 0 and kb % qb == 0
    return pl.pallas_call(
        functools.partial(_dsa_t_kernel, tq=tq, kb=kb, qb=qb, pb=pb, cb=cb, n_valid=nq, q_pos0=0, topk=topk),
        grid=(1, nq // tq),
        in_specs=[pl.BlockSpec((tq, dq), lambda b, i: (i, 0)),
                  pl.BlockSpec((tq, n_qi), lambda b, i: (i, (dq + 2 * n_kv) // n_qi)),
                  pl.BlockSpec((IDX_HEADS, tq), lambda b, i: (0, i)),
                  pl.BlockSpec((nq, n_kv), lambda b, i: (0, dq // n_kv)),
                  pl.BlockSpec(vt.shape, lambda b, i: (0, 0, 0)),
                  pl.BlockSpec((nq, LANES), lambda b, i: (0, (dq + 2 * n_kv + n_qi) // LANES))],
        out_specs=pl.BlockSpec((tq, dq), lambda b, i: (i, 0)),
        out_shape=jax.ShapeDtypeStruct((nq, dq), BF16),
        scratch_shapes=[pltpu.VMEM((s_pad, tq), F32),
                        pltpu.VMEM((IDX_HEADS, tq, LANES), BF16),
                        pltpu.VMEM((C_HEADS, C_HEAD_DIM, tq), F32),
                        pltpu.VMEM((C_HEADS, 1, tq), F32),
                        pltpu.VMEM((C_HEADS, 1, tq), F32),
                        pltpu.VMEM((C_HEADS, 8, tq), F32),
                        pltpu.VMEM((C_KV_HEADS, 1, tq), F32)],
        compiler_params=_params("arbitrary", "arbitrary"),
    )(ob, ob, wi_t, ob, vt, ob)


def _router_kernel(y_ref, g_ref, wr_ref, h_ref, gate_ref):
    h = _rms(y_ref[...], g_ref[...])
    h_ref[...] = h.astype(BF16)
    logits = jnp.dot(h, wr_ref[...], preferred_element_type=F32, precision=lax.Precision.HIGHEST)
    lane = lax.broadcasted_iota(jnp.int32, logits.shape, 1)
    lg = jnp.where(lane < N_EXPERTS, logits, NEG)
    m1 = jnp.max(lg, axis=1, keepdims=True)
    i1 = jnp.min(jnp.where(lg == m1, lane, LANES), axis=1, keepdims=True)
    lg2 = jnp.where(lane == i1, NEG, lg)
    m2 = jnp.max(lg2, axis=1, keepdims=True)
    i2 = jnp.min(jnp.where(lg2 == m2, lane, LANES), axis=1, keepdims=True)
    e = jnp.exp(m2 - m1)
    g1 = 1.0 / (1.0 + e)
    g2 = e / (1.0 + e)
    meta = jnp.where(lane == 0, i1.astype(F32), jnp.where(lane == 1, i2.astype(F32),
                     jnp.where(lane == 2, g1, jnp.where(lane == 3, g2, 0.0))))
    gate_ref[...] = meta


def router(y, g, w_router_pad, tm):
    m, d = y.shape
    return pl.pallas_call(
        _router_kernel,
        grid=(m // tm,),
        in_specs=[pl.BlockSpec((tm, d), lambda i: (i, 0)),
                  pl.BlockSpec((1, d), lambda i: (0, 0)),
                  pl.BlockSpec((d, LANES), lambda i: (0, 0))],
        out_specs=[pl.BlockSpec((tm, d), lambda i: (i, 0)),
                   pl.BlockSpec((tm, LANES), lambda i: (i, 0))],
        out_shape=[jax.ShapeDtypeStruct((m, d), BF16), jax.ShapeDtypeStruct((m, LANES), F32)],
        compiler_params=_params("parallel"),
    )(y, g.reshape(1, d), w_router_pad)


def route_plan(meta, sup):
    m = meta.shape[0]
    n_tiles = (2 * m) // sup + N_EXPERTS
    e_all = jnp.concatenate([meta[:, 0], meta[:, 1]]).astype(jnp.int32)
    onehot = (e_all[:, None] == jnp.arange(N_EXPERTS)[None, :]).astype(jnp.int32)
    rank = jnp.sum((jnp.cumsum(onehot, axis=0) - onehot) * onehot, axis=1)
    counts = jnp.sum(onehot, axis=0)
    n_super = (counts + sup - 1) // sup
    super_end = jnp.cumsum(n_super)
    super_start = super_end - n_super
    pos = (super_start * sup)[e_all] + rank
    tiles = jnp.arange(n_tiles)
    used = super_end[-1]
    t_eff = jnp.minimum(tiles, used - 1)
    tile_expert = jnp.minimum(jnp.searchsorted(super_end, t_eff, side="right"), N_EXPERTS - 1).astype(jnp.int32)
    rows = jnp.clip(counts[tile_expert] - (t_eff - super_start[tile_expert]) * sup, 0, sup)
    tile_rows = jnp.where(tiles < used, rows, 0).astype(jnp.int32)
    token = jnp.concatenate([jnp.arange(m), jnp.arange(m)]).astype(jnp.int32)
    row_token = (jnp.arange(n_tiles * sup, dtype=jnp.int32) % m).at[pos].set(token)
    return pos.astype(jnp.int32), row_token, tile_expert, tile_rows


def _gather_rows_kernel(tok_ref, h_ref, o_ref, sem, *, batch):
    base = pl.program_id(0) * batch

    def issue(j, carry):
        pltpu.make_async_copy(h_ref.at[tok_ref[base + j]], o_ref.at[j], sem).start()
        return carry

    lax.fori_loop(0, batch, issue, 0)
    pltpu.make_async_copy(o_ref, o_ref, sem).wait()


def gather_rows(h3, row_token, batch):
    n_rows = row_token.shape[0]
    blk = (batch,) + h3.shape[1:]
    return pl.pallas_call(
        functools.partial(_gather_rows_kernel, batch=batch),
        grid_spec=pltpu.PrefetchScalarGridSpec(
            num_scalar_prefetch=1,
            grid=(n_rows // batch,),
            in_specs=[pl.BlockSpec(memory_space=pl.ANY)],
            out_specs=pl.BlockSpec(blk, lambda i, tok: (i, 0, 0)),
            scratch_shapes=[pltpu.SemaphoreType.DMA(())]),
        out_shape=jax.ShapeDtypeStruct((n_rows,) + h3.shape[1:], h3.dtype),
        compiler_params=_params("arbitrary"),
    )(row_token, h3)


def _grouped_ffn_kernel(te_ref, tr_ref, x_ref, wg_ref, wu_ref, wd_ref, o_ref,
                        wgb_ref, wub_ref, wdb_ref, *, sup, sub):
    t = pl.program_id(0)
    f = pl.program_id(1)
    rows = tr_ref[t]

    @pl.when(rows > 0)
    def _():
        wgb_ref[...] = wg_ref[...].astype(BF16)
        wub_ref[...] = wu_ref[...].astype(BF16)
        wdb_ref[...] = wd_ref[...].astype(BF16)

    for s in range(sup // sub):
        sl = slice(s * sub, (s + 1) * sub)

        @pl.when(jnp.logical_and(s * sub >= rows, f == 0))
        def _():
            o_ref[sl, :] = jnp.zeros((sub, o_ref.shape[1]), F32)

        @pl.when(s * sub < rows)
        def _():
            part = _swiglu_tile(x_ref[sl, :], wgb_ref[...], wub_ref[...], wdb_ref[...])

            @pl.when(f == 0)
            def _():
                o_ref[sl, :] = part

            @pl.when(f > 0)
            def _():
                o_ref[sl, :] += part


def grouped_ffn(xs, tile_expert, tile_rows, wg, wu, wd, sup, sub, tf):
    n_rows, d = xs.shape
    ff = wg.shape[2]
    n_f = ff // tf
    fidx = lambda t, f, te, tr: jnp.where(tr[t] > 0, f, n_f - 1)
    return pl.pallas_call(
        functools.partial(_grouped_ffn_kernel, sup=sup, sub=sub),
        grid_spec=pltpu.PrefetchScalarGridSpec(
            num_scalar_prefetch=2,
            grid=(n_rows // sup, n_f),
            in_specs=[pl.BlockSpec((sup, d), lambda t, f, te, tr: (t, 0), pipeline_mode=pl.Buffered(1)),
                      pl.BlockSpec((None, d, tf), lambda t, f, te, tr: (te[t], 0, fidx(t, f, te, tr))),
                      pl.BlockSpec((None, d, tf), lambda t, f, te, tr: (te[t], 0, fidx(t, f, te, tr))),
                      pl.BlockSpec((None, tf, d), lambda t, f, te, tr: (te[t], fidx(t, f, te, tr), 0))],
            out_specs=pl.BlockSpec((sup, d), lambda t, f, te, tr: (t, 0), pipeline_mode=pl.Buffered(1)),
            scratch_shapes=[pltpu.VMEM((d, tf), BF16), pltpu.VMEM((d, tf), BF16), pltpu.VMEM((tf, d), BF16)]),
        out_shape=jax.ShapeDtypeStruct((n_rows, d), F32),
        compiler_params=_params("arbitrary", "arbitrary"),
    )(tile_expert, tile_rows, xs, wg, wu, wd)


def _combine_kernel(pos_ref, y_ref, meta_ref, g_ref, ys_ref, op_ref, os_ref, buf_ref, x_ref, sem,
                    *, n_tok, tmc, prompt_tiles):
    i = pl.program_id(0)
    slot = i % 2

    def issue(tile, into):
        def body(j, carry):
            t = tile * tmc + j
            pltpu.make_async_copy(ys_ref.at[pos_ref[t]], buf_ref.at[into, 0, j], sem.at[into]).start()
            pltpu.make_async_copy(ys_ref.at[pos_ref[n_tok + t]], buf_ref.at[into, 1, j], sem.at[into]).start()
            return carry
        lax.fori_loop(0, tmc, body, 0)

    @pl.when(i == 0)
    def _():
        issue(0, 0)

    @pl.when(i + 1 < pl.num_programs(0))
    def _():
        issue(i + 1, 1 - slot)

    for e in range(2):
        pltpu.make_async_copy(buf_ref.at[slot, e], buf_ref.at[slot, e], sem.at[slot]).wait()
    g1 = meta_ref[:, 2:3]
    g2 = meta_ref[:, 3:4]
    ss = jnp.zeros((tmc, 1), F32)
    for c in range(buf_ref.shape[3]):
        cs = slice(c * LANES, (c + 1) * LANES)
        x = y_ref[:, cs] + (buf_ref[slot, 0, :, c, :] * g1 + buf_ref[slot, 1, :, c, :] * g2)
        ss = ss + jnp.sum(x * x, axis=1, keepdims=True)
        x_ref[:, cs] = x
    inv = lax.rsqrt(ss / x_ref.shape[1] + NORM_EPS)

    @pl.when(i < prompt_tiles)
    def _():
        op_ref[...] = x_ref[...] * inv * g_ref[...]

    @pl.when(i >= prompt_tiles)
    def _():
        os_ref[...] = x_ref[...] * inv * g_ref[...]


def combine_final(y, meta, g_final, ys3, pos, n_prompt, tmc):
    m, d = y.shape
    slab = ys3.shape[1:]
    prompt_tiles = n_prompt // tmc
    return pl.pallas_call(
        functools.partial(_combine_kernel, n_tok=m, tmc=tmc, prompt_tiles=prompt_tiles),
        grid_spec=pltpu.PrefetchScalarGridSpec(
            num_scalar_prefetch=1,
            grid=(m // tmc,),
            in_specs=[pl.BlockSpec((tmc, d), lambda i, pos: (i, 0)),
                      pl.BlockSpec((tmc, LANES), lambda i, pos: (i, 0)),
                      pl.BlockSpec((1, d), lambda i, pos: (0, 0)),
                      pl.BlockSpec(memory_space=pl.ANY)],
            out_specs=[pl.BlockSpec((tmc, d), lambda i, pos: (jnp.minimum(i, prompt_tiles - 1), 0)),
                       pl.BlockSpec((tmc, d), lambda i, pos: (jnp.maximum(i - prompt_tiles, 0), 0))],
            scratch_shapes=[pltpu.VMEM((2, 2, tmc) + slab, F32), pltpu.VMEM((tmc, d), F32),
                            pltpu.SemaphoreType.DMA((2,))]),
        out_shape=[jax.ShapeDtypeStruct((n_prompt, d), F32), jax.ShapeDtypeStruct((m - n_prompt, d), F32)],
        compiler_params=_params("arbitrary"),
    )(pos, y, meta, g_final.reshape(1, d), ys3)


def kernel(x_prompt, x_sample, cache_a_k, cache_a_v, state_pool, cache_c_k, cache_c_v, cache_c_idx,
           norm_mix, norm_ffn, norm_final, w_in_even, w_out_even, a_rel_bias, pool_w, pool_scale,
           ffn_w_gate, ffn_w_up, ffn_w_down, w_in_odd, w_out_odd,
           moe_router, moe_w_gate, moe_w_up, moe_w_down):
    nbp, lp, d = x_prompt.shape
    nb, ds, _ = x_sample.shape
    past = cache_c_k.shape[2]
    a_len = cache_a_k.shape[2]
    assert nbp == 1 and lp % 512 == 0 and (nb * ds) % 512 == 0 and ds == POOL_HALO and past >= POOL_HALO
    ns = nb * ds
    m = lp + ns
    tm = 512
    bf = lambda t: t.astype(BF16)

    proj0, x = stack_norm_proj(x_prompt.reshape(lp, d), x_sample.reshape(ns, d), norm_mix[0],
                               bf(w_in_even[0]), tm, 1024)
    k0 = proj0[:, A_WIDTH:2 * A_WIDTH]
    v0 = proj0[:, 2 * A_WIDTH:3 * A_WIDTH]
    u0 = proj0[:, 3 * A_WIDTH:]

    pad = A_PREV_CHUNKS * CHUNK
    bias_p = _rel_bias_tile(a_rel_bias[0], 0, CHUNK, -pad, A_BAND)
    a_p = band_prompt(proj0, _pair_rows(bias_p), lp)

    k_pos = past - a_len + jnp.arange(a_len + ds)
    q_pos = past + jnp.arange(ds)
    qch, kch = q_pos // CHUNK, k_pos // CHUNK
    ok = ((k_pos[None, :] >= 0) & (kch[None, :] <= qch[:, None])
          & (kch[None, :] >= qch[:, None] - A_PREV_CHUNKS))
    bias_s = jnp.where(ok[None], _rel_bias_tile(a_rel_bias[0], past, ds, past - a_len, a_len + ds), NEG)
    a = band_sample(proj0, cache_a_k[0].reshape(nb, a_len, A_WIDTH),
                    cache_a_v[0].reshape(nb, a_len, A_WIDTH), _pair_rows(bias_s), a_p, lp, nb, ds)

    u_s = u0[lp:].reshape(nb, ds, B_WIDTH)
    u_hist = jnp.concatenate([state_pool[0], u_s], axis=1)
    u_ext = jnp.concatenate([jnp.zeros((nb, POOL_HALO - B_HIST, B_WIDTH), F32), u_hist], axis=1)
    pw = bf(pool_w[0])
    ps = pool_scale[0].reshape(1, B_WIDTH)
    p_p = pool_prompt(proj0, pw, ps, lp, tm)
    p = pool_sample(u_ext, pw, ps, p_p, lp, past)
    wo = bf(w_out_even[0])
    y = mm_res([a, p], [wo[:A_WIDTH], wo[A_WIDTH:]], x, tm, d)
    y = ffn(y, norm_ffn[0], bf(ffn_w_gate[0]), bf(ffn_w_up[0]), bf(ffn_w_down[0]), tm, 512)

    n_q = C_HEADS * C_HEAD_DIM
    n_kv = C_KV_HEADS * C_HEAD_DIM
    n_qi = IDX_HEADS * IDX_DIM
    n_main = n_q + 2 * n_kv + n_qi
    w1 = w_in_odd[0]
    w_tail = jnp.pad(w1[:, n_main:], ((0, 0), (0, LANES - (w1.shape[1] - n_main))))
    main = norm_proj(y, norm_mix[1], bf(w1[:, :n_main]), tm, n_main // 2)
    tail = norm_proj(y, norm_mix[1], bf(w_tail), tm, LANES)
    pos = jnp.concatenate([jnp.arange(lp), jnp.tile(past + jnp.arange(ds), nb)])
    kv1, t32, ob = rope_all(main, tail, _rope_tables(pos, C_HEAD_DIM), _rope_tables(pos, IDX_DIM), tm)
    k1 = kv1[:, :n_kv]
    v1 = kv1[:, n_kv:]
    ki1 = t32[:, :IDX_DIM]

    kb_p = 128
    vt = ob[:lp, n_q + n_kv:n_q + 2 * n_kv].reshape(lp // kb_p, kb_p, n_kv).transpose(0, 2, 1)
    wi_t = t32[:lp, IDX_DIM:IDX_DIM + IDX_HEADS].T
    o_p = dsa_t(ob, wi_t, vt, nq=lp, tq=256, kb=kb_p, qb=128, pb=256, cb=1024, topk=min(TOPK_MAX, lp // 4))
    o_s = dsa_sample(ob, t32, cache_c_k[0].reshape(nb, past, n_kv), cache_c_v[0].reshape(nb, past, n_kv), cache_c_idx[0],
                     row0=lp, kb=3 * LANES, topk=min(TOPK_MAX, (past + ds) // 4))
    o = jnp.concatenate([o_p, o_s], axis=0)
    y = mm_res([o], [bf(w_out_odd[0])], y, tm, d)

    wr = jnp.pad(moe_router[0], ((0, 0), (0, LANES - N_EXPERTS)))
    h, meta = router(y, norm_ffn[1], wr, tm)
    sup = 1536
    pos_rows, row_token, tile_expert, tile_rows = route_plan(meta, sup)
    n_rows = row_token.shape[0]
    slab = (d // LANES, LANES)
    xs = gather_rows(h.reshape((m,) + slab), row_token, sup)
    ys = grouped_ffn(xs.reshape(n_rows, d), tile_expert, tile_rows,
                     moe_w_gate[0], moe_w_up[0], moe_w_down[0], sup, 256, 512)
    y_p, y_s = combine_final(y, meta, norm_final, ys.reshape((n_rows,) + slab), pos_rows, lp, 256)

    y_prompt = y_p.reshape(1, lp, d)
    y_sample = y_s.reshape(nb, ds, d)
    keep = min(A_BAND, lp)
    heads = lambda t, n: t.reshape(1, n, -1, A_HEADS, A_HEAD_DIM)
    a_k_prompt = heads(k0[lp - keep:lp], 1)
    a_v_prompt = heads(v0[lp - keep:lp], 1)
    pool_prompt_out = u0[lp - B_HIST:lp].reshape(1, 1, B_HIST, B_WIDTH)
    c_k_prompt = k1[:lp].reshape(1, 1, lp, C_KV_HEADS, C_HEAD_DIM)
    c_v_prompt = v1[:lp].reshape(1, 1, lp, C_KV_HEADS, C_HEAD_DIM)
    c_idx_prompt = ki1[:lp].reshape(1, 1, lp, IDX_DIM)
    shift = lambda cache, new: jnp.concatenate(
        [cache[0], new.reshape(nb, ds, A_HEADS, A_HEAD_DIM)], axis=1)[:, ds:][None]
    a_k_sample = shift(cache_a_k, k0[lp:])
    a_v_sample = shift(cache_a_v, v0[lp:])
    pool_sample_out = u_hist[:, ds:][None]
    c_k_sample = k1[lp:].reshape(1, nb, ds, C_KV_HEADS, C_HEAD_DIM)
    c_v_sample = v1[lp:].reshape(1, nb, ds, C_KV_HEADS, C_HEAD_DIM)
    c_idx_sample = ki1[lp:].reshape(1, nb, ds, IDX_DIM)
    return (y_prompt, y_sample, a_k_prompt, a_v_prompt, pool_prompt_out,
            c_k_prompt, c_v_prompt, c_idx_prompt,
            a_k_sample, a_v_sample, pool_sample_out,
            c_k_sample, c_v_sample, c_idx_sample)
```

```python
import functools
import math

import jax
import jax.numpy as jnp
from jax import lax
from jax.experimental import pallas as pl
from jax.experimental.pallas import tpu as pltpu

F32 = jnp.float32
BF16 = jnp.bfloat16

NORM_EPS = 1e-6
NEG = -1e30
BIG = 1e30
SHIFT_SLACK = 1.01
MIN_SOFTMAX_SUM = 2.0 ** -80

CHUNK = 64
A_HEADS = 16
A_HEAD_DIM = 64
A_WIDTH = A_HEADS * A_HEAD_DIM
A_PREV_CHUNKS = 8
A_BAND = (A_PREV_CHUNKS + 1) * CHUNK
A_REL_CLIP = 128
B_WINDOWS = (2, 4, 8, 16)
B_GROUP = 256
B_WIDTH = B_GROUP * len(B_WINDOWS)
B_HIST = max(B_WINDOWS) - 1
C_HEADS = 16
C_KV_HEADS = 4
C_HEAD_DIM = 128
C_GROUPS = C_HEADS // C_KV_HEADS
IDX_HEADS = 8
IDX_DIM = 64
TOPK_MAX = 256
ROPE_THETA = 500000.0
ROPE_FRAC = 4
N_EXPERTS = 8

LANES = 128
POOL_HALO = 16
BAND_CHUNKS_PER_TRIP = 2
VMEM_LIMIT = 56 * 1024 * 1024

_NT = (((1,), (1,)), ((), ()))


def _params(*sem):
    return pltpu.CompilerParams(dimension_semantics=sem, vmem_limit_bytes=VMEM_LIMIT)


def _rms(x, g):
    ms = jnp.mean(x * x, axis=-1, keepdims=True)
    return x * lax.rsqrt(ms + NORM_EPS) * g


def _norm_proj_kernel(x_ref, g_ref, w_ref, o_ref, h_ref):
    @pl.when(pl.program_id(1) == 0)
    def _():
        h_ref[...] = _rms(x_ref[...], g_ref[...]).astype(BF16)

    o_ref[...] = jnp.dot(h_ref[...], w_ref[...], preferred_element_type=F32)


def norm_proj(x, g, w, tm, tn):
    m, d = x.shape
    n = w.shape[1]
    return pl.pallas_call(
        _norm_proj_kernel,
        grid=(m // tm, n // tn),
        in_specs=[pl.BlockSpec((tm, d), lambda i, j: (i, 0)),
                  pl.BlockSpec((1, d), lambda i, j: (0, 0)),
                  pl.BlockSpec((d, tn), lambda i, j: (0, j))],
        out_specs=pl.BlockSpec((tm, tn), lambda i, j: (i, j)),
        out_shape=jax.ShapeDtypeStruct((m, n), F32),
        scratch_shapes=[pltpu.VMEM((tm, d), BF16)],
        compiler_params=_params("parallel", "arbitrary"),
    )(x, g.reshape(1, d), w)


def _stack_norm_proj_kernel(xp_ref, xs_ref, g_ref, w_ref, o_ref, x_ref, h_ref, *, prompt_tiles):
    i = pl.program_id(0)

    def first_column(src_ref):
        x = src_ref[...]
        x_ref[...] = x
        h_ref[...] = _rms(x, g_ref[...]).astype(BF16)

    @pl.when(jnp.logical_and(pl.program_id(1) == 0, i < prompt_tiles))
    def _():
        first_column(xp_ref)

    @pl.when(jnp.logical_and(pl.program_id(1) == 0, i >= prompt_tiles))
    def _():
        first_column(xs_ref)

    o_ref[...] = jnp.dot(h_ref[...], w_ref[...], preferred_element_type=F32)


def stack_norm_proj(xp, xs, g, w, tm, tn):
    lp, d = xp.shape
    m = lp + xs.shape[0]
    n = w.shape[1]
    prompt_tiles = lp // tm
    return pl.pallas_call(
        functools.partial(_stack_norm_proj_kernel, prompt_tiles=prompt_tiles),
        grid=(m // tm, n // tn),
        in_specs=[pl.BlockSpec((tm, d), lambda i, j: (jnp.minimum(i, prompt_tiles - 1), 0)),
                  pl.BlockSpec((tm, d), lambda i, j: (jnp.maximum(i - prompt_tiles, 0), 0)),
                  pl.BlockSpec((1, d), lambda i, j: (0, 0)),
                  pl.BlockSpec((d, tn), lambda i, j: (0, j))],
        out_specs=[pl.BlockSpec((tm, tn), lambda i, j: (i, j)),
                   pl.BlockSpec((tm, d), lambda i, j: (i, 0))],
        out_shape=[jax.ShapeDtypeStruct((m, n), F32), jax.ShapeDtypeStruct((m, d), F32)],
        scratch_shapes=[pltpu.VMEM((tm, d), BF16)],
        compiler_params=_params("parallel", "arbitrary"),
    )(xp, xs, g.reshape(1, d), w)


def _band_pairs(q_ref, nq, kw_ref, vw_ref, nk, bias_ref, windows, o_ref):
    lane = lax.broadcasted_iota(jnp.int32, (1, LANES), 1)
    first = lane < A_HEAD_DIM
    for hp in range(A_HEADS // 2):
        cs = slice(hp * LANES, (hp + 1) * LANES)
        for q_row0, k_row0, key_ok in windows:
            qp = q_ref[pl.ds(q_row0, nq), cs]
            kp = kw_ref[pl.ds(k_row0, nk), cs]
            vp = vw_ref[pl.ds(k_row0, nk), cs]
            qm = jnp.concatenate([jnp.where(first, qp, 0.0), jnp.where(first, 0.0, qp)], axis=0).astype(BF16)
            s = lax.dot_general(qm, kp, _NT, preferred_element_type=F32)
            s = s * (A_HEAD_DIM ** -0.5) + bias_ref[hp]
            if key_ok is not None:
                s = jnp.where(key_ok, s, NEG)
            mx = jnp.max(s, axis=-1, keepdims=True)
            e = jnp.exp(s - mx)
            l = jnp.sum(e, axis=-1, keepdims=True)
            o = jnp.dot(e.astype(BF16), vp, preferred_element_type=F32) / l
            o_ref[pl.ds(q_row0, nq), cs] = jnp.where(first, o[:nq], o[nq:]).astype(o_ref.dtype)


def _band_prompt_kernel(q_ref, kp_ref, kc_ref, vp_ref, vc_ref, bias_ref, o_ref, kw_ref, vw_ref,
                        *, qb, pad, prompt_blocks):
    i = pl.program_id(0)

    @pl.when(i >= prompt_blocks)
    def _():
        o_ref[...] = jnp.zeros(o_ref.shape, o_ref.dtype)

    @pl.when(i < prompt_blocks)
    def _():
        _band_prompt_block(i, q_ref, kp_ref, kc_ref, vp_ref, vc_ref, bias_ref, o_ref, kw_ref, vw_ref, qb, pad)


def _band_prompt_block(i, q_ref, kp_ref, kc_ref, vp_ref, vc_ref, bias_ref, o_ref, kw_ref, vw_ref, qb, pad):
    kw_ref[0:pad, :] = kp_ref[...].astype(BF16)
    kw_ref[pad:pad + qb, :] = kc_ref[...].astype(BF16)
    vw_ref[0:pad, :] = vp_ref[...].astype(BF16)
    vw_ref[pad:pad + qb, :] = vc_ref[...].astype(BF16)
    col = lax.broadcasted_iota(jnp.int32, (1, A_BAND), 1)

    def chunks(c2, carry):
        windows = []
        for u in range(BAND_CHUNKS_PER_TRIP):
            cc = c2 * BAND_CHUNKS_PER_TRIP + u
            r0 = pl.multiple_of(cc * CHUNK, CHUNK)
            first_valid = jnp.where(i == 0, pad - cc * CHUNK, 0)
            windows.append((r0, r0, col >= first_valid))
        _band_pairs(q_ref, CHUNK, kw_ref, vw_ref, A_BAND, bias_ref, windows, o_ref)
        return carry

    lax.fori_loop(0, qb // CHUNK // BAND_CHUNKS_PER_TRIP, chunks, 0)


def band_prompt(proj, bias, lp):
    pad = A_PREV_CHUNKS * CHUNK
    qb = pad
    blk = (qb, A_WIDTH)
    prev = lambda c: (lambda i: (jnp.maximum(i - 1, 0), c))
    cur = lambda c: (lambda i: (i, c))
    return pl.pallas_call(
        functools.partial(_band_prompt_kernel, qb=qb, pad=pad, prompt_blocks=lp // qb),
        grid=(proj.shape[0] // qb,),
        in_specs=[pl.BlockSpec(blk, cur(0)),
                  pl.BlockSpec(blk, prev(1)), pl.BlockSpec(blk, cur(1)),
                  pl.BlockSpec(blk, prev(2)), pl.BlockSpec(blk, cur(2)),
                  pl.BlockSpec((A_HEADS // 2, 2 * CHUNK, A_BAND), lambda i: (0, 0, 0))],
        out_specs=pl.BlockSpec(blk, lambda i: (i, 0)),
        out_shape=jax.ShapeDtypeStruct((proj.shape[0], A_WIDTH), BF16),
        scratch_shapes=[pltpu.VMEM((pad + qb, A_WIDTH), BF16), pltpu.VMEM((pad + qb, A_WIDTH), BF16)],
        compiler_params=_params("parallel"),
    )(proj, proj, proj, proj, proj, bias)


def _band_sample_kernel(q_ref, kn_ref, vn_ref, ck_ref, cv_ref, bias_ref, dst_ref, o_ref, kw_ref, vw_ref,
                        *, a_len, ds):
    del dst_ref
    kw_ref[0:a_len, :] = ck_ref[...].astype(BF16)
    kw_ref[a_len:a_len + ds, :] = kn_ref[...].astype(BF16)
    vw_ref[0:a_len, :] = cv_ref[...].astype(BF16)
    vw_ref[a_len:a_len + ds, :] = vn_ref[...].astype(BF16)
    _band_pairs(q_ref, ds, kw_ref, vw_ref, a_len + ds, bias_ref, [(0, 0, None)], o_ref)


def band_sample(proj, cache_k, cache_v, bias, dst, lp, nb, ds):
    a_len = cache_k.shape[1]
    row = lambda c: (lambda b: (lp // ds + b, c))
    return pl.pallas_call(
        functools.partial(_band_sample_kernel, a_len=a_len, ds=ds),
        grid=(nb,),
        in_specs=[pl.BlockSpec((ds, A_WIDTH), row(0)),
                  pl.BlockSpec((ds, A_WIDTH), row(1)),
                  pl.BlockSpec((ds, A_WIDTH), row(2)),
                  pl.BlockSpec((None, a_len, A_WIDTH), lambda b: (b, 0, 0)),
                  pl.BlockSpec((None, a_len, A_WIDTH), lambda b: (b, 0, 0)),
                  pl.BlockSpec((A_HEADS // 2, 2 * ds, a_len + ds), lambda b: (0, 0, 0)),
                  pl.BlockSpec(memory_space=pl.ANY)],
        out_specs=pl.BlockSpec((ds, A_WIDTH), row(0)),
        out_shape=jax.ShapeDtypeStruct(dst.shape, dst.dtype),
        input_output_aliases={6: 0},
        scratch_shapes=[pltpu.VMEM((a_len + ds, A_WIDTH), BF16), pltpu.VMEM((a_len + ds, A_WIDTH), BF16)],
        compiler_params=_params("parallel"),
    )(proj, proj, proj, cache_k, cache_v, bias, dst)


def _rel_bias_tile(rel_bias, q0, nq, k0, nk):
    rel_max = q0 - k0 + nq - 1
    rel = jnp.clip(rel_max - jnp.arange(nq + nk - 1), -A_REL_CLIP, A_REL_CLIP) + A_REL_CLIP
    ext = rel_bias[:, rel].astype(F32)
    return jnp.stack([ext[:, nq - 1 - i:nq - 1 - i + nk] for i in range(nq)], axis=1)


def _pair_rows(bias):
    h, nq, nk = bias.shape
    return bias.reshape(h // 2, 2 * nq, nk)


def _pool_kernel(prev_ref, cur_ref, w_ref, sc_ref, *refs, tm, prompt, pos0, prompt_blocks=None):
    o_ref, ext_ref = refs[-2:]
    i = pl.program_id(0)
    if prompt_blocks is None:
        _pool_block(i, prev_ref, cur_ref, w_ref, sc_ref, o_ref, ext_ref, tm, prompt, pos0)
        return

    @pl.when(i >= prompt_blocks)
    def _():
        o_ref[...] = jnp.zeros(o_ref.shape, o_ref.dtype)

    @pl.when(i < prompt_blocks)
    def _():
        _pool_block(i, prev_ref, cur_ref, w_ref, sc_ref, o_ref, ext_ref, tm, prompt, pos0)


def _pool_block(i, prev_ref, cur_ref, w_ref, sc_ref, o_ref, ext_ref, tm, prompt, pos0):
    prev = prev_ref[...]
    if prompt:
        prev = jnp.where(i == 0, 0.0, prev)
        pos = i * tm + lax.broadcasted_iota(jnp.int32, (tm, 1), 0)
    else:
        pos = pos0 + lax.broadcasted_iota(jnp.int32, (tm, 1), 0)
    ext_ref[0:POOL_HALO, :] = prev
    ext_ref[POOL_HALO:POOL_HALO + tm, :] = cur_ref[...]
    for g, w in enumerate(B_WINDOWS):
        cs = slice(g * B_GROUP, (g + 1) * B_GROUP)
        tok = ext_ref[POOL_HALO:POOL_HALO + tm, cs]
        tot = tok
        for j in range(1, w):
            tot = tot + ext_ref[POOL_HALO - j:POOL_HALO - j + tm, cs]
        cnt = jnp.minimum(pos + 1, w).astype(F32)
        pooled = (tot / cnt - tok).astype(BF16)
        o = jnp.dot(pooled, w_ref[g], preferred_element_type=F32) * sc_ref[:, cs]
        o_ref[:, cs] = o.astype(o_ref.dtype)


def pool_prompt(proj, pool_w, pool_scale, lp, tm):
    ucol = 3 * A_WIDTH // B_WIDTH
    per = tm // POOL_HALO
    return pl.pallas_call(
        functools.partial(_pool_kernel, tm=tm, prompt=True, pos0=0, prompt_blocks=lp // tm),
        grid=(proj.shape[0] // tm,),
        in_specs=[pl.BlockSpec((POOL_HALO, B_WIDTH), lambda i: (jnp.maximum(i * per - 1, 0), ucol)),
                  pl.BlockSpec((tm, B_WIDTH), lambda i: (i, ucol)),
                  pl.BlockSpec((len(B_WINDOWS), B_GROUP, B_GROUP), lambda i: (0, 0, 0)),
                  pl.BlockSpec((1, B_WIDTH), lambda i: (0, 0))],
        out_specs=pl.BlockSpec((tm, B_WIDTH), lambda i: (i, 0)),
        out_shape=jax.ShapeDtypeStruct((proj.shape[0], B_WIDTH), BF16),
        scratch_shapes=[pltpu.VMEM((POOL_HALO + tm, B_WIDTH), F32)],
        compiler_params=_params("parallel"),
    )(proj, proj, pool_w, pool_scale)


def pool_sample(u_ext, pool_w, pool_scale, dst, row0, past):
    nb, tot, _ = u_ext.shape
    ds = tot - POOL_HALO
    return pl.pallas_call(
        functools.partial(_pool_kernel, tm=ds, prompt=False, pos0=past),
        grid=(nb,),
        in_specs=[pl.BlockSpec((None, POOL_HALO, B_WIDTH), lambda b: (b, 0, 0)),
                  pl.BlockSpec((None, ds, B_WIDTH), lambda b: (b, POOL_HALO // ds, 0)),
                  pl.BlockSpec((len(B_WINDOWS), B_GROUP, B_GROUP), lambda b: (0, 0, 0)),
                  pl.BlockSpec((1, B_WIDTH), lambda b: (0, 0)),
                  pl.BlockSpec(memory_space=pl.ANY)],
        out_specs=pl.BlockSpec((ds, B_WIDTH), lambda b: (row0 // ds + b, 0)),
        out_shape=jax.ShapeDtypeStruct(dst.shape, dst.dtype),
        input_output_aliases={4: 0},
        scratch_shapes=[pltpu.VMEM((POOL_HALO + ds, B_WIDTH), F32)],
        compiler_params=_params("parallel"),
    )(u_ext, u_ext, pool_w, pool_scale, dst)


def _mm_res_kernel(*refs, n_in):
    xs, ws = refs[:n_in], refs[n_in:2 * n_in]
    res_ref, o_ref = refs[2 * n_in], refs[2 * n_in + 1]
    acc = res_ref[...]
    for x_ref, w_ref in zip(xs, ws):
        acc = acc + jnp.dot(x_ref[...], w_ref[...], preferred_element_type=F32)
    o_ref[...] = acc


def mm_res(xs, ws, res, tm, tn):
    m, n = res.shape
    n_in = len(xs)
    in_specs = ([pl.BlockSpec((tm, x.shape[1]), lambda i, j: (i, 0)) for x in xs]
                + [pl.BlockSpec((w.shape[0], tn), lambda i, j: (0, j)) for w in ws]
                + [pl.BlockSpec((tm, tn), lambda i, j: (i, j))])
    return pl.pallas_call(
        functools.partial(_mm_res_kernel, n_in=n_in),
        grid=(m // tm, n // tn),
        in_specs=in_specs,
        out_specs=pl.BlockSpec((tm, tn), lambda i, j: (i, j)),
        out_shape=jax.ShapeDtypeStruct((m, n), F32),
        compiler_params=_params("parallel", "arbitrary"),
    )(*xs, *ws, res)


def _swiglu_tile(h, wg, wu, wd):
    a = jnp.dot(h, wg, preferred_element_type=F32)
    b = jnp.dot(h, wu, preferred_element_type=F32)
    act = (a * jax.nn.sigmoid(a) * b).astype(BF16)
    return jnp.dot(act, wd, preferred_element_type=F32)


def _ffn_kernel(y_ref, g_ref, wg_ref, wu_ref, wd_ref, o_ref, h_ref, acc_ref):
    f = pl.program_id(1)

    @pl.when(f == 0)
    def _():
        h_ref[...] = _rms(y_ref[...], g_ref[...]).astype(BF16)
        acc_ref[...] = jnp.zeros_like(acc_ref)

    acc_ref[...] += _swiglu_tile(h_ref[...], wg_ref[...], wu_ref[...], wd_ref[...])

    @pl.when(f == pl.num_programs(1) - 1)
    def _():
        o_ref[...] = y_ref[...] + acc_ref[...]


def ffn(y, g, wg, wu, wd, tm, tf):
    m, d = y.shape
    ff = wg.shape[1]
    return pl.pallas_call(
        _ffn_kernel,
        grid=(m // tm, ff // tf),
        in_specs=[pl.BlockSpec((tm, d), lambda i, f: (i, 0)),
                  pl.BlockSpec((1, d), lambda i, f: (0, 0)),
                  pl.BlockSpec((d, tf), lambda i, f: (0, f)),
                  pl.BlockSpec((d, tf), lambda i, f: (0, f)),
                  pl.BlockSpec((tf, d), lambda i, f: (f, 0))],
        out_specs=pl.BlockSpec((tm, d), lambda i, f: (i, 0)),
        out_shape=jax.ShapeDtypeStruct((m, d), F32),
        scratch_shapes=[pltpu.VMEM((tm, d), BF16), pltpu.VMEM((tm, d), F32)],
        compiler_params=_params("parallel", "arbitrary"),
    )(y, g.reshape(1, d), wg, wu, wd)


def _rope_tables(pos, head_dim):
    rot = head_dim // ROPE_FRAC
    half = rot // 2
    inv = jnp.exp(-math.log(ROPE_THETA) * jnp.arange(half, dtype=F32) * (2.0 / rot))
    ang = pos.astype(F32)[:, None] * inv[None, :]
    cos, sin = jnp.cos(ang), jnp.sin(ang)
    m = pos.shape[0]
    one = jnp.ones((m, head_dim - rot), F32)
    zero_r = jnp.zeros((m, head_dim - rot), F32)
    zero_h = jnp.zeros((m, half), F32)
    c = jnp.concatenate([cos, cos, one], axis=1)
    s_dn = jnp.concatenate([-sin, zero_h, zero_r], axis=1)
    s_up = jnp.concatenate([zero_h, sin, zero_r], axis=1)
    rep = LANES // head_dim
    return jnp.stack([jnp.tile(c, (1, rep)), jnp.tile(s_dn, (1, rep)), jnp.tile(s_up, (1, rep))])


def _rot(x, tab_ref, half):
    return (x * tab_ref[0] + pltpu.roll(x, LANES - half, 1) * tab_ref[1]
            + pltpu.roll(x, half, 1) * tab_ref[2])


def _rope_kernel(main_ref, tail_ref, tq_ref, ti_ref, kv_ref, t32_ref, ob_ref,
                 *, n_q, n_k, n_v, n_qi, wi_scale, q_scale):
    half_qk = C_HEAD_DIM // ROPE_FRAC // 2
    half_i = IDX_DIM // ROPE_FRAC // 2
    for c in range(n_q + n_k + n_v + n_qi):
        cs = slice(c * LANES, (c + 1) * LANES)
        x = main_ref[:, cs]
        if c < n_q + n_k:
            x = _rot(x, tq_ref, half_qk)
        elif c >= n_q + n_k + n_v:
            x = _rot(x, ti_ref, half_i)
        if n_q <= c < n_q + n_k + n_v:
            kv_ref[:, (c - n_q) * LANES:(c - n_q + 1) * LANES] = x
        ob_ref[:, cs] = (x * q_scale if c < n_q else x).astype(BF16)
    t = tail_ref[...]
    lane = lax.broadcasted_iota(jnp.int32, (1, LANES), 1)
    r = _rot(t, ti_ref, half_i)
    t32_ref[...] = jnp.where(lane < IDX_DIM, r, t * wi_scale)
    c = n_q + n_k + n_v + n_qi
    ob_ref[:, c * LANES:(c + 1) * LANES] = jnp.where(lane < IDX_DIM, r, pltpu.roll(r, IDX_DIM, 1)).astype(BF16)


def rope_all(main, tail, tab_qk, tab_idx, tm):
    m, nmain = main.shape
    n_q = C_HEADS * C_HEAD_DIM // LANES
    n_k = C_KV_HEADS * C_HEAD_DIM // LANES
    n_qi = IDX_HEADS * IDX_DIM // LANES
    wi_scale = (IDX_HEADS ** -0.5) * (IDX_DIM ** -0.5)
    q_scale = (C_HEAD_DIM ** -0.5) * math.log2(math.e)
    row = lambda i: (i, 0)
    return pl.pallas_call(
        functools.partial(_rope_kernel, n_q=n_q, n_k=n_k, n_v=n_k, n_qi=n_qi, wi_scale=wi_scale, q_scale=q_scale),
        grid=(m // tm,),
        in_specs=[pl.BlockSpec((tm, nmain), row),
                  pl.BlockSpec((tm, LANES), row),
                  pl.BlockSpec((3, tm, LANES), lambda i: (0, i, 0)),
                  pl.BlockSpec((3, tm, LANES), lambda i: (0, i, 0))],
        out_specs=[pl.BlockSpec((tm, 2 * n_k * LANES), row),
                   pl.BlockSpec((tm, LANES), row),
                   pl.BlockSpec((tm, nmain + LANES), row)],
        out_shape=[jax.ShapeDtypeStruct((m, 2 * n_k * LANES), F32),
                   jax.ShapeDtypeStruct((m, LANES), F32),
                   jax.ShapeDtypeStruct((m, nmain + LANES), BF16)],
        compiler_params=_params("parallel"),
    )(main, tail, tab_qk, tab_idx)


def _dsa_kernel(q_ref, qi_ref, wi_ref, kn_ref, vn_ref, kin_ref, *refs, tq, kb, past, n_valid, q_pos0, topk):
    ck_refs, cv_refs = refs[:C_KV_HEADS], refs[C_KV_HEADS:2 * C_KV_HEADS]
    ci_ref, o_ref, k_ref, v_ref, ki_ref, sc_ref, qs_ref, acc_ref, m_ref, l_ref = refs[2 * C_KV_HEADS:]
    i = 0
    rows = C_GROUPS * tq
    sub = kb // LANES
    kf = float(topk)

    new = past + tq
    ci = ci_ref[...].astype(BF16)
    for g in range(C_KV_HEADS):
        gs = slice(g * C_HEAD_DIM, (g + 1) * C_HEAD_DIM)
        k_ref[0:past, gs] = ck_refs[g][...].astype(BF16)
        v_ref[0:past, gs] = cv_refs[g][...].astype(BF16)
    ki_ref[0:past, :] = jnp.concatenate([ci, ci], axis=1)
    k_ref[past:new, :] = kn_ref[...]
    v_ref[past:new, :] = vn_ref[...]
    ki_ref[past:new, :] = kin_ref[...]
    n_pad = k_ref.shape[0] - new
    k_ref[new:, :] = jnp.zeros((n_pad, k_ref.shape[1]), BF16)
    v_ref[new:, :] = jnp.zeros((n_pad, v_ref.shape[1]), BF16)
    ki_ref[new:, :] = jnp.zeros((n_pad, ki_ref.shape[1]), BF16)

    q_pos = q_pos0 + i * tq + lax.broadcasted_iota(jnp.int32, (tq, 1), 0)
    q_chunk = q_pos // CHUNK
    last_chunk = (q_pos0 + i * tq + tq - 1) // CHUNK
    kv_limit = min(n_valid, (last_chunk + 1) * CHUNK)
    nkb = (kv_limit + kb - 1) // kb

    lane = lax.broadcasted_iota(jnp.int32, (1, LANES), 1)
    first = lane < IDX_DIM
    wi = wi_ref[:, IDX_DIM:IDX_DIM + IDX_HEADS]

    def score_block(b, carry):
        for c in range(sub):
            off = pl.multiple_of(b * kb + c * LANES, LANES)
            kib = ki_ref[pl.ds(off, LANES), :]
            acc = jnp.zeros((tq, LANES), F32)
            for hp in range(IDX_HEADS // 2):
                qp = qi_ref[:, hp * LANES:(hp + 1) * LANES]
                for half in range(2):
                    keep = first if half == 0 else jnp.logical_not(first)
                    qm = jnp.where(keep, qp, jnp.zeros_like(qp))
                    d = lax.dot_general(qm, kib, _NT, preferred_element_type=F32)
                    h = hp * 2 + half
                    acc = acc + jnp.maximum(d, 0.0) * wi[:, h:h + 1]
            k_pos = off + lane
            adm = jnp.logical_and(k_pos // CHUNK <= q_chunk, k_pos < n_valid)
            sc_ref[b, :, c * LANES:(c + 1) * LANES] = jnp.where(adm, acc, NEG)
        return carry

    lax.fori_loop(0, nkb, score_block, 0)

    def lane_sum(x):
        return jnp.sum(x, axis=1, keepdims=True)

    def count_ge(t):
        def body(b, acc):
            for c in range(sub):
                blk = sc_ref[b, :, c * LANES:(c + 1) * LANES]
                acc = acc + jnp.where(blk >= t, 1.0, 0.0)
            return acc
        return lane_sum(lax.fori_loop(0, nkb, body, jnp.zeros((tq, LANES), F32)))

    def stats(b, carry):
        mx, mn, cnt = carry
        for c in range(sub):
            blk = sc_ref[b, :, c * LANES:(c + 1) * LANES]
            ok = blk > 0.5 * NEG
            mx = jnp.maximum(mx, blk)
            mn = jnp.minimum(mn, jnp.where(ok, blk, BIG))
            cnt = cnt + jnp.where(ok, 1.0, 0.0)
        return mx, mn, cnt

    mx, mn, cnt = lax.fori_loop(
        0, nkb, stats,
        (jnp.full((tq, LANES), NEG, F32), jnp.full((tq, LANES), BIG, F32), jnp.zeros((tq, LANES), F32)))
    row_max = jnp.max(mx, axis=1, keepdims=True)
    row_min = jnp.min(mn, axis=1, keepdims=True)
    n_adm = lane_sum(cnt)

    done0 = jnp.where(n_adm <= kf, 1.0, 0.0)
    state0 = (row_min, jnp.full((tq, 1), BIG, F32), row_max, jnp.full((tq, 1), 0.5 * NEG, F32), done0)

    def bisect(state, n_steps):
        def cond(c):
            it, st = c
            return jnp.logical_and(it < n_steps, jnp.min(st[4]) < 0.5)

        def body(c):
            it, (lo, hi, mid, thr, done) = c
            cnt = count_ge(mid)
            live = done < 0.5
            hit = jnp.logical_and(live, cnt == kf)
            ge = cnt >= kf
            thr = jnp.where(hit, mid, thr)
            done = jnp.where(hit, 1.0, done)
            lo = jnp.where(ge, mid, lo)
            hi = jnp.where(ge, hi, mid)
            return it + 1, (lo, hi, 0.5 * (lo + hi), thr, done)

        return lax.while_loop(cond, body, (jnp.int32(0), state))[1]

    def snap(state):
        lo, hi, mid, thr, done = state

        def body(b, carry):
            v_lo, v_hi = carry
            for c in range(sub):
                blk = sc_ref[b, :, c * LANES:(c + 1) * LANES]
                v_lo = jnp.minimum(v_lo, jnp.where(blk >= lo, blk, BIG))
                v_hi = jnp.maximum(v_hi, jnp.where(blk < hi, blk, NEG))
            return v_lo, v_hi

        v_lo, v_hi = lax.fori_loop(0, nkb, body,
                                   (jnp.full((tq, LANES), BIG, F32), jnp.full((tq, LANES), NEG, F32)))
        v_lo = jnp.min(v_lo, axis=1, keepdims=True)
        v_hi = jnp.max(v_hi, axis=1, keepdims=True)
        live = done < 0.5
        tie = jnp.logical_and(live, v_lo == v_hi)
        thr = jnp.where(tie, v_lo, thr)
        done = jnp.where(tie, 1.0, done)
        lo = jnp.where(live, v_lo, lo)
        return (lo, hi, 0.5 * (lo + hi), thr, done), jnp.where(tie, 1.0, 0.0)

    state = bisect(state0, 32)

    def refine_cond(c):
        rounds, st, _ = c
        return jnp.logical_and(rounds < 10, jnp.min(st[4]) < 0.5)

    def refine_body(c):
        rounds, st, tie = c
        st, new_tie = snap(st)
        st = bisect(st, 32)
        return rounds + 1, st, jnp.maximum(tie, new_tie)

    _, state, tie = lax.while_loop(refine_cond, refine_body,
                                   (jnp.int32(0), state, jnp.zeros((tq, 1), F32)))
    thr = state[3]
    any_tie = jnp.max(tie) > 0.5

    @pl.when(jnp.logical_not(any_tie))
    def _():
        def body(b, carry):
            for c in range(sub):
                cs = slice(c * LANES, (c + 1) * LANES)
                sc_ref[b, :, cs] = jnp.where(sc_ref[b, :, cs] >= thr, 0.0, NEG)
            return carry
        lax.fori_loop(0, nkb, body, 0)

    @pl.when(any_tie)
    def _():
        def gt_body(b, acc):
            for c in range(sub):
                acc = acc + jnp.where(sc_ref[b, :, c * LANES:(c + 1) * LANES] > thr, 1.0, 0.0)
            return acc
        need = kf - lane_sum(lax.fori_loop(0, nkb, gt_body, jnp.zeros((tq, LANES), F32)))

        def count_eq_upto(j):
            def body(b, acc):
                for c in range(sub):
                    blk = sc_ref[b, :, c * LANES:(c + 1) * LANES]
                    idx = (b * kb + c * LANES + lane).astype(F32)
                    acc = acc + jnp.where(jnp.logical_and(blk == thr, idx <= j), 1.0, 0.0)
                return acc
            return lane_sum(lax.fori_loop(0, nkb, body, jnp.zeros((tq, LANES), F32)))

        def idx_body(_, c):
            lo_j, hi_j = c
            mid_j = jnp.floor(0.5 * (lo_j + hi_j))
            ok = count_eq_upto(mid_j) >= need
            return jnp.where(ok, lo_j, mid_j), jnp.where(ok, mid_j, hi_j)

        n_steps = max(1, math.ceil(math.log2(sc_ref.shape[0] * kb + 1)))
        _, last = lax.fori_loop(
            0, n_steps, idx_body,
            (jnp.full((tq, 1), -1.0, F32), jnp.full((tq, 1), float(sc_ref.shape[0] * kb), F32)))
        last = jnp.where(tie > 0.5, last, BIG)

        def body(b, carry):
            for c in range(sub):
                cs = slice(c * LANES, (c + 1) * LANES)
                blk = sc_ref[b, :, cs]
                idx = (b * kb + c * LANES + lane).astype(F32)
                sel = jnp.logical_or(blk > thr, jnp.logical_and(blk == thr, idx <= last))
                sc_ref[b, :, cs] = jnp.where(sel, 0.0, NEG)
            return carry
        lax.fori_loop(0, nkb, body, 0)

    for g in range(C_KV_HEADS):
        for hh in range(C_GROUPS):
            h = g * C_GROUPS + hh
            qs_ref[g, hh * tq:(hh + 1) * tq, :] = q_ref[:, h * C_HEAD_DIM:(h + 1) * C_HEAD_DIM]
    m_ref[...] = jnp.full(m_ref.shape, NEG, F32)
    l_ref[...] = jnp.zeros(l_ref.shape, F32)
    acc_ref[...] = jnp.zeros(acc_ref.shape, F32)

    def attend(b, carry):
        off = pl.multiple_of(b * kb, kb)
        bias = sc_ref[b]
        bias = jnp.concatenate([bias] * C_GROUPS, axis=0)
        for g in range(C_KV_HEADS):
            cs = slice(g * C_HEAD_DIM, (g + 1) * C_HEAD_DIM)
            kblk = k_ref[pl.ds(off, kb), cs]
            vblk = v_ref[pl.ds(off, kb), cs]
            s = lax.dot_general(qs_ref[g], kblk, _NT, preferred_element_type=F32)
            s = s + bias
            m_old = m_ref[g]
            m_new = jnp.maximum(m_old, jnp.max(s, axis=1, keepdims=True))
            alpha = jnp.exp2(m_old - m_new)
            p = jnp.exp2(s - m_new)
            l_ref[g] = alpha * l_ref[g] + jnp.sum(p, axis=1, keepdims=True)
            acc_ref[g] = alpha * acc_ref[g] + jnp.dot(p.astype(BF16), vblk, preferred_element_type=F32)
            m_ref[g] = m_new
        return carry

    lax.fori_loop(0, nkb, attend, 0, unroll=True)

    for g in range(C_KV_HEADS):
        o = acc_ref[g] / l_ref[g]
        for hh in range(C_GROUPS):
            h = g * C_GROUPS + hh
            o_ref[:, h * C_HEAD_DIM:(h + 1) * C_HEAD_DIM] = o[hh * tq:(hh + 1) * tq, :].astype(o_ref.dtype)


def dsa_sample(ob, t32, cache_k, cache_v, cache_i, *, row0, kb, topk):
    nb, past = cache_k.shape[:2]
    head = lambda g: pl.BlockSpec((None, past, C_HEAD_DIM), lambda b, i: (b, 0, g))
    heads = [head(g) for g in range(C_KV_HEADS)]
    n_kv = C_KV_HEADS * C_HEAD_DIM
    ds = (ob.shape[0] - row0) // nb
    n_q = C_HEADS * C_HEAD_DIM
    n_qi = IDX_HEADS * IDX_DIM
    s_all = past + ds
    s_pad = -(-s_all // kb) * kb
    rows = C_GROUPS * ds
    col = lambda w, off: (lambda b, i: (row0 // ds + b, off // w))
    cmap = lambda b, i: (b, 0, 0)
    return pl.pallas_call(
        functools.partial(_dsa_kernel, tq=ds, kb=kb, past=past, n_valid=s_all, q_pos0=past, topk=topk),
        grid=(nb, 1),
        in_specs=[pl.BlockSpec((ds, n_q), col(n_q, 0)),
                  pl.BlockSpec((ds, n_qi), col(n_qi, n_q + 2 * n_kv)),
                  pl.BlockSpec((ds, LANES), col(LANES, 0)),
                  pl.BlockSpec((ds, n_kv), col(n_kv, n_q)),
                  pl.BlockSpec((ds, n_kv), col(n_kv, n_q + n_kv)),
                  pl.BlockSpec((ds, LANES), col(LANES, n_q + 2 * n_kv + n_qi))]
                 + heads + heads + [pl.BlockSpec((None, past, IDX_DIM), cmap)],
        out_specs=pl.BlockSpec((ds, n_q), lambda b, i: (b, 0)),
        out_shape=jax.ShapeDtypeStruct((nb * ds, n_q), BF16),
        scratch_shapes=[pltpu.VMEM((s_pad, n_kv), BF16),
                        pltpu.VMEM((s_pad, n_kv), BF16),
                        pltpu.VMEM((s_pad, LANES), BF16),
                        pltpu.VMEM((s_pad // kb, ds, kb), F32),
                        pltpu.VMEM((C_KV_HEADS, rows, C_HEAD_DIM), BF16),
                        pltpu.VMEM((C_KV_HEADS, rows, C_HEAD_DIM), F32),
                        pltpu.VMEM((C_KV_HEADS, rows, 1), F32),
                        pltpu.VMEM((C_KV_HEADS, rows, 1), F32)],
        compiler_params=_params("parallel", "arbitrary"),
    )(ob, ob, t32, ob, ob, ob, *([cache_k] * C_KV_HEADS), *([cache_v] * C_KV_HEADS), cache_i)


def _dsa_t_kernel(q_ref, qi_ref, wi_ref, k_ref, vt_ref, ki_ref, o_ref,
                  sc_ref, qim_ref, acc_ref, m_ref, l_ref, l8_ref, kmax_ref,
                  *, tq, kb, qb, pb, cb, n_valid, q_pos0, topk):
    i = pl.program_id(1)
    kf = float(topk)

    q_pos = q_pos0 + i * tq + lax.broadcasted_iota(jnp.int32, (1, tq), 1)
    q_chunk = q_pos // CHUNK
    last_chunk = (q_pos0 + i * tq + tq - 1) // CHUNK
    kv_limit = jnp.minimum(n_valid, (last_chunk + 1) * CHUNK)
    nkb = (kv_limit + kb - 1) // kb
    ncb = (kv_limit + cb - 1) // cb

    def col_reduce(x, op):
        groups = x.shape[0] // 8
        chains = 8 if groups % 8 == 0 else 1
        return op(op(x.reshape(chains, groups // chains, 8, tq), axis=1), axis=0)

    col_sum = lambda x: col_reduce(x, jnp.sum)
    col_max = lambda x: col_reduce(x, jnp.max)
    col_min = lambda x: col_reduce(x, jnp.min)

    lane = lax.broadcasted_iota(jnp.int32, (1, LANES), 1)
    first = lane < IDX_DIM
    for hp in range(IDX_HEADS // 2):
        qp = qi_ref[:, hp * LANES:(hp + 1) * LANES]
        qim_ref[2 * hp] = jnp.where(first, qp, jnp.zeros_like(qp))
        qim_ref[2 * hp + 1] = jnp.where(first, jnp.zeros_like(qp), qp)
    block_iota = lax.broadcasted_iota(jnp.int32, (pb, 1), 0)

    def score_block(b, carry):
        off = pl.multiple_of(b * pb, pb)
        kib = ki_ref[pl.ds(off, pb), :]
        parts = []
        for h in range(IDX_HEADS):
            d = lax.dot_general(kib, qim_ref[h], _NT, preferred_element_type=F32)
            parts.append(jnp.maximum(d, 0.0) * wi_ref[h:h + 1, :])
        while len(parts) > 1:
            parts = [a + b for a, b in zip(parts[0::2], parts[1::2])]
        acc = parts[0]
        k_pos = off + block_iota
        adm = jnp.logical_and(k_pos // CHUNK <= q_chunk, k_pos < n_valid)
        sc_ref[pl.ds(off, pb), :] = jnp.where(adm, acc, NEG)
        return carry

    lax.fori_loop(0, ncb * (cb // pb), score_block, 0)
    key_iota = lax.broadcasted_iota(jnp.int32, (cb, 1), 0)

    def blocks(body, init):
        def step(b, carry):
            off = pl.multiple_of(b * cb, cb)
            return body(off, sc_ref[pl.ds(off, cb), :], carry)
        return lax.fori_loop(0, ncb, step, init)

    def total(x):
        return jnp.sum(x, axis=0, keepdims=True)

    def count_ge(t):
        return total(blocks(lambda off, blk, acc: acc + col_sum(jnp.where(blk >= t, 1.0, 0.0)),
                            jnp.zeros((8, tq), F32)))

    def stats(off, blk, carry):
        mx, mn, cnt, pos, nonneg = carry
        ok = blk > 0.5 * NEG
        return (jnp.maximum(mx, col_max(blk)), jnp.minimum(mn, col_min(jnp.where(ok, blk, BIG))),
                cnt + col_sum(jnp.where(ok, 1.0, 0.0)),
                pos + col_sum(jnp.where(blk > 0.0, 1.0, 0.0)),
                nonneg + col_sum(jnp.where(blk >= 0.0, 1.0, 0.0)))

    zeros8 = jnp.zeros((8, tq), F32)
    mx, mn, cnt, pos, nonneg = blocks(stats, (jnp.full((8, tq), NEG, F32), jnp.full((8, tq), BIG, F32),
                                              zeros8, zeros8, zeros8))
    row_max = jnp.max(mx, axis=0, keepdims=True)
    row_min = jnp.min(mn, axis=0, keepdims=True)
    n_adm = total(cnt)
    n_pos = total(pos)
    n_nonneg = total(nonneg)

    open_row = n_adm > kf
    above = n_pos >= kf
    below = n_nonneg < kf
    zero_tie = jnp.logical_and(open_row, jnp.logical_not(jnp.logical_or(above, below)))
    lo0 = jnp.where(above, 0.0, row_min)
    hi0 = jnp.where(below, 0.0, BIG)
    mid0 = jnp.where(below, 0.5 * (lo0 + hi0), row_max)
    thr0 = jnp.where(zero_tie, 0.0, 0.5 * NEG)
    done0 = jnp.where(jnp.logical_or(jnp.logical_not(open_row), zero_tie), 1.0, 0.0)
    tie0 = jnp.where(zero_tie, 1.0, 0.0)
    c_lo0 = jnp.where(above, n_pos, n_adm)
    c_hi0 = jnp.where(below, n_nonneg, 0.0)
    state0 = (lo0, hi0, mid0, thr0, done0, c_lo0, c_hi0)
    log_k = math.log2(kf + 0.5)

    def bisect(state, n_steps):
        def cond(c):
            it, st = c
            return jnp.logical_and(it < n_steps, jnp.min(st[4]) < 0.5)

        def body(c):
            it, (lo, hi, mid, thr, done, c_lo, c_hi) = c
            cnt = count_ge(mid)
            hit = jnp.logical_and(done < 0.5, cnt == kf)
            ge = cnt >= kf
            thr = jnp.where(hit, mid, thr)
            done = jnp.where(hit, 1.0, done)
            lo = jnp.where(ge, mid, lo)
            hi = jnp.where(ge, hi, mid)
            c_lo = jnp.where(ge, cnt, c_lo)
            c_hi = jnp.where(ge, c_hi, cnt)
            log_lo = jnp.log2(jnp.maximum(c_lo, 1.0))
            frac = (log_lo - log_k) / jnp.maximum(log_lo - jnp.log2(jnp.maximum(c_hi, 0.5)), 1e-6)
            frac = jnp.clip(frac, 0.02, 0.98)
            frac = jnp.where(jnp.logical_or(it % 3 == 2, hi >= BIG), 0.5, frac)
            return it + 1, (lo, hi, lo + (hi - lo) * frac, thr, done, c_lo, c_hi)

        return lax.while_loop(cond, body, (jnp.int32(0), state))[1]

    def snap(state):
        lo, hi, mid, thr, done, c_lo, c_hi = state

        def body(off, blk, carry):
            v_lo, v_hi = carry
            return (jnp.minimum(v_lo, col_min(jnp.where(blk >= lo, blk, BIG))),
                    jnp.maximum(v_hi, col_max(jnp.where(blk < hi, blk, NEG))))

        v_lo, v_hi = blocks(body, (jnp.full((8, tq), BIG, F32), jnp.full((8, tq), NEG, F32)))
        v_lo = jnp.min(v_lo, axis=0, keepdims=True)
        v_hi = jnp.max(v_hi, axis=0, keepdims=True)
        live = done < 0.5
        tie = jnp.logical_and(live, v_lo == v_hi)
        thr = jnp.where(tie, v_lo, thr)
        done = jnp.where(tie, 1.0, done)
        lo = jnp.where(live, v_lo, lo)
        return (lo, hi, 0.5 * (lo + hi), thr, done, c_lo, c_hi), jnp.where(tie, 1.0, 0.0)

    state = bisect(state0, 32)

    def refine_cond(c):
        rounds, st, _ = c
        return jnp.logical_and(rounds < 10, jnp.min(st[4]) < 0.5)

    def refine_body(c):
        rounds, st, tie = c
        st, new_tie = snap(st)
        st = bisect(st, 32)
        return rounds + 1, st, jnp.maximum(tie, new_tie)

    _, state, tie = lax.while_loop(refine_cond, refine_body, (jnp.int32(0), state, tie0))
    thr = state[3]
    any_tie = jnp.max(tie) > 0.5

    @pl.when(jnp.logical_not(any_tie))
    def _():
        def body(off, blk, carry):
            sc_ref[pl.ds(off, cb), :] = jnp.where(blk >= thr, 0.0, NEG)
            return carry
        blocks(body, 0)

    @pl.when(any_tie)
    def _():
        need = kf - total(blocks(lambda off, blk, acc: acc + col_sum(jnp.where(blk > thr, 1.0, 0.0)),
                                 jnp.zeros((8, tq), F32)))

        def count_eq_upto(j):
            def body(off, blk, acc):
                idx = (off + key_iota).astype(F32)
                return acc + col_sum(jnp.where(jnp.logical_and(blk == thr, idx <= j), 1.0, 0.0))
            return total(blocks(body, jnp.zeros((8, tq), F32)))

        def idx_cond(c):
            it, (lo_j, hi_j, settled) = c
            return jnp.logical_and(it < n_steps, jnp.min(settled) < 0.5)

        def idx_body(c):
            it, (lo_j, hi_j, settled) = c
            mid_j = jnp.floor(0.5 * (lo_j + hi_j))
            cnt = count_eq_upto(mid_j)
            ok = cnt >= need
            live = settled < 0.5
            lo_j = jnp.where(jnp.logical_and(live, jnp.logical_not(ok)), mid_j, lo_j)
            hi_j = jnp.where(jnp.logical_and(live, ok), mid_j, hi_j)
            settled = jnp.where(jnp.logical_or(cnt == need, hi_j - lo_j <= 1.0), 1.0, settled)
            return it + 1, (lo_j, hi_j, settled)

        n_keys = sc_ref.shape[0]
        n_steps = max(1, math.ceil(math.log2(n_keys + 1))) + 1
        _, (_, last, _) = lax.while_loop(
            idx_cond, idx_body,
            (jnp.int32(0), (jnp.full((1, tq), -1.0, F32), jnp.full((1, tq), float(n_keys), F32),
                            jnp.where(tie > 0.5, 0.0, 1.0))))
        last = jnp.where(tie > 0.5, last, BIG)

        def body(off, blk, carry):
            idx = (off + key_iota).astype(F32)
            sel = jnp.logical_or(blk > thr, jnp.logical_and(blk == thr, idx <= last))
            sc_ref[pl.ds(off, cb), :] = jnp.where(sel, 0.0, NEG)
            return carry
        blocks(body, 0)

    halves = kb // qb

    def head_scores(off, h):
        gs = slice((h // C_GROUPS) * C_HEAD_DIM, (h // C_GROUPS + 1) * C_HEAD_DIM)
        out = []
        for j in range(halves):
            rows = pl.ds(off + j * qb, qb)
            s = lax.dot_general(k_ref[rows, gs], q_ref[:, h * C_HEAD_DIM:(h + 1) * C_HEAD_DIM],
                                _NT, preferred_element_type=F32)
            out.append(s + sc_ref[rows, :])
        return out, gs

    @pl.when(i == 0)
    def _():
        for g in range(C_KV_HEADS):
            gs = slice(g * C_HEAD_DIM, (g + 1) * C_HEAD_DIM)

            def body(b, mx):
                kk = k_ref[pl.ds(pl.multiple_of(b * cb, cb), cb), gs].astype(F32)
                return jnp.maximum(mx, jnp.sum(kk * kk, axis=1, keepdims=True))

            mx = lax.fori_loop(0, k_ref.shape[0] // cb, body, jnp.zeros((cb, 1), F32))
            kmax_ref[g] = jnp.full((1, tq), 1.0, F32) * jnp.max(mx)

    ones = jnp.ones((8, C_HEAD_DIM), BF16)
    for h in range(C_HEADS):
        qh = q_ref[:, h * C_HEAD_DIM:(h + 1) * C_HEAD_DIM].astype(F32)
        qn2 = lax.dot_general(ones, (qh * qh).astype(BF16), _NT, preferred_element_type=F32)[0:1, :]
        m_ref[h] = jnp.sqrt(qn2 * kmax_ref[h // C_GROUPS]) * SHIFT_SLACK
    l8_ref[...] = jnp.zeros(l8_ref.shape, F32)
    acc_ref[...] = jnp.zeros(acc_ref.shape, F32)

    def attend_fixed(b0, n):
        for h in range(C_HEADS):
            lsum, upd = None, None
            for u in range(n):
                b = b0 + u
                ss, gs = head_scores(pl.multiple_of(b * kb, kb), h)
                ps = [jnp.exp2(s - m_ref[h]) for s in ss]
                part = sum(col_sum(p) for p in ps)
                p_all = jnp.concatenate([p.astype(BF16) for p in ps], axis=0)
                pv = jnp.dot(vt_ref[b, gs, :], p_all, preferred_element_type=F32)
                lsum = part if lsum is None else lsum + part
                upd = pv if upd is None else upd + pv
            l8_ref[h] += lsum
            acc_ref[h] += upd

    def four_blocks(t, carry):
        attend_fixed(4 * t, 4)
        return carry

    lax.fori_loop(0, nkb // 4, four_blocks, 0)

    @pl.when(nkb % 4 >= 2)
    def _():
        attend_fixed((nkb // 4) * 4, 2)
    l_min = jnp.full((1, tq), BIG, F32)
    for h in range(C_HEADS):
        l_ref[h] = total(l8_ref[h])
        l_min = jnp.minimum(l_min, l_ref[h])

    @pl.when(jnp.logical_not(jnp.min(l_min) >= MIN_SOFTMAX_SUM))
    def _():
        m_ref[...] = jnp.full(m_ref.shape, NEG, F32)
        l_ref[...] = jnp.zeros(l_ref.shape, F32)
        acc_ref[...] = jnp.zeros(acc_ref.shape, F32)

        def attend(b, carry):
            off = pl.multiple_of(b * kb, kb)
            for h in range(C_HEADS):
                ss, gs = head_scores(off, h)
                m_old = m_ref[h]
                m_new = m_old
                for s in ss:
                    m_new = jnp.maximum(m_new, jnp.max(col_max(s), axis=0, keepdims=True))
                alpha = jnp.exp2(m_old - m_new)
                ps = [jnp.exp2(s - m_new) for s in ss]
                l_ref[h] = alpha * l_ref[h] + total(sum(col_sum(p) for p in ps))
                p_all = jnp.concatenate([p.astype(BF16) for p in ps], axis=0)
                acc_ref[h] = alpha * acc_ref[h] + jnp.dot(vt_ref[b, gs, :], p_all,
                                                          preferred_element_type=F32)
                m_ref[h] = m_new
            return carry

        lax.fori_loop(0, nkb, attend, 0)

    for h in range(C_HEADS):
        o = acc_ref[h] / l_ref[h]
        o_ref[:, h * C_HEAD_DIM:(h + 1) * C_HEAD_DIM] = o.T.astype(o_ref.dtype)


def dsa_t(ob, wi_t, vt, *, nq, tq, kb, qb, pb, cb, topk):
    dq = C_HEADS * C_HEAD_DIM
    n_kv = C_KV_HEADS * C_HEAD_DIM
    n_qi = IDX_HEADS * IDX_DIM
    s_pad = nq
    assert tq % (2 * kb) == 0 and tq % CHUNK == 0 and nq % cb == 0 and cb % pb == 0 and kb % qb == 0
    return pl.pallas_call(
        functools.partial(_dsa_t_kernel, tq=tq, kb=kb, qb=qb, pb=pb, cb=cb, n_valid=nq, q_pos0=0, topk=topk),
        grid=(1, nq // tq),
        in_specs=[pl.BlockSpec((tq, dq), lambda b, i: (i, 0)),
                  pl.BlockSpec((tq, n_qi), lambda b, i: (i, (dq + 2 * n_kv) // n_qi)),
                  pl.BlockSpec((IDX_HEADS, tq), lambda b, i: (0, i)),
                  pl.BlockSpec((nq, n_kv), lambda b, i: (0, dq // n_kv)),
                  pl.BlockSpec(vt.shape, lambda b, i: (0, 0, 0)),
                  pl.BlockSpec((nq, LANES), lambda b, i: (0, (dq + 2 * n_kv + n_qi) // LANES))],
        out_specs=pl.BlockSpec((tq, dq), lambda b, i: (i, 0)),
        out_shape=jax.ShapeDtypeStruct((nq, dq), BF16),
        scratch_shapes=[pltpu.VMEM((s_pad, tq), F32),
                        pltpu.VMEM((IDX_HEADS, tq, LANES), BF16),
                        pltpu.VMEM((C_HEADS, C_HEAD_DIM, tq), F32),
                        pltpu.VMEM((C_HEADS, 1, tq), F32),
                        pltpu.VMEM((C_HEADS, 1, tq), F32),
                        pltpu.VMEM((C_HEADS, 8, tq), F32),
                        pltpu.VMEM((C_KV_HEADS, 1, tq), F32)],
        compiler_params=_params("arbitrary", "arbitrary"),
    )(ob, ob, wi_t, ob, vt, ob)


def _router_kernel(y_ref, g_ref, wr_ref, h_ref, gate_ref):
    h = _rms(y_ref[...], g_ref[...])
    h_ref[...] = h.astype(BF16)
    logits = jnp.dot(h, wr_ref[...], preferred_element_type=F32, precision=lax.Precision.HIGHEST)
    lane = lax.broadcasted_iota(jnp.int32, logits.shape, 1)
    lg = jnp.where(lane < N_EXPERTS, logits, NEG)
    m1 = jnp.max(lg, axis=1, keepdims=True)
    i1 = jnp.min(jnp.where(lg == m1, lane, LANES), axis=1, keepdims=True)
    lg2 = jnp.where(lane == i1, NEG, lg)
    m2 = jnp.max(lg2, axis=1, keepdims=True)
    i2 = jnp.min(jnp.where(lg2 == m2, lane, LANES), axis=1, keepdims=True)
    e = jnp.exp(m2 - m1)
    g1 = 1.0 / (1.0 + e)
    g2 = e / (1.0 + e)
    meta = jnp.where(lane == 0, i1.astype(F32), jnp.where(lane == 1, i2.astype(F32),
                     jnp.where(lane == 2, g1, jnp.where(lane == 3, g2, 0.0))))
    gate_ref[...] = meta


def router(y, g, w_router_pad, tm):
    m, d = y.shape
    return pl.pallas_call(
        _router_kernel,
        grid=(m // tm,),
        in_specs=[pl.BlockSpec((tm, d), lambda i: (i, 0)),
                  pl.BlockSpec((1, d), lambda i: (0, 0)),
                  pl.BlockSpec((d, LANES), lambda i: (0, 0))],
        out_specs=[pl.BlockSpec((tm, d), lambda i: (i, 0)),
                   pl.BlockSpec((tm, LANES), lambda i: (i, 0))],
        out_shape=[jax.ShapeDtypeStruct((m, d), BF16), jax.ShapeDtypeStruct((m, LANES), F32)],
        compiler_params=_params("parallel"),
    )(y, g.reshape(1, d), w_router_pad)


def route_plan(meta, sup):
    m = meta.shape[0]
    n_tiles = (2 * m) // sup + N_EXPERTS
    e_all = jnp.concatenate([meta[:, 0], meta[:, 1]]).astype(jnp.int32)
    onehot = (e_all[:, None] == jnp.arange(N_EXPERTS)[None, :]).astype(jnp.int32)
    rank = jnp.sum((jnp.cumsum(onehot, axis=0) - onehot) * onehot, axis=1)
    counts = jnp.sum(onehot, axis=0)
    n_super = (counts + sup - 1) // sup
    super_end = jnp.cumsum(n_super)
    super_start = super_end - n_super
    pos = (super_start * sup)[e_all] + rank
    tiles = jnp.arange(n_tiles)
    used = super_end[-1]
    t_eff = jnp.minimum(tiles, used - 1)
    tile_expert = jnp.minimum(jnp.searchsorted(super_end, t_eff, side="right"), N_EXPERTS - 1).astype(jnp.int32)
    rows = jnp.clip(counts[tile_expert] - (t_eff - super_start[tile_expert]) * sup, 0, sup)
    tile_rows = jnp.where(tiles < used, rows, 0).astype(jnp.int32)
    token = jnp.concatenate([jnp.arange(m), jnp.arange(m)]).astype(jnp.int32)
    row_token = (jnp.arange(n_tiles * sup, dtype=jnp.int32) % m).at[pos].set(token)
    return pos.astype(jnp.int32), row_token, tile_expert, tile_rows


def _gather_rows_kernel(tok_ref, h_ref, o_ref, sem, *, batch):
    base = pl.program_id(0) * batch

    def issue(j, carry):
        pltpu.make_async_copy(h_ref.at[tok_ref[base + j]], o_ref.at[j], sem).start()
        return carry

    lax.fori_loop(0, batch, issue, 0)
    pltpu.make_async_copy(o_ref, o_ref, sem).wait()


def gather_rows(h3, row_token, batch):
    n_rows = row_token.shape[0]
    blk = (batch,) + h3.shape[1:]
    return pl.pallas_call(
        functools.partial(_gather_rows_kernel, batch=batch),
        grid_spec=pltpu.PrefetchScalarGridSpec(
            num_scalar_prefetch=1,
            grid=(n_rows // batch,),
            in_specs=[pl.BlockSpec(memory_space=pl.ANY)],
            out_specs=pl.BlockSpec(blk, lambda i, tok: (i, 0, 0)),
            scratch_shapes=[pltpu.SemaphoreType.DMA(())]),
        out_shape=jax.ShapeDtypeStruct((n_rows,) + h3.shape[1:], h3.dtype),
        compiler_params=_params("arbitrary"),
    )(row_token, h3)


def _grouped_ffn_kernel(te_ref, tr_ref, x_ref, wg_ref, wu_ref, wd_ref, o_ref,
                        wgb_ref, wub_ref, wdb_ref, *, sup, sub):
    t = pl.program_id(0)
    f = pl.program_id(1)
    rows = tr_ref[t]

    @pl.when(rows > 0)
    def _():
        wgb_ref[...] = wg_ref[...].astype(BF16)
        wub_ref[...] = wu_ref[...].astype(BF16)
        wdb_ref[...] = wd_ref[...].astype(BF16)

    for s in range(sup // sub):
        sl = slice(s * sub, (s + 1) * sub)

        @pl.when(jnp.logical_and(s * sub >= rows, f == 0))
        def _():
            o_ref[sl, :] = jnp.zeros((sub, o_ref.shape[1]), F32)

        @pl.when(s * sub < rows)
        def _():
            part = _swiglu_tile(x_ref[sl, :], wgb_ref[...], wub_ref[...], wdb_ref[...])

            @pl.when(f == 0)
            def _():
                o_ref[sl, :] = part

            @pl.when(f > 0)
            def _():
                o_ref[sl, :] += part


def grouped_ffn(xs, tile_expert, tile_rows, wg, wu, wd, sup, sub, tf):
    n_rows, d = xs.shape
    ff = wg.shape[2]
    n_f = ff // tf
    fidx = lambda t, f, te, tr: jnp.where(tr[t] > 0, f, n_f - 1)
    return pl.pallas_call(
        functools.partial(_grouped_ffn_kernel, sup=sup, sub=sub),
        grid_spec=pltpu.PrefetchScalarGridSpec(
            num_scalar_prefetch=2,
            grid=(n_rows // sup, n_f),
            in_specs=[pl.BlockSpec((sup, d), lambda t, f, te, tr: (t, 0), pipeline_mode=pl.Buffered(1)),
                      pl.BlockSpec((None, d, tf), lambda t, f, te, tr: (te[t], 0, fidx(t, f, te, tr))),
                      pl.BlockSpec((None, d, tf), lambda t, f, te, tr: (te[t], 0, fidx(t, f, te, tr))),
                      pl.BlockSpec((None, tf, d), lambda t, f, te, tr: (te[t], fidx(t, f, te, tr), 0))],
            out_specs=pl.BlockSpec((sup, d), lambda t, f, te, tr: (t, 0), pipeline_mode=pl.Buffered(1)),
            scratch_shapes=[pltpu.VMEM((d, tf), BF16), pltpu.VMEM((d, tf), BF16), pltpu.VMEM((tf, d), BF16)]),
        out_shape=jax.ShapeDtypeStruct((n_rows, d), F32),
        compiler_params=_params("arbitrary", "arbitrary"),
    )(tile_expert, tile_rows, xs, wg, wu, wd)


def _combine_kernel(pos_ref, y_ref, meta_ref, g_ref, ys_ref, op_ref, os_ref, buf_ref, x_ref, sem,
                    *, n_tok, tmc, prompt_tiles):
    i = pl.program_id(0)
    slot = i % 2

    def issue(tile, into):
        def body(j, carry):
            t = tile * tmc + j
            pltpu.make_async_copy(ys_ref.at[pos_ref[t]], buf_ref.at[into, 0, j], sem.at[into]).start()
            pltpu.make_async_copy(ys_ref.at[pos_ref[n_tok + t]], buf_ref.at[into, 1, j], sem.at[into]).start()
            return carry
        lax.fori_loop(0, tmc, body, 0)

    @pl.when(i == 0)
    def _():
        issue(0, 0)

    @pl.when(i + 1 < pl.num_programs(0))
    def _():
        issue(i + 1, 1 - slot)

    for e in range(2):
        pltpu.make_async_copy(buf_ref.at[slot, e], buf_ref.at[slot, e], sem.at[slot]).wait()
    g1 = meta_ref[:, 2:3]
    g2 = meta_ref[:, 3:4]
    ss = jnp.zeros((tmc, 1), F32)
    for c in range(buf_ref.shape[3]):
        cs = slice(c * LANES, (c + 1) * LANES)
        x = y_ref[:, cs] + (buf_ref[slot, 0, :, c, :] * g1 + buf_ref[slot, 1, :, c, :] * g2)
        ss = ss + jnp.sum(x * x, axis=1, keepdims=True)
        x_ref[:, cs] = x
    inv = lax.rsqrt(ss / x_ref.shape[1] + NORM_EPS)

    @pl.when(i < prompt_tiles)
    def _():
        op_ref[...] = x_ref[...] * inv * g_ref[...]

    @pl.when(i >= prompt_tiles)
    def _():
        os_ref[...] = x_ref[...] * inv * g_ref[...]


def combine_final(y, meta, g_final, ys3, pos, n_prompt, tmc):
    m, d = y.shape
    slab = ys3.shape[1:]
    prompt_tiles = n_prompt // tmc
    return pl.pallas_call(
        functools.partial(_combine_kernel, n_tok=m, tmc=tmc, prompt_tiles=prompt_tiles),
        grid_spec=pltpu.PrefetchScalarGridSpec(
            num_scalar_prefetch=1,
            grid=(m // tmc,),
            in_specs=[pl.BlockSpec((tmc, d), lambda i, pos: (i, 0)),
                      pl.BlockSpec((tmc, LANES), lambda i, pos: (i, 0)),
                      pl.BlockSpec((1, d), lambda i, pos: (0, 0)),
                      pl.BlockSpec(memory_space=pl.ANY)],
            out_specs=[pl.BlockSpec((tmc, d), lambda i, pos: (jnp.minimum(i, prompt_tiles - 1), 0)),
                       pl.BlockSpec((tmc, d), lambda i, pos: (jnp.maximum(i - prompt_tiles, 0), 0))],
            scratch_shapes=[pltpu.VMEM((2, 2, tmc) + slab, F32), pltpu.VMEM((tmc, d), F32),
                            pltpu.SemaphoreType.DMA((2,))]),
        out_shape=[jax.ShapeDtypeStruct((n_prompt, d), F32), jax.ShapeDtypeStruct((m - n_prompt, d), F32)],
        compiler_params=_params("arbitrary"),
    )(pos, y, meta, g_final.reshape(1, d), ys3)


def kernel(x_prompt, x_sample, cache_a_k, cache_a_v, state_pool, cache_c_k, cache_c_v, cache_c_idx,
           norm_mix, norm_ffn, norm_final, w_in_even, w_out_even, a_rel_bias, pool_w, pool_scale,
           ffn_w_gate, ffn_w_up, ffn_w_down, w_in_odd, w_out_odd,
           moe_router, moe_w_gate, moe_w_up, moe_w_down):
    nbp, lp, d = x_prompt.shape
    nb, ds, _ = x_sample.shape
    past = cache_c_k.shape[2]
    a_len = cache_a_k.shape[2]
    assert nbp == 1 and lp % 512 == 0 and (nb * ds) % 512 == 0 and ds == POOL_HALO and past >= POOL_HALO
    ns = nb * ds
    m = lp + ns
    tm = 512
    bf = lambda t: t.astype(BF16)

    proj0, x = stack_norm_proj(x_prompt.reshape(lp, d), x_sample.reshape(ns, d), norm_mix[0],
                               bf(w_in_even[0]), tm, 1024)
    k0 = proj0[:, A_WIDTH:2 * A_WIDTH]
    v0 = proj0[:, 2 * A_WIDTH:3 * A_WIDTH]
    u0 = proj0[:, 3 * A_WIDTH:]

    pad = A_PREV_CHUNKS * CHUNK
    bias_p = _rel_bias_tile(a_rel_bias[0], 0, CHUNK, -pad, A_BAND)
    a_p = band_prompt(proj0, _pair_rows(bias_p), lp)

    k_pos = past - a_len + jnp.arange(a_len + ds)
    q_pos = past + jnp.arange(ds)
    qch, kch = q_pos // CHUNK, k_pos // CHUNK
    ok = ((k_pos[None, :] >= 0) & (kch[None, :] <= qch[:, None])
          & (kch[None, :] >= qch[:, None] - A_PREV_CHUNKS))
    bias_s = jnp.where(ok[None], _rel_bias_tile(a_rel_bias[0], past, ds, past - a_len, a_len + ds), NEG)
    a = band_sample(proj0, cache_a_k[0].reshape(nb, a_len, A_WIDTH),
                    cache_a_v[0].reshape(nb, a_len, A_WIDTH), _pair_rows(bias_s), a_p, lp, nb, ds)

    u_s = u0[lp:].reshape(nb, ds, B_WIDTH)
    u_hist = jnp.concatenate([state_pool[0], u_s], axis=1)
    u_ext = jnp.concatenate([jnp.zeros((nb, POOL_HALO - B_HIST, B_WIDTH), F32), u_hist], axis=1)
    pw = bf(pool_w[0])
    ps = pool_scale[0].reshape(1, B_WIDTH)
    p_p = pool_prompt(proj0, pw, ps, lp, tm)
    p = pool_sample(u_ext, pw, ps, p_p, lp, past)
    wo = bf(w_out_even[0])
    y = mm_res([a, p], [wo[:A_WIDTH], wo[A_WIDTH:]], x, tm, d)
    y = ffn(y, norm_ffn[0], bf(ffn_w_gate[0]), bf(ffn_w_up[0]), bf(ffn_w_down[0]), tm, 512)

    n_q = C_HEADS * C_HEAD_DIM
    n_kv = C_KV_HEADS * C_HEAD_DIM
    n_qi = IDX_HEADS * IDX_DIM
    n_main = n_q + 2 * n_kv + n_qi
    w1 = w_in_odd[0]
    w_tail = jnp.pad(w1[:, n_main:], ((0, 0), (0, LANES - (w1.shape[1] - n_main))))
    main = norm_proj(y, norm_mix[1], bf(w1[:, :n_main]), tm, n_main // 2)
    tail = norm_proj(y, norm_mix[1], bf(w_tail), tm, LANES)
    pos = jnp.concatenate([jnp.arange(lp), jnp.tile(past + jnp.arange(ds), nb)])
    kv1, t32, ob = rope_all(main, tail, _rope_tables(pos, C_HEAD_DIM), _rope_tables(pos, IDX_DIM), tm)
    k1 = kv1[:, :n_kv]
    v1 = kv1[:, n_kv:]
    ki1 = t32[:, :IDX_DIM]

    kb_p = 128
    vt = ob[:lp, n_q + n_kv:n_q + 2 * n_kv].reshape(lp // kb_p, kb_p, n_kv).transpose(0, 2, 1)
    wi_t = t32[:lp, IDX_DIM:IDX_DIM + IDX_HEADS].T
    o_p = dsa_t(ob, wi_t, vt, nq=lp, tq=256, kb=kb_p, qb=128, pb=256, cb=512, topk=min(TOPK_MAX, lp // 4))
    o_s = dsa_sample(ob, t32, cache_c_k[0].reshape(nb, past, n_kv), cache_c_v[0].reshape(nb, past, n_kv), cache_c_idx[0],
                     row0=lp, kb=3 * LANES, topk=min(TOPK_MAX, (past + ds) // 4))
    o = jnp.concatenate([o_p, o_s], axis=0)
    y = mm_res([o], [bf(w_out_odd[0])], y, tm, d)

    wr = jnp.pad(moe_router[0], ((0, 0), (0, LANES - N_EXPERTS)))
    h, meta = router(y, norm_ffn[1], wr, tm)
    sup = 1536
    pos_rows, row_token, tile_expert, tile_rows = route_plan(meta, sup)
    n_rows = row_token.shape[0]
    slab = (d // LANES, LANES)
    xs = gather_rows(h.reshape((m,) + slab), row_token, sup)
    ys = grouped_ffn(xs.reshape(n_rows, d), tile_expert, tile_rows,
                     moe_w_gate[0], moe_w_up[0], moe_w_down[0], sup, 256, 512)
    y_p, y_s = combine_final(y, meta, norm_final, ys.reshape((n_rows,) + slab), pos_rows, lp, 256)

    y_prompt = y_p.reshape(1, lp, d)
    y_sample = y_s.reshape(nb, ds, d)
    keep = min(A_BAND, lp)
    heads = lambda t, n: t.reshape(1, n, -1, A_HEADS, A_HEAD_DIM)
    a_k_prompt = heads(k0[lp - keep:lp], 1)
    a_v_prompt = heads(v0[lp - keep:lp], 1)
    pool_prompt_out = u0[lp - B_HIST:lp].reshape(1, 1, B_HIST, B_WIDTH)
    c_k_prompt = k1[:lp].reshape(1, 1, lp, C_KV_HEADS, C_HEAD_DIM)
    c_v_prompt = v1[:lp].reshape(1, 1, lp, C_KV_HEADS, C_HEAD_DIM)
    c_idx_prompt = ki1[:lp].reshape(1, 1, lp, IDX_DIM)
    shift = lambda cache, new: jnp.concatenate(
        [cache[0], new.reshape(nb, ds, A_HEADS, A_HEAD_DIM)], axis=1)[:, ds:][None]
    a_k_sample = shift(cache_a_k, k0[lp:])
    a_v_sample = shift(cache_a_v, v0[lp:])
    pool_sample_out = u_hist[:, ds:][None]
    c_k_sample = k1[lp:].reshape(1, nb, ds, C_KV_HEADS, C_HEAD_DIM)
    c_v_sample = v1[lp:].reshape(1, nb, ds, C_KV_HEADS, C_HEAD_DIM)
    c_idx_sample = ki1[lp:].reshape(1, nb, ds, IDX_DIM)
    return (y_prompt, y_sample, a_k_prompt, a_v_prompt, pool_prompt_out,
            c_k_prompt, c_v_prompt, c_idx_prompt,
            a_k_sample, a_v_sample, pool_sample_out,
            c_k_sample, c_v_sample, c_idx_sample)
```

```python
import functools
import math

import jax
import jax.numpy as jnp
from jax import lax
from jax.experimental import pallas as pl
from jax.experimental.pallas import tpu as pltpu

F32 = jnp.float32
BF16 = jnp.bfloat16

NORM_EPS = 1e-6
NEG = -1e30
BIG = 1e30
SHIFT_SLACK = 1.01
MIN_SOFTMAX_SUM = 2.0 ** -80

CHUNK = 64
A_HEADS = 16
A_HEAD_DIM = 64
A_WIDTH = A_HEADS * A_HEAD_DIM
A_PREV_CHUNKS = 8
A_BAND = (A_PREV_CHUNKS + 1) * CHUNK
A_REL_CLIP = 128
B_WINDOWS = (2, 4, 8, 16)
B_GROUP = 256
B_WIDTH = B_GROUP * len(B_WINDOWS)
B_HIST = max(B_WINDOWS) - 1
C_HEADS = 16
C_KV_HEADS = 4
C_HEAD_DIM = 128
C_GROUPS = C_HEADS // C_KV_HEADS
IDX_HEADS = 8
IDX_DIM = 64
TOPK_MAX = 256
ROPE_THETA = 500000.0
ROPE_FRAC = 4
N_EXPERTS = 8

LANES = 128
POOL_HALO = 16
BAND_CHUNKS_PER_TRIP = 2
VMEM_LIMIT = 56 * 1024 * 1024

_NT = (((1,), (1,)), ((), ()))


def _params(*sem):
    return pltpu.CompilerParams(dimension_semantics=sem, vmem_limit_bytes=VMEM_LIMIT)


def _rms(x, g):
    ms = jnp.mean(x * x, axis=-1, keepdims=True)
    return x * lax.rsqrt(ms + NORM_EPS) * g


def _norm_proj_kernel(x_ref, g_ref, w_ref, o_ref, h_ref):
    @pl.when(pl.program_id(1) == 0)
    def _():
        h_ref[...] = _rms(x_ref[...], g_ref[...]).astype(BF16)

    o_ref[...] = jnp.dot(h_ref[...], w_ref[...], preferred_element_type=F32)


def norm_proj(x, g, w, tm, tn):
    m, d = x.shape
    n = w.shape[1]
    return pl.pallas_call(
        _norm_proj_kernel,
        grid=(m // tm, n // tn),
        in_specs=[pl.BlockSpec((tm, d), lambda i, j: (i, 0)),
                  pl.BlockSpec((1, d), lambda i, j: (0, 0)),
                  pl.BlockSpec((d, tn), lambda i, j: (0, j))],
        out_specs=pl.BlockSpec((tm, tn), lambda i, j: (i, j)),
        out_shape=jax.ShapeDtypeStruct((m, n), F32),
        scratch_shapes=[pltpu.VMEM((tm, d), BF16)],
        compiler_params=_params("parallel", "arbitrary"),
    )(x, g.reshape(1, d), w)


def _stack_norm_proj_kernel(xp_ref, xs_ref, g_ref, w_ref, o_ref, x_ref, h_ref, *, prompt_tiles):
    i = pl.program_id(0)

    def first_column(src_ref):
        x = src_ref[...]
        x_ref[...] = x
        h_ref[...] = _rms(x, g_ref[...]).astype(BF16)

    @pl.when(jnp.logical_and(pl.program_id(1) == 0, i < prompt_tiles))
    def _():
        first_column(xp_ref)

    @pl.when(jnp.logical_and(pl.program_id(1) == 0, i >= prompt_tiles))
    def _():
        first_column(xs_ref)

    o_ref[...] = jnp.dot(h_ref[...], w_ref[...], preferred_element_type=F32)


def stack_norm_proj(xp, xs, g, w, tm, tn):
    lp, d = xp.shape
    m = lp + xs.shape[0]
    n = w.shape[1]
    prompt_tiles = lp // tm
    return pl.pallas_call(
        functools.partial(_stack_norm_proj_kernel, prompt_tiles=prompt_tiles),
        grid=(m // tm, n // tn),
        in_specs=[pl.BlockSpec((tm, d), lambda i, j: (jnp.minimum(i, prompt_tiles - 1), 0)),
                  pl.BlockSpec((tm, d), lambda i, j: (jnp.maximum(i - prompt_tiles, 0), 0)),
                  pl.BlockSpec((1, d), lambda i, j: (0, 0)),
                  pl.BlockSpec((d, tn), lambda i, j: (0, j))],
        out_specs=[pl.BlockSpec((tm, tn), lambda i, j: (i, j)),
                   pl.BlockSpec((tm, d), lambda i, j: (i, 0))],
        out_shape=[jax.ShapeDtypeStruct((m, n), F32), jax.ShapeDtypeStruct((m, d), F32)],
        scratch_shapes=[pltpu.VMEM((tm, d), BF16)],
        compiler_params=_params("parallel", "arbitrary"),
    )(xp, xs, g.reshape(1, d), w)


def _band_pairs(q_ref, nq, kw_ref, vw_ref, nk, bias_ref, windows, o_ref):
    lane = lax.broadcasted_iota(jnp.int32, (1, LANES), 1)
    first = lane < A_HEAD_DIM
    for hp in range(A_HEADS // 2):
        cs = slice(hp * LANES, (hp + 1) * LANES)
        for q_row0, k_row0, key_ok in windows:
            qp = q_ref[pl.ds(q_row0, nq), cs]
            kp = kw_ref[pl.ds(k_row0, nk), cs]
            vp = vw_ref[pl.ds(k_row0, nk), cs]
            qm = jnp.concatenate([jnp.where(first, qp, 0.0), jnp.where(first, 0.0, qp)], axis=0).astype(BF16)
            s = lax.dot_general(qm, kp, _NT, preferred_element_type=F32)
            s = s * (A_HEAD_DIM ** -0.5) + bias_ref[hp]
            if key_ok is not None:
                s = jnp.where(key_ok, s, NEG)
            mx = jnp.max(s, axis=-1, keepdims=True)
            e = jnp.exp(s - mx)
            l = jnp.sum(e, axis=-1, keepdims=True)
            o = jnp.dot(e.astype(BF16), vp, preferred_element_type=F32) / l
            o_ref[pl.ds(q_row0, nq), cs] = jnp.where(first, o[:nq], o[nq:]).astype(o_ref.dtype)


def _band_prompt_kernel(q_ref, kp_ref, kc_ref, vp_ref, vc_ref, bias_ref, o_ref, kw_ref, vw_ref,
                        *, qb, pad, prompt_blocks):
    i = pl.program_id(0)

    @pl.when(i >= prompt_blocks)
    def _():
        o_ref[...] = jnp.zeros(o_ref.shape, o_ref.dtype)

    @pl.when(i < prompt_blocks)
    def _():
        _band_prompt_block(i, q_ref, kp_ref, kc_ref, vp_ref, vc_ref, bias_ref, o_ref, kw_ref, vw_ref, qb, pad)


def _band_prompt_block(i, q_ref, kp_ref, kc_ref, vp_ref, vc_ref, bias_ref, o_ref, kw_ref, vw_ref, qb, pad):
    kw_ref[0:pad, :] = kp_ref[...].astype(BF16)
    kw_ref[pad:pad + qb, :] = kc_ref[...].astype(BF16)
    vw_ref[0:pad, :] = vp_ref[...].astype(BF16)
    vw_ref[pad:pad + qb, :] = vc_ref[...].astype(BF16)
    col = lax.broadcasted_iota(jnp.int32, (1, A_BAND), 1)

    def chunks(c2, carry):
        windows = []
        for u in range(BAND_CHUNKS_PER_TRIP):
            cc = c2 * BAND_CHUNKS_PER_TRIP + u
            r0 = pl.multiple_of(cc * CHUNK, CHUNK)
            first_valid = jnp.where(i == 0, pad - cc * CHUNK, 0)
            windows.append((r0, r0, col >= first_valid))
        _band_pairs(q_ref, CHUNK, kw_ref, vw_ref, A_BAND, bias_ref, windows, o_ref)
        return carry

    lax.fori_loop(0, qb // CHUNK // BAND_CHUNKS_PER_TRIP, chunks, 0)


def band_prompt(proj, bias, lp):
    pad = A_PREV_CHUNKS * CHUNK
    qb = pad
    blk = (qb, A_WIDTH)
    prev = lambda c: (lambda i: (jnp.maximum(i - 1, 0), c))
    cur = lambda c: (lambda i: (i, c))
    return pl.pallas_call(
        functools.partial(_band_prompt_kernel, qb=qb, pad=pad, prompt_blocks=lp // qb),
        grid=(proj.shape[0] // qb,),
        in_specs=[pl.BlockSpec(blk, cur(0)),
                  pl.BlockSpec(blk, prev(1)), pl.BlockSpec(blk, cur(1)),
                  pl.BlockSpec(blk, prev(2)), pl.BlockSpec(blk, cur(2)),
                  pl.BlockSpec((A_HEADS // 2, 2 * CHUNK, A_BAND), lambda i: (0, 0, 0))],
        out_specs=pl.BlockSpec(blk, lambda i: (i, 0)),
        out_shape=jax.ShapeDtypeStruct((proj.shape[0], A_WIDTH), BF16),
        scratch_shapes=[pltpu.VMEM((pad + qb, A_WIDTH), BF16), pltpu.VMEM((pad + qb, A_WIDTH), BF16)],
        compiler_params=_params("parallel"),
    )(proj, proj, proj, proj, proj, bias)


def _band_sample_kernel(q_ref, kn_ref, vn_ref, ck_ref, cv_ref, bias_ref, dst_ref, o_ref, kw_ref, vw_ref,
                        *, a_len, ds):
    del dst_ref
    kw_ref[0:a_len, :] = ck_ref[...].astype(BF16)
    kw_ref[a_len:a_len + ds, :] = kn_ref[...].astype(BF16)
    vw_ref[0:a_len, :] = cv_ref[...].astype(BF16)
    vw_ref[a_len:a_len + ds, :] = vn_ref[...].astype(BF16)
    _band_pairs(q_ref, ds, kw_ref, vw_ref, a_len + ds, bias_ref, [(0, 0, None)], o_ref)


def band_sample(proj, cache_k, cache_v, bias, dst, lp, nb, ds):
    a_len = cache_k.shape[1]
    row = lambda c: (lambda b: (lp // ds + b, c))
    return pl.pallas_call(
        functools.partial(_band_sample_kernel, a_len=a_len, ds=ds),
        grid=(nb,),
        in_specs=[pl.BlockSpec((ds, A_WIDTH), row(0)),
                  pl.BlockSpec((ds, A_WIDTH), row(1)),
                  pl.BlockSpec((ds, A_WIDTH), row(2)),
                  pl.BlockSpec((None, a_len, A_WIDTH), lambda b: (b, 0, 0)),
                  pl.BlockSpec((None, a_len, A_WIDTH), lambda b: (b, 0, 0)),
                  pl.BlockSpec((A_HEADS // 2, 2 * ds, a_len + ds), lambda b: (0, 0, 0)),
                  pl.BlockSpec(memory_space=pl.ANY)],
        out_specs=pl.BlockSpec((ds, A_WIDTH), row(0)),
        out_shape=jax.ShapeDtypeStruct(dst.shape, dst.dtype),
        input_output_aliases={6: 0},
        scratch_shapes=[pltpu.VMEM((a_len + ds, A_WIDTH), BF16), pltpu.VMEM((a_len + ds, A_WIDTH), BF16)],
        compiler_params=_params("parallel"),
    )(proj, proj, proj, cache_k, cache_v, bias, dst)


def _rel_bias_tile(rel_bias, q0, nq, k0, nk):
    rel_max = q0 - k0 + nq - 1
    rel = jnp.clip(rel_max - jnp.arange(nq + nk - 1), -A_REL_CLIP, A_REL_CLIP) + A_REL_CLIP
    ext = rel_bias[:, rel].astype(F32)
    return jnp.stack([ext[:, nq - 1 - i:nq - 1 - i + nk] for i in range(nq)], axis=1)


def _pair_rows(bias):
    h, nq, nk = bias.shape
    return bias.reshape(h // 2, 2 * nq, nk)


def _pool_kernel(prev_ref, cur_ref, w_ref, sc_ref, *refs, tm, prompt, pos0, prompt_blocks=None):
    o_ref, ext_ref = refs[-2:]
    i = pl.program_id(0)
    if prompt_blocks is None:
        _pool_block(i, prev_ref, cur_ref, w_ref, sc_ref, o_ref, ext_ref, tm, prompt, pos0)
        return

    @pl.when(i >= prompt_blocks)
    def _():
        o_ref[...] = jnp.zeros(o_ref.shape, o_ref.dtype)

    @pl.when(i < prompt_blocks)
    def _():
        _pool_block(i, prev_ref, cur_ref, w_ref, sc_ref, o_ref, ext_ref, tm, prompt, pos0)


def _pool_block(i, prev_ref, cur_ref, w_ref, sc_ref, o_ref, ext_ref, tm, prompt, pos0):
    prev = prev_ref[...]
    if prompt:
        prev = jnp.where(i == 0, 0.0, prev)
        pos = i * tm + lax.broadcasted_iota(jnp.int32, (tm, 1), 0)
    else:
        pos = pos0 + lax.broadcasted_iota(jnp.int32, (tm, 1), 0)
    ext_ref[0:POOL_HALO, :] = prev
    ext_ref[POOL_HALO:POOL_HALO + tm, :] = cur_ref[...]
    for g, w in enumerate(B_WINDOWS):
        cs = slice(g * B_GROUP, (g + 1) * B_GROUP)
        tok = ext_ref[POOL_HALO:POOL_HALO + tm, cs]
        tot = tok
        for j in range(1, w):
            tot = tot + ext_ref[POOL_HALO - j:POOL_HALO - j + tm, cs]
        cnt = jnp.minimum(pos + 1, w).astype(F32)
        pooled = (tot / cnt - tok).astype(BF16)
        o = jnp.dot(pooled, w_ref[g], preferred_element_type=F32) * sc_ref[:, cs]
        o_ref[:, cs] = o.astype(o_ref.dtype)


def pool_prompt(proj, pool_w, pool_scale, lp, tm):
    ucol = 3 * A_WIDTH // B_WIDTH
    per = tm // POOL_HALO
    return pl.pallas_call(
        functools.partial(_pool_kernel, tm=tm, prompt=True, pos0=0, prompt_blocks=lp // tm),
        grid=(proj.shape[0] // tm,),
        in_specs=[pl.BlockSpec((POOL_HALO, B_WIDTH), lambda i: (jnp.maximum(i * per - 1, 0), ucol)),
                  pl.BlockSpec((tm, B_WIDTH), lambda i: (i, ucol)),
                  pl.BlockSpec((len(B_WINDOWS), B_GROUP, B_GROUP), lambda i: (0, 0, 0)),
                  pl.BlockSpec((1, B_WIDTH), lambda i: (0, 0))],
        out_specs=pl.BlockSpec((tm, B_WIDTH), lambda i: (i, 0)),
        out_shape=jax.ShapeDtypeStruct((proj.shape[0], B_WIDTH), BF16),
        scratch_shapes=[pltpu.VMEM((POOL_HALO + tm, B_WIDTH), F32)],
        compiler_params=_params("parallel"),
    )(proj, proj, pool_w, pool_scale)


def pool_sample(u_ext, pool_w, pool_scale, dst, row0, past):
    nb, tot, _ = u_ext.shape
    ds = tot - POOL_HALO
    return pl.pallas_call(
        functools.partial(_pool_kernel, tm=ds, prompt=False, pos0=past),
        grid=(nb,),
        in_specs=[pl.BlockSpec((None, POOL_HALO, B_WIDTH), lambda b: (b, 0, 0)),
                  pl.BlockSpec((None, ds, B_WIDTH), lambda b: (b, POOL_HALO // ds, 0)),
                  pl.BlockSpec((len(B_WINDOWS), B_GROUP, B_GROUP), lambda b: (0, 0, 0)),
                  pl.BlockSpec((1, B_WIDTH), lambda b: (0, 0)),
                  pl.BlockSpec(memory_space=pl.ANY)],
        out_specs=pl.BlockSpec((ds, B_WIDTH), lambda b: (row0 // ds + b, 0)),
        out_shape=jax.ShapeDtypeStruct(dst.shape, dst.dtype),
        input_output_aliases={4: 0},
        scratch_shapes=[pltpu.VMEM((POOL_HALO + ds, B_WIDTH), F32)],
        compiler_params=_params("parallel"),
    )(u_ext, u_ext, pool_w, pool_scale, dst)


def _mm_res_kernel(*refs, n_in):
    xs, ws = refs[:n_in], refs[n_in:2 * n_in]
    res_ref, o_ref = refs[2 * n_in], refs[2 * n_in + 1]
    acc = res_ref[...]
    for x_ref, w_ref in zip(xs, ws):
        acc = acc + jnp.dot(x_ref[...], w_ref[...], preferred_element_type=F32)
    o_ref[...] = acc


def mm_res(xs, ws, res, tm, tn):
    m, n = res.shape
    n_in = len(xs)
    in_specs = ([pl.BlockSpec((tm, x.shape[1]), lambda i, j: (i, 0)) for x in xs]
                + [pl.BlockSpec((w.shape[0], tn), lambda i, j: (0, j)) for w in ws]
                + [pl.BlockSpec((tm, tn), lambda i, j: (i, j))])
    return pl.pallas_call(
        functools.partial(_mm_res_kernel, n_in=n_in),
        grid=(m // tm, n // tn),
        in_specs=in_specs,
        out_specs=pl.BlockSpec((tm, tn), lambda i, j: (i, j)),
        out_shape=jax.ShapeDtypeStruct((m, n), F32),
        compiler_params=_params("parallel", "arbitrary"),
    )(*xs, *ws, res)


def _swiglu_tile(h, wg, wu, wd):
    a = jnp.dot(h, wg, preferred_element_type=F32)
    b = jnp.dot(h, wu, preferred_element_type=F32)
    act = (a * jax.nn.sigmoid(a) * b).astype(BF16)
    return jnp.dot(act, wd, preferred_element_type=F32)


def _ffn_kernel(y_ref, g_ref, wg_ref, wu_ref, wd_ref, o_ref, h_ref, acc_ref):
    f = pl.program_id(1)

    @pl.when(f == 0)
    def _():
        h_ref[...] = _rms(y_ref[...], g_ref[...]).astype(BF16)
        acc_ref[...] = jnp.zeros_like(acc_ref)

    acc_ref[...] += _swiglu_tile(h_ref[...], wg_ref[...], wu_ref[...], wd_ref[...])

    @pl.when(f == pl.num_programs(1) - 1)
    def _():
        o_ref[...] = y_ref[...] + acc_ref[...]


def ffn(y, g, wg, wu, wd, tm, tf):
    m, d = y.shape
    ff = wg.shape[1]
    return pl.pallas_call(
        _ffn_kernel,
        grid=(m // tm, ff // tf),
        in_specs=[pl.BlockSpec((tm, d), lambda i, f: (i, 0)),
                  pl.BlockSpec((1, d), lambda i, f: (0, 0)),
                  pl.BlockSpec((d, tf), lambda i, f: (0, f)),
                  pl.BlockSpec((d, tf), lambda i, f: (0, f)),
                  pl.BlockSpec((tf, d), lambda i, f: (f, 0))],
        out_specs=pl.BlockSpec((tm, d), lambda i, f: (i, 0)),
        out_shape=jax.ShapeDtypeStruct((m, d), F32),
        scratch_shapes=[pltpu.VMEM((tm, d), BF16), pltpu.VMEM((tm, d), F32)],
        compiler_params=_params("parallel", "arbitrary"),
    )(y, g.reshape(1, d), wg, wu, wd)


def _rope_tables(pos, head_dim):
    rot = head_dim // ROPE_FRAC
    half = rot // 2
    inv = jnp.exp(-math.log(ROPE_THETA) * jnp.arange(half, dtype=F32) * (2.0 / rot))
    ang = pos.astype(F32)[:, None] * inv[None, :]
    cos, sin = jnp.cos(ang), jnp.sin(ang)
    m = pos.shape[0]
    one = jnp.ones((m, head_dim - rot), F32)
    zero_r = jnp.zeros((m, head_dim - rot), F32)
    zero_h = jnp.zeros((m, half), F32)
    c = jnp.concatenate([cos, cos, one], axis=1)
    s_dn = jnp.concatenate([-sin, zero_h, zero_r], axis=1)
    s_up = jnp.concatenate([zero_h, sin, zero_r], axis=1)
    rep = LANES // head_dim
    return jnp.stack([jnp.tile(c, (1, rep)), jnp.tile(s_dn, (1, rep)), jnp.tile(s_up, (1, rep))])


def _rot(x, tab_ref, half):
    return (x * tab_ref[0] + pltpu.roll(x, LANES - half, 1) * tab_ref[1]
            + pltpu.roll(x, half, 1) * tab_ref[2])


def _rope_kernel(main_ref, tail_ref, tq_ref, ti_ref, kv_ref, t32_ref, ob_ref,
                 *, n_q, n_k, n_v, n_qi, wi_scale, q_scale):
    half_qk = C_HEAD_DIM // ROPE_FRAC // 2
    half_i = IDX_DIM // ROPE_FRAC // 2
    for c in range(n_q + n_k + n_v + n_qi):
        cs = slice(c * LANES, (c + 1) * LANES)
        x = main_ref[:, cs]
        if c < n_q + n_k:
            x = _rot(x, tq_ref, half_qk)
        elif c >= n_q + n_k + n_v:
            x = _rot(x, ti_ref, half_i)
        if n_q <= c < n_q + n_k + n_v:
            kv_ref[:, (c - n_q) * LANES:(c - n_q + 1) * LANES] = x
        ob_ref[:, cs] = (x * q_scale if c < n_q else x).astype(BF16)
    t = tail_ref[...]
    lane = lax.broadcasted_iota(jnp.int32, (1, LANES), 1)
    r = _rot(t, ti_ref, half_i)
    t32_ref[...] = jnp.where(lane < IDX_DIM, r, t * wi_scale)
    c = n_q + n_k + n_v + n_qi
    ob_ref[:, c * LANES:(c + 1) * LANES] = jnp.where(lane < IDX_DIM, r, pltpu.roll(r, IDX_DIM, 1)).astype(BF16)


def rope_all(main, tail, tab_qk, tab_idx, tm):
    m, nmain = main.shape
    n_q = C_HEADS * C_HEAD_DIM // LANES
    n_k = C_KV_HEADS * C_HEAD_DIM // LANES
    n_qi = IDX_HEADS * IDX_DIM // LANES
    wi_scale = (IDX_HEADS ** -0.5) * (IDX_DIM ** -0.5)
    q_scale = (C_HEAD_DIM ** -0.5) * math.log2(math.e)
    row = lambda i: (i, 0)
    return pl.pallas_call(
        functools.partial(_rope_kernel, n_q=n_q, n_k=n_k, n_v=n_k, n_qi=n_qi, wi_scale=wi_scale, q_scale=q_scale),
        grid=(m // tm,),
        in_specs=[pl.BlockSpec((tm, nmain), row),
                  pl.BlockSpec((tm, LANES), row),
                  pl.BlockSpec((3, tm, LANES), lambda i: (0, i, 0)),
                  pl.BlockSpec((3, tm, LANES), lambda i: (0, i, 0))],
        out_specs=[pl.BlockSpec((tm, 2 * n_k * LANES), row),
                   pl.BlockSpec((tm, LANES), row),
                   pl.BlockSpec((tm, nmain + LANES), row)],
        out_shape=[jax.ShapeDtypeStruct((m, 2 * n_k * LANES), F32),
                   jax.ShapeDtypeStruct((m, LANES), F32),
                   jax.ShapeDtypeStruct((m, nmain + LANES), BF16)],
        compiler_params=_params("parallel"),
    )(main, tail, tab_qk, tab_idx)


def _dsa_kernel(q_ref, qi_ref, wi_ref, kn_ref, vn_ref, kin_ref, *refs, tq, kb, past, n_valid, q_pos0, topk):
    ck_refs, cv_refs = refs[:C_KV_HEADS], refs[C_KV_HEADS:2 * C_KV_HEADS]
    ci_ref, o_ref, k_ref, v_ref, ki_ref, sc_ref, qs_ref, acc_ref, m_ref, l_ref = refs[2 * C_KV_HEADS:]
    i = 0
    rows = C_GROUPS * tq
    sub = kb // LANES
    kf = float(topk)

    new = past + tq
    ci = ci_ref[...].astype(BF16)
    for g in range(C_KV_HEADS):
        gs = slice(g * C_HEAD_DIM, (g + 1) * C_HEAD_DIM)
        k_ref[0:past, gs] = ck_refs[g][...].astype(BF16)
        v_ref[0:past, gs] = cv_refs[g][...].astype(BF16)
    ki_ref[0:past, :] = jnp.concatenate([ci, ci], axis=1)
    k_ref[past:new, :] = kn_ref[...]
    v_ref[past:new, :] = vn_ref[...]
    ki_ref[past:new, :] = kin_ref[...]
    n_pad = k_ref.shape[0] - new
    k_ref[new:, :] = jnp.zeros((n_pad, k_ref.shape[1]), BF16)
    v_ref[new:, :] = jnp.zeros((n_pad, v_ref.shape[1]), BF16)
    ki_ref[new:, :] = jnp.zeros((n_pad, ki_ref.shape[1]), BF16)

    q_pos = q_pos0 + i * tq + lax.broadcasted_iota(jnp.int32, (tq, 1), 0)
    q_chunk = q_pos // CHUNK
    last_chunk = (q_pos0 + i * tq + tq - 1) // CHUNK
    kv_limit = min(n_valid, (last_chunk + 1) * CHUNK)
    nkb = (kv_limit + kb - 1) // kb

    lane = lax.broadcasted_iota(jnp.int32, (1, LANES), 1)
    first = lane < IDX_DIM
    wi = wi_ref[:, IDX_DIM:IDX_DIM + IDX_HEADS]

    def score_block(b, carry):
        for c in range(sub):
            off = pl.multiple_of(b * kb + c * LANES, LANES)
            kib = ki_ref[pl.ds(off, LANES), :]
            acc = jnp.zeros((tq, LANES), F32)
            for hp in range(IDX_HEADS // 2):
                qp = qi_ref[:, hp * LANES:(hp + 1) * LANES]
                for half in range(2):
                    keep = first if half == 0 else jnp.logical_not(first)
                    qm = jnp.where(keep, qp, jnp.zeros_like(qp))
                    d = lax.dot_general(qm, kib, _NT, preferred_element_type=F32)
                    h = hp * 2 + half
                    acc = acc + jnp.maximum(d, 0.0) * wi[:, h:h + 1]
            k_pos = off + lane
            adm = jnp.logical_and(k_pos // CHUNK <= q_chunk, k_pos < n_valid)
            sc_ref[b, :, c * LANES:(c + 1) * LANES] = jnp.where(adm, acc, NEG)
        return carry

    lax.fori_loop(0, nkb, score_block, 0)

    def lane_sum(x):
        return jnp.sum(x, axis=1, keepdims=True)

    def count_ge(t):
        def body(b, acc):
            for c in range(sub):
                blk = sc_ref[b, :, c * LANES:(c + 1) * LANES]
                acc = acc + jnp.where(blk >= t, 1.0, 0.0)
            return acc
        return lane_sum(lax.fori_loop(0, nkb, body, jnp.zeros((tq, LANES), F32)))

    def stats(b, carry):
        mx, mn, cnt = carry
        for c in range(sub):
            blk = sc_ref[b, :, c * LANES:(c + 1) * LANES]
            ok = blk > 0.5 * NEG
            mx = jnp.maximum(mx, blk)
            mn = jnp.minimum(mn, jnp.where(ok, blk, BIG))
            cnt = cnt + jnp.where(ok, 1.0, 0.0)
        return mx, mn, cnt

    mx, mn, cnt = lax.fori_loop(
        0, nkb, stats,
        (jnp.full((tq, LANES), NEG, F32), jnp.full((tq, LANES), BIG, F32), jnp.zeros((tq, LANES), F32)))
    row_max = jnp.max(mx, axis=1, keepdims=True)
    row_min = jnp.min(mn, axis=1, keepdims=True)
    n_adm = lane_sum(cnt)

    done0 = jnp.where(n_adm <= kf, 1.0, 0.0)
    state0 = (row_min, jnp.full((tq, 1), BIG, F32), row_max, jnp.full((tq, 1), 0.5 * NEG, F32), done0)

    def bisect(state, n_steps):
        def cond(c):
            it, st = c
            return jnp.logical_and(it < n_steps, jnp.min(st[4]) < 0.5)

        def body(c):
            it, (lo, hi, mid, thr, done) = c
            cnt = count_ge(mid)
            live = done < 0.5
            hit = jnp.logical_and(live, cnt == kf)
            ge = cnt >= kf
            thr = jnp.where(hit, mid, thr)
            done = jnp.where(hit, 1.0, done)
            lo = jnp.where(ge, mid, lo)
            hi = jnp.where(ge, hi, mid)
            return it + 1, (lo, hi, 0.5 * (lo + hi), thr, done)

        return lax.while_loop(cond, body, (jnp.int32(0), state))[1]

    def snap(state):
        lo, hi, mid, thr, done = state

        def body(b, carry):
            v_lo, v_hi = carry
            for c in range(sub):
                blk = sc_ref[b, :, c * LANES:(c + 1) * LANES]
                v_lo = jnp.minimum(v_lo, jnp.where(blk >= lo, blk, BIG))
                v_hi = jnp.maximum(v_hi, jnp.where(blk < hi, blk, NEG))
            return v_lo, v_hi

        v_lo, v_hi = lax.fori_loop(0, nkb, body,
                                   (jnp.full((tq, LANES), BIG, F32), jnp.full((tq, LANES), NEG, F32)))
        v_lo = jnp.min(v_lo, axis=1, keepdims=True)
        v_hi = jnp.max(v_hi, axis=1, keepdims=True)
        live = done < 0.5
        tie = jnp.logical_and(live, v_lo == v_hi)
        thr = jnp.where(tie, v_lo, thr)
        done = jnp.where(tie, 1.0, done)
        lo = jnp.where(live, v_lo, lo)
        return (lo, hi, 0.5 * (lo + hi), thr, done), jnp.where(tie, 1.0, 0.0)

    state = bisect(state0, 32)

    def refine_cond(c):
        rounds, st, _ = c
        return jnp.logical_and(rounds < 10, jnp.min(st[4]) < 0.5)

    def refine_body(c):
        rounds, st, tie = c
        st, new_tie = snap(st)
        st = bisect(st, 32)
        return rounds + 1, st, jnp.maximum(tie, new_tie)

    _, state, tie = lax.while_loop(refine_cond, refine_body,
                                   (jnp.int32(0), state, jnp.zeros((tq, 1), F32)))
    thr = state[3]
    any_tie = jnp.max(tie) > 0.5

    @pl.when(jnp.logical_not(any_tie))
    def _():
        def body(b, carry):
            for c in range(sub):
                cs = slice(c * LANES, (c + 1) * LANES)
                sc_ref[b, :, cs] = jnp.where(sc_ref[b, :, cs] >= thr, 0.0, NEG)
            return carry
        lax.fori_loop(0, nkb, body, 0)

    @pl.when(any_tie)
    def _():
        def gt_body(b, acc):
            for c in range(sub):
                acc = acc + jnp.where(sc_ref[b, :, c * LANES:(c + 1) * LANES] > thr, 1.0, 0.0)
            return acc
        need = kf - lane_sum(lax.fori_loop(0, nkb, gt_body, jnp.zeros((tq, LANES), F32)))

        def count_eq_upto(j):
            def body(b, acc):
                for c in range(sub):
                    blk = sc_ref[b, :, c * LANES:(c + 1) * LANES]
                    idx = (b * kb + c * LANES + lane).astype(F32)
                    acc = acc + jnp.where(jnp.logical_and(blk == thr, idx <= j), 1.0, 0.0)
                return acc
            return lane_sum(lax.fori_loop(0, nkb, body, jnp.zeros((tq, LANES), F32)))

        def idx_body(_, c):
            lo_j, hi_j = c
            mid_j = jnp.floor(0.5 * (lo_j + hi_j))
            ok = count_eq_upto(mid_j) >= need
            return jnp.where(ok, lo_j, mid_j), jnp.where(ok, mid_j, hi_j)

        n_steps = max(1, math.ceil(math.log2(sc_ref.shape[0] * kb + 1)))
        _, last = lax.fori_loop(
            0, n_steps, idx_body,
            (jnp.full((tq, 1), -1.0, F32), jnp.full((tq, 1), float(sc_ref.shape[0] * kb), F32)))
        last = jnp.where(tie > 0.5, last, BIG)

        def body(b, carry):
            for c in range(sub):
                cs = slice(c * LANES, (c + 1) * LANES)
                blk = sc_ref[b, :, cs]
                idx = (b * kb + c * LANES + lane).astype(F32)
                sel = jnp.logical_or(blk > thr, jnp.logical_and(blk == thr, idx <= last))
                sc_ref[b, :, cs] = jnp.where(sel, 0.0, NEG)
            return carry
        lax.fori_loop(0, nkb, body, 0)

    for g in range(C_KV_HEADS):
        for hh in range(C_GROUPS):
            h = g * C_GROUPS + hh
            qs_ref[g, hh * tq:(hh + 1) * tq, :] = q_ref[:, h * C_HEAD_DIM:(h + 1) * C_HEAD_DIM]
    m_ref[...] = jnp.full(m_ref.shape, NEG, F32)
    l_ref[...] = jnp.zeros(l_ref.shape, F32)
    acc_ref[...] = jnp.zeros(acc_ref.shape, F32)

    def attend(b, carry):
        off = pl.multiple_of(b * kb, kb)
        bias = sc_ref[b]
        bias = jnp.concatenate([bias] * C_GROUPS, axis=0)
        for g in range(C_KV_HEADS):
            cs = slice(g * C_HEAD_DIM, (g + 1) * C_HEAD_DIM)
            kblk = k_ref[pl.ds(off, kb), cs]
            vblk = v_ref[pl.ds(off, kb), cs]
            s = lax.dot_general(qs_ref[g], kblk, _NT, preferred_element_type=F32)
            s = s + bias
            m_old = m_ref[g]
            m_new = jnp.maximum(m_old, jnp.max(s, axis=1, keepdims=True))
            alpha = jnp.exp2(m_old - m_new)
            p = jnp.exp2(s - m_new)
            l_ref[g] = alpha * l_ref[g] + jnp.sum(p, axis=1, keepdims=True)
            acc_ref[g] = alpha * acc_ref[g] + jnp.dot(p.astype(BF16), vblk, preferred_element_type=F32)
            m_ref[g] = m_new
        return carry

    lax.fori_loop(0, nkb, attend, 0, unroll=True)

    for g in range(C_KV_HEADS):
        o = acc_ref[g] / l_ref[g]
        for hh in range(C_GROUPS):
            h = g * C_GROUPS + hh
            o_ref[:, h * C_HEAD_DIM:(h + 1) * C_HEAD_DIM] = o[hh * tq:(hh + 1) * tq, :].astype(o_ref.dtype)


def dsa_sample(ob, t32, cache_k, cache_v, cache_i, *, row0, kb, topk):
    nb, past = cache_k.shape[:2]
    head = lambda g: pl.BlockSpec((None, past, C_HEAD_DIM), lambda b, i: (b, 0, g))
    heads = [head(g) for g in range(C_KV_HEADS)]
    n_kv = C_KV_HEADS * C_HEAD_DIM
    ds = (ob.shape[0] - row0) // nb
    n_q = C_HEADS * C_HEAD_DIM
    n_qi = IDX_HEADS * IDX_DIM
    s_all = past + ds
    s_pad = -(-s_all // kb) * kb
    rows = C_GROUPS * ds
    col = lambda w, off: (lambda b, i: (row0 // ds + b, off // w))
    cmap = lambda b, i: (b, 0, 0)
    return pl.pallas_call(
        functools.partial(_dsa_kernel, tq=ds, kb=kb, past=past, n_valid=s_all, q_pos0=past, topk=topk),
        grid=(nb, 1),
        in_specs=[pl.BlockSpec((ds, n_q), col(n_q, 0)),
                  pl.BlockSpec((ds, n_qi), col(n_qi, n_q + 2 * n_kv)),
                  pl.BlockSpec((ds, LANES), col(LANES, 0)),
                  pl.BlockSpec((ds, n_kv), col(n_kv, n_q)),
                  pl.BlockSpec((ds, n_kv), col(n_kv, n_q + n_kv)),
                  pl.BlockSpec((ds, LANES), col(LANES, n_q + 2 * n_kv + n_qi))]
                 + heads + heads + [pl.BlockSpec((None, past, IDX_DIM), cmap)],
        out_specs=pl.BlockSpec((ds, n_q), lambda b, i: (b, 0)),
        out_shape=jax.ShapeDtypeStruct((nb * ds, n_q), BF16),
        scratch_shapes=[pltpu.VMEM((s_pad, n_kv), BF16),
                        pltpu.VMEM((s_pad, n_kv), BF16),
                        pltpu.VMEM((s_pad, LANES), BF16),
                        pltpu.VMEM((s_pad // kb, ds, kb), F32),
                        pltpu.VMEM((C_KV_HEADS, rows, C_HEAD_DIM), BF16),
                        pltpu.VMEM((C_KV_HEADS, rows, C_HEAD_DIM), F32),
                        pltpu.VMEM((C_KV_HEADS, rows, 1), F32),
                        pltpu.VMEM((C_KV_HEADS, rows, 1), F32)],
        compiler_params=_params("parallel", "arbitrary"),
    )(ob, ob, t32, ob, ob, ob, *([cache_k] * C_KV_HEADS), *([cache_v] * C_KV_HEADS), cache_i)


def _dsa_t_kernel(q_ref, qi_ref, wi_ref, k_ref, vt_ref, ki_ref, o_ref,
                  sc_ref, qim_ref, acc_ref, m_ref, l_ref, l8_ref, kmax_ref,
                  *, tq, kb, qb, pb, cb, n_valid, q_pos0, topk):
    i = pl.program_id(1)
    kf = float(topk)

    q_pos = q_pos0 + i * tq + lax.broadcasted_iota(jnp.int32, (1, tq), 1)
    q_chunk = q_pos // CHUNK
    last_chunk = (q_pos0 + i * tq + tq - 1) // CHUNK
    kv_limit = jnp.minimum(n_valid, (last_chunk + 1) * CHUNK)
    nkb = (kv_limit + kb - 1) // kb
    ncb = (kv_limit + cb - 1) // cb

    def col_reduce(x, op):
        groups = x.shape[0] // 8
        chains = 8 if groups % 8 == 0 else 1
        return op(op(x.reshape(chains, groups // chains, 8, tq), axis=1), axis=0)

    col_sum = lambda x: col_reduce(x, jnp.sum)
    col_max = lambda x: col_reduce(x, jnp.max)
    col_min = lambda x: col_reduce(x, jnp.min)

    lane = lax.broadcasted_iota(jnp.int32, (1, LANES), 1)
    first = lane < IDX_DIM
    for hp in range(IDX_HEADS // 2):
        qp = qi_ref[:, hp * LANES:(hp + 1) * LANES]
        qim_ref[2 * hp] = jnp.where(first, qp, jnp.zeros_like(qp))
        qim_ref[2 * hp + 1] = jnp.where(first, jnp.zeros_like(qp), qp)
    block_iota = lax.broadcasted_iota(jnp.int32, (pb, 1), 0)

    def score_block(b2, carry):
        for u in range(2):
            off = pl.multiple_of((2 * b2 + u) * pb, pb)
            kib = ki_ref[pl.ds(off, pb), :]
            parts = []
            for h in range(IDX_HEADS):
                d = lax.dot_general(kib, qim_ref[h], _NT, preferred_element_type=F32)
                parts.append(jnp.maximum(d, 0.0) * wi_ref[h:h + 1, :])
            while len(parts) > 1:
                parts = [a + b for a, b in zip(parts[0::2], parts[1::2])]
            acc = parts[0]
            k_pos = off + block_iota
            adm = jnp.logical_and(k_pos // CHUNK <= q_chunk, k_pos < n_valid)
            sc_ref[pl.ds(off, pb), :] = jnp.where(adm, acc, NEG)
        return carry

    lax.fori_loop(0, ncb * (cb // pb) // 2, score_block, 0)
    key_iota = lax.broadcasted_iota(jnp.int32, (cb, 1), 0)

    def blocks(body, init):
        def step(b, carry):
            off = pl.multiple_of(b * cb, cb)
            return body(off, sc_ref[pl.ds(off, cb), :], carry)
        return lax.fori_loop(0, ncb, step, init)

    def total(x):
        return jnp.sum(x, axis=0, keepdims=True)

    def count_ge(t):
        return total(blocks(lambda off, blk, acc: acc + col_sum(jnp.where(blk >= t, 1.0, 0.0)),
                            jnp.zeros((8, tq), F32)))

    def stats(off, blk, carry):
        mx, mn, cnt, pos, nonneg = carry
        ok = blk > 0.5 * NEG
        return (jnp.maximum(mx, col_max(blk)), jnp.minimum(mn, col_min(jnp.where(ok, blk, BIG))),
                cnt + col_sum(jnp.where(ok, 1.0, 0.0)),
                pos + col_sum(jnp.where(blk > 0.0, 1.0, 0.0)),
                nonneg + col_sum(jnp.where(blk >= 0.0, 1.0, 0.0)))

    zeros8 = jnp.zeros((8, tq), F32)
    mx, mn, cnt, pos, nonneg = blocks(stats, (jnp.full((8, tq), NEG, F32), jnp.full((8, tq), BIG, F32),
                                              zeros8, zeros8, zeros8))
    row_max = jnp.max(mx, axis=0, keepdims=True)
    row_min = jnp.min(mn, axis=0, keepdims=True)
    n_adm = total(cnt)
    n_pos = total(pos)
    n_nonneg = total(nonneg)

    open_row = n_adm > kf
    above = n_pos >= kf
    below = n_nonneg < kf
    zero_tie = jnp.logical_and(open_row, jnp.logical_not(jnp.logical_or(above, below)))
    lo0 = jnp.where(above, 0.0, row_min)
    hi0 = jnp.where(below, 0.0, BIG)
    mid0 = jnp.where(below, 0.5 * (lo0 + hi0), row_max)
    thr0 = jnp.where(zero_tie, 0.0, 0.5 * NEG)
    done0 = jnp.where(jnp.logical_or(jnp.logical_not(open_row), zero_tie), 1.0, 0.0)
    tie0 = jnp.where(zero_tie, 1.0, 0.0)
    c_lo0 = jnp.where(above, n_pos, n_adm)
    c_hi0 = jnp.where(below, n_nonneg, 0.0)
    state0 = (lo0, hi0, mid0, thr0, done0, c_lo0, c_hi0)
    log_k = math.log2(kf + 0.5)

    def bisect(state, n_steps):
        def cond(c):
            it, st = c
            return jnp.logical_and(it < n_steps, jnp.min(st[4]) < 0.5)

        def body(c):
            it, (lo, hi, mid, thr, done, c_lo, c_hi) = c
            cnt = count_ge(mid)
            hit = jnp.logical_and(done < 0.5, cnt == kf)
            ge = cnt >= kf
            thr = jnp.where(hit, mid, thr)
            done = jnp.where(hit, 1.0, done)
            lo = jnp.where(ge, mid, lo)
            hi = jnp.where(ge, hi, mid)
            c_lo = jnp.where(ge, cnt, c_lo)
            c_hi = jnp.where(ge, c_hi, cnt)
            log_lo = jnp.log2(jnp.maximum(c_lo, 1.0))
            frac = (log_lo - log_k) / jnp.maximum(log_lo - jnp.log2(jnp.maximum(c_hi, 0.5)), 1e-6)
            frac = jnp.clip(frac, 0.02, 0.98)
            frac = jnp.where(jnp.logical_or(it % 3 == 2, hi >= BIG), 0.5, frac)
            return it + 1, (lo, hi, lo + (hi - lo) * frac, thr, done, c_lo, c_hi)

        return lax.while_loop(cond, body, (jnp.int32(0), state))[1]

    def snap(state):
        lo, hi, mid, thr, done, c_lo, c_hi = state

        def body(off, blk, carry):
            v_lo, v_hi = carry
            return (jnp.minimum(v_lo, col_min(jnp.where(blk >= lo, blk, BIG))),
                    jnp.maximum(v_hi, col_max(jnp.where(blk < hi, blk, NEG))))

        v_lo, v_hi = blocks(body, (jnp.full((8, tq), BIG, F32), jnp.full((8, tq), NEG, F32)))
        v_lo = jnp.min(v_lo, axis=0, keepdims=True)
        v_hi = jnp.max(v_hi, axis=0, keepdims=True)
        live = done < 0.5
        tie = jnp.logical_and(live, v_lo == v_hi)
        thr = jnp.where(tie, v_lo, thr)
        done = jnp.where(tie, 1.0, done)
        lo = jnp.where(live, v_lo, lo)
        return (lo, hi, 0.5 * (lo + hi), thr, done, c_lo, c_hi), jnp.where(tie, 1.0, 0.0)

    state = bisect(state0, 32)

    def refine_cond(c):
        rounds, st, _ = c
        return jnp.logical_and(rounds < 10, jnp.min(st[4]) < 0.5)

    def refine_body(c):
        rounds, st, tie = c
        st, new_tie = snap(st)
        st = bisect(st, 32)
        return rounds + 1, st, jnp.maximum(tie, new_tie)

    _, state, tie = lax.while_loop(refine_cond, refine_body, (jnp.int32(0), state, tie0))
    thr = state[3]
    any_tie = jnp.max(tie) > 0.5

    @pl.when(jnp.logical_not(any_tie))
    def _():
        def body(off, blk, carry):
            sc_ref[pl.ds(off, cb), :] = jnp.where(blk >= thr, 0.0, NEG)
            return carry
        blocks(body, 0)

    @pl.when(any_tie)
    def _():
        need = kf - total(blocks(lambda off, blk, acc: acc + col_sum(jnp.where(blk > thr, 1.0, 0.0)),
                                 jnp.zeros((8, tq), F32)))

        def count_eq_upto(j):
            def body(off, blk, acc):
                idx = (off + key_iota).astype(F32)
                return acc + col_sum(jnp.where(jnp.logical_and(blk == thr, idx <= j), 1.0, 0.0))
            return total(blocks(body, jnp.zeros((8, tq), F32)))

        def idx_cond(c):
            it, (lo_j, hi_j, settled) = c
            return jnp.logical_and(it < n_steps, jnp.min(settled) < 0.5)

        def idx_body(c):
            it, (lo_j, hi_j, settled) = c
            mid_j = jnp.floor(0.5 * (lo_j + hi_j))
            cnt = count_eq_upto(mid_j)
            ok = cnt >= need
            live = settled < 0.5
            lo_j = jnp.where(jnp.logical_and(live, jnp.logical_not(ok)), mid_j, lo_j)
            hi_j = jnp.where(jnp.logical_and(live, ok), mid_j, hi_j)
            settled = jnp.where(jnp.logical_or(cnt == need, hi_j - lo_j <= 1.0), 1.0, settled)
            return it + 1, (lo_j, hi_j, settled)

        n_keys = sc_ref.shape[0]
        n_steps = max(1, math.ceil(math.log2(n_keys + 1))) + 1
        _, (_, last, _) = lax.while_loop(
            idx_cond, idx_body,
            (jnp.int32(0), (jnp.full((1, tq), -1.0, F32), jnp.full((1, tq), float(n_keys), F32),
                            jnp.where(tie > 0.5, 0.0, 1.0))))
        last = jnp.where(tie > 0.5, last, BIG)

        def body(off, blk, carry):
            idx = (off + key_iota).astype(F32)
            sel = jnp.logical_or(blk > thr, jnp.logical_and(blk == thr, idx <= last))
            sc_ref[pl.ds(off, cb), :] = jnp.where(sel, 0.0, NEG)
            return carry
        blocks(body, 0)

    halves = kb // qb

    def head_scores(off, h):
        gs = slice((h // C_GROUPS) * C_HEAD_DIM, (h // C_GROUPS + 1) * C_HEAD_DIM)
        out = []
        for j in range(halves):
            rows = pl.ds(off + j * qb, qb)
            s = lax.dot_general(k_ref[rows, gs], q_ref[:, h * C_HEAD_DIM:(h + 1) * C_HEAD_DIM],
                                _NT, preferred_element_type=F32)
            out.append(s + sc_ref[rows, :])
        return out, gs

    @pl.when(i == 0)
    def _():
        for g in range(C_KV_HEADS):
            gs = slice(g * C_HEAD_DIM, (g + 1) * C_HEAD_DIM)

            def body(b, mx):
                kk = k_ref[pl.ds(pl.multiple_of(b * cb, cb), cb), gs].astype(F32)
                return jnp.maximum(mx, jnp.sum(kk * kk, axis=1, keepdims=True))

            mx = lax.fori_loop(0, k_ref.shape[0] // cb, body, jnp.zeros((cb, 1), F32))
            kmax_ref[g] = jnp.full((1, tq), 1.0, F32) * jnp.max(mx)

    ones = jnp.ones((8, C_HEAD_DIM), BF16)
    for h in range(C_HEADS):
        qh = q_ref[:, h * C_HEAD_DIM:(h + 1) * C_HEAD_DIM].astype(F32)
        qn2 = lax.dot_general(ones, (qh * qh).astype(BF16), _NT, preferred_element_type=F32)[0:1, :]
        m_ref[h] = jnp.sqrt(qn2 * kmax_ref[h // C_GROUPS]) * SHIFT_SLACK
    l8_ref[...] = jnp.zeros(l8_ref.shape, F32)
    acc_ref[...] = jnp.zeros(acc_ref.shape, F32)

    def attend_fixed(b0, n):
        for h in range(C_HEADS):
            lsum, upd = None, None
            for u in range(n):
                b = b0 + u
                ss, gs = head_scores(pl.multiple_of(b * kb, kb), h)
                ps = [jnp.exp2(s - m_ref[h]) for s in ss]
                part = sum(col_sum(p) for p in ps)
                p_all = jnp.concatenate([p.astype(BF16) for p in ps], axis=0)
                pv = jnp.dot(vt_ref[b, gs, :], p_all, preferred_element_type=F32)
                lsum = part if lsum is None else lsum + part
                upd = pv if upd is None else upd + pv
            l8_ref[h] += lsum
            acc_ref[h] += upd

    def four_blocks(t, carry):
        attend_fixed(4 * t, 4)
        return carry

    lax.fori_loop(0, nkb // 4, four_blocks, 0)

    @pl.when(nkb % 4 >= 2)
    def _():
        attend_fixed((nkb // 4) * 4, 2)
    l_min = jnp.full((1, tq), BIG, F32)
    for h in range(C_HEADS):
        l_ref[h] = total(l8_ref[h])
        l_min = jnp.minimum(l_min, l_ref[h])

    @pl.when(jnp.logical_not(jnp.min(l_min) >= MIN_SOFTMAX_SUM))
    def _():
        m_ref[...] = jnp.full(m_ref.shape, NEG, F32)
        l_ref[...] = jnp.zeros(l_ref.shape, F32)
        acc_ref[...] = jnp.zeros(acc_ref.shape, F32)

        def attend(b, carry):
            off = pl.multiple_of(b * kb, kb)
            for h in range(C_HEADS):
                ss, gs = head_scores(off, h)
                m_old = m_ref[h]
                m_new = m_old
                for s in ss:
                    m_new = jnp.maximum(m_new, jnp.max(col_max(s), axis=0, keepdims=True))
                alpha = jnp.exp2(m_old - m_new)
                ps = [jnp.exp2(s - m_new) for s in ss]
                l_ref[h] = alpha * l_ref[h] + total(sum(col_sum(p) for p in ps))
                p_all = jnp.concatenate([p.astype(BF16) for p in ps], axis=0)
                acc_ref[h] = alpha * acc_ref[h] + jnp.dot(vt_ref[b, gs, :], p_all,
                                                          preferred_element_type=F32)
                m_ref[h] = m_new
            return carry

        lax.fori_loop(0, nkb, attend, 0)

    for h in range(C_HEADS):
        o = acc_ref[h] / l_ref[h]
        o_ref[:, h * C_HEAD_DIM:(h + 1) * C_HEAD_DIM] = o.T.astype(o_ref.dtype)


def dsa_t(ob, wi_t, vt, *, nq, tq, kb, qb, pb, cb, topk):
    dq = C_HEADS * C_HEAD_DIM
    n_kv = C_KV_HEADS * C_HEAD_DIM
    n_qi = IDX_HEADS * IDX_DIM
    s_pad = nq
    assert tq % (2 * kb) == 0 and tq % CHUNK == 0 and nq % cb == 0 and cb % pb == 0 and kb % qb == 0
    return pl.pallas_call(
        functools.partial(_dsa_t_kernel, tq=tq, kb=kb, qb=qb, pb=pb, cb=cb, n_valid=nq, q_pos0=0, topk=topk),
        grid=(1, nq // tq),
        in_specs=[pl.BlockSpec((tq, dq), lambda b, i: (i, 0)),
                  pl.BlockSpec((tq, n_qi), lambda b, i: (i, (dq + 2 * n_kv) // n_qi)),
                  pl.BlockSpec((IDX_HEADS, tq), lambda b, i: (0, i)),
                  pl.BlockSpec((nq, n_kv), lambda b, i: (0, dq // n_kv)),
                  pl.BlockSpec(vt.shape, lambda b, i: (0, 0, 0)),
                  pl.BlockSpec((nq, LANES), lambda b, i: (0, (dq + 2 * n_kv + n_qi) // LANES))],
        out_specs=pl.BlockSpec((tq, dq), lambda b, i: (i, 0)),
        out_shape=jax.ShapeDtypeStruct((nq, dq), BF16),
        scratch_shapes=[pltpu.VMEM((s_pad, tq), F32),
                        pltpu.VMEM((IDX_HEADS, tq, LANES), BF16),
                        pltpu.VMEM((C_HEADS, C_HEAD_DIM, tq), F32),
                        pltpu.VMEM((C_HEADS, 1, tq), F32),
                        pltpu.VMEM((C_HEADS, 1, tq), F32),
                        pltpu.VMEM((C_HEADS, 8, tq), F32),
                        pltpu.VMEM((C_KV_HEADS, 1, tq), F32)],
        compiler_params=_params("arbitrary", "arbitrary"),
    )(ob, ob, wi_t, ob, vt, ob)


def _router_kernel(y_ref, g_ref, wr_ref, h_ref, gate_ref):
    h = _rms(y_ref[...], g_ref[...])
    h_ref[...] = h.astype(BF16)
    logits = jnp.dot(h, wr_ref[...], preferred_element_type=F32, precision=lax.Precision.HIGHEST)
    lane = lax.broadcasted_iota(jnp.int32, logits.shape, 1)
    lg = jnp.where(lane < N_EXPERTS, logits, NEG)
    m1 = jnp.max(lg, axis=1, keepdims=True)
    i1 = jnp.min(jnp.where(lg == m1, lane, LANES), axis=1, keepdims=True)
    lg2 = jnp.where(lane == i1, NEG, lg)
    m2 = jnp.max(lg2, axis=1, keepdims=True)
    i2 = jnp.min(jnp.where(lg2 == m2, lane, LANES), axis=1, keepdims=True)
    e = jnp.exp(m2 - m1)
    g1 = 1.0 / (1.0 + e)
    g2 = e / (1.0 + e)
    meta = jnp.where(lane == 0, i1.astype(F32), jnp.where(lane == 1, i2.astype(F32),
                     jnp.where(lane == 2, g1, jnp.where(lane == 3, g2, 0.0))))
    gate_ref[...] = meta


def router(y, g, w_router_pad, tm):
    m, d = y.shape
    return pl.pallas_call(
        _router_kernel,
        grid=(m // tm,),
        in_specs=[pl.BlockSpec((tm, d), lambda i: (i, 0)),
                  pl.BlockSpec((1, d), lambda i: (0, 0)),
                  pl.BlockSpec((d, LANES), lambda i: (0, 0))],
        out_specs=[pl.BlockSpec((tm, d), lambda i: (i, 0)),
                   pl.BlockSpec((tm, LANES), lambda i: (i, 0))],
        out_shape=[jax.ShapeDtypeStruct((m, d), BF16), jax.ShapeDtypeStruct((m, LANES), F32)],
        compiler_params=_params("parallel"),
    )(y, g.reshape(1, d), w_router_pad)


def route_plan(meta, sup):
    m = meta.shape[0]
    n_tiles = (2 * m) // sup + N_EXPERTS
    e_all = jnp.concatenate([meta[:, 0], meta[:, 1]]).astype(jnp.int32)
    onehot = (e_all[:, None] == jnp.arange(N_EXPERTS)[None, :]).astype(jnp.int32)
    rank = jnp.sum((jnp.cumsum(onehot, axis=0) - onehot) * onehot, axis=1)
    counts = jnp.sum(onehot, axis=0)
    n_super = (counts + sup - 1) // sup
    super_end = jnp.cumsum(n_super)
    super_start = super_end - n_super
    pos = (super_start * sup)[e_all] + rank
    tiles = jnp.arange(n_tiles)
    used = super_end[-1]
    t_eff = jnp.minimum(tiles, used - 1)
    tile_expert = jnp.minimum(jnp.searchsorted(super_end, t_eff, side="right"), N_EXPERTS - 1).astype(jnp.int32)
    rows = jnp.clip(counts[tile_expert] - (t_eff - super_start[tile_expert]) * sup, 0, sup)
    tile_rows = jnp.where(tiles < used, rows, 0).astype(jnp.int32)
    token = jnp.concatenate([jnp.arange(m), jnp.arange(m)]).astype(jnp.int32)
    row_token = (jnp.arange(n_tiles * sup, dtype=jnp.int32) % m).at[pos].set(token)
    return pos.astype(jnp.int32), row_token, tile_expert, tile_rows


def _gather_rows_kernel(tok_ref, h_ref, o_ref, sem, *, batch):
    base = pl.program_id(0) * batch

    def issue(j, carry):
        pltpu.make_async_copy(h_ref.at[tok_ref[base + j]], o_ref.at[j], sem).start()
        return carry

    lax.fori_loop(0, batch, issue, 0)
    pltpu.make_async_copy(o_ref, o_ref, sem).wait()


def gather_rows(h3, row_token, batch):
    n_rows = row_token.shape[0]
    blk = (batch,) + h3.shape[1:]
    return pl.pallas_call(
        functools.partial(_gather_rows_kernel, batch=batch),
        grid_spec=pltpu.PrefetchScalarGridSpec(
            num_scalar_prefetch=1,
            grid=(n_rows // batch,),
            in_specs=[pl.BlockSpec(memory_space=pl.ANY)],
            out_specs=pl.BlockSpec(blk, lambda i, tok: (i, 0, 0)),
            scratch_shapes=[pltpu.SemaphoreType.DMA(())]),
        out_shape=jax.ShapeDtypeStruct((n_rows,) + h3.shape[1:], h3.dtype),
        compiler_params=_params("arbitrary"),
    )(row_token, h3)


def _grouped_ffn_kernel(te_ref, tr_ref, x_ref, wg_ref, wu_ref, wd_ref, o_ref,
                        wgb_ref, wub_ref, wdb_ref, *, sup, sub):
    t = pl.program_id(0)
    f = pl.program_id(1)
    rows = tr_ref[t]

    @pl.when(rows > 0)
    def _():
        wgb_ref[...] = wg_ref[...].astype(BF16)
        wub_ref[...] = wu_ref[...].astype(BF16)
        wdb_ref[...] = wd_ref[...].astype(BF16)

    for s in range(sup // sub):
        sl = slice(s * sub, (s + 1) * sub)

        @pl.when(jnp.logical_and(s * sub >= rows, f == 0))
        def _():
            o_ref[sl, :] = jnp.zeros((sub, o_ref.shape[1]), F32)

        @pl.when(s * sub < rows)
        def _():
            part = _swiglu_tile(x_ref[sl, :], wgb_ref[...], wub_ref[...], wdb_ref[...])

            @pl.when(f == 0)
            def _():
                o_ref[sl, :] = part

            @pl.when(f > 0)
            def _():
                o_ref[sl, :] += part


def grouped_ffn(xs, tile_expert, tile_rows, wg, wu, wd, sup, sub, tf):
    n_rows, d = xs.shape
    ff = wg.shape[2]
    n_f = ff // tf
    fidx = lambda t, f, te, tr: jnp.where(tr[t] > 0, f, n_f - 1)
    return pl.pallas_call(
        functools.partial(_grouped_ffn_kernel, sup=sup, sub=sub),
        grid_spec=pltpu.PrefetchScalarGridSpec(
            num_scalar_prefetch=2,
            grid=(n_rows // sup, n_f),
            in_specs=[pl.BlockSpec((sup, d), lambda t, f, te, tr: (t, 0), pipeline_mode=pl.Buffered(1)),
                      pl.BlockSpec((None, d, tf), lambda t, f, te, tr: (te[t], 0, fidx(t, f, te, tr))),
                      pl.BlockSpec((None, d, tf), lambda t, f, te, tr: (te[t], 0, fidx(t, f, te, tr))),
                      pl.BlockSpec((None, tf, d), lambda t, f, te, tr: (te[t], fidx(t, f, te, tr), 0))],
            out_specs=pl.BlockSpec((sup, d), lambda t, f, te, tr: (t, 0), pipeline_mode=pl.Buffered(1)),
            scratch_shapes=[pltpu.VMEM((d, tf), BF16), pltpu.VMEM((d, tf), BF16), pltpu.VMEM((tf, d), BF16)]),
        out_shape=jax.ShapeDtypeStruct((n_rows, d), F32),
        compiler_params=_params("arbitrary", "arbitrary"),
    )(tile_expert, tile_rows, xs, wg, wu, wd)


def _combine_kernel(pos_ref, y_ref, meta_ref, g_ref, ys_ref, op_ref, os_ref, buf_ref, x_ref, sem,
                    *, n_tok, tmc, prompt_tiles):
    i = pl.program_id(0)
    slot = i % 2

    def issue(tile, into):
        def body(j, carry):
            t = tile * tmc + j
            pltpu.make_async_copy(ys_ref.at[pos_ref[t]], buf_ref.at[into, 0, j], sem.at[into]).start()
            pltpu.make_async_copy(ys_ref.at[pos_ref[n_tok + t]], buf_ref.at[into, 1, j], sem.at[into]).start()
            return carry
        lax.fori_loop(0, tmc, body, 0)

    @pl.when(i == 0)
    def _():
        issue(0, 0)

    @pl.when(i + 1 < pl.num_programs(0))
    def _():
        issue(i + 1, 1 - slot)

    for e in range(2):
        pltpu.make_async_copy(buf_ref.at[slot, e], buf_ref.at[slot, e], sem.at[slot]).wait()
    g1 = meta_ref[:, 2:3]
    g2 = meta_ref[:, 3:4]
    ss = jnp.zeros((tmc, 1), F32)
    for c in range(buf_ref.shape[3]):
        cs = slice(c * LANES, (c + 1) * LANES)
        x = y_ref[:, cs] + (buf_ref[slot, 0, :, c, :] * g1 + buf_ref[slot, 1, :, c, :] * g2)
        ss = ss + jnp.sum(x * x, axis=1, keepdims=True)
        x_ref[:, cs] = x
    inv = lax.rsqrt(ss / x_ref.shape[1] + NORM_EPS)

    @pl.when(i < prompt_tiles)
    def _():
        op_ref[...] = x_ref[...] * inv * g_ref[...]

    @pl.when(i >= prompt_tiles)
    def _():
        os_ref[...] = x_ref[...] * inv * g_ref[...]


def combine_final(y, meta, g_final, ys3, pos, n_prompt, tmc):
    m, d = y.shape
    slab = ys3.shape[1:]
    prompt_tiles = n_prompt // tmc
    return pl.pallas_call(
        functools.partial(_combine_kernel, n_tok=m, tmc=tmc, prompt_tiles=prompt_tiles),
        grid_spec=pltpu.PrefetchScalarGridSpec(
            num_scalar_prefetch=1,
            grid=(m // tmc,),
            in_specs=[pl.BlockSpec((tmc, d), lambda i, pos: (i, 0)),
                      pl.BlockSpec((tmc, LANES), lambda i, pos: (i, 0)),
                      pl.BlockSpec((1, d), lambda i, pos: (0, 0)),
                      pl.BlockSpec(memory_space=pl.ANY)],
            out_specs=[pl.BlockSpec((tmc, d), lambda i, pos: (jnp.minimum(i, prompt_tiles - 1), 0)),
                       pl.BlockSpec((tmc, d), lambda i, pos: (jnp.maximum(i - prompt_tiles, 0), 0))],
            scratch_shapes=[pltpu.VMEM((2, 2, tmc) + slab, F32), pltpu.VMEM((tmc, d), F32),
                            pltpu.SemaphoreType.DMA((2,))]),
        out_shape=[jax.ShapeDtypeStruct((n_prompt, d), F32), jax.ShapeDtypeStruct((m - n_prompt, d), F32)],
        compiler_params=_params("arbitrary"),
    )(pos, y, meta, g_final.reshape(1, d), ys3)


def kernel(x_prompt, x_sample, cache_a_k, cache_a_v, state_pool, cache_c_k, cache_c_v, cache_c_idx,
           norm_mix, norm_ffn, norm_final, w_in_even, w_out_even, a_rel_bias, pool_w, pool_scale,
           ffn_w_gate, ffn_w_up, ffn_w_down, w_in_odd, w_out_odd,
           moe_router, moe_w_gate, moe_w_up, moe_w_down):
    nbp, lp, d = x_prompt.shape
    nb, ds, _ = x_sample.shape
    past = cache_c_k.shape[2]
    a_len = cache_a_k.shape[2]
    assert nbp == 1 and lp % 512 == 0 and (nb * ds) % 512 == 0 and ds == POOL_HALO and past >= POOL_HALO
    ns = nb * ds
    m = lp + ns
    tm = 512
    bf = lambda t: t.astype(BF16)

    proj0, x = stack_norm_proj(x_prompt.reshape(lp, d), x_sample.reshape(ns, d), norm_mix[0],
                               bf(w_in_even[0]), tm, 1024)
    k0 = proj0[:, A_WIDTH:2 * A_WIDTH]
    v0 = proj0[:, 2 * A_WIDTH:3 * A_WIDTH]
    u0 = proj0[:, 3 * A_WIDTH:]

    pad = A_PREV_CHUNKS * CHUNK
    bias_p = _rel_bias_tile(a_rel_bias[0], 0, CHUNK, -pad, A_BAND)
    a_p = band_prompt(proj0, _pair_rows(bias_p), lp)

    k_pos = past - a_len + jnp.arange(a_len + ds)
    q_pos = past + jnp.arange(ds)
    qch, kch = q_pos // CHUNK, k_pos // CHUNK
    ok = ((k_pos[None, :] >= 0) & (kch[None, :] <= qch[:, None])
          & (kch[None, :] >= qch[:, None] - A_PREV_CHUNKS))
    bias_s = jnp.where(ok[None], _rel_bias_tile(a_rel_bias[0], past, ds, past - a_len, a_len + ds), NEG)
    a = band_sample(proj0, cache_a_k[0].reshape(nb, a_len, A_WIDTH),
                    cache_a_v[0].reshape(nb, a_len, A_WIDTH), _pair_rows(bias_s), a_p, lp, nb, ds)

    u_s = u0[lp:].reshape(nb, ds, B_WIDTH)
    u_hist = jnp.concatenate([state_pool[0], u_s], axis=1)
    u_ext = jnp.concatenate([jnp.zeros((nb, POOL_HALO - B_HIST, B_WIDTH), F32), u_hist], axis=1)
    pw = bf(pool_w[0])
    ps = pool_scale[0].reshape(1, B_WIDTH)
    p_p = pool_prompt(proj0, pw, ps, lp, tm)
    p = pool_sample(u_ext, pw, ps, p_p, lp, past)
    wo = bf(w_out_even[0])
    y = mm_res([a, p], [wo[:A_WIDTH], wo[A_WIDTH:]], x, tm, d)
    y = ffn(y, norm_ffn[0], bf(ffn_w_gate[0]), bf(ffn_w_up[0]), bf(ffn_w_down[0]), tm, 512)

    n_q = C_HEADS * C_HEAD_DIM
    n_kv = C_KV_HEADS * C_HEAD_DIM
    n_qi = IDX_HEADS * IDX_DIM
    n_main = n_q + 2 * n_kv + n_qi
    w1 = w_in_odd[0]
    w_tail = jnp.pad(w1[:, n_main:], ((0, 0), (0, LANES - (w1.shape[1] - n_main))))
    main = norm_proj(y, norm_mix[1], bf(w1[:, :n_main]), tm, n_main // 2)
    tail = norm_proj(y, norm_mix[1], bf(w_tail), tm, LANES)
    pos = jnp.concatenate([jnp.arange(lp), jnp.tile(past + jnp.arange(ds), nb)])
    kv1, t32, ob = rope_all(main, tail, _rope_tables(pos, C_HEAD_DIM), _rope_tables(pos, IDX_DIM), tm)
    k1 = kv1[:, :n_kv]
    v1 = kv1[:, n_kv:]
    ki1 = t32[:, :IDX_DIM]

    kb_p = 128
    vt = ob[:lp, n_q + n_kv:n_q + 2 * n_kv].reshape(lp // kb_p, kb_p, n_kv).transpose(0, 2, 1)
    wi_t = t32[:lp, IDX_DIM:IDX_DIM + IDX_HEADS].T
    o_p = dsa_t(ob, wi_t, vt, nq=lp, tq=256, kb=kb_p, qb=128, pb=256, cb=512, topk=min(TOPK_MAX, lp // 4))
    o_s = dsa_sample(ob, t32, cache_c_k[0].reshape(nb, past, n_kv), cache_c_v[0].reshape(nb, past, n_kv), cache_c_idx[0],
                     row0=lp, kb=3 * LANES, topk=min(TOPK_MAX, (past + ds) // 4))
    o = jnp.concatenate([o_p, o_s], axis=0)
    y = mm_res([o], [bf(w_out_odd[0])], y, tm, d)

    wr = jnp.pad(moe_router[0], ((0, 0), (0, LANES - N_EXPERTS)))
    h, meta = router(y, norm_ffn[1], wr, tm)
    sup = 1536
    pos_rows, row_token, tile_expert, tile_rows = route_plan(meta, sup)
    n_rows = row_token.shape[0]
    slab = (d // LANES, LANES)
    xs = gather_rows(h.reshape((m,) + slab), row_token, sup)
    ys = grouped_ffn(xs.reshape(n_rows, d), tile_expert, tile_rows,
                     moe_w_gate[0], moe_w_up[0], moe_w_down[0], sup, 256, 512)
    y_p, y_s = combine_final(y, meta, norm_final, ys.reshape((n_rows,) + slab), pos_rows, lp, 256)

    y_prompt = y_p.reshape(1, lp, d)
    y_sample = y_s.reshape(nb, ds, d)
    keep = min(A_BAND, lp)
    heads = lambda t, n: t.reshape(1, n, -1, A_HEADS, A_HEAD_DIM)
    a_k_prompt = heads(k0[lp - keep:lp], 1)
    a_v_prompt = heads(v0[lp - keep:lp], 1)
    pool_prompt_out = u0[lp - B_HIST:lp].reshape(1, 1, B_HIST, B_WIDTH)
    c_k_prompt = k1[:lp].reshape(1, 1, lp, C_KV_HEADS, C_HEAD_DIM)
    c_v_prompt = v1[:lp].reshape(1, 1, lp, C_KV_HEADS, C_HEAD_DIM)
    c_idx_prompt = ki1[:lp].reshape(1, 1, lp, IDX_DIM)
    shift = lambda cache, new: jnp.concatenate(
        [cache[0], new.reshape(nb, ds, A_HEADS, A_HEAD_DIM)], axis=1)[:, ds:][None]
    a_k_sample = shift(cache_a_k, k0[lp:])
    a_v_sample = shift(cache_a_v, v0[lp:])
    pool_sample_out = u_hist[:, ds:][None]
    c_k_sample = k1[lp:].reshape(1, nb, ds, C_KV_HEADS, C_HEAD_DIM)
    c_v_sample = v1[lp:].reshape(1, nb, ds, C_KV_HEADS, C_HEAD_DIM)
    c_idx_sample = ki1[lp:].reshape(1, nb, ds, IDX_DIM)
    return (y_prompt, y_sample, a_k_prompt, a_v_prompt, pool_prompt_out,
            c_k_prompt, c_v_prompt, c_idx_prompt,
            a_k_sample, a_v_sample, pool_sample_out,
            c_k_sample, c_v_sample, c_idx_sample)
```

```python
import functools
import math

import jax
import jax.numpy as jnp
from jax import lax
from jax.experimental import pallas as pl
from jax.experimental.pallas import tpu as pltpu

F32 = jnp.float32
BF16 = jnp.bfloat16

NORM_EPS = 1e-6
NEG = -1e30
BIG = 1e30
SHIFT_SLACK = 1.01
MIN_SOFTMAX_SUM = 2.0 ** -80

CHUNK = 64
A_HEADS = 16
A_HEAD_DIM = 64
A_WIDTH = A_HEADS * A_HEAD_DIM
A_PREV_CHUNKS = 8
A_BAND = (A_PREV_CHUNKS + 1) * CHUNK
A_REL_CLIP = 128
B_WINDOWS = (2, 4, 8, 16)
B_GROUP = 256
B_WIDTH = B_GROUP * len(B_WINDOWS)
B_HIST = max(B_WINDOWS) - 1
C_HEADS = 16
C_KV_HEADS = 4
C_HEAD_DIM = 128
C_GROUPS = C_HEADS // C_KV_HEADS
IDX_HEADS = 8
IDX_DIM = 64
TOPK_MAX = 256
ROPE_THETA = 500000.0
ROPE_FRAC = 4
N_EXPERTS = 8

LANES = 128
POOL_HALO = 16
BAND_CHUNKS_PER_TRIP = 2
VMEM_LIMIT = 56 * 1024 * 1024

_NT = (((1,), (1,)), ((), ()))


def _params(*sem):
    return pltpu.CompilerParams(dimension_semantics=sem, vmem_limit_bytes=VMEM_LIMIT)


def _rms(x, g):
    ms = jnp.mean(x * x, axis=-1, keepdims=True)
    return x * lax.rsqrt(ms + NORM_EPS) * g


def _norm_proj_kernel(x_ref, g_ref, w_ref, o_ref, h_ref):
    @pl.when(pl.program_id(1) == 0)
    def _():
        h_ref[...] = _rms(x_ref[...], g_ref[...]).astype(BF16)

    o_ref[...] = jnp.dot(h_ref[...], w_ref[...], preferred_element_type=F32)


def norm_proj(x, g, w, tm, tn):
    m, d = x.shape
    n = w.shape[1]
    return pl.pallas_call(
        _norm_proj_kernel,
        grid=(m // tm, n // tn),
        in_specs=[pl.BlockSpec((tm, d), lambda i, j: (i, 0)),
                  pl.BlockSpec((1, d), lambda i, j: (0, 0)),
                  pl.BlockSpec((d, tn), lambda i, j: (0, j))],
        out_specs=pl.BlockSpec((tm, tn), lambda i, j: (i, j)),
        out_shape=jax.ShapeDtypeStruct((m, n), F32),
        scratch_shapes=[pltpu.VMEM((tm, d), BF16)],
        compiler_params=_params("parallel", "arbitrary"),
    )(x, g.reshape(1, d), w)


def _stack_norm_proj_kernel(xp_ref, xs_ref, g_ref, w_ref, o_ref, x_ref, h_ref, *, prompt_tiles):
    i = pl.program_id(0)

    def first_column(src_ref):
        x = src_ref[...]
        x_ref[...] = x
        h_ref[...] = _rms(x, g_ref[...]).astype(BF16)

    @pl.when(jnp.logical_and(pl.program_id(1) == 0, i < prompt_tiles))
    def _():
        first_column(xp_ref)

    @pl.when(jnp.logical_and(pl.program_id(1) == 0, i >= prompt_tiles))
    def _():
        first_column(xs_ref)

    o_ref[...] = jnp.dot(h_ref[...], w_ref[...], preferred_element_type=F32)


def stack_norm_proj(xp, xs, g, w, tm, tn):
    lp, d = xp.shape
    m = lp + xs.shape[0]
    n = w.shape[1]
    prompt_tiles = lp // tm
    return pl.pallas_call(
        functools.partial(_stack_norm_proj_kernel, prompt_tiles=prompt_tiles),
        grid=(m // tm, n // tn),
        in_specs=[pl.BlockSpec((tm, d), lambda i, j: (jnp.minimum(i, prompt_tiles - 1), 0)),
                  pl.BlockSpec((tm, d), lambda i, j: (jnp.maximum(i - prompt_tiles, 0), 0)),
                  pl.BlockSpec((1, d), lambda i, j: (0, 0)),
                  pl.BlockSpec((d, tn), lambda i, j: (0, j))],
        out_specs=[pl.BlockSpec((tm, tn), lambda i, j: (i, j)),
                   pl.BlockSpec((tm, d), lambda i, j: (i, 0))],
        out_shape=[jax.ShapeDtypeStruct((m, n), F32), jax.ShapeDtypeStruct((m, d), F32)],
        scratch_shapes=[pltpu.VMEM((tm, d), BF16)],
        compiler_params=_params("parallel", "arbitrary"),
    )(xp, xs, g.reshape(1, d), w)


def _band_pairs(q_ref, nq, kw_ref, vw_ref, nk, bias_ref, windows, o_ref):
    lane = lax.broadcasted_iota(jnp.int32, (1, LANES), 1)
    first = lane < A_HEAD_DIM
    for hp in range(A_HEADS // 2):
        cs = slice(hp * LANES, (hp + 1) * LANES)
        for q_row0, k_row0, key_ok in windows:
            qp = q_ref[pl.ds(q_row0, nq), cs]
            kp = kw_ref[pl.ds(k_row0, nk), cs]
            vp = vw_ref[pl.ds(k_row0, nk), cs]
            qm = jnp.concatenate([jnp.where(first, qp, 0.0), jnp.where(first, 0.0, qp)], axis=0).astype(BF16)
            s = lax.dot_general(qm, kp, _NT, preferred_element_type=F32)
            s = s * (A_HEAD_DIM ** -0.5) + bias_ref[hp]
            if key_ok is not None:
                s = jnp.where(key_ok, s, NEG)
            mx = jnp.max(s, axis=-1, keepdims=True)
            e = jnp.exp(s - mx)
            l = jnp.sum(e, axis=-1, keepdims=True)
            o = jnp.dot(e.astype(BF16), vp, preferred_element_type=F32) / l
            o_ref[pl.ds(q_row0, nq), cs] = jnp.where(first, o[:nq], o[nq:]).astype(o_ref.dtype)


def _band_prompt_kernel(q_ref, kp_ref, kc_ref, vp_ref, vc_ref, bias_ref, o_ref, kw_ref, vw_ref,
                        *, qb, pad, prompt_blocks):
    i = pl.program_id(0)

    @pl.when(i >= prompt_blocks)
    def _():
        o_ref[...] = jnp.zeros(o_ref.shape, o_ref.dtype)

    @pl.when(i < prompt_blocks)
    def _():
        _band_prompt_block(i, q_ref, kp_ref, kc_ref, vp_ref, vc_ref, bias_ref, o_ref, kw_ref, vw_ref, qb, pad)


def _band_prompt_block(i, q_ref, kp_ref, kc_ref, vp_ref, vc_ref, bias_ref, o_ref, kw_ref, vw_ref, qb, pad):
    kw_ref[0:pad, :] = kp_ref[...].astype(BF16)
    kw_ref[pad:pad + qb, :] = kc_ref[...].astype(BF16)
    vw_ref[0:pad, :] = vp_ref[...].astype(BF16)
    vw_ref[pad:pad + qb, :] = vc_ref[...].astype(BF16)
    col = lax.broadcasted_iota(jnp.int32, (1, A_BAND), 1)

    def chunks(c2, carry):
        windows = []
        for u in range(BAND_CHUNKS_PER_TRIP):
            cc = c2 * BAND_CHUNKS_PER_TRIP + u
            r0 = pl.multiple_of(cc * CHUNK, CHUNK)
            first_valid = jnp.where(i == 0, pad - cc * CHUNK, 0)
            windows.append((r0, r0, col >= first_valid))
        _band_pairs(q_ref, CHUNK, kw_ref, vw_ref, A_BAND, bias_ref, windows, o_ref)
        return carry

    lax.fori_loop(0, qb // CHUNK // BAND_CHUNKS_PER_TRIP, chunks, 0)


def band_prompt(proj, bias, lp):
    pad = A_PREV_CHUNKS * CHUNK
    qb = pad
    blk = (qb, A_WIDTH)
    prev = lambda c: (lambda i: (jnp.maximum(i - 1, 0), c))
    cur = lambda c: (lambda i: (i, c))
    return pl.pallas_call(
        functools.partial(_band_prompt_kernel, qb=qb, pad=pad, prompt_blocks=lp // qb),
        grid=(proj.shape[0] // qb,),
        in_specs=[pl.BlockSpec(blk, cur(0)),
                  pl.BlockSpec(blk, prev(1)), pl.BlockSpec(blk, cur(1)),
                  pl.BlockSpec(blk, prev(2)), pl.BlockSpec(blk, cur(2)),
                  pl.BlockSpec((A_HEADS // 2, 2 * CHUNK, A_BAND), lambda i: (0, 0, 0))],
        out_specs=pl.BlockSpec(blk, lambda i: (i, 0)),
        out_shape=jax.ShapeDtypeStruct((proj.shape[0], A_WIDTH), BF16),
        scratch_shapes=[pltpu.VMEM((pad + qb, A_WIDTH), BF16), pltpu.VMEM((pad + qb, A_WIDTH), BF16)],
        compiler_params=_params("parallel"),
    )(proj, proj, proj, proj, proj, bias)


def _band_sample_kernel(q_ref, kn_ref, vn_ref, ck_ref, cv_ref, bias_ref, dst_ref, o_ref, kw_ref, vw_ref,
                        *, a_len, ds):
    del dst_ref
    kw_ref[0:a_len, :] = ck_ref[...].astype(BF16)
    kw_ref[a_len:a_len + ds, :] = kn_ref[...].astype(BF16)
    vw_ref[0:a_len, :] = cv_ref[...].astype(BF16)
    vw_ref[a_len:a_len + ds, :] = vn_ref[...].astype(BF16)
    _band_pairs(q_ref, ds, kw_ref, vw_ref, a_len + ds, bias_ref, [(0, 0, None)], o_ref)


def band_sample(proj, cache_k, cache_v, bias, dst, lp, nb, ds):
    a_len = cache_k.shape[1]
    row = lambda c: (lambda b: (lp // ds + b, c))
    return pl.pallas_call(
        functools.partial(_band_sample_kernel, a_len=a_len, ds=ds),
        grid=(nb,),
        in_specs=[pl.BlockSpec((ds, A_WIDTH), row(0)),
                  pl.BlockSpec((ds, A_WIDTH), row(1)),
                  pl.BlockSpec((ds, A_WIDTH), row(2)),
                  pl.BlockSpec((None, a_len, A_WIDTH), lambda b: (b, 0, 0)),
                  pl.BlockSpec((None, a_len, A_WIDTH), lambda b: (b, 0, 0)),
                  pl.BlockSpec((A_HEADS // 2, 2 * ds, a_len + ds), lambda b: (0, 0, 0)),
                  pl.BlockSpec(memory_space=pl.ANY)],
        out_specs=pl.BlockSpec((ds, A_WIDTH), row(0)),
        out_shape=jax.ShapeDtypeStruct(dst.shape, dst.dtype),
        input_output_aliases={6: 0},
        scratch_shapes=[pltpu.VMEM((a_len + ds, A_WIDTH), BF16), pltpu.VMEM((a_len + ds, A_WIDTH), BF16)],
        compiler_params=_params("parallel"),
    )(proj, proj, proj, cache_k, cache_v, bias, dst)


def _rel_bias_tile(rel_bias, q0, nq, k0, nk):
    rel_max = q0 - k0 + nq - 1
    rel = jnp.clip(rel_max - jnp.arange(nq + nk - 1), -A_REL_CLIP, A_REL_CLIP) + A_REL_CLIP
    ext = rel_bias[:, rel].astype(F32)
    return jnp.stack([ext[:, nq - 1 - i:nq - 1 - i + nk] for i in range(nq)], axis=1)


def _pair_rows(bias):
    h, nq, nk = bias.shape
    return bias.reshape(h // 2, 2 * nq, nk)


def _pool_kernel(prev_ref, cur_ref, w_ref, sc_ref, *refs, tm, prompt, pos0, prompt_blocks=None):
    o_ref, ext_ref = refs[-2:]
    i = pl.program_id(0)
    if prompt_blocks is None:
        _pool_block(i, prev_ref, cur_ref, w_ref, sc_ref, o_ref, ext_ref, tm, prompt, pos0)
        return

    @pl.when(i >= prompt_blocks)
    def _():
        o_ref[...] = jnp.zeros(o_ref.shape, o_ref.dtype)

    @pl.when(i < prompt_blocks)
    def _():
        _pool_block(i, prev_ref, cur_ref, w_ref, sc_ref, o_ref, ext_ref, tm, prompt, pos0)


def _pool_block(i, prev_ref, cur_ref, w_ref, sc_ref, o_ref, ext_ref, tm, prompt, pos0):
    prev = prev_ref[...]
    if prompt:
        prev = jnp.where(i == 0, 0.0, prev)
        pos = i * tm + lax.broadcasted_iota(jnp.int32, (tm, 1), 0)
    else:
        pos = pos0 + lax.broadcasted_iota(jnp.int32, (tm, 1), 0)
    ext_ref[0:POOL_HALO, :] = prev
    ext_ref[POOL_HALO:POOL_HALO + tm, :] = cur_ref[...]
    for g, w in enumerate(B_WINDOWS):
        cs = slice(g * B_GROUP, (g + 1) * B_GROUP)
        tok = ext_ref[POOL_HALO:POOL_HALO + tm, cs]
        tot = tok
        for j in range(1, w):
            tot = tot + ext_ref[POOL_HALO - j:POOL_HALO - j + tm, cs]
        cnt = jnp.minimum(pos + 1, w).astype(F32)
        pooled = (tot / cnt - tok).astype(BF16)
        o = jnp.dot(pooled, w_ref[g], preferred_element_type=F32) * sc_ref[:, cs]
        o_ref[:, cs] = o.astype(o_ref.dtype)


def pool_prompt(proj, pool_w, pool_scale, lp, tm):
    ucol = 3 * A_WIDTH // B_WIDTH
    per = tm // POOL_HALO
    return pl.pallas_call(
        functools.partial(_pool_kernel, tm=tm, prompt=True, pos0=0, prompt_blocks=lp // tm),
        grid=(proj.shape[0] // tm,),
        in_specs=[pl.BlockSpec((POOL_HALO, B_WIDTH), lambda i: (jnp.maximum(i * per - 1, 0), ucol)),
                  pl.BlockSpec((tm, B_WIDTH), lambda i: (i, ucol)),
                  pl.BlockSpec((len(B_WINDOWS), B_GROUP, B_GROUP), lambda i: (0, 0, 0)),
                  pl.BlockSpec((1, B_WIDTH), lambda i: (0, 0))],
        out_specs=pl.BlockSpec((tm, B_WIDTH), lambda i: (i, 0)),
        out_shape=jax.ShapeDtypeStruct((proj.shape[0], B_WIDTH), BF16),
        scratch_shapes=[pltpu.VMEM((POOL_HALO + tm, B_WIDTH), F32)],
        compiler_params=_params("parallel"),
    )(proj, proj, pool_w, pool_scale)


def pool_sample(u_ext, pool_w, pool_scale, dst, row0, past):
    nb, tot, _ = u_ext.shape
    ds = tot - POOL_HALO
    return pl.pallas_call(
        functools.partial(_pool_kernel, tm=ds, prompt=False, pos0=past),
        grid=(nb,),
        in_specs=[pl.BlockSpec((None, POOL_HALO, B_WIDTH), lambda b: (b, 0, 0)),
                  pl.BlockSpec((None, ds, B_WIDTH), lambda b: (b, POOL_HALO // ds, 0)),
                  pl.BlockSpec((len(B_WINDOWS), B_GROUP, B_GROUP), lambda b: (0, 0, 0)),
                  pl.BlockSpec((1, B_WIDTH), lambda b: (0, 0)),
                  pl.BlockSpec(memory_space=pl.ANY)],
        out_specs=pl.BlockSpec((ds, B_WIDTH), lambda b: (row0 // ds + b, 0)),
        out_shape=jax.ShapeDtypeStruct(dst.shape, dst.dtype),
        input_output_aliases={4: 0},
        scratch_shapes=[pltpu.VMEM((POOL_HALO + ds, B_WIDTH), F32)],
        compiler_params=_params("parallel"),
    )(u_ext, u_ext, pool_w, pool_scale, dst)


def _mm_res_kernel(*refs, n_in):
    xs, ws = refs[:n_in], refs[n_in:2 * n_in]
    res_ref, o_ref = refs[2 * n_in], refs[2 * n_in + 1]
    acc = res_ref[...]
    for x_ref, w_ref in zip(xs, ws):
        acc = acc + jnp.dot(x_ref[...], w_ref[...], preferred_element_type=F32)
    o_ref[...] = acc


def mm_res(xs, ws, res, tm, tn):
    m, n = res.shape
    n_in = len(xs)
    in_specs = ([pl.BlockSpec((tm, x.shape[1]), lambda i, j: (i, 0)) for x in xs]
                + [pl.BlockSpec((w.shape[0], tn), lambda i, j: (0, j)) for w in ws]
                + [pl.BlockSpec((tm, tn), lambda i, j: (i, j))])
    return pl.pallas_call(
        functools.partial(_mm_res_kernel, n_in=n_in),
        grid=(m // tm, n // tn),
        in_specs=in_specs,
        out_specs=pl.BlockSpec((tm, tn), lambda i, j: (i, j)),
        out_shape=jax.ShapeDtypeStruct((m, n), F32),
        compiler_params=_params("parallel", "arbitrary"),
    )(*xs, *ws, res)


def _swiglu_tile(h, wg, wu, wd):
    a = jnp.dot(h, wg, preferred_element_type=F32)
    b = jnp.dot(h, wu, preferred_element_type=F32)
    act = (a * jax.nn.sigmoid(a) * b).astype(BF16)
    return jnp.dot(act, wd, preferred_element_type=F32)


def _ffn_kernel(y_ref, g_ref, wg_ref, wu_ref, wd_ref, o_ref, h_ref, acc_ref):
    f = pl.program_id(1)

    @pl.when(f == 0)
    def _():
        h_ref[...] = _rms(y_ref[...], g_ref[...]).astype(BF16)
        acc_ref[...] = jnp.zeros_like(acc_ref)

    acc_ref[...] += _swiglu_tile(h_ref[...], wg_ref[...], wu_ref[...], wd_ref[...])

    @pl.when(f == pl.num_programs(1) - 1)
    def _():
        o_ref[...] = y_ref[...] + acc_ref[...]


def ffn(y, g, wg, wu, wd, tm, tf):
    m, d = y.shape
    ff = wg.shape[1]
    return pl.pallas_call(
        _ffn_kernel,
        grid=(m // tm, ff // tf),
        in_specs=[pl.BlockSpec((tm, d), lambda i, f: (i, 0)),
                  pl.BlockSpec((1, d), lambda i, f: (0, 0)),
                  pl.BlockSpec((d, tf), lambda i, f: (0, f)),
                  pl.BlockSpec((d, tf), lambda i, f: (0, f)),
                  pl.BlockSpec((tf, d), lambda i, f: (f, 0))],
        out_specs=pl.BlockSpec((tm, d), lambda i, f: (i, 0)),
        out_shape=jax.ShapeDtypeStruct((m, d), F32),
        scratch_shapes=[pltpu.VMEM((tm, d), BF16), pltpu.VMEM((tm, d), F32)],
        compiler_params=_params("parallel", "arbitrary"),
    )(y, g.reshape(1, d), wg, wu, wd)


def _rope_tables(pos, head_dim):
    rot = head_dim // ROPE_FRAC
    half = rot // 2
    inv = jnp.exp(-math.log(ROPE_THETA) * jnp.arange(half, dtype=F32) * (2.0 / rot))
    ang = pos.astype(F32)[:, None] * inv[None, :]
    cos, sin = jnp.cos(ang), jnp.sin(ang)
    m = pos.shape[0]
    one = jnp.ones((m, head_dim - rot), F32)
    zero_r = jnp.zeros((m, head_dim - rot), F32)
    zero_h = jnp.zeros((m, half), F32)
    c = jnp.concatenate([cos, cos, one], axis=1)
    s_dn = jnp.concatenate([-sin, zero_h, zero_r], axis=1)
    s_up = jnp.concatenate([zero_h, sin, zero_r], axis=1)
    rep = LANES // head_dim
    return jnp.stack([jnp.tile(c, (1, rep)), jnp.tile(s_dn, (1, rep)), jnp.tile(s_up, (1, rep))])


def _rot(x, tab_ref, half):
    return (x * tab_ref[0] + pltpu.roll(x, LANES - half, 1) * tab_ref[1]
            + pltpu.roll(x, half, 1) * tab_ref[2])


def _rope_kernel(main_ref, tail_ref, tq_ref, ti_ref, kv_ref, t32_ref, ob_ref,
                 *, n_q, n_k, n_v, n_qi, wi_scale, q_scale):
    half_qk = C_HEAD_DIM // ROPE_FRAC // 2
    half_i = IDX_DIM // ROPE_FRAC // 2
    for c in range(n_q + n_k + n_v + n_qi):
        cs = slice(c * LANES, (c + 1) * LANES)
        x = main_ref[:, cs]
        if c < n_q + n_k:
            x = _rot(x, tq_ref, half_qk)
        elif c >= n_q + n_k + n_v:
            x = _rot(x, ti_ref, half_i)
        if n_q <= c < n_q + n_k + n_v:
            kv_ref[:, (c - n_q) * LANES:(c - n_q + 1) * LANES] = x
        ob_ref[:, cs] = (x * q_scale if c < n_q else x).astype(BF16)
    t = tail_ref[...]
    lane = lax.broadcasted_iota(jnp.int32, (1, LANES), 1)
    r = _rot(t, ti_ref, half_i)
    t32_ref[...] = jnp.where(lane < IDX_DIM, r, t * wi_scale)
    c = n_q + n_k + n_v + n_qi
    ob_ref[:, c * LANES:(c + 1) * LANES] = jnp.where(lane < IDX_DIM, r, pltpu.roll(r, IDX_DIM, 1)).astype(BF16)


def rope_all(main, tail, tab_qk, tab_idx, tm):
    m, nmain = main.shape
    n_q = C_HEADS * C_HEAD_DIM // LANES
    n_k = C_KV_HEADS * C_HEAD_DIM // LANES
    n_qi = IDX_HEADS * IDX_DIM // LANES
    wi_scale = (IDX_HEADS ** -0.5) * (IDX_DIM ** -0.5)
    q_scale = (C_HEAD_DIM ** -0.5) * math.log2(math.e)
    row = lambda i: (i, 0)
    return pl.pallas_call(
        functools.partial(_rope_kernel, n_q=n_q, n_k=n_k, n_v=n_k, n_qi=n_qi, wi_scale=wi_scale, q_scale=q_scale),
        grid=(m // tm,),
        in_specs=[pl.BlockSpec((tm, nmain), row),
                  pl.BlockSpec((tm, LANES), row),
                  pl.BlockSpec((3, tm, LANES), lambda i: (0, i, 0)),
                  pl.BlockSpec((3, tm, LANES), lambda i: (0, i, 0))],
        out_specs=[pl.BlockSpec((tm, 2 * n_k * LANES), row),
                   pl.BlockSpec((tm, LANES), row),
                   pl.BlockSpec((tm, nmain + LANES), row)],
        out_shape=[jax.ShapeDtypeStruct((m, 2 * n_k * LANES), F32),
                   jax.ShapeDtypeStruct((m, LANES), F32),
                   jax.ShapeDtypeStruct((m, nmain + LANES), BF16)],
        compiler_params=_params("parallel"),
    )(main, tail, tab_qk, tab_idx)


def _dsa_kernel(q_ref, qi_ref, wi_ref, kn_ref, vn_ref, kin_ref, *refs, tq, kb, past, n_valid, q_pos0, topk):
    ck_refs, cv_refs = refs[:C_KV_HEADS], refs[C_KV_HEADS:2 * C_KV_HEADS]
    ci_ref, o_ref, k_ref, v_ref, ki_ref, sc_ref, qs_ref, acc_ref, m_ref, l_ref = refs[2 * C_KV_HEADS:]
    i = 0
    rows = C_GROUPS * tq
    sub = kb // LANES
    kf = float(topk)

    new = past + tq
    ci = ci_ref[...].astype(BF16)
    for g in range(C_KV_HEADS):
        gs = slice(g * C_HEAD_DIM, (g + 1) * C_HEAD_DIM)
        k_ref[0:past, gs] = ck_refs[g][...].astype(BF16)
        v_ref[0:past, gs] = cv_refs[g][...].astype(BF16)
    ki_ref[0:past, :] = jnp.concatenate([ci, ci], axis=1)
    k_ref[past:new, :] = kn_ref[...]
    v_ref[past:new, :] = vn_ref[...]
    ki_ref[past:new, :] = kin_ref[...]
    n_pad = k_ref.shape[0] - new
    k_ref[new:, :] = jnp.zeros((n_pad, k_ref.shape[1]), BF16)
    v_ref[new:, :] = jnp.zeros((n_pad, v_ref.shape[1]), BF16)
    ki_ref[new:, :] = jnp.zeros((n_pad, ki_ref.shape[1]), BF16)

    q_pos = q_pos0 + i * tq + lax.broadcasted_iota(jnp.int32, (tq, 1), 0)
    q_chunk = q_pos // CHUNK
    last_chunk = (q_pos0 + i * tq + tq - 1) // CHUNK
    kv_limit = min(n_valid, (last_chunk + 1) * CHUNK)
    nkb = (kv_limit + kb - 1) // kb

    lane = lax.broadcasted_iota(jnp.int32, (1, LANES), 1)
    first = lane < IDX_DIM
    wi = wi_ref[:, IDX_DIM:IDX_DIM + IDX_HEADS]

    def score_block(b, carry):
        for c in range(sub):
            off = pl.multiple_of(b * kb + c * LANES, LANES)
            kib = ki_ref[pl.ds(off, LANES), :]
            acc = jnp.zeros((tq, LANES), F32)
            for hp in range(IDX_HEADS // 2):
                qp = qi_ref[:, hp * LANES:(hp + 1) * LANES]
                for half in range(2):
                    keep = first if half == 0 else jnp.logical_not(first)
                    qm = jnp.where(keep, qp, jnp.zeros_like(qp))
                    d = lax.dot_general(qm, kib, _NT, preferred_element_type=F32)
                    h = hp * 2 + half
                    acc = acc + jnp.maximum(d, 0.0) * wi[:, h:h + 1]
            k_pos = off + lane
            adm = jnp.logical_and(k_pos // CHUNK <= q_chunk, k_pos < n_valid)
            sc_ref[b, :, c * LANES:(c + 1) * LANES] = jnp.where(adm, acc, NEG)
        return carry

    lax.fori_loop(0, nkb, score_block, 0)

    def lane_sum(x):
        return jnp.sum(x, axis=1, keepdims=True)

    def count_ge(t):
        def body(b, acc):
            for c in range(sub):
                blk = sc_ref[b, :, c * LANES:(c + 1) * LANES]
                acc = acc + jnp.where(blk >= t, 1.0, 0.0)
            return acc
        return lane_sum(lax.fori_loop(0, nkb, body, jnp.zeros((tq, LANES), F32)))

    def stats(b, carry):
        mx, mn, cnt = carry
        for c in range(sub):
            blk = sc_ref[b, :, c * LANES:(c + 1) * LANES]
            ok = blk > 0.5 * NEG
            mx = jnp.maximum(mx, blk)
            mn = jnp.minimum(mn, jnp.where(ok, blk, BIG))
            cnt = cnt + jnp.where(ok, 1.0, 0.0)
        return mx, mn, cnt

    mx, mn, cnt = lax.fori_loop(
        0, nkb, stats,
        (jnp.full((tq, LANES), NEG, F32), jnp.full((tq, LANES), BIG, F32), jnp.zeros((tq, LANES), F32)))
    row_max = jnp.max(mx, axis=1, keepdims=True)
    row_min = jnp.min(mn, axis=1, keepdims=True)
    n_adm = lane_sum(cnt)

    done0 = jnp.where(n_adm <= kf, 1.0, 0.0)
    state0 = (row_min, jnp.full((tq, 1), BIG, F32), row_max, jnp.full((tq, 1), 0.5 * NEG, F32), done0)

    def bisect(state, n_steps):
        def cond(c):
            it, st = c
            return jnp.logical_and(it < n_steps, jnp.min(st[4]) < 0.5)

        def body(c):
            it, (lo, hi, mid, thr, done) = c
            cnt = count_ge(mid)
            live = done < 0.5
            hit = jnp.logical_and(live, cnt == kf)
            ge = cnt >= kf
            thr = jnp.where(hit, mid, thr)
            done = jnp.where(hit, 1.0, done)
            lo = jnp.where(ge, mid, lo)
            hi = jnp.where(ge, hi, mid)
            return it + 1, (lo, hi, 0.5 * (lo + hi), thr, done)

        return lax.while_loop(cond, body, (jnp.int32(0), state))[1]

    def snap(state):
        lo, hi, mid, thr, done = state

        def body(b, carry):
            v_lo, v_hi = carry
            for c in range(sub):
                blk = sc_ref[b, :, c * LANES:(c + 1) * LANES]
                v_lo = jnp.minimum(v_lo, jnp.where(blk >= lo, blk, BIG))
                v_hi = jnp.maximum(v_hi, jnp.where(blk < hi, blk, NEG))
            return v_lo, v_hi

        v_lo, v_hi = lax.fori_loop(0, nkb, body,
                                   (jnp.full((tq, LANES), BIG, F32), jnp.full((tq, LANES), NEG, F32)))
        v_lo = jnp.min(v_lo, axis=1, keepdims=True)
        v_hi = jnp.max(v_hi, axis=1, keepdims=True)
        live = done < 0.5
        tie = jnp.logical_and(live, v_lo == v_hi)
        thr = jnp.where(tie, v_lo, thr)
        done = jnp.where(tie, 1.0, done)
        lo = jnp.where(live, v_lo, lo)
        return (lo, hi, 0.5 * (lo + hi), thr, done), jnp.where(tie, 1.0, 0.0)

    state = bisect(state0, 32)

    def refine_cond(c):
        rounds, st, _ = c
        return jnp.logical_and(rounds < 10, jnp.min(st[4]) < 0.5)

    def refine_body(c):
        rounds, st, tie = c
        st, new_tie = snap(st)
        st = bisect(st, 32)
        return rounds + 1, st, jnp.maximum(tie, new_tie)

    _, state, tie = lax.while_loop(refine_cond, refine_body,
                                   (jnp.int32(0), state, jnp.zeros((tq, 1), F32)))
    thr = state[3]
    any_tie = jnp.max(tie) > 0.5

    @pl.when(jnp.logical_not(any_tie))
    def _():
        def body(b, carry):
            for c in range(sub):
                cs = slice(c * LANES, (c + 1) * LANES)
                sc_ref[b, :, cs] = jnp.where(sc_ref[b, :, cs] >= thr, 0.0, NEG)
            return carry
        lax.fori_loop(0, nkb, body, 0)

    @pl.when(any_tie)
    def _():
        def gt_body(b, acc):
            for c in range(sub):
                acc = acc + jnp.where(sc_ref[b, :, c * LANES:(c + 1) * LANES] > thr, 1.0, 0.0)
            return acc
        need = kf - lane_sum(lax.fori_loop(0, nkb, gt_body, jnp.zeros((tq, LANES), F32)))

        def count_eq_upto(j):
            def body(b, acc):
                for c in range(sub):
                    blk = sc_ref[b, :, c * LANES:(c + 1) * LANES]
                    idx = (b * kb + c * LANES + lane).astype(F32)
                    acc = acc + jnp.where(jnp.logical_and(blk == thr, idx <= j), 1.0, 0.0)
                return acc
            return lane_sum(lax.fori_loop(0, nkb, body, jnp.zeros((tq, LANES), F32)))

        def idx_body(_, c):
            lo_j, hi_j = c
            mid_j = jnp.floor(0.5 * (lo_j + hi_j))
            ok = count_eq_upto(mid_j) >= need
            return jnp.where(ok, lo_j, mid_j), jnp.where(ok, mid_j, hi_j)

        n_steps = max(1, math.ceil(math.log2(sc_ref.shape[0] * kb + 1)))
        _, last = lax.fori_loop(
            0, n_steps, idx_body,
            (jnp.full((tq, 1), -1.0, F32), jnp.full((tq, 1), float(sc_ref.shape[0] * kb), F32)))
        last = jnp.where(tie > 0.5, last, BIG)

        def body(b, carry):
            for c in range(sub):
                cs = slice(c * LANES, (c + 1) * LANES)
                blk = sc_ref[b, :, cs]
                idx = (b * kb + c * LANES + lane).astype(F32)
                sel = jnp.logical_or(blk > thr, jnp.logical_and(blk == thr, idx <= last))
                sc_ref[b, :, cs] = jnp.where(sel, 0.0, NEG)
            return carry
        lax.fori_loop(0, nkb, body, 0)

    for g in range(C_KV_HEADS):
        for hh in range(C_GROUPS):
            h = g * C_GROUPS + hh
            qs_ref[g, hh * tq:(hh + 1) * tq, :] = q_ref[:, h * C_HEAD_DIM:(h + 1) * C_HEAD_DIM]
    m_ref[...] = jnp.full(m_ref.shape, NEG, F32)
    l_ref[...] = jnp.zeros(l_ref.shape, F32)
    acc_ref[...] = jnp.zeros(acc_ref.shape, F32)

    def attend(b, carry):
        off = pl.multiple_of(b * kb, kb)
        bias = sc_ref[b]
        bias = jnp.concatenate([bias] * C_GROUPS, axis=0)
        for g in range(C_KV_HEADS):
            cs = slice(g * C_HEAD_DIM, (g + 1) * C_HEAD_DIM)
            kblk = k_ref[pl.ds(off, kb), cs]
            vblk = v_ref[pl.ds(off, kb), cs]
            s = lax.dot_general(qs_ref[g], kblk, _NT, preferred_element_type=F32)
            s = s + bias
            m_old = m_ref[g]
            m_new = jnp.maximum(m_old, jnp.max(s, axis=1, keepdims=True))
            alpha = jnp.exp2(m_old - m_new)
            p = jnp.exp2(s - m_new)
            l_ref[g] = alpha * l_ref[g] + jnp.sum(p, axis=1, keepdims=True)
            acc_ref[g] = alpha * acc_ref[g] + jnp.dot(p.astype(BF16), vblk, preferred_element_type=F32)
            m_ref[g] = m_new
        return carry

    lax.fori_loop(0, nkb, attend, 0, unroll=True)

    for g in range(C_KV_HEADS):
        o = acc_ref[g] / l_ref[g]
        for hh in range(C_GROUPS):
            h = g * C_GROUPS + hh
            o_ref[:, h * C_HEAD_DIM:(h + 1) * C_HEAD_DIM] = o[hh * tq:(hh + 1) * tq, :].astype(o_ref.dtype)


def dsa_sample(ob, t32, cache_k, cache_v, cache_i, *, row0, kb, topk):
    nb, past = cache_k.shape[:2]
    head = lambda g: pl.BlockSpec((None, past, C_HEAD_DIM), lambda b, i: (b, 0, g))
    heads = [head(g) for g in range(C_KV_HEADS)]
    n_kv = C_KV_HEADS * C_HEAD_DIM
    ds = (ob.shape[0] - row0) // nb
    n_q = C_HEADS * C_HEAD_DIM
    n_qi = IDX_HEADS * IDX_DIM
    s_all = past + ds
    s_pad = -(-s_all // kb) * kb
    rows = C_GROUPS * ds
    col = lambda w, off: (lambda b, i: (row0 // ds + b, off // w))
    cmap = lambda b, i: (b, 0, 0)
    return pl.pallas_call(
        functools.partial(_dsa_kernel, tq=ds, kb=kb, past=past, n_valid=s_all, q_pos0=past, topk=topk),
        grid=(nb, 1),
        in_specs=[pl.BlockSpec((ds, n_q), col(n_q, 0)),
                  pl.BlockSpec((ds, n_qi), col(n_qi, n_q + 2 * n_kv)),
                  pl.BlockSpec((ds, LANES), col(LANES, 0)),
                  pl.BlockSpec((ds, n_kv), col(n_kv, n_q)),
                  pl.BlockSpec((ds, n_kv), col(n_kv, n_q + n_kv)),
                  pl.BlockSpec((ds, LANES), col(LANES, n_q + 2 * n_kv + n_qi))]
                 + heads + heads + [pl.BlockSpec((None, past, IDX_DIM), cmap)],
        out_specs=pl.BlockSpec((ds, n_q), lambda b, i: (b, 0)),
        out_shape=jax.ShapeDtypeStruct((nb * ds, n_q), BF16),
        scratch_shapes=[pltpu.VMEM((s_pad, n_kv), BF16),
                        pltpu.VMEM((s_pad, n_kv), BF16),
                        pltpu.VMEM((s_pad, LANES), BF16),
                        pltpu.VMEM((s_pad // kb, ds, kb), F32),
                        pltpu.VMEM((C_KV_HEADS, rows, C_HEAD_DIM), BF16),
                        pltpu.VMEM((C_KV_HEADS, rows, C_HEAD_DIM), F32),
                        pltpu.VMEM((C_KV_HEADS, rows, 1), F32),
                        pltpu.VMEM((C_KV_HEADS, rows, 1), F32)],
        compiler_params=_params("parallel", "arbitrary"),
    )(ob, ob, t32, ob, ob, ob, *([cache_k] * C_KV_HEADS), *([cache_v] * C_KV_HEADS), cache_i)


def _dsa_t_kernel(q_ref, qi_ref, wi_ref, k_ref, vt_ref, ki_ref, o_ref,
                  sc_ref, qim_ref, acc_ref, m_ref, l_ref, l8_ref, kmax_ref,
                  *, tq, kb, qb, pb, cb, n_valid, q_pos0, topk):
    i = pl.program_id(1)
    kf = float(topk)

    q_pos = q_pos0 + i * tq + lax.broadcasted_iota(jnp.int32, (1, tq), 1)
    q_chunk = q_pos // CHUNK
    last_chunk = (q_pos0 + i * tq + tq - 1) // CHUNK
    kv_limit = jnp.minimum(n_valid, (last_chunk + 1) * CHUNK)
    nkb = (kv_limit + kb - 1) // kb
    ncb = (kv_limit + cb - 1) // cb

    def col_reduce(x, op):
        groups = x.shape[0] // 8
        chains = 8 if groups % 8 == 0 else 1
        return op(op(x.reshape(chains, groups // chains, 8, tq), axis=1), axis=0)

    col_sum = lambda x: col_reduce(x, jnp.sum)
    col_max = lambda x: col_reduce(x, jnp.max)
    col_min = lambda x: col_reduce(x, jnp.min)

    lane = lax.broadcasted_iota(jnp.int32, (1, LANES), 1)
    first = lane < IDX_DIM
    for hp in range(IDX_HEADS // 2):
        qp = qi_ref[:, hp * LANES:(hp + 1) * LANES]
        qim_ref[2 * hp] = jnp.where(first, qp, jnp.zeros_like(qp))
        qim_ref[2 * hp + 1] = jnp.where(first, jnp.zeros_like(qp), qp)
    block_iota = lax.broadcasted_iota(jnp.int32, (pb, 1), 0)

    def score_block(b2, carry):
        for u in range(2):
            off = pl.multiple_of((2 * b2 + u) * pb, pb)
            kib = ki_ref[pl.ds(off, pb), :]
            parts = []
            for h in range(IDX_HEADS):
                d = lax.dot_general(kib, qim_ref[h], _NT, preferred_element_type=F32)
                parts.append(jnp.maximum(d, 0.0) * wi_ref[h:h + 1, :])
            while len(parts) > 1:
                parts = [a + b for a, b in zip(parts[0::2], parts[1::2])]
            acc = parts[0]
            k_pos = off + block_iota
            adm = jnp.logical_and(k_pos // CHUNK <= q_chunk, k_pos < n_valid)
            sc_ref[pl.ds(off, pb), :] = jnp.where(adm, acc, NEG)
        return carry

    lax.fori_loop(0, ncb * (cb // pb) // 2, score_block, 0)
    key_iota = lax.broadcasted_iota(jnp.int32, (cb, 1), 0)

    def blocks(body, init):
        def step(b, carry):
            off = pl.multiple_of(b * cb, cb)
            return body(off, sc_ref[pl.ds(off, cb), :], carry)
        return lax.fori_loop(0, ncb, step, init)

    def total(x):
        return jnp.sum(x, axis=0, keepdims=True)

    def count_ge(t):
        return total(blocks(lambda off, blk, acc: acc + col_sum(jnp.where(blk >= t, 1.0, 0.0)),
                            jnp.zeros((8, tq), F32)))

    def stats(off, blk, carry):
        mx, mn, cnt, pos, nonneg = carry
        ok = blk > 0.5 * NEG
        return (jnp.maximum(mx, col_max(blk)), jnp.minimum(mn, col_min(jnp.where(ok, blk, BIG))),
                cnt + col_sum(jnp.where(ok, 1.0, 0.0)),
                pos + col_sum(jnp.where(blk > 0.0, 1.0, 0.0)),
                nonneg + col_sum(jnp.where(blk >= 0.0, 1.0, 0.0)))

    zeros8 = jnp.zeros((8, tq), F32)
    mx, mn, cnt, pos, nonneg = blocks(stats, (jnp.full((8, tq), NEG, F32), jnp.full((8, tq), BIG, F32),
                                              zeros8, zeros8, zeros8))
    row_max = jnp.max(mx, axis=0, keepdims=True)
    row_min = jnp.min(mn, axis=0, keepdims=True)
    n_adm = total(cnt)
    n_pos = total(pos)
    n_nonneg = total(nonneg)

    open_row = n_adm > kf
    above = n_pos >= kf
    below = n_nonneg < kf
    zero_tie = jnp.logical_and(open_row, jnp.logical_not(jnp.logical_or(above, below)))
    lo0 = jnp.where(above, 0.0, row_min)
    hi0 = jnp.where(below, 0.0, BIG)
    mid0 = jnp.where(below, 0.5 * (lo0 + hi0), row_max)
    thr0 = jnp.where(zero_tie, 0.0, 0.5 * NEG)
    done0 = jnp.where(jnp.logical_or(jnp.logical_not(open_row), zero_tie), 1.0, 0.0)
    tie0 = jnp.where(zero_tie, 1.0, 0.0)
    c_lo0 = jnp.where(above, n_pos, n_adm)
    c_hi0 = jnp.where(below, n_nonneg, 0.0)
    state0 = (lo0, hi0, mid0, thr0, done0, c_lo0, c_hi0)
    log_k = math.log2(kf + 0.5)

    def bisect(state, n_steps):
        def cond(c):
            it, st = c
            return jnp.logical_and(it < n_steps, jnp.min(st[4]) < 0.5)

        def body(c):
            it, (lo, hi, mid, thr, done, c_lo, c_hi) = c
            cnt = count_ge(mid)
            hit = jnp.logical_and(done < 0.5, cnt == kf)
            ge = cnt >= kf
            thr = jnp.where(hit, mid, thr)
            done = jnp.where(hit, 1.0, done)
            lo = jnp.where(ge, mid, lo)
            hi = jnp.where(ge, hi, mid)
            c_lo = jnp.where(ge, cnt, c_lo)
            c_hi = jnp.where(ge, c_hi, cnt)
            log_lo = jnp.log2(jnp.maximum(c_lo, 1.0))
            frac = (log_lo - log_k) / jnp.maximum(log_lo - jnp.log2(jnp.maximum(c_hi, 0.5)), 1e-6)
            frac = jnp.clip(frac, 0.02, 0.98)
            frac = jnp.where(jnp.logical_or(it % 3 == 2, hi >= BIG), 0.5, frac)
            return it + 1, (lo, hi, lo + (hi - lo) * frac, thr, done, c_lo, c_hi)

        return lax.while_loop(cond, body, (jnp.int32(0), state))[1]

    def snap(state):
        lo, hi, mid, thr, done, c_lo, c_hi = state

        def body(off, blk, carry):
            v_lo, v_hi = carry
            return (jnp.minimum(v_lo, col_min(jnp.where(blk >= lo, blk, BIG))),
                    jnp.maximum(v_hi, col_max(jnp.where(blk < hi, blk, NEG))))

        v_lo, v_hi = blocks(body, (jnp.full((8, tq), BIG, F32), jnp.full((8, tq), NEG, F32)))
        v_lo = jnp.min(v_lo, axis=0, keepdims=True)
        v_hi = jnp.max(v_hi, axis=0, keepdims=True)
        live = done < 0.5
        tie = jnp.logical_and(live, v_lo == v_hi)
        thr = jnp.where(tie, v_lo, thr)
        done = jnp.where(tie, 1.0, done)
        lo = jnp.where(live, v_lo, lo)
        return (lo, hi, 0.5 * (lo + hi), thr, done, c_lo, c_hi), jnp.where(tie, 1.0, 0.0)

    state = bisect(state0, 32)

    def refine_cond(c):
        rounds, st, _ = c
        return jnp.logical_and(rounds < 10, jnp.min(st[4]) < 0.5)

    def refine_body(c):
        rounds, st, tie = c
        st, new_tie = snap(st)
        st = bisect(st, 32)
        return rounds + 1, st, jnp.maximum(tie, new_tie)

    _, state, tie = lax.while_loop(refine_cond, refine_body, (jnp.int32(0), state, tie0))
    thr = state[3]
    any_tie = jnp.max(tie) > 0.5

    @pl.when(jnp.logical_not(any_tie))
    def _():
        def body(off, blk, carry):
            sc_ref[pl.ds(off, cb), :] = jnp.where(blk >= thr, 0.0, NEG)
            return carry
        blocks(body, 0)

    @pl.when(any_tie)
    def _():
        need = kf - total(blocks(lambda off, blk, acc: acc + col_sum(jnp.where(blk > thr, 1.0, 0.0)),
                                 jnp.zeros((8, tq), F32)))

        def count_eq_upto(j):
            def body(off, blk, acc):
                idx = (off + key_iota).astype(F32)
                return acc + col_sum(jnp.where(jnp.logical_and(blk == thr, idx <= j), 1.0, 0.0))
            return total(blocks(body, jnp.zeros((8, tq), F32)))

        def idx_cond(c):
            it, (lo_j, hi_j, settled) = c
            return jnp.logical_and(it < n_steps, jnp.min(settled) < 0.5)

        def idx_body(c):
            it, (lo_j, hi_j, settled) = c
            mid_j = jnp.floor(0.5 * (lo_j + hi_j))
            cnt = count_eq_upto(mid_j)
            ok = cnt >= need
            live = settled < 0.5
            lo_j = jnp.where(jnp.logical_and(live, jnp.logical_not(ok)), mid_j, lo_j)
            hi_j = jnp.where(jnp.logical_and(live, ok), mid_j, hi_j)
            settled = jnp.where(jnp.logical_or(cnt == need, hi_j - lo_j <= 1.0), 1.0, settled)
            return it + 1, (lo_j, hi_j, settled)

        n_keys = sc_ref.shape[0]
        n_steps = max(1, math.ceil(math.log2(n_keys + 1))) + 1
        _, (_, last, _) = lax.while_loop(
            idx_cond, idx_body,
            (jnp.int32(0), (jnp.full((1, tq), -1.0, F32), jnp.full((1, tq), float(n_keys), F32),
                            jnp.where(tie > 0.5, 0.0, 1.0))))
        last = jnp.where(tie > 0.5, last, BIG)

        def body(off, blk, carry):
            idx = (off + key_iota).astype(F32)
            sel = jnp.logical_or(blk > thr, jnp.logical_and(blk == thr, idx <= last))
            sc_ref[pl.ds(off, cb), :] = jnp.where(sel, 0.0, NEG)
            return carry
        blocks(body, 0)

    halves = kb // qb

    def head_scores(off, h):
        gs = slice((h // C_GROUPS) * C_HEAD_DIM, (h // C_GROUPS + 1) * C_HEAD_DIM)
        out = []
        for j in range(halves):
            rows = pl.ds(off + j * qb, qb)
            s = lax.dot_general(k_ref[rows, gs], q_ref[:, h * C_HEAD_DIM:(h + 1) * C_HEAD_DIM],
                                _NT, preferred_element_type=F32)
            out.append(s + sc_ref[rows, :])
        return out, gs

    @pl.when(i == 0)
    def _():
        for g in range(C_KV_HEADS):
            gs = slice(g * C_HEAD_DIM, (g + 1) * C_HEAD_DIM)

            def body(b, mx):
                kk = k_ref[pl.ds(pl.multiple_of(b * cb, cb), cb), gs].astype(F32)
                return jnp.maximum(mx, jnp.sum(kk * kk, axis=1, keepdims=True))

            mx = lax.fori_loop(0, k_ref.shape[0] // cb, body, jnp.zeros((cb, 1), F32))
            kmax_ref[g] = jnp.full((1, tq), 1.0, F32) * jnp.max(mx)

    ones = jnp.ones((8, C_HEAD_DIM), BF16)
    for h in range(C_HEADS):
        qh = q_ref[:, h * C_HEAD_DIM:(h + 1) * C_HEAD_DIM].astype(F32)
        qn2 = lax.dot_general(ones, (qh * qh).astype(BF16), _NT, preferred_element_type=F32)[0:1, :]
        m_ref[h] = jnp.sqrt(qn2 * kmax_ref[h // C_GROUPS]) * SHIFT_SLACK
    l8_ref[...] = jnp.zeros(l8_ref.shape, F32)
    acc_ref[...] = jnp.zeros(acc_ref.shape, F32)

    def attend_fixed(b0, n):
        for h in range(C_HEADS):
            lsum, upd = None, None
            for u in range(n):
                b = b0 + u
                ss, gs = head_scores(pl.multiple_of(b * kb, kb), h)
                ps = [jnp.exp2(s - m_ref[h]) for s in ss]
                part = sum(col_sum(p) for p in ps)
                p_all = jnp.concatenate([p.astype(BF16) for p in ps], axis=0)
                pv = jnp.dot(vt_ref[b, gs, :], p_all, preferred_element_type=F32)
                lsum = part if lsum is None else lsum + part
                upd = pv if upd is None else upd + pv
            l8_ref[h] += lsum
            acc_ref[h] += upd

    def eight_blocks(t, carry):
        attend_fixed(8 * t, 8)
        return carry

    lax.fori_loop(0, nkb // 8, eight_blocks, 0)
    rest = (nkb // 8) * 8

    @pl.when(nkb % 8 >= 4)
    def _():
        attend_fixed(rest, 4)

    @pl.when(nkb % 4 >= 2)
    def _():
        attend_fixed(rest + ((nkb % 8) // 4) * 4, 2)
    l_min = jnp.full((1, tq), BIG, F32)
    for h in range(C_HEADS):
        l_ref[h] = total(l8_ref[h])
        l_min = jnp.minimum(l_min, l_ref[h])

    @pl.when(jnp.logical_not(jnp.min(l_min) >= MIN_SOFTMAX_SUM))
    def _():
        m_ref[...] = jnp.full(m_ref.shape, NEG, F32)
        l_ref[...] = jnp.zeros(l_ref.shape, F32)
        acc_ref[...] = jnp.zeros(acc_ref.shape, F32)

        def attend(b, carry):
            off = pl.multiple_of(b * kb, kb)
            for h in range(C_HEADS):
                ss, gs = head_scores(off, h)
                m_old = m_ref[h]
                m_new = m_old
                for s in ss:
                    m_new = jnp.maximum(m_new, jnp.max(col_max(s), axis=0, keepdims=True))
                alpha = jnp.exp2(m_old - m_new)
                ps = [jnp.exp2(s - m_new) for s in ss]
                l_ref[h] = alpha * l_ref[h] + total(sum(col_sum(p) for p in ps))
                p_all = jnp.concatenate([p.astype(BF16) for p in ps], axis=0)
                acc_ref[h] = alpha * acc_ref[h] + jnp.dot(vt_ref[b, gs, :], p_all,
                                                          preferred_element_type=F32)
                m_ref[h] = m_new
            return carry

        lax.fori_loop(0, nkb, attend, 0)

    for h in range(C_HEADS):
        o = acc_ref[h] / l_ref[h]
        o_ref[:, h * C_HEAD_DIM:(h + 1) * C_HEAD_DIM] = o.T.astype(o_ref.dtype)


def dsa_t(ob, wi_t, vt, *, nq, tq, kb, qb, pb, cb, topk):
    dq = C_HEADS * C_HEAD_DIM
    n_kv = C_KV_HEADS * C_HEAD_DIM
    n_qi = IDX_HEADS * IDX_DIM
    s_pad = nq
    assert tq % (2 * kb) == 0 and tq % CHUNK == 0 and nq % cb == 0 and cb % pb == 0 and kb % qb == 0
    return pl.pallas_call(
        functools.partial(_dsa_t_kernel, tq=tq, kb=kb, qb=qb, pb=pb, cb=cb, n_valid=nq, q_pos0=0, topk=topk),
        grid=(1, nq // tq),
        in_specs=[pl.BlockSpec((tq, dq), lambda b, i: (i, 0)),
                  pl.BlockSpec((tq, n_qi), lambda b, i: (i, (dq + 2 * n_kv) // n_qi)),
                  pl.BlockSpec((IDX_HEADS, tq), lambda b, i: (0, i)),
                  pl.BlockSpec((nq, n_kv), lambda b, i: (0, dq // n_kv)),
                  pl.BlockSpec(vt.shape, lambda b, i: (0, 0, 0)),
                  pl.BlockSpec((nq, LANES), lambda b, i: (0, (dq + 2 * n_kv + n_qi) // LANES))],
        out_specs=pl.BlockSpec((tq, dq), lambda b, i: (i, 0)),
        out_shape=jax.ShapeDtypeStruct((nq, dq), BF16),
        scratch_shapes=[pltpu.VMEM((s_pad, tq), F32),
                        pltpu.VMEM((IDX_HEADS, tq, LANES), BF16),
                        pltpu.VMEM((C_HEADS, C_HEAD_DIM, tq), F32),
                        pltpu.VMEM((C_HEADS, 1, tq), F32),
                        pltpu.VMEM((C_HEADS, 1, tq), F32),
                        pltpu.VMEM((C_HEADS, 8, tq), F32),
                        pltpu.VMEM((C_KV_HEADS, 1, tq), F32)],
        compiler_params=_params("arbitrary", "arbitrary"),
    )(ob, ob, wi_t, ob, vt, ob)


def _router_kernel(y_ref, g_ref, wr_ref, h_ref, gate_ref):
    h = _rms(y_ref[...], g_ref[...])
    h_ref[...] = h.astype(BF16)
    logits = jnp.dot(h, wr_ref[...], preferred_element_type=F32, precision=lax.Precision.HIGHEST)
    lane = lax.broadcasted_iota(jnp.int32, logits.shape, 1)
    lg = jnp.where(lane < N_EXPERTS, logits, NEG)
    m1 = jnp.max(lg, axis=1, keepdims=True)
    i1 = jnp.min(jnp.where(lg == m1, lane, LANES), axis=1, keepdims=True)
    lg2 = jnp.where(lane == i1, NEG, lg)
    m2 = jnp.max(lg2, axis=1, keepdims=True)
    i2 = jnp.min(jnp.where(lg2 == m2, lane, LANES), axis=1, keepdims=True)
    e = jnp.exp(m2 - m1)
    g1 = 1.0 / (1.0 + e)
    g2 = e / (1.0 + e)
    meta = jnp.where(lane == 0, i1.astype(F32), jnp.where(lane == 1, i2.astype(F32),
                     jnp.where(lane == 2, g1, jnp.where(lane == 3, g2, 0.0))))
    gate_ref[...] = meta


def router(y, g, w_router_pad, tm):
    m, d = y.shape
    return pl.pallas_call(
        _router_kernel,
        grid=(m // tm,),
        in_specs=[pl.BlockSpec((tm, d), lambda i: (i, 0)),
                  pl.BlockSpec((1, d), lambda i: (0, 0)),
                  pl.BlockSpec((d, LANES), lambda i: (0, 0))],
        out_specs=[pl.BlockSpec((tm, d), lambda i: (i, 0)),
                   pl.BlockSpec((tm, LANES), lambda i: (i, 0))],
        out_shape=[jax.ShapeDtypeStruct((m, d), BF16), jax.ShapeDtypeStruct((m, LANES), F32)],
        compiler_params=_params("parallel"),
    )(y, g.reshape(1, d), w_router_pad)


def route_plan(meta, sup):
    m = meta.shape[0]
    n_tiles = (2 * m) // sup + N_EXPERTS
    e_all = jnp.concatenate([meta[:, 0], meta[:, 1]]).astype(jnp.int32)
    onehot = (e_all[:, None] == jnp.arange(N_EXPERTS)[None, :]).astype(jnp.int32)
    rank = jnp.sum((jnp.cumsum(onehot, axis=0) - onehot) * onehot, axis=1)
    counts = jnp.sum(onehot, axis=0)
    n_super = (counts + sup - 1) // sup
    super_end = jnp.cumsum(n_super)
    super_start = super_end - n_super
    pos = (super_start * sup)[e_all] + rank
    tiles = jnp.arange(n_tiles)
    used = super_end[-1]
    t_eff = jnp.minimum(tiles, used - 1)
    tile_expert = jnp.minimum(jnp.searchsorted(super_end, t_eff, side="right"), N_EXPERTS - 1).astype(jnp.int32)
    rows = jnp.clip(counts[tile_expert] - (t_eff - super_start[tile_expert]) * sup, 0, sup)
    tile_rows = jnp.where(tiles < used, rows, 0).astype(jnp.int32)
    token = jnp.concatenate([jnp.arange(m), jnp.arange(m)]).astype(jnp.int32)
    row_token = (jnp.arange(n_tiles * sup, dtype=jnp.int32) % m).at[pos].set(token)
    return pos.astype(jnp.int32), row_token, tile_expert, tile_rows


def _gather_rows_kernel(tok_ref, h_ref, o_ref, sem, *, batch):
    base = pl.program_id(0) * batch

    def issue(j, carry):
        pltpu.make_async_copy(h_ref.at[tok_ref[base + j]], o_ref.at[j], sem).start()
        return carry

    lax.fori_loop(0, batch, issue, 0)
    pltpu.make_async_copy(o_ref, o_ref, sem).wait()


def gather_rows(h3, row_token, batch):
    n_rows = row_token.shape[0]
    blk = (batch,) + h3.shape[1:]
    return pl.pallas_call(
        functools.partial(_gather_rows_kernel, batch=batch),
        grid_spec=pltpu.PrefetchScalarGridSpec(
            num_scalar_prefetch=1,
            grid=(n_rows // batch,),
            in_specs=[pl.BlockSpec(memory_space=pl.ANY)],
            out_specs=pl.BlockSpec(blk, lambda i, tok: (i, 0, 0)),
            scratch_shapes=[pltpu.SemaphoreType.DMA(())]),
        out_shape=jax.ShapeDtypeStruct((n_rows,) + h3.shape[1:], h3.dtype),
        compiler_params=_params("arbitrary"),
    )(row_token, h3)


def _grouped_ffn_kernel(te_ref, tr_ref, x_ref, wg_ref, wu_ref, wd_ref, o_ref,
                        wgb_ref, wub_ref, wdb_ref, *, sup, sub):
    t = pl.program_id(0)
    f = pl.program_id(1)
    rows = tr_ref[t]

    @pl.when(rows > 0)
    def _():
        wgb_ref[...] = wg_ref[...].astype(BF16)
        wub_ref[...] = wu_ref[...].astype(BF16)
        wdb_ref[...] = wd_ref[...].astype(BF16)

    for s in range(sup // sub):
        sl = slice(s * sub, (s + 1) * sub)

        @pl.when(jnp.logical_and(s * sub >= rows, f == 0))
        def _():
            o_ref[sl, :] = jnp.zeros((sub, o_ref.shape[1]), F32)

        @pl.when(s * sub < rows)
        def _():
            part = _swiglu_tile(x_ref[sl, :], wgb_ref[...], wub_ref[...], wdb_ref[...])

            @pl.when(f == 0)
            def _():
                o_ref[sl, :] = part

            @pl.when(f > 0)
            def _():
                o_ref[sl, :] += part


def grouped_ffn(xs, tile_expert, tile_rows, wg, wu, wd, sup, sub, tf):
    n_rows, d = xs.shape
    ff = wg.shape[2]
    n_f = ff // tf
    fidx = lambda t, f, te, tr: jnp.where(tr[t] > 0, f, n_f - 1)
    return pl.pallas_call(
        functools.partial(_grouped_ffn_kernel, sup=sup, sub=sub),
        grid_spec=pltpu.PrefetchScalarGridSpec(
            num_scalar_prefetch=2,
            grid=(n_rows // sup, n_f),
            in_specs=[pl.BlockSpec((sup, d), lambda t, f, te, tr: (t, 0), pipeline_mode=pl.Buffered(1)),
                      pl.BlockSpec((None, d, tf), lambda t, f, te, tr: (te[t], 0, fidx(t, f, te, tr))),
                      pl.BlockSpec((None, d, tf), lambda t, f, te, tr: (te[t], 0, fidx(t, f, te, tr))),
                      pl.BlockSpec((None, tf, d), lambda t, f, te, tr: (te[t], fidx(t, f, te, tr), 0))],
            out_specs=pl.BlockSpec((sup, d), lambda t, f, te, tr: (t, 0), pipeline_mode=pl.Buffered(1)),
            scratch_shapes=[pltpu.VMEM((d, tf), BF16), pltpu.VMEM((d, tf), BF16), pltpu.VMEM((tf, d), BF16)]),
        out_shape=jax.ShapeDtypeStruct((n_rows, d), F32),
        compiler_params=_params("arbitrary", "arbitrary"),
    )(tile_expert, tile_rows, xs, wg, wu, wd)


def _combine_kernel(pos_ref, y_ref, meta_ref, g_ref, ys_ref, op_ref, os_ref, buf_ref, x_ref, sem,
                    *, n_tok, tmc, prompt_tiles):
    i = pl.program_id(0)
    slot = i % 2

    def issue(tile, into):
        def body(j, carry):
            t = tile * tmc + j
            pltpu.make_async_copy(ys_ref.at[pos_ref[t]], buf_ref.at[into, 0, j], sem.at[into]).start()
            pltpu.make_async_copy(ys_ref.at[pos_ref[n_tok + t]], buf_ref.at[into, 1, j], sem.at[into]).start()
            return carry
        lax.fori_loop(0, tmc, body, 0)

    @pl.when(i == 0)
    def _():
        issue(0, 0)

    @pl.when(i + 1 < pl.num_programs(0))
    def _():
        issue(i + 1, 1 - slot)

    for e in range(2):
        pltpu.make_async_copy(buf_ref.at[slot, e], buf_ref.at[slot, e], sem.at[slot]).wait()
    g1 = meta_ref[:, 2:3]
    g2 = meta_ref[:, 3:4]
    ss = jnp.zeros((tmc, 1), F32)
    for c in range(buf_ref.shape[3]):
        cs = slice(c * LANES, (c + 1) * LANES)
        x = y_ref[:, cs] + (buf_ref[slot, 0, :, c, :] * g1 + buf_ref[slot, 1, :, c, :] * g2)
        ss = ss + jnp.sum(x * x, axis=1, keepdims=True)
        x_ref[:, cs] = x
    inv = lax.rsqrt(ss / x_ref.shape[1] + NORM_EPS)

    @pl.when(i < prompt_tiles)
    def _():
        op_ref[...] = x_ref[...] * inv * g_ref[...]

    @pl.when(i >= prompt_tiles)
    def _():
        os_ref[...] = x_ref[...] * inv * g_ref[...]


def combine_final(y, meta, g_final, ys3, pos, n_prompt, tmc):
    m, d = y.shape
    slab = ys3.shape[1:]
    prompt_tiles = n_prompt // tmc
    return pl.pallas_call(
        functools.partial(_combine_kernel, n_tok=m, tmc=tmc, prompt_tiles=prompt_tiles),
        grid_spec=pltpu.PrefetchScalarGridSpec(
            num_scalar_prefetch=1,
            grid=(m // tmc,),
            in_specs=[pl.BlockSpec((tmc, d), lambda i, pos: (i, 0)),
                      pl.BlockSpec((tmc, LANES), lambda i, pos: (i, 0)),
                      pl.BlockSpec((1, d), lambda i, pos: (0, 0)),
                      pl.BlockSpec(memory_space=pl.ANY)],
            out_specs=[pl.BlockSpec((tmc, d), lambda i, pos: (jnp.minimum(i, prompt_tiles - 1), 0)),
                       pl.BlockSpec((tmc, d), lambda i, pos: (jnp.maximum(i - prompt_tiles, 0), 0))],
            scratch_shapes=[pltpu.VMEM((2, 2, tmc) + slab, F32), pltpu.VMEM((tmc, d), F32),
                            pltpu.SemaphoreType.DMA((2,))]),
        out_shape=[jax.ShapeDtypeStruct((n_prompt, d), F32), jax.ShapeDtypeStruct((m - n_prompt, d), F32)],
        compiler_params=_params("arbitrary"),
    )(pos, y, meta, g_final.reshape(1, d), ys3)


def kernel(x_prompt, x_sample, cache_a_k, cache_a_v, state_pool, cache_c_k, cache_c_v, cache_c_idx,
           norm_mix, norm_ffn, norm_final, w_in_even, w_out_even, a_rel_bias, pool_w, pool_scale,
           ffn_w_gate, ffn_w_up, ffn_w_down, w_in_odd, w_out_odd,
           moe_router, moe_w_gate, moe_w_up, moe_w_down):
    nbp, lp, d = x_prompt.shape
    nb, ds, _ = x_sample.shape
    past = cache_c_k.shape[2]
    a_len = cache_a_k.shape[2]
    assert nbp == 1 and lp % 512 == 0 and (nb * ds) % 512 == 0 and ds == POOL_HALO and past >= POOL_HALO
    ns = nb * ds
    m = lp + ns
    tm = 512
    bf = lambda t: t.astype(BF16)

    proj0, x = stack_norm_proj(x_prompt.reshape(lp, d), x_sample.reshape(ns, d), norm_mix[0],
                               bf(w_in_even[0]), tm, 1024)
    k0 = proj0[:, A_WIDTH:2 * A_WIDTH]
    v0 = proj0[:, 2 * A_WIDTH:3 * A_WIDTH]
    u0 = proj0[:, 3 * A_WIDTH:]

    pad = A_PREV_CHUNKS * CHUNK
    bias_p = _rel_bias_tile(a_rel_bias[0], 0, CHUNK, -pad, A_BAND)
    a_p = band_prompt(proj0, _pair_rows(bias_p), lp)

    k_pos = past - a_len + jnp.arange(a_len + ds)
    q_pos = past + jnp.arange(ds)
    qch, kch = q_pos // CHUNK, k_pos // CHUNK
    ok = ((k_pos[None, :] >= 0) & (kch[None, :] <= qch[:, None])
          & (kch[None, :] >= qch[:, None] - A_PREV_CHUNKS))
    bias_s = jnp.where(ok[None], _rel_bias_tile(a_rel_bias[0], past, ds, past - a_len, a_len + ds), NEG)
    a = band_sample(proj0, cache_a_k[0].reshape(nb, a_len, A_WIDTH),
                    cache_a_v[0].reshape(nb, a_len, A_WIDTH), _pair_rows(bias_s), a_p, lp, nb, ds)

    u_s = u0[lp:].reshape(nb, ds, B_WIDTH)
    u_hist = jnp.concatenate([state_pool[0], u_s], axis=1)
    u_ext = jnp.concatenate([jnp.zeros((nb, POOL_HALO - B_HIST, B_WIDTH), F32), u_hist], axis=1)
    pw = bf(pool_w[0])
    ps = pool_scale[0].reshape(1, B_WIDTH)
    p_p = pool_prompt(proj0, pw, ps, lp, tm)
    p = pool_sample(u_ext, pw, ps, p_p, lp, past)
    wo = bf(w_out_even[0])
    y = mm_res([a, p], [wo[:A_WIDTH], wo[A_WIDTH:]], x, tm, d)
    y = ffn(y, norm_ffn[0], bf(ffn_w_gate[0]), bf(ffn_w_up[0]), bf(ffn_w_down[0]), tm, 512)

    n_q = C_HEADS * C_HEAD_DIM
    n_kv = C_KV_HEADS * C_HEAD_DIM
    n_qi = IDX_HEADS * IDX_DIM
    n_main = n_q + 2 * n_kv + n_qi
    w1 = w_in_odd[0]
    w_tail = jnp.pad(w1[:, n_main:], ((0, 0), (0, LANES - (w1.shape[1] - n_main))))
    main = norm_proj(y, norm_mix[1], bf(w1[:, :n_main]), tm, n_main // 2)
    tail = norm_proj(y, norm_mix[1], bf(w_tail), tm, LANES)
    pos = jnp.concatenate([jnp.arange(lp), jnp.tile(past + jnp.arange(ds), nb)])
    kv1, t32, ob = rope_all(main, tail, _rope_tables(pos, C_HEAD_DIM), _rope_tables(pos, IDX_DIM), tm)
    k1 = kv1[:, :n_kv]
    v1 = kv1[:, n_kv:]
    ki1 = t32[:, :IDX_DIM]

    kb_p = 128
    vt = ob[:lp, n_q + n_kv:n_q + 2 * n_kv].reshape(lp // kb_p, kb_p, n_kv).transpose(0, 2, 1)
    wi_t = t32[:lp, IDX_DIM:IDX_DIM + IDX_HEADS].T
    o_p = dsa_t(ob, wi_t, vt, nq=lp, tq=256, kb=kb_p, qb=128, pb=256, cb=512, topk=min(TOPK_MAX, lp // 4))
    o_s = dsa_sample(ob, t32, cache_c_k[0].reshape(nb, past, n_kv), cache_c_v[0].reshape(nb, past, n_kv), cache_c_idx[0],
                     row0=lp, kb=3 * LANES, topk=min(TOPK_MAX, (past + ds) // 4))
    o = jnp.concatenate([o_p, o_s], axis=0)
    y = mm_res([o], [bf(w_out_odd[0])], y, tm, d)

    wr = jnp.pad(moe_router[0], ((0, 0), (0, LANES - N_EXPERTS)))
    h, meta = router(y, norm_ffn[1], wr, tm)
    sup = 1536
    pos_rows, row_token, tile_expert, tile_rows = route_plan(meta, sup)
    n_rows = row_token.shape[0]
    slab = (d // LANES, LANES)
    xs = gather_rows(h.reshape((m,) + slab), row_token, sup)
    ys = grouped_ffn(xs.reshape(n_rows, d), tile_expert, tile_rows,
                     moe_w_gate[0], moe_w_up[0], moe_w_down[0], sup, 256, 512)
    y_p, y_s = combine_final(y, meta, norm_final, ys.reshape((n_rows,) + slab), pos_rows, lp, 256)

    y_prompt = y_p.reshape(1, lp, d)
    y_sample = y_s.reshape(nb, ds, d)
    keep = min(A_BAND, lp)
    heads = lambda t, n: t.reshape(1, n, -1, A_HEADS, A_HEAD_DIM)
    a_k_prompt = heads(k0[lp - keep:lp], 1)
    a_v_prompt = heads(v0[lp - keep:lp], 1)
    pool_prompt_out = u0[lp - B_HIST:lp].reshape(1, 1, B_HIST, B_WIDTH)
    c_k_prompt = k1[:lp].reshape(1, 1, lp, C_KV_HEADS, C_HEAD_DIM)
    c_v_prompt = v1[:lp].reshape(1, 1, lp, C_KV_HEADS, C_HEAD_DIM)
    c_idx_prompt = ki1[:lp].reshape(1, 1, lp, IDX_DIM)
    shift = lambda cache, new: jnp.concatenate(
        [cache[0], new.reshape(nb, ds, A_HEADS, A_HEAD_DIM)], axis=1)[:, ds:][None]
    a_k_sample = shift(cache_a_k, k0[lp:])
    a_v_sample = shift(cache_a_v, v0[lp:])
    pool_sample_out = u_hist[:, ds:][None]
    c_k_sample = k1[lp:].reshape(1, nb, ds, C_KV_HEADS, C_HEAD_DIM)
    c_v_sample = v1[lp:].reshape(1, nb, ds, C_KV_HEADS, C_HEAD_DIM)
    c_idx_sample = ki1[lp:].reshape(1, nb, ds, IDX_DIM)
    return (y_prompt, y_sample, a_k_prompt, a_v_prompt, pool_prompt_out,
            c_k_prompt, c_v_prompt, c_idx_prompt,
            a_k_sample, a_v_sample, pool_sample_out,
            c_k_sample, c_v_sample, c_idx_sample)
```

```python
import functools
import math

import jax
import jax.numpy as jnp
from jax import lax
from jax.experimental import pallas as pl
from jax.experimental.pallas import tpu as pltpu

F32 = jnp.float32
BF16 = jnp.bfloat16

NORM_EPS = 1e-6
NEG = -1e30
BIG = 1e30
SHIFT_SLACK = 1.01
MIN_SOFTMAX_SUM = 2.0 ** -80

CHUNK = 64
A_HEADS = 16
A_HEAD_DIM = 64
A_WIDTH = A_HEADS * A_HEAD_DIM
A_PREV_CHUNKS = 8
A_BAND = (A_PREV_CHUNKS + 1) * CHUNK
A_REL_CLIP = 128
B_WINDOWS = (2, 4, 8, 16)
B_GROUP = 256
B_WIDTH = B_GROUP * len(B_WINDOWS)
B_HIST = max(B_WINDOWS) - 1
C_HEADS = 16
C_KV_HEADS = 4
C_HEAD_DIM = 128
C_GROUPS = C_HEADS // C_KV_HEADS
IDX_HEADS = 8
IDX_DIM = 64
TOPK_MAX = 256
ROPE_THETA = 500000.0
ROPE_FRAC = 4
N_EXPERTS = 8

LANES = 128
POOL_HALO = 16
BAND_CHUNKS_PER_TRIP = 2
VMEM_LIMIT = 56 * 1024 * 1024

_NT = (((1,), (1,)), ((), ()))


def _params(*sem):
    return pltpu.CompilerParams(dimension_semantics=sem, vmem_limit_bytes=VMEM_LIMIT)


def _rms(x, g):
    ms = jnp.mean(x * x, axis=-1, keepdims=True)
    return x * lax.rsqrt(ms + NORM_EPS) * g


def _norm_proj_kernel(x_ref, g_ref, w_ref, o_ref, h_ref):
    @pl.when(pl.program_id(1) == 0)
    def _():
        h_ref[...] = _rms(x_ref[...], g_ref[...]).astype(BF16)

    o_ref[...] = jnp.dot(h_ref[...], w_ref[...], preferred_element_type=F32)


def norm_proj(x, g, w, tm, tn):
    m, d = x.shape
    n = w.shape[1]
    return pl.pallas_call(
        _norm_proj_kernel,
        grid=(m // tm, n // tn),
        in_specs=[pl.BlockSpec((tm, d), lambda i, j: (i, 0)),
                  pl.BlockSpec((1, d), lambda i, j: (0, 0)),
                  pl.BlockSpec((d, tn), lambda i, j: (0, j))],
        out_specs=pl.BlockSpec((tm, tn), lambda i, j: (i, j)),
        out_shape=jax.ShapeDtypeStruct((m, n), F32),
        scratch_shapes=[pltpu.VMEM((tm, d), BF16)],
        compiler_params=_params("parallel", "arbitrary"),
    )(x, g.reshape(1, d), w)


def _stack_norm_proj_kernel(xp_ref, xs_ref, g_ref, w_ref, o_ref, x_ref, h_ref, *, prompt_tiles):
    i = pl.program_id(0)

    def first_column(src_ref):
        x = src_ref[...]
        x_ref[...] = x
        h_ref[...] = _rms(x, g_ref[...]).astype(BF16)

    @pl.when(jnp.logical_and(pl.program_id(1) == 0, i < prompt_tiles))
    def _():
        first_column(xp_ref)

    @pl.when(jnp.logical_and(pl.program_id(1) == 0, i >= prompt_tiles))
    def _():
        first_column(xs_ref)

    o_ref[...] = jnp.dot(h_ref[...], w_ref[...], preferred_element_type=F32)


def stack_norm_proj(xp, xs, g, w, tm, tn):
    lp, d = xp.shape
    m = lp + xs.shape[0]
    n = w.shape[1]
    prompt_tiles = lp // tm
    return pl.pallas_call(
        functools.partial(_stack_norm_proj_kernel, prompt_tiles=prompt_tiles),
        grid=(m // tm, n // tn),
        in_specs=[pl.BlockSpec((tm, d), lambda i, j: (jnp.minimum(i, prompt_tiles - 1), 0)),
                  pl.BlockSpec((tm, d), lambda i, j: (jnp.maximum(i - prompt_tiles, 0), 0)),
                  pl.BlockSpec((1, d), lambda i, j: (0, 0)),
                  pl.BlockSpec((d, tn), lambda i, j: (0, j))],
        out_specs=[pl.BlockSpec((tm, tn), lambda i, j: (i, j)),
                   pl.BlockSpec((tm, d), lambda i, j: (i, 0))],
        out_shape=[jax.ShapeDtypeStruct((m, n), F32), jax.ShapeDtypeStruct((m, d), F32)],
        scratch_shapes=[pltpu.VMEM((tm, d), BF16)],
        compiler_params=_params("parallel", "arbitrary"),
    )(xp, xs, g.reshape(1, d), w)


def _band_pairs(q_ref, nq, kw_ref, vw_ref, nk, bias_ref, windows, o_ref):
    lane = lax.broadcasted_iota(jnp.int32, (1, LANES), 1)
    first = lane < A_HEAD_DIM
    for hp in range(A_HEADS // 2):
        cs = slice(hp * LANES, (hp + 1) * LANES)
        for q_row0, k_row0, key_ok in windows:
            qp = q_ref[pl.ds(q_row0, nq), cs]
            kp = kw_ref[pl.ds(k_row0, nk), cs]
            vp = vw_ref[pl.ds(k_row0, nk), cs]
            qm = jnp.concatenate([jnp.where(first, qp, 0.0), jnp.where(first, 0.0, qp)], axis=0).astype(BF16)
            s = lax.dot_general(qm, kp, _NT, preferred_element_type=F32)
            s = s * (A_HEAD_DIM ** -0.5) + bias_ref[hp]
            if key_ok is not None:
                s = jnp.where(key_ok, s, NEG)
            mx = jnp.max(s, axis=-1, keepdims=True)
            e = jnp.exp(s - mx)
            l = jnp.sum(e, axis=-1, keepdims=True)
            o = jnp.dot(e.astype(BF16), vp, preferred_element_type=F32) / l
            o_ref[pl.ds(q_row0, nq), cs] = jnp.where(first, o[:nq], o[nq:]).astype(o_ref.dtype)


def _band_prompt_kernel(q_ref, kp_ref, kc_ref, vp_ref, vc_ref, bias_ref, o_ref, kw_ref, vw_ref,
                        *, qb, pad, prompt_blocks):
    i = pl.program_id(0)

    @pl.when(i >= prompt_blocks)
    def _():
        o_ref[...] = jnp.zeros(o_ref.shape, o_ref.dtype)

    @pl.when(i < prompt_blocks)
    def _():
        _band_prompt_block(i, q_ref, kp_ref, kc_ref, vp_ref, vc_ref, bias_ref, o_ref, kw_ref, vw_ref, qb, pad)


def _band_prompt_block(i, q_ref, kp_ref, kc_ref, vp_ref, vc_ref, bias_ref, o_ref, kw_ref, vw_ref, qb, pad):
    kw_ref[0:pad, :] = kp_ref[...].astype(BF16)
    kw_ref[pad:pad + qb, :] = kc_ref[...].astype(BF16)
    vw_ref[0:pad, :] = vp_ref[...].astype(BF16)
    vw_ref[pad:pad + qb, :] = vc_ref[...].astype(BF16)
    col = lax.broadcasted_iota(jnp.int32, (1, A_BAND), 1)

    def chunks(c2, carry):
        windows = []
        for u in range(BAND_CHUNKS_PER_TRIP):
            cc = c2 * BAND_CHUNKS_PER_TRIP + u
            r0 = pl.multiple_of(cc * CHUNK, CHUNK)
            first_valid = jnp.where(i == 0, pad - cc * CHUNK, 0)
            windows.append((r0, r0, col >= first_valid))
        _band_pairs(q_ref, CHUNK, kw_ref, vw_ref, A_BAND, bias_ref, windows, o_ref)
        return carry

    lax.fori_loop(0, qb // CHUNK // BAND_CHUNKS_PER_TRIP, chunks, 0)


def band_prompt(proj, bias, lp):
    pad = A_PREV_CHUNKS * CHUNK
    qb = pad
    blk = (qb, A_WIDTH)
    prev = lambda c: (lambda i: (jnp.maximum(i - 1, 0), c))
    cur = lambda c: (lambda i: (i, c))
    return pl.pallas_call(
        functools.partial(_band_prompt_kernel, qb=qb, pad=pad, prompt_blocks=lp // qb),
        grid=(proj.shape[0] // qb,),
        in_specs=[pl.BlockSpec(blk, cur(0)),
                  pl.BlockSpec(blk, prev(1)), pl.BlockSpec(blk, cur(1)),
                  pl.BlockSpec(blk, prev(2)), pl.BlockSpec(blk, cur(2)),
                  pl.BlockSpec((A_HEADS // 2, 2 * CHUNK, A_BAND), lambda i: (0, 0, 0))],
        out_specs=pl.BlockSpec(blk, lambda i: (i, 0)),
        out_shape=jax.ShapeDtypeStruct((proj.shape[0], A_WIDTH), BF16),
        scratch_shapes=[pltpu.VMEM((pad + qb, A_WIDTH), BF16), pltpu.VMEM((pad + qb, A_WIDTH), BF16)],
        compiler_params=_params("parallel"),
    )(proj, proj, proj, proj, proj, bias)


def _band_sample_kernel(q_ref, kn_ref, vn_ref, ck_ref, cv_ref, bias_ref, dst_ref, o_ref, kw_ref, vw_ref,
                        *, a_len, ds):
    del dst_ref
    kw_ref[0:a_len, :] = ck_ref[...].astype(BF16)
    kw_ref[a_len:a_len + ds, :] = kn_ref[...].astype(BF16)
    vw_ref[0:a_len, :] = cv_ref[...].astype(BF16)
    vw_ref[a_len:a_len + ds, :] = vn_ref[...].astype(BF16)
    _band_pairs(q_ref, ds, kw_ref, vw_ref, a_len + ds, bias_ref, [(0, 0, None)], o_ref)


def band_sample(proj, cache_k, cache_v, bias, dst, lp, nb, ds):
    a_len = cache_k.shape[1]
    row = lambda c: (lambda b: (lp // ds + b, c))
    return pl.pallas_call(
        functools.partial(_band_sample_kernel, a_len=a_len, ds=ds),
        grid=(nb,),
        in_specs=[pl.BlockSpec((ds, A_WIDTH), row(0)),
                  pl.BlockSpec((ds, A_WIDTH), row(1)),
                  pl.BlockSpec((ds, A_WIDTH), row(2)),
                  pl.BlockSpec((None, a_len, A_WIDTH), lambda b: (b, 0, 0)),
                  pl.BlockSpec((None, a_len, A_WIDTH), lambda b: (b, 0, 0)),
                  pl.BlockSpec((A_HEADS // 2, 2 * ds, a_len + ds), lambda b: (0, 0, 0)),
                  pl.BlockSpec(memory_space=pl.ANY)],
        out_specs=pl.BlockSpec((ds, A_WIDTH), row(0)),
        out_shape=jax.ShapeDtypeStruct(dst.shape, dst.dtype),
        input_output_aliases={6: 0},
        scratch_shapes=[pltpu.VMEM((a_len + ds, A_WIDTH), BF16), pltpu.VMEM((a_len + ds, A_WIDTH), BF16)],
        compiler_params=_params("parallel"),
    )(proj, proj, proj, cache_k, cache_v, bias, dst)


def _rel_bias_tile(rel_bias, q0, nq, k0, nk):
    rel_max = q0 - k0 + nq - 1
    rel = jnp.clip(rel_max - jnp.arange(nq + nk - 1), -A_REL_CLIP, A_REL_CLIP) + A_REL_CLIP
    ext = rel_bias[:, rel].astype(F32)
    return jnp.stack([ext[:, nq - 1 - i:nq - 1 - i + nk] for i in range(nq)], axis=1)


def _pair_rows(bias):
    h, nq, nk = bias.shape
    return bias.reshape(h // 2, 2 * nq, nk)


def _pool_kernel(prev_ref, cur_ref, w_ref, sc_ref, *refs, tm, prompt, pos0, prompt_blocks=None):
    o_ref, ext_ref = refs[-2:]
    i = pl.program_id(0)
    if prompt_blocks is None:
        _pool_block(i, prev_ref, cur_ref, w_ref, sc_ref, o_ref, ext_ref, tm, prompt, pos0)
        return

    @pl.when(i >= prompt_blocks)
    def _():
        o_ref[...] = jnp.zeros(o_ref.shape, o_ref.dtype)

    @pl.when(i < prompt_blocks)
    def _():
        _pool_block(i, prev_ref, cur_ref, w_ref, sc_ref, o_ref, ext_ref, tm, prompt, pos0)


def _pool_block(i, prev_ref, cur_ref, w_ref, sc_ref, o_ref, ext_ref, tm, prompt, pos0):
    prev = prev_ref[...]
    if prompt:
        prev = jnp.where(i == 0, 0.0, prev)
        pos = i * tm + lax.broadcasted_iota(jnp.int32, (tm, 1), 0)
    else:
        pos = pos0 + lax.broadcasted_iota(jnp.int32, (tm, 1), 0)
    ext_ref[0:POOL_HALO, :] = prev
    ext_ref[POOL_HALO:POOL_HALO + tm, :] = cur_ref[...]
    for g, w in enumerate(B_WINDOWS):
        cs = slice(g * B_GROUP, (g + 1) * B_GROUP)
        tok = ext_ref[POOL_HALO:POOL_HALO + tm, cs]
        tot = tok
        for j in range(1, w):
            tot = tot + ext_ref[POOL_HALO - j:POOL_HALO - j + tm, cs]
        cnt = jnp.minimum(pos + 1, w).astype(F32)
        pooled = (tot / cnt - tok).astype(BF16)
        o = jnp.dot(pooled, w_ref[g], preferred_element_type=F32) * sc_ref[:, cs]
        o_ref[:, cs] = o.astype(o_ref.dtype)


def pool_prompt(proj, pool_w, pool_scale, lp, tm):
    ucol = 3 * A_WIDTH // B_WIDTH
    per = tm // POOL_HALO
    return pl.pallas_call(
        functools.partial(_pool_kernel, tm=tm, prompt=True, pos0=0, prompt_blocks=lp // tm),
        grid=(proj.shape[0] // tm,),
        in_specs=[pl.BlockSpec((POOL_HALO, B_WIDTH), lambda i: (jnp.maximum(i * per - 1, 0), ucol)),
                  pl.BlockSpec((tm, B_WIDTH), lambda i: (i, ucol)),
                  pl.BlockSpec((len(B_WINDOWS), B_GROUP, B_GROUP), lambda i: (0, 0, 0)),
                  pl.BlockSpec((1, B_WIDTH), lambda i: (0, 0))],
        out_specs=pl.BlockSpec((tm, B_WIDTH), lambda i: (i, 0)),
        out_shape=jax.ShapeDtypeStruct((proj.shape[0], B_WIDTH), BF16),
        scratch_shapes=[pltpu.VMEM((POOL_HALO + tm, B_WIDTH), F32)],
        compiler_params=_params("parallel"),
    )(proj, proj, pool_w, pool_scale)


def pool_sample(u_ext, pool_w, pool_scale, dst, row0, past):
    nb, tot, _ = u_ext.shape
    ds = tot - POOL_HALO
    return pl.pallas_call(
        functools.partial(_pool_kernel, tm=ds, prompt=False, pos0=past),
        grid=(nb,),
        in_specs=[pl.BlockSpec((None, POOL_HALO, B_WIDTH), lambda b: (b, 0, 0)),
                  pl.BlockSpec((None, ds, B_WIDTH), lambda b: (b, POOL_HALO // ds, 0)),
                  pl.BlockSpec((len(B_WINDOWS), B_GROUP, B_GROUP), lambda b: (0, 0, 0)),
                  pl.BlockSpec((1, B_WIDTH), lambda b: (0, 0)),
                  pl.BlockSpec(memory_space=pl.ANY)],
        out_specs=pl.BlockSpec((ds, B_WIDTH), lambda b: (row0 // ds + b, 0)),
        out_shape=jax.ShapeDtypeStruct(dst.shape, dst.dtype),
        input_output_aliases={4: 0},
        scratch_shapes=[pltpu.VMEM((POOL_HALO + ds, B_WIDTH), F32)],
        compiler_params=_params("parallel"),
    )(u_ext, u_ext, pool_w, pool_scale, dst)


def _mm_res_kernel(*refs, n_in):
    xs, ws = refs[:n_in], refs[n_in:2 * n_in]
    res_ref, o_ref = refs[2 * n_in], refs[2 * n_in + 1]
    acc = res_ref[...]
    for x_ref, w_ref in zip(xs, ws):
        acc = acc + jnp.dot(x_ref[...], w_ref[...], preferred_element_type=F32)
    o_ref[...] = acc


def mm_res(xs, ws, res, tm, tn):
    m, n = res.shape
    n_in = len(xs)
    in_specs = ([pl.BlockSpec((tm, x.shape[1]), lambda i, j: (i, 0)) for x in xs]
                + [pl.BlockSpec((w.shape[0], tn), lambda i, j: (0, j)) for w in ws]
                + [pl.BlockSpec((tm, tn), lambda i, j: (i, j))])
    return pl.pallas_call(
        functools.partial(_mm_res_kernel, n_in=n_in),
        grid=(m // tm, n // tn),
        in_specs=in_specs,
        out_specs=pl.BlockSpec((tm, tn), lambda i, j: (i, j)),
        out_shape=jax.ShapeDtypeStruct((m, n), F32),
        compiler_params=_params("parallel", "arbitrary"),
    )(*xs, *ws, res)


def _swiglu_tile(h, wg, wu, wd):
    a = jnp.dot(h, wg, preferred_element_type=F32)
    b = jnp.dot(h, wu, preferred_element_type=F32)
    act = (a * jax.nn.sigmoid(a) * b).astype(BF16)
    return jnp.dot(act, wd, preferred_element_type=F32)


def _ffn_kernel(y_ref, g_ref, wg_ref, wu_ref, wd_ref, o_ref, h_ref, acc_ref):
    f = pl.program_id(1)

    @pl.when(f == 0)
    def _():
        h_ref[...] = _rms(y_ref[...], g_ref[...]).astype(BF16)
        acc_ref[...] = jnp.zeros_like(acc_ref)

    acc_ref[...] += _swiglu_tile(h_ref[...], wg_ref[...], wu_ref[...], wd_ref[...])

    @pl.when(f == pl.num_programs(1) - 1)
    def _():
        o_ref[...] = y_ref[...] + acc_ref[...]


def ffn(y, g, wg, wu, wd, tm, tf):
    m, d = y.shape
    ff = wg.shape[1]
    return pl.pallas_call(
        _ffn_kernel,
        grid=(m // tm, ff // tf),
        in_specs=[pl.BlockSpec((tm, d), lambda i, f: (i, 0)),
                  pl.BlockSpec((1, d), lambda i, f: (0, 0)),
                  pl.BlockSpec((d, tf), lambda i, f: (0, f)),
                  pl.BlockSpec((d, tf), lambda i, f: (0, f)),
                  pl.BlockSpec((tf, d), lambda i, f: (f, 0))],
        out_specs=pl.BlockSpec((tm, d), lambda i, f: (i, 0)),
        out_shape=jax.ShapeDtypeStruct((m, d), F32),
        scratch_shapes=[pltpu.VMEM((tm, d), BF16), pltpu.VMEM((tm, d), F32)],
        compiler_params=_params("parallel", "arbitrary"),
    )(y, g.reshape(1, d), wg, wu, wd)


def _rope_tables(pos, head_dim):
    rot = head_dim // ROPE_FRAC
    half = rot // 2
    inv = jnp.exp(-math.log(ROPE_THETA) * jnp.arange(half, dtype=F32) * (2.0 / rot))
    ang = pos.astype(F32)[:, None] * inv[None, :]
    cos, sin = jnp.cos(ang), jnp.sin(ang)
    m = pos.shape[0]
    one = jnp.ones((m, head_dim - rot), F32)
    zero_r = jnp.zeros((m, head_dim - rot), F32)
    zero_h = jnp.zeros((m, half), F32)
    c = jnp.concatenate([cos, cos, one], axis=1)
    s_dn = jnp.concatenate([-sin, zero_h, zero_r], axis=1)
    s_up = jnp.concatenate([zero_h, sin, zero_r], axis=1)
    rep = LANES // head_dim
    return jnp.stack([jnp.tile(c, (1, rep)), jnp.tile(s_dn, (1, rep)), jnp.tile(s_up, (1, rep))])


def _rot(x, tab_ref, half):
    return (x * tab_ref[0] + pltpu.roll(x, LANES - half, 1) * tab_ref[1]
            + pltpu.roll(x, half, 1) * tab_ref[2])


def _rope_kernel(main_ref, tail_ref, tq_ref, ti_ref, kv_ref, t32_ref, ob_ref,
                 *, n_q, n_k, n_v, n_qi, wi_scale, q_scale):
    half_qk = C_HEAD_DIM // ROPE_FRAC // 2
    half_i = IDX_DIM // ROPE_FRAC // 2
    for c in range(n_q + n_k + n_v + n_qi):
        cs = slice(c * LANES, (c + 1) * LANES)
        x = main_ref[:, cs]
        if c < n_q + n_k:
            x = _rot(x, tq_ref, half_qk)
        elif c >= n_q + n_k + n_v:
            x = _rot(x, ti_ref, half_i)
        if n_q <= c < n_q + n_k + n_v:
            kv_ref[:, (c - n_q) * LANES:(c - n_q + 1) * LANES] = x
        ob_ref[:, cs] = (x * q_scale if c < n_q else x).astype(BF16)
    t = tail_ref[...]
    lane = lax.broadcasted_iota(jnp.int32, (1, LANES), 1)
    r = _rot(t, ti_ref, half_i)
    t32_ref[...] = jnp.where(lane < IDX_DIM, r, t * wi_scale)
    c = n_q + n_k + n_v + n_qi
    ob_ref[:, c * LANES:(c + 1) * LANES] = jnp.where(lane < IDX_DIM, r, pltpu.roll(r, IDX_DIM, 1)).astype(BF16)


def rope_all(main, tail, tab_qk, tab_idx, tm):
    m, nmain = main.shape
    n_q = C_HEADS * C_HEAD_DIM // LANES
    n_k = C_KV_HEADS * C_HEAD_DIM // LANES
    n_qi = IDX_HEADS * IDX_DIM // LANES
    wi_scale = (IDX_HEADS ** -0.5) * (IDX_DIM ** -0.5)
    q_scale = (C_HEAD_DIM ** -0.5) * math.log2(math.e)
    row = lambda i: (i, 0)
    return pl.pallas_call(
        functools.partial(_rope_kernel, n_q=n_q, n_k=n_k, n_v=n_k, n_qi=n_qi, wi_scale=wi_scale, q_scale=q_scale),
        grid=(m // tm,),
        in_specs=[pl.BlockSpec((tm, nmain), row),
                  pl.BlockSpec((tm, LANES), row),
                  pl.BlockSpec((3, tm, LANES), lambda i: (0, i, 0)),
                  pl.BlockSpec((3, tm, LANES), lambda i: (0, i, 0))],
        out_specs=[pl.BlockSpec((tm, 2 * n_k * LANES), row),
                   pl.BlockSpec((tm, LANES), row),
                   pl.BlockSpec((tm, nmain + LANES), row)],
        out_shape=[jax.ShapeDtypeStruct((m, 2 * n_k * LANES), F32),
                   jax.ShapeDtypeStruct((m, LANES), F32),
                   jax.ShapeDtypeStruct((m, nmain + LANES), BF16)],
        compiler_params=_params("parallel"),
    )(main, tail, tab_qk, tab_idx)


def _dsa_kernel(q_ref, qi_ref, wi_ref, kn_ref, vn_ref, kin_ref, *refs, tq, kb, past, n_valid, q_pos0, topk):
    ck_refs, cv_refs = refs[:C_KV_HEADS], refs[C_KV_HEADS:2 * C_KV_HEADS]
    ci_ref, o_ref, k_ref, v_ref, ki_ref, sc_ref, qs_ref, acc_ref, m_ref, l_ref = refs[2 * C_KV_HEADS:]
    i = 0
    rows = C_GROUPS * tq
    sub = kb // LANES
    kf = float(topk)

    new = past + tq
    ci = ci_ref[...].astype(BF16)
    for g in range(C_KV_HEADS):
        gs = slice(g * C_HEAD_DIM, (g + 1) * C_HEAD_DIM)
        k_ref[0:past, gs] = ck_refs[g][...].astype(BF16)
        v_ref[0:past, gs] = cv_refs[g][...].astype(BF16)
    ki_ref[0:past, :] = jnp.concatenate([ci, ci], axis=1)
    k_ref[past:new, :] = kn_ref[...]
    v_ref[past:new, :] = vn_ref[...]
    ki_ref[past:new, :] = kin_ref[...]
    n_pad = k_ref.shape[0] - new
    k_ref[new:, :] = jnp.zeros((n_pad, k_ref.shape[1]), BF16)
    v_ref[new:, :] = jnp.zeros((n_pad, v_ref.shape[1]), BF16)
    ki_ref[new:, :] = jnp.zeros((n_pad, ki_ref.shape[1]), BF16)

    q_pos = q_pos0 + i * tq + lax.broadcasted_iota(jnp.int32, (tq, 1), 0)
    q_chunk = q_pos // CHUNK
    last_chunk = (q_pos0 + i * tq + tq - 1) // CHUNK
    kv_limit = min(n_valid, (last_chunk + 1) * CHUNK)
    nkb = (kv_limit + kb - 1) // kb

    lane = lax.broadcasted_iota(jnp.int32, (1, LANES), 1)
    first = lane < IDX_DIM
    wi = wi_ref[:, IDX_DIM:IDX_DIM + IDX_HEADS]

    def score_block(b, carry):
        for c in range(sub):
            off = pl.multiple_of(b * kb + c * LANES, LANES)
            kib = ki_ref[pl.ds(off, LANES), :]
            acc = jnp.zeros((tq, LANES), F32)
            for hp in range(IDX_HEADS // 2):
                qp = qi_ref[:, hp * LANES:(hp + 1) * LANES]
                for half in range(2):
                    keep = first if half == 0 else jnp.logical_not(first)
                    qm = jnp.where(keep, qp, jnp.zeros_like(qp))
                    d = lax.dot_general(qm, kib, _NT, preferred_element_type=F32)
                    h = hp * 2 + half
                    acc = acc + jnp.maximum(d, 0.0) * wi[:, h:h + 1]
            k_pos = off + lane
            adm = jnp.logical_and(k_pos // CHUNK <= q_chunk, k_pos < n_valid)
            sc_ref[b, :, c * LANES:(c + 1) * LANES] = jnp.where(adm, acc, NEG)
        return carry

    lax.fori_loop(0, nkb, score_block, 0)

    def lane_sum(x):
        return jnp.sum(x, axis=1, keepdims=True)

    def count_ge(t):
        def body(b, acc):
            for c in range(sub):
                blk = sc_ref[b, :, c * LANES:(c + 1) * LANES]
                acc = acc + jnp.where(blk >= t, 1.0, 0.0)
            return acc
        return lane_sum(lax.fori_loop(0, nkb, body, jnp.zeros((tq, LANES), F32)))

    def stats(b, carry):
        mx, mn, cnt = carry
        for c in range(sub):
            blk = sc_ref[b, :, c * LANES:(c + 1) * LANES]
            ok = blk > 0.5 * NEG
            mx = jnp.maximum(mx, blk)
            mn = jnp.minimum(mn, jnp.where(ok, blk, BIG))
            cnt = cnt + jnp.where(ok, 1.0, 0.0)
        return mx, mn, cnt

    mx, mn, cnt = lax.fori_loop(
        0, nkb, stats,
        (jnp.full((tq, LANES), NEG, F32), jnp.full((tq, LANES), BIG, F32), jnp.zeros((tq, LANES), F32)))
    row_max = jnp.max(mx, axis=1, keepdims=True)
    row_min = jnp.min(mn, axis=1, keepdims=True)
    n_adm = lane_sum(cnt)

    done0 = jnp.where(n_adm <= kf, 1.0, 0.0)
    state0 = (row_min, jnp.full((tq, 1), BIG, F32), row_max, jnp.full((tq, 1), 0.5 * NEG, F32), done0)

    def bisect(state, n_steps):
        def cond(c):
            it, st = c
            return jnp.logical_and(it < n_steps, jnp.min(st[4]) < 0.5)

        def body(c):
            it, (lo, hi, mid, thr, done) = c
            cnt = count_ge(mid)
            live = done < 0.5
            hit = jnp.logical_and(live, cnt == kf)
            ge = cnt >= kf
            thr = jnp.where(hit, mid, thr)
            done = jnp.where(hit, 1.0, done)
            lo = jnp.where(ge, mid, lo)
            hi = jnp.where(ge, hi, mid)
            return it + 1, (lo, hi, 0.5 * (lo + hi), thr, done)

        return lax.while_loop(cond, body, (jnp.int32(0), state))[1]

    def snap(state):
        lo, hi, mid, thr, done = state

        def body(b, carry):
            v_lo, v_hi = carry
            for c in range(sub):
                blk = sc_ref[b, :, c * LANES:(c + 1) * LANES]
                v_lo = jnp.minimum(v_lo, jnp.where(blk >= lo, blk, BIG))
                v_hi = jnp.maximum(v_hi, jnp.where(blk < hi, blk, NEG))
            return v_lo, v_hi

        v_lo, v_hi = lax.fori_loop(0, nkb, body,
                                   (jnp.full((tq, LANES), BIG, F32), jnp.full((tq, LANES), NEG, F32)))
        v_lo = jnp.min(v_lo, axis=1, keepdims=True)
        v_hi = jnp.max(v_hi, axis=1, keepdims=True)
        live = done < 0.5
        tie = jnp.logical_and(live, v_lo == v_hi)
        thr = jnp.where(tie, v_lo, thr)
        done = jnp.where(tie, 1.0, done)
        lo = jnp.where(live, v_lo, lo)
        return (lo, hi, 0.5 * (lo + hi), thr, done), jnp.where(tie, 1.0, 0.0)

    state = bisect(state0, 32)

    def refine_cond(c):
        rounds, st, _ = c
        return jnp.logical_and(rounds < 10, jnp.min(st[4]) < 0.5)

    def refine_body(c):
        rounds, st, tie = c
        st, new_tie = snap(st)
        st = bisect(st, 32)
        return rounds + 1, st, jnp.maximum(tie, new_tie)

    _, state, tie = lax.while_loop(refine_cond, refine_body,
                                   (jnp.int32(0), state, jnp.zeros((tq, 1), F32)))
    thr = state[3]
    any_tie = jnp.max(tie) > 0.5

    @pl.when(jnp.logical_not(any_tie))
    def _():
        def body(b, carry):
            for c in range(sub):
                cs = slice(c * LANES, (c + 1) * LANES)
                sc_ref[b, :, cs] = jnp.where(sc_ref[b, :, cs] >= thr, 0.0, NEG)
            return carry
        lax.fori_loop(0, nkb, body, 0)

    @pl.when(any_tie)
    def _():
        def gt_body(b, acc):
            for c in range(sub):
                acc = acc + jnp.where(sc_ref[b, :, c * LANES:(c + 1) * LANES] > thr, 1.0, 0.0)
            return acc
        need = kf - lane_sum(lax.fori_loop(0, nkb, gt_body, jnp.zeros((tq, LANES), F32)))

        def count_eq_upto(j):
            def body(b, acc):
                for c in range(sub):
                    blk = sc_ref[b, :, c * LANES:(c + 1) * LANES]
                    idx = (b * kb + c * LANES + lane).astype(F32)
                    acc = acc + jnp.where(jnp.logical_and(blk == thr, idx <= j), 1.0, 0.0)
                return acc
            return lane_sum(lax.fori_loop(0, nkb, body, jnp.zeros((tq, LANES), F32)))

        def idx_body(_, c):
            lo_j, hi_j = c
            mid_j = jnp.floor(0.5 * (lo_j + hi_j))
            ok = count_eq_upto(mid_j) >= need
            return jnp.where(ok, lo_j, mid_j), jnp.where(ok, mid_j, hi_j)

        n_steps = max(1, math.ceil(math.log2(sc_ref.shape[0] * kb + 1)))
        _, last = lax.fori_loop(
            0, n_steps, idx_body,
            (jnp.full((tq, 1), -1.0, F32), jnp.full((tq, 1), float(sc_ref.shape[0] * kb), F32)))
        last = jnp.where(tie > 0.5, last, BIG)

        def body(b, carry):
            for c in range(sub):
                cs = slice(c * LANES, (c + 1) * LANES)
                blk = sc_ref[b, :, cs]
                idx = (b * kb + c * LANES + lane).astype(F32)
                sel = jnp.logical_or(blk > thr, jnp.logical_and(blk == thr, idx <= last))
                sc_ref[b, :, cs] = jnp.where(sel, 0.0, NEG)
            return carry
        lax.fori_loop(0, nkb, body, 0)

    for g in range(C_KV_HEADS):
        for hh in range(C_GROUPS):
            h = g * C_GROUPS + hh
            qs_ref[g, hh * tq:(hh + 1) * tq, :] = q_ref[:, h * C_HEAD_DIM:(h + 1) * C_HEAD_DIM]
    m_ref[...] = jnp.full(m_ref.shape, NEG, F32)
    l_ref[...] = jnp.zeros(l_ref.shape, F32)
    acc_ref[...] = jnp.zeros(acc_ref.shape, F32)

    def attend(b, carry):
        off = pl.multiple_of(b * kb, kb)
        bias = sc_ref[b]
        bias = jnp.concatenate([bias] * C_GROUPS, axis=0)
        for g in range(C_KV_HEADS):
            cs = slice(g * C_HEAD_DIM, (g + 1) * C_HEAD_DIM)
            kblk = k_ref[pl.ds(off, kb), cs]
            vblk = v_ref[pl.ds(off, kb), cs]
            s = lax.dot_general(qs_ref[g], kblk, _NT, preferred_element_type=F32)
            s = s + bias
            m_old = m_ref[g]
            m_new = jnp.maximum(m_old, jnp.max(s, axis=1, keepdims=True))
            alpha = jnp.exp2(m_old - m_new)
            p = jnp.exp2(s - m_new)
            l_ref[g] = alpha * l_ref[g] + jnp.sum(p, axis=1, keepdims=True)
            acc_ref[g] = alpha * acc_ref[g] + jnp.dot(p.astype(BF16), vblk, preferred_element_type=F32)
            m_ref[g] = m_new
        return carry

    lax.fori_loop(0, nkb, attend, 0, unroll=True)

    for g in range(C_KV_HEADS):
        o = acc_ref[g] / l_ref[g]
        for hh in range(C_GROUPS):
            h = g * C_GROUPS + hh
            o_ref[:, h * C_HEAD_DIM:(h + 1) * C_HEAD_DIM] = o[hh * tq:(hh + 1) * tq, :].astype(o_ref.dtype)


def dsa_sample(ob, t32, cache_k, cache_v, cache_i, *, row0, kb, topk):
    nb, past = cache_k.shape[:2]
    head = lambda g: pl.BlockSpec((None, past, C_HEAD_DIM), lambda b, i: (b, 0, g))
    heads = [head(g) for g in range(C_KV_HEADS)]
    n_kv = C_KV_HEADS * C_HEAD_DIM
    ds = (ob.shape[0] - row0) // nb
    n_q = C_HEADS * C_HEAD_DIM
    n_qi = IDX_HEADS * IDX_DIM
    s_all = past + ds
    s_pad = -(-s_all // kb) * kb
    rows = C_GROUPS * ds
    col = lambda w, off: (lambda b, i: (row0 // ds + b, off // w))
    cmap = lambda b, i: (b, 0, 0)
    return pl.pallas_call(
        functools.partial(_dsa_kernel, tq=ds, kb=kb, past=past, n_valid=s_all, q_pos0=past, topk=topk),
        grid=(nb, 1),
        in_specs=[pl.BlockSpec((ds, n_q), col(n_q, 0)),
                  pl.BlockSpec((ds, n_qi), col(n_qi, n_q + 2 * n_kv)),
                  pl.BlockSpec((ds, LANES), col(LANES, 0)),
                  pl.BlockSpec((ds, n_kv), col(n_kv, n_q)),
                  pl.BlockSpec((ds, n_kv), col(n_kv, n_q + n_kv)),
                  pl.BlockSpec((ds, LANES), col(LANES, n_q + 2 * n_kv + n_qi))]
                 + heads + heads + [pl.BlockSpec((None, past, IDX_DIM), cmap)],
        out_specs=pl.BlockSpec((ds, n_q), lambda b, i: (b, 0)),
        out_shape=jax.ShapeDtypeStruct((nb * ds, n_q), BF16),
        scratch_shapes=[pltpu.VMEM((s_pad, n_kv), BF16),
                        pltpu.VMEM((s_pad, n_kv), BF16),
                        pltpu.VMEM((s_pad, LANES), BF16),
                        pltpu.VMEM((s_pad // kb, ds, kb), F32),
                        pltpu.VMEM((C_KV_HEADS, rows, C_HEAD_DIM), BF16),
                        pltpu.VMEM((C_KV_HEADS, rows, C_HEAD_DIM), F32),
                        pltpu.VMEM((C_KV_HEADS, rows, 1), F32),
                        pltpu.VMEM((C_KV_HEADS, rows, 1), F32)],
        compiler_params=_params("parallel", "arbitrary"),
    )(ob, ob, t32, ob, ob, ob, *([cache_k] * C_KV_HEADS), *([cache_v] * C_KV_HEADS), cache_i)


def _dsa_t_kernel(q_ref, qi_ref, wi_ref, k_ref, vt_ref, ki_ref, o_ref,
                  sc_ref, qim_ref, acc_ref, m_ref, l_ref, l8_ref, kmax_ref,
                  *, tq, kb, qb, pb, cb, n_valid, q_pos0, topk):
    i = pl.program_id(1)
    kf = float(topk)

    q_pos = q_pos0 + i * tq + lax.broadcasted_iota(jnp.int32, (1, tq), 1)
    q_chunk = q_pos // CHUNK
    last_chunk = (q_pos0 + i * tq + tq - 1) // CHUNK
    kv_limit = jnp.minimum(n_valid, (last_chunk + 1) * CHUNK)
    nkb = (kv_limit + kb - 1) // kb
    ncb = (kv_limit + cb - 1) // cb

    def col_reduce(x, op):
        groups = x.shape[0] // 8
        chains = 8 if groups % 8 == 0 else 1
        return op(op(x.reshape(chains, groups // chains, 8, tq), axis=1), axis=0)

    col_sum = lambda x: col_reduce(x, jnp.sum)
    col_max = lambda x: col_reduce(x, jnp.max)
    col_min = lambda x: col_reduce(x, jnp.min)

    lane = lax.broadcasted_iota(jnp.int32, (1, LANES), 1)
    first = lane < IDX_DIM
    for hp in range(IDX_HEADS // 2):
        qp = qi_ref[:, hp * LANES:(hp + 1) * LANES]
        qim_ref[2 * hp] = jnp.where(first, qp, jnp.zeros_like(qp))
        qim_ref[2 * hp + 1] = jnp.where(first, jnp.zeros_like(qp), qp)
    block_iota = lax.broadcasted_iota(jnp.int32, (pb, 1), 0)

    def score_blocks(b0, n):
        for u in range(n):
            off = pl.multiple_of((b0 + u) * pb, pb)
            kib = ki_ref[pl.ds(off, pb), :]
            parts = []
            for h in range(IDX_HEADS):
                d = lax.dot_general(kib, qim_ref[h], _NT, preferred_element_type=F32)
                parts.append(jnp.maximum(d, 0.0) * wi_ref[h:h + 1, :])
            while len(parts) > 1:
                parts = [a + b for a, b in zip(parts[0::2], parts[1::2])]
            acc = parts[0]
            k_pos = off + block_iota
            adm = jnp.logical_and(k_pos // CHUNK <= q_chunk, k_pos < n_valid)
            sc_ref[pl.ds(off, pb), :] = jnp.where(adm, acc, NEG)

    n_pb = ncb * (cb // pb)

    def four_score_blocks(t, carry):
        score_blocks(4 * t, 4)
        return carry

    lax.fori_loop(0, n_pb // 4, four_score_blocks, 0)

    @pl.when(n_pb % 4 >= 2)
    def _():
        score_blocks((n_pb // 4) * 4, 2)

    key_iota = lax.broadcasted_iota(jnp.int32, (cb, 1), 0)

    def blocks(body, init):
        def step(b, carry):
            off = pl.multiple_of(b * cb, cb)
            return body(off, sc_ref[pl.ds(off, cb), :], carry)
        return lax.fori_loop(0, ncb, step, init)

    def total(x):
        return jnp.sum(x, axis=0, keepdims=True)

    def count_ge(t):
        return total(blocks(lambda off, blk, acc: acc + col_sum(jnp.where(blk >= t, 1.0, 0.0)),
                            jnp.zeros((8, tq), F32)))

    def stats(off, blk, carry):
        mx, mn, cnt, pos, nonneg = carry
        ok = blk > 0.5 * NEG
        return (jnp.maximum(mx, col_max(blk)), jnp.minimum(mn, col_min(jnp.where(ok, blk, BIG))),
                cnt + col_sum(jnp.where(ok, 1.0, 0.0)),
                pos + col_sum(jnp.where(blk > 0.0, 1.0, 0.0)),
                nonneg + col_sum(jnp.where(blk >= 0.0, 1.0, 0.0)))

    zeros8 = jnp.zeros((8, tq), F32)
    mx, mn, cnt, pos, nonneg = blocks(stats, (jnp.full((8, tq), NEG, F32), jnp.full((8, tq), BIG, F32),
                                              zeros8, zeros8, zeros8))
    row_max = jnp.max(mx, axis=0, keepdims=True)
    row_min = jnp.min(mn, axis=0, keepdims=True)
    n_adm = total(cnt)
    n_pos = total(pos)
    n_nonneg = total(nonneg)

    open_row = n_adm > kf
    above = n_pos >= kf
    below = n_nonneg < kf
    zero_tie = jnp.logical_and(open_row, jnp.logical_not(jnp.logical_or(above, below)))
    lo0 = jnp.where(above, 0.0, row_min)
    hi0 = jnp.where(below, 0.0, BIG)
    mid0 = jnp.where(below, 0.5 * (lo0 + hi0), row_max)
    thr0 = jnp.where(zero_tie, 0.0, 0.5 * NEG)
    done0 = jnp.where(jnp.logical_or(jnp.logical_not(open_row), zero_tie), 1.0, 0.0)
    tie0 = jnp.where(zero_tie, 1.0, 0.0)
    c_lo0 = jnp.where(above, n_pos, n_adm)
    c_hi0 = jnp.where(below, n_nonneg, 0.0)
    state0 = (lo0, hi0, mid0, thr0, done0, c_lo0, c_hi0)
    log_k = math.log2(kf + 0.5)

    def bisect(state, n_steps):
        def cond(c):
            it, st = c
            return jnp.logical_and(it < n_steps, jnp.min(st[4]) < 0.5)

        def body(c):
            it, (lo, hi, mid, thr, done, c_lo, c_hi) = c
            cnt = count_ge(mid)
            hit = jnp.logical_and(done < 0.5, cnt == kf)
            ge = cnt >= kf
            thr = jnp.where(hit, mid, thr)
            done = jnp.where(hit, 1.0, done)
            lo = jnp.where(ge, mid, lo)
            hi = jnp.where(ge, hi, mid)
            c_lo = jnp.where(ge, cnt, c_lo)
            c_hi = jnp.where(ge, c_hi, cnt)
            log_lo = jnp.log2(jnp.maximum(c_lo, 1.0))
            frac = (log_lo - log_k) / jnp.maximum(log_lo - jnp.log2(jnp.maximum(c_hi, 0.5)), 1e-6)
            frac = jnp.clip(frac, 0.02, 0.98)
            frac = jnp.where(jnp.logical_or(it % 3 == 2, hi >= BIG), 0.5, frac)
            return it + 1, (lo, hi, lo + (hi - lo) * frac, thr, done, c_lo, c_hi)

        return lax.while_loop(cond, body, (jnp.int32(0), state))[1]

    def snap(state):
        lo, hi, mid, thr, done, c_lo, c_hi = state

        def body(off, blk, carry):
            v_lo, v_hi = carry
            return (jnp.minimum(v_lo, col_min(jnp.where(blk >= lo, blk, BIG))),
                    jnp.maximum(v_hi, col_max(jnp.where(blk < hi, blk, NEG))))

        v_lo, v_hi = blocks(body, (jnp.full((8, tq), BIG, F32), jnp.full((8, tq), NEG, F32)))
        v_lo = jnp.min(v_lo, axis=0, keepdims=True)
        v_hi = jnp.max(v_hi, axis=0, keepdims=True)
        live = done < 0.5
        tie = jnp.logical_and(live, v_lo == v_hi)
        thr = jnp.where(tie, v_lo, thr)
        done = jnp.where(tie, 1.0, done)
        lo = jnp.where(live, v_lo, lo)
        return (lo, hi, 0.5 * (lo + hi), thr, done, c_lo, c_hi), jnp.where(tie, 1.0, 0.0)

    state = bisect(state0, 32)

    def refine_cond(c):
        rounds, st, _ = c
        return jnp.logical_and(rounds < 10, jnp.min(st[4]) < 0.5)

    def refine_body(c):
        rounds, st, tie = c
        st, new_tie = snap(st)
        st = bisect(st, 32)
        return rounds + 1, st, jnp.maximum(tie, new_tie)

    _, state, tie = lax.while_loop(refine_cond, refine_body, (jnp.int32(0), state, tie0))
    thr = state[3]
    any_tie = jnp.max(tie) > 0.5

    @pl.when(jnp.logical_not(any_tie))
    def _():
        def body(off, blk, carry):
            sc_ref[pl.ds(off, cb), :] = jnp.where(blk >= thr, 0.0, NEG)
            return carry
        blocks(body, 0)

    @pl.when(any_tie)
    def _():
        need = kf - total(blocks(lambda off, blk, acc: acc + col_sum(jnp.where(blk > thr, 1.0, 0.0)),
                                 jnp.zeros((8, tq), F32)))

        def count_eq_upto(j):
            def body(off, blk, acc):
                idx = (off + key_iota).astype(F32)
                return acc + col_sum(jnp.where(jnp.logical_and(blk == thr, idx <= j), 1.0, 0.0))
            return total(blocks(body, jnp.zeros((8, tq), F32)))

        def idx_cond(c):
            it, (lo_j, hi_j, settled) = c
            return jnp.logical_and(it < n_steps, jnp.min(settled) < 0.5)

        def idx_body(c):
            it, (lo_j, hi_j, settled) = c
            mid_j = jnp.floor(0.5 * (lo_j + hi_j))
            cnt = count_eq_upto(mid_j)
            ok = cnt >= need
            live = settled < 0.5
            lo_j = jnp.where(jnp.logical_and(live, jnp.logical_not(ok)), mid_j, lo_j)
            hi_j = jnp.where(jnp.logical_and(live, ok), mid_j, hi_j)
            settled = jnp.where(jnp.logical_or(cnt == need, hi_j - lo_j <= 1.0), 1.0, settled)
            return it + 1, (lo_j, hi_j, settled)

        n_keys = sc_ref.shape[0]
        n_steps = max(1, math.ceil(math.log2(n_keys + 1))) + 1
        _, (_, last, _) = lax.while_loop(
            idx_cond, idx_body,
            (jnp.int32(0), (jnp.full((1, tq), -1.0, F32), jnp.full((1, tq), float(n_keys), F32),
                            jnp.where(tie > 0.5, 0.0, 1.0))))
        last = jnp.where(tie > 0.5, last, BIG)

        def body(off, blk, carry):
            idx = (off + key_iota).astype(F32)
            sel = jnp.logical_or(blk > thr, jnp.logical_and(blk == thr, idx <= last))
            sc_ref[pl.ds(off, cb), :] = jnp.where(sel, 0.0, NEG)
            return carry
        blocks(body, 0)

    halves = kb // qb

    def head_scores(off, h):
        gs = slice((h // C_GROUPS) * C_HEAD_DIM, (h // C_GROUPS + 1) * C_HEAD_DIM)
        out = []
        for j in range(halves):
            rows = pl.ds(off + j * qb, qb)
            s = lax.dot_general(k_ref[rows, gs], q_ref[:, h * C_HEAD_DIM:(h + 1) * C_HEAD_DIM],
                                _NT, preferred_element_type=F32)
            out.append(s + sc_ref[rows, :])
        return out, gs

    @pl.when(i == 0)
    def _():
        for g in range(C_KV_HEADS):
            gs = slice(g * C_HEAD_DIM, (g + 1) * C_HEAD_DIM)

            def body(b, mx):
                kk = k_ref[pl.ds(pl.multiple_of(b * cb, cb), cb), gs].astype(F32)
                return jnp.maximum(mx, jnp.sum(kk * kk, axis=1, keepdims=True))

            mx = lax.fori_loop(0, k_ref.shape[0] // cb, body, jnp.zeros((cb, 1), F32))
            kmax_ref[g] = jnp.full((1, tq), 1.0, F32) * jnp.max(mx)

    ones = jnp.ones((8, C_HEAD_DIM), BF16)
    for h in range(C_HEADS):
        qh = q_ref[:, h * C_HEAD_DIM:(h + 1) * C_HEAD_DIM].astype(F32)
        qn2 = lax.dot_general(ones, (qh * qh).astype(BF16), _NT, preferred_element_type=F32)[0:1, :]
        m_ref[h] = jnp.sqrt(qn2 * kmax_ref[h // C_GROUPS]) * SHIFT_SLACK
    l8_ref[...] = jnp.zeros(l8_ref.shape, F32)
    acc_ref[...] = jnp.zeros(acc_ref.shape, F32)

    def attend_fixed(b0, n):
        for h in range(C_HEADS):
            lsum, upd = None, None
            for u in range(n):
                b = b0 + u
                ss, gs = head_scores(pl.multiple_of(b * kb, kb), h)
                ps = [jnp.exp2(s - m_ref[h]) for s in ss]
                part = sum(col_sum(p) for p in ps)
                p_all = jnp.concatenate([p.astype(BF16) for p in ps], axis=0)
                pv = jnp.dot(vt_ref[b, gs, :], p_all, preferred_element_type=F32)
                lsum = part if lsum is None else lsum + part
                upd = pv if upd is None else upd + pv
            l8_ref[h] += lsum
            acc_ref[h] += upd

    def eight_blocks(t, carry):
        attend_fixed(8 * t, 8)
        return carry

    lax.fori_loop(0, nkb // 8, eight_blocks, 0)
    rest = (nkb // 8) * 8

    @pl.when(nkb % 8 >= 4)
    def _():
        attend_fixed(rest, 4)

    @pl.when(nkb % 4 >= 2)
    def _():
        attend_fixed(rest + ((nkb % 8) // 4) * 4, 2)
    l_min = jnp.full((1, tq), BIG, F32)
    for h in range(C_HEADS):
        l_ref[h] = total(l8_ref[h])
        l_min = jnp.minimum(l_min, l_ref[h])

    @pl.when(jnp.logical_not(jnp.min(l_min) >= MIN_SOFTMAX_SUM))
    def _():
        m_ref[...] = jnp.full(m_ref.shape, NEG, F32)
        l_ref[...] = jnp.zeros(l_ref.shape, F32)
        acc_ref[...] = jnp.zeros(acc_ref.shape, F32)

        def attend(b, carry):
            off = pl.multiple_of(b * kb, kb)
            for h in range(C_HEADS):
                ss, gs = head_scores(off, h)
                m_old = m_ref[h]
                m_new = m_old
                for s in ss:
                    m_new = jnp.maximum(m_new, jnp.max(col_max(s), axis=0, keepdims=True))
                alpha = jnp.exp2(m_old - m_new)
                ps = [jnp.exp2(s - m_new) for s in ss]
                l_ref[h] = alpha * l_ref[h] + total(sum(col_sum(p) for p in ps))
                p_all = jnp.concatenate([p.astype(BF16) for p in ps], axis=0)
                acc_ref[h] = alpha * acc_ref[h] + jnp.dot(vt_ref[b, gs, :], p_all,
                                                          preferred_element_type=F32)
                m_ref[h] = m_new
            return carry

        lax.fori_loop(0, nkb, attend, 0)

    for h in range(C_HEADS):
        o = acc_ref[h] / l_ref[h]
        o_ref[:, h * C_HEAD_DIM:(h + 1) * C_HEAD_DIM] = o.T.astype(o_ref.dtype)


def dsa_t(ob, wi_t, vt, *, nq, tq, kb, qb, pb, cb, topk):
    dq = C_HEADS * C_HEAD_DIM
    n_kv = C_KV_HEADS * C_HEAD_DIM
    n_qi = IDX_HEADS * IDX_DIM
    s_pad = nq
    assert tq % (2 * kb) == 0 and tq % CHUNK == 0 and nq % cb == 0 and cb % pb == 0 and kb % qb == 0
    return pl.pallas_call(
        functools.partial(_dsa_t_kernel, tq=tq, kb=kb, qb=qb, pb=pb, cb=cb, n_valid=nq, q_pos0=0, topk=topk),
        grid=(1, nq // tq),
        in_specs=[pl.BlockSpec((tq, dq), lambda b, i: (i, 0)),
                  pl.BlockSpec((tq, n_qi), lambda b, i: (i, (dq + 2 * n_kv) // n_qi)),
                  pl.BlockSpec((IDX_HEADS, tq), lambda b, i: (0, i)),
                  pl.BlockSpec((nq, n_kv), lambda b, i: (0, dq // n_kv)),
                  pl.BlockSpec(vt.shape, lambda b, i: (0, 0, 0)),
                  pl.BlockSpec((nq, LANES), lambda b, i: (0, (dq + 2 * n_kv + n_qi) // LANES))],
        out_specs=pl.BlockSpec((tq, dq), lambda b, i: (i, 0)),
        out_shape=jax.ShapeDtypeStruct((nq, dq), BF16),
        scratch_shapes=[pltpu.VMEM((s_pad, tq), F32),
                        pltpu.VMEM((IDX_HEADS, tq, LANES), BF16),
                        pltpu.VMEM((C_HEADS, C_HEAD_DIM, tq), F32),
                        pltpu.VMEM((C_HEADS, 1, tq), F32),
                        pltpu.VMEM((C_HEADS, 1, tq), F32),
                        pltpu.VMEM((C_HEADS, 8, tq), F32),
                        pltpu.VMEM((C_KV_HEADS, 1, tq), F32)],
        compiler_params=_params("arbitrary", "arbitrary"),
    )(ob, ob, wi_t, ob, vt, ob)


def _router_kernel(y_ref, g_ref, wr_ref, h_ref, gate_ref):
    h = _rms(y_ref[...], g_ref[...])
    h_ref[...] = h.astype(BF16)
    logits = jnp.dot(h, wr_ref[...], preferred_element_type=F32, precision=lax.Precision.HIGHEST)
    lane = lax.broadcasted_iota(jnp.int32, logits.shape, 1)
    lg = jnp.where(lane < N_EXPERTS, logits, NEG)
    m1 = jnp.max(lg, axis=1, keepdims=True)
    i1 = jnp.min(jnp.where(lg == m1, lane, LANES), axis=1, keepdims=True)
    lg2 = jnp.where(lane == i1, NEG, lg)
    m2 = jnp.max(lg2, axis=1, keepdims=True)
    i2 = jnp.min(jnp.where(lg2 == m2, lane, LANES), axis=1, keepdims=True)
    e = jnp.exp(m2 - m1)
    g1 = 1.0 / (1.0 + e)
    g2 = e / (1.0 + e)
    meta = jnp.where(lane == 0, i1.astype(F32), jnp.where(lane == 1, i2.astype(F32),
                     jnp.where(lane == 2, g1, jnp.where(lane == 3, g2, 0.0))))
    gate_ref[...] = meta


def router(y, g, w_router_pad, tm):
    m, d = y.shape
    return pl.pallas_call(
        _router_kernel,
        grid=(m // tm,),
        in_specs=[pl.BlockSpec((tm, d), lambda i: (i, 0)),
                  pl.BlockSpec((1, d), lambda i: (0, 0)),
                  pl.BlockSpec((d, LANES), lambda i: (0, 0))],
        out_specs=[pl.BlockSpec((tm, d), lambda i: (i, 0)),
                   pl.BlockSpec((tm, LANES), lambda i: (i, 0))],
        out_shape=[jax.ShapeDtypeStruct((m, d), BF16), jax.ShapeDtypeStruct((m, LANES), F32)],
        compiler_params=_params("parallel"),
    )(y, g.reshape(1, d), w_router_pad)


def route_plan(meta, sup):
    m = meta.shape[0]
    n_tiles = (2 * m) // sup + N_EXPERTS
    e_all = jnp.concatenate([meta[:, 0], meta[:, 1]]).astype(jnp.int32)
    onehot = (e_all[:, None] == jnp.arange(N_EXPERTS)[None, :]).astype(jnp.int32)
    rank = jnp.sum((jnp.cumsum(onehot, axis=0) - onehot) * onehot, axis=1)
    counts = jnp.sum(onehot, axis=0)
    n_super = (counts + sup - 1) // sup
    super_end = jnp.cumsum(n_super)
    super_start = super_end - n_super
    pos = (super_start * sup)[e_all] + rank
    tiles = jnp.arange(n_tiles)
    used = super_end[-1]
    t_eff = jnp.minimum(tiles, used - 1)
    tile_expert = jnp.minimum(jnp.searchsorted(super_end, t_eff, side="right"), N_EXPERTS - 1).astype(jnp.int32)
    rows = jnp.clip(counts[tile_expert] - (t_eff - super_start[tile_expert]) * sup, 0, sup)
    tile_rows = jnp.where(tiles < used, rows, 0).astype(jnp.int32)
    token = jnp.concatenate([jnp.arange(m), jnp.arange(m)]).astype(jnp.int32)
    row_token = (jnp.arange(n_tiles * sup, dtype=jnp.int32) % m).at[pos].set(token)
    return pos.astype(jnp.int32), row_token, tile_expert, tile_rows


def _gather_rows_kernel(tok_ref, h_ref, o_ref, sem, *, batch):
    base = pl.program_id(0) * batch

    def issue(j, carry):
        pltpu.make_async_copy(h_ref.at[tok_ref[base + j]], o_ref.at[j], sem).start()
        return carry

    lax.fori_loop(0, batch, issue, 0)
    pltpu.make_async_copy(o_ref, o_ref, sem).wait()


def gather_rows(h3, row_token, batch):
    n_rows = row_token.shape[0]
    blk = (batch,) + h3.shape[1:]
    return pl.pallas_call(
        functools.partial(_gather_rows_kernel, batch=batch),
        grid_spec=pltpu.PrefetchScalarGridSpec(
            num_scalar_prefetch=1,
            grid=(n_rows // batch,),
            in_specs=[pl.BlockSpec(memory_space=pl.ANY)],
            out_specs=pl.BlockSpec(blk, lambda i, tok: (i, 0, 0)),
            scratch_shapes=[pltpu.SemaphoreType.DMA(())]),
        out_shape=jax.ShapeDtypeStruct((n_rows,) + h3.shape[1:], h3.dtype),
        compiler_params=_params("arbitrary"),
    )(row_token, h3)


def _grouped_ffn_kernel(te_ref, tr_ref, x_ref, wg_ref, wu_ref, wd_ref, o_ref,
                        wgb_ref, wub_ref, wdb_ref, *, sup, sub):
    t = pl.program_id(0)
    f = pl.program_id(1)
    rows = tr_ref[t]

    @pl.when(rows > 0)
    def _():
        wgb_ref[...] = wg_ref[...].astype(BF16)
        wub_ref[...] = wu_ref[...].astype(BF16)
        wdb_ref[...] = wd_ref[...].astype(BF16)

    for s in range(sup // sub):
        sl = slice(s * sub, (s + 1) * sub)

        @pl.when(jnp.logical_and(s * sub >= rows, f == 0))
        def _():
            o_ref[sl, :] = jnp.zeros((sub, o_ref.shape[1]), F32)

        @pl.when(s * sub < rows)
        def _():
            part = _swiglu_tile(x_ref[sl, :], wgb_ref[...], wub_ref[...], wdb_ref[...])

            @pl.when(f == 0)
            def _():
                o_ref[sl, :] = part

            @pl.when(f > 0)
            def _():
                o_ref[sl, :] += part


def grouped_ffn(xs, tile_expert, tile_rows, wg, wu, wd, sup, sub, tf):
    n_rows, d = xs.shape
    ff = wg.shape[2]
    n_f = ff // tf
    fidx = lambda t, f, te, tr: jnp.where(tr[t] > 0, f, n_f - 1)
    return pl.pallas_call(
        functools.partial(_grouped_ffn_kernel, sup=sup, sub=sub),
        grid_spec=pltpu.PrefetchScalarGridSpec(
            num_scalar_prefetch=2,
            grid=(n_rows // sup, n_f),
            in_specs=[pl.BlockSpec((sup, d), lambda t, f, te, tr: (t, 0), pipeline_mode=pl.Buffered(1)),
                      pl.BlockSpec((None, d, tf), lambda t, f, te, tr: (te[t], 0, fidx(t, f, te, tr))),
                      pl.BlockSpec((None, d, tf), lambda t, f, te, tr: (te[t], 0, fidx(t, f, te, tr))),
                      pl.BlockSpec((None, tf, d), lambda t, f, te, tr: (te[t], fidx(t, f, te, tr), 0))],
            out_specs=pl.BlockSpec((sup, d), lambda t, f, te, tr: (t, 0), pipeline_mode=pl.Buffered(1)),
            scratch_shapes=[pltpu.VMEM((d, tf), BF16), pltpu.VMEM((d, tf), BF16), pltpu.VMEM((tf, d), BF16)]),
        out_shape=jax.ShapeDtypeStruct((n_rows, d), F32),
        compiler_params=_params("arbitrary", "arbitrary"),
    )(tile_expert, tile_rows, xs, wg, wu, wd)


def _combine_kernel(pos_ref, y_ref, meta_ref, g_ref, ys_ref, op_ref, os_ref, buf_ref, x_ref, sem,
                    *, n_tok, tmc, prompt_tiles):
    i = pl.program_id(0)
    slot = i % 2

    def issue(tile, into):
        def body(j, carry):
            t = tile * tmc + j
            pltpu.make_async_copy(ys_ref.at[pos_ref[t]], buf_ref.at[into, 0, j], sem.at[into]).start()
            pltpu.make_async_copy(ys_ref.at[pos_ref[n_tok + t]], buf_ref.at[into, 1, j], sem.at[into]).start()
            return carry
        lax.fori_loop(0, tmc, body, 0)

    @pl.when(i == 0)
    def _():
        issue(0, 0)

    @pl.when(i + 1 < pl.num_programs(0))
    def _():
        issue(i + 1, 1 - slot)

    for e in range(2):
        pltpu.make_async_copy(buf_ref.at[slot, e], buf_ref.at[slot, e], sem.at[slot]).wait()
    g1 = meta_ref[:, 2:3]
    g2 = meta_ref[:, 3:4]
    ss = jnp.zeros((tmc, 1), F32)
    for c in range(buf_ref.shape[3]):
        cs = slice(c * LANES, (c + 1) * LANES)
        x = y_ref[:, cs] + (buf_ref[slot, 0, :, c, :] * g1 + buf_ref[slot, 1, :, c, :] * g2)
        ss = ss + jnp.sum(x * x, axis=1, keepdims=True)
        x_ref[:, cs] = x
    inv = lax.rsqrt(ss / x_ref.shape[1] + NORM_EPS)

    @pl.when(i < prompt_tiles)
    def _():
        op_ref[...] = x_ref[...] * inv * g_ref[...]

    @pl.when(i >= prompt_tiles)
    def _():
        os_ref[...] = x_ref[...] * inv * g_ref[...]


def combine_final(y, meta, g_final, ys3, pos, n_prompt, tmc):
    m, d = y.shape
    slab = ys3.shape[1:]
    prompt_tiles = n_prompt // tmc
    return pl.pallas_call(
        functools.partial(_combine_kernel, n_tok=m, tmc=tmc, prompt_tiles=prompt_tiles),
        grid_spec=pltpu.PrefetchScalarGridSpec(
            num_scalar_prefetch=1,
            grid=(m // tmc,),
            in_specs=[pl.BlockSpec((tmc, d), lambda i, pos: (i, 0)),
                      pl.BlockSpec((tmc, LANES), lambda i, pos: (i, 0)),
                      pl.BlockSpec((1, d), lambda i, pos: (0, 0)),
                      pl.BlockSpec(memory_space=pl.ANY)],
            out_specs=[pl.BlockSpec((tmc, d), lambda i, pos: (jnp.minimum(i, prompt_tiles - 1), 0)),
                       pl.BlockSpec((tmc, d), lambda i, pos: (jnp.maximum(i - prompt_tiles, 0), 0))],
            scratch_shapes=[pltpu.VMEM((2, 2, tmc) + slab, F32), pltpu.VMEM((tmc, d), F32),
                            pltpu.SemaphoreType.DMA((2,))]),
        out_shape=[jax.ShapeDtypeStruct((n_prompt, d), F32), jax.ShapeDtypeStruct((m - n_prompt, d), F32)],
        compiler_params=_params("arbitrary"),
    )(pos, y, meta, g_final.reshape(1, d), ys3)


def kernel(x_prompt, x_sample, cache_a_k, cache_a_v, state_pool, cache_c_k, cache_c_v, cache_c_idx,
           norm_mix, norm_ffn, norm_final, w_in_even, w_out_even, a_rel_bias, pool_w, pool_scale,
           ffn_w_gate, ffn_w_up, ffn_w_down, w_in_odd, w_out_odd,
           moe_router, moe_w_gate, moe_w_up, moe_w_down):
    nbp, lp, d = x_prompt.shape
    nb, ds, _ = x_sample.shape
    past = cache_c_k.shape[2]
    a_len = cache_a_k.shape[2]
    assert nbp == 1 and lp % 512 == 0 and (nb * ds) % 512 == 0 and ds == POOL_HALO and past >= POOL_HALO
    ns = nb * ds
    m = lp + ns
    tm = 512
    bf = lambda t: t.astype(BF16)

    proj0, x = stack_norm_proj(x_prompt.reshape(lp, d), x_sample.reshape(ns, d), norm_mix[0],
                               bf(w_in_even[0]), tm, 1024)
    k0 = proj0[:, A_WIDTH:2 * A_WIDTH]
    v0 = proj0[:, 2 * A_WIDTH:3 * A_WIDTH]
    u0 = proj0[:, 3 * A_WIDTH:]

    pad = A_PREV_CHUNKS * CHUNK
    bias_p = _rel_bias_tile(a_rel_bias[0], 0, CHUNK, -pad, A_BAND)
    a_p = band_prompt(proj0, _pair_rows(bias_p), lp)

    k_pos = past - a_len + jnp.arange(a_len + ds)
    q_pos = past + jnp.arange(ds)
    qch, kch = q_pos // CHUNK, k_pos // CHUNK
    ok = ((k_pos[None, :] >= 0) & (kch[None, :] <= qch[:, None])
          & (kch[None, :] >= qch[:, None] - A_PREV_CHUNKS))
    bias_s = jnp.where(ok[None], _rel_bias_tile(a_rel_bias[0], past, ds, past - a_len, a_len + ds), NEG)
    a = band_sample(proj0, cache_a_k[0].reshape(nb, a_len, A_WIDTH),
                    cache_a_v[0].reshape(nb, a_len, A_WIDTH), _pair_rows(bias_s), a_p, lp, nb, ds)

    u_s = u0[lp:].reshape(nb, ds, B_WIDTH)
    u_hist = jnp.concatenate([state_pool[0], u_s], axis=1)
    u_ext = jnp.concatenate([jnp.zeros((nb, POOL_HALO - B_HIST, B_WIDTH), F32), u_hist], axis=1)
    pw = bf(pool_w[0])
    ps = pool_scale[0].reshape(1, B_WIDTH)
    p_p = pool_prompt(proj0, pw, ps, lp, tm)
    p = pool_sample(u_ext, pw, ps, p_p, lp, past)
    wo = bf(w_out_even[0])
    y = mm_res([a, p], [wo[:A_WIDTH], wo[A_WIDTH:]], x, tm, d)
    y = ffn(y, norm_ffn[0], bf(ffn_w_gate[0]), bf(ffn_w_up[0]), bf(ffn_w_down[0]), tm, 512)

    n_q = C_HEADS * C_HEAD_DIM
    n_kv = C_KV_HEADS * C_HEAD_DIM
    n_qi = IDX_HEADS * IDX_DIM
    n_main = n_q + 2 * n_kv + n_qi
    w1 = w_in_odd[0]
    w_tail = jnp.pad(w1[:, n_main:], ((0, 0), (0, LANES - (w1.shape[1] - n_main))))
    main = norm_proj(y, norm_mix[1], bf(w1[:, :n_main]), tm, n_main // 2)
    tail = norm_proj(y, norm_mix[1], bf(w_tail), tm, LANES)
    pos = jnp.concatenate([jnp.arange(lp), jnp.tile(past + jnp.arange(ds), nb)])
    kv1, t32, ob = rope_all(main, tail, _rope_tables(pos, C_HEAD_DIM), _rope_tables(pos, IDX_DIM), tm)
    k1 = kv1[:, :n_kv]
    v1 = kv1[:, n_kv:]
    ki1 = t32[:, :IDX_DIM]

    kb_p = 128
    vt = ob[:lp, n_q + n_kv:n_q + 2 * n_kv].reshape(lp // kb_p, kb_p, n_kv).transpose(0, 2, 1)
    wi_t = t32[:lp, IDX_DIM:IDX_DIM + IDX_HEADS].T
    o_p = dsa_t(ob, wi_t, vt, nq=lp, tq=256, kb=kb_p, qb=128, pb=256, cb=512, topk=min(TOPK_MAX, lp // 4))
    o_s = dsa_sample(ob, t32, cache_c_k[0].reshape(nb, past, n_kv), cache_c_v[0].reshape(nb, past, n_kv), cache_c_idx[0],
                     row0=lp, kb=3 * LANES, topk=min(TOPK_MAX, (past + ds) // 4))
    o = jnp.concatenate([o_p, o_s], axis=0)
    y = mm_res([o], [bf(w_out_odd[0])], y, tm, d)

    wr = jnp.pad(moe_router[0], ((0, 0), (0, LANES - N_EXPERTS)))
    h, meta = router(y, norm_ffn[1], wr, tm)
    sup = 1536
    pos_rows, row_token, tile_expert, tile_rows = route_plan(meta, sup)
    n_rows = row_token.shape[0]
    slab = (d // LANES, LANES)
    xs = gather_rows(h.reshape((m,) + slab), row_token, sup)
    ys = grouped_ffn(xs.reshape(n_rows, d), tile_expert, tile_rows,
                     moe_w_gate[0], moe_w_up[0], moe_w_down[0], sup, 256, 512)
    y_p, y_s = combine_final(y, meta, norm_final, ys.reshape((n_rows,) + slab), pos_rows, lp, 256)

    y_prompt = y_p.reshape(1, lp, d)
    y_sample = y_s.reshape(nb, ds, d)
    keep = min(A_BAND, lp)
    heads = lambda t, n: t.reshape(1, n, -1, A_HEADS, A_HEAD_DIM)
    a_k_prompt = heads(k0[lp - keep:lp], 1)
    a_v_prompt = heads(v0[lp - keep:lp], 1)
    pool_prompt_out = u0[lp - B_HIST:lp].reshape(1, 1, B_HIST, B_WIDTH)
    c_k_prompt = k1[:lp].reshape(1, 1, lp, C_KV_HEADS, C_HEAD_DIM)
    c_v_prompt = v1[:lp].reshape(1, 1, lp, C_KV_HEADS, C_HEAD_DIM)
    c_idx_prompt = ki1[:lp].reshape(1, 1, lp, IDX_DIM)
    shift = lambda cache, new: jnp.concatenate(
        [cache[0], new.reshape(nb, ds, A_HEADS, A_HEAD_DIM)], axis=1)[:, ds:][None]
    a_k_sample = shift(cache_a_k, k0[lp:])
    a_v_sample = shift(cache_a_v, v0[lp:])
    pool_sample_out = u_hist[:, ds:][None]
    c_k_sample = k1[lp:].reshape(1, nb, ds, C_KV_HEADS, C_HEAD_DIM)
    c_v_sample = v1[lp:].reshape(1, nb, ds, C_KV_HEADS, C_HEAD_DIM)
    c_idx_sample = ki1[lp:].reshape(1, nb, ds, IDX_DIM)
    return (y_prompt, y_sample, a_k_prompt, a_v_prompt, pool_prompt_out,
            c_k_prompt, c_v_prompt, c_idx_prompt,
            a_k_sample, a_v_sample, pool_sample_out,
            c_k_sample, c_v_sample, c_idx_sample)
```

```python
import functools
import math

import jax
import jax.numpy as jnp
from jax import lax
from jax.experimental import pallas as pl
from jax.experimental.pallas import tpu as pltpu

F32 = jnp.float32
BF16 = jnp.bfloat16

NORM_EPS = 1e-6
NEG = -1e30
BIG = 1e30
SHIFT_SLACK = 1.01
MIN_SOFTMAX_SUM = 2.0 ** -80

CHUNK = 64
A_HEADS = 16
A_HEAD_DIM = 64
A_WIDTH = A_HEADS * A_HEAD_DIM
A_PREV_CHUNKS = 8
A_BAND = (A_PREV_CHUNKS + 1) * CHUNK
A_REL_CLIP = 128
B_WINDOWS = (2, 4, 8, 16)
B_GROUP = 256
B_WIDTH = B_GROUP * len(B_WINDOWS)
B_HIST = max(B_WINDOWS) - 1
C_HEADS = 16
C_KV_HEADS = 4
C_HEAD_DIM = 128
C_GROUPS = C_HEADS // C_KV_HEADS
IDX_HEADS = 8
IDX_DIM = 64
TOPK_MAX = 256
ROPE_THETA = 500000.0
ROPE_FRAC = 4
N_EXPERTS = 8

LANES = 128
POOL_HALO = 16
BAND_CHUNKS_PER_TRIP = 2
VMEM_LIMIT = 56 * 1024 * 1024

_NT = (((1,), (1,)), ((), ()))


def _params(*sem):
    return pltpu.CompilerParams(dimension_semantics=sem, vmem_limit_bytes=VMEM_LIMIT)


def _rms(x, g):
    ms = jnp.mean(x * x, axis=-1, keepdims=True)
    return x * lax.rsqrt(ms + NORM_EPS) * g


def _norm_proj_kernel(x_ref, g_ref, w_ref, o_ref, h_ref):
    @pl.when(pl.program_id(1) == 0)
    def _():
        h_ref[...] = _rms(x_ref[...], g_ref[...]).astype(BF16)

    o_ref[...] = jnp.dot(h_ref[...], w_ref[...], preferred_element_type=F32)


def norm_proj(x, g, w, tm, tn):
    m, d = x.shape
    n = w.shape[1]
    return pl.pallas_call(
        _norm_proj_kernel,
        grid=(m // tm, n // tn),
        in_specs=[pl.BlockSpec((tm, d), lambda i, j: (i, 0)),
                  pl.BlockSpec((1, d), lambda i, j: (0, 0)),
                  pl.BlockSpec((d, tn), lambda i, j: (0, j))],
        out_specs=pl.BlockSpec((tm, tn), lambda i, j: (i, j)),
        out_shape=jax.ShapeDtypeStruct((m, n), F32),
        scratch_shapes=[pltpu.VMEM((tm, d), BF16)],
        compiler_params=_params("parallel", "arbitrary"),
    )(x, g.reshape(1, d), w)


def _stack_norm_proj_kernel(xp_ref, xs_ref, g_ref, w_ref, o_ref, x_ref, h_ref, *, prompt_tiles):
    i = pl.program_id(0)

    def first_column(src_ref):
        x = src_ref[...]
        x_ref[...] = x
        h_ref[...] = _rms(x, g_ref[...]).astype(BF16)

    @pl.when(jnp.logical_and(pl.program_id(1) == 0, i < prompt_tiles))
    def _():
        first_column(xp_ref)

    @pl.when(jnp.logical_and(pl.program_id(1) == 0, i >= prompt_tiles))
    def _():
        first_column(xs_ref)

    o_ref[...] = jnp.dot(h_ref[...], w_ref[...], preferred_element_type=F32)


def stack_norm_proj(xp, xs, g, w, tm, tn):
    lp, d = xp.shape
    m = lp + xs.shape[0]
    n = w.shape[1]
    prompt_tiles = lp // tm
    return pl.pallas_call(
        functools.partial(_stack_norm_proj_kernel, prompt_tiles=prompt_tiles),
        grid=(m // tm, n // tn),
        in_specs=[pl.BlockSpec((tm, d), lambda i, j: (jnp.minimum(i, prompt_tiles - 1), 0)),
                  pl.BlockSpec((tm, d), lambda i, j: (jnp.maximum(i - prompt_tiles, 0), 0)),
                  pl.BlockSpec((1, d), lambda i, j: (0, 0)),
                  pl.BlockSpec((d, tn), lambda i, j: (0, j))],
        out_specs=[pl.BlockSpec((tm, tn), lambda i, j: (i, j)),
                   pl.BlockSpec((tm, d), lambda i, j: (i, 0))],
        out_shape=[jax.ShapeDtypeStruct((m, n), F32), jax.ShapeDtypeStruct((m, d), F32)],
        scratch_shapes=[pltpu.VMEM((tm, d), BF16)],
        compiler_params=_params("parallel", "arbitrary"),
    )(xp, xs, g.reshape(1, d), w)


def _band_pairs(q_ref, nq, kw_ref, vw_ref, nk, bias_ref, windows, o_ref):
    lane = lax.broadcasted_iota(jnp.int32, (1, LANES), 1)
    first = lane < A_HEAD_DIM
    for hp in range(A_HEADS // 2):
        cs = slice(hp * LANES, (hp + 1) * LANES)
        for q_row0, k_row0, key_ok in windows:
            qp = q_ref[pl.ds(q_row0, nq), cs]
            kp = kw_ref[pl.ds(k_row0, nk), cs]
            vp = vw_ref[pl.ds(k_row0, nk), cs]
            qm = jnp.concatenate([jnp.where(first, qp, 0.0), jnp.where(first, 0.0, qp)], axis=0).astype(BF16)
            s = lax.dot_general(qm, kp, _NT, preferred_element_type=F32)
            s = s * (A_HEAD_DIM ** -0.5) + bias_ref[hp]
            if key_ok is not None:
                s = jnp.where(key_ok, s, NEG)
            mx = jnp.max(s, axis=-1, keepdims=True)
            e = jnp.exp(s - mx)
            l = jnp.sum(e, axis=-1, keepdims=True)
            o = jnp.dot(e.astype(BF16), vp, preferred_element_type=F32) / l
            o_ref[pl.ds(q_row0, nq), cs] = jnp.where(first, o[:nq], o[nq:]).astype(o_ref.dtype)


def _band_prompt_kernel(q_ref, kp_ref, kc_ref, vp_ref, vc_ref, bias_ref, o_ref, kw_ref, vw_ref,
                        *, qb, pad, prompt_blocks):
    i = pl.program_id(0)

    @pl.when(i >= prompt_blocks)
    def _():
        o_ref[...] = jnp.zeros(o_ref.shape, o_ref.dtype)

    @pl.when(i < prompt_blocks)
    def _():
        _band_prompt_block(i, q_ref, kp_ref, kc_ref, vp_ref, vc_ref, bias_ref, o_ref, kw_ref, vw_ref, qb, pad)


def _band_prompt_block(i, q_ref, kp_ref, kc_ref, vp_ref, vc_ref, bias_ref, o_ref, kw_ref, vw_ref, qb, pad):
    kw_ref[0:pad, :] = kp_ref[...].astype(BF16)
    kw_ref[pad:pad + qb, :] = kc_ref[...].astype(BF16)
    vw_ref[0:pad, :] = vp_ref[...].astype(BF16)
    vw_ref[pad:pad + qb, :] = vc_ref[...].astype(BF16)
    col = lax.broadcasted_iota(jnp.int32, (1, A_BAND), 1)

    def chunks(c2, carry):
        windows = []
        for u in range(BAND_CHUNKS_PER_TRIP):
            cc = c2 * BAND_CHUNKS_PER_TRIP + u
            r0 = pl.multiple_of(cc * CHUNK, CHUNK)
            first_valid = jnp.where(i == 0, pad - cc * CHUNK, 0)
            windows.append((r0, r0, col >= first_valid))
        _band_pairs(q_ref, CHUNK, kw_ref, vw_ref, A_BAND, bias_ref, windows, o_ref)
        return carry

    lax.fori_loop(0, qb // CHUNK // BAND_CHUNKS_PER_TRIP, chunks, 0)


def band_prompt(proj, bias, lp):
    pad = A_PREV_CHUNKS * CHUNK
    qb = pad
    blk = (qb, A_WIDTH)
    prev = lambda c: (lambda i: (jnp.maximum(i - 1, 0), c))
    cur = lambda c: (lambda i: (i, c))
    return pl.pallas_call(
        functools.partial(_band_prompt_kernel, qb=qb, pad=pad, prompt_blocks=lp // qb),
        grid=(proj.shape[0] // qb,),
        in_specs=[pl.BlockSpec(blk, cur(0)),
                  pl.BlockSpec(blk, prev(1)), pl.BlockSpec(blk, cur(1)),
                  pl.BlockSpec(blk, prev(2)), pl.BlockSpec(blk, cur(2)),
                  pl.BlockSpec((A_HEADS // 2, 2 * CHUNK, A_BAND), lambda i: (0, 0, 0))],
        out_specs=pl.BlockSpec(blk, lambda i: (i, 0)),
        out_shape=jax.ShapeDtypeStruct((proj.shape[0], A_WIDTH), BF16),
        scratch_shapes=[pltpu.VMEM((pad + qb, A_WIDTH), BF16), pltpu.VMEM((pad + qb, A_WIDTH), BF16)],
        compiler_params=_params("parallel"),
    )(proj, proj, proj, proj, proj, bias)


def _band_sample_kernel(q_ref, kn_ref, vn_ref, ck_ref, cv_ref, bias_ref, dst_ref, o_ref, kw_ref, vw_ref,
                        *, a_len, ds):
    del dst_ref
    kw_ref[0:a_len, :] = ck_ref[...].astype(BF16)
    kw_ref[a_len:a_len + ds, :] = kn_ref[...].astype(BF16)
    vw_ref[0:a_len, :] = cv_ref[...].astype(BF16)
    vw_ref[a_len:a_len + ds, :] = vn_ref[...].astype(BF16)
    _band_pairs(q_ref, ds, kw_ref, vw_ref, a_len + ds, bias_ref, [(0, 0, None)], o_ref)


def band_sample(proj, cache_k, cache_v, bias, dst, lp, nb, ds):
    a_len = cache_k.shape[1]
    row = lambda c: (lambda b: (lp // ds + b, c))
    return pl.pallas_call(
        functools.partial(_band_sample_kernel, a_len=a_len, ds=ds),
        grid=(nb,),
        in_specs=[pl.BlockSpec((ds, A_WIDTH), row(0)),
                  pl.BlockSpec((ds, A_WIDTH), row(1)),
                  pl.BlockSpec((ds, A_WIDTH), row(2)),
                  pl.BlockSpec((None, a_len, A_WIDTH), lambda b: (b, 0, 0)),
                  pl.BlockSpec((None, a_len, A_WIDTH), lambda b: (b, 0, 0)),
                  pl.BlockSpec((A_HEADS // 2, 2 * ds, a_len + ds), lambda b: (0, 0, 0)),
                  pl.BlockSpec(memory_space=pl.ANY)],
        out_specs=pl.BlockSpec((ds, A_WIDTH), row(0)),
        out_shape=jax.ShapeDtypeStruct(dst.shape, dst.dtype),
        input_output_aliases={6: 0},
        scratch_shapes=[pltpu.VMEM((a_len + ds, A_WIDTH), BF16), pltpu.VMEM((a_len + ds, A_WIDTH), BF16)],
        compiler_params=_params("parallel"),
    )(proj, proj, proj, cache_k, cache_v, bias, dst)


def _rel_bias_tile(rel_bias, q0, nq, k0, nk):
    rel_max = q0 - k0 + nq - 1
    rel = jnp.clip(rel_max - jnp.arange(nq + nk - 1), -A_REL_CLIP, A_REL_CLIP) + A_REL_CLIP
    ext = rel_bias[:, rel].astype(F32)
    return jnp.stack([ext[:, nq - 1 - i:nq - 1 - i + nk] for i in range(nq)], axis=1)


def _pair_rows(bias):
    h, nq, nk = bias.shape
    return bias.reshape(h // 2, 2 * nq, nk)


def _pool_kernel(prev_ref, cur_ref, w_ref, sc_ref, *refs, tm, prompt, pos0, prompt_blocks=None):
    o_ref, ext_ref = refs[-2:]
    i = pl.program_id(0)
    if prompt_blocks is None:
        _pool_block(i, prev_ref, cur_ref, w_ref, sc_ref, o_ref, ext_ref, tm, prompt, pos0)
        return

    @pl.when(i >= prompt_blocks)
    def _():
        o_ref[...] = jnp.zeros(o_ref.shape, o_ref.dtype)

    @pl.when(i < prompt_blocks)
    def _():
        _pool_block(i, prev_ref, cur_ref, w_ref, sc_ref, o_ref, ext_ref, tm, prompt, pos0)


def _pool_block(i, prev_ref, cur_ref, w_ref, sc_ref, o_ref, ext_ref, tm, prompt, pos0):
    prev = prev_ref[...]
    if prompt:
        prev = jnp.where(i == 0, 0.0, prev)
        pos = i * tm + lax.broadcasted_iota(jnp.int32, (tm, 1), 0)
    else:
        pos = pos0 + lax.broadcasted_iota(jnp.int32, (tm, 1), 0)
    ext_ref[0:POOL_HALO, :] = prev
    ext_ref[POOL_HALO:POOL_HALO + tm, :] = cur_ref[...]
    for g, w in enumerate(B_WINDOWS):
        cs = slice(g * B_GROUP, (g + 1) * B_GROUP)
        tok = ext_ref[POOL_HALO:POOL_HALO + tm, cs]
        tot = tok
        for j in range(1, w):
            tot = tot + ext_ref[POOL_HALO - j:POOL_HALO - j + tm, cs]
        cnt = jnp.minimum(pos + 1, w).astype(F32)
        pooled = (tot / cnt - tok).astype(BF16)
        o = jnp.dot(pooled, w_ref[g], preferred_element_type=F32) * sc_ref[:, cs]
        o_ref[:, cs] = o.astype(o_ref.dtype)


def pool_prompt(proj, pool_w, pool_scale, lp, tm):
    ucol = 3 * A_WIDTH // B_WIDTH
    per = tm // POOL_HALO
    return pl.pallas_call(
        functools.partial(_pool_kernel, tm=tm, prompt=True, pos0=0, prompt_blocks=lp // tm),
        grid=(proj.shape[0] // tm,),
        in_specs=[pl.BlockSpec((POOL_HALO, B_WIDTH), lambda i: (jnp.maximum(i * per - 1, 0), ucol)),
                  pl.BlockSpec((tm, B_WIDTH), lambda i: (i, ucol)),
                  pl.BlockSpec((len(B_WINDOWS), B_GROUP, B_GROUP), lambda i: (0, 0, 0)),
                  pl.BlockSpec((1, B_WIDTH), lambda i: (0, 0))],
        out_specs=pl.BlockSpec((tm, B_WIDTH), lambda i: (i, 0)),
        out_shape=jax.ShapeDtypeStruct((proj.shape[0], B_WIDTH), BF16),
        scratch_shapes=[pltpu.VMEM((POOL_HALO + tm, B_WIDTH), F32)],
        compiler_params=_params("parallel"),
    )(proj, proj, pool_w, pool_scale)


def pool_sample(u_ext, pool_w, pool_scale, dst, row0, past):
    nb, tot, _ = u_ext.shape
    ds = tot - POOL_HALO
    return pl.pallas_call(
        functools.partial(_pool_kernel, tm=ds, prompt=False, pos0=past),
        grid=(nb,),
        in_specs=[pl.BlockSpec((None, POOL_HALO, B_WIDTH), lambda b: (b, 0, 0)),
                  pl.BlockSpec((None, ds, B_WIDTH), lambda b: (b, POOL_HALO // ds, 0)),
                  pl.BlockSpec((len(B_WINDOWS), B_GROUP, B_GROUP), lambda b: (0, 0, 0)),
                  pl.BlockSpec((1, B_WIDTH), lambda b: (0, 0)),
                  pl.BlockSpec(memory_space=pl.ANY)],
        out_specs=pl.BlockSpec((ds, B_WIDTH), lambda b: (row0 // ds + b, 0)),
        out_shape=jax.ShapeDtypeStruct(dst.shape, dst.dtype),
        input_output_aliases={4: 0},
        scratch_shapes=[pltpu.VMEM((POOL_HALO + ds, B_WIDTH), F32)],
        compiler_params=_params("parallel"),
    )(u_ext, u_ext, pool_w, pool_scale, dst)


def _mm_res_kernel(*refs, n_in):
    xs, ws = refs[:n_in], refs[n_in:2 * n_in]
    res_ref, o_ref = refs[2 * n_in], refs[2 * n_in + 1]
    acc = res_ref[...]
    for x_ref, w_ref in zip(xs, ws):
        acc = acc + jnp.dot(x_ref[...], w_ref[...], preferred_element_type=F32)
    o_ref[...] = acc


def mm_res(xs, ws, res, tm, tn):
    m, n = res.shape
    n_in = len(xs)
    in_specs = ([pl.BlockSpec((tm, x.shape[1]), lambda i, j: (i, 0)) for x in xs]
                + [pl.BlockSpec((w.shape[0], tn), lambda i, j: (0, j)) for w in ws]
                + [pl.BlockSpec((tm, tn), lambda i, j: (i, j))])
    return pl.pallas_call(
        functools.partial(_mm_res_kernel, n_in=n_in),
        grid=(m // tm, n // tn),
        in_specs=in_specs,
        out_specs=pl.BlockSpec((tm, tn), lambda i, j: (i, j)),
        out_shape=jax.ShapeDtypeStruct((m, n), F32),
        compiler_params=_params("parallel", "arbitrary"),
    )(*xs, *ws, res)


def _swiglu_tile(h, wg, wu, wd):
    a = jnp.dot(h, wg, preferred_element_type=F32)
    b = jnp.dot(h, wu, preferred_element_type=F32)
    act = (a * jax.nn.sigmoid(a) * b).astype(BF16)
    return jnp.dot(act, wd, preferred_element_type=F32)


def _ffn_kernel(y_ref, g_ref, wg_ref, wu_ref, wd_ref, o_ref, h_ref, acc_ref):
    f = pl.program_id(1)

    @pl.when(f == 0)
    def _():
        h_ref[...] = _rms(y_ref[...], g_ref[...]).astype(BF16)
        acc_ref[...] = jnp.zeros_like(acc_ref)

    acc_ref[...] += _swiglu_tile(h_ref[...], wg_ref[...], wu_ref[...], wd_ref[...])

    @pl.when(f == pl.num_programs(1) - 1)
    def _():
        o_ref[...] = y_ref[...] + acc_ref[...]


def ffn(y, g, wg, wu, wd, tm, tf):
    m, d = y.shape
    ff = wg.shape[1]
    return pl.pallas_call(
        _ffn_kernel,
        grid=(m // tm, ff // tf),
        in_specs=[pl.BlockSpec((tm, d), lambda i, f: (i, 0)),
                  pl.BlockSpec((1, d), lambda i, f: (0, 0)),
                  pl.BlockSpec((d, tf), lambda i, f: (0, f)),
                  pl.BlockSpec((d, tf), lambda i, f: (0, f)),
                  pl.BlockSpec((tf, d), lambda i, f: (f, 0))],
        out_specs=pl.BlockSpec((tm, d), lambda i, f: (i, 0)),
        out_shape=jax.ShapeDtypeStruct((m, d), F32),
        scratch_shapes=[pltpu.VMEM((tm, d), BF16), pltpu.VMEM((tm, d), F32)],
        compiler_params=_params("parallel", "arbitrary"),
    )(y, g.reshape(1, d), wg, wu, wd)


def _rope_tables(pos, head_dim):
    rot = head_dim // ROPE_FRAC
    half = rot // 2
    inv = jnp.exp(-math.log(ROPE_THETA) * jnp.arange(half, dtype=F32) * (2.0 / rot))
    ang = pos.astype(F32)[:, None] * inv[None, :]
    cos, sin = jnp.cos(ang), jnp.sin(ang)
    m = pos.shape[0]
    one = jnp.ones((m, head_dim - rot), F32)
    zero_r = jnp.zeros((m, head_dim - rot), F32)
    zero_h = jnp.zeros((m, half), F32)
    c = jnp.concatenate([cos, cos, one], axis=1)
    s_dn = jnp.concatenate([-sin, zero_h, zero_r], axis=1)
    s_up = jnp.concatenate([zero_h, sin, zero_r], axis=1)
    rep = LANES // head_dim
    return jnp.stack([jnp.tile(c, (1, rep)), jnp.tile(s_dn, (1, rep)), jnp.tile(s_up, (1, rep))])


def _rot(x, tab_ref, half):
    return (x * tab_ref[0] + pltpu.roll(x, LANES - half, 1) * tab_ref[1]
            + pltpu.roll(x, half, 1) * tab_ref[2])


def _rope_kernel(main_ref, tail_ref, tq_ref, ti_ref, kv_ref, t32_ref, ob_ref,
                 *, n_q, n_k, n_v, n_qi, wi_scale, q_scale):
    half_qk = C_HEAD_DIM // ROPE_FRAC // 2
    half_i = IDX_DIM // ROPE_FRAC // 2
    for c in range(n_q + n_k + n_v + n_qi):
        cs = slice(c * LANES, (c + 1) * LANES)
        x = main_ref[:, cs]
        if c < n_q + n_k:
            x = _rot(x, tq_ref, half_qk)
        elif c >= n_q + n_k + n_v:
            x = _rot(x, ti_ref, half_i)
        if n_q <= c < n_q + n_k + n_v:
            kv_ref[:, (c - n_q) * LANES:(c - n_q + 1) * LANES] = x
        ob_ref[:, cs] = (x * q_scale if c < n_q else x).astype(BF16)
    t = tail_ref[...]
    lane = lax.broadcasted_iota(jnp.int32, (1, LANES), 1)
    r = _rot(t, ti_ref, half_i)
    t32_ref[...] = jnp.where(lane < IDX_DIM, r, t * wi_scale)
    c = n_q + n_k + n_v + n_qi
    ob_ref[:, c * LANES:(c + 1) * LANES] = jnp.where(lane < IDX_DIM, r, pltpu.roll(r, IDX_DIM, 1)).astype(BF16)


def rope_all(main, tail, tab_qk, tab_idx, tm):
    m, nmain = main.shape
    n_q = C_HEADS * C_HEAD_DIM // LANES
    n_k = C_KV_HEADS * C_HEAD_DIM // LANES
    n_qi = IDX_HEADS * IDX_DIM // LANES
    wi_scale = (IDX_HEADS ** -0.5) * (IDX_DIM ** -0.5)
    q_scale = (C_HEAD_DIM ** -0.5) * math.log2(math.e)
    row = lambda i: (i, 0)
    return pl.pallas_call(
        functools.partial(_rope_kernel, n_q=n_q, n_k=n_k, n_v=n_k, n_qi=n_qi, wi_scale=wi_scale, q_scale=q_scale),
        grid=(m // tm,),
        in_specs=[pl.BlockSpec((tm, nmain), row),
                  pl.BlockSpec((tm, LANES), row),
                  pl.BlockSpec((3, tm, LANES), lambda i: (0, i, 0)),
                  pl.BlockSpec((3, tm, LANES), lambda i: (0, i, 0))],
        out_specs=[pl.BlockSpec((tm, 2 * n_k * LANES), row),
                   pl.BlockSpec((tm, LANES), row),
                   pl.BlockSpec((tm, nmain + LANES), row)],
        out_shape=[jax.ShapeDtypeStruct((m, 2 * n_k * LANES), F32),
                   jax.ShapeDtypeStruct((m, LANES), F32),
                   jax.ShapeDtypeStruct((m, nmain + LANES), BF16)],
        compiler_params=_params("parallel"),
    )(main, tail, tab_qk, tab_idx)


def _dsa_kernel(q_ref, qi_ref, wi_ref, kn_ref, vn_ref, kin_ref, *refs, tq, kb, past, n_valid, q_pos0, topk):
    ck_refs, cv_refs = refs[:C_KV_HEADS], refs[C_KV_HEADS:2 * C_KV_HEADS]
    ci_ref, o_ref, k_ref, v_ref, ki_ref, sc_ref, qs_ref, acc_ref, m_ref, l_ref = refs[2 * C_KV_HEADS:]
    i = 0
    rows = C_GROUPS * tq
    sub = kb // LANES
    kf = float(topk)

    new = past + tq
    ci = ci_ref[...].astype(BF16)
    for g in range(C_KV_HEADS):
        gs = slice(g * C_HEAD_DIM, (g + 1) * C_HEAD_DIM)
        k_ref[0:past, gs] = ck_refs[g][...].astype(BF16)
        v_ref[0:past, gs] = cv_refs[g][...].astype(BF16)
    ki_ref[0:past, :] = jnp.concatenate([ci, ci], axis=1)
    k_ref[past:new, :] = kn_ref[...]
    v_ref[past:new, :] = vn_ref[...]
    ki_ref[past:new, :] = kin_ref[...]
    n_pad = k_ref.shape[0] - new
    k_ref[new:, :] = jnp.zeros((n_pad, k_ref.shape[1]), BF16)
    v_ref[new:, :] = jnp.zeros((n_pad, v_ref.shape[1]), BF16)
    ki_ref[new:, :] = jnp.zeros((n_pad, ki_ref.shape[1]), BF16)

    q_pos = q_pos0 + i * tq + lax.broadcasted_iota(jnp.int32, (tq, 1), 0)
    q_chunk = q_pos // CHUNK
    last_chunk = (q_pos0 + i * tq + tq - 1) // CHUNK
    kv_limit = min(n_valid, (last_chunk + 1) * CHUNK)
    nkb = (kv_limit + kb - 1) // kb

    lane = lax.broadcasted_iota(jnp.int32, (1, LANES), 1)
    first = lane < IDX_DIM
    wi = wi_ref[:, IDX_DIM:IDX_DIM + IDX_HEADS]

    def score_block(b, carry):
        for c in range(sub):
            off = pl.multiple_of(b * kb + c * LANES, LANES)
            kib = ki_ref[pl.ds(off, LANES), :]
            acc = jnp.zeros((tq, LANES), F32)
            for hp in range(IDX_HEADS // 2):
                qp = qi_ref[:, hp * LANES:(hp + 1) * LANES]
                for half in range(2):
                    keep = first if half == 0 else jnp.logical_not(first)
                    qm = jnp.where(keep, qp, jnp.zeros_like(qp))
                    d = lax.dot_general(qm, kib, _NT, preferred_element_type=F32)
                    h = hp * 2 + half
                    acc = acc + jnp.maximum(d, 0.0) * wi[:, h:h + 1]
            k_pos = off + lane
            adm = jnp.logical_and(k_pos // CHUNK <= q_chunk, k_pos < n_valid)
            sc_ref[b, :, c * LANES:(c + 1) * LANES] = jnp.where(adm, acc, NEG)
        return carry

    lax.fori_loop(0, nkb, score_block, 0)

    def lane_sum(x):
        return jnp.sum(x, axis=1, keepdims=True)

    def count_ge(t):
        def body(b, acc):
            for c in range(sub):
                blk = sc_ref[b, :, c * LANES:(c + 1) * LANES]
                acc = acc + jnp.where(blk >= t, 1.0, 0.0)
            return acc
        return lane_sum(lax.fori_loop(0, nkb, body, jnp.zeros((tq, LANES), F32)))

    def stats(b, carry):
        mx, mn, cnt = carry
        for c in range(sub):
            blk = sc_ref[b, :, c * LANES:(c + 1) * LANES]
            ok = blk > 0.5 * NEG
            mx = jnp.maximum(mx, blk)
            mn = jnp.minimum(mn, jnp.where(ok, blk, BIG))
            cnt = cnt + jnp.where(ok, 1.0, 0.0)
        return mx, mn, cnt

    mx, mn, cnt = lax.fori_loop(
        0, nkb, stats,
        (jnp.full((tq, LANES), NEG, F32), jnp.full((tq, LANES), BIG, F32), jnp.zeros((tq, LANES), F32)))
    row_max = jnp.max(mx, axis=1, keepdims=True)
    row_min = jnp.min(mn, axis=1, keepdims=True)
    n_adm = lane_sum(cnt)

    done0 = jnp.where(n_adm <= kf, 1.0, 0.0)
    state0 = (row_min, jnp.full((tq, 1), BIG, F32), row_max, jnp.full((tq, 1), 0.5 * NEG, F32), done0)

    def bisect(state, n_steps):
        def cond(c):
            it, st = c
            return jnp.logical_and(it < n_steps, jnp.min(st[4]) < 0.5)

        def body(c):
            it, (lo, hi, mid, thr, done) = c
            cnt = count_ge(mid)
            live = done < 0.5
            hit = jnp.logical_and(live, cnt == kf)
            ge = cnt >= kf
            thr = jnp.where(hit, mid, thr)
            done = jnp.where(hit, 1.0, done)
            lo = jnp.where(ge, mid, lo)
            hi = jnp.where(ge, hi, mid)
            return it + 1, (lo, hi, 0.5 * (lo + hi), thr, done)

        return lax.while_loop(cond, body, (jnp.int32(0), state))[1]

    def snap(state):
        lo, hi, mid, thr, done = state

        def body(b, carry):
            v_lo, v_hi = carry
            for c in range(sub):
                blk = sc_ref[b, :, c * LANES:(c + 1) * LANES]
                v_lo = jnp.minimum(v_lo, jnp.where(blk >= lo, blk, BIG))
                v_hi = jnp.maximum(v_hi, jnp.where(blk < hi, blk, NEG))
            return v_lo, v_hi

        v_lo, v_hi = lax.fori_loop(0, nkb, body,
                                   (jnp.full((tq, LANES), BIG, F32), jnp.full((tq, LANES), NEG, F32)))
        v_lo = jnp.min(v_lo, axis=1, keepdims=True)
        v_hi = jnp.max(v_hi, axis=1, keepdims=True)
        live = done < 0.5
        tie = jnp.logical_and(live, v_lo == v_hi)
        thr = jnp.where(tie, v_lo, thr)
        done = jnp.where(tie, 1.0, done)
        lo = jnp.where(live, v_lo, lo)
        return (lo, hi, 0.5 * (lo + hi), thr, done), jnp.where(tie, 1.0, 0.0)

    state = bisect(state0, 32)

    def refine_cond(c):
        rounds, st, _ = c
        return jnp.logical_and(rounds < 10, jnp.min(st[4]) < 0.5)

    def refine_body(c):
        rounds, st, tie = c
        st, new_tie = snap(st)
        st = bisect(st, 32)
        return rounds + 1, st, jnp.maximum(tie, new_tie)

    _, state, tie = lax.while_loop(refine_cond, refine_body,
                                   (jnp.int32(0), state, jnp.zeros((tq, 1), F32)))
    thr = state[3]
    any_tie = jnp.max(tie) > 0.5

    @pl.when(jnp.logical_not(any_tie))
    def _():
        def body(b, carry):
            for c in range(sub):
                cs = slice(c * LANES, (c + 1) * LANES)
                sc_ref[b, :, cs] = jnp.where(sc_ref[b, :, cs] >= thr, 0.0, NEG)
            return carry
        lax.fori_loop(0, nkb, body, 0)

    @pl.when(any_tie)
    def _():
        def gt_body(b, acc):
            for c in range(sub):
                acc = acc + jnp.where(sc_ref[b, :, c * LANES:(c + 1) * LANES] > thr, 1.0, 0.0)
            return acc
        need = kf - lane_sum(lax.fori_loop(0, nkb, gt_body, jnp.zeros((tq, LANES), F32)))

        def count_eq_upto(j):
            def body(b, acc):
                for c in range(sub):
                    blk = sc_ref[b, :, c * LANES:(c + 1) * LANES]
                    idx = (b * kb + c * LANES + lane).astype(F32)
                    acc = acc + jnp.where(jnp.logical_and(blk == thr, idx <= j), 1.0, 0.0)
                return acc
            return lane_sum(lax.fori_loop(0, nkb, body, jnp.zeros((tq, LANES), F32)))

        def idx_body(_, c):
            lo_j, hi_j = c
            mid_j = jnp.floor(0.5 * (lo_j + hi_j))
            ok = count_eq_upto(mid_j) >= need
            return jnp.where(ok, lo_j, mid_j), jnp.where(ok, mid_j, hi_j)

        n_steps = max(1, math.ceil(math.log2(sc_ref.shape[0] * kb + 1)))
        _, last = lax.fori_loop(
            0, n_steps, idx_body,
            (jnp.full((tq, 1), -1.0, F32), jnp.full((tq, 1), float(sc_ref.shape[0] * kb), F32)))
        last = jnp.where(tie > 0.5, last, BIG)

        def body(b, carry):
            for c in range(sub):
                cs = slice(c * LANES, (c + 1) * LANES)
                blk = sc_ref[b, :, cs]
                idx = (b * kb + c * LANES + lane).astype(F32)
                sel = jnp.logical_or(blk > thr, jnp.logical_and(blk == thr, idx <= last))
                sc_ref[b, :, cs] = jnp.where(sel, 0.0, NEG)
            return carry
        lax.fori_loop(0, nkb, body, 0)

    for g in range(C_KV_HEADS):
        for hh in range(C_GROUPS):
            h = g * C_GROUPS + hh
            qs_ref[g, hh * tq:(hh + 1) * tq, :] = q_ref[:, h * C_HEAD_DIM:(h + 1) * C_HEAD_DIM]
    m_ref[...] = jnp.full(m_ref.shape, NEG, F32)
    l_ref[...] = jnp.zeros(l_ref.shape, F32)
    acc_ref[...] = jnp.zeros(acc_ref.shape, F32)

    def attend(b, carry):
        off = pl.multiple_of(b * kb, kb)
        bias = sc_ref[b]
        bias = jnp.concatenate([bias] * C_GROUPS, axis=0)
        for g in range(C_KV_HEADS):
            cs = slice(g * C_HEAD_DIM, (g + 1) * C_HEAD_DIM)
            kblk = k_ref[pl.ds(off, kb), cs]
            vblk = v_ref[pl.ds(off, kb), cs]
            s = lax.dot_general(qs_ref[g], kblk, _NT, preferred_element_type=F32)
            s = s + bias
            m_old = m_ref[g]
            m_new = jnp.maximum(m_old, jnp.max(s, axis=1, keepdims=True))
            alpha = jnp.exp2(m_old - m_new)
            p = jnp.exp2(s - m_new)
            l_ref[g] = alpha * l_ref[g] + jnp.sum(p, axis=1, keepdims=True)
            acc_ref[g] = alpha * acc_ref[g] + jnp.dot(p.astype(BF16), vblk, preferred_element_type=F32)
            m_ref[g] = m_new
        return carry

    lax.fori_loop(0, nkb, attend, 0, unroll=True)

    for g in range(C_KV_HEADS):
        o = acc_ref[g] / l_ref[g]
        for hh in range(C_GROUPS):
            h = g * C_GROUPS + hh
            o_ref[:, h * C_HEAD_DIM:(h + 1) * C_HEAD_DIM] = o[hh * tq:(hh + 1) * tq, :].astype(o_ref.dtype)


def dsa_sample(ob, t32, cache_k, cache_v, cache_i, *, row0, kb, topk):
    nb, past = cache_k.shape[:2]
    head = lambda g: pl.BlockSpec((None, past, C_HEAD_DIM), lambda b, i: (b, 0, g))
    heads = [head(g) for g in range(C_KV_HEADS)]
    n_kv = C_KV_HEADS * C_HEAD_DIM
    ds = (ob.shape[0] - row0) // nb
    n_q = C_HEADS * C_HEAD_DIM
    n_qi = IDX_HEADS * IDX_DIM
    s_all = past + ds
    s_pad = -(-s_all // kb) * kb
    rows = C_GROUPS * ds
    col = lambda w, off: (lambda b, i: (row0 // ds + b, off // w))
    cmap = lambda b, i: (b, 0, 0)
    return pl.pallas_call(
        functools.partial(_dsa_kernel, tq=ds, kb=kb, past=past, n_valid=s_all, q_pos0=past, topk=topk),
        grid=(nb, 1),
        in_specs=[pl.BlockSpec((ds, n_q), col(n_q, 0)),
                  pl.BlockSpec((ds, n_qi), col(n_qi, n_q + 2 * n_kv)),
                  pl.BlockSpec((ds, LANES), col(LANES, 0)),
                  pl.BlockSpec((ds, n_kv), col(n_kv, n_q)),
                  pl.BlockSpec((ds, n_kv), col(n_kv, n_q + n_kv)),
                  pl.BlockSpec((ds, LANES), col(LANES, n_q + 2 * n_kv + n_qi))]
                 + heads + heads + [pl.BlockSpec((None, past, IDX_DIM), cmap)],
        out_specs=pl.BlockSpec((ds, n_q), lambda b, i: (b, 0)),
        out_shape=jax.ShapeDtypeStruct((nb * ds, n_q), BF16),
        scratch_shapes=[pltpu.VMEM((s_pad, n_kv), BF16),
                        pltpu.VMEM((s_pad, n_kv), BF16),
                        pltpu.VMEM((s_pad, LANES), BF16),
                        pltpu.VMEM((s_pad // kb, ds, kb), F32),
                        pltpu.VMEM((C_KV_HEADS, rows, C_HEAD_DIM), BF16),
                        pltpu.VMEM((C_KV_HEADS, rows, C_HEAD_DIM), F32),
                        pltpu.VMEM((C_KV_HEADS, rows, 1), F32),
                        pltpu.VMEM((C_KV_HEADS, rows, 1), F32)],
        compiler_params=_params("parallel", "arbitrary"),
    )(ob, ob, t32, ob, ob, ob, *([cache_k] * C_KV_HEADS), *([cache_v] * C_KV_HEADS), cache_i)


def _dsa_t_kernel(q_ref, qi_ref, wi_ref, k_ref, vt_ref, ki_ref, o_ref,
                  sc_ref, qim_ref, acc_ref, m_ref, l_ref, l8_ref, kmax_ref,
                  *, tq, kb, qb, pb, cb, n_valid, q_pos0, topk):
    i = pl.program_id(1)
    kf = float(topk)

    q_pos = q_pos0 + i * tq + lax.broadcasted_iota(jnp.int32, (1, tq), 1)
    q_chunk = q_pos // CHUNK
    last_chunk = (q_pos0 + i * tq + tq - 1) // CHUNK
    kv_limit = jnp.minimum(n_valid, (last_chunk + 1) * CHUNK)
    nkb = (kv_limit + kb - 1) // kb
    ncb = (kv_limit + cb - 1) // cb

    def col_reduce(x, op):
        groups = x.shape[0] // 8
        chains = 8 if groups % 8 == 0 else 1
        return op(op(x.reshape(chains, groups // chains, 8, tq), axis=1), axis=0)

    col_sum = lambda x: col_reduce(x, jnp.sum)
    col_max = lambda x: col_reduce(x, jnp.max)
    col_min = lambda x: col_reduce(x, jnp.min)

    lane = lax.broadcasted_iota(jnp.int32, (1, LANES), 1)
    first = lane < IDX_DIM
    for hp in range(IDX_HEADS // 2):
        qp = qi_ref[:, hp * LANES:(hp + 1) * LANES]
        qim_ref[2 * hp] = jnp.where(first, qp, jnp.zeros_like(qp))
        qim_ref[2 * hp + 1] = jnp.where(first, jnp.zeros_like(qp), qp)
    block_iota = lax.broadcasted_iota(jnp.int32, (pb, 1), 0)

    def score_blocks(b0, n):
        for u in range(n):
            off = pl.multiple_of((b0 + u) * pb, pb)
            kib = ki_ref[pl.ds(off, pb), :]
            parts = []
            for h in range(IDX_HEADS):
                d = lax.dot_general(kib, qim_ref[h], _NT, preferred_element_type=F32)
                parts.append(jnp.maximum(d, 0.0) * wi_ref[h:h + 1, :])
            while len(parts) > 1:
                parts = [a + b for a, b in zip(parts[0::2], parts[1::2])]
            acc = parts[0]
            k_pos = off + block_iota
            adm = jnp.logical_and(k_pos // CHUNK <= q_chunk, k_pos < n_valid)
            sc_ref[pl.ds(off, pb), :] = jnp.where(adm, acc, NEG)

    n_pb = ncb * (cb // pb)

    def four_score_blocks(t, carry):
        score_blocks(4 * t, 4)
        return carry

    lax.fori_loop(0, n_pb // 4, four_score_blocks, 0)

    @pl.when(n_pb % 4 >= 2)
    def _():
        score_blocks((n_pb // 4) * 4, 2)

    key_iota = lax.broadcasted_iota(jnp.int32, (cb, 1), 0)

    def blocks(body, init):
        def step(b, carry):
            off = pl.multiple_of(b * cb, cb)
            return body(off, sc_ref[pl.ds(off, cb), :], carry)
        return lax.fori_loop(0, ncb, step, init)

    def total(x):
        return jnp.sum(x, axis=0, keepdims=True)

    def count_ge(t):
        return total(blocks(lambda off, blk, acc: acc + col_sum(jnp.where(blk >= t, 1.0, 0.0)),
                            jnp.zeros((8, tq), F32)))

    def stats(off, blk, carry):
        mx, mn, cnt, pos, nonneg = carry
        ok = blk > 0.5 * NEG
        return (jnp.maximum(mx, col_max(blk)), jnp.minimum(mn, col_min(jnp.where(ok, blk, BIG))),
                cnt + col_sum(jnp.where(ok, 1.0, 0.0)),
                pos + col_sum(jnp.where(blk > 0.0, 1.0, 0.0)),
                nonneg + col_sum(jnp.where(blk >= 0.0, 1.0, 0.0)))

    zeros8 = jnp.zeros((8, tq), F32)
    mx, mn, cnt, pos, nonneg = blocks(stats, (jnp.full((8, tq), NEG, F32), jnp.full((8, tq), BIG, F32),
                                              zeros8, zeros8, zeros8))
    row_max = jnp.max(mx, axis=0, keepdims=True)
    row_min = jnp.min(mn, axis=0, keepdims=True)
    n_adm = total(cnt)
    n_pos = total(pos)
    n_nonneg = total(nonneg)

    open_row = n_adm > kf
    above = n_pos >= kf
    below = n_nonneg < kf
    zero_tie = jnp.logical_and(open_row, jnp.logical_not(jnp.logical_or(above, below)))
    lo0 = jnp.where(above, 0.0, row_min)
    hi0 = jnp.where(below, 0.0, BIG)
    mid0 = jnp.where(below, 0.5 * (lo0 + hi0), row_max)
    thr0 = jnp.where(zero_tie, 0.0, 0.5 * NEG)
    done0 = jnp.where(jnp.logical_or(jnp.logical_not(open_row), zero_tie), 1.0, 0.0)
    tie0 = jnp.where(zero_tie, 1.0, 0.0)
    c_lo0 = jnp.where(above, n_pos, n_adm)
    c_hi0 = jnp.where(below, n_nonneg, 0.0)
    state0 = (lo0, hi0, mid0, thr0, done0, c_lo0, c_hi0)
    log_k = math.log2(kf + 0.5)

    def bisect(state, n_steps):
        def cond(c):
            it, st = c
            return jnp.logical_and(it < n_steps, jnp.min(st[4]) < 0.5)

        def body(c):
            it, (lo, hi, mid, thr, done, c_lo, c_hi) = c
            cnt = count_ge(mid)
            hit = jnp.logical_and(done < 0.5, cnt == kf)
            ge = cnt >= kf
            thr = jnp.where(hit, mid, thr)
            done = jnp.where(hit, 1.0, done)
            lo = jnp.where(ge, mid, lo)
            hi = jnp.where(ge, hi, mid)
            c_lo = jnp.where(ge, cnt, c_lo)
            c_hi = jnp.where(ge, c_hi, cnt)
            log_lo = jnp.log2(jnp.maximum(c_lo, 1.0))
            frac = (log_lo - log_k) / jnp.maximum(log_lo - jnp.log2(jnp.maximum(c_hi, 0.5)), 1e-6)
            frac = jnp.clip(frac, 0.02, 0.98)
            frac = jnp.where(jnp.logical_or(it % 3 == 2, hi >= BIG), 0.5, frac)
            return it + 1, (lo, hi, lo + (hi - lo) * frac, thr, done, c_lo, c_hi)

        return lax.while_loop(cond, body, (jnp.int32(0), state))[1]

    def snap(state):
        lo, hi, mid, thr, done, c_lo, c_hi = state

        def body(off, blk, carry):
            v_lo, v_hi = carry
            return (jnp.minimum(v_lo, col_min(jnp.where(blk >= lo, blk, BIG))),
                    jnp.maximum(v_hi, col_max(jnp.where(blk < hi, blk, NEG))))

        v_lo, v_hi = blocks(body, (jnp.full((8, tq), BIG, F32), jnp.full((8, tq), NEG, F32)))
        v_lo = jnp.min(v_lo, axis=0, keepdims=True)
        v_hi = jnp.max(v_hi, axis=0, keepdims=True)
        live = done < 0.5
        tie = jnp.logical_and(live, v_lo == v_hi)
        thr = jnp.where(tie, v_lo, thr)
        done = jnp.where(tie, 1.0, done)
        lo = jnp.where(live, v_lo, lo)
        return (lo, hi, 0.5 * (lo + hi), thr, done, c_lo, c_hi), jnp.where(tie, 1.0, 0.0)

    state = bisect(state0, 32)

    def refine_cond(c):
        rounds, st, _ = c
        return jnp.logical_and(rounds < 10, jnp.min(st[4]) < 0.5)

    def refine_body(c):
        rounds, st, tie = c
        st, new_tie = snap(st)
        st = bisect(st, 32)
        return rounds + 1, st, jnp.maximum(tie, new_tie)

    _, state, tie = lax.while_loop(refine_cond, refine_body, (jnp.int32(0), state, tie0))
    thr = state[3]
    any_tie = jnp.max(tie) > 0.5

    @pl.when(jnp.logical_not(any_tie))
    def _():
        def body(off, blk, carry):
            sc_ref[pl.ds(off, cb), :] = jnp.where(blk >= thr, 0.0, NEG)
            return carry
        blocks(body, 0)

    @pl.when(any_tie)
    def _():
        need = kf - total(blocks(lambda off, blk, acc: acc + col_sum(jnp.where(blk > thr, 1.0, 0.0)),
                                 jnp.zeros((8, tq), F32)))

        def count_eq_upto(j):
            def body(off, blk, acc):
                idx = (off + key_iota).astype(F32)
                return acc + col_sum(jnp.where(jnp.logical_and(blk == thr, idx <= j), 1.0, 0.0))
            return total(blocks(body, jnp.zeros((8, tq), F32)))

        def idx_cond(c):
            it, (lo_j, hi_j, settled) = c
            return jnp.logical_and(it < n_steps, jnp.min(settled) < 0.5)

        def idx_body(c):
            it, (lo_j, hi_j, settled) = c
            mid_j = jnp.floor(0.5 * (lo_j + hi_j))
            cnt = count_eq_upto(mid_j)
            ok = cnt >= need
            live = settled < 0.5
            lo_j = jnp.where(jnp.logical_and(live, jnp.logical_not(ok)), mid_j, lo_j)
            hi_j = jnp.where(jnp.logical_and(live, ok), mid_j, hi_j)
            settled = jnp.where(jnp.logical_or(cnt == need, hi_j - lo_j <= 1.0), 1.0, settled)
            return it + 1, (lo_j, hi_j, settled)

        n_keys = sc_ref.shape[0]
        n_steps = max(1, math.ceil(math.log2(n_keys + 1))) + 1
        _, (_, last, _) = lax.while_loop(
            idx_cond, idx_body,
            (jnp.int32(0), (jnp.full((1, tq), -1.0, F32), jnp.full((1, tq), float(n_keys), F32),
                            jnp.where(tie > 0.5, 0.0, 1.0))))
        last = jnp.where(tie > 0.5, last, BIG)

        def body(off, blk, carry):
            idx = (off + key_iota).astype(F32)
            sel = jnp.logical_or(blk > thr, jnp.logical_and(blk == thr, idx <= last))
            sc_ref[pl.ds(off, cb), :] = jnp.where(sel, 0.0, NEG)
            return carry
        blocks(body, 0)

    halves = kb // qb

    def head_scores(off, h):
        gs = slice((h // C_GROUPS) * C_HEAD_DIM, (h // C_GROUPS + 1) * C_HEAD_DIM)
        out = []
        for j in range(halves):
            rows = pl.ds(off + j * qb, qb)
            s = lax.dot_general(k_ref[rows, gs], q_ref[:, h * C_HEAD_DIM:(h + 1) * C_HEAD_DIM],
                                _NT, preferred_element_type=F32)
            out.append(s + sc_ref[rows, :])
        return out, gs

    @pl.when(i == 0)
    def _():
        for g in range(C_KV_HEADS):
            gs = slice(g * C_HEAD_DIM, (g + 1) * C_HEAD_DIM)

            def body(b, mx):
                kk = k_ref[pl.ds(pl.multiple_of(b * cb, cb), cb), gs].astype(F32)
                return jnp.maximum(mx, jnp.sum(kk * kk, axis=1, keepdims=True))

            mx = lax.fori_loop(0, k_ref.shape[0] // cb, body, jnp.zeros((cb, 1), F32))
            kmax_ref[g] = jnp.full((1, tq), 1.0, F32) * jnp.max(mx)

    ones = jnp.ones((8, C_HEAD_DIM), BF16)
    for h in range(C_HEADS):
        qh = q_ref[:, h * C_HEAD_DIM:(h + 1) * C_HEAD_DIM].astype(F32)
        qn2 = lax.dot_general(ones, (qh * qh).astype(BF16), _NT, preferred_element_type=F32)[0:1, :]
        m_ref[h] = jnp.sqrt(qn2 * kmax_ref[h // C_GROUPS]) * SHIFT_SLACK
    l8_ref[...] = jnp.zeros(l8_ref.shape, F32)
    acc_ref[...] = jnp.zeros(acc_ref.shape, F32)

    def attend_fixed(b0, n):
        for h in range(C_HEADS):
            lsum, upd = None, None
            for u in range(n):
                b = b0 + u
                ss, gs = head_scores(pl.multiple_of(b * kb, kb), h)
                ps = [jnp.exp2(s - m_ref[h]) for s in ss]
                part = sum(col_sum(p) for p in ps)
                p_all = jnp.concatenate([p.astype(BF16) for p in ps], axis=0)
                pv = jnp.dot(vt_ref[b, gs, :], p_all, preferred_element_type=F32)
                lsum = part if lsum is None else lsum + part
                upd = pv if upd is None else upd + pv
            l8_ref[h] += lsum
            acc_ref[h] += upd

    def eight_blocks(t, carry):
        attend_fixed(8 * t, 8)
        return carry

    lax.fori_loop(0, nkb // 8, eight_blocks, 0)
    rest = (nkb // 8) * 8

    @pl.when(nkb % 8 >= 4)
    def _():
        attend_fixed(rest, 4)

    @pl.when(nkb % 4 >= 2)
    def _():
        attend_fixed(rest + ((nkb % 8) // 4) * 4, 2)
    l_min = jnp.full((1, tq), BIG, F32)
    for h in range(C_HEADS):
        l_ref[h] = total(l8_ref[h])
        l_min = jnp.minimum(l_min, l_ref[h])

    @pl.when(jnp.logical_not(jnp.min(l_min) >= MIN_SOFTMAX_SUM))
    def _():
        m_ref[...] = jnp.full(m_ref.shape, NEG, F32)
        l_ref[...] = jnp.zeros(l_ref.shape, F32)
        acc_ref[...] = jnp.zeros(acc_ref.shape, F32)

        def attend(b, carry):
            off = pl.multiple_of(b * kb, kb)
            for h in range(C_HEADS):
                ss, gs = head_scores(off, h)
                m_old = m_ref[h]
                m_new = m_old
                for s in ss:
                    m_new = jnp.maximum(m_new, jnp.max(col_max(s), axis=0, keepdims=True))
                alpha = jnp.exp2(m_old - m_new)
                ps = [jnp.exp2(s - m_new) for s in ss]
                l_ref[h] = alpha * l_ref[h] + total(sum(col_sum(p) for p in ps))
                p_all = jnp.concatenate([p.astype(BF16) for p in ps], axis=0)
                acc_ref[h] = alpha * acc_ref[h] + jnp.dot(vt_ref[b, gs, :], p_all,
                                                          preferred_element_type=F32)
                m_ref[h] = m_new
            return carry

        lax.fori_loop(0, nkb, attend, 0)

    for h in range(C_HEADS):
        o = acc_ref[h] / l_ref[h]
        o_ref[:, h * C_HEAD_DIM:(h + 1) * C_HEAD_DIM] = o.T.astype(o_ref.dtype)


def dsa_t(ob, wi_t, vt, *, nq, tq, kb, qb, pb, cb, topk):
    dq = C_HEADS * C_HEAD_DIM
    n_kv = C_KV_HEADS * C_HEAD_DIM
    n_qi = IDX_HEADS * IDX_DIM
    s_pad = nq
    assert tq % (2 * kb) == 0 and tq % CHUNK == 0 and nq % cb == 0 and cb % pb == 0 and kb % qb == 0
    return pl.pallas_call(
        functools.partial(_dsa_t_kernel, tq=tq, kb=kb, qb=qb, pb=pb, cb=cb, n_valid=nq, q_pos0=0, topk=topk),
        grid=(1, nq // tq),
        in_specs=[pl.BlockSpec((tq, dq), lambda b, i: (i, 0)),
                  pl.BlockSpec((tq, n_qi), lambda b, i: (i, (dq + 2 * n_kv) // n_qi)),
                  pl.BlockSpec((IDX_HEADS, tq), lambda b, i: (0, i)),
                  pl.BlockSpec((nq, n_kv), lambda b, i: (0, dq // n_kv)),
                  pl.BlockSpec(vt.shape, lambda b, i: (0, 0, 0)),
                  pl.BlockSpec((nq, LANES), lambda b, i: (0, (dq + 2 * n_kv + n_qi) // LANES))],
        out_specs=pl.BlockSpec((tq, dq), lambda b, i: (i, 0)),
        out_shape=jax.ShapeDtypeStruct((nq, dq), BF16),
        scratch_shapes=[pltpu.VMEM((s_pad, tq), F32),
                        pltpu.VMEM((IDX_HEADS, tq, LANES), BF16),
                        pltpu.VMEM((C_HEADS, C_HEAD_DIM, tq), F32),
                        pltpu.VMEM((C_HEADS, 1, tq), F32),
                        pltpu.VMEM((C_HEADS, 1, tq), F32),
                        pltpu.VMEM((C_HEADS, 8, tq), F32),
                        pltpu.VMEM((C_KV_HEADS, 1, tq), F32)],
        compiler_params=_params("arbitrary", "arbitrary"),
    )(ob, ob, wi_t, ob, vt, ob)


def _router_kernel(y_ref, g_ref, wr_ref, h_ref, gate_ref):
    h = _rms(y_ref[...], g_ref[...])
    h_ref[...] = h.astype(BF16)
    logits = jnp.dot(h, wr_ref[...], preferred_element_type=F32, precision=lax.Precision.HIGHEST)
    lane = lax.broadcasted_iota(jnp.int32, logits.shape, 1)
    lg = jnp.where(lane < N_EXPERTS, logits, NEG)
    m1 = jnp.max(lg, axis=1, keepdims=True)
    i1 = jnp.min(jnp.where(lg == m1, lane, LANES), axis=1, keepdims=True)
    lg2 = jnp.where(lane == i1, NEG, lg)
    m2 = jnp.max(lg2, axis=1, keepdims=True)
    i2 = jnp.min(jnp.where(lg2 == m2, lane, LANES), axis=1, keepdims=True)
    e = jnp.exp(m2 - m1)
    g1 = 1.0 / (1.0 + e)
    g2 = e / (1.0 + e)
    meta = jnp.where(lane == 0, i1.astype(F32), jnp.where(lane == 1, i2.astype(F32),
                     jnp.where(lane == 2, g1, jnp.where(lane == 3, g2, 0.0))))
    gate_ref[...] = meta


def router(y, g, w_router_pad, tm):
    m, d = y.shape
    return pl.pallas_call(
        _router_kernel,
        grid=(m // tm,),
        in_specs=[pl.BlockSpec((tm, d), lambda i: (i, 0)),
                  pl.BlockSpec((1, d), lambda i: (0, 0)),
                  pl.BlockSpec((d, LANES), lambda i: (0, 0))],
        out_specs=[pl.BlockSpec((tm, d), lambda i: (i, 0)),
                   pl.BlockSpec((tm, LANES), lambda i: (i, 0))],
        out_shape=[jax.ShapeDtypeStruct((m, d), BF16), jax.ShapeDtypeStruct((m, LANES), F32)],
        compiler_params=_params("parallel"),
    )(y, g.reshape(1, d), w_router_pad)


def route_plan(meta, sup):
    m = meta.shape[0]
    n_tiles = (2 * m) // sup + N_EXPERTS
    e_all = jnp.concatenate([meta[:, 0], meta[:, 1]]).astype(jnp.int32)
    onehot = (e_all[:, None] == jnp.arange(N_EXPERTS)[None, :]).astype(jnp.int32)
    rank = jnp.sum((jnp.cumsum(onehot, axis=0) - onehot) * onehot, axis=1)
    counts = jnp.sum(onehot, axis=0)
    n_super = (counts + sup - 1) // sup
    super_end = jnp.cumsum(n_super)
    super_start = super_end - n_super
    pos = (super_start * sup)[e_all] + rank
    tiles = jnp.arange(n_tiles)
    used = super_end[-1]
    t_eff = jnp.minimum(tiles, used - 1)
    tile_expert = jnp.minimum(jnp.searchsorted(super_end, t_eff, side="right"), N_EXPERTS - 1).astype(jnp.int32)
    rows = jnp.clip(counts[tile_expert] - (t_eff - super_start[tile_expert]) * sup, 0, sup)
    tile_rows = jnp.where(tiles < used, rows, 0).astype(jnp.int32)
    token = jnp.concatenate([jnp.arange(m), jnp.arange(m)]).astype(jnp.int32)
    row_token = (jnp.arange(n_tiles * sup, dtype=jnp.int32) % m).at[pos].set(token)
    return pos.astype(jnp.int32), row_token, tile_expert, tile_rows


def _gather_rows_kernel(tok_ref, h_ref, o_ref, sem, *, batch):
    base = pl.program_id(0) * batch

    def issue(j2, carry):
        for u in range(2):
            j = 2 * j2 + u
            pltpu.make_async_copy(h_ref.at[tok_ref[base + j]], o_ref.at[j], sem).start(priority=u)
        return carry

    lax.fori_loop(0, batch // 2, issue, 0)
    pltpu.make_async_copy(o_ref, o_ref, sem).wait()


def gather_rows(h3, row_token, batch):
    n_rows = row_token.shape[0]
    blk = (batch,) + h3.shape[1:]
    return pl.pallas_call(
        functools.partial(_gather_rows_kernel, batch=batch),
        grid_spec=pltpu.PrefetchScalarGridSpec(
            num_scalar_prefetch=1,
            grid=(n_rows // batch,),
            in_specs=[pl.BlockSpec(memory_space=pl.ANY)],
            out_specs=pl.BlockSpec(blk, lambda i, tok: (i, 0, 0)),
            scratch_shapes=[pltpu.SemaphoreType.DMA(())]),
        out_shape=jax.ShapeDtypeStruct((n_rows,) + h3.shape[1:], h3.dtype),
        compiler_params=_params("arbitrary"),
    )(row_token, h3)


def _grouped_ffn_kernel(te_ref, tr_ref, x_ref, wg_ref, wu_ref, wd_ref, o_ref,
                        wgb_ref, wub_ref, wdb_ref, *, sup, sub):
    t = pl.program_id(0)
    f = pl.program_id(1)
    rows = tr_ref[t]

    @pl.when(rows > 0)
    def _():
        wgb_ref[...] = wg_ref[...].astype(BF16)
        wub_ref[...] = wu_ref[...].astype(BF16)
        wdb_ref[...] = wd_ref[...].astype(BF16)

    for s in range(sup // sub):
        sl = slice(s * sub, (s + 1) * sub)

        @pl.when(jnp.logical_and(s * sub >= rows, f == 0))
        def _():
            o_ref[sl, :] = jnp.zeros((sub, o_ref.shape[1]), F32)

        @pl.when(s * sub < rows)
        def _():
            part = _swiglu_tile(x_ref[sl, :], wgb_ref[...], wub_ref[...], wdb_ref[...])

            @pl.when(f == 0)
            def _():
                o_ref[sl, :] = part

            @pl.when(f > 0)
            def _():
                o_ref[sl, :] += part


def grouped_ffn(xs, tile_expert, tile_rows, wg, wu, wd, sup, sub, tf):
    n_rows, d = xs.shape
    ff = wg.shape[2]
    n_f = ff // tf
    fidx = lambda t, f, te, tr: jnp.where(tr[t] > 0, f, n_f - 1)
    return pl.pallas_call(
        functools.partial(_grouped_ffn_kernel, sup=sup, sub=sub),
        grid_spec=pltpu.PrefetchScalarGridSpec(
            num_scalar_prefetch=2,
            grid=(n_rows // sup, n_f),
            in_specs=[pl.BlockSpec((sup, d), lambda t, f, te, tr: (t, 0), pipeline_mode=pl.Buffered(1)),
                      pl.BlockSpec((None, d, tf), lambda t, f, te, tr: (te[t], 0, fidx(t, f, te, tr))),
                      pl.BlockSpec((None, d, tf), lambda t, f, te, tr: (te[t], 0, fidx(t, f, te, tr))),
                      pl.BlockSpec((None, tf, d), lambda t, f, te, tr: (te[t], fidx(t, f, te, tr), 0))],
            out_specs=pl.BlockSpec((sup, d), lambda t, f, te, tr: (t, 0), pipeline_mode=pl.Buffered(1)),
            scratch_shapes=[pltpu.VMEM((d, tf), BF16), pltpu.VMEM((d, tf), BF16), pltpu.VMEM((tf, d), BF16)]),
        out_shape=jax.ShapeDtypeStruct((n_rows, d), F32),
        compiler_params=_params("arbitrary", "arbitrary"),
    )(tile_expert, tile_rows, xs, wg, wu, wd)


def _combine_kernel(pos_ref, y_ref, meta_ref, g_ref, ys_ref, op_ref, os_ref, buf_ref, x_ref, sem,
                    *, n_tok, tmc, prompt_tiles):
    i = pl.program_id(0)
    slot = i % 2

    def issue(tile, into):
        def body(j, carry):
            t = tile * tmc + j
            pltpu.make_async_copy(ys_ref.at[pos_ref[t]], buf_ref.at[into, 0, j], sem.at[into]).start(priority=0)
            pltpu.make_async_copy(ys_ref.at[pos_ref[n_tok + t]], buf_ref.at[into, 1, j],
                                  sem.at[into]).start(priority=1)
            return carry
        lax.fori_loop(0, tmc, body, 0)

    @pl.when(i == 0)
    def _():
        issue(0, 0)

    @pl.when(i + 1 < pl.num_programs(0))
    def _():
        issue(i + 1, 1 - slot)

    for e in range(2):
        pltpu.make_async_copy(buf_ref.at[slot, e], buf_ref.at[slot, e], sem.at[slot]).wait()
    g1 = meta_ref[:, 2:3]
    g2 = meta_ref[:, 3:4]
    ss = jnp.zeros((tmc, 1), F32)
    for c in range(buf_ref.shape[3]):
        cs = slice(c * LANES, (c + 1) * LANES)
        x = y_ref[:, cs] + (buf_ref[slot, 0, :, c, :] * g1 + buf_ref[slot, 1, :, c, :] * g2)
        ss = ss + jnp.sum(x * x, axis=1, keepdims=True)
        x_ref[:, cs] = x
    inv = lax.rsqrt(ss / x_ref.shape[1] + NORM_EPS)

    @pl.when(i < prompt_tiles)
    def _():
        op_ref[...] = x_ref[...] * inv * g_ref[...]

    @pl.when(i >= prompt_tiles)
    def _():
        os_ref[...] = x_ref[...] * inv * g_ref[...]


def combine_final(y, meta, g_final, ys3, pos, n_prompt, tmc):
    m, d = y.shape
    slab = ys3.shape[1:]
    prompt_tiles = n_prompt // tmc
    return pl.pallas_call(
        functools.partial(_combine_kernel, n_tok=m, tmc=tmc, prompt_tiles=prompt_tiles),
        grid_spec=pltpu.PrefetchScalarGridSpec(
            num_scalar_prefetch=1,
            grid=(m // tmc,),
            in_specs=[pl.BlockSpec((tmc, d), lambda i, pos: (i, 0)),
                      pl.BlockSpec((tmc, LANES), lambda i, pos: (i, 0)),
                      pl.BlockSpec((1, d), lambda i, pos: (0, 0)),
                      pl.BlockSpec(memory_space=pl.ANY)],
            out_specs=[pl.BlockSpec((tmc, d), lambda i, pos: (jnp.minimum(i, prompt_tiles - 1), 0)),
                       pl.BlockSpec((tmc, d), lambda i, pos: (jnp.maximum(i - prompt_tiles, 0), 0))],
            scratch_shapes=[pltpu.VMEM((2, 2, tmc) + slab, F32), pltpu.VMEM((tmc, d), F32),
                            pltpu.SemaphoreType.DMA((2,))]),
        out_shape=[jax.ShapeDtypeStruct((n_prompt, d), F32), jax.ShapeDtypeStruct((m - n_prompt, d), F32)],
        compiler_params=_params("arbitrary"),
    )(pos, y, meta, g_final.reshape(1, d), ys3)


def kernel(x_prompt, x_sample, cache_a_k, cache_a_v, state_pool, cache_c_k, cache_c_v, cache_c_idx,
           norm_mix, norm_ffn, norm_final, w_in_even, w_out_even, a_rel_bias, pool_w, pool_scale,
           ffn_w_gate, ffn_w_up, ffn_w_down, w_in_odd, w_out_odd,
           moe_router, moe_w_gate, moe_w_up, moe_w_down):
    nbp, lp, d = x_prompt.shape
    nb, ds, _ = x_sample.shape
    past = cache_c_k.shape[2]
    a_len = cache_a_k.shape[2]
    assert nbp == 1 and lp % 512 == 0 and (nb * ds) % 512 == 0 and ds == POOL_HALO and past >= POOL_HALO
    ns = nb * ds
    m = lp + ns
    tm = 512
    bf = lambda t: t.astype(BF16)

    proj0, x = stack_norm_proj(x_prompt.reshape(lp, d), x_sample.reshape(ns, d), norm_mix[0],
                               bf(w_in_even[0]), tm, 1024)
    k0 = proj0[:, A_WIDTH:2 * A_WIDTH]
    v0 = proj0[:, 2 * A_WIDTH:3 * A_WIDTH]
    u0 = proj0[:, 3 * A_WIDTH:]

    pad = A_PREV_CHUNKS * CHUNK
    bias_p = _rel_bias_tile(a_rel_bias[0], 0, CHUNK, -pad, A_BAND)
    a_p = band_prompt(proj0, _pair_rows(bias_p), lp)

    k_pos = past - a_len + jnp.arange(a_len + ds)
    q_pos = past + jnp.arange(ds)
    qch, kch = q_pos // CHUNK, k_pos // CHUNK
    ok = ((k_pos[None, :] >= 0) & (kch[None, :] <= qch[:, None])
          & (kch[None, :] >= qch[:, None] - A_PREV_CHUNKS))
    bias_s = jnp.where(ok[None], _rel_bias_tile(a_rel_bias[0], past, ds, past - a_len, a_len + ds), NEG)
    a = band_sample(proj0, cache_a_k[0].reshape(nb, a_len, A_WIDTH),
                    cache_a_v[0].reshape(nb, a_len, A_WIDTH), _pair_rows(bias_s), a_p, lp, nb, ds)

    u_s = u0[lp:].reshape(nb, ds, B_WIDTH)
    u_hist = jnp.concatenate([state_pool[0], u_s], axis=1)
    u_ext = jnp.concatenate([jnp.zeros((nb, POOL_HALO - B_HIST, B_WIDTH), F32), u_hist], axis=1)
    pw = bf(pool_w[0])
    ps = pool_scale[0].reshape(1, B_WIDTH)
    p_p = pool_prompt(proj0, pw, ps, lp, tm)
    p = pool_sample(u_ext, pw, ps, p_p, lp, past)
    wo = bf(w_out_even[0])
    y = mm_res([a, p], [wo[:A_WIDTH], wo[A_WIDTH:]], x, tm, d)
    y = ffn(y, norm_ffn[0], bf(ffn_w_gate[0]), bf(ffn_w_up[0]), bf(ffn_w_down[0]), tm, 512)

    n_q = C_HEADS * C_HEAD_DIM
    n_kv = C_KV_HEADS * C_HEAD_DIM
    n_qi = IDX_HEADS * IDX_DIM
    n_main = n_q + 2 * n_kv + n_qi
    w1 = w_in_odd[0]
    w_tail = jnp.pad(w1[:, n_main:], ((0, 0), (0, LANES - (w1.shape[1] - n_main))))
    main = norm_proj(y, norm_mix[1], bf(w1[:, :n_main]), tm, n_main // 2)
    tail = norm_proj(y, norm_mix[1], bf(w_tail), tm, LANES)
    pos = jnp.concatenate([jnp.arange(lp), jnp.tile(past + jnp.arange(ds), nb)])
    kv1, t32, ob = rope_all(main, tail, _rope_tables(pos, C_HEAD_DIM), _rope_tables(pos, IDX_DIM), tm)
    k1 = kv1[:, :n_kv]
    v1 = kv1[:, n_kv:]
    ki1 = t32[:, :IDX_DIM]

    kb_p = 128
    vt = ob[:lp, n_q + n_kv:n_q + 2 * n_kv].reshape(lp // kb_p, kb_p, n_kv).transpose(0, 2, 1)
    wi_t = t32[:lp, IDX_DIM:IDX_DIM + IDX_HEADS].T
    o_p = dsa_t(ob, wi_t, vt, nq=lp, tq=256, kb=kb_p, qb=128, pb=256, cb=512, topk=min(TOPK_MAX, lp // 4))
    o_s = dsa_sample(ob, t32, cache_c_k[0].reshape(nb, past, n_kv), cache_c_v[0].reshape(nb, past, n_kv), cache_c_idx[0],
                     row0=lp, kb=3 * LANES, topk=min(TOPK_MAX, (past + ds) // 4))
    o = jnp.concatenate([o_p, o_s], axis=0)
    y = mm_res([o], [bf(w_out_odd[0])], y, tm, d)

    wr = jnp.pad(moe_router[0], ((0, 0), (0, LANES - N_EXPERTS)))
    h, meta = router(y, norm_ffn[1], wr, tm)
    sup = 1536
    pos_rows, row_token, tile_expert, tile_rows = route_plan(meta, sup)
    n_rows = row_token.shape[0]
    slab = (d // LANES, LANES)
    xs = gather_rows(h.reshape((m,) + slab), row_token, sup)
    ys = grouped_ffn(xs.reshape(n_rows, d), tile_expert, tile_rows,
                     moe_w_gate[0], moe_w_up[0], moe_w_down[0], sup, 256, 512)
    y_p, y_s = combine_final(y, meta, norm_final, ys.reshape((n_rows,) + slab), pos_rows, lp, 256)

    y_prompt = y_p.reshape(1, lp, d)
    y_sample = y_s.reshape(nb, ds, d)
    keep = min(A_BAND, lp)
    heads = lambda t, n: t.reshape(1, n, -1, A_HEADS, A_HEAD_DIM)
    a_k_prompt = heads(k0[lp - keep:lp], 1)
    a_v_prompt = heads(v0[lp - keep:lp], 1)
    pool_prompt_out = u0[lp - B_HIST:lp].reshape(1, 1, B_HIST, B_WIDTH)
    c_k_prompt = k1[:lp].reshape(1, 1, lp, C_KV_HEADS, C_HEAD_DIM)
    c_v_prompt = v1[:lp].reshape(1, 1, lp, C_KV_HEADS, C_HEAD_DIM)
    c_idx_prompt = ki1[:lp].reshape(1, 1, lp, IDX_DIM)
    shift = lambda cache, new: jnp.concatenate(
        [cache[0], new.reshape(nb, ds, A_HEADS, A_HEAD_DIM)], axis=1)[:, ds:][None]
    a_k_sample = shift(cache_a_k, k0[lp:])
    a_v_sample = shift(cache_a_v, v0[lp:])
    pool_sample_out = u_hist[:, ds:][None]
    c_k_sample = k1[lp:].reshape(1, nb, ds, C_KV_HEADS, C_HEAD_DIM)
    c_v_sample = v1[lp:].reshape(1, nb, ds, C_KV_HEADS, C_HEAD_DIM)
    c_idx_sample = ki1[lp:].reshape(1, nb, ds, IDX_DIM)
    return (y_prompt, y_sample, a_k_prompt, a_v_prompt, pool_prompt_out,
            c_k_prompt, c_v_prompt, c_idx_prompt,
            a_k_sample, a_v_sample, pool_sample_out,
            c_k_sample, c_v_sample, c_idx_sample)
```
